```python
import jax, jax.numpy as jnp
from jax import lax
import numpy as np

D_MODEL = 1024
BATCH = 8
SEQ = 8192
DEPTH = 1

D_A = D_MODEL
D_B = D_MODEL
CONV_A_WIDTH = 3
CONV_B_WIDTH = 31
D_FF = 4 * D_MODEL
RMS_EPS = 1e-6
LN_EPS = 1e-5
N_IN = 3 * D_A + 2 * D_B + 2 * D_MODEL

kernel_name = "hybrid_shortconv_conformer_conv_block"


def rms_norm(x, g):
    xf = x.astype(jnp.float32)
    y = xf * lax.rsqrt(jnp.mean(xf * xf, axis=-1, keepdims=True) + RMS_EPS)
    return (y * g.astype(jnp.float32)).astype(x.dtype)


def layer_norm(x, g, b):
    xf = x.astype(jnp.float32)
    mu = jnp.mean(xf, axis=-1, keepdims=True)
    var = jnp.mean(jnp.square(xf - mu), axis=-1, keepdims=True)
    y = (xf - mu) * lax.rsqrt(var + LN_EPS)
    return (y * g.astype(jnp.float32) + b.astype(jnp.float32)).astype(x.dtype)


def depthwise_conv_centred(u, w, b):
    k, c = w.shape
    pad = (k - 1) // 2
    y = lax.conv_general_dilated(
        u, w[:, None, :].astype(u.dtype), window_strides=(1,), padding=[(pad, pad)],
        dimension_numbers=("NWC", "WIO", "NWC"), feature_group_count=c)
    return y + b.astype(u.dtype)


def _fwd_setup_inputs(seed: int = 0) -> dict:
    key = jax.random.key(seed)
    ks = jax.random.split(key, 24)
    f32 = jnp.float32

    def nrm(k, shape, scale):
        return jax.random.normal(k, shape, f32) * scale

    def gain(k, n):
        return jnp.ones((n,), f32) + 0.05 * jax.random.normal(k, (n,), f32)

    return {
        "x": jax.random.normal(ks[0], (BATCH, SEQ, D_MODEL), f32),
        "norm1_pre_g": gain(ks[1], D_MODEL),
        "w_in": nrm(ks[2], (D_MODEL, N_IN), D_MODEL ** -0.5),
        "b_in": nrm(ks[3], (N_IN,), 0.02),
        "conv_a_w": nrm(ks[4], (CONV_A_WIDTH, D_A), CONV_A_WIDTH ** -0.5),
        "conv_a_b": nrm(ks[5], (D_A,), 0.02),
        "w_a_out": nrm(ks[6], (D_A, D_MODEL), D_A ** -0.5),
        "conv_b_w": nrm(ks[7], (CONV_B_WIDTH, D_B), CONV_B_WIDTH ** -0.5),
        "conv_b_b": nrm(ks[8], (D_B,), 0.02),
        "ln_b_g": gain(ks[9], D_B),
        "ln_b_b": nrm(ks[10], (D_B,), 0.02),
        "w_b_out": nrm(ks[11], (D_B, D_MODEL), D_B ** -0.5),
        "w_o": nrm(ks[12], (D_MODEL, D_MODEL), D_MODEL ** -0.5),
        "norm1_post_g": gain(ks[13], D_MODEL),
        "norm2_pre_g": gain(ks[14], D_MODEL),
        "w_mlp_in": nrm(ks[15], (D_MODEL, D_FF), D_MODEL ** -0.5),
        "w_mlp_out": nrm(ks[16], (D_FF, D_MODEL), D_FF ** -0.5),
        "norm2_post_g": gain(ks[17], D_MODEL),
    }


def _fwd_reference(x, norm1_pre_g, w_in, b_in, conv_a_w, conv_a_b, w_a_out,
              conv_b_w, conv_b_b, ln_b_g, ln_b_b, w_b_out, w_o, norm1_post_g,
              norm2_pre_g, w_mlp_in, w_mlp_out, norm2_post_g):
    for _ in range(DEPTH):
        h = rms_norm(x, norm1_pre_g)
        proj = jnp.einsum("bsd,dn->bsn", h, w_in) + b_in
        splits = np.cumsum([D_A, D_A, D_A, D_B, D_B, D_MODEL])
        b_gate, c_gate, h_a, a_b, g_b, z_a, z_b = jnp.split(proj, splits, axis=-1)

        v_a = depthwise_conv_centred(c_gate * h_a, conv_a_w, conv_a_b)
        y_a = jnp.einsum("bsc,cd->bsd", b_gate * v_a, w_a_out)

        u_b = a_b * jax.nn.sigmoid(g_b)
        v_b = depthwise_conv_centred(u_b, conv_b_w, conv_b_b)
        v_b = jax.nn.silu(layer_norm(v_b, ln_b_g, ln_b_b))
        y_b = jnp.einsum("bsc,cd->bsd", v_b, w_b_out)

        merged = jax.nn.sigmoid(z_a) * y_a + jax.nn.sigmoid(z_b) * y_b
        mix_out = jnp.einsum("bsd,de->bse", merged, w_o)
        x = x + rms_norm(mix_out, norm1_post_g)

        h2 = rms_norm(x, norm2_pre_g)
        f = jnp.square(jax.nn.relu(jnp.einsum("bsd,df->bsf", h2, w_mlp_in)))
        f = jnp.einsum("bsf,fd->bsd", f, w_mlp_out)
        x = x + rms_norm(f, norm2_post_g)
    return x


import jax as _jax
import jax.numpy as _jnp

TWIN_FORMAT = 'train_step'
FWD_PARAMS = ['x', 'norm1_pre_g', 'w_in', 'b_in', 'conv_a_w', 'conv_a_b', 'w_a_out', 'conv_b_w', 'conv_b_b', 'ln_b_g', 'ln_b_b', 'w_b_out', 'w_o', 'norm1_post_g', 'norm2_pre_g', 'w_mlp_in', 'w_mlp_out', 'norm2_post_g']
TWIN_WEIGHTS = ['norm1_pre_g', 'w_in', 'b_in', 'conv_a_w', 'conv_a_b', 'w_a_out', 'conv_b_w', 'conv_b_b', 'ln_b_g', 'ln_b_b', 'w_b_out', 'w_o', 'norm1_post_g', 'norm2_pre_g', 'w_mlp_in', 'w_mlp_out', 'norm2_post_g']
TWIN_DIFF_INPUT = 'x'
TWIN_INPUTS = ['x', 'norm1_pre_g', 'w_in', 'b_in', 'conv_a_w', 'conv_a_b', 'w_a_out', 'conv_b_w', 'conv_b_b', 'ln_b_g', 'ln_b_b', 'w_b_out', 'w_o', 'norm1_post_g', 'norm2_pre_g', 'w_mlp_in', 'w_mlp_out', 'norm2_post_g', 'loss_target', 'm_norm1_pre_g', 'm_w_in', 'm_b_in', 'm_conv_a_w', 'm_conv_a_b', 'm_w_a_out', 'm_conv_b_w', 'm_conv_b_b', 'm_ln_b_g', 'm_ln_b_b', 'm_w_b_out', 'm_w_o', 'm_norm1_post_g', 'm_norm2_pre_g', 'm_w_mlp_in', 'm_w_mlp_out', 'm_norm2_post_g', 'v_norm1_pre_g', 'v_w_in', 'v_b_in', 'v_conv_a_w', 'v_conv_a_b', 'v_w_a_out', 'v_conv_b_w', 'v_conv_b_b', 'v_ln_b_g', 'v_ln_b_b', 'v_w_b_out', 'v_w_o', 'v_norm1_post_g', 'v_norm2_pre_g', 'v_w_mlp_in', 'v_w_mlp_out', 'v_norm2_post_g']
TWIN_OUTPUTS = ['loss', 'grad_x', 'grad_norm1_pre_g', 'grad_w_in', 'grad_b_in', 'grad_conv_a_w', 'grad_conv_a_b', 'grad_w_a_out', 'grad_conv_b_w', 'grad_conv_b_b', 'grad_ln_b_g', 'grad_ln_b_b', 'grad_w_b_out', 'grad_w_o', 'grad_norm1_post_g', 'grad_norm2_pre_g', 'grad_w_mlp_in', 'grad_w_mlp_out', 'grad_norm2_post_g', 'delta_norm1_pre_g', 'delta_w_in', 'delta_b_in', 'delta_conv_a_w', 'delta_conv_a_b', 'delta_w_a_out', 'delta_conv_b_w', 'delta_conv_b_b', 'delta_ln_b_g', 'delta_ln_b_b', 'delta_w_b_out', 'delta_w_o', 'delta_norm1_post_g', 'delta_norm2_pre_g', 'delta_w_mlp_in', 'delta_w_mlp_out', 'delta_norm2_post_g', 'new_m_norm1_pre_g', 'new_m_w_in', 'new_m_b_in', 'new_m_conv_a_w', 'new_m_conv_a_b', 'new_m_w_a_out', 'new_m_conv_b_w', 'new_m_conv_b_b', 'new_m_ln_b_g', 'new_m_ln_b_b', 'new_m_w_b_out', 'new_m_w_o', 'new_m_norm1_post_g', 'new_m_norm2_pre_g', 'new_m_w_mlp_in', 'new_m_w_mlp_out', 'new_m_norm2_post_g', 'new_v_norm1_pre_g', 'new_v_w_in', 'new_v_b_in', 'new_v_conv_a_w', 'new_v_conv_a_b', 'new_v_w_a_out', 'new_v_conv_b_w', 'new_v_conv_b_b', 'new_v_ln_b_g', 'new_v_ln_b_b', 'new_v_w_b_out', 'new_v_w_o', 'new_v_norm1_post_g', 'new_v_norm2_pre_g', 'new_v_w_mlp_in', 'new_v_w_mlp_out', 'new_v_norm2_post_g']
TWIN_LEAF_KINDS = {'loss': 'loss', 'grad_x': 'grad_x', 'grad_norm1_pre_g': 'grad_w', 'grad_w_in': 'grad_w', 'grad_b_in': 'grad_w', 'grad_conv_a_w': 'grad_w', 'grad_conv_a_b': 'grad_w', 'grad_w_a_out': 'grad_w', 'grad_conv_b_w': 'grad_w', 'grad_conv_b_b': 'grad_w', 'grad_ln_b_g': 'grad_w', 'grad_ln_b_b': 'grad_w', 'grad_w_b_out': 'grad_w', 'grad_w_o': 'grad_w', 'grad_norm1_post_g': 'grad_w', 'grad_norm2_pre_g': 'grad_w', 'grad_w_mlp_in': 'grad_w', 'grad_w_mlp_out': 'grad_w', 'grad_norm2_post_g': 'grad_w', 'delta_norm1_pre_g': 'delta_w', 'delta_w_in': 'delta_w', 'delta_b_in': 'delta_w', 'delta_conv_a_w': 'delta_w', 'delta_conv_a_b': 'delta_w', 'delta_w_a_out': 'delta_w', 'delta_conv_b_w': 'delta_w', 'delta_conv_b_b': 'delta_w', 'delta_ln_b_g': 'delta_w', 'delta_ln_b_b': 'delta_w', 'delta_w_b_out': 'delta_w', 'delta_w_o': 'delta_w', 'delta_norm1_post_g': 'delta_w', 'delta_norm2_pre_g': 'delta_w', 'delta_w_mlp_in': 'delta_w', 'delta_w_mlp_out': 'delta_w', 'delta_norm2_post_g': 'delta_w', 'new_m_norm1_pre_g': 'new_m', 'new_m_w_in': 'new_m', 'new_m_b_in': 'new_m', 'new_m_conv_a_w': 'new_m', 'new_m_conv_a_b': 'new_m', 'new_m_w_a_out': 'new_m', 'new_m_conv_b_w': 'new_m', 'new_m_conv_b_b': 'new_m', 'new_m_ln_b_g': 'new_m', 'new_m_ln_b_b': 'new_m', 'new_m_w_b_out': 'new_m', 'new_m_w_o': 'new_m', 'new_m_norm1_post_g': 'new_m', 'new_m_norm2_pre_g': 'new_m', 'new_m_w_mlp_in': 'new_m', 'new_m_w_mlp_out': 'new_m', 'new_m_norm2_post_g': 'new_m', 'new_v_norm1_pre_g': 'new_v', 'new_v_w_in': 'new_v', 'new_v_b_in': 'new_v', 'new_v_conv_a_w': 'new_v', 'new_v_conv_a_b': 'new_v', 'new_v_w_a_out': 'new_v', 'new_v_conv_b_w': 'new_v', 'new_v_conv_b_b': 'new_v', 'new_v_ln_b_g': 'new_v', 'new_v_ln_b_b': 'new_v', 'new_v_w_b_out': 'new_v', 'new_v_w_o': 'new_v', 'new_v_norm1_post_g': 'new_v', 'new_v_norm2_pre_g': 'new_v', 'new_v_w_mlp_in': 'new_v', 'new_v_w_mlp_out': 'new_v', 'new_v_norm2_post_g': 'new_v'}


def _forward(args):
    return _fwd_reference(*[args[k] for k in FWD_PARAMS])


def _output_shape():
    def fwd():
        inp = _fwd_setup_inputs(0)
        return _fwd_reference(*[inp[k] for k in FWD_PARAMS])
    out = _jax.eval_shape(fwd)
    return out.shape, out.dtype

N_MICROBATCH = 1
ADAM_LR = 0.001
ADAM_B1 = 0.9
ADAM_B2 = 0.999
ADAM_EPS = 1e-08
ADAM_WD = 0.01
ADAM_STEP = 10
PER_EXAMPLE_BATCH_AXIS = {'x': 0, 'loss_target': 0}
SHARED_INPUTS = []
_WEIGHT_DTYPES = {'norm1_pre_g': _jnp.float32, 'w_in': _jnp.float32, 'b_in': _jnp.float32, 'conv_a_w': _jnp.float32, 'conv_a_b': _jnp.float32, 'w_a_out': _jnp.float32, 'conv_b_w': _jnp.float32, 'conv_b_b': _jnp.float32, 'ln_b_g': _jnp.float32, 'ln_b_b': _jnp.float32, 'w_b_out': _jnp.float32, 'w_o': _jnp.float32, 'norm1_post_g': _jnp.float32, 'norm2_pre_g': _jnp.float32, 'w_mlp_in': _jnp.float32, 'w_mlp_out': _jnp.float32, 'norm2_post_g': _jnp.float32}
MOMENT_SCALE = {'norm1_pre_g': 9.553542e-01, 'w_in': 3.550705e-01, 'b_in': 3.723680e+00, 'conv_a_w': 5.686575e-01, 'conv_a_b': 1.355982e+00, 'w_a_out': 7.956886e-01, 'conv_b_w': 1.101696e+00, 'conv_b_b': 2.039978e+01, 'ln_b_g': 7.623165e+00, 'ln_b_b': 1.123023e+01, 'w_b_out': 4.355844e+00, 'w_o': 4.490123e+00, 'norm1_post_g': 6.461226e+01, 'norm2_pre_g': 1.869936e+00, 'w_mlp_in': 9.559849e-01, 'w_mlp_out': 4.928599e+00, 'norm2_post_g': 6.622637e+01}


def _to_microbatches(a, axis):
    t = _jnp.moveaxis(a, axis, 0)
    t = t.reshape((N_MICROBATCH, t.shape[0] // N_MICROBATCH) + t.shape[1:])
    return _jnp.moveaxis(t, 1, axis + 1)


def setup_inputs(seed: int = 0) -> dict:
    inp = _fwd_setup_inputs(seed)
    key = _jax.random.fold_in(_jax.random.key(seed), 7919)
    shape, _ = _output_shape()
    out = dict(inp)
    out["loss_target"] = _jax.random.normal(_jax.random.fold_in(key, 0), shape, _jnp.float32)
    for i, name in enumerate(TWIN_WEIGHTS):
        w = inp[name].astype(_jnp.float32)
        if MOMENT_SCALE is None:
            s = _jnp.sqrt(_jnp.mean(_jnp.square(w)) + 1e-30)
        else:
            s = MOMENT_SCALE[name]
        km, kv = _jax.random.split(_jax.random.fold_in(key, i + 1))
        out[name] = w
        out["m_" + name] = s * _jax.random.normal(km, w.shape, _jnp.float32)
        out["v_" + name] = (s * s) * _jax.random.uniform(kv, w.shape, _jnp.float32, 0.5, 1.5)
    if N_MICROBATCH > 1:
        for name, axis in PER_EXAMPLE_BATCH_AXIS.items():
            out[name] = _to_microbatches(out[name], axis)
    return {'x': out['x'], 'norm1_pre_g': out['norm1_pre_g'], 'w_in': out['w_in'], 'b_in': out['b_in'], 'conv_a_w': out['conv_a_w'], 'conv_a_b': out['conv_a_b'], 'w_a_out': out['w_a_out'], 'conv_b_w': out['conv_b_w'], 'conv_b_b': out['conv_b_b'], 'ln_b_g': out['ln_b_g'], 'ln_b_b': out['ln_b_b'], 'w_b_out': out['w_b_out'], 'w_o': out['w_o'], 'norm1_post_g': out['norm1_post_g'], 'norm2_pre_g': out['norm2_pre_g'], 'w_mlp_in': out['w_mlp_in'], 'w_mlp_out': out['w_mlp_out'], 'norm2_post_g': out['norm2_post_g'], 'loss_target': out['loss_target'], 'm_norm1_pre_g': out['m_norm1_pre_g'], 'm_w_in': out['m_w_in'], 'm_b_in': out['m_b_in'], 'm_conv_a_w': out['m_conv_a_w'], 'm_conv_a_b': out['m_conv_a_b'], 'm_w_a_out': out['m_w_a_out'], 'm_conv_b_w': out['m_conv_b_w'], 'm_conv_b_b': out['m_conv_b_b'], 'm_ln_b_g': out['m_ln_b_g'], 'm_ln_b_b': out['m_ln_b_b'], 'm_w_b_out': out['m_w_b_out'], 'm_w_o': out['m_w_o'], 'm_norm1_post_g': out['m_norm1_post_g'], 'm_norm2_pre_g': out['m_norm2_pre_g'], 'm_w_mlp_in': out['m_w_mlp_in'], 'm_w_mlp_out': out['m_w_mlp_out'], 'm_norm2_post_g': out['m_norm2_post_g'], 'v_norm1_pre_g': out['v_norm1_pre_g'], 'v_w_in': out['v_w_in'], 'v_b_in': out['v_b_in'], 'v_conv_a_w': out['v_conv_a_w'], 'v_conv_a_b': out['v_conv_a_b'], 'v_w_a_out': out['v_w_a_out'], 'v_conv_b_w': out['v_conv_b_w'], 'v_conv_b_b': out['v_conv_b_b'], 'v_ln_b_g': out['v_ln_b_g'], 'v_ln_b_b': out['v_ln_b_b'], 'v_w_b_out': out['v_w_b_out'], 'v_w_o': out['v_w_o'], 'v_norm1_post_g': out['v_norm1_post_g'], 'v_norm2_pre_g': out['v_norm2_pre_g'], 'v_w_mlp_in': out['v_w_mlp_in'], 'v_w_mlp_out': out['v_w_mlp_out'], 'v_norm2_post_g': out['v_norm2_post_g']}


def _loss(weights, diff, rest, loss_target):
    with _jax.named_scope("forward"):
        args = {**rest, TWIN_DIFF_INPUT: diff, **{k: w.astype(_WEIGHT_DTYPES[k]) for k, w in weights.items()}}
        y = _forward(args)
    with _jax.named_scope("loss_head"):
        err = _jnp.square(y.astype(_jnp.float32) - loss_target)
        return 0.5 * _jnp.sum(_jnp.mean(err, axis=-1)) if err.ndim else 0.5 * err


def _adamw(w, g, m, v):
    m = ADAM_B1 * m + (1.0 - ADAM_B1) * g
    v = ADAM_B2 * v + (1.0 - ADAM_B2) * _jnp.square(g)
    m_hat = m / (1.0 - ADAM_B1 ** ADAM_STEP)
    v_hat = v / (1.0 - ADAM_B2 ** ADAM_STEP)
    delta = -ADAM_LR * (m_hat / (_jnp.sqrt(v_hat) + ADAM_EPS) + ADAM_WD * w)
    return delta, m, v


def reference(x, norm1_pre_g, w_in, b_in, conv_a_w, conv_a_b, w_a_out, conv_b_w, conv_b_b, ln_b_g, ln_b_b, w_b_out, w_o, norm1_post_g, norm2_pre_g, w_mlp_in, w_mlp_out, norm2_post_g, loss_target, m_norm1_pre_g, m_w_in, m_b_in, m_conv_a_w, m_conv_a_b, m_w_a_out, m_conv_b_w, m_conv_b_b, m_ln_b_g, m_ln_b_b, m_w_b_out, m_w_o, m_norm1_post_g, m_norm2_pre_g, m_w_mlp_in, m_w_mlp_out, m_norm2_post_g, v_norm1_pre_g, v_w_in, v_b_in, v_conv_a_w, v_conv_a_b, v_w_a_out, v_conv_b_w, v_conv_b_b, v_ln_b_g, v_ln_b_b, v_w_b_out, v_w_o, v_norm1_post_g, v_norm2_pre_g, v_w_mlp_in, v_w_mlp_out, v_norm2_post_g):
    given = dict(x=x, norm1_pre_g=norm1_pre_g, w_in=w_in, b_in=b_in, conv_a_w=conv_a_w, conv_a_b=conv_a_b, w_a_out=w_a_out, conv_b_w=conv_b_w, conv_b_b=conv_b_b, ln_b_g=ln_b_g, ln_b_b=ln_b_b, w_b_out=w_b_out, w_o=w_o, norm1_post_g=norm1_post_g, norm2_pre_g=norm2_pre_g, w_mlp_in=w_mlp_in, w_mlp_out=w_mlp_out, norm2_post_g=norm2_post_g, loss_target=loss_target, m_norm1_pre_g=m_norm1_pre_g, m_w_in=m_w_in, m_b_in=m_b_in, m_conv_a_w=m_conv_a_w, m_conv_a_b=m_conv_a_b, m_w_a_out=m_w_a_out, m_conv_b_w=m_conv_b_w, m_conv_b_b=m_conv_b_b, m_ln_b_g=m_ln_b_g, m_ln_b_b=m_ln_b_b, m_w_b_out=m_w_b_out, m_w_o=m_w_o, m_norm1_post_g=m_norm1_post_g, m_norm2_pre_g=m_norm2_pre_g, m_w_mlp_in=m_w_mlp_in, m_w_mlp_out=m_w_mlp_out, m_norm2_post_g=m_norm2_post_g, v_norm1_pre_g=v_norm1_pre_g, v_w_in=v_w_in, v_b_in=v_b_in, v_conv_a_w=v_conv_a_w, v_conv_a_b=v_conv_a_b, v_w_a_out=v_w_a_out, v_conv_b_w=v_conv_b_w, v_conv_b_b=v_conv_b_b, v_ln_b_g=v_ln_b_g, v_ln_b_b=v_ln_b_b, v_w_b_out=v_w_b_out, v_w_o=v_w_o, v_norm1_post_g=v_norm1_post_g, v_norm2_pre_g=v_norm2_pre_g, v_w_mlp_in=v_w_mlp_in, v_w_mlp_out=v_w_mlp_out, v_norm2_post_g=v_norm2_post_g)
    weights = {n: given[n] for n in TWIN_WEIGHTS}
    shared = {n: given[n] for n in SHARED_INPUTS}
    per_example = {n: given[n] for n in ['x']}
    grad_fn = _jax.value_and_grad(_loss, argnums=(0, 1))

    def one_microbatch(ex, loss_target):
        ex = dict(ex)
        diff = ex.pop(TWIN_DIFF_INPUT)
        return grad_fn(weights, diff, {**shared, **ex}, loss_target)

    if N_MICROBATCH == 1:
        loss, (grad_w, grad_x) = one_microbatch(per_example, given["loss_target"])
    else:
        def body(carry, xs):
            loss_sum, grad_sum = carry
            l_k, (gw_k, gx_k) = one_microbatch(xs[0], xs[1])
            with _jax.named_scope("update"):
                return (loss_sum + l_k, _jax.tree.map(_jnp.add, grad_sum, gw_k)), gx_k

        init = (_jnp.zeros((), _jnp.float32), _jax.tree.map(_jnp.zeros_like, weights))
        (loss, grad_w), grad_x = _jax.lax.scan(body, init, (per_example, given["loss_target"]))
    with _jax.named_scope("update"):
        delta_w, new_m, new_v = {}, {}, {}
        for n in TWIN_WEIGHTS:
            delta_w[n], new_m[n], new_v[n] = _adamw(weights[n], grad_w[n], given["m_" + n], given["v_" + n])
    return (loss, grad_x, *[grad_w[n] for n in TWIN_WEIGHTS], *[delta_w[n] for n in TWIN_WEIGHTS],
            *[new_m[n] for n in TWIN_WEIGHTS], *[new_v[n] for n in TWIN_WEIGHTS])
```

```python
import functools

import jax
import jax.numpy as jnp
from jax import lax
from jax.experimental import pallas as pl
from jax.experimental.pallas import tpu as pltpu

F32 = jnp.float32
BF16 = jnp.bfloat16

RMS_EPS = 1e-6
LN_EPS = 1e-5
ADAM_LR = 0.001
ADAM_B1 = 0.9
ADAM_B2 = 0.999
ADAM_EPS = 1e-08
ADAM_WD = 0.01
ADAM_STEP = 10

N_DEV = 8
CONV_A = 3
CONV_B = 31
LANE = 128
SUBLANE = 8
HALO = 16
CONV_ROWS = 64
MIB = 1 << 20
FLIPS = ((0, 0, 1), (0, 1, 0), (1, 0, 0), (0, 1, 1), (1, 0, 1), (1, 1, 0), (1, 1, 1))
MESH = pl.DeviceIdType.MESH


def _layout(d_model):
    rows = {"win": 7 * d_model // N_DEV, "wa": d_model // N_DEV, "w1": 4 * d_model // N_DEV,
            "w2": 4 * d_model // N_DEV, "wb": d_model // N_DEV, "wo": d_model // N_DEV}
    off, o = {}, 0
    for k in ("win", "wa", "w1", "w2", "wb", "wo"):
        off[k] = o
        o += rows[k]
    return rows, off, o


def _params(vmem_mib):
    return pltpu.CompilerParams(dimension_semantics=("arbitrary",), vmem_limit_bytes=vmem_mib * MIB)


def _whole():
    return pl.BlockSpec(memory_space=pltpu.VMEM)


def _hbm():
    return pl.BlockSpec(memory_space=pl.ANY)


def _rows(ts, width, col=0):
    return pl.BlockSpec((ts, width), lambda i: (i, col))


def _halo_specs(ts, width, n_rows):
    per = ts // HALO
    last = n_rows // HALO - 1
    return [
        pl.BlockSpec((ts, width), lambda i: (i, 0)),
        pl.BlockSpec((HALO, width), lambda i: (jnp.maximum(i * per - 1, 0), 0)),
        pl.BlockSpec((HALO, width), lambda i: (jnp.minimum((i + 1) * per, last), 0)),
    ]


def _dot(a, b):
    return jnp.dot(a, b, preferred_element_type=F32)


def _dot_nt(a, b):
    return lax.dot_general(a, b, (((1,), (1,)), ((), ())), preferred_element_type=F32)


def _dot_tn(a, b):
    return lax.dot_general(a, b, (((0,), (0,)), ((), ())), preferred_element_type=F32)


def _rms(u):
    return lax.rsqrt(jnp.mean(u * u, axis=-1, keepdims=True) + RMS_EPS)


def _rms_bwd(dz, u, r, g):
    dzg = dz * g
    return r * dzg - u * (r * r * r) * jnp.mean(dzg * u, axis=-1, keepdims=True)


def _colsum(v):
    return jnp.sum(v, axis=0, keepdims=True)


def _sigmoid(v):
    return jax.nn.sigmoid(v)


def _load_weight(slab_ref, off, rows, dst_ref, sems, first_sem):
    copies = []
    for d in range(N_DEV):
        cp = pltpu.make_async_copy(slab_ref.at[d, pl.ds(off, rows), :], dst_ref.at[pl.ds(d * rows, rows), :],
                                   sems.at[first_sem + d])
        cp.start()
        copies.append(cp)
    return copies


def _fill_ext(ext_ref, main_ref, prev_ref, next_ref, i, n_steps, ts):
    ext_ref[0:HALO, :] = jnp.where(i > 0, prev_ref[...], 0.0)
    ext_ref[HALO:HALO + ts, :] = main_ref[...]
    ext_ref[HALO + ts:HALO + ts + HALO, :] = jnp.where(i < n_steps - 1, next_ref[...], 0.0)


def _broadcast_taps(w_ref, wb_ref, n_taps):
    for k in range(n_taps):
        wb_ref[k] = jnp.broadcast_to(w_ref[k:k + 1, :], wb_ref.shape[1:])


def _phases(starts):
    groups = {}
    for k, st in enumerate(starts):
        groups.setdefault(st % SUBLANE, []).append((k, st // SUBLANE))
    return sorted(groups.items())


def _shifted(blk, b):
    n = blk.shape[0]
    rolled = blk if b == 0 else pltpu.roll(blk, n - b, axis=0)
    return rolled.reshape(n // SUBLANE, SUBLANE, blk.shape[1])


def _conv_tile(ext_ref, wb_ref, starts, ts, width, emit):
    span = CONV_ROWS + 2 * HALO
    groups = _phases(starts)
    nv = CONV_ROWS // SUBLANE

    def row_block(rb, carry):
        r0 = pl.multiple_of(rb * CONV_ROWS, CONV_ROWS)
        for cb in range(width // LANE):
            lanes = pl.ds(cb * LANE, LANE)
            blk = ext_ref[pl.ds(r0, span), lanes]
            acc = jnp.zeros((nv, SUBLANE, LANE), F32)
            for b, taps in groups:
                sh = _shifted(blk, b)
                for k, m in taps:
                    acc = acc + sh[m:m + nv] * wb_ref[k, :, lanes][None]
            emit(r0, lanes, acc.reshape(CONV_ROWS, LANE))
        return carry

    lax.fori_loop(0, ts // CONV_ROWS, row_block, 0)


def _conv_wgrad_tile(ext_ref, dv_ref, acc_ref, starts, ts, width):
    span = CONV_ROWS + 2 * HALO
    groups = _phases(starts)
    nv = CONV_ROWS // SUBLANE

    def row_block(rb, carry):
        r0 = pl.multiple_of(rb * CONV_ROWS, CONV_ROWS)
        for cb in range(width // LANE):
            lanes = pl.ds(cb * LANE, LANE)
            blk = ext_ref[pl.ds(r0, span), lanes]
            dv = dv_ref[pl.ds(r0, CONV_ROWS), lanes].reshape(nv, SUBLANE, LANE)
            for b, taps in groups:
                sh = _shifted(blk, b)
                for k, m in taps:
                    acc_ref[k, :, lanes] += jnp.sum(sh[m:m + nv] * dv, axis=0)
        return carry

    lax.fori_loop(0, ts // CONV_ROWS, row_block, 0)


def _fwd_starts(n_taps):
    pad = (n_taps - 1) // 2
    return [HALO - pad + k for k in range(n_taps)]


def _bwd_starts(n_taps):
    pad = (n_taps - 1) // 2
    return [HALO + pad - k for k in range(n_taps)]


def _peer(x, y, c, flip):
    fx, fy, fc = flip
    return (1 - x if fx else x, 1 - y if fy else y, 1 - c if fc else c)


def _all_gather(shards, name):
    n = len(shards)

    def body(*refs):
        ins, outs = refs[:n], refs[n:2 * n]
        send_sems, recv_sems, local_sems = refs[2 * n:]
        x, y, c = lax.axis_index("x"), lax.axis_index("y"), lax.axis_index("c")
        me = 4 * x + 2 * y + c
        local = [pltpu.make_async_copy(ins[j], outs[j].at[me], local_sems.at[j]) for j in range(n)]
        for cp in local:
            cp.start()
        sends, recvs = [], []
        for k, flip in enumerate(FLIPS):
            px, py, pc = _peer(x, y, c, flip)
            peer = 4 * px + 2 * py + pc
            for j in range(n):
                sem = k * n + j
                sends.append(pltpu.make_async_remote_copy(
                    src_ref=ins[j], dst_ref=outs[j].at[me], send_sem=send_sems.at[sem], recv_sem=recv_sems.at[sem],
                    device_id=(px, py, pc), device_id_type=MESH))
                recvs.append(pltpu.make_async_remote_copy(
                    src_ref=ins[j], dst_ref=outs[j].at[peer], send_sem=send_sems.at[sem], recv_sem=recv_sems.at[sem],
                    device_id=(px, py, pc), device_id_type=MESH))
        for cp in sends:
            cp.start()
        for cp in recvs:
            cp.wait_recv()
        for cp in sends:
            cp.wait_send()
        for cp in local:
            cp.wait()

    return pl.pallas_call(
        body, name=name,
        out_shape=[jax.ShapeDtypeStruct((N_DEV,) + s.shape, s.dtype) for s in shards],
        in_specs=[_hbm()] * n, out_specs=[_hbm()] * n,
        scratch_shapes=[pltpu.SemaphoreType.DMA((7 * n,)), pltpu.SemaphoreType.DMA((7 * n,)),
                        pltpu.SemaphoreType.DMA((n,))],
    )(*shards)


def _reduce_scatter_parts(parts, rows, offs, total_rows, small, name):
    n = len(parts)
    width = parts[0].shape[1]

    def body(*refs):
        ins, small_ref = refs[:n], refs[n]
        recv_ref, recv_small_ref = refs[n + 1], refs[n + 2]
        send_sems, recv_sems, local_sems = refs[n + 3:]
        x, y, c = lax.axis_index("x"), lax.axis_index("y"), lax.axis_index("c")
        me = 4 * x + 2 * y + c

        def src(j, dev):
            return ins[j].at[pl.ds(pl.multiple_of(dev * rows[j], rows[j]), rows[j]), :]

        def dst(j, dev):
            return recv_ref.at[dev, pl.ds(offs[j], rows[j]), :]

        local = [pltpu.make_async_copy(src(j, me), dst(j, me), local_sems.at[j]) for j in range(n)]
        local.append(pltpu.make_async_copy(small_ref, recv_small_ref.at[me], local_sems.at[n]))
        for cp in local:
            cp.start()
        sends, recvs = [], []
        for k, flip in enumerate(FLIPS):
            px, py, pc = _peer(x, y, c, flip)
            peer = 4 * px + 2 * py + pc
            for j in range(n + 1):
                sem = k * (n + 1) + j
                if j < n:
                    s_src, s_dst, r_dst = src(j, peer), dst(j, me), dst(j, peer)
                else:
                    s_src, s_dst, r_dst = small_ref, recv_small_ref.at[me], recv_small_ref.at[peer]
                sends.append(pltpu.make_async_remote_copy(
                    src_ref=s_src, dst_ref=s_dst, send_sem=send_sems.at[sem], recv_sem=recv_sems.at[sem],
                    device_id=(px, py, pc), device_id_type=MESH))
                recvs.append(pltpu.make_async_remote_copy(
                    src_ref=s_src, dst_ref=r_dst, send_sem=send_sems.at[sem], recv_sem=recv_sems.at[sem],
                    device_id=(px, py, pc), device_id_type=MESH))
        for cp in sends:
            cp.start()
        for cp in recvs:
            cp.wait_recv()
        for cp in sends:
            cp.wait_send()
        for cp in local:
            cp.wait()

    return pl.pallas_call(
        body, name=name,
        out_shape=[jax.ShapeDtypeStruct((N_DEV, total_rows, width), parts[0].dtype),
                   jax.ShapeDtypeStruct((N_DEV,) + small.shape, small.dtype)],
        in_specs=[_hbm()] * (n + 1), out_specs=[_hbm()] * 2,
        scratch_shapes=[pltpu.SemaphoreType.DMA((7 * (n + 1),)), pltpu.SemaphoreType.DMA((7 * (n + 1),)),
                        pltpu.SemaphoreType.DMA((n + 1,))],
    )(*parts, small)


def _fwd_in(x, g1, b_in, slab, ts):
    n_tok, dm = x.shape
    rows, off, _ = _layout(dm)
    width = 7 * dm

    def body(x_ref, g1_ref, b_ref, slab_ref, proj_ref, p_ref, u_ref, h_ref, w_v, sems):
        @pl.when(pl.program_id(0) == 0)
        def _():
            for cp in _load_weight(slab_ref, off["win"], rows["win"], w_v, sems, 0):
                cp.wait()

        xv = x_ref[...]
        h = (xv * _rms(xv) * g1_ref[...]).astype(BF16)
        h_ref[...] = h
        cols = []
        for j in range(7):
            pj = _dot_nt(h, w_v[pl.ds(j * dm, dm), :]) + b_ref[:, j * dm:(j + 1) * dm]
            proj_ref[:, j * dm:(j + 1) * dm] = pj.astype(proj_ref.dtype)
            if 1 <= j <= 4:
                cols.append(pj)
            if j == 2:
                p_ref[...] = cols[0] * cols[1]
            if j == 4:
                u_ref[...] = cols[2] * _sigmoid(cols[3])

    return pl.pallas_call(
        body, name="fwd_in", grid=(n_tok // ts,),
        in_specs=[_rows(ts, dm), _whole(), _whole(), _hbm()],
        out_specs=[_rows(ts, width), _rows(ts, dm), _rows(ts, dm), _rows(ts, dm)],
        out_shape=[jax.ShapeDtypeStruct((n_tok, width), BF16), jax.ShapeDtypeStruct((n_tok, dm), F32),
                   jax.ShapeDtypeStruct((n_tok, dm), F32), jax.ShapeDtypeStruct((n_tok, dm), BF16)],
        scratch_shapes=[pltpu.VMEM((width, dm), BF16), pltpu.SemaphoreType.DMA((N_DEV,))],
        compiler_params=_params(40),
    )(x, g1, b_in, slab)


def _fwd_mix(p, u, proj, x, caw, cab, cbw, cbb, lng, lnb, g1post, slab, ts):
    n_tok, dm = x.shape
    rows, off, _ = _layout(dm)
    n_steps = n_tok // ts

    def body(p_ref, p_prev, p_next, u_ref, u_prev, u_next, bg_ref, za_ref, zb_ref, x_ref,
             caw_ref, cab_ref, cbw_ref, cbb_ref, lng_ref, lnb_ref, g1p_ref, slab_ref,
             vb_ref, ya_ref, yb_ref, qa_ref, sb_ref, mg_ref, mix_ref, x1_ref,
             wa_v, wb_v, wo_v, ext_p, ext_u, va_v, tap_a, tap_b, sems):
        i = pl.program_id(0)

        @pl.when(i == 0)
        def _():
            cps = _load_weight(slab_ref, off["wa"], rows["wa"], wa_v, sems, 0)
            cps += _load_weight(slab_ref, off["wb"], rows["wb"], wb_v, sems, N_DEV)
            cps += _load_weight(slab_ref, off["wo"], rows["wo"], wo_v, sems, 2 * N_DEV)
            _broadcast_taps(caw_ref, tap_a, CONV_A)
            _broadcast_taps(cbw_ref, tap_b, CONV_B)
            for cp in cps:
                cp.wait()

        _fill_ext(ext_p, p_ref, p_prev, p_next, i, n_steps, ts)
        _fill_ext(ext_u, u_ref, u_prev, u_next, i, n_steps, ts)

        def emit_a(r0, lanes, acc):
            va_v[pl.ds(r0, CONV_ROWS), lanes] = acc + cab_ref[:, lanes]

        def emit_b(r0, lanes, acc):
            vb_ref[pl.ds(r0, CONV_ROWS), lanes] = acc + cbb_ref[:, lanes]

        _conv_tile(ext_p, tap_a, _fwd_starts(CONV_A), ts, dm, emit_a)
        _conv_tile(ext_u, tap_b, _fwd_starts(CONV_B), ts, dm, emit_b)

        qa = (bg_ref[...].astype(F32) * va_v[...]).astype(BF16)
        qa_ref[...] = qa
        ya = _dot(qa, wa_v[...])
        vb = vb_ref[...]
        xc = vb - jnp.mean(vb, axis=-1, keepdims=True)
        rstd = lax.rsqrt(jnp.mean(xc * xc, axis=-1, keepdims=True) + LN_EPS)
        ln = xc * rstd * lng_ref[...] + lnb_ref[...]
        sb = (ln * _sigmoid(ln)).astype(BF16)
        sb_ref[...] = sb
        yb = _dot(sb, wb_v[...])
        ya_ref[...] = ya.astype(BF16)
        yb_ref[...] = yb.astype(BF16)
        merged = (_sigmoid(za_ref[...].astype(F32)) * ya + _sigmoid(zb_ref[...].astype(F32)) * yb).astype(BF16)
        mg_ref[...] = merged
        mix = _dot(merged, wo_v[...])
        mix_ref[...] = mix
        x1_ref[...] = x_ref[...] + mix * _rms(mix) * g1p_ref[...]

    tok = lambda dt: jax.ShapeDtypeStruct((n_tok, dm), dt)
    return pl.pallas_call(
        body, name="fwd_mix", grid=(n_steps,),
        in_specs=(_halo_specs(ts, dm, n_tok) + _halo_specs(ts, dm, n_tok)
                  + [_rows(ts, dm, 0), _rows(ts, dm, 5), _rows(ts, dm, 6), _rows(ts, dm)]
                  + [_whole()] * 7 + [_hbm()]),
        out_specs=[_rows(ts, dm)] * 8,
        out_shape=[tok(F32), tok(BF16), tok(BF16), tok(BF16), tok(BF16), tok(BF16), tok(F32), tok(F32)],
        scratch_shapes=[pltpu.VMEM((dm, dm), BF16), pltpu.VMEM((dm, dm), BF16), pltpu.VMEM((dm, dm), BF16),
                        pltpu.VMEM((ts + 2 * HALO, dm), F32), pltpu.VMEM((ts + 2 * HALO, dm), F32),
                        pltpu.VMEM((ts, dm), F32),
                        pltpu.VMEM((CONV_A, SUBLANE, dm), F32), pltpu.VMEM((CONV_B, SUBLANE, dm), F32),
                        pltpu.SemaphoreType.DMA((3 * N_DEV,))],
        compiler_params=_params(48),
    )(p, p, p, u, u, u, proj, proj, proj, x, caw, cab, cbw, cbb, lng, lnb, g1post, slab)


def _mlp_fwd_bwd(x1, mix, tgt, g1post, g2pre, g2post, slab, ts):
    n_tok, dm = x1.shape
    rows, off, _ = _layout(dm)
    ff = 4 * dm

    def body(x1_ref, mix_ref, t_ref, g1p_ref, g2pre_ref, g2post_ref, slab_ref,
             f_ref, df1_ref, h2_ref, df2_ref, dmix_ref, dx1_ref, small_ref, w1_v, w2_v, relu_v, sems):
        @pl.when(pl.program_id(0) == 0)
        def _():
            cps = _load_weight(slab_ref, off["w1"], rows["w1"], w1_v, sems, 0)
            cps += _load_weight(slab_ref, off["w2"], rows["w2"], w2_v, sems, N_DEV)
            small_ref[...] = jnp.zeros_like(small_ref)
            for cp in cps:
                cp.wait()

        x1v = x1_ref[...]
        r3 = _rms(x1v)
        g2pre = g2pre_ref[...]
        h2 = (x1v * r3 * g2pre).astype(BF16)
        h2_ref[...] = h2
        f2 = jnp.zeros((ts, dm), F32)
        for c in range(4):
            blk = pl.ds(c * dm, dm)
            relu = jnp.maximum(_dot_nt(h2, w1_v[blk, :]), 0.0)
            relu_v[:, c * dm:(c + 1) * dm] = relu
            fc = (relu * relu).astype(BF16)
            f_ref[:, c * dm:(c + 1) * dm] = fc
            f2 = f2 + _dot(fc, w2_v[blk, :])
        r4 = _rms(f2)
        g2post = g2post_ref[...]
        err = x1v + f2 * r4 * g2post - t_ref[...]
        dy = err * (1.0 / dm)
        small_ref[3:4, :] += _colsum(err * err)
        small_ref[0:1, :] += _colsum(dy * f2 * r4)
        df2 = _rms_bwd(dy, f2, r4, g2post).astype(BF16)
        df2_ref[...] = df2
        dh2 = jnp.zeros((ts, dm), F32)
        for c in range(4):
            blk = pl.ds(c * dm, dm)
            df1 = (_dot_nt(df2, w2_v[blk, :]) * (2.0 * relu_v[:, c * dm:(c + 1) * dm])).astype(BF16)
            df1_ref[:, c * dm:(c + 1) * dm] = df1
            dh2 = dh2 + _dot(df1, w1_v[blk, :])
        small_ref[1:2, :] += _colsum(dh2 * x1v * r3)
        dx1 = dy + _rms_bwd(dh2, x1v, r3, g2pre)
        dx1_ref[...] = dx1
        mixv = mix_ref[...]
        r2 = _rms(mixv)
        small_ref[2:3, :] += _colsum(dx1 * mixv * r2)
        dmix_ref[...] = _rms_bwd(dx1, mixv, r2, g1p_ref[...]).astype(BF16)

    tok = lambda w, dt: jax.ShapeDtypeStruct((n_tok, w), dt)
    return pl.pallas_call(
        body, name="mlp_fwd_bwd", grid=(n_tok // ts,),
        in_specs=[_rows(ts, dm)] * 3 + [_whole()] * 3 + [_hbm()],
        out_specs=[_rows(ts, ff), _rows(ts, ff), _rows(ts, dm), _rows(ts, dm), _rows(ts, dm), _rows(ts, dm),
                   pl.BlockSpec((SUBLANE, dm), lambda i: (0, 0))],
        out_shape=[tok(ff, BF16), tok(ff, BF16), tok(dm, BF16), tok(dm, BF16), tok(dm, BF16), tok(dm, F32),
                   jax.ShapeDtypeStruct((SUBLANE, dm), F32)],
        scratch_shapes=[pltpu.VMEM((ff, dm), BF16), pltpu.VMEM((ff, dm), BF16), pltpu.VMEM((ts, ff), F32),
                        pltpu.SemaphoreType.DMA((2 * N_DEV,))],
        compiler_params=_params(56),
    )(x1, mix, tgt, g1post, g2pre, g2post, slab)


def _bwd_mix(dmix, ya, yb, proj, p, vb, caw, cab, lng, lnb, slab, ts):
    n_tok, dm = dmix.shape
    rows, off, _ = _layout(dm)
    n_steps = n_tok // ts

    def body(dmix_ref, ya_ref, yb_ref, bg_ref, za_ref, zb_ref, p_ref, p_prev, p_next, vb_ref,
             caw_ref, cab_ref, lng_ref, lnb_ref, slab_ref,
             dpa_ref, dya_ref, dyb_ref, dva_ref, dvb_ref, small_ref,
             wa_v, wb_v, wo_v, ext_p, va_v, tap_a, sems):
        i = pl.program_id(0)

        @pl.when(i == 0)
        def _():
            cps = _load_weight(slab_ref, off["wa"], rows["wa"], wa_v, sems, 0)
            cps += _load_weight(slab_ref, off["wb"], rows["wb"], wb_v, sems, N_DEV)
            cps += _load_weight(slab_ref, off["wo"], rows["wo"], wo_v, sems, 2 * N_DEV)
            _broadcast_taps(caw_ref, tap_a, CONV_A)
            small_ref[...] = jnp.zeros_like(small_ref)
            for cp in cps:
                cp.wait()

        _fill_ext(ext_p, p_ref, p_prev, p_next, i, n_steps, ts)

        def emit_a(r0, lanes, acc):
            va_v[pl.ds(r0, CONV_ROWS), lanes] = acc + cab_ref[:, lanes]

        _conv_tile(ext_p, tap_a, _fwd_starts(CONV_A), ts, dm, emit_a)

        dmerged = _dot_nt(dmix_ref[...], wo_v[...])
        sa = _sigmoid(za_ref[...].astype(F32))
        sg = _sigmoid(zb_ref[...].astype(F32))
        dza = dmerged * ya_ref[...].astype(F32) * sa * (1.0 - sa)
        dzb = dmerged * yb_ref[...].astype(F32) * sg * (1.0 - sg)
        dpa_ref[:, dm:2 * dm] = dza.astype(BF16)
        dpa_ref[:, 2 * dm:3 * dm] = dzb.astype(BF16)
        small_ref[5:6, :] += _colsum(dza)
        small_ref[6:7, :] += _colsum(dzb)

        dya = (dmerged * sa).astype(BF16)
        dya_ref[...] = dya
        dqa = _dot_nt(dya, wa_v[...])
        dbg = dqa * va_v[...]
        dpa_ref[:, 0:dm] = dbg.astype(BF16)
        small_ref[4:5, :] += _colsum(dbg)
        dva = dqa * bg_ref[...].astype(F32)
        dva_ref[...] = dva
        small_ref[2:3, :] += _colsum(dva)

        dyb = (dmerged * sg).astype(BF16)
        dyb_ref[...] = dyb
        dsb = _dot_nt(dyb, wb_v[...])
        vb = vb_ref[...]
        xc = vb - jnp.mean(vb, axis=-1, keepdims=True)
        rstd = lax.rsqrt(jnp.mean(xc * xc, axis=-1, keepdims=True) + LN_EPS)
        nrm = xc * rstd
        lng_v = lng_ref[...]
        ln = nrm * lng_v + lnb_ref[...]
        sl = _sigmoid(ln)
        dln = dsb * (sl * (1.0 + ln * (1.0 - sl)))
        small_ref[0:1, :] += _colsum(dln * nrm)
        small_ref[1:2, :] += _colsum(dln)
        dn = dln * lng_v
        dvb = rstd * (dn - jnp.mean(dn, axis=-1, keepdims=True)
                      - nrm * jnp.mean(dn * nrm, axis=-1, keepdims=True))
        dvb_ref[...] = dvb
        small_ref[3:4, :] += _colsum(dvb)

    tok = lambda w, dt: jax.ShapeDtypeStruct((n_tok, w), dt)
    return pl.pallas_call(
        body, name="bwd_mix", grid=(n_steps,),
        in_specs=([_rows(ts, dm)] * 3 + [_rows(ts, dm, 0), _rows(ts, dm, 5), _rows(ts, dm, 6)]
                  + _halo_specs(ts, dm, n_tok) + [_rows(ts, dm)] + [_whole()] * 4 + [_hbm()]),
        out_specs=[_rows(ts, 3 * dm), _rows(ts, dm), _rows(ts, dm), _rows(ts, dm), _rows(ts, dm),
                   pl.BlockSpec((SUBLANE, dm), lambda i: (0, 0))],
        out_shape=[tok(3 * dm, BF16), tok(dm, BF16), tok(dm, BF16), tok(dm, F32), tok(dm, F32),
                   jax.ShapeDtypeStruct((SUBLANE, dm), F32)],
        scratch_shapes=[pltpu.VMEM((dm, dm), BF16), pltpu.VMEM((dm, dm), BF16), pltpu.VMEM((dm, dm), BF16),
                        pltpu.VMEM((ts + 2 * HALO, dm), F32), pltpu.VMEM((ts, dm), F32),
                        pltpu.VMEM((CONV_A, SUBLANE, dm), F32), pltpu.SemaphoreType.DMA((3 * N_DEV,))],
        compiler_params=_params(48),
    )(dmix, ya, yb, proj, proj, proj, p, p, p, vb, caw, cab, lng, lnb, slab)


def _bwd_conv(dva, dvb, p, u, proj, dpa, caw, cbw, ts):
    n_tok, dm = dva.shape
    n_steps = n_tok // ts
    small_rows = 40

    def body(dva_ref, dva_prev, dva_next, dvb_ref, dvb_prev, dvb_next, p_ref, p_prev, p_next,
             u_ref, u_prev, u_next, cg_ref, ha_ref, a_ref, g_ref, dpa_ref, caw_ref, cbw_ref,
             dproj_ref, small_ref,
             ext_dva, ext_dvb, ext_p, ext_u, dp_v, du_v, tap_a, tap_b, gwa_v, gwb_v):
        i = pl.program_id(0)

        @pl.when(i == 0)
        def _():
            _broadcast_taps(caw_ref, tap_a, CONV_A)
            _broadcast_taps(cbw_ref, tap_b, CONV_B)
            small_ref[...] = jnp.zeros_like(small_ref)
            gwa_v[...] = jnp.zeros_like(gwa_v)
            gwb_v[...] = jnp.zeros_like(gwb_v)

        _fill_ext(ext_dva, dva_ref, dva_prev, dva_next, i, n_steps, ts)
        _fill_ext(ext_dvb, dvb_ref, dvb_prev, dvb_next, i, n_steps, ts)
        _fill_ext(ext_p, p_ref, p_prev, p_next, i, n_steps, ts)
        _fill_ext(ext_u, u_ref, u_prev, u_next, i, n_steps, ts)

        def emit_dp(r0, lanes, acc):
            dp_v[pl.ds(r0, CONV_ROWS), lanes] = acc

        def emit_du(r0, lanes, acc):
            du_v[pl.ds(r0, CONV_ROWS), lanes] = acc

        _conv_tile(ext_dva, tap_a, _bwd_starts(CONV_A), ts, dm, emit_dp)
        _conv_tile(ext_dvb, tap_b, _bwd_starts(CONV_B), ts, dm, emit_du)
        _conv_wgrad_tile(ext_p, dva_ref, gwa_v, _fwd_starts(CONV_A), ts, dm)
        _conv_wgrad_tile(ext_u, dvb_ref, gwb_v, _fwd_starts(CONV_B), ts, dm)

        dp = dp_v[...]
        dcg = dp * ha_ref[...].astype(F32)
        dha = dp * cg_ref[...].astype(F32)
        du = du_v[...]
        sg = _sigmoid(g_ref[...].astype(F32))
        da = du * sg
        dg = du * a_ref[...].astype(F32) * sg * (1.0 - sg)
        dproj_ref[:, 0:dm] = dpa_ref[:, 0:dm]
        dproj_ref[:, dm:2 * dm] = dcg.astype(BF16)
        dproj_ref[:, 2 * dm:3 * dm] = dha.astype(BF16)
        dproj_ref[:, 3 * dm:4 * dm] = da.astype(BF16)
        dproj_ref[:, 4 * dm:5 * dm] = dg.astype(BF16)
        dproj_ref[:, 5 * dm:7 * dm] = dpa_ref[:, dm:3 * dm]
        small_ref[3:4, :] += _colsum(dcg)
        small_ref[4:5, :] += _colsum(dha)
        small_ref[5:6, :] += _colsum(da)
        small_ref[6:7, :] += _colsum(dg)

        @pl.when(i == n_steps - 1)
        def _():
            for k in range(CONV_A):
                small_ref[k:k + 1, :] = _colsum(gwa_v[k])
            for k in range(CONV_B):
                small_ref[SUBLANE + k:SUBLANE + k + 1, :] = _colsum(gwb_v[k])

    ext = pltpu.VMEM((ts + 2 * HALO, dm), F32)
    return pl.pallas_call(
        body, name="bwd_conv", grid=(n_steps,),
        in_specs=(_halo_specs(ts, dm, n_tok) * 4
                  + [_rows(ts, dm, 1), _rows(ts, dm, 2), _rows(ts, dm, 3), _rows(ts, dm, 4), _rows(ts, 3 * dm)]
                  + [_whole()] * 2),
        out_specs=[_rows(ts, 7 * dm), pl.BlockSpec((small_rows, dm), lambda i: (0, 0))],
        out_shape=[jax.ShapeDtypeStruct((n_tok, 7 * dm), BF16), jax.ShapeDtypeStruct((small_rows, dm), F32)],
        scratch_shapes=[ext, ext, ext, ext, pltpu.VMEM((ts, dm), F32), pltpu.VMEM((ts, dm), F32),
                        pltpu.VMEM((CONV_A, SUBLANE, dm), F32), pltpu.VMEM((CONV_B, SUBLANE, dm), F32),
                        pltpu.VMEM((CONV_A, SUBLANE, dm), F32), pltpu.VMEM((CONV_B, SUBLANE, dm), F32)],
        compiler_params=_params(48),
    )(dva, dva, dva, dvb, dvb, dvb, p, p, p, u, u, u, proj, proj, proj, proj, dpa, caw, cbw)


def _bwd_in(dproj, x, dx1, g1, slab, ts):
    n_tok, dm = x.shape
    rows, off, _ = _layout(dm)
    width = 7 * dm

    def body(dproj_ref, x_ref, dx1_ref, g1_ref, slab_ref, gx_ref, small_ref, w_v, sems):
        @pl.when(pl.program_id(0) == 0)
        def _():
            cps = _load_weight(slab_ref, off["win"], rows["win"], w_v, sems, 0)
            small_ref[...] = jnp.zeros_like(small_ref)
            for cp in cps:
                cp.wait()

        dh = _dot(dproj_ref[...], w_v[...])
        xv = x_ref[...]
        r1 = _rms(xv)
        small_ref[0:1, :] += _colsum(dh * xv * r1)
        gx_ref[...] = dx1_ref[...] + _rms_bwd(dh, xv, r1, g1_ref[...])

    return pl.pallas_call(
        body, name="bwd_in", grid=(n_tok // ts,),
        in_specs=[_rows(ts, width), _rows(ts, dm), _rows(ts, dm), _whole(), _hbm()],
        out_specs=[_rows(ts, dm), pl.BlockSpec((SUBLANE, dm), lambda i: (0, 0))],
        out_shape=[jax.ShapeDtypeStruct((n_tok, dm), F32), jax.ShapeDtypeStruct((SUBLANE, dm), F32)],
        scratch_shapes=[pltpu.VMEM((width, dm), BF16), pltpu.SemaphoreType.DMA((N_DEV,))],
        compiler_params=_params(40),
    )(dproj, x, dx1, g1, slab)


def _wgrad(a, b, name, tm, tk, out_dtype):
    n_tok, m = a.shape
    n = b.shape[1]
    k_steps = n_tok // tk

    def body(a_ref, b_ref, o_ref, acc_v):
        k = pl.program_id(1)

        @pl.when(k == 0)
        def _():
            acc_v[...] = jnp.zeros_like(acc_v)

        acc_v[...] += _dot_tn(a_ref[...], b_ref[...])

        @pl.when(k == k_steps - 1)
        def _():
            o_ref[...] = acc_v[...].astype(o_ref.dtype)

    return pl.pallas_call(
        body, name=name, grid=(m // tm, k_steps),
        in_specs=[pl.BlockSpec((tk, tm), lambda i, k: (k, i)), pl.BlockSpec((tk, n), lambda i, k: (k, 0))],
        out_specs=pl.BlockSpec((tm, n), lambda i, k: (i, 0)),
        out_shape=jax.ShapeDtypeStruct((m, n), out_dtype),
        scratch_shapes=[pltpu.VMEM((tm, n), F32)],
        compiler_params=pltpu.CompilerParams(dimension_semantics=("arbitrary", "arbitrary"),
                                             vmem_limit_bytes=40 * MIB),
    )(a, b)


def _adamw(w, g, m, v):
    m = ADAM_B1 * m + (1.0 - ADAM_B1) * g
    v = ADAM_B2 * v + (1.0 - ADAM_B2) * (g * g)
    m_hat = m / (1.0 - ADAM_B1 ** ADAM_STEP)
    v_hat = v / (1.0 - ADAM_B2 ** ADAM_STEP)
    delta = -ADAM_LR * (m_hat / (jnp.sqrt(v_hat) + ADAM_EPS) + ADAM_WD * w)
    return delta, m, v


def _adam_big(recv, off, rows, w, m, v, transpose, name, tr):
    dm = recv.shape[2]

    def body(r_ref, w_ref, m_ref, v_ref, g_ref, d_ref, mo_ref, vo_ref):
        g = r_ref[0].astype(F32)
        for d in range(1, N_DEV):
            g = g + r_ref[d].astype(F32)
        if transpose:
            g = g.T
        delta, m_new, v_new = _adamw(w_ref[...], g, m_ref[...], v_ref[...])
        g_ref[...] = g
        d_ref[...] = delta
        mo_ref[...] = m_new
        vo_ref[...] = v_new

    if transpose:
        blk = pl.BlockSpec((dm, tr), lambda i: (0, i))
    else:
        blk = pl.BlockSpec((tr, dm), lambda i: (i, 0))
    first = off // tr
    return pl.pallas_call(
        body, name=name, grid=(rows // tr,),
        in_specs=[pl.BlockSpec((N_DEV, tr, dm), lambda i: (0, first + i, 0)), blk, blk, blk],
        out_specs=[blk] * 4,
        out_shape=[jax.ShapeDtypeStruct(w.shape, F32)] * 4,
        compiler_params=_params(32),
    )(recv, w, m, v)


def _sum_slots(recv_small):
    def body(r_ref, o_ref):
        g = r_ref[0]
        for d in range(1, N_DEV):
            g = g + r_ref[d]
        o_ref[...] = g

    return pl.pallas_call(
        body, name="sum_small", in_specs=[_whole()], out_specs=_whole(),
        out_shape=jax.ShapeDtypeStruct(recv_small.shape[1:], F32),
    )(recv_small)


def _adam_small(g, w, m, v, name):
    def body(g_ref, w_ref, m_ref, v_ref, d_ref, mo_ref, vo_ref):
        delta, m_new, v_new = _adamw(w_ref[...], g_ref[...], m_ref[...], v_ref[...])
        d_ref[...] = delta
        mo_ref[...] = m_new
        vo_ref[...] = v_new

    return pl.pallas_call(
        body, name=name, in_specs=[_whole()] * 4, out_specs=[_whole()] * 3,
        out_shape=[jax.ShapeDtypeStruct(w.shape, F32)] * 3,
    )(g, w, m, v)


def _tile(n_tok, want):
    return min(want, n_tok)


def kernel(x, norm1_pre_g, w_in, b_in, conv_a_w, conv_a_b, w_a_out, conv_b_w, conv_b_b, ln_b_g, ln_b_b, w_b_out, w_o, norm1_post_g, norm2_pre_g, w_mlp_in, w_mlp_out, norm2_post_g, loss_target, m_norm1_pre_g, m_w_in, m_b_in, m_conv_a_w, m_conv_a_b, m_w_a_out, m_conv_b_w, m_conv_b_b, m_ln_b_g, m_ln_b_b, m_w_b_out, m_w_o, m_norm1_post_g, m_norm2_pre_g, m_w_mlp_in, m_w_mlp_out, m_norm2_post_g, v_norm1_pre_g, v_w_in, v_b_in, v_conv_a_w, v_conv_a_b, v_w_a_out, v_conv_b_w, v_conv_b_b, v_ln_b_g, v_ln_b_b, v_w_b_out, v_w_o, v_norm1_post_g, v_norm2_pre_g, v_w_mlp_in, v_w_mlp_out, v_norm2_post_g):
    n_tok, dm = x.shape[1], x.shape[2]
    rows, off, slab_rows = _layout(dm)
    cw = dm // N_DEV
    xs = x.reshape(n_tok, dm)
    tgt = loss_target.reshape(n_tok, dm)
    row = lambda vec: vec.reshape(1, -1)

    own = jnp.concatenate([w_in.T, w_a_out, w_mlp_in.T, w_mlp_out, w_b_out, w_o], axis=0).astype(BF16)
    conv_own = jnp.concatenate([conv_a_w, jnp.zeros((SUBLANE - CONV_A, cw), F32), conv_b_w,
                                jnp.zeros((1, cw), F32)], axis=0)
    slab, conv_all = _all_gather([own, conv_own], "gather_weights")
    conv_full = conv_all.transpose(1, 0, 2).reshape(conv_own.shape[0], dm)
    caw, cbw = conv_full[0:CONV_A], conv_full[SUBLANE:SUBLANE + CONV_B]

    proj, p, u, h = _fwd_in(xs, row(norm1_pre_g), row(b_in), slab, _tile(n_tok, 256))
    vb, ya, yb, qa, sb, merged, mix, x1 = _fwd_mix(
        p, u, proj, xs, caw, row(conv_a_b), cbw, row(conv_b_b), row(ln_b_g), row(ln_b_b), row(norm1_post_g),
        slab, _tile(n_tok, 256))
    f, df1, h2, df2, dmix, dx1, small_mlp = _mlp_fwd_bwd(
        x1, mix, tgt, row(norm1_post_g), row(norm2_pre_g), row(norm2_post_g), slab, _tile(n_tok, 256))

    dpa, dya, dyb, dva, dvb, small_mix = _bwd_mix(
        dmix, ya, yb, proj, p, vb, caw, row(conv_a_b), row(ln_b_g), row(ln_b_b), slab, _tile(n_tok, 256))
    dproj, small_conv = _bwd_conv(dva, dvb, p, u, proj, dpa, caw, cbw, _tile(n_tok, 256))
    grad_x, small_in = _bwd_in(dproj, xs, dx1, row(norm1_pre_g), slab, _tile(n_tok, 256))

    tk = _tile(n_tok, 512)
    parts = {
        "win": _wgrad(dproj, h, "wgrad_in", min(dm, 1024), tk, BF16),
        "wa": _wgrad(qa, dya, "wgrad_a_out", min(dm, 1024), tk, BF16),
        "w1": _wgrad(df1, h2, "wgrad_mlp_in", min(dm, 1024), tk, BF16),
        "w2": _wgrad(f, df2, "wgrad_mlp_out", min(dm, 1024), tk, BF16),
        "wb": _wgrad(sb, dyb, "wgrad_b_out", min(dm, 1024), tk, BF16),
        "wo": _wgrad(merged, dmix, "wgrad_o", min(dm, 1024), tk, BF16),
    }

    zeros = lambda r: jnp.zeros((r, dm), F32)
    small = jnp.concatenate([
        small_in[0:1],
        small_mix[2:3],
        small_mix[3:4],
        small_mix[0:2],
        small_mlp[2:3],
        small_mlp[1:2],
        small_mlp[0:1],
        small_mix[4:5], small_conv[3:7], small_mix[5:7],
        zeros(1),
        small_conv[0:CONV_A], zeros(SUBLANE - CONV_A),
        small_conv[8:8 + CONV_B], zeros(1),
    ], axis=0)

    names = ("win", "wa", "w1", "w2", "wb", "wo")
    recv, recv_small = _reduce_scatter_parts(
        [parts[k] for k in names], [rows[k] for k in names], [off[k] for k in names], slab_rows, small,
        "scatter_grads")

    tr = min(LANE, rows["wa"])
    big = {
        "w_in": _adam_big(recv, off["win"], rows["win"], w_in, m_w_in, v_w_in, True, "adam_w_in", tr),
        "w_a_out": _adam_big(recv, off["wa"], rows["wa"], w_a_out, m_w_a_out, v_w_a_out, False, "adam_w_a_out", tr),
        "w_mlp_in": _adam_big(recv, off["w1"], rows["w1"], w_mlp_in, m_w_mlp_in, v_w_mlp_in, True,
                              "adam_w_mlp_in", tr),
        "w_mlp_out": _adam_big(recv, off["w2"], rows["w2"], w_mlp_out, m_w_mlp_out, v_w_mlp_out, False,
                               "adam_w_mlp_out", tr),
        "w_b_out": _adam_big(recv, off["wb"], rows["wb"], w_b_out, m_w_b_out, v_w_b_out, False, "adam_w_b_out", tr),
        "w_o": _adam_big(recv, off["wo"], rows["wo"], w_o, m_w_o, v_w_o, False, "adam_w_o", tr),
    }

    gsum = _sum_slots(recv_small)
    rep = lambda vec8, bias: jnp.concatenate([jnp.stack(vec8, axis=0), bias.reshape(7, dm), zeros(1)], axis=0)
    w_rep = rep((norm1_pre_g, conv_a_b, conv_b_b, ln_b_g, ln_b_b, norm1_post_g, norm2_pre_g, norm2_post_g), b_in)
    m_rep = rep((m_norm1_pre_g, m_conv_a_b, m_conv_b_b, m_ln_b_g, m_ln_b_b, m_norm1_post_g, m_norm2_pre_g,
                 m_norm2_post_g), m_b_in)
    v_rep = rep((v_norm1_pre_g, v_conv_a_b, v_conv_b_b, v_ln_b_g, v_ln_b_b, v_norm1_post_g, v_norm2_pre_g,
                 v_norm2_post_g), v_b_in)
    g_rep = gsum[0:16]
    d_rep, mo_rep, vo_rep = _adam_small(g_rep, w_rep, m_rep, v_rep, "adam_replicated")

    me = 4 * lax.axis_index("x") + 2 * lax.axis_index("y") + lax.axis_index("c")
    g_conv = lax.dynamic_slice_in_dim(gsum[16:56], me * cw, cw, axis=1)
    pad_conv = lambda a3, b31: jnp.concatenate(
        [a3, jnp.zeros((SUBLANE - CONV_A, cw), F32), b31, jnp.zeros((1, cw), F32)], axis=0)
    d_conv, mo_conv, vo_conv = _adam_small(
        g_conv, pad_conv(conv_a_w, conv_b_w), pad_conv(m_conv_a_w, m_conv_b_w),
        pad_conv(v_conv_a_w, v_conv_b_w), "adam_conv")

    def small_leaf(stack_rep, stack_conv, name):
        idx = {"norm1_pre_g": 0, "conv_a_b": 1, "conv_b_b": 2, "ln_b_g": 3, "ln_b_b": 4, "norm1_post_g": 5,
               "norm2_pre_g": 6, "norm2_post_g": 7}
        if name in idx:
            return stack_rep[idx[name]]
        if name == "b_in":
            return stack_rep[8:15].reshape(7 * dm)
        if name == "conv_a_w":
            return stack_conv[0:CONV_A]
        return stack_conv[SUBLANE:SUBLANE + CONV_B]

    order = ("norm1_pre_g", "w_in", "b_in", "conv_a_w", "conv_a_b", "w_a_out", "conv_b_w", "conv_b_b", "ln_b_g",
             "ln_b_b", "w_b_out", "w_o", "norm1_post_g", "norm2_pre_g", "w_mlp_in", "w_mlp_out", "norm2_post_g")
    grads, deltas, new_m, new_v = [], [], [], []
    for name in order:
        if name in big:
            g, d, mo, vo = big[name]
        else:
            g = small_leaf(g_rep, g_conv, name)
            d = small_leaf(d_rep, d_conv, name)
            mo = small_leaf(mo_rep, mo_conv, name)
            vo = small_leaf(vo_rep, vo_conv, name)
        grads.append(g)
        deltas.append(d)
        new_m.append(mo)
        new_v.append(vo)

    loss_local = (0.5 / dm) * jnp.sum(small_mlp[3])
    loss = lax.psum(loss_local, ("x", "y", "c"))
    return (loss, grad_x.reshape(x.shape), *grads, *deltas, *new_m, *new_v)
```

```python
from typing import NamedTuple

import jax
import jax.numpy as jnp
from jax import lax
from jax.experimental import pallas as pl
from jax.experimental.pallas import tpu as pltpu

F32 = jnp.float32
BF16 = jnp.bfloat16

RMS_EPS = 1e-6
LN_EPS = 1e-5
ADAM_LR = 0.001
ADAM_B1 = 0.9
ADAM_B2 = 0.999
ADAM_EPS = 1e-08
ADAM_WD = 0.01
ADAM_STEP = 10

N_DEV = 8
CONV_A = 3
CONV_B = 31
LANE = 128
SUBLANE = 8
HALO = 16
CONV_ROWS = 64
MIB = 1 << 20
FLIPS = ((0, 0, 1), (0, 1, 0), (1, 0, 0), (0, 1, 1), (1, 0, 1), (1, 1, 0), (1, 1, 1))
MESH = pl.DeviceIdType.MESH


def _layout(d_model):
    e = d_model // N_DEV
    rows = {"win": 7 * e, "w1": 4 * e, "w2": 4 * e, "wa": e, "wb": e, "wo": e}
    off = {"win": 0, "w1": 0, "w2": 4 * e, "wa": 0, "wb": e, "wo": 2 * e}
    return rows, off, {"in": 7 * e, "mlp": 8 * e, "abo": 3 * e}


def _after(body, deps):
    def ordered(*refs):
        return body(*refs[len(deps):])
    return ordered


def _params(vmem_mib):
    return pltpu.CompilerParams(dimension_semantics=("arbitrary",), vmem_limit_bytes=vmem_mib * MIB)


def _whole():
    return pl.BlockSpec(memory_space=pltpu.VMEM)


def _hbm():
    return pl.BlockSpec(memory_space=pl.ANY)


def _rows(ts, width, col=0):
    return pl.BlockSpec((ts, width), lambda i: (i, col))


def _halo_specs(ts, width, n_rows):
    per = ts // HALO
    last = n_rows // HALO - 1
    return [
        pl.BlockSpec((ts, width), lambda i: (i, 0)),
        pl.BlockSpec((HALO, width), lambda i: (jnp.maximum(i * per - 1, 0), 0)),
        pl.BlockSpec((HALO, width), lambda i: (jnp.minimum((i + 1) * per, last), 0)),
    ]


def _dot(a, b):
    return jnp.dot(a, b, preferred_element_type=F32)


def _dot_nt(a, b):
    return lax.dot_general(a, b, (((1,), (1,)), ((), ())), preferred_element_type=F32)


def _dot_tn(a, b):
    return lax.dot_general(a, b, (((0,), (0,)), ((), ())), preferred_element_type=F32)


def _rms(u):
    return lax.rsqrt(jnp.mean(u * u, axis=-1, keepdims=True) + RMS_EPS)


def _rms_bwd(dz, u, r, g):
    dzg = dz * g
    return r * dzg - u * (r * r * r) * jnp.mean(dzg * u, axis=-1, keepdims=True)


def _colsum(v):
    return jnp.sum(v, axis=0, keepdims=True)


def _sigmoid(v):
    return jax.nn.sigmoid(v)


def _load_weight(slab_ref, off, rows, dst_ref, sems, first_sem):
    copies = []
    for d in range(N_DEV):
        cp = pltpu.make_async_copy(slab_ref.at[d, pl.ds(off, rows), :], dst_ref.at[pl.ds(d * rows, rows), :],
                                   sems.at[first_sem + d])
        cp.start()
        copies.append(cp)
    return copies


def _fill_ext(ext_ref, main_ref, prev_ref, next_ref, i, n_steps, ts):
    ext_ref[0:HALO, :] = jnp.where(i > 0, prev_ref[...], 0.0)
    ext_ref[HALO:HALO + ts, :] = main_ref[...]
    ext_ref[HALO + ts:HALO + ts + HALO, :] = jnp.where(i < n_steps - 1, next_ref[...], 0.0)


def _broadcast_taps(w_ref, wb_ref, n_taps):
    for k in range(n_taps):
        wb_ref[k] = jnp.broadcast_to(w_ref[k:k + 1, :], wb_ref.shape[1:])


def _phases(starts):
    groups = {}
    for k, st in enumerate(starts):
        groups.setdefault(st % SUBLANE, []).append((k, st // SUBLANE))
    return sorted(groups.items())


def _shifted(blk, b):
    n = blk.shape[0]
    rolled = blk if b == 0 else pltpu.roll(blk, n - b, axis=0)
    return rolled.reshape(n // SUBLANE, SUBLANE, blk.shape[1])


def _conv_tile(ext_ref, wb_ref, starts, ts, width, emit):
    span = CONV_ROWS + 2 * HALO
    groups = _phases(starts)
    nv = CONV_ROWS // SUBLANE

    def row_block(rb, carry):
        r0 = pl.multiple_of(rb * CONV_ROWS, CONV_ROWS)
        for cb in range(width // LANE):
            lanes = pl.ds(cb * LANE, LANE)
            blk = ext_ref[pl.ds(r0, span), lanes]
            acc = jnp.zeros((nv, SUBLANE, LANE), F32)
            for b, taps in groups:
                sh = _shifted(blk, b)
                for k, m in taps:
                    acc = acc + sh[m:m + nv] * wb_ref[k, :, lanes][None]
            emit(r0, lanes, acc.reshape(CONV_ROWS, LANE))
        return carry

    lax.fori_loop(0, ts // CONV_ROWS, row_block, 0)


def _conv_wgrad_tile(ext_ref, dv_ref, acc_ref, starts, ts, width):
    span = CONV_ROWS + 2 * HALO
    groups = _phases(starts)
    nv = CONV_ROWS // SUBLANE

    def row_block(rb, carry):
        r0 = pl.multiple_of(rb * CONV_ROWS, CONV_ROWS)
        for cb in range(width // LANE):
            lanes = pl.ds(cb * LANE, LANE)
            blk = ext_ref[pl.ds(r0, span), lanes]
            dv = dv_ref[pl.ds(r0, CONV_ROWS), lanes].reshape(nv, SUBLANE, LANE)
            for b, taps in groups:
                sh = _shifted(blk, b)
                for k, m in taps:
                    acc_ref[k, :, lanes] += jnp.sum(sh[m:m + nv] * dv, axis=0)
        return carry

    lax.fori_loop(0, ts // CONV_ROWS, row_block, 0)


def _fwd_starts(n_taps):
    pad = (n_taps - 1) // 2
    return [HALO - pad + k for k in range(n_taps)]


def _bwd_starts(n_taps):
    pad = (n_taps - 1) // 2
    return [HALO + pad - k for k in range(n_taps)]


def _peer(x, y, c, flip):
    fx, fy, fc = flip
    return (1 - x if fx else x, 1 - y if fy else y, 1 - c if fc else c)


def _all_gather(shards, name):
    n = len(shards)

    def body(*refs):
        ins, outs = refs[:n], refs[n:2 * n]
        send_sems, recv_sems, local_sems = refs[2 * n:]
        x, y, c = lax.axis_index("x"), lax.axis_index("y"), lax.axis_index("c")
        me = 4 * x + 2 * y + c
        local = [pltpu.make_async_copy(ins[j], outs[j].at[me], local_sems.at[j]) for j in range(n)]
        for cp in local:
            cp.start()
        sends, recvs = [], []
        for k, flip in enumerate(FLIPS):
            px, py, pc = _peer(x, y, c, flip)
            peer = 4 * px + 2 * py + pc
            for j in range(n):
                sem = k * n + j
                sends.append(pltpu.make_async_remote_copy(
                    src_ref=ins[j], dst_ref=outs[j].at[me], send_sem=send_sems.at[sem], recv_sem=recv_sems.at[sem],
                    device_id=(px, py, pc), device_id_type=MESH))
                recvs.append(pltpu.make_async_remote_copy(
                    src_ref=ins[j], dst_ref=outs[j].at[peer], send_sem=send_sems.at[sem], recv_sem=recv_sems.at[sem],
                    device_id=(px, py, pc), device_id_type=MESH))
        for cp in sends:
            cp.start()
        for cp in recvs:
            cp.wait_recv()
        for cp in sends:
            cp.wait_send()
        for cp in local:
            cp.wait()

    return pl.pallas_call(
        body, name=name,
        out_shape=[jax.ShapeDtypeStruct((N_DEV,) + s.shape, s.dtype) for s in shards],
        in_specs=[_hbm()] * n, out_specs=[_hbm()] * n,
        scratch_shapes=[pltpu.SemaphoreType.DMA((7 * n,)), pltpu.SemaphoreType.DMA((7 * n,)),
                        pltpu.SemaphoreType.DMA((n,))],
    )(*shards)


def _all_gather_two_level(shards, name):
    n = len(shards)

    def body(*refs):
        ins, outs = refs[:n], refs[n:2 * n]
        send_sems, recv_sems, local_sems = refs[2 * n:]
        x, y, c = lax.axis_index("x"), lax.axis_index("y"), lax.axis_index("c")
        me, sibling = (x, y, c), (x, y, 1 - c)
        chips = [(1 - x, y), (x, 1 - y), (1 - x, 1 - y)]

        def slot(j, dev):
            return outs[j].at[4 * dev[0] + 2 * dev[1] + dev[2]]

        def copy(k, j, block, to, src=None):
            return pltpu.make_async_remote_copy(
                src_ref=slot(j, block) if src is None else src, dst_ref=slot(j, block),
                send_sem=send_sems.at[k * n + j], recv_sem=recv_sems.at[k * n + j], device_id=to, device_id_type=MESH)

        local = [pltpu.make_async_copy(ins[j], slot(j, me), local_sems.at[j]) for j in range(n)]
        for cp in local:
            cp.start()
        first = [copy(0, j, me, sibling, src=ins[j]) for j in range(n)]
        first += [copy(1 + t, j, me, (*chip, c), src=ins[j]) for t, chip in enumerate(chips) for j in range(n)]
        for cp in first:
            cp.start()
        passed = []
        for t, chip in enumerate(chips):
            for j in range(n):
                copy(1 + t, j, (*chip, c), me).wait_recv()
                passed.append(copy(4 + t, j, (*chip, c), sibling))
                passed[-1].start()
        for j in range(n):
            copy(0, j, sibling, me).wait_recv()
        for t, chip in enumerate(chips):
            for j in range(n):
                copy(4 + t, j, (*chip, 1 - c), me).wait_recv()
        for cp in first + passed:
            cp.wait_send()
        for cp in local:
            cp.wait()

    return pl.pallas_call(
        body, name=name,
        out_shape=[jax.ShapeDtypeStruct((N_DEV,) + s.shape, s.dtype) for s in shards],
        in_specs=[_hbm()] * n, out_specs=[_hbm()] * n,
        scratch_shapes=[pltpu.SemaphoreType.DMA((7 * n,)), pltpu.SemaphoreType.DMA((7 * n,)),
                        pltpu.SemaphoreType.DMA((n,))],
    )(*shards)


class _Part(NamedTuple):
    src: jax.Array
    scatter: bool
    rows: int
    land: int
    off: int


class _Started(NamedTuple):
    send_sems: jax.Array
    recv_sems: jax.Array
    thru: tuple
    token: jax.Array
    parts: tuple


def _exchange_copies(srcs, lands, send_sems, recv_sems, parts):
    n = len(parts)
    x, y, c = lax.axis_index("x"), lax.axis_index("y"), lax.axis_index("c")
    me = 4 * x + 2 * y + c

    def block(j, dev):
        p = parts[j]
        return srcs[j].at[pl.ds(pl.multiple_of(dev * p.rows, SUBLANE), p.rows), :] if p.scatter else srcs[j]

    def slot(j, dev):
        p = parts[j]
        return lands[p.land].at[dev, pl.ds(p.off, p.rows), :]

    local = [(block(j, me), slot(j, me)) for j in range(n)]
    sends, recvs = [], []
    for k, flip in enumerate(FLIPS):
        px, py, pc = _peer(x, y, c, flip)
        peer = 4 * px + 2 * py + pc
        for j in range(n):
            sems = dict(send_sem=send_sems.at[k * n + j], recv_sem=recv_sems.at[k * n + j],
                        device_id=(px, py, pc), device_id_type=MESH)
            sends.append(pltpu.make_async_remote_copy(src_ref=block(j, peer), dst_ref=slot(j, me), **sems))
            recvs.append(pltpu.make_async_remote_copy(src_ref=block(j, peer), dst_ref=slot(j, peer), **sems))
    return sends, recvs, local


def _exchange_start(parts, lands, name, after=None):
    n, nl = len(parts), len(lands)
    n_in = n + nl + (after is not None)

    def body(*refs):
        srcs, land_refs = refs[:n], refs[n:n + nl]
        send_sems, recv_sems = refs[n_in], refs[n_in + 1]
        token, local_sems = refs[n_in + 2 + n + nl], refs[n_in + 3 + n + nl]
        sends, _, local = _exchange_copies(srcs, land_refs, send_sems, recv_sems, parts)
        own = [pltpu.make_async_copy(s, d, local_sems.at[j]) for j, (s, d) in enumerate(local)]
        for cp in own:
            cp.start()
        for cp in sends:
            cp.start()
        token[...] = jnp.zeros_like(token)
        for cp in own:
            cp.wait()

    hbm = pl.BlockSpec(memory_space=pltpu.HBM)
    sem = pl.BlockSpec(memory_space=pltpu.SEMAPHORE)
    args = [pltpu.with_memory_space_constraint(p.src, pltpu.HBM) for p in parts]
    args += [pltpu.with_memory_space_constraint(lax.empty(s.shape, s.dtype), pltpu.HBM) for s in lands]
    args += [] if after is None else [after]
    out = pl.pallas_call(
        body, name=name,
        out_shape=(pltpu.SemaphoreType.DMA((7 * n,)), pltpu.SemaphoreType.DMA((7 * n,)),
                   *[pltpu.HBM(a.shape, a.dtype) for a in args[:n + nl]], jax.ShapeDtypeStruct((SUBLANE, LANE), F32)),
        in_specs=[hbm] * (n + nl) + [_hbm()] * (after is not None),
        out_specs=(sem, sem, *[hbm] * (n + nl), _whole()),
        input_output_aliases={j: 2 + j for j in range(n + nl)},
        scratch_shapes=[pltpu.SemaphoreType.DMA((n,))],
        compiler_params=pltpu.CompilerParams(has_side_effects=pltpu.SideEffectType.DATAFLOW_SIDE_EFFECTING),
    )(*args)
    return _Started(out[0], out[1], tuple(out[2:2 + n + nl]), out[2 + n + nl], tuple(parts))


def _exchange_wait(started, name, after):
    parts = started.parts
    n, nl = len(parts), len(started.thru) - len(parts)

    def body(*refs):
        srcs, land_refs = refs[:n], refs[n:n + nl]
        send_sems, recv_sems = refs[n + nl], refs[n + nl + 1]
        sends, recvs, _ = _exchange_copies(srcs, land_refs, send_sems, recv_sems, parts)
        for cp in sends:
            cp.wait_send()
        for cp in recvs:
            cp.wait_recv()

    hbm = pl.BlockSpec(memory_space=pltpu.HBM)
    sem = pl.BlockSpec(memory_space=pltpu.SEMAPHORE)
    out = pl.pallas_call(
        body, name=name,
        out_shape=tuple(pltpu.HBM(a.shape, a.dtype) for a in started.thru),
        in_specs=[hbm] * (n + nl) + [sem, sem, _hbm()], out_specs=tuple([hbm] * (n + nl)),
        input_output_aliases={j: j for j in range(n + nl)},
        compiler_params=pltpu.CompilerParams(has_side_effects=pltpu.SideEffectType.DATAFLOW_SIDE_EFFECTING),
    )(*started.thru, started.send_sems, started.recv_sems, after)
    return list(out[n:])


def _fwd_in(x, g1, b_in, slab, ts, deps):
    n_tok, dm = x.shape
    rows, off, _ = _layout(dm)
    width = 7 * dm

    def body(x_ref, g1_ref, b_ref, slab_ref, proj_ref, p_ref, u_ref, h_ref, w_v, sems):
        @pl.when(pl.program_id(0) == 0)
        def _():
            for cp in _load_weight(slab_ref, off["win"], rows["win"], w_v, sems, 0):
                cp.wait()

        xv = x_ref[...]
        h = (xv * _rms(xv) * g1_ref[...]).astype(BF16)
        h_ref[...] = h
        cols = []
        for j in range(7):
            pj = _dot_nt(h, w_v[pl.ds(j * dm, dm), :]) + b_ref[:, j * dm:(j + 1) * dm]
            proj_ref[:, j * dm:(j + 1) * dm] = pj.astype(proj_ref.dtype)
            if 1 <= j <= 4:
                cols.append(pj)
            if j == 2:
                p_ref[...] = cols[0] * cols[1]
            if j == 4:
                u_ref[...] = cols[2] * _sigmoid(cols[3])

    return pl.pallas_call(
        _after(body, deps), name="fwd_in", grid=(n_tok // ts,),
        in_specs=[_whole()] * len(deps) + [_rows(ts, dm), _whole(), _whole(), _hbm()],
        out_specs=[_rows(ts, width), _rows(ts, dm), _rows(ts, dm), _rows(ts, dm)],
        out_shape=[jax.ShapeDtypeStruct((n_tok, width), BF16), jax.ShapeDtypeStruct((n_tok, dm), F32),
                   jax.ShapeDtypeStruct((n_tok, dm), F32), jax.ShapeDtypeStruct((n_tok, dm), BF16)],
        scratch_shapes=[pltpu.VMEM((width, dm), BF16), pltpu.SemaphoreType.DMA((N_DEV,))],
        compiler_params=_params(40),
    )(*deps, x, g1, b_in, slab)


def _fwd_mix(p, u, proj, x, caw, cab, cbw, cbb, lng, lnb, g1post, slab, ts):
    n_tok, dm = x.shape
    rows, off, _ = _layout(dm)
    n_steps = n_tok // ts

    def body(p_ref, p_prev, p_next, u_ref, u_prev, u_next, bg_ref, za_ref, zb_ref, x_ref,
             caw_ref, cab_ref, cbw_ref, cbb_ref, lng_ref, lnb_ref, g1p_ref, slab_ref,
             vb_ref, ya_ref, yb_ref, qa_ref, sb_ref, mg_ref, mix_ref, x1_ref,
             wa_v, wb_v, wo_v, ext_p, ext_u, va_v, tap_a, tap_b, sems):
        i = pl.program_id(0)

        @pl.when(i == 0)
        def _():
            cps = _load_weight(slab_ref, off["wa"], rows["wa"], wa_v, sems, 0)
            cps += _load_weight(slab_ref, off["wb"], rows["wb"], wb_v, sems, N_DEV)
            cps += _load_weight(slab_ref, off["wo"], rows["wo"], wo_v, sems, 2 * N_DEV)
            _broadcast_taps(caw_ref, tap_a, CONV_A)
            _broadcast_taps(cbw_ref, tap_b, CONV_B)
            for cp in cps:
                cp.wait()

        _fill_ext(ext_p, p_ref, p_prev, p_next, i, n_steps, ts)
        _fill_ext(ext_u, u_ref, u_prev, u_next, i, n_steps, ts)

        def emit_a(r0, lanes, acc):
            va_v[pl.ds(r0, CONV_ROWS), lanes] = acc + cab_ref[:, lanes]

        def emit_b(r0, lanes, acc):
            vb_ref[pl.ds(r0, CONV_ROWS), lanes] = acc + cbb_ref[:, lanes]

        _conv_tile(ext_p, tap_a, _fwd_starts(CONV_A), ts, dm, emit_a)
        _conv_tile(ext_u, tap_b, _fwd_starts(CONV_B), ts, dm, emit_b)

        qa = (bg_ref[...].astype(F32) * va_v[...]).astype(BF16)
        qa_ref[...] = qa
        ya = _dot(qa, wa_v[...])
        vb = vb_ref[...]
        xc = vb - jnp.mean(vb, axis=-1, keepdims=True)
        rstd = lax.rsqrt(jnp.mean(xc * xc, axis=-1, keepdims=True) + LN_EPS)
        ln = xc * rstd * lng_ref[...] + lnb_ref[...]
        sb = (ln * _sigmoid(ln)).astype(BF16)
        sb_ref[...] = sb
        yb = _dot(sb, wb_v[...])
        ya_ref[...] = ya.astype(BF16)
        yb_ref[...] = yb.astype(BF16)
        merged = (_sigmoid(za_ref[...].astype(F32)) * ya + _sigmoid(zb_ref[...].astype(F32)) * yb).astype(BF16)
        mg_ref[...] = merged
        mix = _dot(merged, wo_v[...])
        mix_ref[...] = mix
        x1_ref[...] = x_ref[...] + mix * _rms(mix) * g1p_ref[...]

    tok = lambda dt: jax.ShapeDtypeStruct((n_tok, dm), dt)
    return pl.pallas_call(
        body, name="fwd_mix", grid=(n_steps,),
        in_specs=(_halo_specs(ts, dm, n_tok) + _halo_specs(ts, dm, n_tok)
                  + [_rows(ts, dm, 0), _rows(ts, dm, 5), _rows(ts, dm, 6), _rows(ts, dm)]
                  + [_whole()] * 7 + [_hbm()]),
        out_specs=[_rows(ts, dm)] * 8,
        out_shape=[tok(F32), tok(BF16), tok(BF16), tok(BF16), tok(BF16), tok(BF16), tok(F32), tok(F32)],
        scratch_shapes=[pltpu.VMEM((dm, dm), BF16), pltpu.VMEM((dm, dm), BF16), pltpu.VMEM((dm, dm), BF16),
                        pltpu.VMEM((ts + 2 * HALO, dm), F32), pltpu.VMEM((ts + 2 * HALO, dm), F32),
                        pltpu.VMEM((ts, dm), F32),
                        pltpu.VMEM((CONV_A, SUBLANE, dm), F32), pltpu.VMEM((CONV_B, SUBLANE, dm), F32),
                        pltpu.SemaphoreType.DMA((3 * N_DEV,))],
        compiler_params=_params(48),
    )(p, p, p, u, u, u, proj, proj, proj, x, caw, cab, cbw, cbb, lng, lnb, g1post, slab)


def _mlp_fwd_bwd(x1, mix, tgt, g1post, g2pre, g2post, slab, ts):
    n_tok, dm = x1.shape
    rows, off, _ = _layout(dm)
    ff = 4 * dm

    def body(x1_ref, mix_ref, t_ref, g1p_ref, g2pre_ref, g2post_ref, slab_ref,
             f_ref, df1_ref, h2_ref, df2_ref, dmix_ref, dx1_ref, small_ref, w1_v, w2_v, relu_v, sems):
        @pl.when(pl.program_id(0) == 0)
        def _():
            cps = _load_weight(slab_ref, off["w1"], rows["w1"], w1_v, sems, 0)
            cps += _load_weight(slab_ref, off["w2"], rows["w2"], w2_v, sems, N_DEV)
            small_ref[...] = jnp.zeros_like(small_ref)
            for cp in cps:
                cp.wait()

        x1v = x1_ref[...]
        r3 = _rms(x1v)
        g2pre = g2pre_ref[...]
        h2 = (x1v * r3 * g2pre).astype(BF16)
        h2_ref[...] = h2
        f2 = jnp.zeros((ts, dm), F32)
        for c in range(4):
            blk = pl.ds(c * dm, dm)
            relu = jnp.maximum(_dot_nt(h2, w1_v[blk, :]), 0.0)
            relu_v[:, c * dm:(c + 1) * dm] = relu
            fc = (relu * relu).astype(BF16)
            f_ref[:, c * dm:(c + 1) * dm] = fc
            f2 = f2 + _dot(fc, w2_v[blk, :])
        r4 = _rms(f2)
        g2post = g2post_ref[...]
        err = x1v + f2 * r4 * g2post - t_ref[...]
        dy = err * (1.0 / dm)
        small_ref[3:4, :] += _colsum(err * err)
        small_ref[0:1, :] += _colsum(dy * f2 * r4)
        df2 = _rms_bwd(dy, f2, r4, g2post).astype(BF16)
        df2_ref[...] = df2
        dh2 = jnp.zeros((ts, dm), F32)
        for c in range(4):
            blk = pl.ds(c * dm, dm)
            df1 = (_dot_nt(df2, w2_v[blk, :]) * (2.0 * relu_v[:, c * dm:(c + 1) * dm])).astype(BF16)
            df1_ref[:, c * dm:(c + 1) * dm] = df1
            dh2 = dh2 + _dot(df1, w1_v[blk, :])
        small_ref[1:2, :] += _colsum(dh2 * x1v * r3)
        dx1 = dy + _rms_bwd(dh2, x1v, r3, g2pre)
        dx1_ref[...] = dx1
        mixv = mix_ref[...]
        r2 = _rms(mixv)
        small_ref[2:3, :] += _colsum(dx1 * mixv * r2)
        dmix_ref[...] = _rms_bwd(dx1, mixv, r2, g1p_ref[...]).astype(BF16)

    tok = lambda w, dt: jax.ShapeDtypeStruct((n_tok, w), dt)
    return pl.pallas_call(
        body, name="mlp_fwd_bwd", grid=(n_tok // ts,),
        in_specs=[_rows(ts, dm)] * 3 + [_whole()] * 3 + [_hbm()],
        out_specs=[_rows(ts, ff), _rows(ts, ff), _rows(ts, dm), _rows(ts, dm), _rows(ts, dm), _rows(ts, dm),
                   pl.BlockSpec((SUBLANE, dm), lambda i: (0, 0))],
        out_shape=[tok(ff, BF16), tok(ff, BF16), tok(dm, BF16), tok(dm, BF16), tok(dm, BF16), tok(dm, F32),
                   jax.ShapeDtypeStruct((SUBLANE, dm), F32)],
        scratch_shapes=[pltpu.VMEM((ff, dm), BF16), pltpu.VMEM((ff, dm), BF16), pltpu.VMEM((ts, ff), F32),
                        pltpu.SemaphoreType.DMA((2 * N_DEV,))],
        compiler_params=_params(56),
    )(x1, mix, tgt, g1post, g2pre, g2post, slab)


def _bwd_mix(dmix, ya, yb, proj, p, vb, caw, cab, lng, lnb, slab, ts, deps):
    n_tok, dm = dmix.shape
    rows, off, _ = _layout(dm)
    n_steps = n_tok // ts

    def body(dmix_ref, ya_ref, yb_ref, bg_ref, za_ref, zb_ref, p_ref, p_prev, p_next, vb_ref,
             caw_ref, cab_ref, lng_ref, lnb_ref, slab_ref,
             dpa_ref, dya_ref, dyb_ref, dva_ref, dvb_ref, small_ref,
             wa_v, wb_v, wo_v, ext_p, va_v, tap_a, sems):
        i = pl.program_id(0)

        @pl.when(i == 0)
        def _():
            cps = _load_weight(slab_ref, off["wa"], rows["wa"], wa_v, sems, 0)
            cps += _load_weight(slab_ref, off["wb"], rows["wb"], wb_v, sems, N_DEV)
            cps += _load_weight(slab_ref, off["wo"], rows["wo"], wo_v, sems, 2 * N_DEV)
            _broadcast_taps(caw_ref, tap_a, CONV_A)
            small_ref[...] = jnp.zeros_like(small_ref)
            for cp in cps:
                cp.wait()

        _fill_ext(ext_p, p_ref, p_prev, p_next, i, n_steps, ts)

        def emit_a(r0, lanes, acc):
            va_v[pl.ds(r0, CONV_ROWS), lanes] = acc + cab_ref[:, lanes]

        _conv_tile(ext_p, tap_a, _fwd_starts(CONV_A), ts, dm, emit_a)

        dmerged = _dot_nt(dmix_ref[...], wo_v[...])
        sa = _sigmoid(za_ref[...].astype(F32))
        sg = _sigmoid(zb_ref[...].astype(F32))
        dza = dmerged * ya_ref[...].astype(F32) * sa * (1.0 - sa)
        dzb = dmerged * yb_ref[...].astype(F32) * sg * (1.0 - sg)
        dpa_ref[:, dm:2 * dm] = dza.astype(BF16)
        dpa_ref[:, 2 * dm:3 * dm] = dzb.astype(BF16)
        small_ref[5:6, :] += _colsum(dza)
        small_ref[6:7, :] += _colsum(dzb)

        dya = (dmerged * sa).astype(BF16)
        dya_ref[...] = dya
        dqa = _dot_nt(dya, wa_v[...])
        dbg = dqa * va_v[...]
        dpa_ref[:, 0:dm] = dbg.astype(BF16)
        small_ref[4:5, :] += _colsum(dbg)
        dva = dqa * bg_ref[...].astype(F32)
        dva_ref[...] = dva
        small_ref[2:3, :] += _colsum(dva)

        dyb = (dmerged * sg).astype(BF16)
        dyb_ref[...] = dyb
        dsb = _dot_nt(dyb, wb_v[...])
        vb = vb_ref[...]
        xc = vb - jnp.mean(vb, axis=-1, keepdims=True)
        rstd = lax.rsqrt(jnp.mean(xc * xc, axis=-1, keepdims=True) + LN_EPS)
        nrm = xc * rstd
        lng_v = lng_ref[...]
        ln = nrm * lng_v + lnb_ref[...]
        sl = _sigmoid(ln)
        dln = dsb * (sl * (1.0 + ln * (1.0 - sl)))
        small_ref[0:1, :] += _colsum(dln * nrm)
        small_ref[1:2, :] += _colsum(dln)
        dn = dln * lng_v
        dvb = rstd * (dn - jnp.mean(dn, axis=-1, keepdims=True)
                      - nrm * jnp.mean(dn * nrm, axis=-1, keepdims=True))
        dvb_ref[...] = dvb
        small_ref[3:4, :] += _colsum(dvb)

    tok = lambda w, dt: jax.ShapeDtypeStruct((n_tok, w), dt)
    return pl.pallas_call(
        _after(body, deps), name="bwd_mix", grid=(n_steps,),
        in_specs=([_whole()] * len(deps) + [_rows(ts, dm)] * 3+ [_rows(ts, dm, 0), _rows(ts, dm, 5), _rows(ts, dm, 6)]
                  + _halo_specs(ts, dm, n_tok) + [_rows(ts, dm)] + [_whole()] * 4 + [_hbm()]),
        out_specs=[_rows(ts, 3 * dm), _rows(ts, dm), _rows(ts, dm), _rows(ts, dm), _rows(ts, dm),
                   pl.BlockSpec((SUBLANE, dm), lambda i: (0, 0))],
        out_shape=[tok(3 * dm, BF16), tok(dm, BF16), tok(dm, BF16), tok(dm, F32), tok(dm, F32),
                   jax.ShapeDtypeStruct((SUBLANE, dm), F32)],
        scratch_shapes=[pltpu.VMEM((dm, dm), BF16), pltpu.VMEM((dm, dm), BF16), pltpu.VMEM((dm, dm), BF16),
                        pltpu.VMEM((ts + 2 * HALO, dm), F32), pltpu.VMEM((ts, dm), F32),
                        pltpu.VMEM((CONV_A, SUBLANE, dm), F32), pltpu.SemaphoreType.DMA((3 * N_DEV,))],
        compiler_params=_params(48),
    )(*deps, dmix, ya, yb, proj, proj, proj, p, p, p, vb, caw, cab, lng, lnb, slab)


def _bwd_conv(dva, dvb, p, u, proj, dpa, caw, cbw, ts, deps):
    n_tok, dm = dva.shape
    n_steps = n_tok // ts
    small_rows = 40

    def body(dva_ref, dva_prev, dva_next, dvb_ref, dvb_prev, dvb_next, p_ref, p_prev, p_next,
             u_ref, u_prev, u_next, cg_ref, ha_ref, a_ref, g_ref, dpa_ref, caw_ref, cbw_ref,
             dproj_ref, small_ref,
             ext_dva, ext_dvb, ext_p, ext_u, dp_v, du_v, tap_a, tap_b, gwa_v, gwb_v):
        i = pl.program_id(0)

        @pl.when(i == 0)
        def _():
            _broadcast_taps(caw_ref, tap_a, CONV_A)
            _broadcast_taps(cbw_ref, tap_b, CONV_B)
            small_ref[...] = jnp.zeros_like(small_ref)
            gwa_v[...] = jnp.zeros_like(gwa_v)
            gwb_v[...] = jnp.zeros_like(gwb_v)

        _fill_ext(ext_dva, dva_ref, dva_prev, dva_next, i, n_steps, ts)
        _fill_ext(ext_dvb, dvb_ref, dvb_prev, dvb_next, i, n_steps, ts)
        _fill_ext(ext_p, p_ref, p_prev, p_next, i, n_steps, ts)
        _fill_ext(ext_u, u_ref, u_prev, u_next, i, n_steps, ts)

        def emit_dp(r0, lanes, acc):
            dp_v[pl.ds(r0, CONV_ROWS), lanes] = acc

        def emit_du(r0, lanes, acc):
            du_v[pl.ds(r0, CONV_ROWS), lanes] = acc

        _conv_tile(ext_dva, tap_a, _bwd_starts(CONV_A), ts, dm, emit_dp)
        _conv_tile(ext_dvb, tap_b, _bwd_starts(CONV_B), ts, dm, emit_du)
        _conv_wgrad_tile(ext_p, dva_ref, gwa_v, _fwd_starts(CONV_A), ts, dm)
        _conv_wgrad_tile(ext_u, dvb_ref, gwb_v, _fwd_starts(CONV_B), ts, dm)

        dp = dp_v[...]
        dcg = dp * ha_ref[...].astype(F32)
        dha = dp * cg_ref[...].astype(F32)
        du = du_v[...]
        sg = _sigmoid(g_ref[...].astype(F32))
        da = du * sg
        dg = du * a_ref[...].astype(F32) * sg * (1.0 - sg)
        dproj_ref[:, 0:dm] = dpa_ref[:, 0:dm]
        dproj_ref[:, dm:2 * dm] = dcg.astype(BF16)
        dproj_ref[:, 2 * dm:3 * dm] = dha.astype(BF16)
        dproj_ref[:, 3 * dm:4 * dm] = da.astype(BF16)
        dproj_ref[:, 4 * dm:5 * dm] = dg.astype(BF16)
        dproj_ref[:, 5 * dm:7 * dm] = dpa_ref[:, dm:3 * dm]
        small_ref[3:4, :] += _colsum(dcg)
        small_ref[4:5, :] += _colsum(dha)
        small_ref[5:6, :] += _colsum(da)
        small_ref[6:7, :] += _colsum(dg)

        @pl.when(i == n_steps - 1)
        def _():
            for k in range(CONV_A):
                small_ref[k:k + 1, :] = _colsum(gwa_v[k])
            for k in range(CONV_B):
                small_ref[SUBLANE + k:SUBLANE + k + 1, :] = _colsum(gwb_v[k])

    ext = pltpu.VMEM((ts + 2 * HALO, dm), F32)
    return pl.pallas_call(
        _after(body, deps), name="bwd_conv", grid=(n_steps,),
        in_specs=([_whole()] * len(deps) + _halo_specs(ts, dm, n_tok) * 4
                  + [_rows(ts, dm, 1), _rows(ts, dm, 2), _rows(ts, dm, 3), _rows(ts, dm, 4), _rows(ts, 3 * dm)]
                  + [_whole()] * 2),
        out_specs=[_rows(ts, 7 * dm), pl.BlockSpec((small_rows, dm), lambda i: (0, 0))],
        out_shape=[jax.ShapeDtypeStruct((n_tok, 7 * dm), BF16), jax.ShapeDtypeStruct((small_rows, dm), F32)],
        scratch_shapes=[ext, ext, ext, ext, pltpu.VMEM((ts, dm), F32), pltpu.VMEM((ts, dm), F32),
                        pltpu.VMEM((CONV_A, SUBLANE, dm), F32), pltpu.VMEM((CONV_B, SUBLANE, dm), F32),
                        pltpu.VMEM((CONV_A, SUBLANE, dm), F32), pltpu.VMEM((CONV_B, SUBLANE, dm), F32)],
        compiler_params=_params(48),
    )(*deps, dva, dva, dva, dvb, dvb, dvb, p, p, p, u, u, u, proj, proj, proj, proj, dpa, caw, cbw)


def _bwd_in(dproj, x, dx1, g1, slab, ts, deps):
    n_tok, dm = x.shape
    rows, off, _ = _layout(dm)
    width = 7 * dm

    def body(dproj_ref, x_ref, dx1_ref, g1_ref, slab_ref, gx_ref, small_ref, w_v, sems):
        @pl.when(pl.program_id(0) == 0)
        def _():
            cps = _load_weight(slab_ref, off["win"], rows["win"], w_v, sems, 0)
            small_ref[...] = jnp.zeros_like(small_ref)
            for cp in cps:
                cp.wait()

        dh = _dot(dproj_ref[...], w_v[...])
        xv = x_ref[...]
        r1 = _rms(xv)
        small_ref[0:1, :] += _colsum(dh * xv * r1)
        gx_ref[...] = dx1_ref[...] + _rms_bwd(dh, xv, r1, g1_ref[...])

    return pl.pallas_call(
        _after(body, deps), name="bwd_in", grid=(n_tok // ts,),
        in_specs=[_whole()] * len(deps) + [_rows(ts, width), _rows(ts, dm), _rows(ts, dm), _whole(), _hbm()],
        out_specs=[_rows(ts, dm), pl.BlockSpec((SUBLANE, dm), lambda i: (0, 0))],
        out_shape=[jax.ShapeDtypeStruct((n_tok, dm), F32), jax.ShapeDtypeStruct((SUBLANE, dm), F32)],
        scratch_shapes=[pltpu.VMEM((width, dm), BF16), pltpu.SemaphoreType.DMA((N_DEV,))],
        compiler_params=_params(40),
    )(*deps, dproj, x, dx1, g1, slab)


def _wgrad(a, b, name, tm, tk, out_dtype):
    n_tok, m = a.shape
    n = b.shape[1]
    k_steps = n_tok // tk

    def body(a_ref, b_ref, o_ref, acc_v):
        k = pl.program_id(1)

        @pl.when(k == 0)
        def _():
            acc_v[...] = jnp.zeros_like(acc_v)

        acc_v[...] += _dot_tn(a_ref[...], b_ref[...])

        @pl.when(k == k_steps - 1)
        def _():
            o_ref[...] = acc_v[...].astype(o_ref.dtype)

    return pl.pallas_call(
        body, name=name, grid=(m // tm, k_steps),
        in_specs=[pl.BlockSpec((tk, tm), lambda i, k: (k, i)), pl.BlockSpec((tk, n), lambda i, k: (k, 0))],
        out_specs=pl.BlockSpec((tm, n), lambda i, k: (i, 0)),
        out_shape=jax.ShapeDtypeStruct((m, n), out_dtype),
        scratch_shapes=[pltpu.VMEM((tm, n), F32)],
        compiler_params=pltpu.CompilerParams(dimension_semantics=("arbitrary", "arbitrary"),
                                             vmem_limit_bytes=40 * MIB),
    )(a, b)


def _adamw(w, g, m, v):
    m = ADAM_B1 * m + (1.0 - ADAM_B1) * g
    v = ADAM_B2 * v + (1.0 - ADAM_B2) * (g * g)
    m_hat = m / (1.0 - ADAM_B1 ** ADAM_STEP)
    v_hat = v / (1.0 - ADAM_B2 ** ADAM_STEP)
    delta = -ADAM_LR * (m_hat / (jnp.sqrt(v_hat) + ADAM_EPS) + ADAM_WD * w)
    return delta, m, v


def _adam_big(recv, off, rows, w, m, v, transpose, name, tr):
    dm = recv.shape[2]

    def body(r_ref, w_ref, m_ref, v_ref, g_ref, d_ref, mo_ref, vo_ref):
        g = r_ref[0].astype(F32)
        for d in range(1, N_DEV):
            g = g + r_ref[d].astype(F32)
        if transpose:
            g = g.T
        delta, m_new, v_new = _adamw(w_ref[...], g, m_ref[...], v_ref[...])
        g_ref[...] = g
        d_ref[...] = delta
        mo_ref[...] = m_new
        vo_ref[...] = v_new

    if transpose:
        blk = pl.BlockSpec((dm, tr), lambda i: (0, i))
    else:
        blk = pl.BlockSpec((tr, dm), lambda i: (i, 0))
    first = off // tr
    return pl.pallas_call(
        body, name=name, grid=(rows // tr,),
        in_specs=[pl.BlockSpec((N_DEV, tr, dm), lambda i: (0, first + i, 0)), blk, blk, blk],
        out_specs=[blk] * 4,
        out_shape=[jax.ShapeDtypeStruct(w.shape, F32)] * 4,
        compiler_params=_params(32),
    )(recv, w, m, v)


def _sum_slots(recv_small, recv_last):
    def body(r_ref, l_ref, o_ref):
        g = r_ref[0]
        last = l_ref[0]
        for d in range(1, N_DEV):
            g = g + r_ref[d]
            last = last + l_ref[d]
        o_ref[...] = g
        o_ref[0:1, :] = last[0:1, :]

    return pl.pallas_call(
        body, name="sum_small", in_specs=[_whole()] * 2, out_specs=_whole(),
        out_shape=jax.ShapeDtypeStruct(recv_small.shape[1:], F32),
    )(recv_small, recv_last)


def _adam_small(g, w, m, v, name):
    def body(g_ref, w_ref, m_ref, v_ref, d_ref, mo_ref, vo_ref):
        delta, m_new, v_new = _adamw(w_ref[...], g_ref[...], m_ref[...], v_ref[...])
        d_ref[...] = delta
        mo_ref[...] = m_new
        vo_ref[...] = v_new

    return pl.pallas_call(
        body, name=name, in_specs=[_whole()] * 4, out_specs=[_whole()] * 3,
        out_shape=[jax.ShapeDtypeStruct(w.shape, F32)] * 3,
    )(g, w, m, v)


def _tile(n_tok, want):
    return min(want, n_tok)


def kernel(x, norm1_pre_g, w_in, b_in, conv_a_w, conv_a_b, w_a_out, conv_b_w, conv_b_b, ln_b_g, ln_b_b, w_b_out, w_o, norm1_post_g, norm2_pre_g, w_mlp_in, w_mlp_out, norm2_post_g, loss_target, m_norm1_pre_g, m_w_in, m_b_in, m_conv_a_w, m_conv_a_b, m_w_a_out, m_conv_b_w, m_conv_b_b, m_ln_b_g, m_ln_b_b, m_w_b_out, m_w_o, m_norm1_post_g, m_norm2_pre_g, m_w_mlp_in, m_w_mlp_out, m_norm2_post_g, v_norm1_pre_g, v_w_in, v_b_in, v_conv_a_w, v_conv_a_b, v_w_a_out, v_conv_b_w, v_conv_b_b, v_ln_b_g, v_ln_b_b, v_w_b_out, v_w_o, v_norm1_post_g, v_norm2_pre_g, v_w_mlp_in, v_w_mlp_out, v_norm2_post_g):
    n_tok, dm = x.shape[1], x.shape[2]
    rows, off, slab_rows = _layout(dm)
    cw = dm // N_DEV
    xs = x.reshape(n_tok, dm)
    tgt = loss_target.reshape(n_tok, dm)
    row = lambda vec: vec.reshape(1, -1)
    landing = lambda group, dt: jax.ShapeDtypeStruct((N_DEV, slab_rows[group], dm), dt)
    tm, tk = min(dm, 1024), _tile(n_tok, 512)

    conv_own = jnp.concatenate([conv_a_w, jnp.zeros((SUBLANE - CONV_A, cw), F32), conv_b_w,
                                jnp.zeros((1, cw), F32)], axis=0)
    slab_in, conv_all = _all_gather_two_level([w_in.T.astype(BF16), conv_own], "gather_w_in")
    conv_full = conv_all.transpose(1, 0, 2).reshape(conv_own.shape[0], dm)
    caw, cbw = conv_full[0:CONV_A], conv_full[SUBLANE:SUBLANE + CONV_B]
    own_abo = jnp.concatenate([w_a_out, w_b_out, w_o], axis=0).astype(BF16)
    own_mlp = jnp.concatenate([w_mlp_in.T, w_mlp_out], axis=0).astype(BF16)
    ag_abo = _exchange_start([_Part(own_abo, False, slab_rows["abo"], 0, 0)], [landing("abo", BF16)],
                             "gather_abo_start", after=slab_in)
    ag_mlp = _exchange_start([_Part(own_mlp, False, slab_rows["mlp"], 0, 0)], [landing("mlp", BF16)],
                             "gather_mlp_start", after=ag_abo.token)

    proj, p, u, h = _fwd_in(xs, row(norm1_pre_g), row(b_in), slab_in, _tile(n_tok, 256),
                            [ag_abo.token, ag_mlp.token])
    slab_abo, = _exchange_wait(ag_abo, "gather_abo_wait", after=proj)
    vb, ya, yb, qa, sb, merged, mix, x1 = _fwd_mix(
        p, u, proj, xs, caw, row(conv_a_b), cbw, row(conv_b_b), row(ln_b_g), row(ln_b_b), row(norm1_post_g),
        slab_abo, _tile(n_tok, 256))
    slab_mlp, = _exchange_wait(ag_mlp, "gather_mlp_wait", after=x1)
    f, df1, h2, df2, dmix, dx1, small_mlp = _mlp_fwd_bwd(
        x1, mix, tgt, row(norm1_post_g), row(norm2_pre_g), row(norm2_post_g), slab_mlp, _tile(n_tok, 256))

    rs_mlp = _exchange_start(
        [_Part(_wgrad(df1, h2, "wgrad_mlp_in", tm, tk, BF16), True, rows["w1"], 0, off["w1"]),
         _Part(_wgrad(f, df2, "wgrad_mlp_out", tm, tk, BF16), True, rows["w2"], 0, off["w2"])],
        [landing("mlp", BF16)], "scatter_mlp_start")
    dpa, dya, dyb, dva, dvb, small_mix = _bwd_mix(
        dmix, ya, yb, proj, p, vb, caw, row(conv_a_b), row(ln_b_g), row(ln_b_b), slab_abo, _tile(n_tok, 256),
        [rs_mlp.token])
    rs_abo = _exchange_start(
        [_Part(_wgrad(qa, dya, "wgrad_a_out", tm, tk, BF16), True, rows["wa"], 0, off["wa"]),
         _Part(_wgrad(sb, dyb, "wgrad_b_out", tm, tk, BF16), True, rows["wb"], 0, off["wb"]),
         _Part(_wgrad(merged, dmix, "wgrad_o", tm, tk, BF16), True, rows["wo"], 0, off["wo"])],
        [landing("abo", BF16)], "scatter_abo_start")
    dproj, small_conv = _bwd_conv(dva, dvb, p, u, proj, dpa, caw, cbw, _tile(n_tok, 256), [rs_abo.token])

    zeros = lambda r: jnp.zeros((r, dm), F32)
    small = jnp.concatenate([
        zeros(1),
        small_mix[2:3],
        small_mix[3:4],
        small_mix[0:2],
        small_mlp[2:3],
        small_mlp[1:2],
        small_mlp[0:1],
        small_mix[4:5], small_conv[3:7], small_mix[5:7],
        zeros(1),
        small_conv[0:CONV_A], zeros(SUBLANE - CONV_A),
        small_conv[8:8 + CONV_B], zeros(1),
    ], axis=0)

    rs_in = _exchange_start(
        [_Part(_wgrad(dproj, h, "wgrad_in", tm, tk, BF16), True, rows["win"], 0, off["win"]),
         _Part(small, False, small.shape[0], 1, 0)],
        [landing("in", BF16), jax.ShapeDtypeStruct((N_DEV,) + small.shape, F32)], "scatter_in_start")
    grad_x, small_in = _bwd_in(dproj, xs, dx1, row(norm1_pre_g), slab_in, _tile(n_tok, 256), [rs_in.token])
    recv_mlp, = _exchange_wait(rs_mlp, "scatter_mlp_wait", after=grad_x)
    recv_abo, = _exchange_wait(rs_abo, "scatter_abo_wait", after=grad_x)
    recv_in, recv_small = _exchange_wait(rs_in, "scatter_in_wait", after=grad_x)
    recv_last, = _all_gather([small_in], "gather_last")

    tr = min(LANE, rows["wa"])
    big = {
        "w_in": _adam_big(recv_in, off["win"], rows["win"], w_in, m_w_in, v_w_in, True, "adam_w_in", tr),
        "w_a_out": _adam_big(recv_abo, off["wa"], rows["wa"], w_a_out, m_w_a_out, v_w_a_out, False,
                             "adam_w_a_out", tr),
        "w_mlp_in": _adam_big(recv_mlp, off["w1"], rows["w1"], w_mlp_in, m_w_mlp_in, v_w_mlp_in, True,
                              "adam_w_mlp_in", tr),
        "w_mlp_out": _adam_big(recv_mlp, off["w2"], rows["w2"], w_mlp_out, m_w_mlp_out, v_w_mlp_out, False,
                               "adam_w_mlp_out", tr),
        "w_b_out": _adam_big(recv_abo, off["wb"], rows["wb"], w_b_out, m_w_b_out, v_w_b_out, False,
                             "adam_w_b_out", tr),
        "w_o": _adam_big(recv_abo, off["wo"], rows["wo"], w_o, m_w_o, v_w_o, False, "adam_w_o", tr),
    }

    gsum = _sum_slots(recv_small, recv_last)
    rep = lambda vec8, bias: jnp.concatenate([jnp.stack(vec8, axis=0), bias.reshape(7, dm), zeros(1)], axis=0)
    w_rep = rep((norm1_pre_g, conv_a_b, conv_b_b, ln_b_g, ln_b_b, norm1_post_g, norm2_pre_g, norm2_post_g), b_in)
    m_rep = rep((m_norm1_pre_g, m_conv_a_b, m_conv_b_b, m_ln_b_g, m_ln_b_b, m_norm1_post_g, m_norm2_pre_g,
                 m_norm2_post_g), m_b_in)
    v_rep = rep((v_norm1_pre_g, v_conv_a_b, v_conv_b_b, v_ln_b_g, v_ln_b_b, v_norm1_post_g, v_norm2_pre_g,
                 v_norm2_post_g), v_b_in)
    g_rep = gsum[0:16]
    d_rep, mo_rep, vo_rep = _adam_small(g_rep, w_rep, m_rep, v_rep, "adam_replicated")

    me = 4 * lax.axis_index("x") + 2 * lax.axis_index("y") + lax.axis_index("c")
    g_conv = lax.dynamic_slice_in_dim(gsum[16:56], me * cw, cw, axis=1)
    pad_conv = lambda a3, b31: jnp.concatenate(
        [a3, jnp.zeros((SUBLANE - CONV_A, cw), F32), b31, jnp.zeros((1, cw), F32)], axis=0)
    d_conv, mo_conv, vo_conv = _adam_small(
        g_conv, pad_conv(conv_a_w, conv_b_w), pad_conv(m_conv_a_w, m_conv_b_w),
        pad_conv(v_conv_a_w, v_conv_b_w), "adam_conv")

    def small_leaf(stack_rep, stack_conv, name):
        idx = {"norm1_pre_g": 0, "conv_a_b": 1, "conv_b_b": 2, "ln_b_g": 3, "ln_b_b": 4, "norm1_post_g": 5,
               "norm2_pre_g": 6, "norm2_post_g": 7}
        if name in idx:
            return stack_rep[idx[name]]
        if name == "b_in":
            return stack_rep[8:15].reshape(7 * dm)
        if name == "conv_a_w":
            return stack_conv[0:CONV_A]
        return stack_conv[SUBLANE:SUBLANE + CONV_B]

    order = ("norm1_pre_g", "w_in", "b_in", "conv_a_w", "conv_a_b", "w_a_out", "conv_b_w", "conv_b_b", "ln_b_g",
             "ln_b_b", "w_b_out", "w_o", "norm1_post_g", "norm2_pre_g", "w_mlp_in", "w_mlp_out", "norm2_post_g")
    grads, deltas, new_m, new_v = [], [], [], []
    for name in order:
        if name in big:
            g, d, mo, vo = big[name]
        else:
            g = small_leaf(g_rep, g_conv, name)
            d = small_leaf(d_rep, d_conv, name)
            mo = small_leaf(mo_rep, mo_conv, name)
            vo = small_leaf(vo_rep, vo_conv, name)
        grads.append(g)
        deltas.append(d)
        new_m.append(mo)
        new_v.append(vo)

    loss_local = (0.5 / dm) * jnp.sum(small_mlp[3])
    loss = lax.psum(loss_local, ("x", "y", "c"))
    return (loss, grad_x.reshape(x.shape), *grads, *deltas, *new_m, *new_v)
```

```python
from typing import NamedTuple

import jax
import jax.numpy as jnp
from jax import lax
from jax.experimental import pallas as pl
from jax.experimental.pallas import tpu as pltpu

F32 = jnp.float32
BF16 = jnp.bfloat16

RMS_EPS = 1e-6
LN_EPS = 1e-5
ADAM_LR = 0.001
ADAM_B1 = 0.9
ADAM_B2 = 0.999
ADAM_EPS = 1e-08
ADAM_WD = 0.01
ADAM_STEP = 10

N_DEV = 8
CONV_A = 3
CONV_B = 31
LANE = 128
SUBLANE = 8
HALO = 16
CONV_ROWS = 64
MIB = 1 << 20
FLIPS = ((0, 0, 1), (0, 1, 0), (1, 0, 0), (0, 1, 1), (1, 0, 1), (1, 1, 0), (1, 1, 1))
MESH = pl.DeviceIdType.MESH


def _layout(d_model):
    e = d_model // N_DEV
    rows = {"win": 7 * e, "w1": 4 * e, "w2": 4 * e, "wa": e, "wb": e, "wo": e}
    off = {"win": 0, "w1": 0, "w2": 4 * e, "wa": 0, "wb": e, "wo": 2 * e}
    return rows, off, {"in": 7 * e, "mlp": 8 * e, "abo": 3 * e}


def _after(body, deps):
    def ordered(*refs):
        return body(*refs[len(deps):])
    return ordered


def _params(vmem_mib):
    return pltpu.CompilerParams(dimension_semantics=("arbitrary",), vmem_limit_bytes=vmem_mib * MIB)


def _whole():
    return pl.BlockSpec(memory_space=pltpu.VMEM)


def _hbm():
    return pl.BlockSpec(memory_space=pl.ANY)


def _rows(ts, width, col=0):
    return pl.BlockSpec((ts, width), lambda i: (i, col))


def _halo_specs(ts, width, n_rows):
    per = ts // HALO
    last = n_rows // HALO - 1
    return [
        pl.BlockSpec((ts, width), lambda i: (i, 0)),
        pl.BlockSpec((HALO, width), lambda i: (jnp.maximum(i * per - 1, 0), 0)),
        pl.BlockSpec((HALO, width), lambda i: (jnp.minimum((i + 1) * per, last), 0)),
    ]


def _dot(a, b):
    return jnp.dot(a, b, preferred_element_type=F32)


def _dot_nt(a, b):
    return lax.dot_general(a, b, (((1,), (1,)), ((), ())), preferred_element_type=F32)


def _dot_tn(a, b):
    return lax.dot_general(a, b, (((0,), (0,)), ((), ())), preferred_element_type=F32)


def _rms(u):
    return lax.rsqrt(jnp.mean(u * u, axis=-1, keepdims=True) + RMS_EPS)


def _rms_bwd(dz, u, r, g):
    dzg = dz * g
    return r * dzg - u * (r * r * r) * jnp.mean(dzg * u, axis=-1, keepdims=True)


def _colsum(v):
    return jnp.sum(v, axis=0, keepdims=True)


def _sigmoid(v):
    return jax.nn.sigmoid(v)


def _load_weight(slab_ref, off, rows, dst_ref, sems, first_sem):
    copies = []
    for d in range(N_DEV):
        cp = pltpu.make_async_copy(slab_ref.at[d, pl.ds(off, rows), :], dst_ref.at[pl.ds(d * rows, rows), :],
                                   sems.at[first_sem + d])
        cp.start()
        copies.append(cp)
    return copies


def _fill_ext(ext_ref, main_ref, prev_ref, next_ref, i, n_steps, ts):
    ext_ref[0:HALO, :] = jnp.where(i > 0, prev_ref[...], 0.0)
    ext_ref[HALO:HALO + ts, :] = main_ref[...]
    ext_ref[HALO + ts:HALO + ts + HALO, :] = jnp.where(i < n_steps - 1, next_ref[...], 0.0)


def _broadcast_taps(w_ref, wb_ref, n_taps):
    for k in range(n_taps):
        wb_ref[k] = jnp.broadcast_to(w_ref[k:k + 1, :], wb_ref.shape[1:])


def _phases(starts):
    groups = {}
    for k, st in enumerate(starts):
        groups.setdefault(st % SUBLANE, []).append((k, st // SUBLANE))
    return sorted(groups.items())


def _shifted(blk, b):
    n = blk.shape[0]
    rolled = blk if b == 0 else pltpu.roll(blk, n - b, axis=0)
    return rolled.reshape(n // SUBLANE, SUBLANE, blk.shape[1])


def _conv_tile(ext_ref, wb_ref, starts, ts, width, emit):
    span = CONV_ROWS + 2 * HALO
    groups = _phases(starts)
    nv = CONV_ROWS // SUBLANE

    def row_block(rb, carry):
        r0 = pl.multiple_of(rb * CONV_ROWS, CONV_ROWS)
        for cb in range(width // LANE):
            lanes = pl.ds(cb * LANE, LANE)
            blk = ext_ref[pl.ds(r0, span), lanes]
            acc = jnp.zeros((nv, SUBLANE, LANE), F32)
            for b, taps in groups:
                sh = _shifted(blk, b)
                for k, m in taps:
                    acc = acc + sh[m:m + nv] * wb_ref[k, :, lanes][None]
            emit(r0, lanes, acc.reshape(CONV_ROWS, LANE))
        return carry

    lax.fori_loop(0, ts // CONV_ROWS, row_block, 0)


def _conv_wgrad_tile(ext_ref, dv_ref, acc_ref, starts, ts, width):
    span = CONV_ROWS + 2 * HALO
    groups = _phases(starts)
    nv = CONV_ROWS // SUBLANE

    def row_block(rb, carry):
        r0 = pl.multiple_of(rb * CONV_ROWS, CONV_ROWS)
        for cb in range(width // LANE):
            lanes = pl.ds(cb * LANE, LANE)
            blk = ext_ref[pl.ds(r0, span), lanes]
            dv = dv_ref[pl.ds(r0, CONV_ROWS), lanes].reshape(nv, SUBLANE, LANE)
            for b, taps in groups:
                sh = _shifted(blk, b)
                for k, m in taps:
                    acc_ref[k, :, lanes] += jnp.sum(sh[m:m + nv] * dv, axis=0)
        return carry

    lax.fori_loop(0, ts // CONV_ROWS, row_block, 0)


def _fwd_starts(n_taps):
    pad = (n_taps - 1) // 2
    return [HALO - pad + k for k in range(n_taps)]


def _bwd_starts(n_taps):
    pad = (n_taps - 1) // 2
    return [HALO + pad - k for k in range(n_taps)]


def _peer(x, y, c, flip):
    fx, fy, fc = flip
    return (1 - x if fx else x, 1 - y if fy else y, 1 - c if fc else c)


def _all_gather(shards, name):
    n = len(shards)

    def body(*refs):
        ins, outs = refs[:n], refs[n:2 * n]
        send_sems, recv_sems, local_sems = refs[2 * n:]
        x, y, c = lax.axis_index("x"), lax.axis_index("y"), lax.axis_index("c")
        me = 4 * x + 2 * y + c
        local = [pltpu.make_async_copy(ins[j], outs[j].at[me], local_sems.at[j]) for j in range(n)]
        for cp in local:
            cp.start()
        sends, recvs = [], []
        for k, flip in enumerate(FLIPS):
            px, py, pc = _peer(x, y, c, flip)
            peer = 4 * px + 2 * py + pc
            for j in range(n):
                sem = k * n + j
                sends.append(pltpu.make_async_remote_copy(
                    src_ref=ins[j], dst_ref=outs[j].at[me], send_sem=send_sems.at[sem], recv_sem=recv_sems.at[sem],
                    device_id=(px, py, pc), device_id_type=MESH))
                recvs.append(pltpu.make_async_remote_copy(
                    src_ref=ins[j], dst_ref=outs[j].at[peer], send_sem=send_sems.at[sem], recv_sem=recv_sems.at[sem],
                    device_id=(px, py, pc), device_id_type=MESH))
        for cp in sends:
            cp.start()
        for cp in recvs:
            cp.wait_recv()
        for cp in sends:
            cp.wait_send()
        for cp in local:
            cp.wait()

    return pl.pallas_call(
        body, name=name,
        out_shape=[jax.ShapeDtypeStruct((N_DEV,) + s.shape, s.dtype) for s in shards],
        in_specs=[_hbm()] * n, out_specs=[_hbm()] * n,
        scratch_shapes=[pltpu.SemaphoreType.DMA((7 * n,)), pltpu.SemaphoreType.DMA((7 * n,)),
                        pltpu.SemaphoreType.DMA((n,))],
    )(*shards)


def _place_own(src, n_slots_shape, me, name):
    rows, width = src.shape
    tr = next(t for t in (256, 128, 64, 32, 16, SUBLANE) if rows % t == 0)

    def body(me_ref, src_ref, out_ref):
        out_ref[...] = src_ref[...]

    return pl.pallas_call(
        body, name=name,
        grid_spec=pltpu.PrefetchScalarGridSpec(
            num_scalar_prefetch=1, grid=(rows // tr,),
            in_specs=[pl.BlockSpec((tr, width), lambda i, me_ref: (i, 0))],
            out_specs=pl.BlockSpec((None, tr, width), lambda i, me_ref: (me_ref[0], i, 0))),
        out_shape=jax.ShapeDtypeStruct(n_slots_shape, src.dtype),
    )(me, src)


def _all_gather_two_level(shards, placed, name):
    n = len(shards)
    given = [j for j in range(n) if placed[j] is not None]

    def body(*refs):
        ins, outs = refs[:n], refs[n + len(given):2 * n + len(given)]
        send_sems, recv_sems, local_sems = refs[2 * n + len(given):]
        x, y, c = lax.axis_index("x"), lax.axis_index("y"), lax.axis_index("c")
        me, sibling = (x, y, c), (x, y, 1 - c)
        chips = [(1 - x, y), (x, 1 - y), (1 - x, 1 - y)]

        def slot(j, dev):
            return outs[j].at[4 * dev[0] + 2 * dev[1] + dev[2]]

        def copy(k, j, block, to, src=None):
            return pltpu.make_async_remote_copy(
                src_ref=slot(j, block) if src is None else src, dst_ref=slot(j, block),
                send_sem=send_sems.at[k * n + j], recv_sem=recv_sems.at[k * n + j], device_id=to, device_id_type=MESH)

        local = [pltpu.make_async_copy(ins[j], slot(j, me), local_sems.at[j]) for j in range(n) if j not in given]
        for cp in local:
            cp.start()
        first = [copy(0, j, me, sibling, src=ins[j]) for j in range(n)]
        first += [copy(1 + t, j, me, (*chip, c), src=ins[j]) for t, chip in enumerate(chips) for j in range(n)]
        for cp in first:
            cp.start()
        passed = []
        for t, chip in enumerate(chips):
            for j in range(n):
                copy(1 + t, j, (*chip, c), me).wait_recv()
                passed.append(copy(4 + t, j, (*chip, c), sibling))
                passed[-1].start()
        for j in range(n):
            copy(0, j, sibling, me).wait_recv()
        for t, chip in enumerate(chips):
            for j in range(n):
                copy(4 + t, j, (*chip, 1 - c), me).wait_recv()
        for cp in first + passed:
            cp.wait_send()
        for cp in local:
            cp.wait()

    return pl.pallas_call(
        body, name=name,
        out_shape=[jax.ShapeDtypeStruct((N_DEV,) + s.shape, s.dtype) for s in shards],
        in_specs=[_hbm()] * (n + len(given)), out_specs=[_hbm()] * n,
        input_output_aliases={n + i: j for i, j in enumerate(given)},
        scratch_shapes=[pltpu.SemaphoreType.DMA((7 * n,)), pltpu.SemaphoreType.DMA((7 * n,)),
                        pltpu.SemaphoreType.DMA((n,))],
    )(*shards, *[placed[j] for j in given])


class _Part(NamedTuple):
    src: jax.Array
    scatter: bool
    rows: int
    land: int
    off: int


class _Started(NamedTuple):
    send_sems: jax.Array
    recv_sems: jax.Array
    thru: tuple
    token: jax.Array
    parts: tuple


def _exchange_copies(srcs, lands, send_sems, recv_sems, parts):
    n = len(parts)
    x, y, c = lax.axis_index("x"), lax.axis_index("y"), lax.axis_index("c")
    me = 4 * x + 2 * y + c

    def block(j, dev):
        p = parts[j]
        return srcs[j].at[pl.ds(pl.multiple_of(dev * p.rows, SUBLANE), p.rows), :] if p.scatter else srcs[j]

    def slot(j, index):
        p = parts[j]
        return lands[p.land].at[index, pl.ds(p.off, p.rows), :]

    sends, recvs = [], []
    for k, flip in enumerate(FLIPS):
        px, py, pc = _peer(x, y, c, flip)
        peer = 4 * px + 2 * py + pc
        for j in range(n):
            sems = dict(send_sem=send_sems.at[k * n + j], recv_sem=recv_sems.at[k * n + j],
                        device_id=(px, py, pc), device_id_type=MESH)
            to, got = (k, k) if parts[j].scatter else (me, peer)
            sends.append(pltpu.make_async_remote_copy(src_ref=block(j, peer), dst_ref=slot(j, to), **sems))
            recvs.append(pltpu.make_async_remote_copy(src_ref=block(j, peer), dst_ref=slot(j, got), **sems))
    return sends, recvs


def _exchange_start(parts, lands, name, after=None):
    n, nl = len(parts), len(lands)
    n_in = n + nl + (after is not None)

    def body(*refs):
        srcs, land_refs = refs[:n], refs[n:n + nl]
        send_sems, recv_sems = refs[n_in], refs[n_in + 1]
        token = refs[n_in + 2 + n + nl]
        sends, _ = _exchange_copies(srcs, land_refs, send_sems, recv_sems, parts)
        for cp in sends:
            cp.start()
        token[...] = jnp.zeros_like(token)

    hbm = pl.BlockSpec(memory_space=pltpu.HBM)
    sem = pl.BlockSpec(memory_space=pltpu.SEMAPHORE)
    fresh = lambda s: lax.empty(s.shape, s.dtype) if isinstance(s, jax.ShapeDtypeStruct) else s
    args = [pltpu.with_memory_space_constraint(p.src, pltpu.HBM) for p in parts]
    args += [pltpu.with_memory_space_constraint(fresh(s), pltpu.HBM) for s in lands]
    args += [] if after is None else [after]
    out = pl.pallas_call(
        body, name=name,
        out_shape=(pltpu.SemaphoreType.DMA((7 * n,)), pltpu.SemaphoreType.DMA((7 * n,)),
                   *[pltpu.HBM(a.shape, a.dtype) for a in args[:n + nl]], jax.ShapeDtypeStruct((SUBLANE, LANE), F32)),
        in_specs=[hbm] * (n + nl) + [_hbm()] * (after is not None),
        out_specs=(sem, sem, *[hbm] * (n + nl), _whole()),
        input_output_aliases={j: 2 + j for j in range(n + nl)},
        compiler_params=pltpu.CompilerParams(has_side_effects=pltpu.SideEffectType.DATAFLOW_SIDE_EFFECTING),
    )(*args)
    return _Started(out[0], out[1], tuple(out[2:2 + n + nl]), out[2 + n + nl], tuple(parts))


def _exchange_wait(started, name, after):
    parts = started.parts
    n, nl = len(parts), len(started.thru) - len(parts)

    def body(*refs):
        srcs, land_refs = refs[:n], refs[n:n + nl]
        send_sems, recv_sems = refs[n + nl], refs[n + nl + 1]
        sends, recvs = _exchange_copies(srcs, land_refs, send_sems, recv_sems, parts)
        for cp in sends:
            cp.wait_send()
        for cp in recvs:
            cp.wait_recv()

    hbm = pl.BlockSpec(memory_space=pltpu.HBM)
    sem = pl.BlockSpec(memory_space=pltpu.SEMAPHORE)
    out = pl.pallas_call(
        body, name=name,
        out_shape=tuple(pltpu.HBM(a.shape, a.dtype) for a in started.thru),
        in_specs=[hbm] * (n + nl) + [sem, sem, _hbm()], out_specs=tuple([hbm] * (n + nl)),
        input_output_aliases={j: j for j in range(n + nl)},
        compiler_params=pltpu.CompilerParams(has_side_effects=pltpu.SideEffectType.DATAFLOW_SIDE_EFFECTING),
    )(*started.thru, started.send_sems, started.recv_sems, after)
    return list(out[:n]), list(out[n:])


def _fwd_in(x, g1, b_in, slab, ts, deps):
    n_tok, dm = x.shape
    rows, off, _ = _layout(dm)
    width = 7 * dm

    def body(x_ref, g1_ref, b_ref, slab_ref, proj_ref, p_ref, u_ref, h_ref, w_v, sems):
        @pl.when(pl.program_id(0) == 0)
        def _():
            for cp in _load_weight(slab_ref, off["win"], rows["win"], w_v, sems, 0):
                cp.wait()

        xv = x_ref[...]
        h = (xv * _rms(xv) * g1_ref[...]).astype(BF16)
        h_ref[...] = h
        cols = []
        for j in range(7):
            pj = _dot_nt(h, w_v[pl.ds(j * dm, dm), :]) + b_ref[:, j * dm:(j + 1) * dm]
            proj_ref[:, j * dm:(j + 1) * dm] = pj.astype(proj_ref.dtype)
            if 1 <= j <= 4:
                cols.append(pj)
            if j == 2:
                p_ref[...] = cols[0] * cols[1]
            if j == 4:
                u_ref[...] = cols[2] * _sigmoid(cols[3])

    return pl.pallas_call(
        _after(body, deps), name="fwd_in", grid=(n_tok // ts,),
        in_specs=[_whole()] * len(deps) + [_rows(ts, dm), _whole(), _whole(), _hbm()],
        out_specs=[_rows(ts, width), _rows(ts, dm), _rows(ts, dm), _rows(ts, dm)],
        out_shape=[jax.ShapeDtypeStruct((n_tok, width), BF16), jax.ShapeDtypeStruct((n_tok, dm), F32),
                   jax.ShapeDtypeStruct((n_tok, dm), F32), jax.ShapeDtypeStruct((n_tok, dm), BF16)],
        scratch_shapes=[pltpu.VMEM((width, dm), BF16), pltpu.SemaphoreType.DMA((N_DEV,))],
        compiler_params=_params(40),
    )(*deps, x, g1, b_in, slab)


def _fwd_mix(p, u, proj, x, caw, cab, cbw, cbb, lng, lnb, g1post, slab, ts):
    n_tok, dm = x.shape
    rows, off, _ = _layout(dm)
    n_steps = n_tok // ts

    def body(p_ref, p_prev, p_next, u_ref, u_prev, u_next, bg_ref, za_ref, zb_ref, x_ref,
             caw_ref, cab_ref, cbw_ref, cbb_ref, lng_ref, lnb_ref, g1p_ref, slab_ref,
             vb_ref, ya_ref, yb_ref, qa_ref, sb_ref, mg_ref, mix_ref, x1_ref,
             wa_v, wb_v, wo_v, ext_p, ext_u, va_v, tap_a, tap_b, sems):
        i = pl.program_id(0)

        @pl.when(i == 0)
        def _():
            cps = _load_weight(slab_ref, off["wa"], rows["wa"], wa_v, sems, 0)
            cps += _load_weight(slab_ref, off["wb"], rows["wb"], wb_v, sems, N_DEV)
            cps += _load_weight(slab_ref, off["wo"], rows["wo"], wo_v, sems, 2 * N_DEV)
            _broadcast_taps(caw_ref, tap_a, CONV_A)
            _broadcast_taps(cbw_ref, tap_b, CONV_B)
            for cp in cps:
                cp.wait()

        _fill_ext(ext_p, p_ref, p_prev, p_next, i, n_steps, ts)
        _fill_ext(ext_u, u_ref, u_prev, u_next, i, n_steps, ts)

        def emit_a(r0, lanes, acc):
            va_v[pl.ds(r0, CONV_ROWS), lanes] = acc + cab_ref[:, lanes]

        def emit_b(r0, lanes, acc):
            vb_ref[pl.ds(r0, CONV_ROWS), lanes] = acc + cbb_ref[:, lanes]

        _conv_tile(ext_p, tap_a, _fwd_starts(CONV_A), ts, dm, emit_a)
        _conv_tile(ext_u, tap_b, _fwd_starts(CONV_B), ts, dm, emit_b)

        qa = (bg_ref[...].astype(F32) * va_v[...]).astype(BF16)
        qa_ref[...] = qa
        ya = _dot(qa, wa_v[...])
        vb = vb_ref[...]
        xc = vb - jnp.mean(vb, axis=-1, keepdims=True)
        rstd = lax.rsqrt(jnp.mean(xc * xc, axis=-1, keepdims=True) + LN_EPS)
        ln = xc * rstd * lng_ref[...] + lnb_ref[...]
        sb = (ln * _sigmoid(ln)).astype(BF16)
        sb_ref[...] = sb
        yb = _dot(sb, wb_v[...])
        ya_ref[...] = ya.astype(BF16)
        yb_ref[...] = yb.astype(BF16)
        merged = (_sigmoid(za_ref[...].astype(F32)) * ya + _sigmoid(zb_ref[...].astype(F32)) * yb).astype(BF16)
        mg_ref[...] = merged
        mix = _dot(merged, wo_v[...])
        mix_ref[...] = mix
        x1_ref[...] = x_ref[...] + mix * _rms(mix) * g1p_ref[...]

    tok = lambda dt: jax.ShapeDtypeStruct((n_tok, dm), dt)
    return pl.pallas_call(
        body, name="fwd_mix", grid=(n_steps,),
        in_specs=(_halo_specs(ts, dm, n_tok) + _halo_specs(ts, dm, n_tok)
                  + [_rows(ts, dm, 0), _rows(ts, dm, 5), _rows(ts, dm, 6), _rows(ts, dm)]
                  + [_whole()] * 7 + [_hbm()]),
        out_specs=[_rows(ts, dm)] * 8,
        out_shape=[tok(F32), tok(BF16), tok(BF16), tok(BF16), tok(BF16), tok(BF16), tok(F32), tok(F32)],
        scratch_shapes=[pltpu.VMEM((dm, dm), BF16), pltpu.VMEM((dm, dm), BF16), pltpu.VMEM((dm, dm), BF16),
                        pltpu.VMEM((ts + 2 * HALO, dm), F32), pltpu.VMEM((ts + 2 * HALO, dm), F32),
                        pltpu.VMEM((ts, dm), F32),
                        pltpu.VMEM((CONV_A, SUBLANE, dm), F32), pltpu.VMEM((CONV_B, SUBLANE, dm), F32),
                        pltpu.SemaphoreType.DMA((3 * N_DEV,))],
        compiler_params=_params(48),
    )(p, p, p, u, u, u, proj, proj, proj, x, caw, cab, cbw, cbb, lng, lnb, g1post, slab)


def _mlp_fwd_bwd(x1, mix, tgt, g1post, g2pre, g2post, slab, ts):
    n_tok, dm = x1.shape
    rows, off, _ = _layout(dm)
    ff = 4 * dm

    def body(x1_ref, mix_ref, t_ref, g1p_ref, g2pre_ref, g2post_ref, slab_ref,
             f_ref, df1_ref, h2_ref, df2_ref, dmix_ref, dx1_ref, small_ref, w1_v, w2_v, relu_v, sems):
        @pl.when(pl.program_id(0) == 0)
        def _():
            cps = _load_weight(slab_ref, off["w1"], rows["w1"], w1_v, sems, 0)
            cps += _load_weight(slab_ref, off["w2"], rows["w2"], w2_v, sems, N_DEV)
            small_ref[...] = jnp.zeros_like(small_ref)
            for cp in cps:
                cp.wait()

        x1v = x1_ref[...]
        r3 = _rms(x1v)
        g2pre = g2pre_ref[...]
        h2 = (x1v * r3 * g2pre).astype(BF16)
        h2_ref[...] = h2
        f2 = jnp.zeros((ts, dm), F32)
        for c in range(4):
            blk = pl.ds(c * dm, dm)
            relu = jnp.maximum(_dot_nt(h2, w1_v[blk, :]), 0.0)
            relu_v[:, c * dm:(c + 1) * dm] = relu
            fc = (relu * relu).astype(BF16)
            f_ref[:, c * dm:(c + 1) * dm] = fc
            f2 = f2 + _dot(fc, w2_v[blk, :])
        r4 = _rms(f2)
        g2post = g2post_ref[...]
        err = x1v + f2 * r4 * g2post - t_ref[...]
        dy = err * (1.0 / dm)
        small_ref[3:4, :] += _colsum(err * err)
        small_ref[0:1, :] += _colsum(dy * f2 * r4)
        df2 = _rms_bwd(dy, f2, r4, g2post).astype(BF16)
        df2_ref[...] = df2
        dh2 = jnp.zeros((ts, dm), F32)
        for c in range(4):
            blk = pl.ds(c * dm, dm)
            df1 = (_dot_nt(df2, w2_v[blk, :]) * (2.0 * relu_v[:, c * dm:(c + 1) * dm])).astype(BF16)
            df1_ref[:, c * dm:(c + 1) * dm] = df1
            dh2 = dh2 + _dot(df1, w1_v[blk, :])
        small_ref[1:2, :] += _colsum(dh2 * x1v * r3)
        dx1 = dy + _rms_bwd(dh2, x1v, r3, g2pre)
        dx1_ref[...] = dx1
        mixv = mix_ref[...]
        r2 = _rms(mixv)
        small_ref[2:3, :] += _colsum(dx1 * mixv * r2)
        dmix_ref[...] = _rms_bwd(dx1, mixv, r2, g1p_ref[...]).astype(BF16)

    tok = lambda w, dt: jax.ShapeDtypeStruct((n_tok, w), dt)
    return pl.pallas_call(
        body, name="mlp_fwd_bwd", grid=(n_tok // ts,),
        in_specs=[_rows(ts, dm)] * 3 + [_whole()] * 3 + [_hbm()],
        out_specs=[_rows(ts, ff), _rows(ts, ff), _rows(ts, dm), _rows(ts, dm), _rows(ts, dm), _rows(ts, dm),
                   pl.BlockSpec((SUBLANE, dm), lambda i: (0, 0))],
        out_shape=[tok(ff, BF16), tok(ff, BF16), tok(dm, BF16), tok(dm, BF16), tok(dm, BF16), tok(dm, F32),
                   jax.ShapeDtypeStruct((SUBLANE, dm), F32)],
        scratch_shapes=[pltpu.VMEM((ff, dm), BF16), pltpu.VMEM((ff, dm), BF16), pltpu.VMEM((ts, ff), F32),
                        pltpu.SemaphoreType.DMA((2 * N_DEV,))],
        compiler_params=_params(56),
    )(x1, mix, tgt, g1post, g2pre, g2post, slab)


def _bwd_mix(dmix, ya, yb, proj, p, vb, caw, cab, lng, lnb, slab, ts, deps):
    n_tok, dm = dmix.shape
    rows, off, _ = _layout(dm)
    n_steps = n_tok // ts

    def body(dmix_ref, ya_ref, yb_ref, bg_ref, za_ref, zb_ref, p_ref, p_prev, p_next, vb_ref,
             caw_ref, cab_ref, lng_ref, lnb_ref, slab_ref,
             dpa_ref, dya_ref, dyb_ref, dva_ref, dvb_ref, small_ref,
             wa_v, wb_v, wo_v, ext_p, va_v, tap_a, sems):
        i = pl.program_id(0)

        @pl.when(i == 0)
        def _():
            cps = _load_weight(slab_ref, off["wa"], rows["wa"], wa_v, sems, 0)
            cps += _load_weight(slab_ref, off["wb"], rows["wb"], wb_v, sems, N_DEV)
            cps += _load_weight(slab_ref, off["wo"], rows["wo"], wo_v, sems, 2 * N_DEV)
            _broadcast_taps(caw_ref, tap_a, CONV_A)
            small_ref[...] = jnp.zeros_like(small_ref)
            for cp in cps:
                cp.wait()

        _fill_ext(ext_p, p_ref, p_prev, p_next, i, n_steps, ts)

        def emit_a(r0, lanes, acc):
            va_v[pl.ds(r0, CONV_ROWS), lanes] = acc + cab_ref[:, lanes]

        _conv_tile(ext_p, tap_a, _fwd_starts(CONV_A), ts, dm, emit_a)

        dmerged = _dot_nt(dmix_ref[...], wo_v[...])
        sa = _sigmoid(za_ref[...].astype(F32))
        sg = _sigmoid(zb_ref[...].astype(F32))
        dza = dmerged * ya_ref[...].astype(F32) * sa * (1.0 - sa)
        dzb = dmerged * yb_ref[...].astype(F32) * sg * (1.0 - sg)
        dpa_ref[:, dm:2 * dm] = dza.astype(BF16)
        dpa_ref[:, 2 * dm:3 * dm] = dzb.astype(BF16)
        small_ref[5:6, :] += _colsum(dza)
        small_ref[6:7, :] += _colsum(dzb)

        dya = (dmerged * sa).astype(BF16)
        dya_ref[...] = dya
        dqa = _dot_nt(dya, wa_v[...])
        dbg = dqa * va_v[...]
        dpa_ref[:, 0:dm] = dbg.astype(BF16)
        small_ref[4:5, :] += _colsum(dbg)
        dva = dqa * bg_ref[...].astype(F32)
        dva_ref[...] = dva
        small_ref[2:3, :] += _colsum(dva)

        dyb = (dmerged * sg).astype(BF16)
        dyb_ref[...] = dyb
        dsb = _dot_nt(dyb, wb_v[...])
        vb = vb_ref[...]
        xc = vb - jnp.mean(vb, axis=-1, keepdims=True)
        rstd = lax.rsqrt(jnp.mean(xc * xc, axis=-1, keepdims=True) + LN_EPS)
        nrm = xc * rstd
        lng_v = lng_ref[...]
        ln = nrm * lng_v + lnb_ref[...]
        sl = _sigmoid(ln)
        dln = dsb * (sl * (1.0 + ln * (1.0 - sl)))
        small_ref[0:1, :] += _colsum(dln * nrm)
        small_ref[1:2, :] += _colsum(dln)
        dn = dln * lng_v
        dvb = rstd * (dn - jnp.mean(dn, axis=-1, keepdims=True)
                      - nrm * jnp.mean(dn * nrm, axis=-1, keepdims=True))
        dvb_ref[...] = dvb
        small_ref[3:4, :] += _colsum(dvb)

    tok = lambda w, dt: jax.ShapeDtypeStruct((n_tok, w), dt)
    return pl.pallas_call(
        _after(body, deps), name="bwd_mix", grid=(n_steps,),
        in_specs=([_whole()] * len(deps) + [_rows(ts, dm)] * 3+ [_rows(ts, dm, 0), _rows(ts, dm, 5), _rows(ts, dm, 6)]
                  + _halo_specs(ts, dm, n_tok) + [_rows(ts, dm)] + [_whole()] * 4 + [_hbm()]),
        out_specs=[_rows(ts, 3 * dm), _rows(ts, dm), _rows(ts, dm), _rows(ts, dm), _rows(ts, dm),
                   pl.BlockSpec((SUBLANE, dm), lambda i: (0, 0))],
        out_shape=[tok(3 * dm, BF16), tok(dm, BF16), tok(dm, BF16), tok(dm, F32), tok(dm, F32),
                   jax.ShapeDtypeStruct((SUBLANE, dm), F32)],
        scratch_shapes=[pltpu.VMEM((dm, dm), BF16), pltpu.VMEM((dm, dm), BF16), pltpu.VMEM((dm, dm), BF16),
                        pltpu.VMEM((ts + 2 * HALO, dm), F32), pltpu.VMEM((ts, dm), F32),
                        pltpu.VMEM((CONV_A, SUBLANE, dm), F32), pltpu.SemaphoreType.DMA((3 * N_DEV,))],
        compiler_params=_params(48),
    )(*deps, dmix, ya, yb, proj, proj, proj, p, p, p, vb, caw, cab, lng, lnb, slab)


def _bwd_conv(dva, dvb, p, u, proj, dpa, caw, cbw, ts, deps):
    n_tok, dm = dva.shape
    n_steps = n_tok // ts
    small_rows = 40

    def body(dva_ref, dva_prev, dva_next, dvb_ref, dvb_prev, dvb_next, p_ref, p_prev, p_next,
             u_ref, u_prev, u_next, cg_ref, ha_ref, a_ref, g_ref, dpa_ref, caw_ref, cbw_ref,
             dproj_ref, small_ref,
             ext_dva, ext_dvb, ext_p, ext_u, dp_v, du_v, tap_a, tap_b, gwa_v, gwb_v):
        i = pl.program_id(0)

        @pl.when(i == 0)
        def _():
            _broadcast_taps(caw_ref, tap_a, CONV_A)
            _broadcast_taps(cbw_ref, tap_b, CONV_B)
            small_ref[...] = jnp.zeros_like(small_ref)
            gwa_v[...] = jnp.zeros_like(gwa_v)
            gwb_v[...] = jnp.zeros_like(gwb_v)

        _fill_ext(ext_dva, dva_ref, dva_prev, dva_next, i, n_steps, ts)
        _fill_ext(ext_dvb, dvb_ref, dvb_prev, dvb_next, i, n_steps, ts)
        _fill_ext(ext_p, p_ref, p_prev, p_next, i, n_steps, ts)
        _fill_ext(ext_u, u_ref, u_prev, u_next, i, n_steps, ts)

        def emit_dp(r0, lanes, acc):
            dp_v[pl.ds(r0, CONV_ROWS), lanes] = acc

        def emit_du(r0, lanes, acc):
            du_v[pl.ds(r0, CONV_ROWS), lanes] = acc

        _conv_tile(ext_dva, tap_a, _bwd_starts(CONV_A), ts, dm, emit_dp)
        _conv_tile(ext_dvb, tap_b, _bwd_starts(CONV_B), ts, dm, emit_du)
        _conv_wgrad_tile(ext_p, dva_ref, gwa_v, _fwd_starts(CONV_A), ts, dm)
        _conv_wgrad_tile(ext_u, dvb_ref, gwb_v, _fwd_starts(CONV_B), ts, dm)

        dp = dp_v[...]
        dcg = dp * ha_ref[...].astype(F32)
        dha = dp * cg_ref[...].astype(F32)
        du = du_v[...]
        sg = _sigmoid(g_ref[...].astype(F32))
        da = du * sg
        dg = du * a_ref[...].astype(F32) * sg * (1.0 - sg)
        dproj_ref[:, 0:dm] = dpa_ref[:, 0:dm]
        dproj_ref[:, dm:2 * dm] = dcg.astype(BF16)
        dproj_ref[:, 2 * dm:3 * dm] = dha.astype(BF16)
        dproj_ref[:, 3 * dm:4 * dm] = da.astype(BF16)
        dproj_ref[:, 4 * dm:5 * dm] = dg.astype(BF16)
        dproj_ref[:, 5 * dm:7 * dm] = dpa_ref[:, dm:3 * dm]
        small_ref[3:4, :] += _colsum(dcg)
        small_ref[4:5, :] += _colsum(dha)
        small_ref[5:6, :] += _colsum(da)
        small_ref[6:7, :] += _colsum(dg)

        @pl.when(i == n_steps - 1)
        def _():
            for k in range(CONV_A):
                small_ref[k:k + 1, :] = _colsum(gwa_v[k])
            for k in range(CONV_B):
                small_ref[SUBLANE + k:SUBLANE + k + 1, :] = _colsum(gwb_v[k])

    ext = pltpu.VMEM((ts + 2 * HALO, dm), F32)
    return pl.pallas_call(
        _after(body, deps), name="bwd_conv", grid=(n_steps,),
        in_specs=([_whole()] * len(deps) + _halo_specs(ts, dm, n_tok) * 4
                  + [_rows(ts, dm, 1), _rows(ts, dm, 2), _rows(ts, dm, 3), _rows(ts, dm, 4), _rows(ts, 3 * dm)]
                  + [_whole()] * 2),
        out_specs=[_rows(ts, 7 * dm), pl.BlockSpec((small_rows, dm), lambda i: (0, 0))],
        out_shape=[jax.ShapeDtypeStruct((n_tok, 7 * dm), BF16), jax.ShapeDtypeStruct((small_rows, dm), F32)],
        scratch_shapes=[ext, ext, ext, ext, pltpu.VMEM((ts, dm), F32), pltpu.VMEM((ts, dm), F32),
                        pltpu.VMEM((CONV_A, SUBLANE, dm), F32), pltpu.VMEM((CONV_B, SUBLANE, dm), F32),
                        pltpu.VMEM((CONV_A, SUBLANE, dm), F32), pltpu.VMEM((CONV_B, SUBLANE, dm), F32)],
        compiler_params=_params(48),
    )(*deps, dva, dva, dva, dvb, dvb, dvb, p, p, p, u, u, u, proj, proj, proj, proj, dpa, caw, cbw)


def _bwd_in(dproj, x, dx1, g1, slab, ts, deps):
    n_tok, dm = x.shape
    rows, off, _ = _layout(dm)
    width = 7 * dm

    def body(dproj_ref, x_ref, dx1_ref, g1_ref, slab_ref, gx_ref, small_ref, w_v, sems):
        @pl.when(pl.program_id(0) == 0)
        def _():
            cps = _load_weight(slab_ref, off["win"], rows["win"], w_v, sems, 0)
            small_ref[...] = jnp.zeros_like(small_ref)
            for cp in cps:
                cp.wait()

        dh = _dot(dproj_ref[...], w_v[...])
        xv = x_ref[...]
        r1 = _rms(xv)
        small_ref[0:1, :] += _colsum(dh * xv * r1)
        gx_ref[...] = dx1_ref[...] + _rms_bwd(dh, xv, r1, g1_ref[...])

    return pl.pallas_call(
        _after(body, deps), name="bwd_in", grid=(n_tok // ts,),
        in_specs=[_whole()] * len(deps) + [_rows(ts, width), _rows(ts, dm), _rows(ts, dm), _whole(), _hbm()],
        out_specs=[_rows(ts, dm), pl.BlockSpec((SUBLANE, dm), lambda i: (0, 0))],
        out_shape=[jax.ShapeDtypeStruct((n_tok, dm), F32), jax.ShapeDtypeStruct((SUBLANE, dm), F32)],
        scratch_shapes=[pltpu.VMEM((width, dm), BF16), pltpu.SemaphoreType.DMA((N_DEV,))],
        compiler_params=_params(40),
    )(*deps, dproj, x, dx1, g1, slab)


def _wgrad(a, b, name, tm, tk, out_dtype):
    n_tok, m = a.shape
    n = b.shape[1]
    k_steps = n_tok // tk

    def body(a_ref, b_ref, o_ref, acc_v):
        k = pl.program_id(1)

        @pl.when(k == 0)
        def _():
            acc_v[...] = jnp.zeros_like(acc_v)

        acc_v[...] += _dot_tn(a_ref[...], b_ref[...])

        @pl.when(k == k_steps - 1)
        def _():
            o_ref[...] = acc_v[...].astype(o_ref.dtype)

    return pl.pallas_call(
        body, name=name, grid=(m // tm, k_steps),
        in_specs=[pl.BlockSpec((tk, tm), lambda i, k: (k, i)), pl.BlockSpec((tk, n), lambda i, k: (k, 0))],
        out_specs=pl.BlockSpec((tm, n), lambda i, k: (i, 0)),
        out_shape=jax.ShapeDtypeStruct((m, n), out_dtype),
        scratch_shapes=[pltpu.VMEM((tm, n), F32)],
        compiler_params=pltpu.CompilerParams(dimension_semantics=("arbitrary", "arbitrary"),
                                             vmem_limit_bytes=40 * MIB),
    )(a, b)


def _adamw(w, g, m, v):
    m = ADAM_B1 * m + (1.0 - ADAM_B1) * g
    v = ADAM_B2 * v + (1.0 - ADAM_B2) * (g * g)
    m_hat = m / (1.0 - ADAM_B1 ** ADAM_STEP)
    v_hat = v / (1.0 - ADAM_B2 ** ADAM_STEP)
    delta = -ADAM_LR * (m_hat / (jnp.sqrt(v_hat) + ADAM_EPS) + ADAM_WD * w)
    return delta, m, v


def _adam_big(recv, part, me, off, rows, w, m, v, transpose, name, tr):
    dm = recv.shape[2]
    per = rows // tr

    def body(me_ref, own_ref, r_ref, w_ref, m_ref, v_ref, g_ref, d_ref, mo_ref, vo_ref):
        g = own_ref[...].astype(F32)
        for k in range(len(FLIPS)):
            g = g + r_ref[k].astype(F32)
        if transpose:
            g = g.T
        delta, m_new, v_new = _adamw(w_ref[...], g, m_ref[...], v_ref[...])
        g_ref[...] = g
        d_ref[...] = delta
        mo_ref[...] = m_new
        vo_ref[...] = v_new

    if transpose:
        blk = pl.BlockSpec((dm, tr), lambda i, me_ref: (0, i))
    else:
        blk = pl.BlockSpec((tr, dm), lambda i, me_ref: (i, 0))
    first = off // tr
    return pl.pallas_call(
        body, name=name,
        grid_spec=pltpu.PrefetchScalarGridSpec(
            num_scalar_prefetch=1, grid=(per,),
            in_specs=[pl.BlockSpec((tr, dm), lambda i, me_ref: (me_ref[0] * per + i, 0)),
                      pl.BlockSpec((len(FLIPS), tr, dm), lambda i, me_ref: (0, first + i, 0)), blk, blk, blk],
            out_specs=[blk] * 4),
        out_shape=[jax.ShapeDtypeStruct(w.shape, F32)] * 4,
        compiler_params=_params(32),
    )(me, part, recv, w, m, v)


def _sum_slots(recv_small, recv_last):
    def body(r_ref, l_ref, o_ref):
        g = r_ref[0]
        last = l_ref[0]
        for d in range(1, N_DEV):
            g = g + r_ref[d]
            last = last + l_ref[d]
        o_ref[...] = g
        o_ref[0:1, :] = last[0:1, :]

    return pl.pallas_call(
        body, name="sum_small", in_specs=[_whole()] * 2, out_specs=_whole(),
        out_shape=jax.ShapeDtypeStruct(recv_small.shape[1:], F32),
    )(recv_small, recv_last)


def _adam_small(g, w, m, v, name):
    def body(g_ref, w_ref, m_ref, v_ref, d_ref, mo_ref, vo_ref):
        delta, m_new, v_new = _adamw(w_ref[...], g_ref[...], m_ref[...], v_ref[...])
        d_ref[...] = delta
        mo_ref[...] = m_new
        vo_ref[...] = v_new

    return pl.pallas_call(
        body, name=name, in_specs=[_whole()] * 4, out_specs=[_whole()] * 3,
        out_shape=[jax.ShapeDtypeStruct(w.shape, F32)] * 3,
    )(g, w, m, v)


def _tile(n_tok, want):
    return min(want, n_tok)


def kernel(x, norm1_pre_g, w_in, b_in, conv_a_w, conv_a_b, w_a_out, conv_b_w, conv_b_b, ln_b_g, ln_b_b, w_b_out, w_o, norm1_post_g, norm2_pre_g, w_mlp_in, w_mlp_out, norm2_post_g, loss_target, m_norm1_pre_g, m_w_in, m_b_in, m_conv_a_w, m_conv_a_b, m_w_a_out, m_conv_b_w, m_conv_b_b, m_ln_b_g, m_ln_b_b, m_w_b_out, m_w_o, m_norm1_post_g, m_norm2_pre_g, m_w_mlp_in, m_w_mlp_out, m_norm2_post_g, v_norm1_pre_g, v_w_in, v_b_in, v_conv_a_w, v_conv_a_b, v_w_a_out, v_conv_b_w, v_conv_b_b, v_ln_b_g, v_ln_b_b, v_w_b_out, v_w_o, v_norm1_post_g, v_norm2_pre_g, v_w_mlp_in, v_w_mlp_out, v_norm2_post_g):
    n_tok, dm = x.shape[1], x.shape[2]
    rows, off, slab_rows = _layout(dm)
    cw = dm // N_DEV
    xs = x.reshape(n_tok, dm)
    tgt = loss_target.reshape(n_tok, dm)
    row = lambda vec: vec.reshape(1, -1)
    gathered = lambda group: (N_DEV, slab_rows[group], dm)
    scattered = lambda group: jax.ShapeDtypeStruct((len(FLIPS), slab_rows[group], dm), BF16)
    tm, tk = min(dm, 1024), _tile(n_tok, 2048)
    me = (4 * lax.axis_index("x") + 2 * lax.axis_index("y") + lax.axis_index("c")).astype(jnp.int32).reshape(1)

    conv_own = jnp.concatenate([conv_a_w, jnp.zeros((SUBLANE - CONV_A, cw), F32), conv_b_w,
                                jnp.zeros((1, cw), F32)], axis=0)
    own_in = w_in.T.astype(BF16)
    slab_in, conv_all = _all_gather_two_level(
        [own_in, conv_own], [_place_own(own_in, gathered("in"), me, "place_w_in"), None], "gather_w_in")
    conv_full = conv_all.transpose(1, 0, 2).reshape(conv_own.shape[0], dm)
    caw, cbw = conv_full[0:CONV_A], conv_full[SUBLANE:SUBLANE + CONV_B]
    own_abo = jnp.concatenate([w_a_out, w_b_out, w_o], axis=0).astype(BF16)
    own_mlp = jnp.concatenate([w_mlp_in.T, w_mlp_out], axis=0).astype(BF16)
    ag_abo = _exchange_start([_Part(own_abo, False, slab_rows["abo"], 0, 0)],
                             [_place_own(own_abo, gathered("abo"), me, "place_abo")],
                             "gather_abo_start", after=slab_in)
    ag_mlp = _exchange_start([_Part(own_mlp, False, slab_rows["mlp"], 0, 0)],
                             [_place_own(own_mlp, gathered("mlp"), me, "place_mlp")],
                             "gather_mlp_start", after=ag_abo.token)

    proj, p, u, h = _fwd_in(xs, row(norm1_pre_g), row(b_in), slab_in, _tile(n_tok, 256),
                            [ag_abo.token, ag_mlp.token])
    _, (slab_abo,) = _exchange_wait(ag_abo, "gather_abo_wait", after=proj)
    vb, ya, yb, qa, sb, merged, mix, x1 = _fwd_mix(
        p, u, proj, xs, caw, row(conv_a_b), cbw, row(conv_b_b), row(ln_b_g), row(ln_b_b), row(norm1_post_g),
        slab_abo, _tile(n_tok, 256))
    _, (slab_mlp,) = _exchange_wait(ag_mlp, "gather_mlp_wait", after=x1)
    f, df1, h2, df2, dmix, dx1, small_mlp = _mlp_fwd_bwd(
        x1, mix, tgt, row(norm1_post_g), row(norm2_pre_g), row(norm2_post_g), slab_mlp, _tile(n_tok, 256))

    rs_mlp = _exchange_start(
        [_Part(_wgrad(df1, h2, "wgrad_mlp_in", tm, tk, BF16), True, rows["w1"], 0, off["w1"]),
         _Part(_wgrad(f, df2, "wgrad_mlp_out", tm, tk, BF16), True, rows["w2"], 0, off["w2"])],
        [scattered("mlp")], "scatter_mlp_start")
    dpa, dya, dyb, dva, dvb, small_mix = _bwd_mix(
        dmix, ya, yb, proj, p, vb, caw, row(conv_a_b), row(ln_b_g), row(ln_b_b), slab_abo, _tile(n_tok, 256),
        [rs_mlp.token])
    rs_abo = _exchange_start(
        [_Part(_wgrad(qa, dya, "wgrad_a_out", tm, tk, BF16), True, rows["wa"], 0, off["wa"]),
         _Part(_wgrad(sb, dyb, "wgrad_b_out", tm, tk, BF16), True, rows["wb"], 0, off["wb"]),
         _Part(_wgrad(merged, dmix, "wgrad_o", tm, tk, BF16), True, rows["wo"], 0, off["wo"])],
        [scattered("abo")], "scatter_abo_start")
    dproj, small_conv = _bwd_conv(dva, dvb, p, u, proj, dpa, caw, cbw, _tile(n_tok, 256), [rs_abo.token])

    zeros = lambda r: jnp.zeros((r, dm), F32)
    small = jnp.concatenate([
        zeros(1),
        small_mix[2:3],
        small_mix[3:4],
        small_mix[0:2],
        small_mlp[2:3],
        small_mlp[1:2],
        small_mlp[0:1],
        small_mix[4:5], small_conv[3:7], small_mix[5:7],
        zeros(1),
        small_conv[0:CONV_A], zeros(SUBLANE - CONV_A),
        small_conv[8:8 + CONV_B], zeros(1),
    ], axis=0)

    rs_in = _exchange_start(
        [_Part(_wgrad(dproj, h, "wgrad_in", tm, tk, BF16), True, rows["win"], 0, off["win"]),
         _Part(small, False, small.shape[0], 1, 0)],
        [scattered("in"), _place_own(small, (N_DEV,) + small.shape, me, "place_small")], "scatter_in_start")
    grad_x, small_in = _bwd_in(dproj, xs, dx1, row(norm1_pre_g), slab_in, _tile(n_tok, 256), [rs_in.token])
    (g_w1, g_w2), (recv_mlp,) = _exchange_wait(rs_mlp, "scatter_mlp_wait", after=grad_x)
    (g_wa, g_wb, g_wo), (recv_abo,) = _exchange_wait(rs_abo, "scatter_abo_wait", after=grad_x)
    (g_win, _), (recv_in, recv_small) = _exchange_wait(rs_in, "scatter_in_wait", after=grad_x)
    recv_last, = _all_gather([small_in], "gather_last")

    tr = min(LANE, rows["wa"])
    big = {
        "w_in": _adam_big(recv_in, g_win, me, off["win"], rows["win"], w_in, m_w_in, v_w_in, True, "adam_w_in", tr),
        "w_a_out": _adam_big(recv_abo, g_wa, me, off["wa"], rows["wa"], w_a_out, m_w_a_out, v_w_a_out, False,
                             "adam_w_a_out", tr),
        "w_mlp_in": _adam_big(recv_mlp, g_w1, me, off["w1"], rows["w1"], w_mlp_in, m_w_mlp_in, v_w_mlp_in, True,
                              "adam_w_mlp_in", tr),
        "w_mlp_out": _adam_big(recv_mlp, g_w2, me, off["w2"], rows["w2"], w_mlp_out, m_w_mlp_out, v_w_mlp_out,
                               False, "adam_w_mlp_out", tr),
        "w_b_out": _adam_big(recv_abo, g_wb, me, off["wb"], rows["wb"], w_b_out, m_w_b_out, v_w_b_out, False,
                             "adam_w_b_out", tr),
        "w_o": _adam_big(recv_abo, g_wo, me, off["wo"], rows["wo"], w_o, m_w_o, v_w_o, False, "adam_w_o", tr),
    }

    gsum = _sum_slots(recv_small, recv_last)
    rep = lambda vec8, bias: jnp.concatenate([jnp.stack(vec8, axis=0), bias.reshape(7, dm), zeros(1)], axis=0)
    w_rep = rep((norm1_pre_g, conv_a_b, conv_b_b, ln_b_g, ln_b_b, norm1_post_g, norm2_pre_g, norm2_post_g), b_in)
    m_rep = rep((m_norm1_pre_g, m_conv_a_b, m_conv_b_b, m_ln_b_g, m_ln_b_b, m_norm1_post_g, m_norm2_pre_g,
                 m_norm2_post_g), m_b_in)
    v_rep = rep((v_norm1_pre_g, v_conv_a_b, v_conv_b_b, v_ln_b_g, v_ln_b_b, v_norm1_post_g, v_norm2_pre_g,
                 v_norm2_post_g), v_b_in)
    g_rep = gsum[0:16]
    d_rep, mo_rep, vo_rep = _adam_small(g_rep, w_rep, m_rep, v_rep, "adam_replicated")

    me = 4 * lax.axis_index("x") + 2 * lax.axis_index("y") + lax.axis_index("c")
    g_conv = lax.dynamic_slice_in_dim(gsum[16:56], me * cw, cw, axis=1)
    pad_conv = lambda a3, b31: jnp.concatenate(
        [a3, jnp.zeros((SUBLANE - CONV_A, cw), F32), b31, jnp.zeros((1, cw), F32)], axis=0)
    d_conv, mo_conv, vo_conv = _adam_small(
        g_conv, pad_conv(conv_a_w, conv_b_w), pad_conv(m_conv_a_w, m_conv_b_w),
        pad_conv(v_conv_a_w, v_conv_b_w), "adam_conv")

    def small_leaf(stack_rep, stack_conv, name):
        idx = {"norm1_pre_g": 0, "conv_a_b": 1, "conv_b_b": 2, "ln_b_g": 3, "ln_b_b": 4, "norm1_post_g": 5,
               "norm2_pre_g": 6, "norm2_post_g": 7}
        if name in idx:
            return stack_rep[idx[name]]
        if name == "b_in":
            return stack_rep[8:15].reshape(7 * dm)
        if name == "conv_a_w":
            return stack_conv[0:CONV_A]
        return stack_conv[SUBLANE:SUBLANE + CONV_B]

    order = ("norm1_pre_g", "w_in", "b_in", "conv_a_w", "conv_a_b", "w_a_out", "conv_b_w", "conv_b_b", "ln_b_g",
             "ln_b_b", "w_b_out", "w_o", "norm1_post_g", "norm2_pre_g", "w_mlp_in", "w_mlp_out", "norm2_post_g")
    grads, deltas, new_m, new_v = [], [], [], []
    for name in order:
        if name in big:
            g, d, mo, vo = big[name]
        else:
            g = small_leaf(g_rep, g_conv, name)
            d = small_leaf(d_rep, d_conv, name)
            mo = small_leaf(mo_rep, mo_conv, name)
            vo = small_leaf(vo_rep, vo_conv, name)
        grads.append(g)
        deltas.append(d)
        new_m.append(mo)
        new_v.append(vo)

    loss_local = (0.5 / dm) * jnp.sum(small_mlp[3])
    loss = lax.psum(loss_local, ("x", "y", "c"))
    return (loss, grad_x.reshape(x.shape), *grads, *deltas, *new_m, *new_v)
```

```python
from typing import NamedTuple

import jax
import jax.numpy as jnp
from jax import lax
from jax.experimental import pallas as pl
from jax.experimental.pallas import tpu as pltpu

F32 = jnp.float32
BF16 = jnp.bfloat16

RMS_EPS = 1e-6
LN_EPS = 1e-5
ADAM_LR = 0.001
ADAM_B1 = 0.9
ADAM_B2 = 0.999
ADAM_EPS = 1e-08
ADAM_WD = 0.01
ADAM_STEP = 10

N_DEV = 8
CONV_A = 3
CONV_B = 31
LANE = 128
SUBLANE = 8
HALO = 16
CONV_ROWS = 64
MIB = 1 << 20
FLIPS = ((0, 0, 1), (0, 1, 0), (1, 0, 0), (0, 1, 1), (1, 0, 1), (1, 1, 0), (1, 1, 1))
MESH = pl.DeviceIdType.MESH


def _layout(d_model):
    e = d_model // N_DEV
    rows = {"win": 7 * e, "w1": 4 * e, "w2": 4 * e, "wa": e, "wb": e, "wo": e}
    off = {"win": 0, "w1": 0, "w2": 4 * e, "wa": 0, "wb": e, "wo": 2 * e}
    return rows, off, {"in": 7 * e, "mlp": 8 * e, "abo": 3 * e}


def _after(body, deps):
    def ordered(*refs):
        return body(*refs[len(deps):])
    return ordered


def _params(vmem_mib):
    return pltpu.CompilerParams(dimension_semantics=("arbitrary",), vmem_limit_bytes=vmem_mib * MIB)


def _whole():
    return pl.BlockSpec(memory_space=pltpu.VMEM)


def _hbm():
    return pl.BlockSpec(memory_space=pl.ANY)


def _rows(ts, width, col=0):
    return pl.BlockSpec((ts, width), lambda i: (i, col))


def _halo_specs(ts, width, n_rows):
    per = ts // HALO
    last = n_rows // HALO - 1
    return [
        pl.BlockSpec((ts, width), lambda i: (i, 0)),
        pl.BlockSpec((HALO, width), lambda i: (jnp.maximum(i * per - 1, 0), 0)),
        pl.BlockSpec((HALO, width), lambda i: (jnp.minimum((i + 1) * per, last), 0)),
    ]


def _dot(a, b):
    return jnp.dot(a, b, preferred_element_type=F32)


def _dot_nt(a, b):
    return lax.dot_general(a, b, (((1,), (1,)), ((), ())), preferred_element_type=F32)


def _dot_tn(a, b):
    return lax.dot_general(a, b, (((0,), (0,)), ((), ())), preferred_element_type=F32)


def _rms(u):
    return lax.rsqrt(jnp.mean(u * u, axis=-1, keepdims=True) + RMS_EPS)


def _rms_bwd(dz, u, r, g):
    dzg = dz * g
    return r * dzg - u * (r * r * r) * jnp.mean(dzg * u, axis=-1, keepdims=True)


def _colsum(v):
    return jnp.sum(v, axis=0, keepdims=True)


def _sigmoid(v):
    return jax.nn.sigmoid(v)


def _load_weight(slab_ref, off, rows, dst_ref, sems, first_sem):
    copies = []
    for d in range(N_DEV):
        cp = pltpu.make_async_copy(slab_ref.at[d, pl.ds(off, rows), :], dst_ref.at[pl.ds(d * rows, rows), :],
                                   sems.at[first_sem + d])
        cp.start()
        copies.append(cp)
    return copies


def _fill_ext(ext_ref, main_ref, prev_ref, next_ref, i, n_steps, ts):
    ext_ref[0:HALO, :] = jnp.where(i > 0, prev_ref[...], 0.0)
    ext_ref[HALO:HALO + ts, :] = main_ref[...]
    ext_ref[HALO + ts:HALO + ts + HALO, :] = jnp.where(i < n_steps - 1, next_ref[...], 0.0)


def _broadcast_taps(w_ref, wb_ref, n_taps):
    for k in range(n_taps):
        wb_ref[k] = jnp.broadcast_to(w_ref[k:k + 1, :], wb_ref.shape[1:])


def _phases(starts):
    groups = {}
    for k, st in enumerate(starts):
        groups.setdefault(st % SUBLANE, []).append((k, st // SUBLANE))
    return sorted(groups.items())


def _shifted(blk, b):
    n = blk.shape[0]
    rolled = blk if b == 0 else pltpu.roll(blk, n - b, axis=0)
    return rolled.reshape(n // SUBLANE, SUBLANE, blk.shape[1])


def _conv_tile(ext_ref, wb_ref, starts, ts, width, emit):
    span = CONV_ROWS + 2 * HALO
    groups = _phases(starts)
    nv = CONV_ROWS // SUBLANE

    def row_block(rb, carry):
        r0 = pl.multiple_of(rb * CONV_ROWS, CONV_ROWS)
        for cb in range(width // LANE):
            lanes = pl.ds(cb * LANE, LANE)
            blk = ext_ref[pl.ds(r0, span), lanes]
            acc = jnp.zeros((nv, SUBLANE, LANE), F32)
            for b, taps in groups:
                sh = _shifted(blk, b)
                for k, m in taps:
                    acc = acc + sh[m:m + nv] * wb_ref[k, :, lanes][None]
            emit(r0, lanes, acc.reshape(CONV_ROWS, LANE))
        return carry

    lax.fori_loop(0, ts // CONV_ROWS, row_block, 0)


def _conv_wgrad_tile(ext_ref, dv_ref, acc_ref, starts, ts, width):
    span = CONV_ROWS + 2 * HALO
    groups = _phases(starts)
    nv = CONV_ROWS // SUBLANE

    def row_block(rb, carry):
        r0 = pl.multiple_of(rb * CONV_ROWS, CONV_ROWS)
        for cb in range(width // LANE):
            lanes = pl.ds(cb * LANE, LANE)
            blk = ext_ref[pl.ds(r0, span), lanes]
            dv = dv_ref[pl.ds(r0, CONV_ROWS), lanes].reshape(nv, SUBLANE, LANE)
            for b, taps in groups:
                sh = _shifted(blk, b)
                for k, m in taps:
                    acc_ref[k, :, lanes] += jnp.sum(sh[m:m + nv] * dv, axis=0)
        return carry

    lax.fori_loop(0, ts // CONV_ROWS, row_block, 0)


def _fwd_starts(n_taps):
    pad = (n_taps - 1) // 2
    return [HALO - pad + k for k in range(n_taps)]


def _bwd_starts(n_taps):
    pad = (n_taps - 1) // 2
    return [HALO + pad - k for k in range(n_taps)]


def _peer(x, y, c, flip):
    fx, fy, fc = flip
    return (1 - x if fx else x, 1 - y if fy else y, 1 - c if fc else c)


def _all_gather(shards, name):
    n = len(shards)

    def body(*refs):
        ins, outs = refs[:n], refs[n:2 * n]
        send_sems, recv_sems, local_sems = refs[2 * n:]
        x, y, c = lax.axis_index("x"), lax.axis_index("y"), lax.axis_index("c")
        me = 4 * x + 2 * y + c
        local = [pltpu.make_async_copy(ins[j], outs[j].at[me], local_sems.at[j]) for j in range(n)]
        for cp in local:
            cp.start()
        sends, recvs = [], []
        for k, flip in enumerate(FLIPS):
            px, py, pc = _peer(x, y, c, flip)
            peer = 4 * px + 2 * py + pc
            for j in range(n):
                sem = k * n + j
                sends.append(pltpu.make_async_remote_copy(
                    src_ref=ins[j], dst_ref=outs[j].at[me], send_sem=send_sems.at[sem], recv_sem=recv_sems.at[sem],
                    device_id=(px, py, pc), device_id_type=MESH))
                recvs.append(pltpu.make_async_remote_copy(
                    src_ref=ins[j], dst_ref=outs[j].at[peer], send_sem=send_sems.at[sem], recv_sem=recv_sems.at[sem],
                    device_id=(px, py, pc), device_id_type=MESH))
        for cp in sends:
            cp.start()
        for cp in recvs:
            cp.wait_recv()
        for cp in sends:
            cp.wait_send()
        for cp in local:
            cp.wait()

    return pl.pallas_call(
        body, name=name,
        out_shape=[jax.ShapeDtypeStruct((N_DEV,) + s.shape, s.dtype) for s in shards],
        in_specs=[_hbm()] * n, out_specs=[_hbm()] * n,
        scratch_shapes=[pltpu.SemaphoreType.DMA((7 * n,)), pltpu.SemaphoreType.DMA((7 * n,)),
                        pltpu.SemaphoreType.DMA((n,))],
    )(*shards)


def _place_own(src, n_slots_shape, me, name):
    rows, width = src.shape
    tr = next(t for t in (256, 128, 64, 32, 16, SUBLANE) if rows % t == 0)

    def body(me_ref, src_ref, out_ref):
        out_ref[...] = src_ref[...]

    return pl.pallas_call(
        body, name=name,
        grid_spec=pltpu.PrefetchScalarGridSpec(
            num_scalar_prefetch=1, grid=(rows // tr,),
            in_specs=[pl.BlockSpec((tr, width), lambda i, me_ref: (i, 0))],
            out_specs=pl.BlockSpec((None, tr, width), lambda i, me_ref: (me_ref[0], i, 0))),
        out_shape=jax.ShapeDtypeStruct(n_slots_shape, src.dtype),
    )(me, src)


def _all_gather_two_level(shards, placed, name):
    n = len(shards)
    given = [j for j in range(n) if placed[j] is not None]

    def body(*refs):
        ins, outs = refs[:n], refs[n + len(given):2 * n + len(given)]
        send_sems, recv_sems, local_sems = refs[2 * n + len(given):]
        x, y, c = lax.axis_index("x"), lax.axis_index("y"), lax.axis_index("c")
        me, sibling = (x, y, c), (x, y, 1 - c)
        chips = [(1 - x, y), (x, 1 - y), (1 - x, 1 - y)]

        def slot(j, dev):
            return outs[j].at[4 * dev[0] + 2 * dev[1] + dev[2]]

        def copy(k, j, block, to, src=None):
            return pltpu.make_async_remote_copy(
                src_ref=slot(j, block) if src is None else src, dst_ref=slot(j, block),
                send_sem=send_sems.at[k * n + j], recv_sem=recv_sems.at[k * n + j], device_id=to, device_id_type=MESH)

        local = [pltpu.make_async_copy(ins[j], slot(j, me), local_sems.at[j]) for j in range(n) if j not in given]
        for cp in local:
            cp.start()
        first = [copy(0, j, me, sibling, src=ins[j]) for j in range(n)]
        first += [copy(1 + t, j, me, (*chip, c), src=ins[j]) for t, chip in enumerate(chips) for j in range(n)]
        for cp in first:
            cp.start()
        passed = []
        for t, chip in enumerate(chips):
            for j in range(n):
                copy(1 + t, j, (*chip, c), me).wait_recv()
                passed.append(copy(4 + t, j, (*chip, c), sibling))
                passed[-1].start()
        for j in range(n):
            copy(0, j, sibling, me).wait_recv()
        for t, chip in enumerate(chips):
            for j in range(n):
                copy(4 + t, j, (*chip, 1 - c), me).wait_recv()
        for cp in first + passed:
            cp.wait_send()
        for cp in local:
            cp.wait()

    return pl.pallas_call(
        body, name=name,
        out_shape=[jax.ShapeDtypeStruct((N_DEV,) + s.shape, s.dtype) for s in shards],
        in_specs=[_hbm()] * (n + len(given)), out_specs=[_hbm()] * n,
        input_output_aliases={n + i: j for i, j in enumerate(given)},
        scratch_shapes=[pltpu.SemaphoreType.DMA((7 * n,)), pltpu.SemaphoreType.DMA((7 * n,)),
                        pltpu.SemaphoreType.DMA((n,))],
    )(*shards, *[placed[j] for j in given])


class _Part(NamedTuple):
    src: jax.Array
    scatter: bool
    rows: int
    land: int
    off: int


class _Started(NamedTuple):
    send_sems: jax.Array
    recv_sems: jax.Array
    thru: tuple
    token: jax.Array
    parts: tuple


def _exchange_copies(srcs, lands, send_sems, recv_sems, parts):
    n = len(parts)
    x, y, c = lax.axis_index("x"), lax.axis_index("y"), lax.axis_index("c")
    me = 4 * x + 2 * y + c

    def block(j, dev):
        p = parts[j]
        return srcs[j].at[pl.ds(pl.multiple_of(dev * p.rows, SUBLANE), p.rows), :] if p.scatter else srcs[j]

    def slot(j, index):
        p = parts[j]
        return lands[p.land].at[index, pl.ds(p.off, p.rows), :]

    sends, recvs = [], []
    for k, flip in enumerate(FLIPS):
        px, py, pc = _peer(x, y, c, flip)
        peer = 4 * px + 2 * py + pc
        for j in range(n):
            sems = dict(send_sem=send_sems.at[k * n + j], recv_sem=recv_sems.at[k * n + j],
                        device_id=(px, py, pc), device_id_type=MESH)
            to, got = (k, k) if parts[j].scatter else (me, peer)
            sends.append(pltpu.make_async_remote_copy(src_ref=block(j, peer), dst_ref=slot(j, to), **sems))
            recvs.append(pltpu.make_async_remote_copy(src_ref=block(j, peer), dst_ref=slot(j, got), **sems))
    return sends, recvs


def _exchange_start(parts, lands, name, after=None):
    n, nl = len(parts), len(lands)
    n_in = n + nl + (after is not None)

    def body(*refs):
        srcs, land_refs = refs[:n], refs[n:n + nl]
        send_sems, recv_sems = refs[n_in], refs[n_in + 1]
        token = refs[n_in + 2 + n + nl]
        sends, _ = _exchange_copies(srcs, land_refs, send_sems, recv_sems, parts)
        for cp in sends:
            cp.start()
        token[...] = jnp.zeros_like(token)

    hbm = pl.BlockSpec(memory_space=pltpu.HBM)
    sem = pl.BlockSpec(memory_space=pltpu.SEMAPHORE)
    fresh = lambda s: lax.empty(s.shape, s.dtype) if isinstance(s, jax.ShapeDtypeStruct) else s
    args = [pltpu.with_memory_space_constraint(p.src, pltpu.HBM) for p in parts]
    args += [pltpu.with_memory_space_constraint(fresh(s), pltpu.HBM) for s in lands]
    args += [] if after is None else [after]
    out = pl.pallas_call(
        body, name=name,
        out_shape=(pltpu.SemaphoreType.DMA((7 * n,)), pltpu.SemaphoreType.DMA((7 * n,)),
                   *[pltpu.HBM(a.shape, a.dtype) for a in args[:n + nl]], jax.ShapeDtypeStruct((SUBLANE, LANE), F32)),
        in_specs=[hbm] * (n + nl) + [_hbm()] * (after is not None),
        out_specs=(sem, sem, *[hbm] * (n + nl), _whole()),
        input_output_aliases={j: 2 + j for j in range(n + nl)},
        compiler_params=pltpu.CompilerParams(has_side_effects=pltpu.SideEffectType.DATAFLOW_SIDE_EFFECTING),
    )(*args)
    return _Started(out[0], out[1], tuple(out[2:2 + n + nl]), out[2 + n + nl], tuple(parts))


def _exchange_wait(started, name, after):
    parts = started.parts
    n, nl = len(parts), len(started.thru) - len(parts)

    def body(*refs):
        srcs, land_refs = refs[:n], refs[n:n + nl]
        send_sems, recv_sems = refs[n + nl], refs[n + nl + 1]
        sends, recvs = _exchange_copies(srcs, land_refs, send_sems, recv_sems, parts)
        for cp in sends:
            cp.wait_send()
        for cp in recvs:
            cp.wait_recv()

    hbm = pl.BlockSpec(memory_space=pltpu.HBM)
    sem = pl.BlockSpec(memory_space=pltpu.SEMAPHORE)
    out = pl.pallas_call(
        body, name=name,
        out_shape=tuple(pltpu.HBM(a.shape, a.dtype) for a in started.thru),
        in_specs=[hbm] * (n + nl) + [sem, sem, _hbm()], out_specs=tuple([hbm] * (n + nl)),
        input_output_aliases={j: j for j in range(n + nl)},
        compiler_params=pltpu.CompilerParams(has_side_effects=pltpu.SideEffectType.DATAFLOW_SIDE_EFFECTING),
    )(*started.thru, started.send_sems, started.recv_sems, after)
    return list(out[:n]), list(out[n:])


def _fwd_in(x, g1, b_in, slab, ts, deps):
    n_tok, dm = x.shape
    rows, off, _ = _layout(dm)
    width = 7 * dm

    def body(x_ref, g1_ref, b_ref, slab_ref, proj_ref, p_ref, u_ref, h_ref, w_v, sems):
        @pl.when(pl.program_id(0) == 0)
        def _():
            for cp in _load_weight(slab_ref, off["win"], rows["win"], w_v, sems, 0):
                cp.wait()

        xv = x_ref[...]
        h = (xv * _rms(xv) * g1_ref[...]).astype(BF16)
        h_ref[...] = h
        cols = []
        for j in range(7):
            pj = _dot_nt(h, w_v[pl.ds(j * dm, dm), :]) + b_ref[:, j * dm:(j + 1) * dm]
            proj_ref[:, j * dm:(j + 1) * dm] = pj.astype(proj_ref.dtype)
            if 1 <= j <= 4:
                cols.append(pj)
            if j == 2:
                p_ref[...] = cols[0] * cols[1]
            if j == 4:
                u_ref[...] = cols[2] * _sigmoid(cols[3])

    return pl.pallas_call(
        _after(body, deps), name="fwd_in", grid=(n_tok // ts,),
        in_specs=[_whole()] * len(deps) + [_rows(ts, dm), _whole(), _whole(), _hbm()],
        out_specs=[_rows(ts, width), _rows(ts, dm), _rows(ts, dm), _rows(ts, dm)],
        out_shape=[jax.ShapeDtypeStruct((n_tok, width), BF16), jax.ShapeDtypeStruct((n_tok, dm), F32),
                   jax.ShapeDtypeStruct((n_tok, dm), F32), jax.ShapeDtypeStruct((n_tok, dm), BF16)],
        scratch_shapes=[pltpu.VMEM((width, dm), BF16), pltpu.SemaphoreType.DMA((N_DEV,))],
        compiler_params=_params(40),
    )(*deps, x, g1, b_in, slab)


def _fwd_mix(p, u, proj, x, caw, cab, cbw, cbb, lng, lnb, g1post, slab, ts):
    n_tok, dm = x.shape
    rows, off, _ = _layout(dm)
    n_steps = n_tok // ts

    def body(p_ref, p_prev, p_next, u_ref, u_prev, u_next, bg_ref, za_ref, zb_ref, x_ref,
             caw_ref, cab_ref, cbw_ref, cbb_ref, lng_ref, lnb_ref, g1p_ref, slab_ref,
             vb_ref, ya_ref, yb_ref, qa_ref, sb_ref, mg_ref, mix_ref, x1_ref,
             wa_v, wb_v, wo_v, ext_p, ext_u, va_v, tap_a, tap_b, sems):
        i = pl.program_id(0)

        @pl.when(i == 0)
        def _():
            cps = _load_weight(slab_ref, off["wa"], rows["wa"], wa_v, sems, 0)
            cps += _load_weight(slab_ref, off["wb"], rows["wb"], wb_v, sems, N_DEV)
            cps += _load_weight(slab_ref, off["wo"], rows["wo"], wo_v, sems, 2 * N_DEV)
            _broadcast_taps(caw_ref, tap_a, CONV_A)
            _broadcast_taps(cbw_ref, tap_b, CONV_B)
            for cp in cps:
                cp.wait()

        _fill_ext(ext_p, p_ref, p_prev, p_next, i, n_steps, ts)
        _fill_ext(ext_u, u_ref, u_prev, u_next, i, n_steps, ts)

        def emit_a(r0, lanes, acc):
            va_v[pl.ds(r0, CONV_ROWS), lanes] = acc + cab_ref[:, lanes]

        def emit_b(r0, lanes, acc):
            vb_ref[pl.ds(r0, CONV_ROWS), lanes] = acc + cbb_ref[:, lanes]

        _conv_tile(ext_p, tap_a, _fwd_starts(CONV_A), ts, dm, emit_a)
        _conv_tile(ext_u, tap_b, _fwd_starts(CONV_B), ts, dm, emit_b)

        qa = (bg_ref[...].astype(F32) * va_v[...]).astype(BF16)
        qa_ref[...] = qa
        ya = _dot(qa, wa_v[...])
        vb = vb_ref[...]
        xc = vb - jnp.mean(vb, axis=-1, keepdims=True)
        rstd = lax.rsqrt(jnp.mean(xc * xc, axis=-1, keepdims=True) + LN_EPS)
        ln = xc * rstd * lng_ref[...] + lnb_ref[...]
        sb = (ln * _sigmoid(ln)).astype(BF16)
        sb_ref[...] = sb
        yb = _dot(sb, wb_v[...])
        ya_ref[...] = ya.astype(BF16)
        yb_ref[...] = yb.astype(BF16)
        merged = (_sigmoid(za_ref[...].astype(F32)) * ya + _sigmoid(zb_ref[...].astype(F32)) * yb).astype(BF16)
        mg_ref[...] = merged
        mix = _dot(merged, wo_v[...])
        mix_ref[...] = mix
        x1_ref[...] = x_ref[...] + mix * _rms(mix) * g1p_ref[...]

    tok = lambda dt: jax.ShapeDtypeStruct((n_tok, dm), dt)
    return pl.pallas_call(
        body, name="fwd_mix", grid=(n_steps,),
        in_specs=(_halo_specs(ts, dm, n_tok) + _halo_specs(ts, dm, n_tok)
                  + [_rows(ts, dm, 0), _rows(ts, dm, 5), _rows(ts, dm, 6), _rows(ts, dm)]
                  + [_whole()] * 7 + [_hbm()]),
        out_specs=[_rows(ts, dm)] * 8,
        out_shape=[tok(F32), tok(BF16), tok(BF16), tok(BF16), tok(BF16), tok(BF16), tok(F32), tok(F32)],
        scratch_shapes=[pltpu.VMEM((dm, dm), BF16), pltpu.VMEM((dm, dm), BF16), pltpu.VMEM((dm, dm), BF16),
                        pltpu.VMEM((ts + 2 * HALO, dm), F32), pltpu.VMEM((ts + 2 * HALO, dm), F32),
                        pltpu.VMEM((ts, dm), F32),
                        pltpu.VMEM((CONV_A, SUBLANE, dm), F32), pltpu.VMEM((CONV_B, SUBLANE, dm), F32),
                        pltpu.SemaphoreType.DMA((3 * N_DEV,))],
        compiler_params=_params(48),
    )(p, p, p, u, u, u, proj, proj, proj, x, caw, cab, cbw, cbb, lng, lnb, g1post, slab)


def _mlp_fwd_bwd(x1, mix, tgt, g1post, g2pre, g2post, slab, ts):
    n_tok, dm = x1.shape
    rows, off, _ = _layout(dm)
    ff = 4 * dm

    def body(x1_ref, mix_ref, t_ref, g1p_ref, g2pre_ref, g2post_ref, slab_ref,
             f_ref, df1_ref, h2_ref, df2_ref, dmix_ref, dx1_ref, small_ref, w1_v, w2_v, relu_v, sems):
        @pl.when(pl.program_id(0) == 0)
        def _():
            cps = _load_weight(slab_ref, off["w1"], rows["w1"], w1_v, sems, 0)
            cps += _load_weight(slab_ref, off["w2"], rows["w2"], w2_v, sems, N_DEV)
            small_ref[...] = jnp.zeros_like(small_ref)
            for cp in cps:
                cp.wait()

        x1v = x1_ref[...]
        r3 = _rms(x1v)
        g2pre = g2pre_ref[...]
        h2 = (x1v * r3 * g2pre).astype(BF16)
        h2_ref[...] = h2
        f2 = jnp.zeros((ts, dm), F32)
        for c in range(4):
            blk = pl.ds(c * dm, dm)
            relu = jnp.maximum(_dot_nt(h2, w1_v[blk, :]), 0.0)
            relu_v[:, c * dm:(c + 1) * dm] = relu
            fc = (relu * relu).astype(BF16)
            f_ref[:, c * dm:(c + 1) * dm] = fc
            f2 = f2 + _dot(fc, w2_v[blk, :])
        r4 = _rms(f2)
        g2post = g2post_ref[...]
        err = x1v + f2 * r4 * g2post - t_ref[...]
        dy = err * (1.0 / dm)
        small_ref[3:4, :] += _colsum(err * err)
        small_ref[0:1, :] += _colsum(dy * f2 * r4)
        df2 = _rms_bwd(dy, f2, r4, g2post).astype(BF16)
        df2_ref[...] = df2
        dh2 = jnp.zeros((ts, dm), F32)
        for c in range(4):
            blk = pl.ds(c * dm, dm)
            df1 = (_dot_nt(df2, w2_v[blk, :]) * (2.0 * relu_v[:, c * dm:(c + 1) * dm])).astype(BF16)
            df1_ref[:, c * dm:(c + 1) * dm] = df1
            dh2 = dh2 + _dot(df1, w1_v[blk, :])
        small_ref[1:2, :] += _colsum(dh2 * x1v * r3)
        dx1 = dy + _rms_bwd(dh2, x1v, r3, g2pre)
        dx1_ref[...] = dx1
        mixv = mix_ref[...]
        r2 = _rms(mixv)
        small_ref[2:3, :] += _colsum(dx1 * mixv * r2)
        dmix_ref[...] = _rms_bwd(dx1, mixv, r2, g1p_ref[...]).astype(BF16)

    tok = lambda w, dt: jax.ShapeDtypeStruct((n_tok, w), dt)
    return pl.pallas_call(
        body, name="mlp_fwd_bwd", grid=(n_tok // ts,),
        in_specs=[_rows(ts, dm)] * 3 + [_whole()] * 3 + [_hbm()],
        out_specs=[_rows(ts, ff), _rows(ts, ff), _rows(ts, dm), _rows(ts, dm), _rows(ts, dm), _rows(ts, dm),
                   pl.BlockSpec((SUBLANE, dm), lambda i: (0, 0))],
        out_shape=[tok(ff, BF16), tok(ff, BF16), tok(dm, BF16), tok(dm, BF16), tok(dm, BF16), tok(dm, F32),
                   jax.ShapeDtypeStruct((SUBLANE, dm), F32)],
        scratch_shapes=[pltpu.VMEM((ff, dm), BF16), pltpu.VMEM((ff, dm), BF16), pltpu.VMEM((ts, ff), F32),
                        pltpu.SemaphoreType.DMA((2 * N_DEV,))],
        compiler_params=_params(56),
    )(x1, mix, tgt, g1post, g2pre, g2post, slab)


def _bwd_mix(dmix, ya, yb, proj, p, vb, caw, cab, lng, lnb, slab, ts, deps):
    n_tok, dm = dmix.shape
    rows, off, _ = _layout(dm)
    n_steps = n_tok // ts

    def body(dmix_ref, ya_ref, yb_ref, bg_ref, za_ref, zb_ref, p_ref, p_prev, p_next, vb_ref,
             caw_ref, cab_ref, lng_ref, lnb_ref, slab_ref,
             dpa_ref, dya_ref, dyb_ref, dva_ref, dvb_ref, small_ref,
             wa_v, wb_v, wo_v, ext_p, va_v, tap_a, sems):
        i = pl.program_id(0)

        @pl.when(i == 0)
        def _():
            cps = _load_weight(slab_ref, off["wa"], rows["wa"], wa_v, sems, 0)
            cps += _load_weight(slab_ref, off["wb"], rows["wb"], wb_v, sems, N_DEV)
            cps += _load_weight(slab_ref, off["wo"], rows["wo"], wo_v, sems, 2 * N_DEV)
            _broadcast_taps(caw_ref, tap_a, CONV_A)
            small_ref[...] = jnp.zeros_like(small_ref)
            for cp in cps:
                cp.wait()

        _fill_ext(ext_p, p_ref, p_prev, p_next, i, n_steps, ts)


        def emit_a(r0, lanes, acc):
            va_v[pl.ds(r0, CONV_ROWS), lanes] = acc + cab_ref[:, lanes]

        _conv_tile(ext_p, tap_a, _fwd_starts(CONV_A), ts, dm, emit_a)

        dmerged = _dot_nt(dmix_ref[...], wo_v[...])
        sa = _sigmoid(za_ref[...].astype(F32))
        sg = _sigmoid(zb_ref[...].astype(F32))
        dza = dmerged * ya_ref[...].astype(F32) * sa * (1.0 - sa)
        dzb = dmerged * yb_ref[...].astype(F32) * sg * (1.0 - sg)
        dpa_ref[:, dm:2 * dm] = dza.astype(BF16)
        dpa_ref[:, 2 * dm:3 * dm] = dzb.astype(BF16)
        small_ref[5:6, :] += _colsum(dza)
        small_ref[6:7, :] += _colsum(dzb)

        dya = (dmerged * sa).astype(BF16)
        dya_ref[...] = dya
        dqa = _dot_nt(dya, wa_v[...])
        dbg = dqa * va_v[...]
        dpa_ref[:, 0:dm] = dbg.astype(BF16)
        small_ref[4:5, :] += _colsum(dbg)
        dva = dqa * bg_ref[...].astype(F32)
        dva_ref[...] = dva
        small_ref[2:3, :] += _colsum(dva)

        dyb = (dmerged * sg).astype(BF16)
        dyb_ref[...] = dyb
        dsb = _dot_nt(dyb, wb_v[...])
        vb = vb_ref[...]
        xc = vb - jnp.mean(vb, axis=-1, keepdims=True)
        rstd = lax.rsqrt(jnp.mean(xc * xc, axis=-1, keepdims=True) + LN_EPS)
        nrm = xc * rstd
        lng_v = lng_ref[...]
        ln = nrm * lng_v + lnb_ref[...]
        sl = _sigmoid(ln)
        dln = dsb * (sl * (1.0 + ln * (1.0 - sl)))
        small_ref[0:1, :] += _colsum(dln * nrm)
        small_ref[1:2, :] += _colsum(dln)
        dn = dln * lng_v
        dvb = rstd * (dn - jnp.mean(dn, axis=-1, keepdims=True)
                      - nrm * jnp.mean(dn * nrm, axis=-1, keepdims=True))
        dvb_ref[...] = dvb
        small_ref[3:4, :] += _colsum(dvb)

    tok = lambda w, dt: jax.ShapeDtypeStruct((n_tok, w), dt)
    return pl.pallas_call(
        _after(body, deps), name="bwd_mix", grid=(n_steps,),
        in_specs=([_whole()] * len(deps) + [_rows(ts, dm)] * 3+ [_rows(ts, dm, 0), _rows(ts, dm, 5), _rows(ts, dm, 6)]
                  + _halo_specs(ts, dm, n_tok) + [_rows(ts, dm)] + [_whole()] * 4 + [_hbm()]),
        out_specs=[_rows(ts, 3 * dm), _rows(ts, dm), _rows(ts, dm), _rows(ts, dm), _rows(ts, dm),
                   pl.BlockSpec((SUBLANE, dm), lambda i: (0, 0))],
        out_shape=[tok(3 * dm, BF16), tok(dm, BF16), tok(dm, BF16), tok(dm, F32), tok(dm, F32),
                   jax.ShapeDtypeStruct((SUBLANE, dm), F32)],
        scratch_shapes=[pltpu.VMEM((dm, dm), BF16), pltpu.VMEM((dm, dm), BF16), pltpu.VMEM((dm, dm), BF16),
                        pltpu.VMEM((ts + 2 * HALO, dm), F32), pltpu.VMEM((ts, dm), F32),
                        pltpu.VMEM((CONV_A, SUBLANE, dm), F32), pltpu.SemaphoreType.DMA((3 * N_DEV,))],
        compiler_params=_params(48),
    )(*deps, dmix, ya, yb, proj, proj, proj, p, p, p, vb, caw, cab, lng, lnb, slab)


def _bwd_conv(dva, dvb, p, u, proj, dpa, caw, cbw, ts, deps):
    n_tok, dm = dva.shape
    n_steps = n_tok // ts
    small_rows = 40

    def body(dva_ref, dva_prev, dva_next, dvb_ref, dvb_prev, dvb_next, p_ref, p_prev, p_next,
             u_ref, u_prev, u_next, cg_ref, ha_ref, a_ref, g_ref, dpa_ref, caw_ref, cbw_ref,
             dproj_ref, small_ref,
             ext_dva, ext_dvb, ext_p, ext_u, dp_v, du_v, tap_a, tap_b, gwa_v, gwb_v):
        i = pl.program_id(0)

        @pl.when(i == 0)
        def _():
            _broadcast_taps(caw_ref, tap_a, CONV_A)
            _broadcast_taps(cbw_ref, tap_b, CONV_B)
            small_ref[...] = jnp.zeros_like(small_ref)
            gwa_v[...] = jnp.zeros_like(gwa_v)
            gwb_v[...] = jnp.zeros_like(gwb_v)

        _fill_ext(ext_dva, dva_ref, dva_prev, dva_next, i, n_steps, ts)
        _fill_ext(ext_dvb, dvb_ref, dvb_prev, dvb_next, i, n_steps, ts)
        _fill_ext(ext_p, p_ref, p_prev, p_next, i, n_steps, ts)
        _fill_ext(ext_u, u_ref, u_prev, u_next, i, n_steps, ts)


        def emit_dp(r0, lanes, acc):
            dp_v[pl.ds(r0, CONV_ROWS), lanes] = acc

        def emit_du(r0, lanes, acc):
            du_v[pl.ds(r0, CONV_ROWS), lanes] = acc

        _conv_tile(ext_dva, tap_a, _bwd_starts(CONV_A), ts, dm, emit_dp)
        _conv_tile(ext_dvb, tap_b, _bwd_starts(CONV_B), ts, dm, emit_du)
        _conv_wgrad_tile(ext_p, dva_ref, gwa_v, _fwd_starts(CONV_A), ts, dm)
        _conv_wgrad_tile(ext_u, dvb_ref, gwb_v, _fwd_starts(CONV_B), ts, dm)

        dp = dp_v[...]
        dcg = dp * ha_ref[...].astype(F32)
        dha = dp * cg_ref[...].astype(F32)
        du = du_v[...]
        sg = _sigmoid(g_ref[...].astype(F32))
        da = du * sg
        dg = du * a_ref[...].astype(F32) * sg * (1.0 - sg)
        dproj_ref[:, 0:dm] = dpa_ref[:, 0:dm]
        dproj_ref[:, dm:2 * dm] = dcg.astype(BF16)
        dproj_ref[:, 2 * dm:3 * dm] = dha.astype(BF16)
        dproj_ref[:, 3 * dm:4 * dm] = da.astype(BF16)
        dproj_ref[:, 4 * dm:5 * dm] = dg.astype(BF16)
        dproj_ref[:, 5 * dm:7 * dm] = dpa_ref[:, dm:3 * dm]
        small_ref[3:4, :] += _colsum(dcg)
        small_ref[4:5, :] += _colsum(dha)
        small_ref[5:6, :] += _colsum(da)
        small_ref[6:7, :] += _colsum(dg)

        @pl.when(i == n_steps - 1)
        def _():
            for k in range(CONV_A):
                small_ref[k:k + 1, :] = _colsum(gwa_v[k])
            for k in range(CONV_B):
                small_ref[SUBLANE + k:SUBLANE + k + 1, :] = _colsum(gwb_v[k])

    ext = pltpu.VMEM((ts + 2 * HALO, dm), F32)
    return pl.pallas_call(
        _after(body, deps), name="bwd_conv", grid=(n_steps,),
        in_specs=([_whole()] * len(deps) + _halo_specs(ts, dm, n_tok) * 4
                  + [_rows(ts, dm, 1), _rows(ts, dm, 2), _rows(ts, dm, 3), _rows(ts, dm, 4), _rows(ts, 3 * dm)]
                  + [_whole()] * 2),
        out_specs=[_rows(ts, 7 * dm), pl.BlockSpec((small_rows, dm), lambda i: (0, 0))],
        out_shape=[jax.ShapeDtypeStruct((n_tok, 7 * dm), BF16), jax.ShapeDtypeStruct((small_rows, dm), F32)],
        scratch_shapes=[ext, ext, ext, ext, pltpu.VMEM((ts, dm), F32), pltpu.VMEM((ts, dm), F32),
                        pltpu.VMEM((CONV_A, SUBLANE, dm), F32), pltpu.VMEM((CONV_B, SUBLANE, dm), F32),
                        pltpu.VMEM((CONV_A, SUBLANE, dm), F32), pltpu.VMEM((CONV_B, SUBLANE, dm), F32)],
        compiler_params=_params(48),
    )(*deps, dva, dva, dva, dvb, dvb, dvb, p, p, p, u, u, u, proj, proj, proj, proj, dpa, caw, cbw)


def _bwd_in(dproj, x, dx1, g1, slab, ts, deps):
    n_tok, dm = x.shape
    rows, off, _ = _layout(dm)
    width = 7 * dm

    def body(dproj_ref, x_ref, dx1_ref, g1_ref, slab_ref, gx_ref, small_ref, w_v, sems):
        @pl.when(pl.program_id(0) == 0)
        def _():
            cps = _load_weight(slab_ref, off["win"], rows["win"], w_v, sems, 0)
            small_ref[...] = jnp.zeros_like(small_ref)
            for cp in cps:
                cp.wait()

        dh = _dot(dproj_ref[...], w_v[...])
        xv = x_ref[...]
        r1 = _rms(xv)
        small_ref[0:1, :] += _colsum(dh * xv * r1)
        gx_ref[...] = dx1_ref[...] + _rms_bwd(dh, xv, r1, g1_ref[...])

    return pl.pallas_call(
        _after(body, deps), name="bwd_in", grid=(n_tok // ts,),
        in_specs=[_whole()] * len(deps) + [_rows(ts, width), _rows(ts, dm), _rows(ts, dm), _whole(), _hbm()],
        out_specs=[_rows(ts, dm), pl.BlockSpec((SUBLANE, dm), lambda i: (0, 0))],
        out_shape=[jax.ShapeDtypeStruct((n_tok, dm), F32), jax.ShapeDtypeStruct((SUBLANE, dm), F32)],
        scratch_shapes=[pltpu.VMEM((width, dm), BF16), pltpu.SemaphoreType.DMA((N_DEV,))],
        compiler_params=_params(40),
    )(*deps, dproj, x, dx1, g1, slab)


def _wgrad(a, b, name, tm, tk, out_dtype):
    n_tok, m = a.shape
    n = b.shape[1]
    k_steps = n_tok // tk

    def body(a_ref, b_ref, o_ref, acc_v):
        k = pl.program_id(1)

        @pl.when(k == 0)
        def _():
            acc_v[...] = jnp.zeros_like(acc_v)

        acc_v[...] += _dot_tn(a_ref[...], b_ref[...])

        @pl.when(k == k_steps - 1)
        def _():
            o_ref[...] = acc_v[...].astype(o_ref.dtype)

    return pl.pallas_call(
        body, name=name, grid=(m // tm, k_steps),
        in_specs=[pl.BlockSpec((tk, tm), lambda i, k: (k, i)), pl.BlockSpec((tk, n), lambda i, k: (k, 0))],
        out_specs=pl.BlockSpec((tm, n), lambda i, k: (i, 0)),
        out_shape=jax.ShapeDtypeStruct((m, n), out_dtype),
        scratch_shapes=[pltpu.VMEM((tm, n), F32)],
        compiler_params=pltpu.CompilerParams(dimension_semantics=("arbitrary", "arbitrary"),
                                             vmem_limit_bytes=40 * MIB),
    )(a, b)


def _adamw(w, g, m, v):
    m = ADAM_B1 * m + (1.0 - ADAM_B1) * g
    v = ADAM_B2 * v + (1.0 - ADAM_B2) * (g * g)
    m_hat = m / (1.0 - ADAM_B1 ** ADAM_STEP)
    v_hat = v / (1.0 - ADAM_B2 ** ADAM_STEP)
    delta = -ADAM_LR * (m_hat / (jnp.sqrt(v_hat) + ADAM_EPS) + ADAM_WD * w)
    return delta, m, v


def _adam_big(recv, part, me, off, rows, w, m, v, transpose, name, tr):
    dm = recv.shape[2]
    per = rows // tr

    def body(me_ref, own_ref, r_ref, w_ref, m_ref, v_ref, g_ref, d_ref, mo_ref, vo_ref):
        g = own_ref[...].astype(F32)
        for k in range(len(FLIPS)):
            g = g + r_ref[k].astype(F32)
        if transpose:
            g = g.T
        delta, m_new, v_new = _adamw(w_ref[...], g, m_ref[...], v_ref[...])
        g_ref[...] = g
        d_ref[...] = delta
        mo_ref[...] = m_new
        vo_ref[...] = v_new

    if transpose:
        blk = pl.BlockSpec((dm, tr), lambda i, me_ref: (0, i))
    else:
        blk = pl.BlockSpec((tr, dm), lambda i, me_ref: (i, 0))
    first = off // tr
    return pl.pallas_call(
        body, name=name,
        grid_spec=pltpu.PrefetchScalarGridSpec(
            num_scalar_prefetch=1, grid=(per,),
            in_specs=[pl.BlockSpec((tr, dm), lambda i, me_ref: (me_ref[0] * per + i, 0)),
                      pl.BlockSpec((len(FLIPS), tr, dm), lambda i, me_ref: (0, first + i, 0)), blk, blk, blk],
            out_specs=[blk] * 4),
        out_shape=[jax.ShapeDtypeStruct(w.shape, F32)] * 4,
        compiler_params=_params(32),
    )(me, part, recv, w, m, v)


LOSS_ROW = 15
CONV_A_ROW = 16
CONV_B_ROW = 24


def _adam_small(recv_small, recv_last, me, params, d_model):
    n = len(params)
    cw = d_model // N_DEV

    def body(me_ref, r_ref, rc_ref, l_ref, *refs):
        ins, loss_ref, outs = refs[:3 * n], refs[3 * n], refs[3 * n + 1:3 * n + 1 + 4 * n]
        g_v, gc_v, last_v = refs[3 * n + 1 + 4 * n:]
        g, gc, last = r_ref[0], rc_ref[0], l_ref[0]
        for d in range(1, N_DEV):
            g, gc, last = g + r_ref[d], gc + rc_ref[d], last + l_ref[d]
        g_v[...], gc_v[...], last_v[...] = g, gc, last
        loss_ref[...] = (0.5 / d_model) * jnp.sum(g_v[LOSS_ROW:LOSS_ROW + 1, :], axis=-1, keepdims=True)
        for j, (row0, own_columns, (w, _, _)) in enumerate(params):
            w_ref, m_ref, v_ref = ins[3 * j:3 * j + 3]
            source = gc_v if own_columns else (last_v if row0 == 0 else g_v)
            grad = source[row0:row0 + w.shape[0], :]
            delta, m_new, v_new = _adamw(w_ref[...], grad, m_ref[...], v_ref[...])
            for ref, val in zip(outs[4 * j:4 * j + 4], (grad, delta, m_new, v_new)):
                ref[...] = val

    full = lambda shape: pl.BlockSpec(shape, lambda i, me_ref: (0,) * len(shape))
    stack_rows = recv_small.shape[1]
    flat = [a for _, _, triple in params for a in triple]
    shapes = [w.shape for _, _, (w, _, _) in params for _ in range(4)]
    out = pl.pallas_call(
        body, name="adam_small",
        grid_spec=pltpu.PrefetchScalarGridSpec(
            num_scalar_prefetch=1, grid=(1,),
            in_specs=[full(recv_small.shape),
                      pl.BlockSpec((N_DEV, stack_rows, cw), lambda i, me_ref: (0, 0, me_ref[0])),
                      full(recv_last.shape)] + [full(a.shape) for a in flat],
            out_specs=[full((1, 1))] + [full(s) for s in shapes],
            scratch_shapes=[pltpu.VMEM((stack_rows, d_model), F32), pltpu.VMEM((stack_rows, cw), F32),
                            pltpu.VMEM(recv_last.shape[1:], F32)]),
        out_shape=[jax.ShapeDtypeStruct((1, 1), F32)] + [jax.ShapeDtypeStruct(s, F32) for s in shapes],
    )(me, recv_small, recv_small, recv_last, *flat)
    return out[0], [tuple(out[1 + 4 * j:5 + 4 * j]) for j in range(n)]


def _tile(n_tok, want):
    return min(want, n_tok)


def kernel(x, norm1_pre_g, w_in, b_in, conv_a_w, conv_a_b, w_a_out, conv_b_w, conv_b_b, ln_b_g, ln_b_b, w_b_out, w_o, norm1_post_g, norm2_pre_g, w_mlp_in, w_mlp_out, norm2_post_g, loss_target, m_norm1_pre_g, m_w_in, m_b_in, m_conv_a_w, m_conv_a_b, m_w_a_out, m_conv_b_w, m_conv_b_b, m_ln_b_g, m_ln_b_b, m_w_b_out, m_w_o, m_norm1_post_g, m_norm2_pre_g, m_w_mlp_in, m_w_mlp_out, m_norm2_post_g, v_norm1_pre_g, v_w_in, v_b_in, v_conv_a_w, v_conv_a_b, v_w_a_out, v_conv_b_w, v_conv_b_b, v_ln_b_g, v_ln_b_b, v_w_b_out, v_w_o, v_norm1_post_g, v_norm2_pre_g, v_w_mlp_in, v_w_mlp_out, v_norm2_post_g):
    n_tok, dm = x.shape[1], x.shape[2]
    rows, off, slab_rows = _layout(dm)
    cw = dm // N_DEV
    xs = x.reshape(n_tok, dm)
    tgt = loss_target.reshape(n_tok, dm)
    row = lambda vec: vec.reshape(1, -1)
    gathered = lambda group: (N_DEV, slab_rows[group], dm)
    scattered = lambda group: jax.ShapeDtypeStruct((len(FLIPS), slab_rows[group], dm), BF16)
    tm, tk = min(dm, 1024), _tile(n_tok, 2048)
    me = (4 * lax.axis_index("x") + 2 * lax.axis_index("y") + lax.axis_index("c")).astype(jnp.int32).reshape(1)

    conv_own = jnp.concatenate([conv_a_w, jnp.zeros((SUBLANE - CONV_A, cw), F32), conv_b_w,
                                jnp.zeros((1, cw), F32)], axis=0)
    own_in = w_in.T.astype(BF16)
    slab_in, conv_all = _all_gather_two_level(
        [own_in, conv_own], [_place_own(own_in, gathered("in"), me, "place_w_in"), None], "gather_w_in")
    conv_full = conv_all.transpose(1, 0, 2).reshape(conv_own.shape[0], dm)
    caw, cbw = conv_full[0:CONV_A], conv_full[SUBLANE:SUBLANE + CONV_B]
    own_abo = jnp.concatenate([w_a_out, w_b_out, w_o], axis=0).astype(BF16)
    own_mlp = jnp.concatenate([w_mlp_in.T, w_mlp_out], axis=0).astype(BF16)
    ag_abo = _exchange_start([_Part(own_abo, False, slab_rows["abo"], 0, 0)],
                             [_place_own(own_abo, gathered("abo"), me, "place_abo")],
                             "gather_abo_start", after=slab_in)
    ag_mlp = _exchange_start([_Part(own_mlp, False, slab_rows["mlp"], 0, 0)],
                             [_place_own(own_mlp, gathered("mlp"), me, "place_mlp")],
                             "gather_mlp_start", after=ag_abo.token)

    proj, p, u, h = _fwd_in(xs, row(norm1_pre_g), row(b_in), slab_in, _tile(n_tok, 256),
                            [ag_abo.token, ag_mlp.token])
    _, (slab_abo,) = _exchange_wait(ag_abo, "gather_abo_wait", after=proj)
    vb, ya, yb, qa, sb, merged, mix, x1 = _fwd_mix(
        p, u, proj, xs, caw, row(conv_a_b), cbw, row(conv_b_b), row(ln_b_g), row(ln_b_b), row(norm1_post_g),
        slab_abo, _tile(n_tok, 256))
    _, (slab_mlp,) = _exchange_wait(ag_mlp, "gather_mlp_wait", after=x1)
    f, df1, h2, df2, dmix, dx1, small_mlp = _mlp_fwd_bwd(
        x1, mix, tgt, row(norm1_post_g), row(norm2_pre_g), row(norm2_post_g), slab_mlp, _tile(n_tok, 256))

    rs_mlp = _exchange_start(
        [_Part(_wgrad(df1, h2, "wgrad_mlp_in", tm, tk, BF16), True, rows["w1"], 0, off["w1"]),
         _Part(_wgrad(f, df2, "wgrad_mlp_out", tm, tk, BF16), True, rows["w2"], 0, off["w2"])],
        [scattered("mlp")], "scatter_mlp_start")
    dpa, dya, dyb, dva, dvb, small_mix = _bwd_mix(
        dmix, ya, yb, proj, p, vb, caw, row(conv_a_b), row(ln_b_g), row(ln_b_b), slab_abo, _tile(n_tok, 256),
        [rs_mlp.token])
    rs_abo = _exchange_start(
        [_Part(_wgrad(qa, dya, "wgrad_a_out", tm, tk, BF16), True, rows["wa"], 0, off["wa"]),
         _Part(_wgrad(sb, dyb, "wgrad_b_out", tm, tk, BF16), True, rows["wb"], 0, off["wb"]),
         _Part(_wgrad(merged, dmix, "wgrad_o", tm, tk, BF16), True, rows["wo"], 0, off["wo"])],
        [scattered("abo")], "scatter_abo_start")
    dproj, small_conv = _bwd_conv(dva, dvb, p, u, proj, dpa, caw, cbw, _tile(n_tok, 256), [rs_abo.token])

    zeros = lambda r: jnp.zeros((r, dm), F32)
    small = jnp.concatenate([
        zeros(1),
        small_mix[2:3],
        small_mix[3:4],
        small_mix[0:2],
        small_mlp[2:3],
        small_mlp[1:2],
        small_mlp[0:1],
        small_mix[4:5], small_conv[3:7], small_mix[5:7],
        small_mlp[3:4],
        small_conv[0:CONV_A], zeros(SUBLANE - CONV_A),
        small_conv[8:8 + CONV_B], zeros(1),
    ], axis=0)

    rs_in = _exchange_start(
        [_Part(_wgrad(dproj, h, "wgrad_in", tm, tk, BF16), True, rows["win"], 0, off["win"]),
         _Part(small, False, small.shape[0], 1, 0)],
        [scattered("in"), _place_own(small, (N_DEV,) + small.shape, me, "place_small")], "scatter_in_start")
    grad_x, small_in = _bwd_in(dproj, xs, dx1, row(norm1_pre_g), slab_in, _tile(n_tok, 256), [rs_in.token])

    tr = min(LANE, rows["wa"])
    (g_w1, g_w2), (recv_mlp,) = _exchange_wait(rs_mlp, "scatter_mlp_wait", after=grad_x)
    (g_wa, g_wb, g_wo), (recv_abo,) = _exchange_wait(rs_abo, "scatter_abo_wait", after=grad_x)
    big = {
        "w_mlp_in": _adam_big(recv_mlp, g_w1, me, off["w1"], rows["w1"], w_mlp_in, m_w_mlp_in, v_w_mlp_in, True,
                              "adam_w_mlp_in", tr),
        "w_mlp_out": _adam_big(recv_mlp, g_w2, me, off["w2"], rows["w2"], w_mlp_out, m_w_mlp_out, v_w_mlp_out,
                               False, "adam_w_mlp_out", tr),
        "w_a_out": _adam_big(recv_abo, g_wa, me, off["wa"], rows["wa"], w_a_out, m_w_a_out, v_w_a_out, False,
                             "adam_w_a_out", tr),
        "w_b_out": _adam_big(recv_abo, g_wb, me, off["wb"], rows["wb"], w_b_out, m_w_b_out, v_w_b_out, False,
                             "adam_w_b_out", tr),
        "w_o": _adam_big(recv_abo, g_wo, me, off["wo"], rows["wo"], w_o, m_w_o, v_w_o, False, "adam_w_o", tr),
    }
    (g_win, _), (recv_in, recv_small) = _exchange_wait(rs_in, "scatter_in_wait", after=big["w_o"][3])
    recv_last, = _all_gather([small_in], "gather_last")
    big["w_in"] = _adam_big(recv_in, g_win, me, off["win"], rows["win"], w_in, m_w_in, v_w_in, True, "adam_w_in", tr)

    small_names = ("norm1_pre_g", "conv_a_b", "conv_b_b", "ln_b_g", "ln_b_b", "norm1_post_g", "norm2_pre_g",
                   "norm2_post_g")
    given = dict(
        norm1_pre_g=(norm1_pre_g, m_norm1_pre_g, v_norm1_pre_g), conv_a_b=(conv_a_b, m_conv_a_b, v_conv_a_b),
        conv_b_b=(conv_b_b, m_conv_b_b, v_conv_b_b), ln_b_g=(ln_b_g, m_ln_b_g, v_ln_b_g),
        ln_b_b=(ln_b_b, m_ln_b_b, v_ln_b_b), norm1_post_g=(norm1_post_g, m_norm1_post_g, v_norm1_post_g),
        norm2_pre_g=(norm2_pre_g, m_norm2_pre_g, v_norm2_pre_g),
        norm2_post_g=(norm2_post_g, m_norm2_post_g, v_norm2_post_g))
    params = [(j, False, tuple(row(a) for a in given[name])) for j, name in enumerate(small_names)]
    params.append((SUBLANE, False, tuple(a.reshape(7, dm) for a in (b_in, m_b_in, v_b_in))))
    params.append((CONV_A_ROW, True, (conv_a_w, m_conv_a_w, v_conv_a_w)))
    params.append((CONV_B_ROW, True, (conv_b_w, m_conv_b_w, v_conv_b_w)))
    loss, small_out = _adam_small(recv_small, recv_last, me, params, dm)
    small_leaves = {name: tuple(a.reshape(dm) for a in small_out[j]) for j, name in enumerate(small_names)}
    small_leaves["b_in"] = tuple(a.reshape(7 * dm) for a in small_out[len(small_names)])
    small_leaves["conv_a_w"] = small_out[len(small_names) + 1]
    small_leaves["conv_b_w"] = small_out[len(small_names) + 2]

    order = ("norm1_pre_g", "w_in", "b_in", "conv_a_w", "conv_a_b", "w_a_out", "conv_b_w", "conv_b_b", "ln_b_g",
             "ln_b_b", "w_b_out", "w_o", "norm1_post_g", "norm2_pre_g", "w_mlp_in", "w_mlp_out", "norm2_post_g")
    leaves = [big[name] if name in big else small_leaves[name] for name in order]
    grads, deltas, new_m, new_v = zip(*leaves)
    return (loss.reshape(()), grad_x.reshape(x.shape), *grads, *deltas, *new_m, *new_v)
```

```python
from typing import NamedTuple

import jax
import jax.numpy as jnp
from jax import lax
from jax.experimental import pallas as pl
from jax.experimental.pallas import tpu as pltpu

F32 = jnp.float32
BF16 = jnp.bfloat16

RMS_EPS = 1e-6
LN_EPS = 1e-5
ADAM_LR = 0.001
ADAM_B1 = 0.9
ADAM_B2 = 0.999
ADAM_EPS = 1e-08
ADAM_WD = 0.01
ADAM_STEP = 10

N_DEV = 8
CONV_A = 3
CONV_B = 31
LANE = 128
SUBLANE = 8
HALO = 16
CONV_ROWS = 64
MIB = 1 << 20
FLIPS = ((0, 0, 1), (0, 1, 0), (1, 0, 0), (0, 1, 1), (1, 0, 1), (1, 1, 0), (1, 1, 1))
MESH = pl.DeviceIdType.MESH


def _layout(d_model):
    e = d_model // N_DEV
    rows = {"win": 7 * e, "w1": 4 * e, "w2": 4 * e, "wa": e, "wb": e, "wo": e}
    off = {"win": 0, "w1": 0, "w2": 4 * e, "wa": 0, "wb": e, "wo": 2 * e}
    return rows, off, {"in": 7 * e, "mlp": 8 * e, "abo": 3 * e}


def _after(body, deps):
    def ordered(*refs):
        return body(*refs[len(deps):])
    return ordered


def _params(vmem_mib):
    return pltpu.CompilerParams(dimension_semantics=("arbitrary",), vmem_limit_bytes=vmem_mib * MIB)


def _whole():
    return pl.BlockSpec(memory_space=pltpu.VMEM)


def _hbm():
    return pl.BlockSpec(memory_space=pl.ANY)


def _rows(ts, width, col=0):
    return pl.BlockSpec((ts, width), lambda i: (i, col))


def _halo_specs(ts, width, n_rows):
    per = ts // HALO
    last = n_rows // HALO - 1
    return [
        pl.BlockSpec((ts, width), lambda i: (i, 0)),
        pl.BlockSpec((HALO, width), lambda i: (jnp.maximum(i * per - 1, 0), 0)),
        pl.BlockSpec((HALO, width), lambda i: (jnp.minimum((i + 1) * per, last), 0)),
    ]


def _dot(a, b):
    return jnp.dot(a, b, preferred_element_type=F32)


def _dot_nt(a, b):
    return lax.dot_general(a, b, (((1,), (1,)), ((), ())), preferred_element_type=F32)


def _dot_tn(a, b):
    return lax.dot_general(a, b, (((0,), (0,)), ((), ())), preferred_element_type=F32)


def _rms(u):
    return lax.rsqrt(jnp.mean(u * u, axis=-1, keepdims=True) + RMS_EPS)


def _rms_bwd(dz, u, r, g):
    dzg = dz * g
    return r * dzg - u * (r * r * r) * jnp.mean(dzg * u, axis=-1, keepdims=True)


def _colsum(v):
    return jnp.sum(v, axis=0, keepdims=True)


def _sigmoid(v):
    return jax.nn.sigmoid(v)


def _weight_copies(slab_ref, off, rows, dst_ref, sems, first_sem):
    return [pltpu.make_async_copy(slab_ref.at[d, pl.ds(off, rows), :], dst_ref.at[pl.ds(d * rows, rows), :],
                                  sems.at[first_sem + d]) for d in range(N_DEV)]


def _on_first_step(copies, method):
    @pl.when(pl.program_id(0) == 0)
    def _():
        for cp in copies:
            getattr(cp, method)()


def _fill_ext(ext_ref, main_ref, prev_ref, next_ref, i, n_steps, ts):
    ext_ref[0:HALO, :] = jnp.where(i > 0, prev_ref[...], 0.0)
    ext_ref[HALO:HALO + ts, :] = main_ref[...]
    ext_ref[HALO + ts:HALO + ts + HALO, :] = jnp.where(i < n_steps - 1, next_ref[...], 0.0)


def _broadcast_taps(w_ref, wb_ref, n_taps):
    for k in range(n_taps):
        wb_ref[k] = jnp.broadcast_to(w_ref[k:k + 1, :], wb_ref.shape[1:])


def _phases(starts):
    groups = {}
    for k, st in enumerate(starts):
        groups.setdefault(st % SUBLANE, []).append((k, st // SUBLANE))
    return sorted(groups.items())


def _shifted(blk, b):
    n = blk.shape[0]
    rolled = blk if b == 0 else pltpu.roll(blk, n - b, axis=0)
    return rolled.reshape(n // SUBLANE, SUBLANE, blk.shape[1])


def _conv_tile(ext_ref, wb_ref, starts, ts, width, emit):
    span = CONV_ROWS + 2 * HALO
    groups = _phases(starts)
    nv = CONV_ROWS // SUBLANE

    def row_block(rb, carry):
        r0 = pl.multiple_of(rb * CONV_ROWS, CONV_ROWS)
        for cb in range(width // LANE):
            lanes = pl.ds(cb * LANE, LANE)
            blk = ext_ref[pl.ds(r0, span), lanes]
            acc = jnp.zeros((nv, SUBLANE, LANE), F32)
            for b, taps in groups:
                sh = _shifted(blk, b)
                for k, m in taps:
                    acc = acc + sh[m:m + nv] * wb_ref[k, :, lanes][None]
            emit(r0, lanes, acc.reshape(CONV_ROWS, LANE))
        return carry

    lax.fori_loop(0, ts // CONV_ROWS, row_block, 0)


def _conv_wgrad_tile(ext_ref, dv_ref, acc_ref, starts, ts, width):
    span = CONV_ROWS + 2 * HALO
    groups = _phases(starts)
    nv = CONV_ROWS // SUBLANE

    def row_block(rb, carry):
        r0 = pl.multiple_of(rb * CONV_ROWS, CONV_ROWS)
        for cb in range(width // LANE):
            lanes = pl.ds(cb * LANE, LANE)
            blk = ext_ref[pl.ds(r0, span), lanes]
            dv = dv_ref[pl.ds(r0, CONV_ROWS), lanes].reshape(nv, SUBLANE, LANE)
            for b, taps in groups:
                sh = _shifted(blk, b)
                for k, m in taps:
                    acc_ref[k, :, lanes] += jnp.sum(sh[m:m + nv] * dv, axis=0)
        return carry

    lax.fori_loop(0, ts // CONV_ROWS, row_block, 0)


def _fwd_starts(n_taps):
    pad = (n_taps - 1) // 2
    return [HALO - pad + k for k in range(n_taps)]


def _bwd_starts(n_taps):
    pad = (n_taps - 1) // 2
    return [HALO + pad - k for k in range(n_taps)]


def _peer(x, y, c, flip):
    fx, fy, fc = flip
    return (1 - x if fx else x, 1 - y if fy else y, 1 - c if fc else c)


def _all_gather(shards, name):
    n = len(shards)

    def body(*refs):
        ins, outs = refs[:n], refs[n:2 * n]
        send_sems, recv_sems, local_sems = refs[2 * n:]
        x, y, c = lax.axis_index("x"), lax.axis_index("y"), lax.axis_index("c")
        me = 4 * x + 2 * y + c
        local = [pltpu.make_async_copy(ins[j], outs[j].at[me], local_sems.at[j]) for j in range(n)]
        for cp in local:
            cp.start()
        sends, recvs = [], []
        for k, flip in enumerate(FLIPS):
            px, py, pc = _peer(x, y, c, flip)
            peer = 4 * px + 2 * py + pc
            for j in range(n):
                sem = k * n + j
                sends.append(pltpu.make_async_remote_copy(
                    src_ref=ins[j], dst_ref=outs[j].at[me], send_sem=send_sems.at[sem], recv_sem=recv_sems.at[sem],
                    device_id=(px, py, pc), device_id_type=MESH))
                recvs.append(pltpu.make_async_remote_copy(
                    src_ref=ins[j], dst_ref=outs[j].at[peer], send_sem=send_sems.at[sem], recv_sem=recv_sems.at[sem],
                    device_id=(px, py, pc), device_id_type=MESH))
        for cp in sends:
            cp.start()
        for cp in recvs:
            cp.wait_recv()
        for cp in sends:
            cp.wait_send()
        for cp in local:
            cp.wait()

    return pl.pallas_call(
        body, name=name,
        out_shape=[jax.ShapeDtypeStruct((N_DEV,) + s.shape, s.dtype) for s in shards],
        in_specs=[_hbm()] * n, out_specs=[_hbm()] * n,
        scratch_shapes=[pltpu.SemaphoreType.DMA((7 * n,)), pltpu.SemaphoreType.DMA((7 * n,)),
                        pltpu.SemaphoreType.DMA((n,))],
    )(*shards)


def _place_own(src, n_slots_shape, me, name):
    rows, width = src.shape
    tr = next(t for t in (256, 128, 64, 32, 16, SUBLANE) if rows % t == 0)

    def body(me_ref, src_ref, out_ref):
        out_ref[...] = src_ref[...]

    return pl.pallas_call(
        body, name=name,
        grid_spec=pltpu.PrefetchScalarGridSpec(
            num_scalar_prefetch=1, grid=(rows // tr,),
            in_specs=[pl.BlockSpec((tr, width), lambda i, me_ref: (i, 0))],
            out_specs=pl.BlockSpec((None, tr, width), lambda i, me_ref: (me_ref[0], i, 0))),
        out_shape=jax.ShapeDtypeStruct(n_slots_shape, src.dtype),
    )(me, src)


def _all_gather_two_level(shards, placed, name):
    n = len(shards)
    given = [j for j in range(n) if placed[j] is not None]

    def body(*refs):
        ins, outs = refs[:n], refs[n + len(given):2 * n + len(given)]
        send_sems, recv_sems, local_sems = refs[2 * n + len(given):]
        x, y, c = lax.axis_index("x"), lax.axis_index("y"), lax.axis_index("c")
        me, sibling = (x, y, c), (x, y, 1 - c)
        chips = [(1 - x, y), (x, 1 - y), (1 - x, 1 - y)]

        def slot(j, dev):
            return outs[j].at[4 * dev[0] + 2 * dev[1] + dev[2]]

        def copy(k, j, block, to, src=None):
            return pltpu.make_async_remote_copy(
                src_ref=slot(j, block) if src is None else src, dst_ref=slot(j, block),
                send_sem=send_sems.at[k * n + j], recv_sem=recv_sems.at[k * n + j], device_id=to, device_id_type=MESH)

        local = [pltpu.make_async_copy(ins[j], slot(j, me), local_sems.at[j]) for j in range(n) if j not in given]
        for cp in local:
            cp.start()
        first = [copy(0, j, me, sibling, src=ins[j]) for j in range(n)]
        first += [copy(1 + t, j, me, (*chip, c), src=ins[j]) for t, chip in enumerate(chips) for j in range(n)]
        for cp in first:
            cp.start()
        passed = []
        for t, chip in enumerate(chips):
            for j in range(n):
                copy(1 + t, j, (*chip, c), me).wait_recv()
                passed.append(copy(4 + t, j, (*chip, c), sibling))
                passed[-1].start()
        for j in range(n):
            copy(0, j, sibling, me).wait_recv()
        for t, chip in enumerate(chips):
            for j in range(n):
                copy(4 + t, j, (*chip, 1 - c), me).wait_recv()
        for cp in first + passed:
            cp.wait_send()
        for cp in local:
            cp.wait()

    return pl.pallas_call(
        body, name=name,
        out_shape=[jax.ShapeDtypeStruct((N_DEV,) + s.shape, s.dtype) for s in shards],
        in_specs=[_hbm()] * (n + len(given)), out_specs=[_hbm()] * n,
        input_output_aliases={n + i: j for i, j in enumerate(given)},
        scratch_shapes=[pltpu.SemaphoreType.DMA((7 * n,)), pltpu.SemaphoreType.DMA((7 * n,)),
                        pltpu.SemaphoreType.DMA((n,))],
    )(*shards, *[placed[j] for j in given])


class _Part(NamedTuple):
    src: jax.Array
    scatter: bool
    rows: int
    land: int
    off: int


class _Started(NamedTuple):
    send_sems: jax.Array
    recv_sems: jax.Array
    thru: tuple
    token: jax.Array
    parts: tuple


def _exchange_copies(srcs, lands, send_sems, recv_sems, parts):
    n = len(parts)
    x, y, c = lax.axis_index("x"), lax.axis_index("y"), lax.axis_index("c")
    me = 4 * x + 2 * y + c

    def block(j, dev):
        p = parts[j]
        return srcs[j].at[pl.ds(pl.multiple_of(dev * p.rows, SUBLANE), p.rows), :] if p.scatter else srcs[j]

    def slot(j, index):
        p = parts[j]
        return lands[p.land].at[index, pl.ds(p.off, p.rows), :]

    sends, recvs = [], []
    for k, flip in enumerate(FLIPS):
        px, py, pc = _peer(x, y, c, flip)
        peer = 4 * px + 2 * py + pc
        for j in range(n):
            sems = dict(send_sem=send_sems.at[k * n + j], recv_sem=recv_sems.at[k * n + j],
                        device_id=(px, py, pc), device_id_type=MESH)
            to, got = (k, k) if parts[j].scatter else (me, peer)
            sends.append(pltpu.make_async_remote_copy(src_ref=block(j, peer), dst_ref=slot(j, to), **sems))
            recvs.append(pltpu.make_async_remote_copy(src_ref=block(j, peer), dst_ref=slot(j, got), **sems))
    return sends, recvs


def _exchange_start(parts, lands, name, after=None):
    n, nl = len(parts), len(lands)
    n_in = n + nl + (after is not None)

    def body(*refs):
        srcs, land_refs = refs[:n], refs[n:n + nl]
        send_sems, recv_sems = refs[n_in], refs[n_in + 1]
        token = refs[n_in + 2 + n + nl]
        sends, _ = _exchange_copies(srcs, land_refs, send_sems, recv_sems, parts)
        for cp in sends:
            cp.start()
        token[...] = jnp.zeros_like(token)

    hbm = pl.BlockSpec(memory_space=pltpu.HBM)
    sem = pl.BlockSpec(memory_space=pltpu.SEMAPHORE)
    fresh = lambda s: lax.empty(s.shape, s.dtype) if isinstance(s, jax.ShapeDtypeStruct) else s
    args = [pltpu.with_memory_space_constraint(p.src, pltpu.HBM) for p in parts]
    args += [pltpu.with_memory_space_constraint(fresh(s), pltpu.HBM) for s in lands]
    args += [] if after is None else [after]
    out = pl.pallas_call(
        body, name=name,
        out_shape=(pltpu.SemaphoreType.DMA((7 * n,)), pltpu.SemaphoreType.DMA((7 * n,)),
                   *[pltpu.HBM(a.shape, a.dtype) for a in args[:n + nl]], jax.ShapeDtypeStruct((SUBLANE, LANE), F32)),
        in_specs=[hbm] * (n + nl) + [_hbm()] * (after is not None),
        out_specs=(sem, sem, *[hbm] * (n + nl), _whole()),
        input_output_aliases={j: 2 + j for j in range(n + nl)},
        compiler_params=pltpu.CompilerParams(has_side_effects=pltpu.SideEffectType.DATAFLOW_SIDE_EFFECTING),
    )(*args)
    return _Started(out[0], out[1], tuple(out[2:2 + n + nl]), out[2 + n + nl], tuple(parts))


def _exchange_wait(started, name, after):
    parts = started.parts
    n, nl = len(parts), len(started.thru) - len(parts)

    def body(*refs):
        srcs, land_refs = refs[:n], refs[n:n + nl]
        send_sems, recv_sems = refs[n + nl], refs[n + nl + 1]
        sends, recvs = _exchange_copies(srcs, land_refs, send_sems, recv_sems, parts)
        for cp in sends:
            cp.wait_send()
        for cp in recvs:
            cp.wait_recv()

    hbm = pl.BlockSpec(memory_space=pltpu.HBM)
    sem = pl.BlockSpec(memory_space=pltpu.SEMAPHORE)
    out = pl.pallas_call(
        body, name=name,
        out_shape=tuple(pltpu.HBM(a.shape, a.dtype) for a in started.thru),
        in_specs=[hbm] * (n + nl) + [sem, sem, _hbm()], out_specs=tuple([hbm] * (n + nl)),
        input_output_aliases={j: j for j in range(n + nl)},
        compiler_params=pltpu.CompilerParams(has_side_effects=pltpu.SideEffectType.DATAFLOW_SIDE_EFFECTING),
    )(*started.thru, started.send_sems, started.recv_sems, after)
    return list(out[:n]), list(out[n:])


def _fwd_in(x, g1, b_in, slab, ts, deps):
    n_tok, dm = x.shape
    rows, off, _ = _layout(dm)
    width = 7 * dm

    def body(x_ref, g1_ref, b_ref, slab_ref, proj_ref, p_ref, u_ref, h_ref, w_v, sems):
        copies = _weight_copies(slab_ref, off["win"], rows["win"], w_v, sems, 0)
        _on_first_step(copies, "start")
        xv = x_ref[...]
        h = (xv * _rms(xv) * g1_ref[...]).astype(BF16)
        h_ref[...] = h
        cols = []
        waited = 0
        for j in range(7):
            need = min(N_DEV, -(-(j + 1) * dm // rows["win"]))
            _on_first_step(copies[waited:need], "wait")
            waited = need
            pj = _dot_nt(h, w_v[pl.ds(j * dm, dm), :]) + b_ref[:, j * dm:(j + 1) * dm]
            proj_ref[:, j * dm:(j + 1) * dm] = pj.astype(proj_ref.dtype)
            if 1 <= j <= 4:
                cols.append(pj)
            if j == 2:
                p_ref[...] = cols[0] * cols[1]
            if j == 4:
                u_ref[...] = cols[2] * _sigmoid(cols[3])

    return pl.pallas_call(
        _after(body, deps), name="fwd_in", grid=(n_tok // ts,),
        in_specs=[_whole()] * len(deps) + [_rows(ts, dm), _whole(), _whole(), _hbm()],
        out_specs=[_rows(ts, width), _rows(ts, dm), _rows(ts, dm), _rows(ts, dm)],
        out_shape=[jax.ShapeDtypeStruct((n_tok, width), BF16), jax.ShapeDtypeStruct((n_tok, dm), F32),
                   jax.ShapeDtypeStruct((n_tok, dm), F32), jax.ShapeDtypeStruct((n_tok, dm), BF16)],
        scratch_shapes=[pltpu.VMEM((width, dm), BF16), pltpu.SemaphoreType.DMA((N_DEV,))],
        compiler_params=_params(56),
    )(*deps, x, g1, b_in, slab)


def _fwd_mix(p, u, proj, x, caw, cab, cbw, cbb, lng, lnb, g1post, slab, ts):
    n_tok, dm = x.shape
    rows, off, _ = _layout(dm)
    n_steps = n_tok // ts

    def body(p_ref, p_prev, p_next, u_ref, u_prev, u_next, bg_ref, za_ref, zb_ref, x_ref,
             caw_ref, cab_ref, cbw_ref, cbb_ref, lng_ref, lnb_ref, g1p_ref, slab_ref,
             va_ref, vb_ref, ya_ref, yb_ref, qa_ref, sb_ref, mg_ref, mix_ref, x1_ref,
             wa_v, wb_v, wo_v, ext_p, ext_u, tap_a, tap_b, sems):
        i = pl.program_id(0)
        copies = (_weight_copies(slab_ref, off["wa"], rows["wa"], wa_v, sems, 0)
                  + _weight_copies(slab_ref, off["wb"], rows["wb"], wb_v, sems, N_DEV)
                  + _weight_copies(slab_ref, off["wo"], rows["wo"], wo_v, sems, 2 * N_DEV))
        _on_first_step(copies, "start")

        @pl.when(i == 0)
        def _():
            _broadcast_taps(caw_ref, tap_a, CONV_A)
            _broadcast_taps(cbw_ref, tap_b, CONV_B)

        _fill_ext(ext_p, p_ref, p_prev, p_next, i, n_steps, ts)
        _fill_ext(ext_u, u_ref, u_prev, u_next, i, n_steps, ts)

        def emit_a(r0, lanes, acc):
            va_ref[pl.ds(r0, CONV_ROWS), lanes] = acc + cab_ref[:, lanes]

        def emit_b(r0, lanes, acc):
            vb_ref[pl.ds(r0, CONV_ROWS), lanes] = acc + cbb_ref[:, lanes]

        _conv_tile(ext_p, tap_a, _fwd_starts(CONV_A), ts, dm, emit_a)
        _conv_tile(ext_u, tap_b, _fwd_starts(CONV_B), ts, dm, emit_b)
        _on_first_step(copies, "wait")

        qa = (bg_ref[...].astype(F32) * va_ref[...]).astype(BF16)
        qa_ref[...] = qa
        ya = _dot(qa, wa_v[...])
        vb = vb_ref[...]
        xc = vb - jnp.mean(vb, axis=-1, keepdims=True)
        rstd = lax.rsqrt(jnp.mean(xc * xc, axis=-1, keepdims=True) + LN_EPS)
        ln = xc * rstd * lng_ref[...] + lnb_ref[...]
        sb = (ln * _sigmoid(ln)).astype(BF16)
        sb_ref[...] = sb
        yb = _dot(sb, wb_v[...])
        ya_ref[...] = ya.astype(BF16)
        yb_ref[...] = yb.astype(BF16)
        merged = (_sigmoid(za_ref[...].astype(F32)) * ya + _sigmoid(zb_ref[...].astype(F32)) * yb).astype(BF16)
        mg_ref[...] = merged
        mix = _dot(merged, wo_v[...])
        mix_ref[...] = mix
        x1_ref[...] = x_ref[...] + mix * _rms(mix) * g1p_ref[...]

    tok = lambda dt: jax.ShapeDtypeStruct((n_tok, dm), dt)
    return pl.pallas_call(
        body, name="fwd_mix", grid=(n_steps,),
        in_specs=(_halo_specs(ts, dm, n_tok) + _halo_specs(ts, dm, n_tok)
                  + [_rows(ts, dm, 0), _rows(ts, dm, 5), _rows(ts, dm, 6), _rows(ts, dm)]
                  + [_whole()] * 7 + [_hbm()]),
        out_specs=[_rows(ts, dm)] * 9,
        out_shape=[tok(F32), tok(F32), tok(BF16), tok(BF16), tok(BF16), tok(BF16), tok(BF16), tok(F32), tok(F32)],
        scratch_shapes=[pltpu.VMEM((dm, dm), BF16), pltpu.VMEM((dm, dm), BF16), pltpu.VMEM((dm, dm), BF16),
                        pltpu.VMEM((ts + 2 * HALO, dm), F32), pltpu.VMEM((ts + 2 * HALO, dm), F32),
                        pltpu.VMEM((CONV_A, SUBLANE, dm), F32), pltpu.VMEM((CONV_B, SUBLANE, dm), F32),
                        pltpu.SemaphoreType.DMA((3 * N_DEV,))],
        compiler_params=_params(48),
    )(p, p, p, u, u, u, proj, proj, proj, x, caw, cab, cbw, cbb, lng, lnb, g1post, slab)


def _mlp_fwd_bwd(x1, mix, tgt, g1post, g2pre, g2post, slab, ts):
    n_tok, dm = x1.shape
    rows, off, _ = _layout(dm)
    ff = 4 * dm

    def body(x1_ref, mix_ref, t_ref, g1p_ref, g2pre_ref, g2post_ref, slab_ref,
             f_ref, df1_ref, h2_ref, df2_ref, dmix_ref, dx1_ref, small_ref, w1_v, w2_v, relu_v, sems):
        w1_copies = _weight_copies(slab_ref, off["w1"], rows["w1"], w1_v, sems, 0)
        w2_copies = _weight_copies(slab_ref, off["w2"], rows["w2"], w2_v, sems, N_DEV)
        _on_first_step(w1_copies + w2_copies, "start")

        @pl.when(pl.program_id(0) == 0)
        def _():
            small_ref[...] = jnp.zeros_like(small_ref)

        x1v = x1_ref[...]
        r3 = _rms(x1v)
        g2pre = g2pre_ref[...]
        h2 = (x1v * r3 * g2pre).astype(BF16)
        h2_ref[...] = h2
        f2 = jnp.zeros((ts, dm), F32)
        per_chunk = N_DEV // 4
        for c in range(4):
            _on_first_step(w1_copies[c * per_chunk:(c + 1) * per_chunk]
                           + w2_copies[c * per_chunk:(c + 1) * per_chunk], "wait")
            blk = pl.ds(c * dm, dm)
            relu = jnp.maximum(_dot_nt(h2, w1_v[blk, :]), 0.0)
            relu_v[:, c * dm:(c + 1) * dm] = relu
            fc = (relu * relu).astype(BF16)
            f_ref[:, c * dm:(c + 1) * dm] = fc
            f2 = f2 + _dot(fc, w2_v[blk, :])
        r4 = _rms(f2)
        g2post = g2post_ref[...]
        err = x1v + f2 * r4 * g2post - t_ref[...]
        dy = err * (1.0 / dm)
        small_ref[3:4, :] += _colsum(err * err)
        small_ref[0:1, :] += _colsum(dy * f2 * r4)
        df2 = _rms_bwd(dy, f2, r4, g2post).astype(BF16)
        df2_ref[...] = df2
        dh2 = jnp.zeros((ts, dm), F32)
        for c in range(4):
            blk = pl.ds(c * dm, dm)
            df1 = (_dot_nt(df2, w2_v[blk, :]) * (2.0 * relu_v[:, c * dm:(c + 1) * dm])).astype(BF16)
            df1_ref[:, c * dm:(c + 1) * dm] = df1
            dh2 = dh2 + _dot(df1, w1_v[blk, :])
        small_ref[1:2, :] += _colsum(dh2 * x1v * r3)
        dx1 = dy + _rms_bwd(dh2, x1v, r3, g2pre)
        dx1_ref[...] = dx1
        mixv = mix_ref[...]
        r2 = _rms(mixv)
        small_ref[2:3, :] += _colsum(dx1 * mixv * r2)
        dmix_ref[...] = _rms_bwd(dx1, mixv, r2, g1p_ref[...]).astype(BF16)

    tok = lambda w, dt: jax.ShapeDtypeStruct((n_tok, w), dt)
    return pl.pallas_call(
        body, name="mlp_fwd_bwd", grid=(n_tok // ts,),
        in_specs=[_rows(ts, dm)] * 3 + [_whole()] * 3 + [_hbm()],
        out_specs=[_rows(ts, ff), _rows(ts, ff), _rows(ts, dm), _rows(ts, dm), _rows(ts, dm), _rows(ts, dm),
                   pl.BlockSpec((SUBLANE, dm), lambda i: (0, 0))],
        out_shape=[tok(ff, BF16), tok(ff, BF16), tok(dm, BF16), tok(dm, BF16), tok(dm, BF16), tok(dm, F32),
                   jax.ShapeDtypeStruct((SUBLANE, dm), F32)],
        scratch_shapes=[pltpu.VMEM((ff, dm), BF16), pltpu.VMEM((ff, dm), BF16), pltpu.VMEM((ts, ff), F32),
                        pltpu.SemaphoreType.DMA((2 * N_DEV,))],
        compiler_params=_params(56),
    )(x1, mix, tgt, g1post, g2pre, g2post, slab)


def _bwd_mix(dmix, ya, yb, proj, va, vb, lng, lnb, slab, ts, deps):
    n_tok, dm = dmix.shape
    rows, off, _ = _layout(dm)
    n_steps = n_tok // ts

    def body(dmix_ref, ya_ref, yb_ref, bg_ref, za_ref, zb_ref, va_ref, vb_ref, lng_ref, lnb_ref, slab_ref,
             dpa_ref, dya_ref, dyb_ref, dva_ref, dvb_ref, small_ref, wa_v, wb_v, wo_v, sems):
        wo_copies = _weight_copies(slab_ref, off["wo"], rows["wo"], wo_v, sems, 2 * N_DEV)
        ab_copies = (_weight_copies(slab_ref, off["wa"], rows["wa"], wa_v, sems, 0)
                     + _weight_copies(slab_ref, off["wb"], rows["wb"], wb_v, sems, N_DEV))
        _on_first_step(wo_copies + ab_copies, "start")

        @pl.when(pl.program_id(0) == 0)
        def _():
            small_ref[...] = jnp.zeros_like(small_ref)

        _on_first_step(wo_copies, "wait")
        dmerged = _dot_nt(dmix_ref[...], wo_v[...])
        _on_first_step(ab_copies, "wait")
        sa = _sigmoid(za_ref[...].astype(F32))
        sg = _sigmoid(zb_ref[...].astype(F32))
        dza = dmerged * ya_ref[...].astype(F32) * sa * (1.0 - sa)
        dzb = dmerged * yb_ref[...].astype(F32) * sg * (1.0 - sg)
        dpa_ref[:, dm:2 * dm] = dza.astype(BF16)
        dpa_ref[:, 2 * dm:3 * dm] = dzb.astype(BF16)
        small_ref[5:6, :] += _colsum(dza)
        small_ref[6:7, :] += _colsum(dzb)

        dya = (dmerged * sa).astype(BF16)
        dya_ref[...] = dya
        dqa = _dot_nt(dya, wa_v[...])
        dbg = dqa * va_ref[...]
        dpa_ref[:, 0:dm] = dbg.astype(BF16)
        small_ref[4:5, :] += _colsum(dbg)
        dva = dqa * bg_ref[...].astype(F32)
        dva_ref[...] = dva
        small_ref[2:3, :] += _colsum(dva)

        dyb = (dmerged * sg).astype(BF16)
        dyb_ref[...] = dyb
        dsb = _dot_nt(dyb, wb_v[...])
        vb = vb_ref[...]
        xc = vb - jnp.mean(vb, axis=-1, keepdims=True)
        rstd = lax.rsqrt(jnp.mean(xc * xc, axis=-1, keepdims=True) + LN_EPS)
        nrm = xc * rstd
        lng_v = lng_ref[...]
        ln = nrm * lng_v + lnb_ref[...]
        sl = _sigmoid(ln)
        dln = dsb * (sl * (1.0 + ln * (1.0 - sl)))
        small_ref[0:1, :] += _colsum(dln * nrm)
        small_ref[1:2, :] += _colsum(dln)
        dn = dln * lng_v
        dvb = rstd * (dn - jnp.mean(dn, axis=-1, keepdims=True)
                      - nrm * jnp.mean(dn * nrm, axis=-1, keepdims=True))
        dvb_ref[...] = dvb
        small_ref[3:4, :] += _colsum(dvb)

    tok = lambda w, dt: jax.ShapeDtypeStruct((n_tok, w), dt)
    return pl.pallas_call(
        _after(body, deps), name="bwd_mix", grid=(n_steps,),
        in_specs=([_whole()] * len(deps) + [_rows(ts, dm)] * 3
                  + [_rows(ts, dm, 0), _rows(ts, dm, 5), _rows(ts, dm, 6)]
                  + [_rows(ts, dm)] * 2 + [_whole()] * 2 + [_hbm()]),
        out_specs=[_rows(ts, 3 * dm), _rows(ts, dm), _rows(ts, dm), _rows(ts, dm), _rows(ts, dm),
                   pl.BlockSpec((SUBLANE, dm), lambda i: (0, 0))],
        out_shape=[tok(3 * dm, BF16), tok(dm, BF16), tok(dm, BF16), tok(dm, F32), tok(dm, F32),
                   jax.ShapeDtypeStruct((SUBLANE, dm), F32)],
        scratch_shapes=[pltpu.VMEM((dm, dm), BF16), pltpu.VMEM((dm, dm), BF16), pltpu.VMEM((dm, dm), BF16),
                        pltpu.SemaphoreType.DMA((3 * N_DEV,))],
        compiler_params=_params(48),
    )(*deps, dmix, ya, yb, proj, proj, proj, va, vb, lng, lnb, slab)


def _bwd_conv(dva, dvb, p, u, proj, dpa, caw, cbw, ts, deps):
    n_tok, dm = dva.shape
    n_steps = n_tok // ts
    small_rows = 40

    def body(dva_ref, dva_prev, dva_next, dvb_ref, dvb_prev, dvb_next, p_ref, p_prev, p_next,
             u_ref, u_prev, u_next, cg_ref, ha_ref, a_ref, g_ref, dpa_ref, caw_ref, cbw_ref,
             dproj_ref, small_ref,
             ext_dva, ext_dvb, ext_p, ext_u, dp_v, du_v, tap_a, tap_b, gwa_v, gwb_v):
        i = pl.program_id(0)

        @pl.when(i == 0)
        def _():
            _broadcast_taps(caw_ref, tap_a, CONV_A)
            _broadcast_taps(cbw_ref, tap_b, CONV_B)
            small_ref[...] = jnp.zeros_like(small_ref)
            gwa_v[...] = jnp.zeros_like(gwa_v)
            gwb_v[...] = jnp.zeros_like(gwb_v)

        _fill_ext(ext_dva, dva_ref, dva_prev, dva_next, i, n_steps, ts)
        _fill_ext(ext_dvb, dvb_ref, dvb_prev, dvb_next, i, n_steps, ts)
        _fill_ext(ext_p, p_ref, p_prev, p_next, i, n_steps, ts)
        _fill_ext(ext_u, u_ref, u_prev, u_next, i, n_steps, ts)


        def emit_dp(r0, lanes, acc):
            dp_v[pl.ds(r0, CONV_ROWS), lanes] = acc

        def emit_du(r0, lanes, acc):
            du_v[pl.ds(r0, CONV_ROWS), lanes] = acc

        _conv_tile(ext_dva, tap_a, _bwd_starts(CONV_A), ts, dm, emit_dp)
        _conv_tile(ext_dvb, tap_b, _bwd_starts(CONV_B), ts, dm, emit_du)
        _conv_wgrad_tile(ext_p, dva_ref, gwa_v, _fwd_starts(CONV_A), ts, dm)
        _conv_wgrad_tile(ext_u, dvb_ref, gwb_v, _fwd_starts(CONV_B), ts, dm)

        dp = dp_v[...]
        dcg = dp * ha_ref[...].astype(F32)
        dha = dp * cg_ref[...].astype(F32)
        du = du_v[...]
        sg = _sigmoid(g_ref[...].astype(F32))
        da = du * sg
        dg = du * a_ref[...].astype(F32) * sg * (1.0 - sg)
        dproj_ref[:, 0:dm] = dpa_ref[:, 0:dm]
        dproj_ref[:, dm:2 * dm] = dcg.astype(BF16)
        dproj_ref[:, 2 * dm:3 * dm] = dha.astype(BF16)
        dproj_ref[:, 3 * dm:4 * dm] = da.astype(BF16)
        dproj_ref[:, 4 * dm:5 * dm] = dg.astype(BF16)
        dproj_ref[:, 5 * dm:7 * dm] = dpa_ref[:, dm:3 * dm]
        small_ref[3:4, :] += _colsum(dcg)
        small_ref[4:5, :] += _colsum(dha)
        small_ref[5:6, :] += _colsum(da)
        small_ref[6:7, :] += _colsum(dg)

        @pl.when(i == n_steps - 1)
        def _():
            for k in range(CONV_A):
                small_ref[k:k + 1, :] = _colsum(gwa_v[k])
            for k in range(CONV_B):
                small_ref[SUBLANE + k:SUBLANE + k + 1, :] = _colsum(gwb_v[k])

    ext = pltpu.VMEM((ts + 2 * HALO, dm), F32)
    return pl.pallas_call(
        _after(body, deps), name="bwd_conv", grid=(n_steps,),
        in_specs=([_whole()] * len(deps) + _halo_specs(ts, dm, n_tok) * 4
                  + [_rows(ts, dm, 1), _rows(ts, dm, 2), _rows(ts, dm, 3), _rows(ts, dm, 4), _rows(ts, 3 * dm)]
                  + [_whole()] * 2),
        out_specs=[_rows(ts, 7 * dm), pl.BlockSpec((small_rows, dm), lambda i: (0, 0))],
        out_shape=[jax.ShapeDtypeStruct((n_tok, 7 * dm), BF16), jax.ShapeDtypeStruct((small_rows, dm), F32)],
        scratch_shapes=[ext, ext, ext, ext, pltpu.VMEM((ts, dm), F32), pltpu.VMEM((ts, dm), F32),
                        pltpu.VMEM((CONV_A, SUBLANE, dm), F32), pltpu.VMEM((CONV_B, SUBLANE, dm), F32),
                        pltpu.VMEM((CONV_A, SUBLANE, dm), F32), pltpu.VMEM((CONV_B, SUBLANE, dm), F32)],
        compiler_params=_params(48),
    )(*deps, dva, dva, dva, dvb, dvb, dvb, p, p, p, u, u, u, proj, proj, proj, proj, dpa, caw, cbw)


def _bwd_in(dproj, x, dx1, g1, slab, ts, deps):
    n_tok, dm = x.shape
    rows, off, _ = _layout(dm)
    width = 7 * dm

    def body(dproj_ref, x_ref, dx1_ref, g1_ref, slab_ref, gx_ref, small_ref, w_v, sems):
        copies = _weight_copies(slab_ref, off["win"], rows["win"], w_v, sems, 0)
        _on_first_step(copies, "start")

        @pl.when(pl.program_id(0) == 0)
        def _():
            small_ref[...] = jnp.zeros_like(small_ref)

        dh = jnp.zeros((ts, dm), F32)
        waited = 0
        for j in range(7):
            need = min(N_DEV, -(-(j + 1) * dm // rows["win"]))
            _on_first_step(copies[waited:need], "wait")
            waited = need
            dh = dh + _dot(dproj_ref[:, j * dm:(j + 1) * dm], w_v[pl.ds(j * dm, dm), :])
        xv = x_ref[...]
        r1 = _rms(xv)
        small_ref[0:1, :] += _colsum(dh * xv * r1)
        gx_ref[...] = dx1_ref[...] + _rms_bwd(dh, xv, r1, g1_ref[...])

    return pl.pallas_call(
        _after(body, deps), name="bwd_in", grid=(n_tok // ts,),
        in_specs=[_whole()] * len(deps) + [_rows(ts, width), _rows(ts, dm), _rows(ts, dm), _whole(), _hbm()],
        out_specs=[_rows(ts, dm), pl.BlockSpec((SUBLANE, dm), lambda i: (0, 0))],
        out_shape=[jax.ShapeDtypeStruct((n_tok, dm), F32), jax.ShapeDtypeStruct((SUBLANE, dm), F32)],
        scratch_shapes=[pltpu.VMEM((width, dm), BF16), pltpu.SemaphoreType.DMA((N_DEV,))],
        compiler_params=_params(56),
    )(*deps, dproj, x, dx1, g1, slab)


def _wgrad(a, b, name, tm, tk, out_dtype):
    n_tok, m = a.shape
    n = b.shape[1]
    k_steps = n_tok // tk

    def body(a_ref, b_ref, o_ref, acc_v):
        k = pl.program_id(1)

        @pl.when(k == 0)
        def _():
            acc_v[...] = jnp.zeros_like(acc_v)

        acc_v[...] += _dot_tn(a_ref[...], b_ref[...])

        @pl.when(k == k_steps - 1)
        def _():
            o_ref[...] = acc_v[...].astype(o_ref.dtype)

    return pl.pallas_call(
        body, name=name, grid=(m // tm, k_steps),
        in_specs=[pl.BlockSpec((tk, tm), lambda i, k: (k, i)), pl.BlockSpec((tk, n), lambda i, k: (k, 0))],
        out_specs=pl.BlockSpec((tm, n), lambda i, k: (i, 0)),
        out_shape=jax.ShapeDtypeStruct((m, n), out_dtype),
        scratch_shapes=[pltpu.VMEM((tm, n), F32)],
        compiler_params=pltpu.CompilerParams(dimension_semantics=("arbitrary", "arbitrary"),
                                             vmem_limit_bytes=40 * MIB),
    )(a, b)


def _adamw(w, g, m, v):
    m = ADAM_B1 * m + (1.0 - ADAM_B1) * g
    v = ADAM_B2 * v + (1.0 - ADAM_B2) * (g * g)
    m_hat = m / (1.0 - ADAM_B1 ** ADAM_STEP)
    v_hat = v / (1.0 - ADAM_B2 ** ADAM_STEP)
    delta = -ADAM_LR * (m_hat / (jnp.sqrt(v_hat) + ADAM_EPS) + ADAM_WD * w)
    return delta, m, v


def _adam_big(recv, part, me, off, rows, w, m, v, transpose, name, tr):
    dm = recv.shape[2]
    per = rows // tr

    def body(me_ref, own_ref, r_ref, w_ref, m_ref, v_ref, g_ref, d_ref, mo_ref, vo_ref):
        g = own_ref[...].astype(F32)
        for k in range(len(FLIPS)):
            g = g + r_ref[k].astype(F32)
        if transpose:
            g = g.T
        delta, m_new, v_new = _adamw(w_ref[...], g, m_ref[...], v_ref[...])
        g_ref[...] = g
        d_ref[...] = delta
        mo_ref[...] = m_new
        vo_ref[...] = v_new

    if transpose:
        blk = pl.BlockSpec((dm, tr), lambda i, me_ref: (0, i))
    else:
        blk = pl.BlockSpec((tr, dm), lambda i, me_ref: (i, 0))
    first = off // tr
    return pl.pallas_call(
        body, name=name,
        grid_spec=pltpu.PrefetchScalarGridSpec(
            num_scalar_prefetch=1, grid=(per,),
            in_specs=[pl.BlockSpec((tr, dm), lambda i, me_ref: (me_ref[0] * per + i, 0)),
                      pl.BlockSpec((len(FLIPS), tr, dm), lambda i, me_ref: (0, first + i, 0)), blk, blk, blk],
            out_specs=[blk] * 4),
        out_shape=[jax.ShapeDtypeStruct(w.shape, F32)] * 4,
        compiler_params=_params(32),
    )(me, part, recv, w, m, v)


LOSS_ROW = 15
CONV_A_ROW = 16
CONV_B_ROW = 24


def _adam_small(recv_small, recv_last, me, params, d_model):
    n = len(params)
    cw = d_model // N_DEV

    def body(me_ref, r_ref, rc_ref, l_ref, *refs):
        ins, loss_ref, outs = refs[:3 * n], refs[3 * n], refs[3 * n + 1:3 * n + 1 + 4 * n]
        g_v, gc_v, last_v = refs[3 * n + 1 + 4 * n:]
        g, gc, last = r_ref[0], rc_ref[0], l_ref[0]
        for d in range(1, N_DEV):
            g, gc, last = g + r_ref[d], gc + rc_ref[d], last + l_ref[d]
        g_v[...], gc_v[...], last_v[...] = g, gc, last
        loss_ref[...] = (0.5 / d_model) * jnp.sum(g_v[LOSS_ROW:LOSS_ROW + 1, :], axis=-1, keepdims=True)
        for j, (row0, own_columns, (w, _, _)) in enumerate(params):
            w_ref, m_ref, v_ref = ins[3 * j:3 * j + 3]
            source = gc_v if own_columns else (last_v if row0 == 0 else g_v)
            grad = source[row0:row0 + w.shape[0], :]
            delta, m_new, v_new = _adamw(w_ref[...], grad, m_ref[...], v_ref[...])
            for ref, val in zip(outs[4 * j:4 * j + 4], (grad, delta, m_new, v_new)):
                ref[...] = val

    full = lambda shape: pl.BlockSpec(shape, lambda i, me_ref: (0,) * len(shape))
    stack_rows = recv_small.shape[1]
    flat = [a for _, _, triple in params for a in triple]
    shapes = [w.shape for _, _, (w, _, _) in params for _ in range(4)]
    out = pl.pallas_call(
        body, name="adam_small",
        grid_spec=pltpu.PrefetchScalarGridSpec(
            num_scalar_prefetch=1, grid=(1,),
            in_specs=[full(recv_small.shape),
                      pl.BlockSpec((N_DEV, stack_rows, cw), lambda i, me_ref: (0, 0, me_ref[0])),
                      full(recv_last.shape)] + [full(a.shape) for a in flat],
            out_specs=[full((1, 1))] + [full(s) for s in shapes],
            scratch_shapes=[pltpu.VMEM((stack_rows, d_model), F32), pltpu.VMEM((stack_rows, cw), F32),
                            pltpu.VMEM(recv_last.shape[1:], F32)]),
        out_shape=[jax.ShapeDtypeStruct((1, 1), F32)] + [jax.ShapeDtypeStruct(s, F32) for s in shapes],
    )(me, recv_small, recv_small, recv_last, *flat)
    return out[0], [tuple(out[1 + 4 * j:5 + 4 * j]) for j in range(n)]


def _tile(n_tok, want):
    return min(want, n_tok)


def kernel(x, norm1_pre_g, w_in, b_in, conv_a_w, conv_a_b, w_a_out, conv_b_w, conv_b_b, ln_b_g, ln_b_b, w_b_out, w_o, norm1_post_g, norm2_pre_g, w_mlp_in, w_mlp_out, norm2_post_g, loss_target, m_norm1_pre_g, m_w_in, m_b_in, m_conv_a_w, m_conv_a_b, m_w_a_out, m_conv_b_w, m_conv_b_b, m_ln_b_g, m_ln_b_b, m_w_b_out, m_w_o, m_norm1_post_g, m_norm2_pre_g, m_w_mlp_in, m_w_mlp_out, m_norm2_post_g, v_norm1_pre_g, v_w_in, v_b_in, v_conv_a_w, v_conv_a_b, v_w_a_out, v_conv_b_w, v_conv_b_b, v_ln_b_g, v_ln_b_b, v_w_b_out, v_w_o, v_norm1_post_g, v_norm2_pre_g, v_w_mlp_in, v_w_mlp_out, v_norm2_post_g):
    n_tok, dm = x.shape[1], x.shape[2]
    rows, off, slab_rows = _layout(dm)
    cw = dm // N_DEV
    xs = x.reshape(n_tok, dm)
    tgt = loss_target.reshape(n_tok, dm)
    row = lambda vec: vec.reshape(1, -1)
    gathered = lambda group: (N_DEV, slab_rows[group], dm)
    scattered = lambda group: jax.ShapeDtypeStruct((len(FLIPS), slab_rows[group], dm), BF16)
    tm, tk = min(dm, 1024), _tile(n_tok, 2048)
    me = (4 * lax.axis_index("x") + 2 * lax.axis_index("y") + lax.axis_index("c")).astype(jnp.int32).reshape(1)

    conv_own = jnp.concatenate([conv_a_w, jnp.zeros((SUBLANE - CONV_A, cw), F32), conv_b_w,
                                jnp.zeros((1, cw), F32)], axis=0)
    own_in = w_in.T.astype(BF16)
    slab_in, conv_all = _all_gather_two_level(
        [own_in, conv_own], [_place_own(own_in, gathered("in"), me, "place_w_in"), None], "gather_w_in")
    conv_full = conv_all.transpose(1, 0, 2).reshape(conv_own.shape[0], dm)
    caw, cbw = conv_full[0:CONV_A], conv_full[SUBLANE:SUBLANE + CONV_B]
    own_abo = jnp.concatenate([w_a_out, w_b_out, w_o], axis=0).astype(BF16)
    own_mlp = jnp.concatenate([w_mlp_in.T, w_mlp_out], axis=0).astype(BF16)
    ag_abo = _exchange_start([_Part(own_abo, False, slab_rows["abo"], 0, 0)],
                             [_place_own(own_abo, gathered("abo"), me, "place_abo")],
                             "gather_abo_start", after=slab_in)
    ag_mlp = _exchange_start([_Part(own_mlp, False, slab_rows["mlp"], 0, 0)],
                             [_place_own(own_mlp, gathered("mlp"), me, "place_mlp")],
                             "gather_mlp_start", after=ag_abo.token)

    proj, p, u, h = _fwd_in(xs, row(norm1_pre_g), row(b_in), slab_in, _tile(n_tok, 512),
                            [ag_abo.token, ag_mlp.token])
    _, (slab_abo,) = _exchange_wait(ag_abo, "gather_abo_wait", after=proj)
    va, vb, ya, yb, qa, sb, merged, mix, x1 = _fwd_mix(
        p, u, proj, xs, caw, row(conv_a_b), cbw, row(conv_b_b), row(ln_b_g), row(ln_b_b), row(norm1_post_g),
        slab_abo, _tile(n_tok, 256))
    _, (slab_mlp,) = _exchange_wait(ag_mlp, "gather_mlp_wait", after=x1)
    f, df1, h2, df2, dmix, dx1, small_mlp = _mlp_fwd_bwd(
        x1, mix, tgt, row(norm1_post_g), row(norm2_pre_g), row(norm2_post_g), slab_mlp, _tile(n_tok, 256))

    rs_mlp = _exchange_start(
        [_Part(_wgrad(df1, h2, "wgrad_mlp_in", tm, tk, BF16), True, rows["w1"], 0, off["w1"]),
         _Part(_wgrad(f, df2, "wgrad_mlp_out", tm, tk, BF16), True, rows["w2"], 0, off["w2"])],
        [scattered("mlp")], "scatter_mlp_start")
    dpa, dya, dyb, dva, dvb, small_mix = _bwd_mix(
        dmix, ya, yb, proj, va, vb, row(ln_b_g), row(ln_b_b), slab_abo, _tile(n_tok, 256), [rs_mlp.token])
    rs_abo = _exchange_start(
        [_Part(_wgrad(qa, dya, "wgrad_a_out", tm, tk, BF16), True, rows["wa"], 0, off["wa"]),
         _Part(_wgrad(sb, dyb, "wgrad_b_out", tm, tk, BF16), True, rows["wb"], 0, off["wb"]),
         _Part(_wgrad(merged, dmix, "wgrad_o", tm, tk, BF16), True, rows["wo"], 0, off["wo"])],
        [scattered("abo")], "scatter_abo_start")
    dproj, small_conv = _bwd_conv(dva, dvb, p, u, proj, dpa, caw, cbw, _tile(n_tok, 256), [rs_abo.token])

    zeros = lambda r: jnp.zeros((r, dm), F32)
    small = jnp.concatenate([
        zeros(1),
        small_mix[2:3],
        small_mix[3:4],
        small_mix[0:2],
        small_mlp[2:3],
        small_mlp[1:2],
        small_mlp[0:1],
        small_mix[4:5], small_conv[3:7], small_mix[5:7],
        small_mlp[3:4],
        small_conv[0:CONV_A], zeros(SUBLANE - CONV_A),
        small_conv[8:8 + CONV_B], zeros(1),
    ], axis=0)

    rs_in = _exchange_start(
        [_Part(_wgrad(dproj, h, "wgrad_in", tm, tk, BF16), True, rows["win"], 0, off["win"]),
         _Part(small, False, small.shape[0], 1, 0)],
        [scattered("in"), _place_own(small, (N_DEV,) + small.shape, me, "place_small")], "scatter_in_start")
    grad_x, small_in = _bwd_in(dproj, xs, dx1, row(norm1_pre_g), slab_in, _tile(n_tok, 512), [rs_in.token])

    tr = min(LANE, rows["wa"])
    (g_w1, g_w2), (recv_mlp,) = _exchange_wait(rs_mlp, "scatter_mlp_wait", after=grad_x)
    (g_wa, g_wb, g_wo), (recv_abo,) = _exchange_wait(rs_abo, "scatter_abo_wait", after=grad_x)
    big = {
        "w_mlp_in": _adam_big(recv_mlp, g_w1, me, off["w1"], rows["w1"], w_mlp_in, m_w_mlp_in, v_w_mlp_in, True,
                              "adam_w_mlp_in", tr),
        "w_mlp_out": _adam_big(recv_mlp, g_w2, me, off["w2"], rows["w2"], w_mlp_out, m_w_mlp_out, v_w_mlp_out,
                               False, "adam_w_mlp_out", tr),
        "w_a_out": _adam_big(recv_abo, g_wa, me, off["wa"], rows["wa"], w_a_out, m_w_a_out, v_w_a_out, False,
                             "adam_w_a_out", tr),
        "w_b_out": _adam_big(recv_abo, g_wb, me, off["wb"], rows["wb"], w_b_out, m_w_b_out, v_w_b_out, False,
                             "adam_w_b_out", tr),
        "w_o": _adam_big(recv_abo, g_wo, me, off["wo"], rows["wo"], w_o, m_w_o, v_w_o, False, "adam_w_o", tr),
    }
    (g_win, _), (recv_in, recv_small) = _exchange_wait(rs_in, "scatter_in_wait", after=big["w_o"][3])
    recv_last, = _all_gather([small_in], "gather_last")
    big["w_in"] = _adam_big(recv_in, g_win, me, off["win"], rows["win"], w_in, m_w_in, v_w_in, True, "adam_w_in", tr)

    small_names = ("norm1_pre_g", "conv_a_b", "conv_b_b", "ln_b_g", "ln_b_b", "norm1_post_g", "norm2_pre_g",
                   "norm2_post_g")
    given = dict(
        norm1_pre_g=(norm1_pre_g, m_norm1_pre_g, v_norm1_pre_g), conv_a_b=(conv_a_b, m_conv_a_b, v_conv_a_b),
        conv_b_b=(conv_b_b, m_conv_b_b, v_conv_b_b), ln_b_g=(ln_b_g, m_ln_b_g, v_ln_b_g),
        ln_b_b=(ln_b_b, m_ln_b_b, v_ln_b_b), norm1_post_g=(norm1_post_g, m_norm1_post_g, v_norm1_post_g),
        norm2_pre_g=(norm2_pre_g, m_norm2_pre_g, v_norm2_pre_g),
        norm2_post_g=(norm2_post_g, m_norm2_post_g, v_norm2_post_g))
    params = [(j, False, tuple(row(a) for a in given[name])) for j, name in enumerate(small_names)]
    params.append((SUBLANE, False, tuple(a.reshape(7, dm) for a in (b_in, m_b_in, v_b_in))))
    params.append((CONV_A_ROW, True, (conv_a_w, m_conv_a_w, v_conv_a_w)))
    params.append((CONV_B_ROW, True, (conv_b_w, m_conv_b_w, v_conv_b_w)))
    loss, small_out = _adam_small(recv_small, recv_last, me, params, dm)
    small_leaves = {name: tuple(a.reshape(dm) for a in small_out[j]) for j, name in enumerate(small_names)}
    small_leaves["b_in"] = tuple(a.reshape(7 * dm) for a in small_out[len(small_names)])
    small_leaves["conv_a_w"] = small_out[len(small_names) + 1]
    small_leaves["conv_b_w"] = small_out[len(small_names) + 2]

    order = ("norm1_pre_g", "w_in", "b_in", "conv_a_w", "conv_a_b", "w_a_out", "conv_b_w", "conv_b_b", "ln_b_g",
             "ln_b_b", "w_b_out", "w_o", "norm1_post_g", "norm2_pre_g", "w_mlp_in", "w_mlp_out", "norm2_post_g")
    leaves = [big[name] if name in big else small_leaves[name] for name in order]
    grads, deltas, new_m, new_v = zip(*leaves)
    return (loss.reshape(()), grad_x.reshape(x.shape), *grads, *deltas, *new_m, *new_v)
```

```python
from typing import NamedTuple

import jax
import jax.numpy as jnp
from jax import lax
from jax.experimental import pallas as pl
from jax.experimental.pallas import tpu as pltpu

F32 = jnp.float32
BF16 = jnp.bfloat16

RMS_EPS = 1e-6
LN_EPS = 1e-5
ADAM_LR = 0.001
ADAM_B1 = 0.9
ADAM_B2 = 0.999
ADAM_EPS = 1e-08
ADAM_WD = 0.01
ADAM_STEP = 10

N_DEV = 8
CONV_A = 3
CONV_B = 31
LANE = 128
SUBLANE = 8
HALO = 16
CONV_ROWS = 64
MIB = 1 << 20
FLIPS = ((0, 0, 1), (0, 1, 0), (1, 0, 0), (0, 1, 1), (1, 0, 1), (1, 1, 0), (1, 1, 1))
MESH = pl.DeviceIdType.MESH


def _layout(d_model):
    e = d_model // N_DEV
    rows = {"win": 7 * e, "w1": 4 * e, "w2": 4 * e, "wa": e, "wb": e, "wo": e}
    off = {"win": 0, "w1": 0, "w2": 4 * e, "wa": 0, "wb": e, "wo": 2 * e}
    return rows, off, {"in": 7 * e, "mlp": 8 * e, "abo": 3 * e}


def _after(body, deps):
    def ordered(*refs):
        return body(*refs[len(deps):])
    return ordered


def _params(vmem_mib):
    return pltpu.CompilerParams(dimension_semantics=("arbitrary",), vmem_limit_bytes=vmem_mib * MIB)


def _whole():
    return pl.BlockSpec(memory_space=pltpu.VMEM)


def _hbm():
    return pl.BlockSpec(memory_space=pl.ANY)


def _rows(ts, width, col=0):
    return pl.BlockSpec((ts, width), lambda i: (i, col))


def _halo_specs(ts, width, n_rows):
    per = ts // HALO
    last = n_rows // HALO - 1
    return [
        pl.BlockSpec((ts, width), lambda i: (i, 0)),
        pl.BlockSpec((HALO, width), lambda i: (jnp.maximum(i * per - 1, 0), 0)),
        pl.BlockSpec((HALO, width), lambda i: (jnp.minimum((i + 1) * per, last), 0)),
    ]


def _dot(a, b):
    return jnp.dot(a, b, preferred_element_type=F32)


def _dot_nt(a, b):
    return lax.dot_general(a, b, (((1,), (1,)), ((), ())), preferred_element_type=F32)


def _dot_tn(a, b):
    return lax.dot_general(a, b, (((0,), (0,)), ((), ())), preferred_element_type=F32)


def _rms(u):
    return lax.rsqrt(jnp.mean(u * u, axis=-1, keepdims=True) + RMS_EPS)


def _rms_bwd(dz, u, r, g):
    dzg = dz * g
    return r * dzg - u * (r * r * r) * jnp.mean(dzg * u, axis=-1, keepdims=True)


def _colsum(v):
    return jnp.sum(v, axis=0, keepdims=True)


def _sigmoid(v):
    return jax.nn.sigmoid(v)


def _weight_copies(slab_ref, off, rows, dst_ref, sems, first_sem):
    return [pltpu.make_async_copy(slab_ref.at[d, pl.ds(off, rows), :], dst_ref.at[pl.ds(d * rows, rows), :],
                                  sems.at[first_sem + d]) for d in range(N_DEV)]


def _on_first_step(copies, method):
    @pl.when(pl.program_id(0) == 0)
    def _():
        for cp in copies:
            getattr(cp, method)()


def _with_halos(main_ref, prev_ref, next_ref, i, n_steps):
    return (main_ref, jnp.where(i > 0, prev_ref[...], 0.0), jnp.where(i < n_steps - 1, next_ref[...], 0.0))


def _window(tile, r0, cb, ts):
    main_ref, prev, nxt = tile
    lanes = slice(cb * LANE, (cb + 1) * LANE)
    lo, hi = max(r0 - HALO, 0), min(r0 + CONV_ROWS + HALO, ts)
    pieces = [prev[:, lanes]] if r0 - HALO < 0 else []
    pieces.append(main_ref[lo:hi, lanes])
    if r0 + CONV_ROWS + HALO > ts:
        pieces.append(nxt[:, lanes])
    return pieces[0] if len(pieces) == 1 else jnp.concatenate(pieces, axis=0)


def _broadcast_taps(w_ref, wb_ref, n_taps):
    for k in range(n_taps):
        wb_ref[k] = jnp.broadcast_to(w_ref[k:k + 1, :], wb_ref.shape[1:])


def _phases(starts):
    groups = {}
    for k, st in enumerate(starts):
        groups.setdefault(st % SUBLANE, []).append((k, st // SUBLANE))
    return sorted(groups.items())


def _shifted(blk, b):
    n = blk.shape[0]
    rolled = blk if b == 0 else pltpu.roll(blk, n - b, axis=0)
    return rolled.reshape(n // SUBLANE, SUBLANE, blk.shape[1])


def _conv_tile(tile, wb_ref, starts, ts, width, emit):
    groups = _phases(starts)
    nv = CONV_ROWS // SUBLANE
    for r0 in range(0, ts, CONV_ROWS):
        for cb in range(width // LANE):
            lanes = pl.ds(cb * LANE, LANE)
            blk = _window(tile, r0, cb, ts)
            acc = jnp.zeros((nv, SUBLANE, LANE), F32)
            for b, taps in groups:
                sh = _shifted(blk, b)
                for k, m in taps:
                    acc = acc + sh[m:m + nv] * wb_ref[k, :, lanes][None]
            emit(r0, lanes, acc.reshape(CONV_ROWS, LANE))


def _conv_wgrad_tile(tile, dv_ref, acc_ref, starts, ts, width):
    groups = _phases(starts)
    nv = CONV_ROWS // SUBLANE
    for r0 in range(0, ts, CONV_ROWS):
        for cb in range(width // LANE):
            lanes = pl.ds(cb * LANE, LANE)
            blk = _window(tile, r0, cb, ts)
            dv = dv_ref[pl.ds(r0, CONV_ROWS), lanes].reshape(nv, SUBLANE, LANE)
            for b, taps in groups:
                sh = _shifted(blk, b)
                for k, m in taps:
                    acc_ref[k, :, lanes] += jnp.sum(sh[m:m + nv] * dv, axis=0)


def _fwd_starts(n_taps):
    pad = (n_taps - 1) // 2
    return [HALO - pad + k for k in range(n_taps)]


def _bwd_starts(n_taps):
    pad = (n_taps - 1) // 2
    return [HALO + pad - k for k in range(n_taps)]


def _peer(x, y, c, flip):
    fx, fy, fc = flip
    return (1 - x if fx else x, 1 - y if fy else y, 1 - c if fc else c)


def _all_gather(shards, name):
    n = len(shards)

    def body(*refs):
        ins, outs = refs[:n], refs[n:2 * n]
        send_sems, recv_sems, local_sems = refs[2 * n:]
        x, y, c = lax.axis_index("x"), lax.axis_index("y"), lax.axis_index("c")
        me = 4 * x + 2 * y + c
        local = [pltpu.make_async_copy(ins[j], outs[j].at[me], local_sems.at[j]) for j in range(n)]
        for cp in local:
            cp.start()
        sends, recvs = [], []
        for k, flip in enumerate(FLIPS):
            px, py, pc = _peer(x, y, c, flip)
            peer = 4 * px + 2 * py + pc
            for j in range(n):
                sem = k * n + j
                sends.append(pltpu.make_async_remote_copy(
                    src_ref=ins[j], dst_ref=outs[j].at[me], send_sem=send_sems.at[sem], recv_sem=recv_sems.at[sem],
                    device_id=(px, py, pc), device_id_type=MESH))
                recvs.append(pltpu.make_async_remote_copy(
                    src_ref=ins[j], dst_ref=outs[j].at[peer], send_sem=send_sems.at[sem], recv_sem=recv_sems.at[sem],
                    device_id=(px, py, pc), device_id_type=MESH))
        for cp in sends:
            cp.start()
        for cp in recvs:
            cp.wait_recv()
        for cp in sends:
            cp.wait_send()
        for cp in local:
            cp.wait()

    return pl.pallas_call(
        body, name=name,
        out_shape=[jax.ShapeDtypeStruct((N_DEV,) + s.shape, s.dtype) for s in shards],
        in_specs=[_hbm()] * n, out_specs=[_hbm()] * n,
        scratch_shapes=[pltpu.SemaphoreType.DMA((7 * n,)), pltpu.SemaphoreType.DMA((7 * n,)),
                        pltpu.SemaphoreType.DMA((n,))],
    )(*shards)


def _place_own(src, n_slots_shape, me, name):
    rows, width = src.shape
    tr = next(t for t in (256, 128, 64, 32, 16, SUBLANE) if rows % t == 0)

    def body(me_ref, src_ref, out_ref):
        out_ref[...] = src_ref[...]

    return pl.pallas_call(
        body, name=name,
        grid_spec=pltpu.PrefetchScalarGridSpec(
            num_scalar_prefetch=1, grid=(rows // tr,),
            in_specs=[pl.BlockSpec((tr, width), lambda i, me_ref: (i, 0))],
            out_specs=pl.BlockSpec((None, tr, width), lambda i, me_ref: (me_ref[0], i, 0))),
        out_shape=jax.ShapeDtypeStruct(n_slots_shape, src.dtype),
    )(me, src)


def _all_gather_two_level(shards, placed, name):
    n = len(shards)
    given = [j for j in range(n) if placed[j] is not None]

    def body(*refs):
        ins, outs = refs[:n], refs[n + len(given):2 * n + len(given)]
        send_sems, recv_sems, local_sems = refs[2 * n + len(given):]
        x, y, c = lax.axis_index("x"), lax.axis_index("y"), lax.axis_index("c")
        me, sibling = (x, y, c), (x, y, 1 - c)
        chips = [(1 - x, y), (x, 1 - y), (1 - x, 1 - y)]

        def slot(j, dev):
            return outs[j].at[4 * dev[0] + 2 * dev[1] + dev[2]]

        def copy(k, j, block, to, src=None):
            return pltpu.make_async_remote_copy(
                src_ref=slot(j, block) if src is None else src, dst_ref=slot(j, block),
                send_sem=send_sems.at[k * n + j], recv_sem=recv_sems.at[k * n + j], device_id=to, device_id_type=MESH)

        local = [pltpu.make_async_copy(ins[j], slot(j, me), local_sems.at[j]) for j in range(n) if j not in given]
        for cp in local:
            cp.start()
        first = [copy(0, j, me, sibling, src=ins[j]) for j in range(n)]
        first += [copy(1 + t, j, me, (*chip, c), src=ins[j]) for t, chip in enumerate(chips) for j in range(n)]
        for cp in first:
            cp.start()
        passed = []
        for t, chip in enumerate(chips):
            for j in range(n):
                copy(1 + t, j, (*chip, c), me).wait_recv()
                passed.append(copy(4 + t, j, (*chip, c), sibling))
                passed[-1].start()
        for j in range(n):
            copy(0, j, sibling, me).wait_recv()
        for t, chip in enumerate(chips):
            for j in range(n):
                copy(4 + t, j, (*chip, 1 - c), me).wait_recv()
        for cp in first + passed:
            cp.wait_send()
        for cp in local:
            cp.wait()

    return pl.pallas_call(
        body, name=name,
        out_shape=[jax.ShapeDtypeStruct((N_DEV,) + s.shape, s.dtype) for s in shards],
        in_specs=[_hbm()] * (n + len(given)), out_specs=[_hbm()] * n,
        input_output_aliases={n + i: j for i, j in enumerate(given)},
        scratch_shapes=[pltpu.SemaphoreType.DMA((7 * n,)), pltpu.SemaphoreType.DMA((7 * n,)),
                        pltpu.SemaphoreType.DMA((n,))],
    )(*shards, *[placed[j] for j in given])


class _Part(NamedTuple):
    src: jax.Array
    scatter: bool
    rows: int
    land: int
    off: int


class _Started(NamedTuple):
    send_sems: jax.Array
    recv_sems: jax.Array
    thru: tuple
    token: jax.Array
    parts: tuple


def _exchange_copies(srcs, lands, send_sems, recv_sems, parts):
    n = len(parts)
    x, y, c = lax.axis_index("x"), lax.axis_index("y"), lax.axis_index("c")
    me = 4 * x + 2 * y + c

    def block(j, dev):
        p = parts[j]
        return srcs[j].at[pl.ds(pl.multiple_of(dev * p.rows, SUBLANE), p.rows), :] if p.scatter else srcs[j]

    def slot(j, index):
        p = parts[j]
        return lands[p.land].at[index, pl.ds(p.off, p.rows), :]

    sends, recvs = [], []
    for k, flip in enumerate(FLIPS):
        px, py, pc = _peer(x, y, c, flip)
        peer = 4 * px + 2 * py + pc
        for j in range(n):
            sems = dict(send_sem=send_sems.at[k * n + j], recv_sem=recv_sems.at[k * n + j],
                        device_id=(px, py, pc), device_id_type=MESH)
            to, got = (k, k) if parts[j].scatter else (me, peer)
            sends.append(pltpu.make_async_remote_copy(src_ref=block(j, peer), dst_ref=slot(j, to), **sems))
            recvs.append(pltpu.make_async_remote_copy(src_ref=block(j, peer), dst_ref=slot(j, got), **sems))
    return sends, recvs


def _exchange_start(parts, lands, name, after=None):
    n, nl = len(parts), len(lands)
    n_in = n + nl + (after is not None)

    def body(*refs):
        srcs, land_refs = refs[:n], refs[n:n + nl]
        send_sems, recv_sems = refs[n_in], refs[n_in + 1]
        token = refs[n_in + 2 + n + nl]
        sends, _ = _exchange_copies(srcs, land_refs, send_sems, recv_sems, parts)
        for cp in sends:
            cp.start()
        token[...] = jnp.zeros_like(token)

    hbm = pl.BlockSpec(memory_space=pltpu.HBM)
    sem = pl.BlockSpec(memory_space=pltpu.SEMAPHORE)
    fresh = lambda s: lax.empty(s.shape, s.dtype) if isinstance(s, jax.ShapeDtypeStruct) else s
    args = [pltpu.with_memory_space_constraint(p.src, pltpu.HBM) for p in parts]
    args += [pltpu.with_memory_space_constraint(fresh(s), pltpu.HBM) for s in lands]
    args += [] if after is None else [after]
    out = pl.pallas_call(
        body, name=name,
        out_shape=(pltpu.SemaphoreType.DMA((7 * n,)), pltpu.SemaphoreType.DMA((7 * n,)),
                   *[pltpu.HBM(a.shape, a.dtype) for a in args[:n + nl]], jax.ShapeDtypeStruct((SUBLANE, LANE), F32)),
        in_specs=[hbm] * (n + nl) + [_hbm()] * (after is not None),
        out_specs=(sem, sem, *[hbm] * (n + nl), _whole()),
        input_output_aliases={j: 2 + j for j in range(n + nl)},
        compiler_params=pltpu.CompilerParams(has_side_effects=pltpu.SideEffectType.DATAFLOW_SIDE_EFFECTING),
    )(*args)
    return _Started(out[0], out[1], tuple(out[2:2 + n + nl]), out[2 + n + nl], tuple(parts))


def _exchange_wait(started, name, after):
    parts = started.parts
    n, nl = len(parts), len(started.thru) - len(parts)

    def body(*refs):
        srcs, land_refs = refs[:n], refs[n:n + nl]
        send_sems, recv_sems = refs[n + nl], refs[n + nl + 1]
        sends, recvs = _exchange_copies(srcs, land_refs, send_sems, recv_sems, parts)
        for cp in sends:
            cp.wait_send()
        for cp in recvs:
            cp.wait_recv()

    hbm = pl.BlockSpec(memory_space=pltpu.HBM)
    sem = pl.BlockSpec(memory_space=pltpu.SEMAPHORE)
    out = pl.pallas_call(
        body, name=name,
        out_shape=tuple(pltpu.HBM(a.shape, a.dtype) for a in started.thru),
        in_specs=[hbm] * (n + nl) + [sem, sem, _hbm()], out_specs=tuple([hbm] * (n + nl)),
        input_output_aliases={j: j for j in range(n + nl)},
        compiler_params=pltpu.CompilerParams(has_side_effects=pltpu.SideEffectType.DATAFLOW_SIDE_EFFECTING),
    )(*started.thru, started.send_sems, started.recv_sems, after)
    return list(out[:n]), list(out[n:])


def _fwd_in(x, g1, b_in, slab, ts, deps):
    n_tok, dm = x.shape
    rows, off, _ = _layout(dm)
    width = 7 * dm

    def body(x_ref, g1_ref, b_ref, slab_ref, proj_ref, p_ref, u_ref, h_ref, w_v, sems):
        copies = _weight_copies(slab_ref, off["win"], rows["win"], w_v, sems, 0)
        _on_first_step(copies, "start")
        _on_first_step(copies, "wait")
        xv = x_ref[...]
        h = (xv * _rms(xv) * g1_ref[...]).astype(BF16)
        h_ref[...] = h
        cols = []
        for j in range(7):
            pj = _dot_nt(h, w_v[pl.ds(j * dm, dm), :]) + b_ref[:, j * dm:(j + 1) * dm]
            proj_ref[:, j * dm:(j + 1) * dm] = pj.astype(proj_ref.dtype)
            if 1 <= j <= 4:
                cols.append(pj)
            if j == 2:
                p_ref[...] = cols[0] * cols[1]
            if j == 4:
                u_ref[...] = cols[2] * _sigmoid(cols[3])

    return pl.pallas_call(
        _after(body, deps), name="fwd_in", grid=(n_tok // ts,),
        in_specs=[_whole()] * len(deps) + [_rows(ts, dm), _whole(), _whole(), _hbm()],
        out_specs=[_rows(ts, width), _rows(ts, dm), _rows(ts, dm), _rows(ts, dm)],
        out_shape=[jax.ShapeDtypeStruct((n_tok, width), BF16), jax.ShapeDtypeStruct((n_tok, dm), F32),
                   jax.ShapeDtypeStruct((n_tok, dm), F32), jax.ShapeDtypeStruct((n_tok, dm), BF16)],
        scratch_shapes=[pltpu.VMEM((width, dm), BF16), pltpu.SemaphoreType.DMA((N_DEV,))],
        compiler_params=_params(56),
    )(*deps, x, g1, b_in, slab)


def _fwd_mix(p, u, proj, x, caw, cab, cbw, cbb, lng, lnb, g1post, slab, ts):
    n_tok, dm = x.shape
    rows, off, _ = _layout(dm)
    n_steps = n_tok // ts

    def body(p_ref, p_prev, p_next, u_ref, u_prev, u_next, bg_ref, za_ref, zb_ref, x_ref,
             caw_ref, cab_ref, cbw_ref, cbb_ref, lng_ref, lnb_ref, g1p_ref, slab_ref,
             va_ref, vb_ref, ya_ref, yb_ref, qa_ref, sb_ref, mg_ref, mix_ref, x1_ref,
             wa_v, wb_v, wo_v, tap_a, tap_b, sems):
        i = pl.program_id(0)
        copies = (_weight_copies(slab_ref, off["wa"], rows["wa"], wa_v, sems, 0)
                  + _weight_copies(slab_ref, off["wb"], rows["wb"], wb_v, sems, N_DEV)
                  + _weight_copies(slab_ref, off["wo"], rows["wo"], wo_v, sems, 2 * N_DEV))
        _on_first_step(copies, "start")

        @pl.when(i == 0)
        def _():
            _broadcast_taps(caw_ref, tap_a, CONV_A)
            _broadcast_taps(cbw_ref, tap_b, CONV_B)

        def emit_a(r0, lanes, acc):
            va_ref[pl.ds(r0, CONV_ROWS), lanes] = acc + cab_ref[:, lanes]

        def emit_b(r0, lanes, acc):
            vb_ref[pl.ds(r0, CONV_ROWS), lanes] = acc + cbb_ref[:, lanes]

        _conv_tile(_with_halos(p_ref, p_prev, p_next, i, n_steps), tap_a, _fwd_starts(CONV_A), ts, dm, emit_a)
        _conv_tile(_with_halos(u_ref, u_prev, u_next, i, n_steps), tap_b, _fwd_starts(CONV_B), ts, dm, emit_b)
        _on_first_step(copies, "wait")

        qa = (bg_ref[...].astype(F32) * va_ref[...]).astype(BF16)
        qa_ref[...] = qa
        ya = _dot(qa, wa_v[...])
        vb = vb_ref[...]
        xc = vb - jnp.mean(vb, axis=-1, keepdims=True)
        rstd = lax.rsqrt(jnp.mean(xc * xc, axis=-1, keepdims=True) + LN_EPS)
        ln = xc * rstd * lng_ref[...] + lnb_ref[...]
        sb = (ln * _sigmoid(ln)).astype(BF16)
        sb_ref[...] = sb
        yb = _dot(sb, wb_v[...])
        ya_ref[...] = ya.astype(BF16)
        yb_ref[...] = yb.astype(BF16)
        merged = (_sigmoid(za_ref[...].astype(F32)) * ya + _sigmoid(zb_ref[...].astype(F32)) * yb).astype(BF16)
        mg_ref[...] = merged
        mix = _dot(merged, wo_v[...])
        mix_ref[...] = mix
        x1_ref[...] = x_ref[...] + mix * _rms(mix) * g1p_ref[...]

    tok = lambda dt: jax.ShapeDtypeStruct((n_tok, dm), dt)
    return pl.pallas_call(
        body, name="fwd_mix", grid=(n_steps,),
        in_specs=(_halo_specs(ts, dm, n_tok) + _halo_specs(ts, dm, n_tok)
                  + [_rows(ts, dm, 0), _rows(ts, dm, 5), _rows(ts, dm, 6), _rows(ts, dm)]
                  + [_whole()] * 7 + [_hbm()]),
        out_specs=[_rows(ts, dm)] * 9,
        out_shape=[tok(F32), tok(F32), tok(BF16), tok(BF16), tok(BF16), tok(BF16), tok(BF16), tok(F32), tok(F32)],
        scratch_shapes=[pltpu.VMEM((dm, dm), BF16), pltpu.VMEM((dm, dm), BF16), pltpu.VMEM((dm, dm), BF16),
                        pltpu.VMEM((CONV_A, SUBLANE, dm), F32), pltpu.VMEM((CONV_B, SUBLANE, dm), F32),
                        pltpu.SemaphoreType.DMA((3 * N_DEV,))],
        compiler_params=_params(48),
    )(p, p, p, u, u, u, proj, proj, proj, x, caw, cab, cbw, cbb, lng, lnb, g1post, slab)


def _mlp_fwd_bwd(x1, mix, tgt, g1post, g2pre, g2post, slab, ts):
    n_tok, dm = x1.shape
    rows, off, _ = _layout(dm)
    ff = 4 * dm

    def body(x1_ref, mix_ref, t_ref, g1p_ref, g2pre_ref, g2post_ref, slab_ref,
             f_ref, df1_ref, h2_ref, df2_ref, dmix_ref, dx1_ref, small_ref, w1_v, w2_v, relu_v, sems):
        w1_copies = _weight_copies(slab_ref, off["w1"], rows["w1"], w1_v, sems, 0)
        w2_copies = _weight_copies(slab_ref, off["w2"], rows["w2"], w2_v, sems, N_DEV)
        _on_first_step(w1_copies + w2_copies, "start")

        @pl.when(pl.program_id(0) == 0)
        def _():
            small_ref[...] = jnp.zeros_like(small_ref)

        _on_first_step(w1_copies + w2_copies, "wait")
        x1v = x1_ref[...]
        r3 = _rms(x1v)
        g2pre = g2pre_ref[...]
        h2 = (x1v * r3 * g2pre).astype(BF16)
        h2_ref[...] = h2
        f2 = jnp.zeros((ts, dm), F32)
        for c in range(4):
            blk = pl.ds(c * dm, dm)
            relu = jnp.maximum(_dot_nt(h2, w1_v[blk, :]), 0.0)
            relu_v[:, c * dm:(c + 1) * dm] = relu
            fc = (relu * relu).astype(BF16)
            f_ref[:, c * dm:(c + 1) * dm] = fc
            f2 = f2 + _dot(fc, w2_v[blk, :])
        r4 = _rms(f2)
        g2post = g2post_ref[...]
        err = x1v + f2 * r4 * g2post - t_ref[...]
        dy = err * (1.0 / dm)
        small_ref[3:4, :] += _colsum(err * err)
        small_ref[0:1, :] += _colsum(dy * f2 * r4)
        df2 = _rms_bwd(dy, f2, r4, g2post).astype(BF16)
        df2_ref[...] = df2
        dh2 = jnp.zeros((ts, dm), F32)
        for c in range(4):
            blk = pl.ds(c * dm, dm)
            df1 = (_dot_nt(df2, w2_v[blk, :]) * (2.0 * relu_v[:, c * dm:(c + 1) * dm])).astype(BF16)
            df1_ref[:, c * dm:(c + 1) * dm] = df1
            dh2 = dh2 + _dot(df1, w1_v[blk, :])
        small_ref[1:2, :] += _colsum(dh2 * x1v * r3)
        dx1 = dy + _rms_bwd(dh2, x1v, r3, g2pre)
        dx1_ref[...] = dx1
        mixv = mix_ref[...]
        r2 = _rms(mixv)
        small_ref[2:3, :] += _colsum(dx1 * mixv * r2)
        dmix_ref[...] = _rms_bwd(dx1, mixv, r2, g1p_ref[...]).astype(BF16)

    tok = lambda w, dt: jax.ShapeDtypeStruct((n_tok, w), dt)
    return pl.pallas_call(
        body, name="mlp_fwd_bwd", grid=(n_tok // ts,),
        in_specs=[_rows(ts, dm)] * 3 + [_whole()] * 3 + [_hbm()],
        out_specs=[_rows(ts, ff), _rows(ts, ff), _rows(ts, dm), _rows(ts, dm), _rows(ts, dm), _rows(ts, dm),
                   pl.BlockSpec((SUBLANE, dm), lambda i: (0, 0))],
        out_shape=[tok(ff, BF16), tok(ff, BF16), tok(dm, BF16), tok(dm, BF16), tok(dm, BF16), tok(dm, F32),
                   jax.ShapeDtypeStruct((SUBLANE, dm), F32)],
        scratch_shapes=[pltpu.VMEM((ff, dm), BF16), pltpu.VMEM((ff, dm), BF16), pltpu.VMEM((ts, ff), F32),
                        pltpu.SemaphoreType.DMA((2 * N_DEV,))],
        compiler_params=_params(56),
    )(x1, mix, tgt, g1post, g2pre, g2post, slab)


def _bwd_mix(dmix, ya, yb, proj, va, vb, lng, lnb, slab, ts, deps):
    n_tok, dm = dmix.shape
    rows, off, _ = _layout(dm)
    n_steps = n_tok // ts

    def body(dmix_ref, ya_ref, yb_ref, bg_ref, za_ref, zb_ref, va_ref, vb_ref, lng_ref, lnb_ref, slab_ref,
             dpa_ref, dya_ref, dyb_ref, dva_ref, dvb_ref, small_ref, wa_v, wb_v, wo_v, sems):
        wo_copies = _weight_copies(slab_ref, off["wo"], rows["wo"], wo_v, sems, 2 * N_DEV)
        ab_copies = (_weight_copies(slab_ref, off["wa"], rows["wa"], wa_v, sems, 0)
                     + _weight_copies(slab_ref, off["wb"], rows["wb"], wb_v, sems, N_DEV))
        _on_first_step(wo_copies + ab_copies, "start")

        @pl.when(pl.program_id(0) == 0)
        def _():
            small_ref[...] = jnp.zeros_like(small_ref)

        _on_first_step(wo_copies, "wait")
        dmerged = _dot_nt(dmix_ref[...], wo_v[...])
        _on_first_step(ab_copies, "wait")
        sa = _sigmoid(za_ref[...].astype(F32))
        sg = _sigmoid(zb_ref[...].astype(F32))
        dza = dmerged * ya_ref[...].astype(F32) * sa * (1.0 - sa)
        dzb = dmerged * yb_ref[...].astype(F32) * sg * (1.0 - sg)
        dpa_ref[:, dm:2 * dm] = dza.astype(BF16)
        dpa_ref[:, 2 * dm:3 * dm] = dzb.astype(BF16)
        small_ref[5:6, :] += _colsum(dza)
        small_ref[6:7, :] += _colsum(dzb)

        dya = (dmerged * sa).astype(BF16)
        dya_ref[...] = dya
        dqa = _dot_nt(dya, wa_v[...])
        dbg = dqa * va_ref[...]
        dpa_ref[:, 0:dm] = dbg.astype(BF16)
        small_ref[4:5, :] += _colsum(dbg)
        dva = dqa * bg_ref[...].astype(F32)
        dva_ref[...] = dva
        small_ref[2:3, :] += _colsum(dva)

        dyb = (dmerged * sg).astype(BF16)
        dyb_ref[...] = dyb
        dsb = _dot_nt(dyb, wb_v[...])
        vb = vb_ref[...]
        xc = vb - jnp.mean(vb, axis=-1, keepdims=True)
        rstd = lax.rsqrt(jnp.mean(xc * xc, axis=-1, keepdims=True) + LN_EPS)
        nrm = xc * rstd
        lng_v = lng_ref[...]
        ln = nrm * lng_v + lnb_ref[...]
        sl = _sigmoid(ln)
        dln = dsb * (sl * (1.0 + ln * (1.0 - sl)))
        small_ref[0:1, :] += _colsum(dln * nrm)
        small_ref[1:2, :] += _colsum(dln)
        dn = dln * lng_v
        dvb = rstd * (dn - jnp.mean(dn, axis=-1, keepdims=True)
                      - nrm * jnp.mean(dn * nrm, axis=-1, keepdims=True))
        dvb_ref[...] = dvb
        small_ref[3:4, :] += _colsum(dvb)

    tok = lambda w, dt: jax.ShapeDtypeStruct((n_tok, w), dt)
    return pl.pallas_call(
        _after(body, deps), name="bwd_mix", grid=(n_steps,),
        in_specs=([_whole()] * len(deps) + [_rows(ts, dm)] * 3
                  + [_rows(ts, dm, 0), _rows(ts, dm, 5), _rows(ts, dm, 6)]
                  + [_rows(ts, dm)] * 2 + [_whole()] * 2 + [_hbm()]),
        out_specs=[_rows(ts, 3 * dm), _rows(ts, dm), _rows(ts, dm), _rows(ts, dm), _rows(ts, dm),
                   pl.BlockSpec((SUBLANE, dm), lambda i: (0, 0))],
        out_shape=[tok(3 * dm, BF16), tok(dm, BF16), tok(dm, BF16), tok(dm, F32), tok(dm, F32),
                   jax.ShapeDtypeStruct((SUBLANE, dm), F32)],
        scratch_shapes=[pltpu.VMEM((dm, dm), BF16), pltpu.VMEM((dm, dm), BF16), pltpu.VMEM((dm, dm), BF16),
                        pltpu.SemaphoreType.DMA((3 * N_DEV,))],
        compiler_params=_params(48),
    )(*deps, dmix, ya, yb, proj, proj, proj, va, vb, lng, lnb, slab)


def _bwd_conv(dva, dvb, p, u, proj, dpa, caw, cbw, ts, deps):
    n_tok, dm = dva.shape
    n_steps = n_tok // ts
    small_rows = 40

    def body(dva_ref, dva_prev, dva_next, dvb_ref, dvb_prev, dvb_next, p_ref, p_prev, p_next,
             u_ref, u_prev, u_next, cg_ref, ha_ref, a_ref, g_ref, dpa_ref, caw_ref, cbw_ref,
             dproj_ref, small_ref,
             dp_v, du_v, tap_a, tap_b, gwa_v, gwb_v):
        i = pl.program_id(0)

        @pl.when(i == 0)
        def _():
            _broadcast_taps(caw_ref, tap_a, CONV_A)
            _broadcast_taps(cbw_ref, tap_b, CONV_B)
            small_ref[...] = jnp.zeros_like(small_ref)
            gwa_v[...] = jnp.zeros_like(gwa_v)
            gwb_v[...] = jnp.zeros_like(gwb_v)

        ext_dva = _with_halos(dva_ref, dva_prev, dva_next, i, n_steps)
        ext_dvb = _with_halos(dvb_ref, dvb_prev, dvb_next, i, n_steps)
        ext_p = _with_halos(p_ref, p_prev, p_next, i, n_steps)
        ext_u = _with_halos(u_ref, u_prev, u_next, i, n_steps)


        def emit_dp(r0, lanes, acc):
            dp_v[pl.ds(r0, CONV_ROWS), lanes] = acc

        def emit_du(r0, lanes, acc):
            du_v[pl.ds(r0, CONV_ROWS), lanes] = acc

        _conv_tile(ext_dva, tap_a, _bwd_starts(CONV_A), ts, dm, emit_dp)
        _conv_tile(ext_dvb, tap_b, _bwd_starts(CONV_B), ts, dm, emit_du)
        _conv_wgrad_tile(ext_p, dva_ref, gwa_v, _fwd_starts(CONV_A), ts, dm)
        _conv_wgrad_tile(ext_u, dvb_ref, gwb_v, _fwd_starts(CONV_B), ts, dm)

        dp = dp_v[...]
        dcg = dp * ha_ref[...].astype(F32)
        dha = dp * cg_ref[...].astype(F32)
        du = du_v[...]
        sg = _sigmoid(g_ref[...].astype(F32))
        da = du * sg
        dg = du * a_ref[...].astype(F32) * sg * (1.0 - sg)
        dproj_ref[:, 0:dm] = dpa_ref[:, 0:dm]
        dproj_ref[:, dm:2 * dm] = dcg.astype(BF16)
        dproj_ref[:, 2 * dm:3 * dm] = dha.astype(BF16)
        dproj_ref[:, 3 * dm:4 * dm] = da.astype(BF16)
        dproj_ref[:, 4 * dm:5 * dm] = dg.astype(BF16)
        dproj_ref[:, 5 * dm:7 * dm] = dpa_ref[:, dm:3 * dm]
        small_ref[3:4, :] += _colsum(dcg)
        small_ref[4:5, :] += _colsum(dha)
        small_ref[5:6, :] += _colsum(da)
        small_ref[6:7, :] += _colsum(dg)

        @pl.when(i == n_steps - 1)
        def _():
            for k in range(CONV_A):
                small_ref[k:k + 1, :] = _colsum(gwa_v[k])
            for k in range(CONV_B):
                small_ref[SUBLANE + k:SUBLANE + k + 1, :] = _colsum(gwb_v[k])

    return pl.pallas_call(
        _after(body, deps), name="bwd_conv", grid=(n_steps,),
        in_specs=([_whole()] * len(deps) + _halo_specs(ts, dm, n_tok) * 4
                  + [_rows(ts, dm, 1), _rows(ts, dm, 2), _rows(ts, dm, 3), _rows(ts, dm, 4), _rows(ts, 3 * dm)]
                  + [_whole()] * 2),
        out_specs=[_rows(ts, 7 * dm), pl.BlockSpec((small_rows, dm), lambda i: (0, 0))],
        out_shape=[jax.ShapeDtypeStruct((n_tok, 7 * dm), BF16), jax.ShapeDtypeStruct((small_rows, dm), F32)],
        scratch_shapes=[pltpu.VMEM((ts, dm), F32), pltpu.VMEM((ts, dm), F32),
                        pltpu.VMEM((CONV_A, SUBLANE, dm), F32), pltpu.VMEM((CONV_B, SUBLANE, dm), F32),
                        pltpu.VMEM((CONV_A, SUBLANE, dm), F32), pltpu.VMEM((CONV_B, SUBLANE, dm), F32)],
        compiler_params=_params(48),
    )(*deps, dva, dva, dva, dvb, dvb, dvb, p, p, p, u, u, u, proj, proj, proj, proj, dpa, caw, cbw)


def _bwd_in(dproj, x, dx1, g1, slab, ts, deps):
    n_tok, dm = x.shape
    rows, off, _ = _layout(dm)
    width = 7 * dm

    def body(dproj_ref, x_ref, dx1_ref, g1_ref, slab_ref, gx_ref, small_ref, w_v, sems):
        copies = _weight_copies(slab_ref, off["win"], rows["win"], w_v, sems, 0)
        _on_first_step(copies, "start")

        @pl.when(pl.program_id(0) == 0)
        def _():
            small_ref[...] = jnp.zeros_like(small_ref)

        _on_first_step(copies, "wait")
        dh = _dot(dproj_ref[...], w_v[...])
        xv = x_ref[...]
        r1 = _rms(xv)
        small_ref[0:1, :] += _colsum(dh * xv * r1)
        gx_ref[...] = dx1_ref[...] + _rms_bwd(dh, xv, r1, g1_ref[...])

    return pl.pallas_call(
        _after(body, deps), name="bwd_in", grid=(n_tok // ts,),
        in_specs=[_whole()] * len(deps) + [_rows(ts, width), _rows(ts, dm), _rows(ts, dm), _whole(), _hbm()],
        out_specs=[_rows(ts, dm), pl.BlockSpec((SUBLANE, dm), lambda i: (0, 0))],
        out_shape=[jax.ShapeDtypeStruct((n_tok, dm), F32), jax.ShapeDtypeStruct((SUBLANE, dm), F32)],
        scratch_shapes=[pltpu.VMEM((width, dm), BF16), pltpu.SemaphoreType.DMA((N_DEV,))],
        compiler_params=_params(56),
    )(*deps, dproj, x, dx1, g1, slab)


def _wgrad(a, b, name, tm, tk, out_dtype):
    n_tok, m = a.shape
    n = b.shape[1]
    k_steps = n_tok // tk

    def body(a_ref, b_ref, o_ref, acc_v):
        k = pl.program_id(1)

        @pl.when(k == 0)
        def _():
            acc_v[...] = jnp.zeros_like(acc_v)

        acc_v[...] += _dot_tn(a_ref[...], b_ref[...])

        @pl.when(k == k_steps - 1)
        def _():
            o_ref[...] = acc_v[...].astype(o_ref.dtype)

    return pl.pallas_call(
        body, name=name, grid=(m // tm, k_steps),
        in_specs=[pl.BlockSpec((tk, tm), lambda i, k: (k, i)), pl.BlockSpec((tk, n), lambda i, k: (k, 0))],
        out_specs=pl.BlockSpec((tm, n), lambda i, k: (i, 0)),
        out_shape=jax.ShapeDtypeStruct((m, n), out_dtype),
        scratch_shapes=[pltpu.VMEM((tm, n), F32)],
        compiler_params=pltpu.CompilerParams(dimension_semantics=("arbitrary", "arbitrary"),
                                             vmem_limit_bytes=40 * MIB),
    )(a, b)


def _adamw(w, g, m, v):
    m = ADAM_B1 * m + (1.0 - ADAM_B1) * g
    v = ADAM_B2 * v + (1.0 - ADAM_B2) * (g * g)
    m_hat = m / (1.0 - ADAM_B1 ** ADAM_STEP)
    v_hat = v / (1.0 - ADAM_B2 ** ADAM_STEP)
    delta = -ADAM_LR * (m_hat / (jnp.sqrt(v_hat) + ADAM_EPS) + ADAM_WD * w)
    return delta, m, v


def _adam_big(recv, part, me, off, rows, w, m, v, transpose, name, tr):
    dm = recv.shape[2]
    per = rows // tr

    def body(me_ref, own_ref, r_ref, w_ref, m_ref, v_ref, g_ref, d_ref, mo_ref, vo_ref):
        g = own_ref[...].astype(F32)
        for k in range(len(FLIPS)):
            g = g + r_ref[k].astype(F32)
        if transpose:
            g = g.T
        delta, m_new, v_new = _adamw(w_ref[...], g, m_ref[...], v_ref[...])
        g_ref[...] = g
        d_ref[...] = delta
        mo_ref[...] = m_new
        vo_ref[...] = v_new

    if transpose:
        blk = pl.BlockSpec((dm, tr), lambda i, me_ref: (0, i))
    else:
        blk = pl.BlockSpec((tr, dm), lambda i, me_ref: (i, 0))
    first = off // tr
    return pl.pallas_call(
        body, name=name,
        grid_spec=pltpu.PrefetchScalarGridSpec(
            num_scalar_prefetch=1, grid=(per,),
            in_specs=[pl.BlockSpec((tr, dm), lambda i, me_ref: (me_ref[0] * per + i, 0)),
                      pl.BlockSpec((len(FLIPS), tr, dm), lambda i, me_ref: (0, first + i, 0)), blk, blk, blk],
            out_specs=[blk] * 4),
        out_shape=[jax.ShapeDtypeStruct(w.shape, F32)] * 4,
        compiler_params=_params(32),
    )(me, part, recv, w, m, v)


LOSS_ROW = 15
CONV_A_ROW = 16
CONV_B_ROW = 24


def _adam_small(recv_small, recv_last, me, params, d_model):
    n = len(params)
    cw = d_model // N_DEV

    def body(me_ref, r_ref, rc_ref, l_ref, *refs):
        ins, loss_ref, outs = refs[:3 * n], refs[3 * n], refs[3 * n + 1:3 * n + 1 + 4 * n]
        g_v, gc_v, last_v = refs[3 * n + 1 + 4 * n:]
        g, gc, last = r_ref[0], rc_ref[0], l_ref[0]
        for d in range(1, N_DEV):
            g, gc, last = g + r_ref[d], gc + rc_ref[d], last + l_ref[d]
        g_v[...], gc_v[...], last_v[...] = g, gc, last
        loss_ref[...] = (0.5 / d_model) * jnp.sum(g_v[LOSS_ROW:LOSS_ROW + 1, :], axis=-1, keepdims=True)
        for j, (row0, own_columns, (w, _, _)) in enumerate(params):
            w_ref, m_ref, v_ref = ins[3 * j:3 * j + 3]
            source = gc_v if own_columns else (last_v if row0 == 0 else g_v)
            grad = source[row0:row0 + w.shape[0], :]
            delta, m_new, v_new = _adamw(w_ref[...], grad, m_ref[...], v_ref[...])
            for ref, val in zip(outs[4 * j:4 * j + 4], (grad, delta, m_new, v_new)):
                ref[...] = val

    full = lambda shape: pl.BlockSpec(shape, lambda i, me_ref: (0,) * len(shape))
    stack_rows = recv_small.shape[1]
    flat = [a for _, _, triple in params for a in triple]
    shapes = [w.shape for _, _, (w, _, _) in params for _ in range(4)]
    out = pl.pallas_call(
        body, name="adam_small",
        grid_spec=pltpu.PrefetchScalarGridSpec(
            num_scalar_prefetch=1, grid=(1,),
            in_specs=[full(recv_small.shape),
                      pl.BlockSpec((N_DEV, stack_rows, cw), lambda i, me_ref: (0, 0, me_ref[0])),
                      full(recv_last.shape)] + [full(a.shape) for a in flat],
            out_specs=[full((1, 1))] + [full(s) for s in shapes],
            scratch_shapes=[pltpu.VMEM((stack_rows, d_model), F32), pltpu.VMEM((stack_rows, cw), F32),
                            pltpu.VMEM(recv_last.shape[1:], F32)]),
        out_shape=[jax.ShapeDtypeStruct((1, 1), F32)] + [jax.ShapeDtypeStruct(s, F32) for s in shapes],
    )(me, recv_small, recv_small, recv_last, *flat)
    return out[0], [tuple(out[1 + 4 * j:5 + 4 * j]) for j in range(n)]


def _tile(n_tok, want):
    return min(want, n_tok)


def kernel(x, norm1_pre_g, w_in, b_in, conv_a_w, conv_a_b, w_a_out, conv_b_w, conv_b_b, ln_b_g, ln_b_b, w_b_out, w_o, norm1_post_g, norm2_pre_g, w_mlp_in, w_mlp_out, norm2_post_g, loss_target, m_norm1_pre_g, m_w_in, m_b_in, m_conv_a_w, m_conv_a_b, m_w_a_out, m_conv_b_w, m_conv_b_b, m_ln_b_g, m_ln_b_b, m_w_b_out, m_w_o, m_norm1_post_g, m_norm2_pre_g, m_w_mlp_in, m_w_mlp_out, m_norm2_post_g, v_norm1_pre_g, v_w_in, v_b_in, v_conv_a_w, v_conv_a_b, v_w_a_out, v_conv_b_w, v_conv_b_b, v_ln_b_g, v_ln_b_b, v_w_b_out, v_w_o, v_norm1_post_g, v_norm2_pre_g, v_w_mlp_in, v_w_mlp_out, v_norm2_post_g):
    n_tok, dm = x.shape[1], x.shape[2]
    rows, off, slab_rows = _layout(dm)
    cw = dm // N_DEV
    xs = x.reshape(n_tok, dm)
    tgt = loss_target.reshape(n_tok, dm)
    row = lambda vec: vec.reshape(1, -1)
    gathered = lambda group: (N_DEV, slab_rows[group], dm)
    scattered = lambda group: jax.ShapeDtypeStruct((len(FLIPS), slab_rows[group], dm), BF16)
    tm, tk = min(dm, 1024), _tile(n_tok, 2048)
    me = (4 * lax.axis_index("x") + 2 * lax.axis_index("y") + lax.axis_index("c")).astype(jnp.int32).reshape(1)

    conv_own = jnp.concatenate([conv_a_w, jnp.zeros((SUBLANE - CONV_A, cw), F32), conv_b_w,
                                jnp.zeros((1, cw), F32)], axis=0)
    own_in = w_in.T.astype(BF16)
    slab_in, conv_all = _all_gather_two_level(
        [own_in, conv_own], [_place_own(own_in, gathered("in"), me, "place_w_in"), None], "gather_w_in")
    conv_full = conv_all.transpose(1, 0, 2).reshape(conv_own.shape[0], dm)
    caw, cbw = conv_full[0:CONV_A], conv_full[SUBLANE:SUBLANE + CONV_B]
    own_abo = jnp.concatenate([w_a_out, w_b_out, w_o], axis=0).astype(BF16)
    own_mlp = jnp.concatenate([w_mlp_in.T, w_mlp_out], axis=0).astype(BF16)
    ag_abo = _exchange_start([_Part(own_abo, False, slab_rows["abo"], 0, 0)],
                             [_place_own(own_abo, gathered("abo"), me, "place_abo")],
                             "gather_abo_start", after=slab_in)
    ag_mlp = _exchange_start([_Part(own_mlp, False, slab_rows["mlp"], 0, 0)],
                             [_place_own(own_mlp, gathered("mlp"), me, "place_mlp")],
                             "gather_mlp_start", after=ag_abo.token)

    proj, p, u, h = _fwd_in(xs, row(norm1_pre_g), row(b_in), slab_in, _tile(n_tok, 512),
                            [ag_abo.token, ag_mlp.token])
    _, (slab_abo,) = _exchange_wait(ag_abo, "gather_abo_wait", after=proj)
    va, vb, ya, yb, qa, sb, merged, mix, x1 = _fwd_mix(
        p, u, proj, xs, caw, row(conv_a_b), cbw, row(conv_b_b), row(ln_b_g), row(ln_b_b), row(norm1_post_g),
        slab_abo, _tile(n_tok, 256))
    _, (slab_mlp,) = _exchange_wait(ag_mlp, "gather_mlp_wait", after=x1)
    f, df1, h2, df2, dmix, dx1, small_mlp = _mlp_fwd_bwd(
        x1, mix, tgt, row(norm1_post_g), row(norm2_pre_g), row(norm2_post_g), slab_mlp, _tile(n_tok, 256))

    rs_mlp = _exchange_start(
        [_Part(_wgrad(df1, h2, "wgrad_mlp_in", tm, tk, BF16), True, rows["w1"], 0, off["w1"]),
         _Part(_wgrad(f, df2, "wgrad_mlp_out", tm, tk, BF16), True, rows["w2"], 0, off["w2"])],
        [scattered("mlp")], "scatter_mlp_start")
    dpa, dya, dyb, dva, dvb, small_mix = _bwd_mix(
        dmix, ya, yb, proj, va, vb, row(ln_b_g), row(ln_b_b), slab_abo, _tile(n_tok, 256), [rs_mlp.token])
    rs_abo = _exchange_start(
        [_Part(_wgrad(qa, dya, "wgrad_a_out", tm, tk, BF16), True, rows["wa"], 0, off["wa"]),
         _Part(_wgrad(sb, dyb, "wgrad_b_out", tm, tk, BF16), True, rows["wb"], 0, off["wb"]),
         _Part(_wgrad(merged, dmix, "wgrad_o", tm, tk, BF16), True, rows["wo"], 0, off["wo"])],
        [scattered("abo")], "scatter_abo_start")
    dproj, small_conv = _bwd_conv(dva, dvb, p, u, proj, dpa, caw, cbw, _tile(n_tok, 256), [rs_abo.token])

    zeros = lambda r: jnp.zeros((r, dm), F32)
    small = jnp.concatenate([
        zeros(1),
        small_mix[2:3],
        small_mix[3:4],
        small_mix[0:2],
        small_mlp[2:3],
        small_mlp[1:2],
        small_mlp[0:1],
        small_mix[4:5], small_conv[3:7], small_mix[5:7],
        small_mlp[3:4],
        small_conv[0:CONV_A], zeros(SUBLANE - CONV_A),
        small_conv[8:8 + CONV_B], zeros(1),
    ], axis=0)

    rs_in = _exchange_start(
        [_Part(_wgrad(dproj, h, "wgrad_in", tm, tk, BF16), True, rows["win"], 0, off["win"]),
         _Part(small, False, small.shape[0], 1, 0)],
        [scattered("in"), _place_own(small, (N_DEV,) + small.shape, me, "place_small")], "scatter_in_start")
    grad_x, small_in = _bwd_in(dproj, xs, dx1, row(norm1_pre_g), slab_in, _tile(n_tok, 256), [rs_in.token])

    tr = min(LANE, rows["wa"])
    (g_w1, g_w2), (recv_mlp,) = _exchange_wait(rs_mlp, "scatter_mlp_wait", after=grad_x)
    (g_wa, g_wb, g_wo), (recv_abo,) = _exchange_wait(rs_abo, "scatter_abo_wait", after=grad_x)
    big = {
        "w_mlp_in": _adam_big(recv_mlp, g_w1, me, off["w1"], rows["w1"], w_mlp_in, m_w_mlp_in, v_w_mlp_in, True,
                              "adam_w_mlp_in", tr),
        "w_mlp_out": _adam_big(recv_mlp, g_w2, me, off["w2"], rows["w2"], w_mlp_out, m_w_mlp_out, v_w_mlp_out,
                               False, "adam_w_mlp_out", tr),
        "w_a_out": _adam_big(recv_abo, g_wa, me, off["wa"], rows["wa"], w_a_out, m_w_a_out, v_w_a_out, False,
                             "adam_w_a_out", tr),
        "w_b_out": _adam_big(recv_abo, g_wb, me, off["wb"], rows["wb"], w_b_out, m_w_b_out, v_w_b_out, False,
                             "adam_w_b_out", tr),
        "w_o": _adam_big(recv_abo, g_wo, me, off["wo"], rows["wo"], w_o, m_w_o, v_w_o, False, "adam_w_o", tr),
    }
    (g_win, _), (recv_in, recv_small) = _exchange_wait(rs_in, "scatter_in_wait", after=big["w_o"][3])
    recv_last, = _all_gather([small_in], "gather_last")
    big["w_in"] = _adam_big(recv_in, g_win, me, off["win"], rows["win"], w_in, m_w_in, v_w_in, True, "adam_w_in", tr)

    small_names = ("norm1_pre_g", "conv_a_b", "conv_b_b", "ln_b_g", "ln_b_b", "norm1_post_g", "norm2_pre_g",
                   "norm2_post_g")
    given = dict(
        norm1_pre_g=(norm1_pre_g, m_norm1_pre_g, v_norm1_pre_g), conv_a_b=(conv_a_b, m_conv_a_b, v_conv_a_b),
        conv_b_b=(conv_b_b, m_conv_b_b, v_conv_b_b), ln_b_g=(ln_b_g, m_ln_b_g, v_ln_b_g),
        ln_b_b=(ln_b_b, m_ln_b_b, v_ln_b_b), norm1_post_g=(norm1_post_g, m_norm1_post_g, v_norm1_post_g),
        norm2_pre_g=(norm2_pre_g, m_norm2_pre_g, v_norm2_pre_g),
        norm2_post_g=(norm2_post_g, m_norm2_post_g, v_norm2_post_g))
    params = [(j, False, tuple(row(a) for a in given[name])) for j, name in enumerate(small_names)]
    params.append((SUBLANE, False, tuple(a.reshape(7, dm) for a in (b_in, m_b_in, v_b_in))))
    params.append((CONV_A_ROW, True, (conv_a_w, m_conv_a_w, v_conv_a_w)))
    params.append((CONV_B_ROW, True, (conv_b_w, m_conv_b_w, v_conv_b_w)))
    loss, small_out = _adam_small(recv_small, recv_last, me, params, dm)
    small_leaves = {name: tuple(a.reshape(dm) for a in small_out[j]) for j, name in enumerate(small_names)}
    small_leaves["b_in"] = tuple(a.reshape(7 * dm) for a in small_out[len(small_names)])
    small_leaves["conv_a_w"] = small_out[len(small_names) + 1]
    small_leaves["conv_b_w"] = small_out[len(small_names) + 2]

    order = ("norm1_pre_g", "w_in", "b_in", "conv_a_w", "conv_a_b", "w_a_out", "conv_b_w", "conv_b_b", "ln_b_g",
             "ln_b_b", "w_b_out", "w_o", "norm1_post_g", "norm2_pre_g", "w_mlp_in", "w_mlp_out", "norm2_post_g")
    leaves = [big[name] if name in big else small_leaves[name] for name in order]
    grads, deltas, new_m, new_v = zip(*leaves)
    return (loss.reshape(()), grad_x.reshape(x.shape), *grads, *deltas, *new_m, *new_v)
```

```python
from typing import NamedTuple

import jax
import jax.numpy as jnp
from jax import lax
from jax.experimental import pallas as pl
from jax.experimental.pallas import tpu as pltpu

F32 = jnp.float32
BF16 = jnp.bfloat16

RMS_EPS = 1e-6
LN_EPS = 1e-5
ADAM_LR = 0.001
ADAM_B1 = 0.9
ADAM_B2 = 0.999
ADAM_EPS = 1e-08
ADAM_WD = 0.01
ADAM_STEP = 10

N_DEV = 8
CONV_A = 3
CONV_B = 31
LANE = 128
SUBLANE = 8
HALO = 16
CONV_ROWS = 64
MIB = 1 << 20
FLIPS = ((0, 0, 1), (0, 1, 0), (1, 0, 0), (0, 1, 1), (1, 0, 1), (1, 1, 0), (1, 1, 1))
MESH = pl.DeviceIdType.MESH


def _layout(d_model):
    e = d_model // N_DEV
    rows = {"win": 7 * e, "w1": 4 * e, "w2": 4 * e, "wa": e, "wb": e, "wo": e}
    off = {"win": 0, "w1": 0, "w2": 4 * e, "wa": 0, "wb": e, "wo": 2 * e}
    return rows, off, {"in": 7 * e, "mlp": 8 * e, "abo": 3 * e}


def _after(body, deps):
    def ordered(*refs):
        return body(*refs[len(deps):])
    return ordered


def _params(vmem_mib):
    return pltpu.CompilerParams(dimension_semantics=("arbitrary",), vmem_limit_bytes=vmem_mib * MIB)


def _whole():
    return pl.BlockSpec(memory_space=pltpu.VMEM)


def _hbm():
    return pl.BlockSpec(memory_space=pl.ANY)


def _rows(ts, width, col=0):
    return pl.BlockSpec((ts, width), lambda i: (i, col))


def _halo_specs(ts, width, n_rows):
    per = ts // HALO
    last = n_rows // HALO - 1
    return [
        pl.BlockSpec((ts, width), lambda i: (i, 0)),
        pl.BlockSpec((HALO, width), lambda i: (jnp.maximum(i * per - 1, 0), 0)),
        pl.BlockSpec((HALO, width), lambda i: (jnp.minimum((i + 1) * per, last), 0)),
    ]


def _dot(a, b):
    return jnp.dot(a, b, preferred_element_type=F32)


def _dot_nt(a, b):
    return lax.dot_general(a, b, (((1,), (1,)), ((), ())), preferred_element_type=F32)


def _dot_tn(a, b):
    return lax.dot_general(a, b, (((0,), (0,)), ((), ())), preferred_element_type=F32)


def _rms(u):
    return lax.rsqrt(jnp.mean(u * u, axis=-1, keepdims=True) + RMS_EPS)


def _rms_bwd(dz, u, r, g):
    dzg = dz * g
    return r * dzg - u * (r * r * r) * jnp.mean(dzg * u, axis=-1, keepdims=True)


def _colsum(v):
    return jnp.sum(v, axis=0, keepdims=True)


def _sigmoid(v):
    return jax.nn.sigmoid(v)


def _weight_copies(slab_ref, off, rows, dst_ref, sems, first_sem):
    return [pltpu.make_async_copy(slab_ref.at[d, pl.ds(off, rows), :], dst_ref.at[pl.ds(d * rows, rows), :],
                                  sems.at[first_sem + d]) for d in range(N_DEV)]


def _on_first_step(copies, method):
    @pl.when(pl.program_id(0) == 0)
    def _():
        for cp in copies:
            getattr(cp, method)()


def _with_halos(main_ref, prev_ref, next_ref, i, n_steps):
    return (main_ref, jnp.where(i > 0, prev_ref[...], 0.0), jnp.where(i < n_steps - 1, next_ref[...], 0.0))


def _window(tile, r0, cb, ts):
    main_ref, prev, nxt = tile
    lanes = slice(cb * LANE, (cb + 1) * LANE)
    lo, hi = max(r0 - HALO, 0), min(r0 + CONV_ROWS + HALO, ts)
    pieces = [prev[:, lanes]] if r0 - HALO < 0 else []
    pieces.append(main_ref[lo:hi, lanes])
    if r0 + CONV_ROWS + HALO > ts:
        pieces.append(nxt[:, lanes])
    return pieces[0] if len(pieces) == 1 else jnp.concatenate(pieces, axis=0)


def _broadcast_taps(w_ref, wb_ref, n_taps):
    for k in range(n_taps):
        wb_ref[k] = jnp.broadcast_to(w_ref[k:k + 1, :], wb_ref.shape[1:])


def _phases(starts):
    groups = {}
    for k, st in enumerate(starts):
        groups.setdefault(st % SUBLANE, []).append((k, st // SUBLANE))
    return sorted(groups.items())


def _shifted(blk, b):
    n = blk.shape[0]
    rolled = blk if b == 0 else pltpu.roll(blk, n - b, axis=0)
    return rolled.reshape(n // SUBLANE, SUBLANE, blk.shape[1])


def _conv_tile(tile, wb_ref, starts, ts, width, emit):
    groups = _phases(starts)
    nv = CONV_ROWS // SUBLANE
    for r0 in range(0, ts, CONV_ROWS):
        for cb in range(width // LANE):
            lanes = pl.ds(cb * LANE, LANE)
            blk = _window(tile, r0, cb, ts)
            acc = jnp.zeros((nv, SUBLANE, LANE), F32)
            for b, taps in groups:
                sh = _shifted(blk, b)
                for k, m in taps:
                    acc = acc + sh[m:m + nv] * wb_ref[k, :, lanes][None]
            emit(r0, lanes, acc.reshape(CONV_ROWS, LANE))


def _conv_bwd_tile(dv_tile, u_ref, wb_ref, acc_ref, n_taps, ts, width, emit):
    groups = _phases(_bwd_starts(n_taps))
    nv = CONV_ROWS // SUBLANE
    for r0 in range(0, ts, CONV_ROWS):
        for cb in range(width // LANE):
            lanes = pl.ds(cb * LANE, LANE)
            blk = _window(dv_tile, r0, cb, ts)
            u = u_ref[pl.ds(r0, CONV_ROWS), lanes].reshape(nv, SUBLANE, LANE)
            du = jnp.zeros((nv, SUBLANE, LANE), F32)
            for b, taps in groups:
                sh = _shifted(blk, b)
                for k, m in taps:
                    du = du + sh[m:m + nv] * wb_ref[k, :, lanes][None]
                    acc_ref[k, :, lanes] += jnp.sum(sh[m:m + nv] * u, axis=0)
            emit(r0, lanes, du.reshape(CONV_ROWS, LANE))


def _fwd_starts(n_taps):
    pad = (n_taps - 1) // 2
    return [HALO - pad + k for k in range(n_taps)]


def _bwd_starts(n_taps):
    pad = (n_taps - 1) // 2
    return [HALO + pad - k for k in range(n_taps)]


def _peer(x, y, c, flip):
    fx, fy, fc = flip
    return (1 - x if fx else x, 1 - y if fy else y, 1 - c if fc else c)


def _all_gather(shards, name):
    n = len(shards)

    def body(*refs):
        ins, outs = refs[:n], refs[n:2 * n]
        send_sems, recv_sems, local_sems = refs[2 * n:]
        x, y, c = lax.axis_index("x"), lax.axis_index("y"), lax.axis_index("c")
        me = 4 * x + 2 * y + c
        local = [pltpu.make_async_copy(ins[j], outs[j].at[me], local_sems.at[j]) for j in range(n)]
        for cp in local:
            cp.start()
        sends, recvs = [], []
        for k, flip in enumerate(FLIPS):
            px, py, pc = _peer(x, y, c, flip)
            peer = 4 * px + 2 * py + pc
            for j in range(n):
                sem = k * n + j
                sends.append(pltpu.make_async_remote_copy(
                    src_ref=ins[j], dst_ref=outs[j].at[me], send_sem=send_sems.at[sem], recv_sem=recv_sems.at[sem],
                    device_id=(px, py, pc), device_id_type=MESH))
                recvs.append(pltpu.make_async_remote_copy(
                    src_ref=ins[j], dst_ref=outs[j].at[peer], send_sem=send_sems.at[sem], recv_sem=recv_sems.at[sem],
                    device_id=(px, py, pc), device_id_type=MESH))
        for cp in sends:
            cp.start()
        for cp in recvs:
            cp.wait_recv()
        for cp in sends:
            cp.wait_send()
        for cp in local:
            cp.wait()

    return pl.pallas_call(
        body, name=name,
        out_shape=[jax.ShapeDtypeStruct((N_DEV,) + s.shape, s.dtype) for s in shards],
        in_specs=[_hbm()] * n, out_specs=[_hbm()] * n,
        scratch_shapes=[pltpu.SemaphoreType.DMA((7 * n,)), pltpu.SemaphoreType.DMA((7 * n,)),
                        pltpu.SemaphoreType.DMA((n,))],
    )(*shards)


def _place_own(src, n_slots_shape, me, name):
    rows, width = src.shape
    tr = next(t for t in (256, 128, 64, 32, 16, SUBLANE) if rows % t == 0)

    def body(me_ref, src_ref, out_ref):
        out_ref[...] = src_ref[...]

    return pl.pallas_call(
        body, name=name,
        grid_spec=pltpu.PrefetchScalarGridSpec(
            num_scalar_prefetch=1, grid=(rows // tr,),
            in_specs=[pl.BlockSpec((tr, width), lambda i, me_ref: (i, 0))],
            out_specs=pl.BlockSpec((None, tr, width), lambda i, me_ref: (me_ref[0], i, 0))),
        out_shape=jax.ShapeDtypeStruct(n_slots_shape, src.dtype),
    )(me, src)


def _all_gather_two_level(shards, placed, name):
    n = len(shards)
    given = [j for j in range(n) if placed[j] is not None]

    def body(*refs):
        ins, outs = refs[:n], refs[n + len(given):2 * n + len(given)]
        send_sems, recv_sems, local_sems = refs[2 * n + len(given):]
        x, y, c = lax.axis_index("x"), lax.axis_index("y"), lax.axis_index("c")
        me, sibling = (x, y, c), (x, y, 1 - c)
        chips = [(1 - x, y), (x, 1 - y), (1 - x, 1 - y)]

        def slot(j, dev):
            return outs[j].at[4 * dev[0] + 2 * dev[1] + dev[2]]

        def copy(k, j, block, to, src=None):
            return pltpu.make_async_remote_copy(
                src_ref=slot(j, block) if src is None else src, dst_ref=slot(j, block),
                send_sem=send_sems.at[k * n + j], recv_sem=recv_sems.at[k * n + j], device_id=to, device_id_type=MESH)

        local = [pltpu.make_async_copy(ins[j], slot(j, me), local_sems.at[j]) for j in range(n) if j not in given]
        for cp in local:
            cp.start()
        first = [copy(0, j, me, sibling, src=ins[j]) for j in range(n)]
        first += [copy(1 + t, j, me, (*chip, c), src=ins[j]) for t, chip in enumerate(chips) for j in range(n)]
        for cp in first:
            cp.start()
        passed = []
        for t, chip in enumerate(chips):
            for j in range(n):
                copy(1 + t, j, (*chip, c), me).wait_recv()
                passed.append(copy(4 + t, j, (*chip, c), sibling))
                passed[-1].start()
        for j in range(n):
            copy(0, j, sibling, me).wait_recv()
        for t, chip in enumerate(chips):
            for j in range(n):
                copy(4 + t, j, (*chip, 1 - c), me).wait_recv()
        for cp in first + passed:
            cp.wait_send()
        for cp in local:
            cp.wait()

    return pl.pallas_call(
        body, name=name,
        out_shape=[jax.ShapeDtypeStruct((N_DEV,) + s.shape, s.dtype) for s in shards],
        in_specs=[_hbm()] * (n + len(given)), out_specs=[_hbm()] * n,
        input_output_aliases={n + i: j for i, j in enumerate(given)},
        scratch_shapes=[pltpu.SemaphoreType.DMA((7 * n,)), pltpu.SemaphoreType.DMA((7 * n,)),
                        pltpu.SemaphoreType.DMA((n,))],
    )(*shards, *[placed[j] for j in given])


class _Part(NamedTuple):
    src: jax.Array
    scatter: bool
    rows: int
    land: int
    off: int


class _Started(NamedTuple):
    send_sems: jax.Array
    recv_sems: jax.Array
    thru: tuple
    token: jax.Array
    parts: tuple


def _exchange_copies(srcs, lands, send_sems, recv_sems, parts):
    n = len(parts)
    x, y, c = lax.axis_index("x"), lax.axis_index("y"), lax.axis_index("c")
    me = 4 * x + 2 * y + c

    def block(j, dev):
        p = parts[j]
        return srcs[j].at[pl.ds(pl.multiple_of(dev * p.rows, SUBLANE), p.rows), :] if p.scatter else srcs[j]

    def slot(j, index):
        p = parts[j]
        return lands[p.land].at[index, pl.ds(p.off, p.rows), :]

    sends, recvs = [], []
    for k, flip in enumerate(FLIPS):
        px, py, pc = _peer(x, y, c, flip)
        peer = 4 * px + 2 * py + pc
        for j in range(n):
            sems = dict(send_sem=send_sems.at[k * n + j], recv_sem=recv_sems.at[k * n + j],
                        device_id=(px, py, pc), device_id_type=MESH)
            to, got = (k, k) if parts[j].scatter else (me, peer)
            sends.append(pltpu.make_async_remote_copy(src_ref=block(j, peer), dst_ref=slot(j, to), **sems))
            recvs.append(pltpu.make_async_remote_copy(src_ref=block(j, peer), dst_ref=slot(j, got), **sems))
    return sends, recvs


def _exchange_start(parts, lands, name, after=None):
    n, nl = len(parts), len(lands)
    n_in = n + nl + (after is not None)

    def body(*refs):
        srcs, land_refs = refs[:n], refs[n:n + nl]
        send_sems, recv_sems = refs[n_in], refs[n_in + 1]
        token = refs[n_in + 2 + n + nl]
        sends, _ = _exchange_copies(srcs, land_refs, send_sems, recv_sems, parts)
        for cp in sends:
            cp.start()
        token[...] = jnp.zeros_like(token)

    hbm = pl.BlockSpec(memory_space=pltpu.HBM)
    sem = pl.BlockSpec(memory_space=pltpu.SEMAPHORE)
    fresh = lambda s: lax.empty(s.shape, s.dtype) if isinstance(s, jax.ShapeDtypeStruct) else s
    args = [pltpu.with_memory_space_constraint(p.src, pltpu.HBM) for p in parts]
    args += [pltpu.with_memory_space_constraint(fresh(s), pltpu.HBM) for s in lands]
    args += [] if after is None else [after]
    out = pl.pallas_call(
        body, name=name,
        out_shape=(pltpu.SemaphoreType.DMA((7 * n,)), pltpu.SemaphoreType.DMA((7 * n,)),
                   *[pltpu.HBM(a.shape, a.dtype) for a in args[:n + nl]], jax.ShapeDtypeStruct((SUBLANE, LANE), F32)),
        in_specs=[hbm] * (n + nl) + [_hbm()] * (after is not None),
        out_specs=(sem, sem, *[hbm] * (n + nl), _whole()),
        input_output_aliases={j: 2 + j for j in range(n + nl)},
        compiler_params=pltpu.CompilerParams(has_side_effects=pltpu.SideEffectType.DATAFLOW_SIDE_EFFECTING),
    )(*args)
    return _Started(out[0], out[1], tuple(out[2:2 + n + nl]), out[2 + n + nl], tuple(parts))


def _exchange_wait(started, name, after):
    parts = started.parts
    n, nl = len(parts), len(started.thru) - len(parts)

    def body(*refs):
        srcs, land_refs = refs[:n], refs[n:n + nl]
        send_sems, recv_sems = refs[n + nl], refs[n + nl + 1]
        sends, recvs = _exchange_copies(srcs, land_refs, send_sems, recv_sems, parts)
        for cp in sends:
            cp.wait_send()
        for cp in recvs:
            cp.wait_recv()

    hbm = pl.BlockSpec(memory_space=pltpu.HBM)
    sem = pl.BlockSpec(memory_space=pltpu.SEMAPHORE)
    out = pl.pallas_call(
        body, name=name,
        out_shape=tuple(pltpu.HBM(a.shape, a.dtype) for a in started.thru),
        in_specs=[hbm] * (n + nl) + [sem, sem, _hbm()], out_specs=tuple([hbm] * (n + nl)),
        input_output_aliases={j: j for j in range(n + nl)},
        compiler_params=pltpu.CompilerParams(has_side_effects=pltpu.SideEffectType.DATAFLOW_SIDE_EFFECTING),
    )(*started.thru, started.send_sems, started.recv_sems, after)
    return list(out[:n]), list(out[n:])


def _fwd_in(x, g1, b_in, slab, ts, deps):
    n_tok, dm = x.shape
    rows, off, _ = _layout(dm)
    width = 7 * dm

    def body(x_ref, g1_ref, b_ref, slab_ref, proj_ref, p_ref, u_ref, h_ref, w_v, sems):
        copies = _weight_copies(slab_ref, off["win"], rows["win"], w_v, sems, 0)
        _on_first_step(copies, "start")
        _on_first_step(copies, "wait")
        xv = x_ref[...]
        h = (xv * _rms(xv) * g1_ref[...]).astype(BF16)
        h_ref[...] = h
        cols = []
        for j in range(7):
            pj = _dot_nt(h, w_v[pl.ds(j * dm, dm), :]) + b_ref[:, j * dm:(j + 1) * dm]
            proj_ref[:, j * dm:(j + 1) * dm] = pj.astype(proj_ref.dtype)
            if 1 <= j <= 4:
                cols.append(pj)
            if j == 2:
                p_ref[...] = cols[0] * cols[1]
            if j == 4:
                u_ref[...] = cols[2] * _sigmoid(cols[3])

    return pl.pallas_call(
        _after(body, deps), name="fwd_in", grid=(n_tok // ts,),
        in_specs=[_whole()] * len(deps) + [_rows(ts, dm), _whole(), _whole(), _hbm()],
        out_specs=[_rows(ts, width), _rows(ts, dm), _rows(ts, dm), _rows(ts, dm)],
        out_shape=[jax.ShapeDtypeStruct((n_tok, width), BF16), jax.ShapeDtypeStruct((n_tok, dm), F32),
                   jax.ShapeDtypeStruct((n_tok, dm), F32), jax.ShapeDtypeStruct((n_tok, dm), BF16)],
        scratch_shapes=[pltpu.VMEM((width, dm), BF16), pltpu.SemaphoreType.DMA((N_DEV,))],
        compiler_params=_params(56),
    )(*deps, x, g1, b_in, slab)


def _fwd_mix(p, u, proj, x, caw, cab, cbw, cbb, lng, lnb, g1post, slab, ts):
    n_tok, dm = x.shape
    rows, off, _ = _layout(dm)
    n_steps = n_tok // ts

    def body(p_ref, p_prev, p_next, u_ref, u_prev, u_next, bg_ref, za_ref, zb_ref, x_ref,
             caw_ref, cab_ref, cbw_ref, cbb_ref, lng_ref, lnb_ref, g1p_ref, slab_ref,
             va_ref, vb_ref, ya_ref, yb_ref, qa_ref, sb_ref, mg_ref, mix_ref, x1_ref,
             wa_v, wb_v, wo_v, tap_a, tap_b, sems):
        i = pl.program_id(0)
        copies = (_weight_copies(slab_ref, off["wa"], rows["wa"], wa_v, sems, 0)
                  + _weight_copies(slab_ref, off["wb"], rows["wb"], wb_v, sems, N_DEV)
                  + _weight_copies(slab_ref, off["wo"], rows["wo"], wo_v, sems, 2 * N_DEV))
        _on_first_step(copies, "start")

        @pl.when(i == 0)
        def _():
            _broadcast_taps(caw_ref, tap_a, CONV_A)
            _broadcast_taps(cbw_ref, tap_b, CONV_B)

        def emit_a(r0, lanes, acc):
            va_ref[pl.ds(r0, CONV_ROWS), lanes] = acc + cab_ref[:, lanes]

        def emit_b(r0, lanes, acc):
            vb_ref[pl.ds(r0, CONV_ROWS), lanes] = acc + cbb_ref[:, lanes]

        _conv_tile(_with_halos(p_ref, p_prev, p_next, i, n_steps), tap_a, _fwd_starts(CONV_A), ts, dm, emit_a)
        _conv_tile(_with_halos(u_ref, u_prev, u_next, i, n_steps), tap_b, _fwd_starts(CONV_B), ts, dm, emit_b)
        _on_first_step(copies, "wait")

        qa = (bg_ref[...].astype(F32) * va_ref[...]).astype(BF16)
        qa_ref[...] = qa
        ya = _dot(qa, wa_v[...])
        vb = vb_ref[...]
        xc = vb - jnp.mean(vb, axis=-1, keepdims=True)
        rstd = lax.rsqrt(jnp.mean(xc * xc, axis=-1, keepdims=True) + LN_EPS)
        ln = xc * rstd * lng_ref[...] + lnb_ref[...]
        sb = (ln * _sigmoid(ln)).astype(BF16)
        sb_ref[...] = sb
        yb = _dot(sb, wb_v[...])
        ya_ref[...] = ya.astype(BF16)
        yb_ref[...] = yb.astype(BF16)
        merged = (_sigmoid(za_ref[...].astype(F32)) * ya + _sigmoid(zb_ref[...].astype(F32)) * yb).astype(BF16)
        mg_ref[...] = merged
        mix = _dot(merged, wo_v[...])
        mix_ref[...] = mix
        x1_ref[...] = x_ref[...] + mix * _rms(mix) * g1p_ref[...]

    tok = lambda dt: jax.ShapeDtypeStruct((n_tok, dm), dt)
    return pl.pallas_call(
        body, name="fwd_mix", grid=(n_steps,),
        in_specs=(_halo_specs(ts, dm, n_tok) + _halo_specs(ts, dm, n_tok)
                  + [_rows(ts, dm, 0), _rows(ts, dm, 5), _rows(ts, dm, 6), _rows(ts, dm)]
                  + [_whole()] * 7 + [_hbm()]),
        out_specs=[_rows(ts, dm)] * 9,
        out_shape=[tok(F32), tok(F32), tok(BF16), tok(BF16), tok(BF16), tok(BF16), tok(BF16), tok(F32), tok(F32)],
        scratch_shapes=[pltpu.VMEM((dm, dm), BF16), pltpu.VMEM((dm, dm), BF16), pltpu.VMEM((dm, dm), BF16),
                        pltpu.VMEM((CONV_A, SUBLANE, dm), F32), pltpu.VMEM((CONV_B, SUBLANE, dm), F32),
                        pltpu.SemaphoreType.DMA((3 * N_DEV,))],
        compiler_params=_params(48),
    )(p, p, p, u, u, u, proj, proj, proj, x, caw, cab, cbw, cbb, lng, lnb, g1post, slab)


def _mlp_fwd_bwd(x1, mix, tgt, g1post, g2pre, g2post, slab, ts):
    n_tok, dm = x1.shape
    rows, off, _ = _layout(dm)
    ff = 4 * dm

    def body(x1_ref, mix_ref, t_ref, g1p_ref, g2pre_ref, g2post_ref, slab_ref,
             f_ref, df1_ref, h2_ref, df2_ref, dmix_ref, dx1_ref, small_ref, w1_v, w2_v, relu_v, sems):
        w1_copies = _weight_copies(slab_ref, off["w1"], rows["w1"], w1_v, sems, 0)
        w2_copies = _weight_copies(slab_ref, off["w2"], rows["w2"], w2_v, sems, N_DEV)
        _on_first_step(w1_copies + w2_copies, "start")

        @pl.when(pl.program_id(0) == 0)
        def _():
            small_ref[...] = jnp.zeros_like(small_ref)

        _on_first_step(w1_copies + w2_copies, "wait")
        x1v = x1_ref[...]
        r3 = _rms(x1v)
        g2pre = g2pre_ref[...]
        h2 = (x1v * r3 * g2pre).astype(BF16)
        h2_ref[...] = h2
        f2 = jnp.zeros((ts, dm), F32)
        for c in range(4):
            blk = pl.ds(c * dm, dm)
            relu = jnp.maximum(_dot_nt(h2, w1_v[blk, :]), 0.0)
            relu_v[:, c * dm:(c + 1) * dm] = relu
            fc = (relu * relu).astype(BF16)
            f_ref[:, c * dm:(c + 1) * dm] = fc
            f2 = f2 + _dot(fc, w2_v[blk, :])
        r4 = _rms(f2)
        g2post = g2post_ref[...]
        err = x1v + f2 * r4 * g2post - t_ref[...]
        dy = err * (1.0 / dm)
        small_ref[3:4, :] += _colsum(err * err)
        small_ref[0:1, :] += _colsum(dy * f2 * r4)
        df2 = _rms_bwd(dy, f2, r4, g2post).astype(BF16)
        df2_ref[...] = df2
        dh2 = jnp.zeros((ts, dm), F32)
        for c in range(4):
            blk = pl.ds(c * dm, dm)
            df1 = (_dot_nt(df2, w2_v[blk, :]) * (2.0 * relu_v[:, c * dm:(c + 1) * dm])).astype(BF16)
            df1_ref[:, c * dm:(c + 1) * dm] = df1
            dh2 = dh2 + _dot(df1, w1_v[blk, :])
        small_ref[1:2, :] += _colsum(dh2 * x1v * r3)
        dx1 = dy + _rms_bwd(dh2, x1v, r3, g2pre)
        dx1_ref[...] = dx1
        mixv = mix_ref[...]
        r2 = _rms(mixv)
        small_ref[2:3, :] += _colsum(dx1 * mixv * r2)
        dmix_ref[...] = _rms_bwd(dx1, mixv, r2, g1p_ref[...]).astype(BF16)

    tok = lambda w, dt: jax.ShapeDtypeStruct((n_tok, w), dt)
    return pl.pallas_call(
        body, name="mlp_fwd_bwd", grid=(n_tok // ts,),
        in_specs=[_rows(ts, dm)] * 3 + [_whole()] * 3 + [_hbm()],
        out_specs=[_rows(ts, ff), _rows(ts, ff), _rows(ts, dm), _rows(ts, dm), _rows(ts, dm), _rows(ts, dm),
                   pl.BlockSpec((SUBLANE, dm), lambda i: (0, 0))],
        out_shape=[tok(ff, BF16), tok(ff, BF16), tok(dm, BF16), tok(dm, BF16), tok(dm, BF16), tok(dm, F32),
                   jax.ShapeDtypeStruct((SUBLANE, dm), F32)],
        scratch_shapes=[pltpu.VMEM((ff, dm), BF16), pltpu.VMEM((ff, dm), BF16), pltpu.VMEM((ts, ff), F32),
                        pltpu.SemaphoreType.DMA((2 * N_DEV,))],
        compiler_params=_params(56),
    )(x1, mix, tgt, g1post, g2pre, g2post, slab)


def _bwd_mix(dmix, ya, yb, proj, va, vb, lng, lnb, slab, ts, deps):
    n_tok, dm = dmix.shape
    rows, off, _ = _layout(dm)
    n_steps = n_tok // ts

    def body(dmix_ref, ya_ref, yb_ref, bg_ref, za_ref, zb_ref, va_ref, vb_ref, lng_ref, lnb_ref, slab_ref,
             dpa_ref, dya_ref, dyb_ref, dva_ref, dvb_ref, small_ref, wa_v, wb_v, wo_v, sems):
        wo_copies = _weight_copies(slab_ref, off["wo"], rows["wo"], wo_v, sems, 2 * N_DEV)
        ab_copies = (_weight_copies(slab_ref, off["wa"], rows["wa"], wa_v, sems, 0)
                     + _weight_copies(slab_ref, off["wb"], rows["wb"], wb_v, sems, N_DEV))
        _on_first_step(wo_copies + ab_copies, "start")

        @pl.when(pl.program_id(0) == 0)
        def _():
            small_ref[...] = jnp.zeros_like(small_ref)

        _on_first_step(wo_copies, "wait")
        dmerged = _dot_nt(dmix_ref[...], wo_v[...])
        _on_first_step(ab_copies, "wait")
        sa = _sigmoid(za_ref[...].astype(F32))
        sg = _sigmoid(zb_ref[...].astype(F32))
        dza = dmerged * ya_ref[...].astype(F32) * sa * (1.0 - sa)
        dzb = dmerged * yb_ref[...].astype(F32) * sg * (1.0 - sg)
        dpa_ref[:, dm:2 * dm] = dza.astype(BF16)
        dpa_ref[:, 2 * dm:3 * dm] = dzb.astype(BF16)
        small_ref[5:6, :] += _colsum(dza)
        small_ref[6:7, :] += _colsum(dzb)

        dya = (dmerged * sa).astype(BF16)
        dya_ref[...] = dya
        dqa = _dot_nt(dya, wa_v[...])
        dbg = dqa * va_ref[...]
        dpa_ref[:, 0:dm] = dbg.astype(BF16)
        small_ref[4:5, :] += _colsum(dbg)
        dva = dqa * bg_ref[...].astype(F32)
        dva_ref[...] = dva
        small_ref[2:3, :] += _colsum(dva)

        dyb = (dmerged * sg).astype(BF16)
        dyb_ref[...] = dyb
        dsb = _dot_nt(dyb, wb_v[...])
        vb = vb_ref[...]
        xc = vb - jnp.mean(vb, axis=-1, keepdims=True)
        rstd = lax.rsqrt(jnp.mean(xc * xc, axis=-1, keepdims=True) + LN_EPS)
        nrm = xc * rstd
        lng_v = lng_ref[...]
        ln = nrm * lng_v + lnb_ref[...]
        sl = _sigmoid(ln)
        dln = dsb * (sl * (1.0 + ln * (1.0 - sl)))
        small_ref[0:1, :] += _colsum(dln * nrm)
        small_ref[1:2, :] += _colsum(dln)
        dn = dln * lng_v
        dvb = rstd * (dn - jnp.mean(dn, axis=-1, keepdims=True)
                      - nrm * jnp.mean(dn * nrm, axis=-1, keepdims=True))
        dvb_ref[...] = dvb
        small_ref[3:4, :] += _colsum(dvb)

    tok = lambda w, dt: jax.ShapeDtypeStruct((n_tok, w), dt)
    return pl.pallas_call(
        _after(body, deps), name="bwd_mix", grid=(n_steps,),
        in_specs=([_whole()] * len(deps) + [_rows(ts, dm)] * 3
                  + [_rows(ts, dm, 0), _rows(ts, dm, 5), _rows(ts, dm, 6)]
                  + [_rows(ts, dm)] * 2 + [_whole()] * 2 + [_hbm()]),
        out_specs=[_rows(ts, 3 * dm), _rows(ts, dm), _rows(ts, dm), _rows(ts, dm), _rows(ts, dm),
                   pl.BlockSpec((SUBLANE, dm), lambda i: (0, 0))],
        out_shape=[tok(3 * dm, BF16), tok(dm, BF16), tok(dm, BF16), tok(dm, F32), tok(dm, F32),
                   jax.ShapeDtypeStruct((SUBLANE, dm), F32)],
        scratch_shapes=[pltpu.VMEM((dm, dm), BF16), pltpu.VMEM((dm, dm), BF16), pltpu.VMEM((dm, dm), BF16),
                        pltpu.SemaphoreType.DMA((3 * N_DEV,))],
        compiler_params=_params(48),
    )(*deps, dmix, ya, yb, proj, proj, proj, va, vb, lng, lnb, slab)


def _bwd_conv(dva, dvb, p, u, proj, dpa, caw, cbw, ts, deps):
    n_tok, dm = dva.shape
    n_steps = n_tok // ts
    small_rows = 40

    def body(dva_ref, dva_prev, dva_next, dvb_ref, dvb_prev, dvb_next, p_ref, u_ref,
             cg_ref, ha_ref, a_ref, g_ref, dpa_ref, caw_ref, cbw_ref,
             dproj_ref, small_ref,
             dp_v, du_v, tap_a, tap_b, gwa_v, gwb_v):
        i = pl.program_id(0)

        @pl.when(i == 0)
        def _():
            _broadcast_taps(caw_ref, tap_a, CONV_A)
            _broadcast_taps(cbw_ref, tap_b, CONV_B)
            small_ref[...] = jnp.zeros_like(small_ref)
            gwa_v[...] = jnp.zeros_like(gwa_v)
            gwb_v[...] = jnp.zeros_like(gwb_v)

        def emit_dp(r0, lanes, acc):
            dp_v[pl.ds(r0, CONV_ROWS), lanes] = acc

        def emit_du(r0, lanes, acc):
            du_v[pl.ds(r0, CONV_ROWS), lanes] = acc

        _conv_bwd_tile(_with_halos(dva_ref, dva_prev, dva_next, i, n_steps), p_ref, tap_a, gwa_v, CONV_A, ts, dm,
                       emit_dp)
        _conv_bwd_tile(_with_halos(dvb_ref, dvb_prev, dvb_next, i, n_steps), u_ref, tap_b, gwb_v, CONV_B, ts, dm,
                       emit_du)

        dp = dp_v[...]
        dcg = dp * ha_ref[...].astype(F32)
        dha = dp * cg_ref[...].astype(F32)
        du = du_v[...]
        sg = _sigmoid(g_ref[...].astype(F32))
        da = du * sg
        dg = du * a_ref[...].astype(F32) * sg * (1.0 - sg)
        dproj_ref[:, 0:dm] = dpa_ref[:, 0:dm]
        dproj_ref[:, dm:2 * dm] = dcg.astype(BF16)
        dproj_ref[:, 2 * dm:3 * dm] = dha.astype(BF16)
        dproj_ref[:, 3 * dm:4 * dm] = da.astype(BF16)
        dproj_ref[:, 4 * dm:5 * dm] = dg.astype(BF16)
        dproj_ref[:, 5 * dm:7 * dm] = dpa_ref[:, dm:3 * dm]
        small_ref[3:4, :] += _colsum(dcg)
        small_ref[4:5, :] += _colsum(dha)
        small_ref[5:6, :] += _colsum(da)
        small_ref[6:7, :] += _colsum(dg)

        @pl.when(i == n_steps - 1)
        def _():
            for k in range(CONV_A):
                small_ref[k:k + 1, :] = _colsum(gwa_v[k])
            for k in range(CONV_B):
                small_ref[SUBLANE + k:SUBLANE + k + 1, :] = _colsum(gwb_v[k])

    return pl.pallas_call(
        _after(body, deps), name="bwd_conv", grid=(n_steps,),
        in_specs=([_whole()] * len(deps) + _halo_specs(ts, dm, n_tok) * 2 + [_rows(ts, dm)] * 2
                  + [_rows(ts, dm, 1), _rows(ts, dm, 2), _rows(ts, dm, 3), _rows(ts, dm, 4), _rows(ts, 3 * dm)]
                  + [_whole()] * 2),
        out_specs=[_rows(ts, 7 * dm), pl.BlockSpec((small_rows, dm), lambda i: (0, 0))],
        out_shape=[jax.ShapeDtypeStruct((n_tok, 7 * dm), BF16), jax.ShapeDtypeStruct((small_rows, dm), F32)],
        scratch_shapes=[pltpu.VMEM((ts, dm), F32), pltpu.VMEM((ts, dm), F32),
                        pltpu.VMEM((CONV_A, SUBLANE, dm), F32), pltpu.VMEM((CONV_B, SUBLANE, dm), F32),
                        pltpu.VMEM((CONV_A, SUBLANE, dm), F32), pltpu.VMEM((CONV_B, SUBLANE, dm), F32)],
        compiler_params=_params(48),
    )(*deps, dva, dva, dva, dvb, dvb, dvb, p, u, proj, proj, proj, proj, dpa, caw, cbw)


def _bwd_in(dproj, x, dx1, g1, slab, ts, deps):
    n_tok, dm = x.shape
    rows, off, _ = _layout(dm)
    width = 7 * dm

    def body(dproj_ref, x_ref, dx1_ref, g1_ref, slab_ref, gx_ref, small_ref, w_v, sems):
        copies = _weight_copies(slab_ref, off["win"], rows["win"], w_v, sems, 0)
        _on_first_step(copies, "start")

        @pl.when(pl.program_id(0) == 0)
        def _():
            small_ref[...] = jnp.zeros_like(small_ref)

        _on_first_step(copies, "wait")
        dh = _dot(dproj_ref[...], w_v[...])
        xv = x_ref[...]
        r1 = _rms(xv)
        small_ref[0:1, :] += _colsum(dh * xv * r1)
        gx_ref[...] = dx1_ref[...] + _rms_bwd(dh, xv, r1, g1_ref[...])

    return pl.pallas_call(
        _after(body, deps), name="bwd_in", grid=(n_tok // ts,),
        in_specs=[_whole()] * len(deps) + [_rows(ts, width), _rows(ts, dm), _rows(ts, dm), _whole(), _hbm()],
        out_specs=[_rows(ts, dm), pl.BlockSpec((SUBLANE, dm), lambda i: (0, 0))],
        out_shape=[jax.ShapeDtypeStruct((n_tok, dm), F32), jax.ShapeDtypeStruct((SUBLANE, dm), F32)],
        scratch_shapes=[pltpu.VMEM((width, dm), BF16), pltpu.SemaphoreType.DMA((N_DEV,))],
        compiler_params=_params(56),
    )(*deps, dproj, x, dx1, g1, slab)


def _wgrad(a, b, name, tm, tk, out_dtype):
    n_tok, m = a.shape
    n = b.shape[1]
    k_steps = n_tok // tk

    def body(a_ref, b_ref, o_ref, acc_v):
        k = pl.program_id(1)

        @pl.when(k == 0)
        def _():
            acc_v[...] = jnp.zeros_like(acc_v)

        acc_v[...] += _dot_tn(a_ref[...], b_ref[...])

        @pl.when(k == k_steps - 1)
        def _():
            o_ref[...] = acc_v[...].astype(o_ref.dtype)

    return pl.pallas_call(
        body, name=name, grid=(m // tm, k_steps),
        in_specs=[pl.BlockSpec((tk, tm), lambda i, k: (k, i)), pl.BlockSpec((tk, n), lambda i, k: (k, 0))],
        out_specs=pl.BlockSpec((tm, n), lambda i, k: (i, 0)),
        out_shape=jax.ShapeDtypeStruct((m, n), out_dtype),
        scratch_shapes=[pltpu.VMEM((tm, n), F32)],
        compiler_params=pltpu.CompilerParams(dimension_semantics=("arbitrary", "arbitrary"),
                                             vmem_limit_bytes=40 * MIB),
    )(a, b)


def _adamw(w, g, m, v):
    m = ADAM_B1 * m + (1.0 - ADAM_B1) * g
    v = ADAM_B2 * v + (1.0 - ADAM_B2) * (g * g)
    m_hat = m / (1.0 - ADAM_B1 ** ADAM_STEP)
    v_hat = v / (1.0 - ADAM_B2 ** ADAM_STEP)
    delta = -ADAM_LR * (m_hat / (jnp.sqrt(v_hat) + ADAM_EPS) + ADAM_WD * w)
    return delta, m, v


def _adam_big(recv, part, me, off, rows, w, m, v, transpose, name, tr):
    dm = recv.shape[2]
    per = rows // tr

    def body(me_ref, own_ref, r_ref, w_ref, m_ref, v_ref, g_ref, d_ref, mo_ref, vo_ref):
        g = own_ref[...].astype(F32)
        for k in range(len(FLIPS)):
            g = g + r_ref[k].astype(F32)
        if transpose:
            g = g.T
        delta, m_new, v_new = _adamw(w_ref[...], g, m_ref[...], v_ref[...])
        g_ref[...] = g
        d_ref[...] = delta
        mo_ref[...] = m_new
        vo_ref[...] = v_new

    if transpose:
        blk = pl.BlockSpec((dm, tr), lambda i, me_ref: (0, i))
    else:
        blk = pl.BlockSpec((tr, dm), lambda i, me_ref: (i, 0))
    first = off // tr
    return pl.pallas_call(
        body, name=name,
        grid_spec=pltpu.PrefetchScalarGridSpec(
            num_scalar_prefetch=1, grid=(per,),
            in_specs=[pl.BlockSpec((tr, dm), lambda i, me_ref: (me_ref[0] * per + i, 0)),
                      pl.BlockSpec((len(FLIPS), tr, dm), lambda i, me_ref: (0, first + i, 0)), blk, blk, blk],
            out_specs=[blk] * 4),
        out_shape=[jax.ShapeDtypeStruct(w.shape, F32)] * 4,
        compiler_params=_params(32),
    )(me, part, recv, w, m, v)


LOSS_ROW = 15
CONV_A_ROW = 16
CONV_B_ROW = 24


def _adam_small(recv_small, recv_last, me, params, d_model):
    n = len(params)
    cw = d_model // N_DEV

    def body(me_ref, r_ref, rc_ref, l_ref, *refs):
        ins, loss_ref, outs = refs[:3 * n], refs[3 * n], refs[3 * n + 1:3 * n + 1 + 4 * n]
        g_v, gc_v, last_v = refs[3 * n + 1 + 4 * n:]
        g, gc, last = r_ref[0], rc_ref[0], l_ref[0]
        for d in range(1, N_DEV):
            g, gc, last = g + r_ref[d], gc + rc_ref[d], last + l_ref[d]
        g_v[...], gc_v[...], last_v[...] = g, gc, last
        loss_ref[...] = (0.5 / d_model) * jnp.sum(g_v[LOSS_ROW:LOSS_ROW + 1, :], axis=-1, keepdims=True)
        for j, (row0, own_columns, (w, _, _)) in enumerate(params):
            w_ref, m_ref, v_ref = ins[3 * j:3 * j + 3]
            source = gc_v if own_columns else (last_v if row0 == 0 else g_v)
            grad = source[row0:row0 + w.shape[0], :]
            delta, m_new, v_new = _adamw(w_ref[...], grad, m_ref[...], v_ref[...])
            for ref, val in zip(outs[4 * j:4 * j + 4], (grad, delta, m_new, v_new)):
                ref[...] = val

    full = lambda shape: pl.BlockSpec(shape, lambda i, me_ref: (0,) * len(shape))
    stack_rows = recv_small.shape[1]
    flat = [a for _, _, triple in params for a in triple]
    shapes = [w.shape for _, _, (w, _, _) in params for _ in range(4)]
    out = pl.pallas_call(
        body, name="adam_small",
        grid_spec=pltpu.PrefetchScalarGridSpec(
            num_scalar_prefetch=1, grid=(1,),
            in_specs=[full(recv_small.shape),
                      pl.BlockSpec((N_DEV, stack_rows, cw), lambda i, me_ref: (0, 0, me_ref[0])),
                      full(recv_last.shape)] + [full(a.shape) for a in flat],
            out_specs=[full((1, 1))] + [full(s) for s in shapes],
            scratch_shapes=[pltpu.VMEM((stack_rows, d_model), F32), pltpu.VMEM((stack_rows, cw), F32),
                            pltpu.VMEM(recv_last.shape[1:], F32)]),
        out_shape=[jax.ShapeDtypeStruct((1, 1), F32)] + [jax.ShapeDtypeStruct(s, F32) for s in shapes],
    )(me, recv_small, recv_small, recv_last, *flat)
    return out[0], [tuple(out[1 + 4 * j:5 + 4 * j]) for j in range(n)]


def _tile(n_tok, want):
    return min(want, n_tok)


def kernel(x, norm1_pre_g, w_in, b_in, conv_a_w, conv_a_b, w_a_out, conv_b_w, conv_b_b, ln_b_g, ln_b_b, w_b_out, w_o, norm1_post_g, norm2_pre_g, w_mlp_in, w_mlp_out, norm2_post_g, loss_target, m_norm1_pre_g, m_w_in, m_b_in, m_conv_a_w, m_conv_a_b, m_w_a_out, m_conv_b_w, m_conv_b_b, m_ln_b_g, m_ln_b_b, m_w_b_out, m_w_o, m_norm1_post_g, m_norm2_pre_g, m_w_mlp_in, m_w_mlp_out, m_norm2_post_g, v_norm1_pre_g, v_w_in, v_b_in, v_conv_a_w, v_conv_a_b, v_w_a_out, v_conv_b_w, v_conv_b_b, v_ln_b_g, v_ln_b_b, v_w_b_out, v_w_o, v_norm1_post_g, v_norm2_pre_g, v_w_mlp_in, v_w_mlp_out, v_norm2_post_g):
    n_tok, dm = x.shape[1], x.shape[2]
    rows, off, slab_rows = _layout(dm)
    cw = dm // N_DEV
    xs = x.reshape(n_tok, dm)
    tgt = loss_target.reshape(n_tok, dm)
    row = lambda vec: vec.reshape(1, -1)
    gathered = lambda group: (N_DEV, slab_rows[group], dm)
    scattered = lambda group: jax.ShapeDtypeStruct((len(FLIPS), slab_rows[group], dm), BF16)
    tm, tk = min(dm, 1024), _tile(n_tok, 2048)
    me = (4 * lax.axis_index("x") + 2 * lax.axis_index("y") + lax.axis_index("c")).astype(jnp.int32).reshape(1)

    conv_own = jnp.concatenate([conv_a_w, jnp.zeros((SUBLANE - CONV_A, cw), F32), conv_b_w,
                                jnp.zeros((1, cw), F32)], axis=0)
    own_in = w_in.T.astype(BF16)
    slab_in, conv_all = _all_gather_two_level(
        [own_in, conv_own], [_place_own(own_in, gathered("in"), me, "place_w_in"), None], "gather_w_in")
    conv_full = conv_all.transpose(1, 0, 2).reshape(conv_own.shape[0], dm)
    caw, cbw = conv_full[0:CONV_A], conv_full[SUBLANE:SUBLANE + CONV_B]
    own_abo = jnp.concatenate([w_a_out, w_b_out, w_o], axis=0).astype(BF16)
    own_mlp = jnp.concatenate([w_mlp_in.T, w_mlp_out], axis=0).astype(BF16)
    ag_abo = _exchange_start([_Part(own_abo, False, slab_rows["abo"], 0, 0)],
                             [_place_own(own_abo, gathered("abo"), me, "place_abo")],
                             "gather_abo_start", after=slab_in)
    ag_mlp = _exchange_start([_Part(own_mlp, False, slab_rows["mlp"], 0, 0)],
                             [_place_own(own_mlp, gathered("mlp"), me, "place_mlp")],
                             "gather_mlp_start", after=ag_abo.token)

    proj, p, u, h = _fwd_in(xs, row(norm1_pre_g), row(b_in), slab_in, _tile(n_tok, 512),
                            [ag_abo.token, ag_mlp.token])
    _, (slab_abo,) = _exchange_wait(ag_abo, "gather_abo_wait", after=proj)
    va, vb, ya, yb, qa, sb, merged, mix, x1 = _fwd_mix(
        p, u, proj, xs, caw, row(conv_a_b), cbw, row(conv_b_b), row(ln_b_g), row(ln_b_b), row(norm1_post_g),
        slab_abo, _tile(n_tok, 256))
    _, (slab_mlp,) = _exchange_wait(ag_mlp, "gather_mlp_wait", after=x1)
    f, df1, h2, df2, dmix, dx1, small_mlp = _mlp_fwd_bwd(
        x1, mix, tgt, row(norm1_post_g), row(norm2_pre_g), row(norm2_post_g), slab_mlp, _tile(n_tok, 256))

    rs_mlp = _exchange_start(
        [_Part(_wgrad(df1, h2, "wgrad_mlp_in", tm, tk, BF16), True, rows["w1"], 0, off["w1"]),
         _Part(_wgrad(f, df2, "wgrad_mlp_out", tm, tk, BF16), True, rows["w2"], 0, off["w2"])],
        [scattered("mlp")], "scatter_mlp_start")
    dpa, dya, dyb, dva, dvb, small_mix = _bwd_mix(
        dmix, ya, yb, proj, va, vb, row(ln_b_g), row(ln_b_b), slab_abo, _tile(n_tok, 256), [rs_mlp.token])
    rs_abo = _exchange_start(
        [_Part(_wgrad(qa, dya, "wgrad_a_out", tm, tk, BF16), True, rows["wa"], 0, off["wa"]),
         _Part(_wgrad(sb, dyb, "wgrad_b_out", tm, tk, BF16), True, rows["wb"], 0, off["wb"]),
         _Part(_wgrad(merged, dmix, "wgrad_o", tm, tk, BF16), True, rows["wo"], 0, off["wo"])],
        [scattered("abo")], "scatter_abo_start")
    dproj, small_conv = _bwd_conv(dva, dvb, p, u, proj, dpa, caw, cbw, _tile(n_tok, 256), [rs_abo.token])

    zeros = lambda r: jnp.zeros((r, dm), F32)
    small = jnp.concatenate([
        zeros(1),
        small_mix[2:3],
        small_mix[3:4],
        small_mix[0:2],
        small_mlp[2:3],
        small_mlp[1:2],
        small_mlp[0:1],
        small_mix[4:5], small_conv[3:7], small_mix[5:7],
        small_mlp[3:4],
        small_conv[0:CONV_A], zeros(SUBLANE - CONV_A),
        small_conv[8:8 + CONV_B], zeros(1),
    ], axis=0)

    rs_in = _exchange_start(
        [_Part(_wgrad(dproj, h, "wgrad_in", tm, tk, BF16), True, rows["win"], 0, off["win"]),
         _Part(small, False, small.shape[0], 1, 0)],
        [scattered("in"), _place_own(small, (N_DEV,) + small.shape, me, "place_small")], "scatter_in_start")
    grad_x, small_in = _bwd_in(dproj, xs, dx1, row(norm1_pre_g), slab_in, _tile(n_tok, 256), [rs_in.token])

    tr = min(LANE, rows["wa"])
    (g_w1, g_w2), (recv_mlp,) = _exchange_wait(rs_mlp, "scatter_mlp_wait", after=grad_x)
    (g_wa, g_wb, g_wo), (recv_abo,) = _exchange_wait(rs_abo, "scatter_abo_wait", after=grad_x)
    big = {
        "w_mlp_in": _adam_big(recv_mlp, g_w1, me, off["w1"], rows["w1"], w_mlp_in, m_w_mlp_in, v_w_mlp_in, True,
                              "adam_w_mlp_in", tr),
        "w_mlp_out": _adam_big(recv_mlp, g_w2, me, off["w2"], rows["w2"], w_mlp_out, m_w_mlp_out, v_w_mlp_out,
                               False, "adam_w_mlp_out", tr),
        "w_a_out": _adam_big(recv_abo, g_wa, me, off["wa"], rows["wa"], w_a_out, m_w_a_out, v_w_a_out, False,
                             "adam_w_a_out", tr),
        "w_b_out": _adam_big(recv_abo, g_wb, me, off["wb"], rows["wb"], w_b_out, m_w_b_out, v_w_b_out, False,
                             "adam_w_b_out", tr),
        "w_o": _adam_big(recv_abo, g_wo, me, off["wo"], rows["wo"], w_o, m_w_o, v_w_o, False, "adam_w_o", tr),
    }
    (g_win, _), (recv_in, recv_small) = _exchange_wait(rs_in, "scatter_in_wait", after=big["w_o"][3])
    recv_last, = _all_gather([small_in], "gather_last")
    big["w_in"] = _adam_big(recv_in, g_win, me, off["win"], rows["win"], w_in, m_w_in, v_w_in, True, "adam_w_in", tr)

    small_names = ("norm1_pre_g", "conv_a_b", "conv_b_b", "ln_b_g", "ln_b_b", "norm1_post_g", "norm2_pre_g",
                   "norm2_post_g")
    given = dict(
        norm1_pre_g=(norm1_pre_g, m_norm1_pre_g, v_norm1_pre_g), conv_a_b=(conv_a_b, m_conv_a_b, v_conv_a_b),
        conv_b_b=(conv_b_b, m_conv_b_b, v_conv_b_b), ln_b_g=(ln_b_g, m_ln_b_g, v_ln_b_g),
        ln_b_b=(ln_b_b, m_ln_b_b, v_ln_b_b), norm1_post_g=(norm1_post_g, m_norm1_post_g, v_norm1_post_g),
        norm2_pre_g=(norm2_pre_g, m_norm2_pre_g, v_norm2_pre_g),
        norm2_post_g=(norm2_post_g, m_norm2_post_g, v_norm2_post_g))
    params = [(j, False, tuple(row(a) for a in given[name])) for j, name in enumerate(small_names)]
    params.append((SUBLANE, False, tuple(a.reshape(7, dm) for a in (b_in, m_b_in, v_b_in))))
    params.append((CONV_A_ROW, True, (conv_a_w, m_conv_a_w, v_conv_a_w)))
    params.append((CONV_B_ROW, True, (conv_b_w, m_conv_b_w, v_conv_b_w)))
    loss, small_out = _adam_small(recv_small, recv_last, me, params, dm)
    small_leaves = {name: tuple(a.reshape(dm) for a in small_out[j]) for j, name in enumerate(small_names)}
    small_leaves["b_in"] = tuple(a.reshape(7 * dm) for a in small_out[len(small_names)])
    small_leaves["conv_a_w"] = small_out[len(small_names) + 1]
    small_leaves["conv_b_w"] = small_out[len(small_names) + 2]

    order = ("norm1_pre_g", "w_in", "b_in", "conv_a_w", "conv_a_b", "w_a_out", "conv_b_w", "conv_b_b", "ln_b_g",
             "ln_b_b", "w_b_out", "w_o", "norm1_post_g", "norm2_pre_g", "w_mlp_in", "w_mlp_out", "norm2_post_g")
    leaves = [big[name] if name in big else small_leaves[name] for name in order]
    grads, deltas, new_m, new_v = zip(*leaves)
    return (loss.reshape(()), grad_x.reshape(x.shape), *grads, *deltas, *new_m, *new_v)
```

```python
from typing import NamedTuple

import jax
import jax.numpy as jnp
from jax import lax
from jax.experimental import pallas as pl
from jax.experimental.pallas import tpu as pltpu

F32 = jnp.float32
BF16 = jnp.bfloat16

RMS_EPS = 1e-6
LN_EPS = 1e-5
ADAM_LR = 0.001
ADAM_B1 = 0.9
ADAM_B2 = 0.999
ADAM_EPS = 1e-08
ADAM_WD = 0.01
ADAM_STEP = 10

N_DEV = 8
CONV_A = 3
CONV_B = 31
LANE = 128
SUBLANE = 8
HALO = 16
FWD_CONV_ROWS = 32
BWD_CONV_ROWS = 64
MIB = 1 << 20
FLIPS = ((0, 0, 1), (0, 1, 0), (1, 0, 0), (0, 1, 1), (1, 0, 1), (1, 1, 0), (1, 1, 1))
MESH = pl.DeviceIdType.MESH


def _layout(d_model):
    e = d_model // N_DEV
    rows = {"win": 7 * e, "w1": 4 * e, "w2": 4 * e, "wa": e, "wb": e, "wo": e}
    off = {"win": 0, "w1": 0, "w2": 4 * e, "wa": 0, "wb": e, "wo": 2 * e}
    return rows, off, {"in": 7 * e, "mlp": 8 * e, "abo": 3 * e}


def _after(body, deps):
    def ordered(*refs):
        return body(*refs[len(deps):])
    return ordered


def _params(vmem_mib):
    return pltpu.CompilerParams(dimension_semantics=("arbitrary",), vmem_limit_bytes=vmem_mib * MIB)


def _whole():
    return pl.BlockSpec(memory_space=pltpu.VMEM)


def _hbm():
    return pl.BlockSpec(memory_space=pl.ANY)


def _rows(ts, width, col=0):
    return pl.BlockSpec((ts, width), lambda i: (i, col))


def _halo_specs(ts, width, n_rows):
    per = ts // HALO
    last = n_rows // HALO - 1
    return [
        pl.BlockSpec((ts, width), lambda i: (i, 0)),
        pl.BlockSpec((HALO, width), lambda i: (jnp.maximum(i * per - 1, 0), 0)),
        pl.BlockSpec((HALO, width), lambda i: (jnp.minimum((i + 1) * per, last), 0)),
    ]


def _dot(a, b):
    return jnp.dot(a, b, preferred_element_type=F32)


def _dot_nt(a, b):
    return lax.dot_general(a, b, (((1,), (1,)), ((), ())), preferred_element_type=F32)


def _dot_tn(a, b):
    return lax.dot_general(a, b, (((0,), (0,)), ((), ())), preferred_element_type=F32)


def _rms(u):
    return lax.rsqrt(jnp.mean(u * u, axis=-1, keepdims=True) + RMS_EPS)


def _rms_bwd(dz, u, r, g):
    dzg = dz * g
    return r * dzg - u * (r * r * r) * jnp.mean(dzg * u, axis=-1, keepdims=True)


def _colsum(v):
    return jnp.sum(v, axis=0, keepdims=True)


def _sigmoid(v):
    return jax.nn.sigmoid(v)


def _weight_copies(slab_ref, off, rows, dst_ref, sems, first_sem):
    return [pltpu.make_async_copy(slab_ref.at[d, pl.ds(off, rows), :], dst_ref.at[pl.ds(d * rows, rows), :],
                                  sems.at[first_sem + d]) for d in range(N_DEV)]


def _on_first_step(copies, method):
    @pl.when(pl.program_id(0) == 0)
    def _():
        for cp in copies:
            getattr(cp, method)()


def _with_halos(main_ref, prev_ref, next_ref, i, n_steps):
    return (main_ref, jnp.where(i > 0, prev_ref[...], 0.0), jnp.where(i < n_steps - 1, next_ref[...], 0.0))


def _broadcast_taps(w_ref, wb_ref, n_taps):
    for k in range(n_taps):
        wb_ref[k] = jnp.broadcast_to(w_ref[k:k + 1, :], wb_ref.shape[1:])


def _conv_tile(tile, wb_ref, starts, ts, width, emit, rolled_ref, rows=FWD_CONV_ROWS):
    main_ref, prev, nxt = tile
    span = ts + 2 * HALO
    nv = rows // SUBLANE
    for cb in range(width // LANE):
        lanes = slice(cb * LANE, (cb + 1) * LANE)
        slot = cb % 2
        window = jnp.concatenate([prev[:, lanes], main_ref[:, lanes], nxt[:, lanes]], axis=0)
        for b in sorted({st % SUBLANE for st in starts}):
            rolled_ref[slot, b] = window if b == 0 else pltpu.roll(window, span - b, axis=0)
        for r0 in range(0, ts, rows):
            acc = jnp.zeros((nv, SUBLANE, LANE), F32)
            for k, st in enumerate(starts):
                shifted = rolled_ref[slot, st % SUBLANE, pl.ds(r0 + st - st % SUBLANE, rows), :]
                acc = acc + shifted.reshape(nv, SUBLANE, LANE) * wb_ref[k, :, lanes][None]
            emit(r0, pl.ds(cb * LANE, LANE), acc.reshape(rows, LANE))


def _window(tile, r0, cb, ts, rows):
    main_ref, prev, nxt = tile
    lanes = slice(cb * LANE, (cb + 1) * LANE)
    lo, hi = max(r0 - HALO, 0), min(r0 + rows + HALO, ts)
    pieces = [prev[:, lanes]] if r0 - HALO < 0 else []
    pieces.append(main_ref[lo:hi, lanes])
    if r0 + rows + HALO > ts:
        pieces.append(nxt[:, lanes])
    return pieces[0] if len(pieces) == 1 else jnp.concatenate(pieces, axis=0)


def _phases(starts):
    groups = {}
    for k, st in enumerate(starts):
        groups.setdefault(st % SUBLANE, []).append((k, st // SUBLANE))
    return sorted(groups.items())


def _shifted(blk, b):
    n = blk.shape[0]
    rolled = blk if b == 0 else pltpu.roll(blk, n - b, axis=0)
    return rolled.reshape(n // SUBLANE, SUBLANE, blk.shape[1])


def _conv_bwd_tile(dv_tile, u_ref, wb_ref, acc_ref, n_taps, ts, width, emit, rows=BWD_CONV_ROWS):
    groups = _phases(_bwd_starts(n_taps))
    nv = rows // SUBLANE
    for r0 in range(0, ts, rows):
        for cb in range(width // LANE):
            lanes = pl.ds(cb * LANE, LANE)
            blk = _window(dv_tile, r0, cb, ts, rows)
            u = u_ref[pl.ds(r0, rows), lanes].reshape(nv, SUBLANE, LANE)
            du = jnp.zeros((nv, SUBLANE, LANE), F32)
            for b, taps in groups:
                sh = _shifted(blk, b)
                for k, m in taps:
                    du = du + sh[m:m + nv] * wb_ref[k, :, lanes][None]
                    acc_ref[k, :, lanes] += jnp.sum(sh[m:m + nv] * u, axis=0)
            emit(r0, lanes, du.reshape(rows, LANE))


def _fwd_starts(n_taps):
    pad = (n_taps - 1) // 2
    return [HALO - pad + k for k in range(n_taps)]


def _bwd_starts(n_taps):
    pad = (n_taps - 1) // 2
    return [HALO + pad - k for k in range(n_taps)]


def _peer(x, y, c, flip):
    fx, fy, fc = flip
    return (1 - x if fx else x, 1 - y if fy else y, 1 - c if fc else c)


def _all_gather(shards, name):
    n = len(shards)

    def body(*refs):
        ins, outs = refs[:n], refs[n:2 * n]
        send_sems, recv_sems, local_sems = refs[2 * n:]
        x, y, c = lax.axis_index("x"), lax.axis_index("y"), lax.axis_index("c")
        me = 4 * x + 2 * y + c
        local = [pltpu.make_async_copy(ins[j], outs[j].at[me], local_sems.at[j]) for j in range(n)]
        for cp in local:
            cp.start()
        sends, recvs = [], []
        for k, flip in enumerate(FLIPS):
            px, py, pc = _peer(x, y, c, flip)
            peer = 4 * px + 2 * py + pc
            for j in range(n):
                sem = k * n + j
                sends.append(pltpu.make_async_remote_copy(
                    src_ref=ins[j], dst_ref=outs[j].at[me], send_sem=send_sems.at[sem], recv_sem=recv_sems.at[sem],
                    device_id=(px, py, pc), device_id_type=MESH))
                recvs.append(pltpu.make_async_remote_copy(
                    src_ref=ins[j], dst_ref=outs[j].at[peer], send_sem=send_sems.at[sem], recv_sem=recv_sems.at[sem],
                    device_id=(px, py, pc), device_id_type=MESH))
        for cp in sends:
            cp.start()
        for cp in recvs:
            cp.wait_recv()
        for cp in sends:
            cp.wait_send()
        for cp in local:
            cp.wait()

    return pl.pallas_call(
        body, name=name,
        out_shape=[jax.ShapeDtypeStruct((N_DEV,) + s.shape, s.dtype) for s in shards],
        in_specs=[_hbm()] * n, out_specs=[_hbm()] * n,
        scratch_shapes=[pltpu.SemaphoreType.DMA((7 * n,)), pltpu.SemaphoreType.DMA((7 * n,)),
                        pltpu.SemaphoreType.DMA((n,))],
    )(*shards)


def _place_own(src, n_slots_shape, me, name):
    rows, width = src.shape
    tr = next(t for t in (256, 128, 64, 32, 16, SUBLANE) if rows % t == 0)

    def body(me_ref, src_ref, out_ref):
        out_ref[...] = src_ref[...]

    return pl.pallas_call(
        body, name=name,
        grid_spec=pltpu.PrefetchScalarGridSpec(
            num_scalar_prefetch=1, grid=(rows // tr,),
            in_specs=[pl.BlockSpec((tr, width), lambda i, me_ref: (i, 0))],
            out_specs=pl.BlockSpec((None, tr, width), lambda i, me_ref: (me_ref[0], i, 0))),
        out_shape=jax.ShapeDtypeStruct(n_slots_shape, src.dtype),
    )(me, src)


def _all_gather_two_level(shards, placed, name):
    n = len(shards)
    given = [j for j in range(n) if placed[j] is not None]

    def body(*refs):
        ins, outs = refs[:n], refs[n + len(given):2 * n + len(given)]
        send_sems, recv_sems, local_sems = refs[2 * n + len(given):]
        x, y, c = lax.axis_index("x"), lax.axis_index("y"), lax.axis_index("c")
        me, sibling = (x, y, c), (x, y, 1 - c)
        chips = [(1 - x, y), (x, 1 - y), (1 - x, 1 - y)]

        def slot(j, dev):
            return outs[j].at[4 * dev[0] + 2 * dev[1] + dev[2]]

        def copy(k, j, block, to, src=None):
            return pltpu.make_async_remote_copy(
                src_ref=slot(j, block) if src is None else src, dst_ref=slot(j, block),
                send_sem=send_sems.at[k * n + j], recv_sem=recv_sems.at[k * n + j], device_id=to, device_id_type=MESH)

        local = [pltpu.make_async_copy(ins[j], slot(j, me), local_sems.at[j]) for j in range(n) if j not in given]
        for cp in local:
            cp.start()
        first = [copy(0, j, me, sibling, src=ins[j]) for j in range(n)]
        first += [copy(1 + t, j, me, (*chip, c), src=ins[j]) for t, chip in enumerate(chips) for j in range(n)]
        for cp in first:
            cp.start()
        passed = []
        for t, chip in enumerate(chips):
            for j in range(n):
                copy(1 + t, j, (*chip, c), me).wait_recv()
                passed.append(copy(4 + t, j, (*chip, c), sibling))
                passed[-1].start()
        for j in range(n):
            copy(0, j, sibling, me).wait_recv()
        for t, chip in enumerate(chips):
            for j in range(n):
                copy(4 + t, j, (*chip, 1 - c), me).wait_recv()
        for cp in first + passed:
            cp.wait_send()
        for cp in local:
            cp.wait()

    return pl.pallas_call(
        body, name=name,
        out_shape=[jax.ShapeDtypeStruct((N_DEV,) + s.shape, s.dtype) for s in shards],
        in_specs=[_hbm()] * (n + len(given)), out_specs=[_hbm()] * n,
        input_output_aliases={n + i: j for i, j in enumerate(given)},
        scratch_shapes=[pltpu.SemaphoreType.DMA((7 * n,)), pltpu.SemaphoreType.DMA((7 * n,)),
                        pltpu.SemaphoreType.DMA((n,))],
    )(*shards, *[placed[j] for j in given])


class _Part(NamedTuple):
    src: jax.Array
    scatter: bool
    rows: int
    land: int
    off: int


class _Started(NamedTuple):
    send_sems: jax.Array
    recv_sems: jax.Array
    thru: tuple
    token: jax.Array
    parts: tuple


def _exchange_copies(srcs, lands, send_sems, recv_sems, parts):
    n = len(parts)
    x, y, c = lax.axis_index("x"), lax.axis_index("y"), lax.axis_index("c")
    me = 4 * x + 2 * y + c

    def block(j, dev):
        p = parts[j]
        return srcs[j].at[pl.ds(pl.multiple_of(dev * p.rows, SUBLANE), p.rows), :] if p.scatter else srcs[j]

    def slot(j, index):
        p = parts[j]
        return lands[p.land].at[index, pl.ds(p.off, p.rows), :]

    sends, recvs = [], []
    for k, flip in enumerate(FLIPS):
        px, py, pc = _peer(x, y, c, flip)
        peer = 4 * px + 2 * py + pc
        for j in range(n):
            sems = dict(send_sem=send_sems.at[k * n + j], recv_sem=recv_sems.at[k * n + j],
                        device_id=(px, py, pc), device_id_type=MESH)
            to, got = (k, k) if parts[j].scatter else (me, peer)
            sends.append(pltpu.make_async_remote_copy(src_ref=block(j, peer), dst_ref=slot(j, to), **sems))
            recvs.append(pltpu.make_async_remote_copy(src_ref=block(j, peer), dst_ref=slot(j, got), **sems))
    return sends, recvs


def _exchange_start(parts, lands, name, after=None):
    n, nl = len(parts), len(lands)
    n_in = n + nl + (after is not None)

    def body(*refs):
        srcs, land_refs = refs[:n], refs[n:n + nl]
        send_sems, recv_sems = refs[n_in], refs[n_in + 1]
        token = refs[n_in + 2 + n + nl]
        sends, _ = _exchange_copies(srcs, land_refs, send_sems, recv_sems, parts)
        for cp in sends:
            cp.start()
        token[...] = jnp.zeros_like(token)

    hbm = pl.BlockSpec(memory_space=pltpu.HBM)
    sem = pl.BlockSpec(memory_space=pltpu.SEMAPHORE)
    fresh = lambda s: lax.empty(s.shape, s.dtype) if isinstance(s, jax.ShapeDtypeStruct) else s
    args = [pltpu.with_memory_space_constraint(p.src, pltpu.HBM) for p in parts]
    args += [pltpu.with_memory_space_constraint(fresh(s), pltpu.HBM) for s in lands]
    args += [] if after is None else [after]
    out = pl.pallas_call(
        body, name=name,
        out_shape=(pltpu.SemaphoreType.DMA((7 * n,)), pltpu.SemaphoreType.DMA((7 * n,)),
                   *[pltpu.HBM(a.shape, a.dtype) for a in args[:n + nl]], jax.ShapeDtypeStruct((SUBLANE, LANE), F32)),
        in_specs=[hbm] * (n + nl) + [_hbm()] * (after is not None),
        out_specs=(sem, sem, *[hbm] * (n + nl), _whole()),
        input_output_aliases={j: 2 + j for j in range(n + nl)},
        compiler_params=pltpu.CompilerParams(has_side_effects=pltpu.SideEffectType.DATAFLOW_SIDE_EFFECTING),
    )(*args)
    return _Started(out[0], out[1], tuple(out[2:2 + n + nl]), out[2 + n + nl], tuple(parts))


def _exchange_wait(started, name, after):
    parts = started.parts
    n, nl = len(parts), len(started.thru) - len(parts)

    def body(*refs):
        srcs, land_refs = refs[:n], refs[n:n + nl]
        send_sems, recv_sems = refs[n + nl], refs[n + nl + 1]
        sends, recvs = _exchange_copies(srcs, land_refs, send_sems, recv_sems, parts)
        for cp in sends:
            cp.wait_send()
        for cp in recvs:
            cp.wait_recv()

    hbm = pl.BlockSpec(memory_space=pltpu.HBM)
    sem = pl.BlockSpec(memory_space=pltpu.SEMAPHORE)
    out = pl.pallas_call(
        body, name=name,
        out_shape=tuple(pltpu.HBM(a.shape, a.dtype) for a in started.thru),
        in_specs=[hbm] * (n + nl) + [sem, sem, _hbm()], out_specs=tuple([hbm] * (n + nl)),
        input_output_aliases={j: j for j in range(n + nl)},
        compiler_params=pltpu.CompilerParams(has_side_effects=pltpu.SideEffectType.DATAFLOW_SIDE_EFFECTING),
    )(*started.thru, started.send_sems, started.recv_sems, after)
    return list(out[:n]), list(out[n:])


def _fwd_in(x, g1, b_in, slab, ts, deps):
    n_tok, dm = x.shape
    rows, off, _ = _layout(dm)
    width = 7 * dm

    def body(x_ref, g1_ref, b_ref, slab_ref, proj_ref, p_ref, u_ref, h_ref, w_v, sems):
        copies = _weight_copies(slab_ref, off["win"], rows["win"], w_v, sems, 0)
        _on_first_step(copies, "start")
        _on_first_step(copies, "wait")
        xv = x_ref[...]
        h = (xv * _rms(xv) * g1_ref[...]).astype(BF16)
        h_ref[...] = h
        cols = []
        for j in range(7):
            pj = _dot_nt(h, w_v[pl.ds(j * dm, dm), :]) + b_ref[:, j * dm:(j + 1) * dm]
            proj_ref[:, j * dm:(j + 1) * dm] = pj.astype(proj_ref.dtype)
            if 1 <= j <= 4:
                cols.append(pj)
            if j == 2:
                p_ref[...] = cols[0] * cols[1]
            if j == 4:
                u_ref[...] = cols[2] * _sigmoid(cols[3])

    return pl.pallas_call(
        _after(body, deps), name="fwd_in", grid=(n_tok // ts,),
        in_specs=[_whole()] * len(deps) + [_rows(ts, dm), _whole(), _whole(), _hbm()],
        out_specs=[_rows(ts, width), _rows(ts, dm), _rows(ts, dm), _rows(ts, dm)],
        out_shape=[jax.ShapeDtypeStruct((n_tok, width), BF16), jax.ShapeDtypeStruct((n_tok, dm), F32),
                   jax.ShapeDtypeStruct((n_tok, dm), F32), jax.ShapeDtypeStruct((n_tok, dm), BF16)],
        scratch_shapes=[pltpu.VMEM((width, dm), BF16), pltpu.SemaphoreType.DMA((N_DEV,))],
        compiler_params=_params(56),
    )(*deps, x, g1, b_in, slab)


def _fwd_mix(p, u, proj, x, caw, cab, cbw, cbb, lng, lnb, g1post, slab, ts):
    n_tok, dm = x.shape
    rows, off, _ = _layout(dm)
    n_steps = n_tok // ts

    def body(p_ref, p_prev, p_next, u_ref, u_prev, u_next, bg_ref, za_ref, zb_ref, x_ref,
             caw_ref, cab_ref, cbw_ref, cbb_ref, lng_ref, lnb_ref, g1p_ref, slab_ref,
             va_ref, vb_ref, ya_ref, yb_ref, qa_ref, sb_ref, mg_ref, mix_ref, x1_ref,
             wa_v, wb_v, wo_v, tap_a, tap_b, sems, rolled_a, rolled_b):
        i = pl.program_id(0)
        copies = (_weight_copies(slab_ref, off["wa"], rows["wa"], wa_v, sems, 0)
                  + _weight_copies(slab_ref, off["wb"], rows["wb"], wb_v, sems, N_DEV)
                  + _weight_copies(slab_ref, off["wo"], rows["wo"], wo_v, sems, 2 * N_DEV))
        _on_first_step(copies, "start")

        @pl.when(i == 0)
        def _():
            _broadcast_taps(caw_ref, tap_a, CONV_A)
            _broadcast_taps(cbw_ref, tap_b, CONV_B)

        def emit_a(r0, lanes, acc):
            va_ref[pl.ds(r0, acc.shape[0]), lanes] = acc + cab_ref[:, lanes]

        def emit_b(r0, lanes, acc):
            vb_ref[pl.ds(r0, acc.shape[0]), lanes] = acc + cbb_ref[:, lanes]

        _conv_tile(_with_halos(p_ref, p_prev, p_next, i, n_steps), tap_a, _fwd_starts(CONV_A), ts, dm, emit_a,
                   rolled_a)
        _conv_tile(_with_halos(u_ref, u_prev, u_next, i, n_steps), tap_b, _fwd_starts(CONV_B), ts, dm, emit_b,
                   rolled_b)
        _on_first_step(copies, "wait")

        qa = (bg_ref[...].astype(F32) * va_ref[...]).astype(BF16)
        qa_ref[...] = qa
        ya = _dot(qa, wa_v[...])
        vb = vb_ref[...]
        xc = vb - jnp.mean(vb, axis=-1, keepdims=True)
        rstd = lax.rsqrt(jnp.mean(xc * xc, axis=-1, keepdims=True) + LN_EPS)
        ln = xc * rstd * lng_ref[...] + lnb_ref[...]
        sb = (ln * _sigmoid(ln)).astype(BF16)
        sb_ref[...] = sb
        yb = _dot(sb, wb_v[...])
        ya_ref[...] = ya.astype(BF16)
        yb_ref[...] = yb.astype(BF16)
        merged = (_sigmoid(za_ref[...].astype(F32)) * ya + _sigmoid(zb_ref[...].astype(F32)) * yb).astype(BF16)
        mg_ref[...] = merged
        mix = _dot(merged, wo_v[...])
        mix_ref[...] = mix
        x1_ref[...] = x_ref[...] + mix * _rms(mix) * g1p_ref[...]

    tok = lambda dt: jax.ShapeDtypeStruct((n_tok, dm), dt)
    return pl.pallas_call(
        body, name="fwd_mix", grid=(n_steps,),
        in_specs=(_halo_specs(ts, dm, n_tok) + _halo_specs(ts, dm, n_tok)
                  + [_rows(ts, dm, 0), _rows(ts, dm, 5), _rows(ts, dm, 6), _rows(ts, dm)]
                  + [_whole()] * 7 + [_hbm()]),
        out_specs=[_rows(ts, dm)] * 9,
        out_shape=[tok(F32), tok(F32), tok(BF16), tok(BF16), tok(BF16), tok(BF16), tok(BF16), tok(F32), tok(F32)],
        scratch_shapes=[pltpu.VMEM((dm, dm), BF16), pltpu.VMEM((dm, dm), BF16), pltpu.VMEM((dm, dm), BF16),
                        pltpu.VMEM((CONV_A, SUBLANE, dm), F32), pltpu.VMEM((CONV_B, SUBLANE, dm), F32),
                        pltpu.SemaphoreType.DMA((3 * N_DEV,)),
                        pltpu.VMEM((2, SUBLANE, ts + 2 * HALO, LANE), F32),
                        pltpu.VMEM((2, SUBLANE, ts + 2 * HALO, LANE), F32)],
        compiler_params=_params(48),
    )(p, p, p, u, u, u, proj, proj, proj, x, caw, cab, cbw, cbb, lng, lnb, g1post, slab)


def _mlp_fwd_bwd(x1, mix, tgt, g1post, g2pre, g2post, slab, ts):
    n_tok, dm = x1.shape
    rows, off, _ = _layout(dm)
    ff = 4 * dm

    def body(x1_ref, mix_ref, t_ref, g1p_ref, g2pre_ref, g2post_ref, slab_ref,
             f_ref, df1_ref, h2_ref, df2_ref, dmix_ref, dx1_ref, small_ref, w1_v, w2_v, relu_v, sems):
        w1_copies = _weight_copies(slab_ref, off["w1"], rows["w1"], w1_v, sems, 0)
        w2_copies = _weight_copies(slab_ref, off["w2"], rows["w2"], w2_v, sems, N_DEV)
        _on_first_step(w1_copies + w2_copies, "start")

        @pl.when(pl.program_id(0) == 0)
        def _():
            small_ref[...] = jnp.zeros_like(small_ref)

        _on_first_step(w1_copies + w2_copies, "wait")
        x1v = x1_ref[...]
        r3 = _rms(x1v)
        g2pre = g2pre_ref[...]
        h2 = (x1v * r3 * g2pre).astype(BF16)
        h2_ref[...] = h2
        f2 = jnp.zeros((ts, dm), F32)
        for c in range(4):
            blk = pl.ds(c * dm, dm)
            relu = jnp.maximum(_dot_nt(h2, w1_v[blk, :]), 0.0)
            relu_v[:, c * dm:(c + 1) * dm] = relu
            fc = (relu * relu).astype(BF16)
            f_ref[:, c * dm:(c + 1) * dm] = fc
            f2 = f2 + _dot(fc, w2_v[blk, :])
        r4 = _rms(f2)
        g2post = g2post_ref[...]
        err = x1v + f2 * r4 * g2post - t_ref[...]
        dy = err * (1.0 / dm)
        small_ref[3:4, :] += _colsum(err * err)
        small_ref[0:1, :] += _colsum(dy * f2 * r4)
        df2 = _rms_bwd(dy, f2, r4, g2post).astype(BF16)
        df2_ref[...] = df2
        dh2 = jnp.zeros((ts, dm), F32)
        for c in range(4):
            blk = pl.ds(c * dm, dm)
            df1 = (_dot_nt(df2, w2_v[blk, :]) * (2.0 * relu_v[:, c * dm:(c + 1) * dm])).astype(BF16)
            df1_ref[:, c * dm:(c + 1) * dm] = df1
            dh2 = dh2 + _dot(df1, w1_v[blk, :])
        small_ref[1:2, :] += _colsum(dh2 * x1v * r3)
        dx1 = dy + _rms_bwd(dh2, x1v, r3, g2pre)
        dx1_ref[...] = dx1
        mixv = mix_ref[...]
        r2 = _rms(mixv)
        small_ref[2:3, :] += _colsum(dx1 * mixv * r2)
        dmix_ref[...] = _rms_bwd(dx1, mixv, r2, g1p_ref[...]).astype(BF16)

    tok = lambda w, dt: jax.ShapeDtypeStruct((n_tok, w), dt)
    return pl.pallas_call(
        body, name="mlp_fwd_bwd", grid=(n_tok // ts,),
        in_specs=[_rows(ts, dm)] * 3 + [_whole()] * 3 + [_hbm()],
        out_specs=[_rows(ts, ff), _rows(ts, ff), _rows(ts, dm), _rows(ts, dm), _rows(ts, dm), _rows(ts, dm),
                   pl.BlockSpec((SUBLANE, dm), lambda i: (0, 0))],
        out_shape=[tok(ff, BF16), tok(ff, BF16), tok(dm, BF16), tok(dm, BF16), tok(dm, BF16), tok(dm, F32),
                   jax.ShapeDtypeStruct((SUBLANE, dm), F32)],
        scratch_shapes=[pltpu.VMEM((ff, dm), BF16), pltpu.VMEM((ff, dm), BF16), pltpu.VMEM((ts, ff), F32),
                        pltpu.SemaphoreType.DMA((2 * N_DEV,))],
        compiler_params=_params(56),
    )(x1, mix, tgt, g1post, g2pre, g2post, slab)


def _bwd_mix(dmix, ya, yb, proj, va, vb, lng, lnb, slab, ts, deps):
    n_tok, dm = dmix.shape
    rows, off, _ = _layout(dm)
    n_steps = n_tok // ts

    def body(dmix_ref, ya_ref, yb_ref, bg_ref, za_ref, zb_ref, va_ref, vb_ref, lng_ref, lnb_ref, slab_ref,
             dpa_ref, dya_ref, dyb_ref, dva_ref, dvb_ref, small_ref, wa_v, wb_v, wo_v, sems):
        wo_copies = _weight_copies(slab_ref, off["wo"], rows["wo"], wo_v, sems, 2 * N_DEV)
        ab_copies = (_weight_copies(slab_ref, off["wa"], rows["wa"], wa_v, sems, 0)
                     + _weight_copies(slab_ref, off["wb"], rows["wb"], wb_v, sems, N_DEV))
        _on_first_step(wo_copies + ab_copies, "start")

        @pl.when(pl.program_id(0) == 0)
        def _():
            small_ref[...] = jnp.zeros_like(small_ref)

        _on_first_step(wo_copies, "wait")
        dmerged = _dot_nt(dmix_ref[...], wo_v[...])
        _on_first_step(ab_copies, "wait")
        sa = _sigmoid(za_ref[...].astype(F32))
        sg = _sigmoid(zb_ref[...].astype(F32))
        dza = dmerged * ya_ref[...].astype(F32) * sa * (1.0 - sa)
        dzb = dmerged * yb_ref[...].astype(F32) * sg * (1.0 - sg)
        dpa_ref[:, dm:2 * dm] = dza.astype(BF16)
        dpa_ref[:, 2 * dm:3 * dm] = dzb.astype(BF16)
        small_ref[5:6, :] += _colsum(dza)
        small_ref[6:7, :] += _colsum(dzb)

        dya = (dmerged * sa).astype(BF16)
        dya_ref[...] = dya
        dqa = _dot_nt(dya, wa_v[...])
        dbg = dqa * va_ref[...]
        dpa_ref[:, 0:dm] = dbg.astype(BF16)
        small_ref[4:5, :] += _colsum(dbg)
        dva = dqa * bg_ref[...].astype(F32)
        dva_ref[...] = dva
        small_ref[2:3, :] += _colsum(dva)

        dyb = (dmerged * sg).astype(BF16)
        dyb_ref[...] = dyb
        dsb = _dot_nt(dyb, wb_v[...])
        vb = vb_ref[...]
        xc = vb - jnp.mean(vb, axis=-1, keepdims=True)
        rstd = lax.rsqrt(jnp.mean(xc * xc, axis=-1, keepdims=True) + LN_EPS)
        nrm = xc * rstd
        lng_v = lng_ref[...]
        ln = nrm * lng_v + lnb_ref[...]
        sl = _sigmoid(ln)
        dln = dsb * (sl * (1.0 + ln * (1.0 - sl)))
        small_ref[0:1, :] += _colsum(dln * nrm)
        small_ref[1:2, :] += _colsum(dln)
        dn = dln * lng_v
        dvb = rstd * (dn - jnp.mean(dn, axis=-1, keepdims=True)
                      - nrm * jnp.mean(dn * nrm, axis=-1, keepdims=True))
        dvb_ref[...] = dvb
        small_ref[3:4, :] += _colsum(dvb)

    tok = lambda w, dt: jax.ShapeDtypeStruct((n_tok, w), dt)
    return pl.pallas_call(
        _after(body, deps), name="bwd_mix", grid=(n_steps,),
        in_specs=([_whole()] * len(deps) + [_rows(ts, dm)] * 3
                  + [_rows(ts, dm, 0), _rows(ts, dm, 5), _rows(ts, dm, 6)]
                  + [_rows(ts, dm)] * 2 + [_whole()] * 2 + [_hbm()]),
        out_specs=[_rows(ts, 3 * dm), _rows(ts, dm), _rows(ts, dm), _rows(ts, dm), _rows(ts, dm),
                   pl.BlockSpec((SUBLANE, dm), lambda i: (0, 0))],
        out_shape=[tok(3 * dm, BF16), tok(dm, BF16), tok(dm, BF16), tok(dm, F32), tok(dm, F32),
                   jax.ShapeDtypeStruct((SUBLANE, dm), F32)],
        scratch_shapes=[pltpu.VMEM((dm, dm), BF16), pltpu.VMEM((dm, dm), BF16), pltpu.VMEM((dm, dm), BF16),
                        pltpu.SemaphoreType.DMA((3 * N_DEV,))],
        compiler_params=_params(48),
    )(*deps, dmix, ya, yb, proj, proj, proj, va, vb, lng, lnb, slab)


def _bwd_conv(dva, dvb, p, u, proj, dpa, caw, cbw, ts, deps):
    n_tok, dm = dva.shape
    n_steps = n_tok // ts
    small_rows = 40

    def body(dva_ref, dva_prev, dva_next, dvb_ref, dvb_prev, dvb_next, p_ref, u_ref,
             cg_ref, ha_ref, a_ref, g_ref, dpa_ref, caw_ref, cbw_ref,
             dproj_ref, small_ref,
             dp_v, du_v, tap_a, tap_b, gwa_v, gwb_v):
        i = pl.program_id(0)

        @pl.when(i == 0)
        def _():
            _broadcast_taps(caw_ref, tap_a, CONV_A)
            _broadcast_taps(cbw_ref, tap_b, CONV_B)
            small_ref[...] = jnp.zeros_like(small_ref)
            gwa_v[...] = jnp.zeros_like(gwa_v)
            gwb_v[...] = jnp.zeros_like(gwb_v)

        def emit_dp(r0, lanes, acc):
            dp_v[pl.ds(r0, acc.shape[0]), lanes] = acc

        def emit_du(r0, lanes, acc):
            du_v[pl.ds(r0, acc.shape[0]), lanes] = acc

        _conv_bwd_tile(_with_halos(dva_ref, dva_prev, dva_next, i, n_steps), p_ref, tap_a, gwa_v, CONV_A, ts, dm,
                       emit_dp)
        _conv_bwd_tile(_with_halos(dvb_ref, dvb_prev, dvb_next, i, n_steps), u_ref, tap_b, gwb_v, CONV_B, ts, dm,
                       emit_du)

        dp = dp_v[...]
        dcg = dp * ha_ref[...].astype(F32)
        dha = dp * cg_ref[...].astype(F32)
        du = du_v[...]
        sg = _sigmoid(g_ref[...].astype(F32))
        da = du * sg
        dg = du * a_ref[...].astype(F32) * sg * (1.0 - sg)
        dproj_ref[:, 0:dm] = dpa_ref[:, 0:dm]
        dproj_ref[:, dm:2 * dm] = dcg.astype(BF16)
        dproj_ref[:, 2 * dm:3 * dm] = dha.astype(BF16)
        dproj_ref[:, 3 * dm:4 * dm] = da.astype(BF16)
        dproj_ref[:, 4 * dm:5 * dm] = dg.astype(BF16)
        dproj_ref[:, 5 * dm:7 * dm] = dpa_ref[:, dm:3 * dm]
        small_ref[3:4, :] += _colsum(dcg)
        small_ref[4:5, :] += _colsum(dha)
        small_ref[5:6, :] += _colsum(da)
        small_ref[6:7, :] += _colsum(dg)

        @pl.when(i == n_steps - 1)
        def _():
            for k in range(CONV_A):
                small_ref[k:k + 1, :] = _colsum(gwa_v[k])
            for k in range(CONV_B):
                small_ref[SUBLANE + k:SUBLANE + k + 1, :] = _colsum(gwb_v[k])

    return pl.pallas_call(
        _after(body, deps), name="bwd_conv", grid=(n_steps,),
        in_specs=([_whole()] * len(deps) + _halo_specs(ts, dm, n_tok) * 2 + [_rows(ts, dm)] * 2
                  + [_rows(ts, dm, 1), _rows(ts, dm, 2), _rows(ts, dm, 3), _rows(ts, dm, 4), _rows(ts, 3 * dm)]
                  + [_whole()] * 2),
        out_specs=[_rows(ts, 7 * dm), pl.BlockSpec((small_rows, dm), lambda i: (0, 0))],
        out_shape=[jax.ShapeDtypeStruct((n_tok, 7 * dm), BF16), jax.ShapeDtypeStruct((small_rows, dm), F32)],
        scratch_shapes=[pltpu.VMEM((ts, dm), F32), pltpu.VMEM((ts, dm), F32),
                        pltpu.VMEM((CONV_A, SUBLANE, dm), F32), pltpu.VMEM((CONV_B, SUBLANE, dm), F32),
                        pltpu.VMEM((CONV_A, SUBLANE, dm), F32), pltpu.VMEM((CONV_B, SUBLANE, dm), F32)],
        compiler_params=_params(48),
    )(*deps, dva, dva, dva, dvb, dvb, dvb, p, u, proj, proj, proj, proj, dpa, caw, cbw)


def _bwd_in(dproj, x, dx1, g1, slab, ts, deps):
    n_tok, dm = x.shape
    rows, off, _ = _layout(dm)
    width = 7 * dm

    def body(dproj_ref, x_ref, dx1_ref, g1_ref, slab_ref, gx_ref, small_ref, w_v, sems):
        copies = _weight_copies(slab_ref, off["win"], rows["win"], w_v, sems, 0)
        _on_first_step(copies, "start")

        @pl.when(pl.program_id(0) == 0)
        def _():
            small_ref[...] = jnp.zeros_like(small_ref)

        _on_first_step(copies, "wait")
        dh = _dot(dproj_ref[...], w_v[...])
        xv = x_ref[...]
        r1 = _rms(xv)
        small_ref[0:1, :] += _colsum(dh * xv * r1)
        gx_ref[...] = dx1_ref[...] + _rms_bwd(dh, xv, r1, g1_ref[...])

    return pl.pallas_call(
        _after(body, deps), name="bwd_in", grid=(n_tok // ts,),
        in_specs=[_whole()] * len(deps) + [_rows(ts, width), _rows(ts, dm), _rows(ts, dm), _whole(), _hbm()],
        out_specs=[_rows(ts, dm), pl.BlockSpec((SUBLANE, dm), lambda i: (0, 0))],
        out_shape=[jax.ShapeDtypeStruct((n_tok, dm), F32), jax.ShapeDtypeStruct((SUBLANE, dm), F32)],
        scratch_shapes=[pltpu.VMEM((width, dm), BF16), pltpu.SemaphoreType.DMA((N_DEV,))],
        compiler_params=_params(56),
    )(*deps, dproj, x, dx1, g1, slab)


def _wgrad(a, b, name, tm, tk, out_dtype):
    n_tok, m = a.shape
    n = b.shape[1]
    k_steps = n_tok // tk

    def body(a_ref, b_ref, o_ref, acc_v):
        k = pl.program_id(1)

        @pl.when(k == 0)
        def _():
            acc_v[...] = jnp.zeros_like(acc_v)

        acc_v[...] += _dot_tn(a_ref[...], b_ref[...])

        @pl.when(k == k_steps - 1)
        def _():
            o_ref[...] = acc_v[...].astype(o_ref.dtype)

    return pl.pallas_call(
        body, name=name, grid=(m // tm, k_steps),
        in_specs=[pl.BlockSpec((tk, tm), lambda i, k: (k, i)), pl.BlockSpec((tk, n), lambda i, k: (k, 0))],
        out_specs=pl.BlockSpec((tm, n), lambda i, k: (i, 0)),
        out_shape=jax.ShapeDtypeStruct((m, n), out_dtype),
        scratch_shapes=[pltpu.VMEM((tm, n), F32)],
        compiler_params=pltpu.CompilerParams(dimension_semantics=("arbitrary", "arbitrary"),
                                             vmem_limit_bytes=40 * MIB),
    )(a, b)


def _adamw(w, g, m, v):
    m = ADAM_B1 * m + (1.0 - ADAM_B1) * g
    v = ADAM_B2 * v + (1.0 - ADAM_B2) * (g * g)
    m_hat = m / (1.0 - ADAM_B1 ** ADAM_STEP)
    v_hat = v / (1.0 - ADAM_B2 ** ADAM_STEP)
    delta = -ADAM_LR * (m_hat / (jnp.sqrt(v_hat) + ADAM_EPS) + ADAM_WD * w)
    return delta, m, v


def _adam_big(recv, part, me, off, rows, w, m, v, transpose, name, tr):
    dm = recv.shape[2]
    per = rows // tr

    def body(me_ref, own_ref, r_ref, w_ref, m_ref, v_ref, g_ref, d_ref, mo_ref, vo_ref):
        g = own_ref[...].astype(F32)
        for k in range(len(FLIPS)):
            g = g + r_ref[k].astype(F32)
        if transpose:
            g = g.T
        delta, m_new, v_new = _adamw(w_ref[...], g, m_ref[...], v_ref[...])
        g_ref[...] = g
        d_ref[...] = delta
        mo_ref[...] = m_new
        vo_ref[...] = v_new

    if transpose:
        blk = pl.BlockSpec((dm, tr), lambda i, me_ref: (0, i))
    else:
        blk = pl.BlockSpec((tr, dm), lambda i, me_ref: (i, 0))
    first = off // tr
    return pl.pallas_call(
        body, name=name,
        grid_spec=pltpu.PrefetchScalarGridSpec(
            num_scalar_prefetch=1, grid=(per,),
            in_specs=[pl.BlockSpec((tr, dm), lambda i, me_ref: (me_ref[0] * per + i, 0)),
                      pl.BlockSpec((len(FLIPS), tr, dm), lambda i, me_ref: (0, first + i, 0)), blk, blk, blk],
            out_specs=[blk] * 4),
        out_shape=[jax.ShapeDtypeStruct(w.shape, F32)] * 4,
        compiler_params=_params(32),
    )(me, part, recv, w, m, v)


LOSS_ROW = 15
CONV_A_ROW = 16
CONV_B_ROW = 24


def _adam_small(recv_small, recv_last, me, params, d_model):
    n = len(params)
    cw = d_model // N_DEV

    def body(me_ref, r_ref, rc_ref, l_ref, *refs):
        ins, loss_ref, outs = refs[:3 * n], refs[3 * n], refs[3 * n + 1:3 * n + 1 + 4 * n]
        g_v, gc_v, last_v = refs[3 * n + 1 + 4 * n:]
        g, gc, last = r_ref[0], rc_ref[0], l_ref[0]
        for d in range(1, N_DEV):
            g, gc, last = g + r_ref[d], gc + rc_ref[d], last + l_ref[d]
        g_v[...], gc_v[...], last_v[...] = g, gc, last
        loss_ref[...] = (0.5 / d_model) * jnp.sum(g_v[LOSS_ROW:LOSS_ROW + 1, :], axis=-1, keepdims=True)
        for j, (row0, own_columns, (w, _, _)) in enumerate(params):
            w_ref, m_ref, v_ref = ins[3 * j:3 * j + 3]
            source = gc_v if own_columns else (last_v if row0 == 0 else g_v)
            grad = source[row0:row0 + w.shape[0], :]
            delta, m_new, v_new = _adamw(w_ref[...], grad, m_ref[...], v_ref[...])
            for ref, val in zip(outs[4 * j:4 * j + 4], (grad, delta, m_new, v_new)):
                ref[...] = val

    full = lambda shape: pl.BlockSpec(shape, lambda i, me_ref: (0,) * len(shape))
    stack_rows = recv_small.shape[1]
    flat = [a for _, _, triple in params for a in triple]
    shapes = [w.shape for _, _, (w, _, _) in params for _ in range(4)]
    out = pl.pallas_call(
        body, name="adam_small",
        grid_spec=pltpu.PrefetchScalarGridSpec(
            num_scalar_prefetch=1, grid=(1,),
            in_specs=[full(recv_small.shape),
                      pl.BlockSpec((N_DEV, stack_rows, cw), lambda i, me_ref: (0, 0, me_ref[0])),
                      full(recv_last.shape)] + [full(a.shape) for a in flat],
            out_specs=[full((1, 1))] + [full(s) for s in shapes],
            scratch_shapes=[pltpu.VMEM((stack_rows, d_model), F32), pltpu.VMEM((stack_rows, cw), F32),
                            pltpu.VMEM(recv_last.shape[1:], F32)]),
        out_shape=[jax.ShapeDtypeStruct((1, 1), F32)] + [jax.ShapeDtypeStruct(s, F32) for s in shapes],
    )(me, recv_small, recv_small, recv_last, *flat)
    return out[0], [tuple(out[1 + 4 * j:5 + 4 * j]) for j in range(n)]


def _tile(n_tok, want):
    return min(want, n_tok)


def kernel(x, norm1_pre_g, w_in, b_in, conv_a_w, conv_a_b, w_a_out, conv_b_w, conv_b_b, ln_b_g, ln_b_b, w_b_out, w_o, norm1_post_g, norm2_pre_g, w_mlp_in, w_mlp_out, norm2_post_g, loss_target, m_norm1_pre_g, m_w_in, m_b_in, m_conv_a_w, m_conv_a_b, m_w_a_out, m_conv_b_w, m_conv_b_b, m_ln_b_g, m_ln_b_b, m_w_b_out, m_w_o, m_norm1_post_g, m_norm2_pre_g, m_w_mlp_in, m_w_mlp_out, m_norm2_post_g, v_norm1_pre_g, v_w_in, v_b_in, v_conv_a_w, v_conv_a_b, v_w_a_out, v_conv_b_w, v_conv_b_b, v_ln_b_g, v_ln_b_b, v_w_b_out, v_w_o, v_norm1_post_g, v_norm2_pre_g, v_w_mlp_in, v_w_mlp_out, v_norm2_post_g):
    n_tok, dm = x.shape[1], x.shape[2]
    rows, off, slab_rows = _layout(dm)
    cw = dm // N_DEV
    xs = x.reshape(n_tok, dm)
    tgt = loss_target.reshape(n_tok, dm)
    row = lambda vec: vec.reshape(1, -1)
    gathered = lambda group: (N_DEV, slab_rows[group], dm)
    scattered = lambda group: jax.ShapeDtypeStruct((len(FLIPS), slab_rows[group], dm), BF16)
    tm, tk = min(dm, 1024), _tile(n_tok, 2048)
    me = (4 * lax.axis_index("x") + 2 * lax.axis_index("y") + lax.axis_index("c")).astype(jnp.int32).reshape(1)

    conv_own = jnp.concatenate([conv_a_w, jnp.zeros((SUBLANE - CONV_A, cw), F32), conv_b_w,
                                jnp.zeros((1, cw), F32)], axis=0)
    own_in = w_in.T.astype(BF16)
    slab_in, conv_all = _all_gather_two_level(
        [own_in, conv_own], [_place_own(own_in, gathered("in"), me, "place_w_in"), None], "gather_w_in")
    conv_full = conv_all.transpose(1, 0, 2).reshape(conv_own.shape[0], dm)
    caw, cbw = conv_full[0:CONV_A], conv_full[SUBLANE:SUBLANE + CONV_B]
    own_abo = jnp.concatenate([w_a_out, w_b_out, w_o], axis=0).astype(BF16)
    own_mlp = jnp.concatenate([w_mlp_in.T, w_mlp_out], axis=0).astype(BF16)
    ag_abo = _exchange_start([_Part(own_abo, False, slab_rows["abo"], 0, 0)],
                             [_place_own(own_abo, gathered("abo"), me, "place_abo")],
                             "gather_abo_start", after=slab_in)
    ag_mlp = _exchange_start([_Part(own_mlp, False, slab_rows["mlp"], 0, 0)],
                             [_place_own(own_mlp, gathered("mlp"), me, "place_mlp")],
                             "gather_mlp_start", after=ag_abo.token)

    proj, p, u, h = _fwd_in(xs, row(norm1_pre_g), row(b_in), slab_in, _tile(n_tok, 512),
                            [ag_abo.token, ag_mlp.token])
    _, (slab_abo,) = _exchange_wait(ag_abo, "gather_abo_wait", after=proj)
    va, vb, ya, yb, qa, sb, merged, mix, x1 = _fwd_mix(
        p, u, proj, xs, caw, row(conv_a_b), cbw, row(conv_b_b), row(ln_b_g), row(ln_b_b), row(norm1_post_g),
        slab_abo, _tile(n_tok, 256))
    _, (slab_mlp,) = _exchange_wait(ag_mlp, "gather_mlp_wait", after=x1)
    f, df1, h2, df2, dmix, dx1, small_mlp = _mlp_fwd_bwd(
        x1, mix, tgt, row(norm1_post_g), row(norm2_pre_g), row(norm2_post_g), slab_mlp, _tile(n_tok, 256))

    rs_mlp = _exchange_start(
        [_Part(_wgrad(df1, h2, "wgrad_mlp_in", tm, tk, BF16), True, rows["w1"], 0, off["w1"]),
         _Part(_wgrad(f, df2, "wgrad_mlp_out", tm, tk, BF16), True, rows["w2"], 0, off["w2"])],
        [scattered("mlp")], "scatter_mlp_start")
    dpa, dya, dyb, dva, dvb, small_mix = _bwd_mix(
        dmix, ya, yb, proj, va, vb, row(ln_b_g), row(ln_b_b), slab_abo, _tile(n_tok, 256), [rs_mlp.token])
    rs_abo = _exchange_start(
        [_Part(_wgrad(qa, dya, "wgrad_a_out", tm, tk, BF16), True, rows["wa"], 0, off["wa"]),
         _Part(_wgrad(sb, dyb, "wgrad_b_out", tm, tk, BF16), True, rows["wb"], 0, off["wb"]),
         _Part(_wgrad(merged, dmix, "wgrad_o", tm, tk, BF16), True, rows["wo"], 0, off["wo"])],
        [scattered("abo")], "scatter_abo_start")
    dproj, small_conv = _bwd_conv(dva, dvb, p, u, proj, dpa, caw, cbw, _tile(n_tok, 256), [rs_abo.token])

    zeros = lambda r: jnp.zeros((r, dm), F32)
    small = jnp.concatenate([
        zeros(1),
        small_mix[2:3],
        small_mix[3:4],
        small_mix[0:2],
        small_mlp[2:3],
        small_mlp[1:2],
        small_mlp[0:1],
        small_mix[4:5], small_conv[3:7], small_mix[5:7],
        small_mlp[3:4],
        small_conv[0:CONV_A], zeros(SUBLANE - CONV_A),
        small_conv[8:8 + CONV_B], zeros(1),
    ], axis=0)

    rs_in = _exchange_start(
        [_Part(_wgrad(dproj, h, "wgrad_in", tm, tk, BF16), True, rows["win"], 0, off["win"]),
         _Part(small, False, small.shape[0], 1, 0)],
        [scattered("in"), _place_own(small, (N_DEV,) + small.shape, me, "place_small")], "scatter_in_start")
    grad_x, small_in = _bwd_in(dproj, xs, dx1, row(norm1_pre_g), slab_in, _tile(n_tok, 256), [rs_in.token])

    tr = min(LANE, rows["wa"])
    (g_w1, g_w2), (recv_mlp,) = _exchange_wait(rs_mlp, "scatter_mlp_wait", after=grad_x)
    (g_wa, g_wb, g_wo), (recv_abo,) = _exchange_wait(rs_abo, "scatter_abo_wait", after=grad_x)
    big = {
        "w_mlp_in": _adam_big(recv_mlp, g_w1, me, off["w1"], rows["w1"], w_mlp_in, m_w_mlp_in, v_w_mlp_in, True,
                              "adam_w_mlp_in", tr),
        "w_mlp_out": _adam_big(recv_mlp, g_w2, me, off["w2"], rows["w2"], w_mlp_out, m_w_mlp_out, v_w_mlp_out,
                               False, "adam_w_mlp_out", tr),
        "w_a_out": _adam_big(recv_abo, g_wa, me, off["wa"], rows["wa"], w_a_out, m_w_a_out, v_w_a_out, False,
                             "adam_w_a_out", tr),
        "w_b_out": _adam_big(recv_abo, g_wb, me, off["wb"], rows["wb"], w_b_out, m_w_b_out, v_w_b_out, False,
                             "adam_w_b_out", tr),
        "w_o": _adam_big(recv_abo, g_wo, me, off["wo"], rows["wo"], w_o, m_w_o, v_w_o, False, "adam_w_o", tr),
    }
    (g_win, _), (recv_in, recv_small) = _exchange_wait(rs_in, "scatter_in_wait", after=big["w_o"][3])
    recv_last, = _all_gather([small_in], "gather_last")
    big["w_in"] = _adam_big(recv_in, g_win, me, off["win"], rows["win"], w_in, m_w_in, v_w_in, True, "adam_w_in", tr)

    small_names = ("norm1_pre_g", "conv_a_b", "conv_b_b", "ln_b_g", "ln_b_b", "norm1_post_g", "norm2_pre_g",
                   "norm2_post_g")
    given = dict(
        norm1_pre_g=(norm1_pre_g, m_norm1_pre_g, v_norm1_pre_g), conv_a_b=(conv_a_b, m_conv_a_b, v_conv_a_b),
        conv_b_b=(conv_b_b, m_conv_b_b, v_conv_b_b), ln_b_g=(ln_b_g, m_ln_b_g, v_ln_b_g),
        ln_b_b=(ln_b_b, m_ln_b_b, v_ln_b_b), norm1_post_g=(norm1_post_g, m_norm1_post_g, v_norm1_post_g),
        norm2_pre_g=(norm2_pre_g, m_norm2_pre_g, v_norm2_pre_g),
        norm2_post_g=(norm2_post_g, m_norm2_post_g, v_norm2_post_g))
    params = [(j, False, tuple(row(a) for a in given[name])) for j, name in enumerate(small_names)]
    params.append((SUBLANE, False, tuple(a.reshape(7, dm) for a in (b_in, m_b_in, v_b_in))))
    params.append((CONV_A_ROW, True, (conv_a_w, m_conv_a_w, v_conv_a_w)))
    params.append((CONV_B_ROW, True, (conv_b_w, m_conv_b_w, v_conv_b_w)))
    loss, small_out = _adam_small(recv_small, recv_last, me, params, dm)
    small_leaves = {name: tuple(a.reshape(dm) for a in small_out[j]) for j, name in enumerate(small_names)}
    small_leaves["b_in"] = tuple(a.reshape(7 * dm) for a in small_out[len(small_names)])
    small_leaves["conv_a_w"] = small_out[len(small_names) + 1]
    small_leaves["conv_b_w"] = small_out[len(small_names) + 2]

    order = ("norm1_pre_g", "w_in", "b_in", "conv_a_w", "conv_a_b", "w_a_out", "conv_b_w", "conv_b_b", "ln_b_g",
             "ln_b_b", "w_b_out", "w_o", "norm1_post_g", "norm2_pre_g", "w_mlp_in", "w_mlp_out", "norm2_post_g")
    leaves = [big[name] if name in big else small_leaves[name] for name in order]
    grads, deltas, new_m, new_v = zip(*leaves)
    return (loss.reshape(()), grad_x.reshape(x.shape), *grads, *deltas, *new_m, *new_v)
```

```python
from typing import NamedTuple

import jax
import jax.numpy as jnp
from jax import lax
from jax.experimental import pallas as pl
from jax.experimental.pallas import tpu as pltpu

F32 = jnp.float32
BF16 = jnp.bfloat16

RMS_EPS = 1e-6
LN_EPS = 1e-5
ADAM_LR = 0.001
ADAM_B1 = 0.9
ADAM_B2 = 0.999
ADAM_EPS = 1e-08
ADAM_WD = 0.01
ADAM_STEP = 10

N_DEV = 8
CONV_A = 3
CONV_B = 31
LANE = 128
SUBLANE = 8
HALO = 16
FWD_CONV_ROWS = 32
BWD_CONV_ROWS = 64
MIB = 1 << 20
FLIPS = ((0, 0, 1), (0, 1, 0), (1, 0, 0), (0, 1, 1), (1, 0, 1), (1, 1, 0), (1, 1, 1))
MESH = pl.DeviceIdType.MESH


def _layout(d_model):
    e = d_model // N_DEV
    rows = {"win": 7 * e, "w1": 4 * e, "w2": 4 * e, "wa": e, "wb": e, "wo": e}
    off = {"win": 0, "w1": 0, "w2": 4 * e, "wa": 0, "wb": e, "wo": 2 * e}
    return rows, off, {"in": 7 * e, "mlp": 8 * e, "abo": 3 * e}


def _after(body, deps):
    def ordered(*refs):
        return body(*refs[len(deps):])
    return ordered


def _params(vmem_mib):
    return pltpu.CompilerParams(dimension_semantics=("arbitrary",), vmem_limit_bytes=vmem_mib * MIB)


def _whole():
    return pl.BlockSpec(memory_space=pltpu.VMEM)


def _hbm():
    return pl.BlockSpec(memory_space=pl.ANY)


def _rows(ts, width, col=0):
    return pl.BlockSpec((ts, width), lambda i: (i, col))


def _halo_specs(ts, width, n_rows):
    per = ts // HALO
    last = n_rows // HALO - 1
    return [
        pl.BlockSpec((ts, width), lambda i: (i, 0)),
        pl.BlockSpec((HALO, width), lambda i: (jnp.maximum(i * per - 1, 0), 0)),
        pl.BlockSpec((HALO, width), lambda i: (jnp.minimum((i + 1) * per, last), 0)),
    ]


def _dot(a, b):
    return jnp.dot(a, b, preferred_element_type=F32)


def _dot_nt(a, b):
    return lax.dot_general(a, b, (((1,), (1,)), ((), ())), preferred_element_type=F32)


def _dot_tn(a, b):
    return lax.dot_general(a, b, (((0,), (0,)), ((), ())), preferred_element_type=F32)


def _rms(u):
    return lax.rsqrt(jnp.mean(u * u, axis=-1, keepdims=True) + RMS_EPS)


def _rms_bwd(dz, u, r, g):
    dzg = dz * g
    return r * dzg - u * (r * r * r) * jnp.mean(dzg * u, axis=-1, keepdims=True)


def _colsum(v):
    return jnp.sum(v, axis=0, keepdims=True)


def _sigmoid(v):
    return jax.nn.sigmoid(v)


def _weight_copies(slab_ref, off, rows, dst_ref, sems, first_sem):
    return [pltpu.make_async_copy(slab_ref.at[d, pl.ds(off, rows), :], dst_ref.at[pl.ds(d * rows, rows), :],
                                  sems.at[first_sem + d]) for d in range(N_DEV)]


def _on_first_step(copies, method):
    @pl.when(pl.program_id(0) == 0)
    def _():
        for cp in copies:
            getattr(cp, method)()


def _with_halos(main_ref, prev_ref, next_ref, i, n_steps):
    return (main_ref, jnp.where(i > 0, prev_ref[...], 0.0), jnp.where(i < n_steps - 1, next_ref[...], 0.0))


def _broadcast_taps(w_ref, wb_ref, n_taps):
    for k in range(n_taps):
        wb_ref[k] = jnp.broadcast_to(w_ref[k:k + 1, :], wb_ref.shape[1:])


def _conv_tile(tile, wb_ref, starts, ts, width, emit, rolled_ref, rows=FWD_CONV_ROWS):
    main_ref, prev, nxt = tile
    span = ts + 2 * HALO
    nv = rows // SUBLANE
    for cb in range(width // LANE):
        lanes = slice(cb * LANE, (cb + 1) * LANE)
        slot = cb % 2
        window = jnp.concatenate([prev[:, lanes], main_ref[:, lanes], nxt[:, lanes]], axis=0)
        for b in sorted({st % SUBLANE for st in starts}):
            rolled_ref[slot, b] = window if b == 0 else pltpu.roll(window, span - b, axis=0)
        for r0 in range(0, ts, rows):
            acc = jnp.zeros((nv, SUBLANE, LANE), F32)
            for k, st in enumerate(starts):
                shifted = rolled_ref[slot, st % SUBLANE, pl.ds(r0 + st - st % SUBLANE, rows), :]
                acc = acc + shifted.reshape(nv, SUBLANE, LANE) * wb_ref[k, :, lanes][None]
            emit(r0, pl.ds(cb * LANE, LANE), acc.reshape(rows, LANE))


def _window(tile, r0, cb, ts, rows):
    main_ref, prev, nxt = tile
    lanes = slice(cb * LANE, (cb + 1) * LANE)
    lo, hi = max(r0 - HALO, 0), min(r0 + rows + HALO, ts)
    pieces = [prev[:, lanes]] if r0 - HALO < 0 else []
    pieces.append(main_ref[lo:hi, lanes])
    if r0 + rows + HALO > ts:
        pieces.append(nxt[:, lanes])
    return pieces[0] if len(pieces) == 1 else jnp.concatenate(pieces, axis=0)


def _phases(starts):
    groups = {}
    for k, st in enumerate(starts):
        groups.setdefault(st % SUBLANE, []).append((k, st // SUBLANE))
    return sorted(groups.items())


def _shifted(blk, b):
    n = blk.shape[0]
    rolled = blk if b == 0 else pltpu.roll(blk, n - b, axis=0)
    return rolled.reshape(n // SUBLANE, SUBLANE, blk.shape[1])


def _conv_bwd_tile(dv_tile, u_ref, wb_ref, acc_ref, n_taps, ts, width, emit, rows=BWD_CONV_ROWS):
    groups = _phases(_bwd_starts(n_taps))
    nv = rows // SUBLANE
    for r0 in range(0, ts, rows):
        for cb in range(width // LANE):
            lanes = pl.ds(cb * LANE, LANE)
            blk = _window(dv_tile, r0, cb, ts, rows)
            u = u_ref[pl.ds(r0, rows), lanes].reshape(nv, SUBLANE, LANE)
            du = jnp.zeros((nv, SUBLANE, LANE), F32)
            for b, taps in groups:
                sh = _shifted(blk, b)
                for k, m in taps:
                    du = du + sh[m:m + nv] * wb_ref[k, :, lanes][None]
                    acc_ref[k, :, lanes] += jnp.sum(sh[m:m + nv] * u, axis=0)
            emit(r0, lanes, du.reshape(rows, LANE))


def _fwd_starts(n_taps):
    pad = (n_taps - 1) // 2
    return [HALO - pad + k for k in range(n_taps)]


def _bwd_starts(n_taps):
    pad = (n_taps - 1) // 2
    return [HALO + pad - k for k in range(n_taps)]


def _peer(x, y, c, flip):
    fx, fy, fc = flip
    return (1 - x if fx else x, 1 - y if fy else y, 1 - c if fc else c)


def _all_gather(shards, name):
    n = len(shards)

    def body(*refs):
        ins, outs = refs[:n], refs[n:2 * n]
        send_sems, recv_sems, local_sems = refs[2 * n:]
        x, y, c = lax.axis_index("x"), lax.axis_index("y"), lax.axis_index("c")
        me = 4 * x + 2 * y + c
        local = [pltpu.make_async_copy(ins[j], outs[j].at[me], local_sems.at[j]) for j in range(n)]
        for cp in local:
            cp.start()
        sends, recvs = [], []
        for k, flip in enumerate(FLIPS):
            px, py, pc = _peer(x, y, c, flip)
            peer = 4 * px + 2 * py + pc
            for j in range(n):
                sem = k * n + j
                sends.append(pltpu.make_async_remote_copy(
                    src_ref=ins[j], dst_ref=outs[j].at[me], send_sem=send_sems.at[sem], recv_sem=recv_sems.at[sem],
                    device_id=(px, py, pc), device_id_type=MESH))
                recvs.append(pltpu.make_async_remote_copy(
                    src_ref=ins[j], dst_ref=outs[j].at[peer], send_sem=send_sems.at[sem], recv_sem=recv_sems.at[sem],
                    device_id=(px, py, pc), device_id_type=MESH))
        for cp in sends:
            cp.start()
        for cp in recvs:
            cp.wait_recv()
        for cp in sends:
            cp.wait_send()
        for cp in local:
            cp.wait()

    return pl.pallas_call(
        body, name=name,
        out_shape=[jax.ShapeDtypeStruct((N_DEV,) + s.shape, s.dtype) for s in shards],
        in_specs=[_hbm()] * n, out_specs=[_hbm()] * n,
        scratch_shapes=[pltpu.SemaphoreType.DMA((7 * n,)), pltpu.SemaphoreType.DMA((7 * n,)),
                        pltpu.SemaphoreType.DMA((n,))],
    )(*shards)


def _place_own(src, n_slots_shape, me, name):
    rows, width = src.shape
    tr = next(t for t in (256, 128, 64, 32, 16, SUBLANE) if rows % t == 0)

    def body(me_ref, src_ref, out_ref):
        out_ref[...] = src_ref[...]

    return pl.pallas_call(
        body, name=name,
        grid_spec=pltpu.PrefetchScalarGridSpec(
            num_scalar_prefetch=1, grid=(rows // tr,),
            in_specs=[pl.BlockSpec((tr, width), lambda i, me_ref: (i, 0))],
            out_specs=pl.BlockSpec((None, tr, width), lambda i, me_ref: (me_ref[0], i, 0))),
        out_shape=jax.ShapeDtypeStruct(n_slots_shape, src.dtype),
    )(me, src)


def _place_cast(pieces, me, name):
    n = len(pieces)
    counts = [a.shape[1] if t else a.shape[0] for a, t in pieces]
    width = pieces[0][0].shape[0] if pieces[0][1] else pieces[0][0].shape[1]
    total = sum(counts)

    def body(me_ref, *refs):
        ins, own_ref, land_ref = refs[:n], refs[n], refs[n + 1]
        first = 0
        for (a, transpose), in_ref, count in zip(pieces, ins, counts):
            block = (in_ref[...].T if transpose else in_ref[...]).astype(BF16)
            own_ref[first:first + count, :] = block
            land_ref[first:first + count, :] = block
            first += count

    return pl.pallas_call(
        body, name=name,
        grid_spec=pltpu.PrefetchScalarGridSpec(
            num_scalar_prefetch=1, grid=(1,),
            in_specs=[pl.BlockSpec(a.shape, lambda i, me_ref: (0, 0)) for a, _ in pieces],
            out_specs=[pl.BlockSpec((total, width), lambda i, me_ref: (0, 0)),
                       pl.BlockSpec((None, total, width), lambda i, me_ref: (me_ref[0], 0, 0))]),
        out_shape=[jax.ShapeDtypeStruct((total, width), BF16), jax.ShapeDtypeStruct((N_DEV, total, width), BF16)],
        compiler_params=_params(32),
    )(me, *[a for a, _ in pieces])


def _all_gather_two_level(shards, placed, name):
    n = len(shards)
    given = [j for j in range(n) if placed[j] is not None]

    def body(*refs):
        ins, outs = refs[:n], refs[n + len(given):2 * n + len(given)]
        send_sems, recv_sems, local_sems = refs[2 * n + len(given):]
        x, y, c = lax.axis_index("x"), lax.axis_index("y"), lax.axis_index("c")
        me, sibling = (x, y, c), (x, y, 1 - c)
        chips = [(1 - x, y), (x, 1 - y), (1 - x, 1 - y)]

        def slot(j, dev):
            return outs[j].at[4 * dev[0] + 2 * dev[1] + dev[2]]

        def copy(k, j, block, to, src=None):
            return pltpu.make_async_remote_copy(
                src_ref=slot(j, block) if src is None else src, dst_ref=slot(j, block),
                send_sem=send_sems.at[k * n + j], recv_sem=recv_sems.at[k * n + j], device_id=to, device_id_type=MESH)

        local = [pltpu.make_async_copy(ins[j], slot(j, me), local_sems.at[j]) for j in range(n) if j not in given]
        for cp in local:
            cp.start()
        first = [copy(0, j, me, sibling, src=ins[j]) for j in range(n)]
        first += [copy(1 + t, j, me, (*chip, c), src=ins[j]) for t, chip in enumerate(chips) for j in range(n)]
        for cp in first:
            cp.start()
        passed = []
        for t, chip in enumerate(chips):
            for j in range(n):
                copy(1 + t, j, (*chip, c), me).wait_recv()
                passed.append(copy(4 + t, j, (*chip, c), sibling))
                passed[-1].start()
        for j in range(n):
            copy(0, j, sibling, me).wait_recv()
        for t, chip in enumerate(chips):
            for j in range(n):
                copy(4 + t, j, (*chip, 1 - c), me).wait_recv()
        for cp in first + passed:
            cp.wait_send()
        for cp in local:
            cp.wait()

    return pl.pallas_call(
        body, name=name,
        out_shape=[jax.ShapeDtypeStruct((N_DEV,) + s.shape, s.dtype) for s in shards],
        in_specs=[_hbm()] * (n + len(given)), out_specs=[_hbm()] * n,
        input_output_aliases={n + i: j for i, j in enumerate(given)},
        scratch_shapes=[pltpu.SemaphoreType.DMA((7 * n,)), pltpu.SemaphoreType.DMA((7 * n,)),
                        pltpu.SemaphoreType.DMA((n,))],
    )(*shards, *[placed[j] for j in given])


class _Part(NamedTuple):
    src: jax.Array
    scatter: bool
    rows: int
    land: int
    off: int


class _Started(NamedTuple):
    send_sems: jax.Array
    recv_sems: jax.Array
    thru: tuple
    token: jax.Array
    parts: tuple


def _exchange_copies(srcs, lands, send_sems, recv_sems, parts):
    n = len(parts)
    x, y, c = lax.axis_index("x"), lax.axis_index("y"), lax.axis_index("c")
    me = 4 * x + 2 * y + c

    def block(j, dev):
        p = parts[j]
        return srcs[j].at[pl.ds(pl.multiple_of(dev * p.rows, SUBLANE), p.rows), :] if p.scatter else srcs[j]

    def slot(j, index):
        p = parts[j]
        return lands[p.land].at[index, pl.ds(p.off, p.rows), :]

    sends, recvs = [], []
    for k, flip in enumerate(FLIPS):
        px, py, pc = _peer(x, y, c, flip)
        peer = 4 * px + 2 * py + pc
        for j in range(n):
            sems = dict(send_sem=send_sems.at[k * n + j], recv_sem=recv_sems.at[k * n + j],
                        device_id=(px, py, pc), device_id_type=MESH)
            to, got = (k, k) if parts[j].scatter else (me, peer)
            sends.append(pltpu.make_async_remote_copy(src_ref=block(j, peer), dst_ref=slot(j, to), **sems))
            recvs.append(pltpu.make_async_remote_copy(src_ref=block(j, peer), dst_ref=slot(j, got), **sems))
    return sends, recvs


def _exchange_start(parts, lands, name, after=None):
    n, nl = len(parts), len(lands)
    n_in = n + nl + (after is not None)

    def body(*refs):
        srcs, land_refs = refs[:n], refs[n:n + nl]
        send_sems, recv_sems = refs[n_in], refs[n_in + 1]
        token = refs[n_in + 2 + n + nl]
        sends, _ = _exchange_copies(srcs, land_refs, send_sems, recv_sems, parts)
        for cp in sends:
            cp.start()
        token[...] = jnp.zeros_like(token)

    hbm = pl.BlockSpec(memory_space=pltpu.HBM)
    sem = pl.BlockSpec(memory_space=pltpu.SEMAPHORE)
    fresh = lambda s: lax.empty(s.shape, s.dtype) if isinstance(s, jax.ShapeDtypeStruct) else s
    args = [pltpu.with_memory_space_constraint(p.src, pltpu.HBM) for p in parts]
    args += [pltpu.with_memory_space_constraint(fresh(s), pltpu.HBM) for s in lands]
    args += [] if after is None else [after]
    out = pl.pallas_call(
        body, name=name,
        out_shape=(pltpu.SemaphoreType.DMA((7 * n,)), pltpu.SemaphoreType.DMA((7 * n,)),
                   *[pltpu.HBM(a.shape, a.dtype) for a in args[:n + nl]], jax.ShapeDtypeStruct((SUBLANE, LANE), F32)),
        in_specs=[hbm] * (n + nl) + [_hbm()] * (after is not None),
        out_specs=(sem, sem, *[hbm] * (n + nl), _whole()),
        input_output_aliases={j: 2 + j for j in range(n + nl)},
        compiler_params=pltpu.CompilerParams(has_side_effects=pltpu.SideEffectType.DATAFLOW_SIDE_EFFECTING),
    )(*args)
    return _Started(out[0], out[1], tuple(out[2:2 + n + nl]), out[2 + n + nl], tuple(parts))


def _exchange_wait(started, name, after):
    parts = started.parts
    n, nl = len(parts), len(started.thru) - len(parts)

    def body(*refs):
        srcs, land_refs = refs[:n], refs[n:n + nl]
        send_sems, recv_sems = refs[n + nl], refs[n + nl + 1]
        sends, recvs = _exchange_copies(srcs, land_refs, send_sems, recv_sems, parts)
        for cp in sends:
            cp.wait_send()
        for cp in recvs:
            cp.wait_recv()

    hbm = pl.BlockSpec(memory_space=pltpu.HBM)
    sem = pl.BlockSpec(memory_space=pltpu.SEMAPHORE)
    out = pl.pallas_call(
        body, name=name,
        out_shape=tuple(pltpu.HBM(a.shape, a.dtype) for a in started.thru),
        in_specs=[hbm] * (n + nl) + [sem, sem, _hbm()], out_specs=tuple([hbm] * (n + nl)),
        input_output_aliases={j: j for j in range(n + nl)},
        compiler_params=pltpu.CompilerParams(has_side_effects=pltpu.SideEffectType.DATAFLOW_SIDE_EFFECTING),
    )(*started.thru, started.send_sems, started.recv_sems, after)
    return list(out[:n]), list(out[n:])


def _fwd_in(x, g1, b_in, slab, ts, deps):
    n_tok, dm = x.shape
    rows, off, _ = _layout(dm)
    width = 7 * dm

    def body(x_ref, g1_ref, b_ref, slab_ref, proj_ref, p_ref, u_ref, h_ref, w_v, sems):
        copies = _weight_copies(slab_ref, off["win"], rows["win"], w_v, sems, 0)
        _on_first_step(copies, "start")
        _on_first_step(copies, "wait")
        xv = x_ref[...]
        h = (xv * _rms(xv) * g1_ref[...]).astype(BF16)
        h_ref[...] = h
        cols = []
        for j in range(7):
            pj = _dot_nt(h, w_v[pl.ds(j * dm, dm), :]) + b_ref[:, j * dm:(j + 1) * dm]
            proj_ref[:, j * dm:(j + 1) * dm] = pj.astype(proj_ref.dtype)
            if 1 <= j <= 4:
                cols.append(pj)
            if j == 2:
                p_ref[...] = cols[0] * cols[1]
            if j == 4:
                u_ref[...] = cols[2] * _sigmoid(cols[3])

    return pl.pallas_call(
        _after(body, deps), name="fwd_in", grid=(n_tok // ts,),
        in_specs=[_whole()] * len(deps) + [_rows(ts, dm), _whole(), _whole(), _hbm()],
        out_specs=[_rows(ts, width), _rows(ts, dm), _rows(ts, dm), _rows(ts, dm)],
        out_shape=[jax.ShapeDtypeStruct((n_tok, width), BF16), jax.ShapeDtypeStruct((n_tok, dm), F32),
                   jax.ShapeDtypeStruct((n_tok, dm), F32), jax.ShapeDtypeStruct((n_tok, dm), BF16)],
        scratch_shapes=[pltpu.VMEM((width, dm), BF16), pltpu.SemaphoreType.DMA((N_DEV,))],
        compiler_params=_params(56),
    )(*deps, x, g1, b_in, slab)


def _fwd_mix(p, u, proj, x, caw, cab, cbw, cbb, lng, lnb, g1post, slab, ts):
    n_tok, dm = x.shape
    rows, off, _ = _layout(dm)
    n_steps = n_tok // ts

    def body(p_ref, p_prev, p_next, u_ref, u_prev, u_next, bg_ref, za_ref, zb_ref, x_ref,
             caw_ref, cab_ref, cbw_ref, cbb_ref, lng_ref, lnb_ref, g1p_ref, slab_ref,
             va_ref, vb_ref, ya_ref, yb_ref, qa_ref, sb_ref, mg_ref, mix_ref, x1_ref,
             wa_v, wb_v, wo_v, tap_a, tap_b, sems, rolled_a, rolled_b):
        i = pl.program_id(0)
        copies = (_weight_copies(slab_ref, off["wa"], rows["wa"], wa_v, sems, 0)
                  + _weight_copies(slab_ref, off["wb"], rows["wb"], wb_v, sems, N_DEV)
                  + _weight_copies(slab_ref, off["wo"], rows["wo"], wo_v, sems, 2 * N_DEV))
        _on_first_step(copies, "start")

        @pl.when(i == 0)
        def _():
            _broadcast_taps(caw_ref, tap_a, CONV_A)
            _broadcast_taps(cbw_ref, tap_b, CONV_B)

        def emit_a(r0, lanes, acc):
            va_ref[pl.ds(r0, acc.shape[0]), lanes] = acc + cab_ref[:, lanes]

        def emit_b(r0, lanes, acc):
            vb_ref[pl.ds(r0, acc.shape[0]), lanes] = acc + cbb_ref[:, lanes]

        _conv_tile(_with_halos(p_ref, p_prev, p_next, i, n_steps), tap_a, _fwd_starts(CONV_A), ts, dm, emit_a,
                   rolled_a)
        _conv_tile(_with_halos(u_ref, u_prev, u_next, i, n_steps), tap_b, _fwd_starts(CONV_B), ts, dm, emit_b,
                   rolled_b)
        _on_first_step(copies, "wait")

        qa = (bg_ref[...].astype(F32) * va_ref[...]).astype(BF16)
        qa_ref[...] = qa
        ya = _dot(qa, wa_v[...])
        vb = vb_ref[...]
        xc = vb - jnp.mean(vb, axis=-1, keepdims=True)
        rstd = lax.rsqrt(jnp.mean(xc * xc, axis=-1, keepdims=True) + LN_EPS)
        ln = xc * rstd * lng_ref[...] + lnb_ref[...]
        sb = (ln * _sigmoid(ln)).astype(BF16)
        sb_ref[...] = sb
        yb = _dot(sb, wb_v[...])
        ya_ref[...] = ya.astype(BF16)
        yb_ref[...] = yb.astype(BF16)
        merged = (_sigmoid(za_ref[...].astype(F32)) * ya + _sigmoid(zb_ref[...].astype(F32)) * yb).astype(BF16)
        mg_ref[...] = merged
        mix = _dot(merged, wo_v[...])
        mix_ref[...] = mix
        x1_ref[...] = x_ref[...] + mix * _rms(mix) * g1p_ref[...]

    tok = lambda dt: jax.ShapeDtypeStruct((n_tok, dm), dt)
    return pl.pallas_call(
        body, name="fwd_mix", grid=(n_steps,),
        in_specs=(_halo_specs(ts, dm, n_tok) + _halo_specs(ts, dm, n_tok)
                  + [_rows(ts, dm, 0), _rows(ts, dm, 5), _rows(ts, dm, 6), _rows(ts, dm)]
                  + [_whole()] * 7 + [_hbm()]),
        out_specs=[_rows(ts, dm)] * 9,
        out_shape=[tok(F32), tok(F32), tok(BF16), tok(BF16), tok(BF16), tok(BF16), tok(BF16), tok(F32), tok(F32)],
        scratch_shapes=[pltpu.VMEM((dm, dm), BF16), pltpu.VMEM((dm, dm), BF16), pltpu.VMEM((dm, dm), BF16),
                        pltpu.VMEM((CONV_A, SUBLANE, dm), F32), pltpu.VMEM((CONV_B, SUBLANE, dm), F32),
                        pltpu.SemaphoreType.DMA((3 * N_DEV,)),
                        pltpu.VMEM((2, SUBLANE, ts + 2 * HALO, LANE), F32),
                        pltpu.VMEM((2, SUBLANE, ts + 2 * HALO, LANE), F32)],
        compiler_params=_params(48),
    )(p, p, p, u, u, u, proj, proj, proj, x, caw, cab, cbw, cbb, lng, lnb, g1post, slab)


def _mlp_fwd_bwd(x1, mix, tgt, g1post, g2pre, g2post, slab, ts):
    n_tok, dm = x1.shape
    rows, off, _ = _layout(dm)
    ff = 4 * dm

    def body(x1_ref, mix_ref, t_ref, g1p_ref, g2pre_ref, g2post_ref, slab_ref,
             f_ref, df1_ref, h2_ref, df2_ref, dmix_ref, dx1_ref, small_ref, w1_v, w2_v, relu_v, sems):
        w1_copies = _weight_copies(slab_ref, off["w1"], rows["w1"], w1_v, sems, 0)
        w2_copies = _weight_copies(slab_ref, off["w2"], rows["w2"], w2_v, sems, N_DEV)
        _on_first_step(w1_copies + w2_copies, "start")

        @pl.when(pl.program_id(0) == 0)
        def _():
            small_ref[...] = jnp.zeros_like(small_ref)

        _on_first_step(w1_copies + w2_copies, "wait")
        x1v = x1_ref[...]
        r3 = _rms(x1v)
        g2pre = g2pre_ref[...]
        h2 = (x1v * r3 * g2pre).astype(BF16)
        h2_ref[...] = h2
        for c in range(4):
            blk = pl.ds(c * dm, dm)
            relu = jnp.maximum(_dot_nt(h2, w1_v[blk, :]), 0.0)
            relu_v[:, c * dm:(c + 1) * dm] = relu
            f_ref[:, c * dm:(c + 1) * dm] = (relu * relu).astype(BF16)
        f2 = _dot(f_ref[...], w2_v[...])
        r4 = _rms(f2)
        g2post = g2post_ref[...]
        err = x1v + f2 * r4 * g2post - t_ref[...]
        dy = err * (1.0 / dm)
        small_ref[3:4, :] += _colsum(err * err)
        small_ref[0:1, :] += _colsum(dy * f2 * r4)
        df2 = _rms_bwd(dy, f2, r4, g2post).astype(BF16)
        df2_ref[...] = df2
        for c in range(4):
            blk = pl.ds(c * dm, dm)
            df1 = (_dot_nt(df2, w2_v[blk, :]) * (2.0 * relu_v[:, c * dm:(c + 1) * dm])).astype(BF16)
            df1_ref[:, c * dm:(c + 1) * dm] = df1
        dh2 = _dot(df1_ref[...], w1_v[...])
        small_ref[1:2, :] += _colsum(dh2 * x1v * r3)
        dx1 = dy + _rms_bwd(dh2, x1v, r3, g2pre)
        dx1_ref[...] = dx1
        mixv = mix_ref[...]
        r2 = _rms(mixv)
        small_ref[2:3, :] += _colsum(dx1 * mixv * r2)
        dmix_ref[...] = _rms_bwd(dx1, mixv, r2, g1p_ref[...]).astype(BF16)

    tok = lambda w, dt: jax.ShapeDtypeStruct((n_tok, w), dt)
    return pl.pallas_call(
        body, name="mlp_fwd_bwd", grid=(n_tok // ts,),
        in_specs=[_rows(ts, dm)] * 3 + [_whole()] * 3 + [_hbm()],
        out_specs=[_rows(ts, ff), _rows(ts, ff), _rows(ts, dm), _rows(ts, dm), _rows(ts, dm), _rows(ts, dm),
                   pl.BlockSpec((SUBLANE, dm), lambda i: (0, 0))],
        out_shape=[tok(ff, BF16), tok(ff, BF16), tok(dm, BF16), tok(dm, BF16), tok(dm, BF16), tok(dm, F32),
                   jax.ShapeDtypeStruct((SUBLANE, dm), F32)],
        scratch_shapes=[pltpu.VMEM((ff, dm), BF16), pltpu.VMEM((ff, dm), BF16), pltpu.VMEM((ts, ff), F32),
                        pltpu.SemaphoreType.DMA((2 * N_DEV,))],
        compiler_params=_params(56),
    )(x1, mix, tgt, g1post, g2pre, g2post, slab)


def _bwd_mix(dmix, ya, yb, proj, va, vb, lng, lnb, slab, ts, deps):
    n_tok, dm = dmix.shape
    rows, off, _ = _layout(dm)
    n_steps = n_tok // ts

    def body(dmix_ref, ya_ref, yb_ref, bg_ref, za_ref, zb_ref, va_ref, vb_ref, lng_ref, lnb_ref, slab_ref,
             dpa_ref, dya_ref, dyb_ref, dva_ref, dvb_ref, small_ref, wa_v, wb_v, wo_v, sems):
        wo_copies = _weight_copies(slab_ref, off["wo"], rows["wo"], wo_v, sems, 2 * N_DEV)
        ab_copies = (_weight_copies(slab_ref, off["wa"], rows["wa"], wa_v, sems, 0)
                     + _weight_copies(slab_ref, off["wb"], rows["wb"], wb_v, sems, N_DEV))
        _on_first_step(wo_copies + ab_copies, "start")

        @pl.when(pl.program_id(0) == 0)
        def _():
            small_ref[...] = jnp.zeros_like(small_ref)

        _on_first_step(wo_copies, "wait")
        dmerged = _dot_nt(dmix_ref[...], wo_v[...])
        _on_first_step(ab_copies, "wait")
        sa = _sigmoid(za_ref[...].astype(F32))
        sg = _sigmoid(zb_ref[...].astype(F32))
        dza = dmerged * ya_ref[...].astype(F32) * sa * (1.0 - sa)
        dzb = dmerged * yb_ref[...].astype(F32) * sg * (1.0 - sg)
        dpa_ref[:, dm:2 * dm] = dza.astype(BF16)
        dpa_ref[:, 2 * dm:3 * dm] = dzb.astype(BF16)
        small_ref[5:6, :] += _colsum(dza)
        small_ref[6:7, :] += _colsum(dzb)

        dya = (dmerged * sa).astype(BF16)
        dya_ref[...] = dya
        dqa = _dot_nt(dya, wa_v[...])
        dbg = dqa * va_ref[...]
        dpa_ref[:, 0:dm] = dbg.astype(BF16)
        small_ref[4:5, :] += _colsum(dbg)
        dva = dqa * bg_ref[...].astype(F32)
        dva_ref[...] = dva
        small_ref[2:3, :] += _colsum(dva)

        dyb = (dmerged * sg).astype(BF16)
        dyb_ref[...] = dyb
        dsb = _dot_nt(dyb, wb_v[...])
        vb = vb_ref[...]
        xc = vb - jnp.mean(vb, axis=-1, keepdims=True)
        rstd = lax.rsqrt(jnp.mean(xc * xc, axis=-1, keepdims=True) + LN_EPS)
        nrm = xc * rstd
        lng_v = lng_ref[...]
        ln = nrm * lng_v + lnb_ref[...]
        sl = _sigmoid(ln)
        dln = dsb * (sl * (1.0 + ln * (1.0 - sl)))
        small_ref[0:1, :] += _colsum(dln * nrm)
        small_ref[1:2, :] += _colsum(dln)
        dn = dln * lng_v
        dvb = rstd * (dn - jnp.mean(dn, axis=-1, keepdims=True)
                      - nrm * jnp.mean(dn * nrm, axis=-1, keepdims=True))
        dvb_ref[...] = dvb
        small_ref[3:4, :] += _colsum(dvb)

    tok = lambda w, dt: jax.ShapeDtypeStruct((n_tok, w), dt)
    return pl.pallas_call(
        _after(body, deps), name="bwd_mix", grid=(n_steps,),
        in_specs=([_whole()] * len(deps) + [_rows(ts, dm)] * 3
                  + [_rows(ts, dm, 0), _rows(ts, dm, 5), _rows(ts, dm, 6)]
                  + [_rows(ts, dm)] * 2 + [_whole()] * 2 + [_hbm()]),
        out_specs=[_rows(ts, 3 * dm), _rows(ts, dm), _rows(ts, dm), _rows(ts, dm), _rows(ts, dm),
                   pl.BlockSpec((SUBLANE, dm), lambda i: (0, 0))],
        out_shape=[tok(3 * dm, BF16), tok(dm, BF16), tok(dm, BF16), tok(dm, F32), tok(dm, F32),
                   jax.ShapeDtypeStruct((SUBLANE, dm), F32)],
        scratch_shapes=[pltpu.VMEM((dm, dm), BF16), pltpu.VMEM((dm, dm), BF16), pltpu.VMEM((dm, dm), BF16),
                        pltpu.SemaphoreType.DMA((3 * N_DEV,))],
        compiler_params=_params(48),
    )(*deps, dmix, ya, yb, proj, proj, proj, va, vb, lng, lnb, slab)


def _bwd_conv(dva, dvb, p, u, proj, dpa, caw, cbw, ts, deps):
    n_tok, dm = dva.shape
    n_steps = n_tok // ts
    small_rows = 40

    def body(dva_ref, dva_prev, dva_next, dvb_ref, dvb_prev, dvb_next, p_ref, u_ref,
             cg_ref, ha_ref, a_ref, g_ref, dpa_ref, caw_ref, cbw_ref,
             dproj_ref, small_ref,
             dp_v, du_v, tap_a, tap_b, gwa_v, gwb_v):
        i = pl.program_id(0)

        @pl.when(i == 0)
        def _():
            _broadcast_taps(caw_ref, tap_a, CONV_A)
            _broadcast_taps(cbw_ref, tap_b, CONV_B)
            small_ref[...] = jnp.zeros_like(small_ref)
            gwa_v[...] = jnp.zeros_like(gwa_v)
            gwb_v[...] = jnp.zeros_like(gwb_v)

        def emit_dp(r0, lanes, acc):
            dp_v[pl.ds(r0, acc.shape[0]), lanes] = acc

        def emit_du(r0, lanes, acc):
            du_v[pl.ds(r0, acc.shape[0]), lanes] = acc

        _conv_bwd_tile(_with_halos(dva_ref, dva_prev, dva_next, i, n_steps), p_ref, tap_a, gwa_v, CONV_A, ts, dm,
                       emit_dp)
        _conv_bwd_tile(_with_halos(dvb_ref, dvb_prev, dvb_next, i, n_steps), u_ref, tap_b, gwb_v, CONV_B, ts, dm,
                       emit_du)

        dp = dp_v[...]
        dcg = dp * ha_ref[...].astype(F32)
        dha = dp * cg_ref[...].astype(F32)
        du = du_v[...]
        sg = _sigmoid(g_ref[...].astype(F32))
        da = du * sg
        dg = du * a_ref[...].astype(F32) * sg * (1.0 - sg)
        dproj_ref[:, 0:dm] = dpa_ref[:, 0:dm]
        dproj_ref[:, dm:2 * dm] = dcg.astype(BF16)
        dproj_ref[:, 2 * dm:3 * dm] = dha.astype(BF16)
        dproj_ref[:, 3 * dm:4 * dm] = da.astype(BF16)
        dproj_ref[:, 4 * dm:5 * dm] = dg.astype(BF16)
        dproj_ref[:, 5 * dm:7 * dm] = dpa_ref[:, dm:3 * dm]
        small_ref[3:4, :] += _colsum(dcg)
        small_ref[4:5, :] += _colsum(dha)
        small_ref[5:6, :] += _colsum(da)
        small_ref[6:7, :] += _colsum(dg)

        @pl.when(i == n_steps - 1)
        def _():
            for k in range(CONV_A):
                small_ref[k:k + 1, :] = _colsum(gwa_v[k])
            for k in range(CONV_B):
                small_ref[SUBLANE + k:SUBLANE + k + 1, :] = _colsum(gwb_v[k])

    return pl.pallas_call(
        _after(body, deps), name="bwd_conv", grid=(n_steps,),
        in_specs=([_whole()] * len(deps) + _halo_specs(ts, dm, n_tok) * 2 + [_rows(ts, dm)] * 2
                  + [_rows(ts, dm, 1), _rows(ts, dm, 2), _rows(ts, dm, 3), _rows(ts, dm, 4), _rows(ts, 3 * dm)]
                  + [_whole()] * 2),
        out_specs=[_rows(ts, 7 * dm), pl.BlockSpec((small_rows, dm), lambda i: (0, 0))],
        out_shape=[jax.ShapeDtypeStruct((n_tok, 7 * dm), BF16), jax.ShapeDtypeStruct((small_rows, dm), F32)],
        scratch_shapes=[pltpu.VMEM((ts, dm), F32), pltpu.VMEM((ts, dm), F32),
                        pltpu.VMEM((CONV_A, SUBLANE, dm), F32), pltpu.VMEM((CONV_B, SUBLANE, dm), F32),
                        pltpu.VMEM((CONV_A, SUBLANE, dm), F32), pltpu.VMEM((CONV_B, SUBLANE, dm), F32)],
        compiler_params=_params(48),
    )(*deps, dva, dva, dva, dvb, dvb, dvb, p, u, proj, proj, proj, proj, dpa, caw, cbw)


def _bwd_in(dproj, x, dx1, g1, slab, ts, deps):
    n_tok, dm = x.shape
    rows, off, _ = _layout(dm)
    width = 7 * dm

    def body(dproj_ref, x_ref, dx1_ref, g1_ref, slab_ref, gx_ref, small_ref, w_v, sems):
        copies = _weight_copies(slab_ref, off["win"], rows["win"], w_v, sems, 0)
        _on_first_step(copies, "start")

        @pl.when(pl.program_id(0) == 0)
        def _():
            small_ref[...] = jnp.zeros_like(small_ref)

        _on_first_step(copies, "wait")
        dh = _dot(dproj_ref[...], w_v[...])
        xv = x_ref[...]
        r1 = _rms(xv)
        small_ref[0:1, :] += _colsum(dh * xv * r1)
        gx_ref[...] = dx1_ref[...] + _rms_bwd(dh, xv, r1, g1_ref[...])

    return pl.pallas_call(
        _after(body, deps), name="bwd_in", grid=(n_tok // ts,),
        in_specs=[_whole()] * len(deps) + [_rows(ts, width), _rows(ts, dm), _rows(ts, dm), _whole(), _hbm()],
        out_specs=[_rows(ts, dm), pl.BlockSpec((SUBLANE, dm), lambda i: (0, 0))],
        out_shape=[jax.ShapeDtypeStruct((n_tok, dm), F32), jax.ShapeDtypeStruct((SUBLANE, dm), F32)],
        scratch_shapes=[pltpu.VMEM((width, dm), BF16), pltpu.SemaphoreType.DMA((N_DEV,))],
        compiler_params=_params(56),
    )(*deps, dproj, x, dx1, g1, slab)


def _wgrad(a, b, name, tm, tk, out_dtype):
    n_tok, m = a.shape
    n = b.shape[1]
    k_steps = n_tok // tk

    def body(a_ref, b_ref, o_ref, acc_v):
        k = pl.program_id(1)

        @pl.when(k == 0)
        def _():
            acc_v[...] = jnp.zeros_like(acc_v)

        acc_v[...] += _dot_tn(a_ref[...], b_ref[...])

        @pl.when(k == k_steps - 1)
        def _():
            o_ref[...] = acc_v[...].astype(o_ref.dtype)

    return pl.pallas_call(
        body, name=name, grid=(m // tm, k_steps),
        in_specs=[pl.BlockSpec((tk, tm), lambda i, k: (k, i)), pl.BlockSpec((tk, n), lambda i, k: (k, 0))],
        out_specs=pl.BlockSpec((tm, n), lambda i, k: (i, 0)),
        out_shape=jax.ShapeDtypeStruct((m, n), out_dtype),
        scratch_shapes=[pltpu.VMEM((tm, n), F32)],
        compiler_params=pltpu.CompilerParams(dimension_semantics=("arbitrary", "arbitrary"),
                                             vmem_limit_bytes=40 * MIB),
    )(a, b)


def _adamw(w, g, m, v):
    m = ADAM_B1 * m + (1.0 - ADAM_B1) * g
    v = ADAM_B2 * v + (1.0 - ADAM_B2) * (g * g)
    m_hat = m / (1.0 - ADAM_B1 ** ADAM_STEP)
    v_hat = v / (1.0 - ADAM_B2 ** ADAM_STEP)
    delta = -ADAM_LR * (m_hat / (jnp.sqrt(v_hat) + ADAM_EPS) + ADAM_WD * w)
    return delta, m, v


def _adam_big(recv, part, me, off, rows, w, m, v, transpose, name, tr):
    dm = recv.shape[2]
    per = rows // tr

    def body(me_ref, own_ref, r_ref, w_ref, m_ref, v_ref, g_ref, d_ref, mo_ref, vo_ref):
        g = own_ref[...].astype(F32)
        for k in range(len(FLIPS)):
            g = g + r_ref[k].astype(F32)
        if transpose:
            g = g.T
        delta, m_new, v_new = _adamw(w_ref[...], g, m_ref[...], v_ref[...])
        g_ref[...] = g
        d_ref[...] = delta
        mo_ref[...] = m_new
        vo_ref[...] = v_new

    if transpose:
        blk = pl.BlockSpec((dm, tr), lambda i, me_ref: (0, i))
    else:
        blk = pl.BlockSpec((tr, dm), lambda i, me_ref: (i, 0))
    first = off // tr
    return pl.pallas_call(
        body, name=name,
        grid_spec=pltpu.PrefetchScalarGridSpec(
            num_scalar_prefetch=1, grid=(per,),
            in_specs=[pl.BlockSpec((tr, dm), lambda i, me_ref: (me_ref[0] * per + i, 0)),
                      pl.BlockSpec((len(FLIPS), tr, dm), lambda i, me_ref: (0, first + i, 0)), blk, blk, blk],
            out_specs=[blk] * 4),
        out_shape=[jax.ShapeDtypeStruct(w.shape, F32)] * 4,
        compiler_params=_params(32),
    )(me, part, recv, w, m, v)


LOSS_ROW = 15
CONV_A_ROW = 16
CONV_B_ROW = 24


def _adam_small(recv_small, recv_last, me, params, d_model):
    n = len(params)
    cw = d_model // N_DEV

    def body(me_ref, r_ref, rc_ref, l_ref, *refs):
        ins, loss_ref, outs = refs[:3 * n], refs[3 * n], refs[3 * n + 1:3 * n + 1 + 4 * n]
        g_v, gc_v, last_v = refs[3 * n + 1 + 4 * n:]
        g, gc, last = r_ref[0], rc_ref[0], l_ref[0]
        for d in range(1, N_DEV):
            g, gc, last = g + r_ref[d], gc + rc_ref[d], last + l_ref[d]
        g_v[...], gc_v[...], last_v[...] = g, gc, last
        loss_ref[...] = (0.5 / d_model) * jnp.sum(g_v[LOSS_ROW:LOSS_ROW + 1, :], axis=-1, keepdims=True)
        for j, (row0, own_columns, (w, _, _)) in enumerate(params):
            w_ref, m_ref, v_ref = ins[3 * j:3 * j + 3]
            source = gc_v if own_columns else (last_v if row0 == 0 else g_v)
            grad = source[row0:row0 + w.shape[0], :]
            delta, m_new, v_new = _adamw(w_ref[...], grad, m_ref[...], v_ref[...])
            for ref, val in zip(outs[4 * j:4 * j + 4], (grad, delta, m_new, v_new)):
                ref[...] = val

    full = lambda shape: pl.BlockSpec(shape, lambda i, me_ref: (0,) * len(shape))
    stack_rows = recv_small.shape[1]
    flat = [a for _, _, triple in params for a in triple]
    shapes = [w.shape for _, _, (w, _, _) in params for _ in range(4)]
    out = pl.pallas_call(
        body, name="adam_small",
        grid_spec=pltpu.PrefetchScalarGridSpec(
            num_scalar_prefetch=1, grid=(1,),
            in_specs=[full(recv_small.shape),
                      pl.BlockSpec((N_DEV, stack_rows, cw), lambda i, me_ref: (0, 0, me_ref[0])),
                      full(recv_last.shape)] + [full(a.shape) for a in flat],
            out_specs=[full((1, 1))] + [full(s) for s in shapes],
            scratch_shapes=[pltpu.VMEM((stack_rows, d_model), F32), pltpu.VMEM((stack_rows, cw), F32),
                            pltpu.VMEM(recv_last.shape[1:], F32)]),
        out_shape=[jax.ShapeDtypeStruct((1, 1), F32)] + [jax.ShapeDtypeStruct(s, F32) for s in shapes],
    )(me, recv_small, recv_small, recv_last, *flat)
    return out[0], [tuple(out[1 + 4 * j:5 + 4 * j]) for j in range(n)]


def _tile(n_tok, want):
    return min(want, n_tok)


def kernel(x, norm1_pre_g, w_in, b_in, conv_a_w, conv_a_b, w_a_out, conv_b_w, conv_b_b, ln_b_g, ln_b_b, w_b_out, w_o, norm1_post_g, norm2_pre_g, w_mlp_in, w_mlp_out, norm2_post_g, loss_target, m_norm1_pre_g, m_w_in, m_b_in, m_conv_a_w, m_conv_a_b, m_w_a_out, m_conv_b_w, m_conv_b_b, m_ln_b_g, m_ln_b_b, m_w_b_out, m_w_o, m_norm1_post_g, m_norm2_pre_g, m_w_mlp_in, m_w_mlp_out, m_norm2_post_g, v_norm1_pre_g, v_w_in, v_b_in, v_conv_a_w, v_conv_a_b, v_w_a_out, v_conv_b_w, v_conv_b_b, v_ln_b_g, v_ln_b_b, v_w_b_out, v_w_o, v_norm1_post_g, v_norm2_pre_g, v_w_mlp_in, v_w_mlp_out, v_norm2_post_g):
    n_tok, dm = x.shape[1], x.shape[2]
    rows, off, slab_rows = _layout(dm)
    cw = dm // N_DEV
    xs = x.reshape(n_tok, dm)
    tgt = loss_target.reshape(n_tok, dm)
    row = lambda vec: vec.reshape(1, -1)
    scattered = lambda group: jax.ShapeDtypeStruct((len(FLIPS), slab_rows[group], dm), BF16)
    tm, tk = min(dm, 1024), _tile(n_tok, 2048)
    me = (4 * lax.axis_index("x") + 2 * lax.axis_index("y") + lax.axis_index("c")).astype(jnp.int32).reshape(1)

    conv_own = jnp.concatenate([conv_a_w, jnp.zeros((SUBLANE - CONV_A, cw), F32), conv_b_w,
                                jnp.zeros((1, cw), F32)], axis=0)
    own_in, land_in = _place_cast([(w_in, True)], me, "place_w_in")
    slab_in, conv_all = _all_gather_two_level([own_in, conv_own], [land_in, None], "gather_w_in")
    conv_full = conv_all.transpose(1, 0, 2).reshape(conv_own.shape[0], dm)
    caw, cbw = conv_full[0:CONV_A], conv_full[SUBLANE:SUBLANE + CONV_B]
    own_abo, land_abo = _place_cast([(w_a_out, False), (w_b_out, False), (w_o, False)], me, "place_abo")
    own_mlp, land_mlp = _place_cast([(w_mlp_in, True), (w_mlp_out, False)], me, "place_mlp")
    ag_abo = _exchange_start([_Part(own_abo, False, slab_rows["abo"], 0, 0)], [land_abo],
                             "gather_abo_start", after=slab_in)
    ag_mlp = _exchange_start([_Part(own_mlp, False, slab_rows["mlp"], 0, 0)], [land_mlp],
                             "gather_mlp_start", after=ag_abo.token)

    proj, p, u, h = _fwd_in(xs, row(norm1_pre_g), row(b_in), slab_in, _tile(n_tok, 512),
                            [ag_abo.token, ag_mlp.token])
    _, (slab_abo,) = _exchange_wait(ag_abo, "gather_abo_wait", after=proj)
    va, vb, ya, yb, qa, sb, merged, mix, x1 = _fwd_mix(
        p, u, proj, xs, caw, row(conv_a_b), cbw, row(conv_b_b), row(ln_b_g), row(ln_b_b), row(norm1_post_g),
        slab_abo, _tile(n_tok, 256))
    _, (slab_mlp,) = _exchange_wait(ag_mlp, "gather_mlp_wait", after=x1)
    f, df1, h2, df2, dmix, dx1, small_mlp = _mlp_fwd_bwd(
        x1, mix, tgt, row(norm1_post_g), row(norm2_pre_g), row(norm2_post_g), slab_mlp, _tile(n_tok, 256))

    rs_mlp = _exchange_start(
        [_Part(_wgrad(df1, h2, "wgrad_mlp_in", tm, tk, BF16), True, rows["w1"], 0, off["w1"]),
         _Part(_wgrad(f, df2, "wgrad_mlp_out", tm, tk, BF16), True, rows["w2"], 0, off["w2"])],
        [scattered("mlp")], "scatter_mlp_start")
    dpa, dya, dyb, dva, dvb, small_mix = _bwd_mix(
        dmix, ya, yb, proj, va, vb, row(ln_b_g), row(ln_b_b), slab_abo, _tile(n_tok, 256), [rs_mlp.token])
    rs_abo = _exchange_start(
        [_Part(_wgrad(qa, dya, "wgrad_a_out", tm, tk, BF16), True, rows["wa"], 0, off["wa"]),
         _Part(_wgrad(sb, dyb, "wgrad_b_out", tm, tk, BF16), True, rows["wb"], 0, off["wb"]),
         _Part(_wgrad(merged, dmix, "wgrad_o", tm, tk, BF16), True, rows["wo"], 0, off["wo"])],
        [scattered("abo")], "scatter_abo_start")
    dproj, small_conv = _bwd_conv(dva, dvb, p, u, proj, dpa, caw, cbw, _tile(n_tok, 256), [rs_abo.token])

    zeros = lambda r: jnp.zeros((r, dm), F32)
    small = jnp.concatenate([
        zeros(1),
        small_mix[2:3],
        small_mix[3:4],
        small_mix[0:2],
        small_mlp[2:3],
        small_mlp[1:2],
        small_mlp[0:1],
        small_mix[4:5], small_conv[3:7], small_mix[5:7],
        small_mlp[3:4],
        small_conv[0:CONV_A], zeros(SUBLANE - CONV_A),
        small_conv[8:8 + CONV_B], zeros(1),
    ], axis=0)

    rs_in = _exchange_start(
        [_Part(_wgrad(dproj, h, "wgrad_in", tm, tk, BF16), True, rows["win"], 0, off["win"]),
         _Part(small, False, small.shape[0], 1, 0)],
        [scattered("in"), _place_own(small, (N_DEV,) + small.shape, me, "place_small")], "scatter_in_start")
    grad_x, small_in = _bwd_in(dproj, xs, dx1, row(norm1_pre_g), slab_in, _tile(n_tok, 256), [rs_in.token])

    tr = min(LANE, rows["wa"])
    (g_w1, g_w2), (recv_mlp,) = _exchange_wait(rs_mlp, "scatter_mlp_wait", after=grad_x)
    (g_wa, g_wb, g_wo), (recv_abo,) = _exchange_wait(rs_abo, "scatter_abo_wait", after=grad_x)
    big = {
        "w_mlp_in": _adam_big(recv_mlp, g_w1, me, off["w1"], rows["w1"], w_mlp_in, m_w_mlp_in, v_w_mlp_in, True,
                              "adam_w_mlp_in", tr),
        "w_mlp_out": _adam_big(recv_mlp, g_w2, me, off["w2"], rows["w2"], w_mlp_out, m_w_mlp_out, v_w_mlp_out,
                               False, "adam_w_mlp_out", tr),
        "w_a_out": _adam_big(recv_abo, g_wa, me, off["wa"], rows["wa"], w_a_out, m_w_a_out, v_w_a_out, False,
                             "adam_w_a_out", tr),
        "w_b_out": _adam_big(recv_abo, g_wb, me, off["wb"], rows["wb"], w_b_out, m_w_b_out, v_w_b_out, False,
                             "adam_w_b_out", tr),
        "w_o": _adam_big(recv_abo, g_wo, me, off["wo"], rows["wo"], w_o, m_w_o, v_w_o, False, "adam_w_o", tr),
    }
    (g_win, _), (recv_in, recv_small) = _exchange_wait(rs_in, "scatter_in_wait", after=big["w_o"][3])
    recv_last, = _all_gather([small_in], "gather_last")
    big["w_in"] = _adam_big(recv_in, g_win, me, off["win"], rows["win"], w_in, m_w_in, v_w_in, True, "adam_w_in", tr)

    small_names = ("norm1_pre_g", "conv_a_b", "conv_b_b", "ln_b_g", "ln_b_b", "norm1_post_g", "norm2_pre_g",
                   "norm2_post_g")
    given = dict(
        norm1_pre_g=(norm1_pre_g, m_norm1_pre_g, v_norm1_pre_g), conv_a_b=(conv_a_b, m_conv_a_b, v_conv_a_b),
        conv_b_b=(conv_b_b, m_conv_b_b, v_conv_b_b), ln_b_g=(ln_b_g, m_ln_b_g, v_ln_b_g),
        ln_b_b=(ln_b_b, m_ln_b_b, v_ln_b_b), norm1_post_g=(norm1_post_g, m_norm1_post_g, v_norm1_post_g),
        norm2_pre_g=(norm2_pre_g, m_norm2_pre_g, v_norm2_pre_g),
        norm2_post_g=(norm2_post_g, m_norm2_post_g, v_norm2_post_g))
    params = [(j, False, tuple(row(a) for a in given[name])) for j, name in enumerate(small_names)]
    params.append((SUBLANE, False, tuple(a.reshape(7, dm) for a in (b_in, m_b_in, v_b_in))))
    params.append((CONV_A_ROW, True, (conv_a_w, m_conv_a_w, v_conv_a_w)))
    params.append((CONV_B_ROW, True, (conv_b_w, m_conv_b_w, v_conv_b_w)))
    loss, small_out = _adam_small(recv_small, recv_last, me, params, dm)
    small_leaves = {name: tuple(a.reshape(dm) for a in small_out[j]) for j, name in enumerate(small_names)}
    small_leaves["b_in"] = tuple(a.reshape(7 * dm) for a in small_out[len(small_names)])
    small_leaves["conv_a_w"] = small_out[len(small_names) + 1]
    small_leaves["conv_b_w"] = small_out[len(small_names) + 2]

    order = ("norm1_pre_g", "w_in", "b_in", "conv_a_w", "conv_a_b", "w_a_out", "conv_b_w", "conv_b_b", "ln_b_g",
             "ln_b_b", "w_b_out", "w_o", "norm1_post_g", "norm2_pre_g", "w_mlp_in", "w_mlp_out", "norm2_post_g")
    leaves = [big[name] if name in big else small_leaves[name] for name in order]
    grads, deltas, new_m, new_v = zip(*leaves)
    return (loss.reshape(()), grad_x.reshape(x.shape), *grads, *deltas, *new_m, *new_v)
```

```python
from typing import NamedTuple

import jax
import jax.numpy as jnp
from jax import lax
from jax.experimental import pallas as pl
from jax.experimental.pallas import tpu as pltpu

F32 = jnp.float32
BF16 = jnp.bfloat16

RMS_EPS = 1e-6
LN_EPS = 1e-5
ADAM_LR = 0.001
ADAM_B1 = 0.9
ADAM_B2 = 0.999
ADAM_EPS = 1e-08
ADAM_WD = 0.01
ADAM_STEP = 10

N_DEV = 8
CONV_A = 3
CONV_B = 31
LANE = 128
SUBLANE = 8
HALO = 16
FWD_CONV_ROWS = 32
BWD_CONV_ROWS = 64
MIB = 1 << 20
FLIPS = ((0, 0, 1), (0, 1, 0), (1, 0, 0), (0, 1, 1), (1, 0, 1), (1, 1, 0), (1, 1, 1))
MESH = pl.DeviceIdType.MESH


def _layout(d_model):
    e = d_model // N_DEV
    rows = {"win": 7 * e, "w1": 4 * e, "w2": 4 * e, "wa": e, "wb": e, "wo": e}
    off = {"win": 0, "w1": 0, "w2": 4 * e, "wa": 0, "wb": e, "wo": 2 * e}
    return rows, off, {"in": 7 * e, "mlp": 8 * e, "abo": 3 * e}


def _after(body, deps):
    def ordered(*refs):
        return body(*refs[len(deps):])
    return ordered


def _params(vmem_mib):
    return pltpu.CompilerParams(dimension_semantics=("arbitrary",), vmem_limit_bytes=vmem_mib * MIB)


def _whole():
    return pl.BlockSpec(memory_space=pltpu.VMEM)


def _hbm():
    return pl.BlockSpec(memory_space=pl.ANY)


def _in_hbm(a):
    return pltpu.with_memory_space_constraint(a, pltpu.HBM)


def _rows(ts, width, col=0):
    return pl.BlockSpec((ts, width), lambda i: (i, col))


def _halo_specs(ts, width, n_rows):
    per = ts // HALO
    last = n_rows // HALO - 1
    return [
        pl.BlockSpec((ts, width), lambda i: (i, 0)),
        pl.BlockSpec((HALO, width), lambda i: (jnp.maximum(i * per - 1, 0), 0)),
        pl.BlockSpec((HALO, width), lambda i: (jnp.minimum((i + 1) * per, last), 0)),
    ]


def _dot(a, b):
    return jnp.dot(a, b, preferred_element_type=F32)


def _dot_nt(a, b):
    return lax.dot_general(a, b, (((1,), (1,)), ((), ())), preferred_element_type=F32)


def _dot_tn(a, b):
    return lax.dot_general(a, b, (((0,), (0,)), ((), ())), preferred_element_type=F32)


def _rms(u):
    return lax.rsqrt(jnp.mean(u * u, axis=-1, keepdims=True) + RMS_EPS)


def _rms_bwd(dz, u, r, g):
    dzg = dz * g
    return r * dzg - u * (r * r * r) * jnp.mean(dzg * u, axis=-1, keepdims=True)


def _colsum(v):
    return jnp.sum(v, axis=0, keepdims=True)


def _sigmoid(v):
    return jax.nn.sigmoid(v)


def _weight_copies(slab_ref, off, rows, dst_ref, sems, first_sem):
    return [pltpu.make_async_copy(slab_ref.at[d, pl.ds(off, rows), :], dst_ref.at[pl.ds(d * rows, rows), :],
                                  sems.at[first_sem + d]) for d in range(N_DEV)]


def _on_first_step(copies, method):
    @pl.when(pl.program_id(0) == 0)
    def _():
        for cp in copies:
            getattr(cp, method)()


def _with_halos(main_ref, prev_ref, next_ref, i, n_steps):
    return (main_ref, jnp.where(i > 0, prev_ref[...], 0.0), jnp.where(i < n_steps - 1, next_ref[...], 0.0))


def _broadcast_taps(w_ref, wb_ref, n_taps):
    for k in range(n_taps):
        wb_ref[k] = jnp.broadcast_to(w_ref[k:k + 1, :], wb_ref.shape[1:])


def _conv_tile(tile, wb_ref, starts, ts, width, emit, rolled_ref, rows=FWD_CONV_ROWS):
    main_ref, prev, nxt = tile
    span = ts + 2 * HALO
    nv = rows // SUBLANE
    for cb in range(width // LANE):
        lanes = slice(cb * LANE, (cb + 1) * LANE)
        slot = cb % 2
        window = jnp.concatenate([prev[:, lanes], main_ref[:, lanes], nxt[:, lanes]], axis=0)
        for b in sorted({st % SUBLANE for st in starts}):
            rolled_ref[slot, b] = window if b == 0 else pltpu.roll(window, span - b, axis=0)
        for r0 in range(0, ts, rows):
            acc = jnp.zeros((nv, SUBLANE, LANE), F32)
            for k, st in enumerate(starts):
                shifted = rolled_ref[slot, st % SUBLANE, pl.ds(r0 + st - st % SUBLANE, rows), :]
                acc = acc + shifted.reshape(nv, SUBLANE, LANE) * wb_ref[k, :, lanes][None]
            emit(r0, pl.ds(cb * LANE, LANE), acc.reshape(rows, LANE))


def _window(tile, r0, cb, ts, rows):
    main_ref, prev, nxt = tile
    lanes = slice(cb * LANE, (cb + 1) * LANE)
    lo, hi = max(r0 - HALO, 0), min(r0 + rows + HALO, ts)
    pieces = [prev[:, lanes]] if r0 - HALO < 0 else []
    pieces.append(main_ref[lo:hi, lanes])
    if r0 + rows + HALO > ts:
        pieces.append(nxt[:, lanes])
    return pieces[0] if len(pieces) == 1 else jnp.concatenate(pieces, axis=0)


def _phases(starts):
    groups = {}
    for k, st in enumerate(starts):
        groups.setdefault(st % SUBLANE, []).append((k, st // SUBLANE))
    return sorted(groups.items())


def _shifted(blk, b):
    n = blk.shape[0]
    rolled = blk if b == 0 else pltpu.roll(blk, n - b, axis=0)
    return rolled.reshape(n // SUBLANE, SUBLANE, blk.shape[1])


def _conv_bwd_tile(dv_tile, u_ref, wb_ref, acc_ref, n_taps, ts, width, emit, rows=BWD_CONV_ROWS):
    groups = _phases(_bwd_starts(n_taps))
    nv = rows // SUBLANE
    for r0 in range(0, ts, rows):
        for cb in range(width // LANE):
            lanes = pl.ds(cb * LANE, LANE)
            blk = _window(dv_tile, r0, cb, ts, rows)
            u = u_ref[pl.ds(r0, rows), lanes].reshape(nv, SUBLANE, LANE)
            du = jnp.zeros((nv, SUBLANE, LANE), F32)
            for b, taps in groups:
                sh = _shifted(blk, b)
                for k, m in taps:
                    du = du + sh[m:m + nv] * wb_ref[k, :, lanes][None]
                    acc_ref[k, :, lanes] += jnp.sum(sh[m:m + nv] * u, axis=0)
            emit(r0, lanes, du.reshape(rows, LANE))


def _fwd_starts(n_taps):
    pad = (n_taps - 1) // 2
    return [HALO - pad + k for k in range(n_taps)]


def _bwd_starts(n_taps):
    pad = (n_taps - 1) // 2
    return [HALO + pad - k for k in range(n_taps)]


def _peer(x, y, c, flip):
    fx, fy, fc = flip
    return (1 - x if fx else x, 1 - y if fy else y, 1 - c if fc else c)


def _all_gather(shards, name):
    n = len(shards)

    def body(*refs):
        ins, outs = refs[:n], refs[n:2 * n]
        send_sems, recv_sems, local_sems = refs[2 * n:]
        x, y, c = lax.axis_index("x"), lax.axis_index("y"), lax.axis_index("c")
        me = 4 * x + 2 * y + c
        local = [pltpu.make_async_copy(ins[j], outs[j].at[me], local_sems.at[j]) for j in range(n)]
        for cp in local:
            cp.start()
        sends, recvs = [], []
        for k, flip in enumerate(FLIPS):
            px, py, pc = _peer(x, y, c, flip)
            peer = 4 * px + 2 * py + pc
            for j in range(n):
                sem = k * n + j
                sends.append(pltpu.make_async_remote_copy(
                    src_ref=ins[j], dst_ref=outs[j].at[me], send_sem=send_sems.at[sem], recv_sem=recv_sems.at[sem],
                    device_id=(px, py, pc), device_id_type=MESH))
                recvs.append(pltpu.make_async_remote_copy(
                    src_ref=ins[j], dst_ref=outs[j].at[peer], send_sem=send_sems.at[sem], recv_sem=recv_sems.at[sem],
                    device_id=(px, py, pc), device_id_type=MESH))
        for cp in sends:
            cp.start()
        for cp in recvs:
            cp.wait_recv()
        for cp in sends:
            cp.wait_send()
        for cp in local:
            cp.wait()

    return pl.pallas_call(
        body, name=name,
        out_shape=[jax.ShapeDtypeStruct((N_DEV,) + s.shape, s.dtype) for s in shards],
        in_specs=[_hbm()] * n, out_specs=[_hbm()] * n,
        scratch_shapes=[pltpu.SemaphoreType.DMA((7 * n,)), pltpu.SemaphoreType.DMA((7 * n,)),
                        pltpu.SemaphoreType.DMA((n,))],
    )(*shards)


def _place_own(src, n_slots_shape, me, name):
    rows, width = src.shape
    tr = next(t for t in (256, 128, 64, 32, 16, SUBLANE) if rows % t == 0)

    def body(me_ref, src_ref, out_ref):
        out_ref[...] = src_ref[...]

    return pl.pallas_call(
        body, name=name,
        grid_spec=pltpu.PrefetchScalarGridSpec(
            num_scalar_prefetch=1, grid=(rows // tr,),
            in_specs=[pl.BlockSpec((tr, width), lambda i, me_ref: (i, 0))],
            out_specs=pl.BlockSpec((None, tr, width), lambda i, me_ref: (me_ref[0], i, 0))),
        out_shape=pltpu.HBM(n_slots_shape, src.dtype),
    )(me, src)


def _place_cast(pieces, me, name):
    n = len(pieces)
    counts = [a.shape[1] if t else a.shape[0] for a, t in pieces]
    width = pieces[0][0].shape[0] if pieces[0][1] else pieces[0][0].shape[1]
    total = sum(counts)

    def body(me_ref, *refs):
        ins, own_ref, land_ref = refs[:n], refs[n], refs[n + 1]
        first = 0
        for (a, transpose), in_ref, count in zip(pieces, ins, counts):
            block = (in_ref[...].T if transpose else in_ref[...]).astype(BF16)
            own_ref[first:first + count, :] = block
            land_ref[first:first + count, :] = block
            first += count

    return pl.pallas_call(
        body, name=name,
        grid_spec=pltpu.PrefetchScalarGridSpec(
            num_scalar_prefetch=1, grid=(1,),
            in_specs=[pl.BlockSpec(a.shape, lambda i, me_ref: (0, 0)) for a, _ in pieces],
            out_specs=[pl.BlockSpec((total, width), lambda i, me_ref: (0, 0)),
                       pl.BlockSpec((None, total, width), lambda i, me_ref: (me_ref[0], 0, 0))]),
        out_shape=[pltpu.HBM((total, width), BF16), pltpu.HBM((N_DEV, total, width), BF16)],
        compiler_params=_params(32),
    )(me, *[_in_hbm(a) for a, _ in pieces])


def _all_gather_two_level(shards, placed, name):
    n = len(shards)
    given = [j for j in range(n) if placed[j] is not None]

    def body(*refs):
        ins, outs = refs[:n], refs[n + len(given):2 * n + len(given)]
        send_sems, recv_sems, local_sems = refs[2 * n + len(given):]
        x, y, c = lax.axis_index("x"), lax.axis_index("y"), lax.axis_index("c")
        me, sibling = (x, y, c), (x, y, 1 - c)
        chips = [(1 - x, y), (x, 1 - y), (1 - x, 1 - y)]

        def slot(j, dev):
            return outs[j].at[4 * dev[0] + 2 * dev[1] + dev[2]]

        def copy(k, j, block, to, src=None):
            return pltpu.make_async_remote_copy(
                src_ref=slot(j, block) if src is None else src, dst_ref=slot(j, block),
                send_sem=send_sems.at[k * n + j], recv_sem=recv_sems.at[k * n + j], device_id=to, device_id_type=MESH)

        local = [pltpu.make_async_copy(ins[j], slot(j, me), local_sems.at[j]) for j in range(n) if j not in given]
        for cp in local:
            cp.start()
        first = [copy(0, j, me, sibling, src=ins[j]) for j in range(n)]
        first += [copy(1 + t, j, me, (*chip, c), src=ins[j]) for t, chip in enumerate(chips) for j in range(n)]
        for cp in first:
            cp.start()
        passed = []
        for t, chip in enumerate(chips):
            for j in range(n):
                copy(1 + t, j, (*chip, c), me).wait_recv()
                passed.append(copy(4 + t, j, (*chip, c), sibling))
                passed[-1].start()
        for j in range(n):
            copy(0, j, sibling, me).wait_recv()
        for t, chip in enumerate(chips):
            for j in range(n):
                copy(4 + t, j, (*chip, 1 - c), me).wait_recv()
        for cp in first + passed:
            cp.wait_send()
        for cp in local:
            cp.wait()

    return pl.pallas_call(
        body, name=name,
        out_shape=[jax.ShapeDtypeStruct((N_DEV,) + s.shape, s.dtype) for s in shards],
        in_specs=[_hbm()] * (n + len(given)), out_specs=[_hbm()] * n,
        input_output_aliases={n + i: j for i, j in enumerate(given)},
        scratch_shapes=[pltpu.SemaphoreType.DMA((7 * n,)), pltpu.SemaphoreType.DMA((7 * n,)),
                        pltpu.SemaphoreType.DMA((n,))],
    )(*shards, *[placed[j] for j in given])


class _Part(NamedTuple):
    src: jax.Array
    scatter: bool
    rows: int
    land: int
    off: int


class _Started(NamedTuple):
    send_sems: jax.Array
    recv_sems: jax.Array
    thru: tuple
    token: jax.Array
    parts: tuple


def _exchange_copies(srcs, lands, send_sems, recv_sems, parts):
    n = len(parts)
    x, y, c = lax.axis_index("x"), lax.axis_index("y"), lax.axis_index("c")
    me = 4 * x + 2 * y + c

    def block(j, dev):
        p = parts[j]
        return srcs[j].at[pl.ds(pl.multiple_of(dev * p.rows, SUBLANE), p.rows), :] if p.scatter else srcs[j]

    def slot(j, index):
        p = parts[j]
        return lands[p.land].at[index, pl.ds(p.off, p.rows), :]

    sends, recvs = [], []
    for k, flip in enumerate(FLIPS):
        px, py, pc = _peer(x, y, c, flip)
        peer = 4 * px + 2 * py + pc
        for j in range(n):
            sems = dict(send_sem=send_sems.at[k * n + j], recv_sem=recv_sems.at[k * n + j],
                        device_id=(px, py, pc), device_id_type=MESH)
            to, got = (k, k) if parts[j].scatter else (me, peer)
            sends.append(pltpu.make_async_remote_copy(src_ref=block(j, peer), dst_ref=slot(j, to), **sems))
            recvs.append(pltpu.make_async_remote_copy(src_ref=block(j, peer), dst_ref=slot(j, got), **sems))
    return sends, recvs


def _exchange_start(parts, lands, name, after=None):
    n, nl = len(parts), len(lands)
    n_in = n + nl + (after is not None)

    def body(*refs):
        srcs, land_refs = refs[:n], refs[n:n + nl]
        send_sems, recv_sems = refs[n_in], refs[n_in + 1]
        token = refs[n_in + 2 + n + nl]
        sends, _ = _exchange_copies(srcs, land_refs, send_sems, recv_sems, parts)
        for cp in sends:
            cp.start()
        token[...] = jnp.zeros_like(token)

    hbm = pl.BlockSpec(memory_space=pltpu.HBM)
    sem = pl.BlockSpec(memory_space=pltpu.SEMAPHORE)
    fresh = lambda s: lax.empty(s.shape, s.dtype) if isinstance(s, jax.ShapeDtypeStruct) else s
    args = [pltpu.with_memory_space_constraint(p.src, pltpu.HBM) for p in parts]
    args += [pltpu.with_memory_space_constraint(fresh(s), pltpu.HBM) for s in lands]
    args += [] if after is None else [after]
    out = pl.pallas_call(
        body, name=name,
        out_shape=(pltpu.SemaphoreType.DMA((7 * n,)), pltpu.SemaphoreType.DMA((7 * n,)),
                   *[pltpu.HBM(a.shape, a.dtype) for a in args[:n + nl]], jax.ShapeDtypeStruct((SUBLANE, LANE), F32)),
        in_specs=[hbm] * (n + nl) + [_hbm()] * (after is not None),
        out_specs=(sem, sem, *[hbm] * (n + nl), _whole()),
        input_output_aliases={j: 2 + j for j in range(n + nl)},
        compiler_params=pltpu.CompilerParams(has_side_effects=pltpu.SideEffectType.DATAFLOW_SIDE_EFFECTING),
    )(*args)
    return _Started(out[0], out[1], tuple(out[2:2 + n + nl]), out[2 + n + nl], tuple(parts))


def _exchange_wait(started, name, after):
    parts = started.parts
    n, nl = len(parts), len(started.thru) - len(parts)

    def body(*refs):
        srcs, land_refs = refs[:n], refs[n:n + nl]
        send_sems, recv_sems = refs[n + nl], refs[n + nl + 1]
        sends, recvs = _exchange_copies(srcs, land_refs, send_sems, recv_sems, parts)
        for cp in sends:
            cp.wait_send()
        for cp in recvs:
            cp.wait_recv()

    hbm = pl.BlockSpec(memory_space=pltpu.HBM)
    sem = pl.BlockSpec(memory_space=pltpu.SEMAPHORE)
    out = pl.pallas_call(
        body, name=name,
        out_shape=tuple(pltpu.HBM(a.shape, a.dtype) for a in started.thru),
        in_specs=[hbm] * (n + nl) + [sem, sem, _hbm()], out_specs=tuple([hbm] * (n + nl)),
        input_output_aliases={j: j for j in range(n + nl)},
        compiler_params=pltpu.CompilerParams(has_side_effects=pltpu.SideEffectType.DATAFLOW_SIDE_EFFECTING),
    )(*started.thru, started.send_sems, started.recv_sems, after)
    return list(out[:n]), list(out[n:])


def _fwd_in(x, g1, b_in, slab, ts, deps):
    n_tok, dm = x.shape
    rows, off, _ = _layout(dm)
    width = 7 * dm

    def body(x_ref, g1_ref, b_ref, slab_ref, proj_ref, p_ref, u_ref, h_ref, w_v, sems):
        copies = _weight_copies(slab_ref, off["win"], rows["win"], w_v, sems, 0)
        _on_first_step(copies, "start")
        _on_first_step(copies, "wait")
        xv = x_ref[...]
        h = (xv * _rms(xv) * g1_ref[...]).astype(BF16)
        h_ref[...] = h
        cols = []
        for j in range(7):
            pj = _dot_nt(h, w_v[pl.ds(j * dm, dm), :]) + b_ref[:, j * dm:(j + 1) * dm]
            proj_ref[:, j * dm:(j + 1) * dm] = pj.astype(proj_ref.dtype)
            if 1 <= j <= 4:
                cols.append(pj)
            if j == 2:
                p_ref[...] = cols[0] * cols[1]
            if j == 4:
                u_ref[...] = cols[2] * _sigmoid(cols[3])

    return pl.pallas_call(
        _after(body, deps), name="fwd_in", grid=(n_tok // ts,),
        in_specs=[_whole()] * len(deps) + [_rows(ts, dm), _whole(), _whole(), _hbm()],
        out_specs=[_rows(ts, width), _rows(ts, dm), _rows(ts, dm), _rows(ts, dm)],
        out_shape=[jax.ShapeDtypeStruct((n_tok, width), BF16), jax.ShapeDtypeStruct((n_tok, dm), F32),
                   jax.ShapeDtypeStruct((n_tok, dm), F32), jax.ShapeDtypeStruct((n_tok, dm), BF16)],
        scratch_shapes=[pltpu.VMEM((width, dm), BF16), pltpu.SemaphoreType.DMA((N_DEV,))],
        compiler_params=_params(56),
    )(*deps, x, g1, b_in, slab)


def _fwd_mix(p, u, proj, x, caw, cab, cbw, cbb, lng, lnb, g1post, slab, ts):
    n_tok, dm = x.shape
    rows, off, _ = _layout(dm)
    n_steps = n_tok // ts

    def body(p_ref, p_prev, p_next, u_ref, u_prev, u_next, bg_ref, za_ref, zb_ref, x_ref,
             caw_ref, cab_ref, cbw_ref, cbb_ref, lng_ref, lnb_ref, g1p_ref, slab_ref,
             va_ref, vb_ref, ya_ref, yb_ref, qa_ref, sb_ref, mg_ref, mix_ref, x1_ref,
             wa_v, wb_v, wo_v, tap_a, tap_b, sems, rolled_a, rolled_b):
        i = pl.program_id(0)
        copies = (_weight_copies(slab_ref, off["wa"], rows["wa"], wa_v, sems, 0)
                  + _weight_copies(slab_ref, off["wb"], rows["wb"], wb_v, sems, N_DEV)
                  + _weight_copies(slab_ref, off["wo"], rows["wo"], wo_v, sems, 2 * N_DEV))
        _on_first_step(copies, "start")

        @pl.when(i == 0)
        def _():
            _broadcast_taps(caw_ref, tap_a, CONV_A)
            _broadcast_taps(cbw_ref, tap_b, CONV_B)

        def emit_a(r0, lanes, acc):
            va_ref[pl.ds(r0, acc.shape[0]), lanes] = acc + cab_ref[:, lanes]

        def emit_b(r0, lanes, acc):
            vb_ref[pl.ds(r0, acc.shape[0]), lanes] = acc + cbb_ref[:, lanes]

        _conv_tile(_with_halos(p_ref, p_prev, p_next, i, n_steps), tap_a, _fwd_starts(CONV_A), ts, dm, emit_a,
                   rolled_a)
        _conv_tile(_with_halos(u_ref, u_prev, u_next, i, n_steps), tap_b, _fwd_starts(CONV_B), ts, dm, emit_b,
                   rolled_b)
        _on_first_step(copies, "wait")

        qa = (bg_ref[...].astype(F32) * va_ref[...]).astype(BF16)
        qa_ref[...] = qa
        ya = _dot(qa, wa_v[...])
        vb = vb_ref[...]
        xc = vb - jnp.mean(vb, axis=-1, keepdims=True)
        rstd = lax.rsqrt(jnp.mean(xc * xc, axis=-1, keepdims=True) + LN_EPS)
        ln = xc * rstd * lng_ref[...] + lnb_ref[...]
        sb = (ln * _sigmoid(ln)).astype(BF16)
        sb_ref[...] = sb
        yb = _dot(sb, wb_v[...])
        ya_ref[...] = ya.astype(BF16)
        yb_ref[...] = yb.astype(BF16)
        merged = (_sigmoid(za_ref[...].astype(F32)) * ya + _sigmoid(zb_ref[...].astype(F32)) * yb).astype(BF16)
        mg_ref[...] = merged
        mix = _dot(merged, wo_v[...])
        mix_ref[...] = mix
        x1_ref[...] = x_ref[...] + mix * _rms(mix) * g1p_ref[...]

    tok = lambda dt: jax.ShapeDtypeStruct((n_tok, dm), dt)
    return pl.pallas_call(
        body, name="fwd_mix", grid=(n_steps,),
        in_specs=(_halo_specs(ts, dm, n_tok) + _halo_specs(ts, dm, n_tok)
                  + [_rows(ts, dm, 0), _rows(ts, dm, 5), _rows(ts, dm, 6), _rows(ts, dm)]
                  + [_whole()] * 7 + [_hbm()]),
        out_specs=[_rows(ts, dm)] * 9,
        out_shape=[tok(F32), tok(F32), tok(BF16), tok(BF16), tok(BF16), tok(BF16), tok(BF16), tok(F32), tok(F32)],
        scratch_shapes=[pltpu.VMEM((dm, dm), BF16), pltpu.VMEM((dm, dm), BF16), pltpu.VMEM((dm, dm), BF16),
                        pltpu.VMEM((CONV_A, SUBLANE, dm), F32), pltpu.VMEM((CONV_B, SUBLANE, dm), F32),
                        pltpu.SemaphoreType.DMA((3 * N_DEV,)),
                        pltpu.VMEM((2, SUBLANE, ts + 2 * HALO, LANE), F32),
                        pltpu.VMEM((2, SUBLANE, ts + 2 * HALO, LANE), F32)],
        compiler_params=_params(48),
    )(p, p, p, u, u, u, proj, proj, proj, x, caw, cab, cbw, cbb, lng, lnb, g1post, slab)


def _mlp_fwd_bwd(x1, mix, tgt, g1post, g2pre, g2post, slab, ts):
    n_tok, dm = x1.shape
    rows, off, _ = _layout(dm)
    ff = 4 * dm

    def body(x1_ref, mix_ref, t_ref, g1p_ref, g2pre_ref, g2post_ref, slab_ref,
             f_ref, df1_ref, h2_ref, df2_ref, dmix_ref, dx1_ref, small_ref, w1_v, w2_v, relu_v, sems):
        w1_copies = _weight_copies(slab_ref, off["w1"], rows["w1"], w1_v, sems, 0)
        w2_copies = _weight_copies(slab_ref, off["w2"], rows["w2"], w2_v, sems, N_DEV)
        _on_first_step(w1_copies + w2_copies, "start")

        @pl.when(pl.program_id(0) == 0)
        def _():
            small_ref[...] = jnp.zeros_like(small_ref)

        _on_first_step(w1_copies + w2_copies, "wait")
        x1v = x1_ref[...]
        r3 = _rms(x1v)
        g2pre = g2pre_ref[...]
        h2 = (x1v * r3 * g2pre).astype(BF16)
        h2_ref[...] = h2
        for c in range(4):
            blk = pl.ds(c * dm, dm)
            relu = jnp.maximum(_dot_nt(h2, w1_v[blk, :]), 0.0)
            relu_v[:, c * dm:(c + 1) * dm] = relu
            f_ref[:, c * dm:(c + 1) * dm] = (relu * relu).astype(BF16)
        f2 = _dot(f_ref[...], w2_v[...])
        r4 = _rms(f2)
        g2post = g2post_ref[...]
        err = x1v + f2 * r4 * g2post - t_ref[...]
        dy = err * (1.0 / dm)
        small_ref[3:4, :] += _colsum(err * err)
        small_ref[0:1, :] += _colsum(dy * f2 * r4)
        df2 = _rms_bwd(dy, f2, r4, g2post).astype(BF16)
        df2_ref[...] = df2
        for c in range(4):
            blk = pl.ds(c * dm, dm)
            df1 = (_dot_nt(df2, w2_v[blk, :]) * (2.0 * relu_v[:, c * dm:(c + 1) * dm])).astype(BF16)
            df1_ref[:, c * dm:(c + 1) * dm] = df1
        dh2 = _dot(df1_ref[...], w1_v[...])
        small_ref[1:2, :] += _colsum(dh2 * x1v * r3)
        dx1 = dy + _rms_bwd(dh2, x1v, r3, g2pre)
        dx1_ref[...] = dx1
        mixv = mix_ref[...]
        r2 = _rms(mixv)
        small_ref[2:3, :] += _colsum(dx1 * mixv * r2)
        dmix_ref[...] = _rms_bwd(dx1, mixv, r2, g1p_ref[...]).astype(BF16)

    tok = lambda w, dt: jax.ShapeDtypeStruct((n_tok, w), dt)
    return pl.pallas_call(
        body, name="mlp_fwd_bwd", grid=(n_tok // ts,),
        in_specs=[_rows(ts, dm)] * 3 + [_whole()] * 3 + [_hbm()],
        out_specs=[_rows(ts, ff), _rows(ts, ff), _rows(ts, dm), _rows(ts, dm), _rows(ts, dm), _rows(ts, dm),
                   pl.BlockSpec((SUBLANE, dm), lambda i: (0, 0))],
        out_shape=[tok(ff, BF16), tok(ff, BF16), tok(dm, BF16), tok(dm, BF16), tok(dm, BF16), tok(dm, F32),
                   jax.ShapeDtypeStruct((SUBLANE, dm), F32)],
        scratch_shapes=[pltpu.VMEM((ff, dm), BF16), pltpu.VMEM((ff, dm), BF16), pltpu.VMEM((ts, ff), F32),
                        pltpu.SemaphoreType.DMA((2 * N_DEV,))],
        compiler_params=_params(56),
    )(x1, mix, tgt, g1post, g2pre, g2post, slab)


def _bwd_mix(dmix, ya, yb, proj, va, vb, lng, lnb, slab, ts, deps):
    n_tok, dm = dmix.shape
    rows, off, _ = _layout(dm)
    n_steps = n_tok // ts

    def body(dmix_ref, ya_ref, yb_ref, bg_ref, za_ref, zb_ref, va_ref, vb_ref, lng_ref, lnb_ref, slab_ref,
             dpa_ref, dya_ref, dyb_ref, dva_ref, dvb_ref, small_ref, wa_v, wb_v, wo_v, sems):
        wo_copies = _weight_copies(slab_ref, off["wo"], rows["wo"], wo_v, sems, 2 * N_DEV)
        ab_copies = (_weight_copies(slab_ref, off["wa"], rows["wa"], wa_v, sems, 0)
                     + _weight_copies(slab_ref, off["wb"], rows["wb"], wb_v, sems, N_DEV))
        _on_first_step(wo_copies + ab_copies, "start")

        @pl.when(pl.program_id(0) == 0)
        def _():
            small_ref[...] = jnp.zeros_like(small_ref)

        _on_first_step(wo_copies, "wait")
        dmerged = _dot_nt(dmix_ref[...], wo_v[...])
        _on_first_step(ab_copies, "wait")
        sa = _sigmoid(za_ref[...].astype(F32))
        sg = _sigmoid(zb_ref[...].astype(F32))
        dza = dmerged * ya_ref[...].astype(F32) * sa * (1.0 - sa)
        dzb = dmerged * yb_ref[...].astype(F32) * sg * (1.0 - sg)
        dpa_ref[:, dm:2 * dm] = dza.astype(BF16)
        dpa_ref[:, 2 * dm:3 * dm] = dzb.astype(BF16)
        small_ref[5:6, :] += _colsum(dza)
        small_ref[6:7, :] += _colsum(dzb)

        dya = (dmerged * sa).astype(BF16)
        dya_ref[...] = dya
        dqa = _dot_nt(dya, wa_v[...])
        dbg = dqa * va_ref[...]
        dpa_ref[:, 0:dm] = dbg.astype(BF16)
        small_ref[4:5, :] += _colsum(dbg)
        dva = dqa * bg_ref[...].astype(F32)
        dva_ref[...] = dva
        small_ref[2:3, :] += _colsum(dva)

        dyb = (dmerged * sg).astype(BF16)
        dyb_ref[...] = dyb
        dsb = _dot_nt(dyb, wb_v[...])
        vb = vb_ref[...]
        xc = vb - jnp.mean(vb, axis=-1, keepdims=True)
        rstd = lax.rsqrt(jnp.mean(xc * xc, axis=-1, keepdims=True) + LN_EPS)
        nrm = xc * rstd
        lng_v = lng_ref[...]
        ln = nrm * lng_v + lnb_ref[...]
        sl = _sigmoid(ln)
        dln = dsb * (sl * (1.0 + ln * (1.0 - sl)))
        small_ref[0:1, :] += _colsum(dln * nrm)
        small_ref[1:2, :] += _colsum(dln)
        dn = dln * lng_v
        dvb = rstd * (dn - jnp.mean(dn, axis=-1, keepdims=True)
                      - nrm * jnp.mean(dn * nrm, axis=-1, keepdims=True))
        dvb_ref[...] = dvb
        small_ref[3:4, :] += _colsum(dvb)

    tok = lambda w, dt: jax.ShapeDtypeStruct((n_tok, w), dt)
    return pl.pallas_call(
        _after(body, deps), name="bwd_mix", grid=(n_steps,),
        in_specs=([_whole()] * len(deps) + [_rows(ts, dm)] * 3
                  + [_rows(ts, dm, 0), _rows(ts, dm, 5), _rows(ts, dm, 6)]
                  + [_rows(ts, dm)] * 2 + [_whole()] * 2 + [_hbm()]),
        out_specs=[_rows(ts, 3 * dm), _rows(ts, dm), _rows(ts, dm), _rows(ts, dm), _rows(ts, dm),
                   pl.BlockSpec((SUBLANE, dm), lambda i: (0, 0))],
        out_shape=[tok(3 * dm, BF16), tok(dm, BF16), tok(dm, BF16), tok(dm, F32), tok(dm, F32),
                   jax.ShapeDtypeStruct((SUBLANE, dm), F32)],
        scratch_shapes=[pltpu.VMEM((dm, dm), BF16), pltpu.VMEM((dm, dm), BF16), pltpu.VMEM((dm, dm), BF16),
                        pltpu.SemaphoreType.DMA((3 * N_DEV,))],
        compiler_params=_params(48),
    )(*deps, dmix, ya, yb, proj, proj, proj, va, vb, lng, lnb, slab)


def _bwd_conv(dva, dvb, p, u, proj, dpa, caw, cbw, ts, deps):
    n_tok, dm = dva.shape
    n_steps = n_tok // ts
    small_rows = 40

    def body(dva_ref, dva_prev, dva_next, dvb_ref, dvb_prev, dvb_next, p_ref, u_ref,
             cg_ref, ha_ref, a_ref, g_ref, dpa_ref, caw_ref, cbw_ref,
             dproj_ref, small_ref,
             dp_v, du_v, tap_a, tap_b, gwa_v, gwb_v):
        i = pl.program_id(0)

        @pl.when(i == 0)
        def _():
            _broadcast_taps(caw_ref, tap_a, CONV_A)
            _broadcast_taps(cbw_ref, tap_b, CONV_B)
            small_ref[...] = jnp.zeros_like(small_ref)
            gwa_v[...] = jnp.zeros_like(gwa_v)
            gwb_v[...] = jnp.zeros_like(gwb_v)

        def emit_dp(r0, lanes, acc):
            dp_v[pl.ds(r0, acc.shape[0]), lanes] = acc

        def emit_du(r0, lanes, acc):
            du_v[pl.ds(r0, acc.shape[0]), lanes] = acc

        _conv_bwd_tile(_with_halos(dva_ref, dva_prev, dva_next, i, n_steps), p_ref, tap_a, gwa_v, CONV_A, ts, dm,
                       emit_dp)
        _conv_bwd_tile(_with_halos(dvb_ref, dvb_prev, dvb_next, i, n_steps), u_ref, tap_b, gwb_v, CONV_B, ts, dm,
                       emit_du)

        dp = dp_v[...]
        dcg = dp * ha_ref[...].astype(F32)
        dha = dp * cg_ref[...].astype(F32)
        du = du_v[...]
        sg = _sigmoid(g_ref[...].astype(F32))
        da = du * sg
        dg = du * a_ref[...].astype(F32) * sg * (1.0 - sg)
        dproj_ref[:, 0:dm] = dpa_ref[:, 0:dm]
        dproj_ref[:, dm:2 * dm] = dcg.astype(BF16)
        dproj_ref[:, 2 * dm:3 * dm] = dha.astype(BF16)
        dproj_ref[:, 3 * dm:4 * dm] = da.astype(BF16)
        dproj_ref[:, 4 * dm:5 * dm] = dg.astype(BF16)
        dproj_ref[:, 5 * dm:7 * dm] = dpa_ref[:, dm:3 * dm]
        small_ref[3:4, :] += _colsum(dcg)
        small_ref[4:5, :] += _colsum(dha)
        small_ref[5:6, :] += _colsum(da)
        small_ref[6:7, :] += _colsum(dg)

        @pl.when(i == n_steps - 1)
        def _():
            for k in range(CONV_A):
                small_ref[k:k + 1, :] = _colsum(gwa_v[k])
            for k in range(CONV_B):
                small_ref[SUBLANE + k:SUBLANE + k + 1, :] = _colsum(gwb_v[k])

    return pl.pallas_call(
        _after(body, deps), name="bwd_conv", grid=(n_steps,),
        in_specs=([_whole()] * len(deps) + _halo_specs(ts, dm, n_tok) * 2 + [_rows(ts, dm)] * 2
                  + [_rows(ts, dm, 1), _rows(ts, dm, 2), _rows(ts, dm, 3), _rows(ts, dm, 4), _rows(ts, 3 * dm)]
                  + [_whole()] * 2),
        out_specs=[_rows(ts, 7 * dm), pl.BlockSpec((small_rows, dm), lambda i: (0, 0))],
        out_shape=[jax.ShapeDtypeStruct((n_tok, 7 * dm), BF16), jax.ShapeDtypeStruct((small_rows, dm), F32)],
        scratch_shapes=[pltpu.VMEM((ts, dm), F32), pltpu.VMEM((ts, dm), F32),
                        pltpu.VMEM((CONV_A, SUBLANE, dm), F32), pltpu.VMEM((CONV_B, SUBLANE, dm), F32),
                        pltpu.VMEM((CONV_A, SUBLANE, dm), F32), pltpu.VMEM((CONV_B, SUBLANE, dm), F32)],
        compiler_params=_params(48),
    )(*deps, dva, dva, dva, dvb, dvb, dvb, p, u, proj, proj, proj, proj, dpa, caw, cbw)


def _bwd_in(dproj, x, dx1, g1, slab, ts, deps):
    n_tok, dm = x.shape
    rows, off, _ = _layout(dm)
    width = 7 * dm

    def body(dproj_ref, x_ref, dx1_ref, g1_ref, slab_ref, gx_ref, small_ref, w_v, sems):
        copies = _weight_copies(slab_ref, off["win"], rows["win"], w_v, sems, 0)
        _on_first_step(copies, "start")

        @pl.when(pl.program_id(0) == 0)
        def _():
            small_ref[...] = jnp.zeros_like(small_ref)

        _on_first_step(copies, "wait")
        dh = _dot(dproj_ref[...], w_v[...])
        xv = x_ref[...]
        r1 = _rms(xv)
        small_ref[0:1, :] += _colsum(dh * xv * r1)
        gx_ref[...] = dx1_ref[...] + _rms_bwd(dh, xv, r1, g1_ref[...])

    return pl.pallas_call(
        _after(body, deps), name="bwd_in", grid=(n_tok // ts,),
        in_specs=[_whole()] * len(deps) + [_rows(ts, width), _rows(ts, dm), _rows(ts, dm), _whole(), _hbm()],
        out_specs=[_rows(ts, dm), pl.BlockSpec((SUBLANE, dm), lambda i: (0, 0))],
        out_shape=[jax.ShapeDtypeStruct((n_tok, dm), F32), jax.ShapeDtypeStruct((SUBLANE, dm), F32)],
        scratch_shapes=[pltpu.VMEM((width, dm), BF16), pltpu.SemaphoreType.DMA((N_DEV,))],
        compiler_params=_params(56),
    )(*deps, dproj, x, dx1, g1, slab)


def _wgrad(a, b, name, tm, tk, out_dtype):
    n_tok, m = a.shape
    n = b.shape[1]
    k_steps = n_tok // tk

    def body(a_ref, b_ref, o_ref, acc_v):
        k = pl.program_id(1)

        @pl.when(k == 0)
        def _():
            acc_v[...] = jnp.zeros_like(acc_v)

        acc_v[...] += _dot_tn(a_ref[...], b_ref[...])

        @pl.when(k == k_steps - 1)
        def _():
            o_ref[...] = acc_v[...].astype(o_ref.dtype)

    return pl.pallas_call(
        body, name=name, grid=(m // tm, k_steps),
        in_specs=[pl.BlockSpec((tk, tm), lambda i, k: (k, i)), pl.BlockSpec((tk, n), lambda i, k: (k, 0))],
        out_specs=pl.BlockSpec((tm, n), lambda i, k: (i, 0)),
        out_shape=pltpu.HBM((m, n), out_dtype),
        scratch_shapes=[pltpu.VMEM((tm, n), F32)],
        compiler_params=pltpu.CompilerParams(dimension_semantics=("arbitrary", "arbitrary"),
                                             vmem_limit_bytes=40 * MIB),
    )(a, b)


def _adamw(w, g, m, v):
    m = ADAM_B1 * m + (1.0 - ADAM_B1) * g
    v = ADAM_B2 * v + (1.0 - ADAM_B2) * (g * g)
    m_hat = m / (1.0 - ADAM_B1 ** ADAM_STEP)
    v_hat = v / (1.0 - ADAM_B2 ** ADAM_STEP)
    delta = -ADAM_LR * (m_hat / (jnp.sqrt(v_hat) + ADAM_EPS) + ADAM_WD * w)
    return delta, m, v


def _adam_big(recv, part, me, off, rows, w, m, v, transpose, name, tr):
    dm = recv.shape[2]
    per = rows // tr

    def body(me_ref, own_ref, r_ref, w_ref, m_ref, v_ref, g_ref, d_ref, mo_ref, vo_ref):
        g = own_ref[...].astype(F32)
        for k in range(len(FLIPS)):
            g = g + r_ref[k].astype(F32)
        if transpose:
            g = g.T
        delta, m_new, v_new = _adamw(w_ref[...], g, m_ref[...], v_ref[...])
        g_ref[...] = g
        d_ref[...] = delta
        mo_ref[...] = m_new
        vo_ref[...] = v_new

    if transpose:
        blk = pl.BlockSpec((dm, tr), lambda i, me_ref: (0, i))
    else:
        blk = pl.BlockSpec((tr, dm), lambda i, me_ref: (i, 0))
    first = off // tr
    return pl.pallas_call(
        body, name=name,
        grid_spec=pltpu.PrefetchScalarGridSpec(
            num_scalar_prefetch=1, grid=(per,),
            in_specs=[pl.BlockSpec((tr, dm), lambda i, me_ref: (me_ref[0] * per + i, 0)),
                      pl.BlockSpec((len(FLIPS), tr, dm), lambda i, me_ref: (0, first + i, 0)), blk, blk, blk],
            out_specs=[blk] * 4),
        out_shape=[jax.ShapeDtypeStruct(w.shape, F32)] * 4,
        compiler_params=_params(32),
    )(me, *[_in_hbm(a) for a in (part, recv, w, m, v)])


LOSS_ROW = 15
CONV_A_ROW = 16
CONV_B_ROW = 24


def _adam_small(recv_small, recv_last, me, params, d_model):
    n = len(params)
    cw = d_model // N_DEV

    def body(me_ref, r_ref, rc_ref, l_ref, *refs):
        ins, loss_ref, outs = refs[:3 * n], refs[3 * n], refs[3 * n + 1:3 * n + 1 + 4 * n]
        g_v, gc_v, last_v = refs[3 * n + 1 + 4 * n:]
        g, gc, last = r_ref[0], rc_ref[0], l_ref[0]
        for d in range(1, N_DEV):
            g, gc, last = g + r_ref[d], gc + rc_ref[d], last + l_ref[d]
        g_v[...], gc_v[...], last_v[...] = g, gc, last
        loss_ref[...] = (0.5 / d_model) * jnp.sum(g_v[LOSS_ROW:LOSS_ROW + 1, :], axis=-1, keepdims=True)
        for j, (row0, own_columns, (w, _, _)) in enumerate(params):
            w_ref, m_ref, v_ref = ins[3 * j:3 * j + 3]
            source = gc_v if own_columns else (last_v if row0 == 0 else g_v)
            grad = source[row0:row0 + w.shape[0], :]
            delta, m_new, v_new = _adamw(w_ref[...], grad, m_ref[...], v_ref[...])
            for ref, val in zip(outs[4 * j:4 * j + 4], (grad, delta, m_new, v_new)):
                ref[...] = val

    full = lambda shape: pl.BlockSpec(shape, lambda i, me_ref: (0,) * len(shape))
    stack_rows = recv_small.shape[1]
    flat = [a for _, _, triple in params for a in triple]
    shapes = [w.shape for _, _, (w, _, _) in params for _ in range(4)]
    out = pl.pallas_call(
        body, name="adam_small",
        grid_spec=pltpu.PrefetchScalarGridSpec(
            num_scalar_prefetch=1, grid=(1,),
            in_specs=[full(recv_small.shape),
                      pl.BlockSpec((N_DEV, stack_rows, cw), lambda i, me_ref: (0, 0, me_ref[0])),
                      full(recv_last.shape)] + [full(a.shape) for a in flat],
            out_specs=[full((1, 1))] + [full(s) for s in shapes],
            scratch_shapes=[pltpu.VMEM((stack_rows, d_model), F32), pltpu.VMEM((stack_rows, cw), F32),
                            pltpu.VMEM(recv_last.shape[1:], F32)]),
        out_shape=[jax.ShapeDtypeStruct((1, 1), F32)] + [jax.ShapeDtypeStruct(s, F32) for s in shapes],
    )(me, *[_in_hbm(a) for a in (recv_small, recv_small, recv_last, *flat)])
    return out[0], [tuple(out[1 + 4 * j:5 + 4 * j]) for j in range(n)]


def _tile(n_tok, want):
    return min(want, n_tok)


def kernel(x, norm1_pre_g, w_in, b_in, conv_a_w, conv_a_b, w_a_out, conv_b_w, conv_b_b, ln_b_g, ln_b_b, w_b_out, w_o, norm1_post_g, norm2_pre_g, w_mlp_in, w_mlp_out, norm2_post_g, loss_target, m_norm1_pre_g, m_w_in, m_b_in, m_conv_a_w, m_conv_a_b, m_w_a_out, m_conv_b_w, m_conv_b_b, m_ln_b_g, m_ln_b_b, m_w_b_out, m_w_o, m_norm1_post_g, m_norm2_pre_g, m_w_mlp_in, m_w_mlp_out, m_norm2_post_g, v_norm1_pre_g, v_w_in, v_b_in, v_conv_a_w, v_conv_a_b, v_w_a_out, v_conv_b_w, v_conv_b_b, v_ln_b_g, v_ln_b_b, v_w_b_out, v_w_o, v_norm1_post_g, v_norm2_pre_g, v_w_mlp_in, v_w_mlp_out, v_norm2_post_g):
    n_tok, dm = x.shape[1], x.shape[2]
    rows, off, slab_rows = _layout(dm)
    cw = dm // N_DEV
    xs = x.reshape(n_tok, dm)
    tgt = loss_target.reshape(n_tok, dm)
    row = lambda vec: vec.reshape(1, -1)
    scattered = lambda group: jax.ShapeDtypeStruct((len(FLIPS), slab_rows[group], dm), BF16)
    tm, tk = min(dm, 1024), _tile(n_tok, 2048)
    me = (4 * lax.axis_index("x") + 2 * lax.axis_index("y") + lax.axis_index("c")).astype(jnp.int32).reshape(1)

    conv_own = jnp.concatenate([conv_a_w, jnp.zeros((SUBLANE - CONV_A, cw), F32), conv_b_w,
                                jnp.zeros((1, cw), F32)], axis=0)
    own_in, land_in = _place_cast([(w_in, True)], me, "place_w_in")
    slab_in, conv_all = _all_gather_two_level([own_in, conv_own], [land_in, None], "gather_w_in")
    conv_full = conv_all.transpose(1, 0, 2).reshape(conv_own.shape[0], dm)
    caw, cbw = conv_full[0:CONV_A], conv_full[SUBLANE:SUBLANE + CONV_B]
    own_abo, land_abo = _place_cast([(w_a_out, False), (w_b_out, False), (w_o, False)], me, "place_abo")
    own_mlp, land_mlp = _place_cast([(w_mlp_in, True), (w_mlp_out, False)], me, "place_mlp")
    ag_abo = _exchange_start([_Part(own_abo, False, slab_rows["abo"], 0, 0)], [land_abo],
                             "gather_abo_start", after=slab_in)
    ag_mlp = _exchange_start([_Part(own_mlp, False, slab_rows["mlp"], 0, 0)], [land_mlp],
                             "gather_mlp_start", after=ag_abo.token)

    proj, p, u, h = _fwd_in(xs, row(norm1_pre_g), row(b_in), slab_in, _tile(n_tok, 512),
                            [ag_abo.token, ag_mlp.token])
    _, (slab_abo,) = _exchange_wait(ag_abo, "gather_abo_wait", after=proj)
    va, vb, ya, yb, qa, sb, merged, mix, x1 = _fwd_mix(
        p, u, proj, xs, caw, row(conv_a_b), cbw, row(conv_b_b), row(ln_b_g), row(ln_b_b), row(norm1_post_g),
        slab_abo, _tile(n_tok, 256))
    _, (slab_mlp,) = _exchange_wait(ag_mlp, "gather_mlp_wait", after=x1)
    f, df1, h2, df2, dmix, dx1, small_mlp = _mlp_fwd_bwd(
        x1, mix, tgt, row(norm1_post_g), row(norm2_pre_g), row(norm2_post_g), slab_mlp, _tile(n_tok, 256))

    rs_mlp = _exchange_start(
        [_Part(_wgrad(df1, h2, "wgrad_mlp_in", tm, tk, BF16), True, rows["w1"], 0, off["w1"]),
         _Part(_wgrad(f, df2, "wgrad_mlp_out", tm, tk, BF16), True, rows["w2"], 0, off["w2"])],
        [scattered("mlp")], "scatter_mlp_start")
    dpa, dya, dyb, dva, dvb, small_mix = _bwd_mix(
        dmix, ya, yb, proj, va, vb, row(ln_b_g), row(ln_b_b), slab_abo, _tile(n_tok, 256), [rs_mlp.token])
    rs_abo = _exchange_start(
        [_Part(_wgrad(qa, dya, "wgrad_a_out", tm, tk, BF16), True, rows["wa"], 0, off["wa"]),
         _Part(_wgrad(sb, dyb, "wgrad_b_out", tm, tk, BF16), True, rows["wb"], 0, off["wb"]),
         _Part(_wgrad(merged, dmix, "wgrad_o", tm, tk, BF16), True, rows["wo"], 0, off["wo"])],
        [scattered("abo")], "scatter_abo_start")
    dproj, small_conv = _bwd_conv(dva, dvb, p, u, proj, dpa, caw, cbw, _tile(n_tok, 256), [rs_abo.token])

    zeros = lambda r: jnp.zeros((r, dm), F32)
    small = jnp.concatenate([
        zeros(1),
        small_mix[2:3],
        small_mix[3:4],
        small_mix[0:2],
        small_mlp[2:3],
        small_mlp[1:2],
        small_mlp[0:1],
        small_mix[4:5], small_conv[3:7], small_mix[5:7],
        small_mlp[3:4],
        small_conv[0:CONV_A], zeros(SUBLANE - CONV_A),
        small_conv[8:8 + CONV_B], zeros(1),
    ], axis=0)

    rs_in = _exchange_start(
        [_Part(_wgrad(dproj, h, "wgrad_in", tm, tk, BF16), True, rows["win"], 0, off["win"]),
         _Part(small, False, small.shape[0], 1, 0)],
        [scattered("in"), _place_own(small, (N_DEV,) + small.shape, me, "place_small")], "scatter_in_start")
    grad_x, small_in = _bwd_in(dproj, xs, dx1, row(norm1_pre_g), slab_in, _tile(n_tok, 256), [rs_in.token])

    tr = min(LANE, rows["wa"])
    (g_w1, g_w2), (recv_mlp,) = _exchange_wait(rs_mlp, "scatter_mlp_wait", after=grad_x)
    (g_wa, g_wb, g_wo), (recv_abo,) = _exchange_wait(rs_abo, "scatter_abo_wait", after=grad_x)
    big = {
        "w_mlp_in": _adam_big(recv_mlp, g_w1, me, off["w1"], rows["w1"], w_mlp_in, m_w_mlp_in, v_w_mlp_in, True,
                              "adam_w_mlp_in", tr),
        "w_mlp_out": _adam_big(recv_mlp, g_w2, me, off["w2"], rows["w2"], w_mlp_out, m_w_mlp_out, v_w_mlp_out,
                               False, "adam_w_mlp_out", tr),
        "w_a_out": _adam_big(recv_abo, g_wa, me, off["wa"], rows["wa"], w_a_out, m_w_a_out, v_w_a_out, False,
                             "adam_w_a_out", tr),
        "w_b_out": _adam_big(recv_abo, g_wb, me, off["wb"], rows["wb"], w_b_out, m_w_b_out, v_w_b_out, False,
                             "adam_w_b_out", tr),
        "w_o": _adam_big(recv_abo, g_wo, me, off["wo"], rows["wo"], w_o, m_w_o, v_w_o, False, "adam_w_o", tr),
    }
    (g_win, _), (recv_in, recv_small) = _exchange_wait(rs_in, "scatter_in_wait", after=big["w_o"][3])
    recv_last, = _all_gather([small_in], "gather_last")
    big["w_in"] = _adam_big(recv_in, g_win, me, off["win"], rows["win"], w_in, m_w_in, v_w_in, True, "adam_w_in", tr)

    small_names = ("norm1_pre_g", "conv_a_b", "conv_b_b", "ln_b_g", "ln_b_b", "norm1_post_g", "norm2_pre_g",
                   "norm2_post_g")
    given = dict(
        norm1_pre_g=(norm1_pre_g, m_norm1_pre_g, v_norm1_pre_g), conv_a_b=(conv_a_b, m_conv_a_b, v_conv_a_b),
        conv_b_b=(conv_b_b, m_conv_b_b, v_conv_b_b), ln_b_g=(ln_b_g, m_ln_b_g, v_ln_b_g),
        ln_b_b=(ln_b_b, m_ln_b_b, v_ln_b_b), norm1_post_g=(norm1_post_g, m_norm1_post_g, v_norm1_post_g),
        norm2_pre_g=(norm2_pre_g, m_norm2_pre_g, v_norm2_pre_g),
        norm2_post_g=(norm2_post_g, m_norm2_post_g, v_norm2_post_g))
    params = [(j, False, tuple(row(a) for a in given[name])) for j, name in enumerate(small_names)]
    params.append((SUBLANE, False, tuple(a.reshape(7, dm) for a in (b_in, m_b_in, v_b_in))))
    params.append((CONV_A_ROW, True, (conv_a_w, m_conv_a_w, v_conv_a_w)))
    params.append((CONV_B_ROW, True, (conv_b_w, m_conv_b_w, v_conv_b_w)))
    loss, small_out = _adam_small(recv_small, recv_last, me, params, dm)
    small_leaves = {name: tuple(a.reshape(dm) for a in small_out[j]) for j, name in enumerate(small_names)}
    small_leaves["b_in"] = tuple(a.reshape(7 * dm) for a in small_out[len(small_names)])
    small_leaves["conv_a_w"] = small_out[len(small_names) + 1]
    small_leaves["conv_b_w"] = small_out[len(small_names) + 2]

    order = ("norm1_pre_g", "w_in", "b_in", "conv_a_w", "conv_a_b", "w_a_out", "conv_b_w", "conv_b_b", "ln_b_g",
             "ln_b_b", "w_b_out", "w_o", "norm1_post_g", "norm2_pre_g", "w_mlp_in", "w_mlp_out", "norm2_post_g")
    leaves = [big[name] if name in big else small_leaves[name] for name in order]
    grads, deltas, new_m, new_v = zip(*leaves)
    return (loss.reshape(()), grad_x.reshape(x.shape), *grads, *deltas, *new_m, *new_v)
```

```python
from typing import NamedTuple

import jax
import jax.numpy as jnp
from jax import lax
from jax.experimental import pallas as pl
from jax.experimental.pallas import tpu as pltpu

F32 = jnp.float32
BF16 = jnp.bfloat16

RMS_EPS = 1e-6
LN_EPS = 1e-5
ADAM_LR = 0.001
ADAM_B1 = 0.9
ADAM_B2 = 0.999
ADAM_EPS = 1e-08
ADAM_WD = 0.01
ADAM_STEP = 10

N_DEV = 8
CONV_A = 3
CONV_B = 31
LANE = 128
SUBLANE = 8
HALO = 16
FWD_CONV_ROWS = 32
BWD_CONV_ROWS = 64
MIB = 1 << 20
FLIPS = ((0, 0, 1), (0, 1, 0), (1, 0, 0), (0, 1, 1), (1, 0, 1), (1, 1, 0), (1, 1, 1))
MESH = pl.DeviceIdType.MESH


def _layout(d_model):
    e = d_model // N_DEV
    rows = {"win": 7 * e, "w1": 4 * e, "w2": 4 * e, "wa": e, "wb": e, "wo": e}
    off = {"win": 0, "w1": 0, "w2": 4 * e, "wa": 0, "wb": e, "wo": 2 * e}
    return rows, off, {"in": 7 * e, "mlp": 8 * e, "abo": 3 * e}


def _after(body, deps):
    def ordered(*refs):
        return body(*refs[len(deps):])
    return ordered


def _params(vmem_mib):
    return pltpu.CompilerParams(dimension_semantics=("arbitrary",), vmem_limit_bytes=vmem_mib * MIB)


def _whole():
    return pl.BlockSpec(memory_space=pltpu.VMEM)


def _hbm():
    return pl.BlockSpec(memory_space=pl.ANY)


def _in_hbm(a):
    return pltpu.with_memory_space_constraint(a, pltpu.HBM)


def _rows(ts, width, col=0):
    return pl.BlockSpec((ts, width), lambda i: (i, col))


def _halo_specs(ts, width, n_rows):
    per = ts // HALO
    last = n_rows // HALO - 1
    return [
        pl.BlockSpec((ts, width), lambda i: (i, 0)),
        pl.BlockSpec((HALO, width), lambda i: (jnp.maximum(i * per - 1, 0), 0)),
        pl.BlockSpec((HALO, width), lambda i: (jnp.minimum((i + 1) * per, last), 0)),
    ]


def _dot(a, b):
    return jnp.dot(a, b, preferred_element_type=F32)


def _dot_nt(a, b):
    return lax.dot_general(a, b, (((1,), (1,)), ((), ())), preferred_element_type=F32)


def _dot_tn(a, b):
    return lax.dot_general(a, b, (((0,), (0,)), ((), ())), preferred_element_type=F32)


def _rms(u):
    return lax.rsqrt(jnp.mean(u * u, axis=-1, keepdims=True) + RMS_EPS)


def _rms_bwd(dz, u, r, g):
    dzg = dz * g
    return r * dzg - u * (r * r * r) * jnp.mean(dzg * u, axis=-1, keepdims=True)


def _colsum(v):
    return jnp.sum(v, axis=0, keepdims=True)


def _sigmoid(v):
    return jax.nn.sigmoid(v)


def _weight_copies(slab_ref, off, rows, dst_ref, sems, first_sem):
    return [pltpu.make_async_copy(slab_ref.at[d, pl.ds(off, rows), :], dst_ref.at[pl.ds(d * rows, rows), :],
                                  sems.at[first_sem + d]) for d in range(N_DEV)]


def _on_first_step(copies, method):
    @pl.when(pl.program_id(0) == 0)
    def _():
        for cp in copies:
            getattr(cp, method)()


def _with_halos(main_ref, prev_ref, next_ref, i, n_steps):
    return (main_ref, jnp.where(i > 0, prev_ref[...], 0.0), jnp.where(i < n_steps - 1, next_ref[...], 0.0))


def _broadcast_taps(w_ref, wb_ref, n_taps):
    for k in range(n_taps):
        wb_ref[k] = jnp.broadcast_to(w_ref[k:k + 1, :], wb_ref.shape[1:])


def _conv_tile(tile, wb_ref, starts, ts, width, emit, rolled_ref, rows=FWD_CONV_ROWS):
    main_ref, prev, nxt = tile
    span = ts + 2 * HALO
    nv = rows // SUBLANE
    for cb in range(width // LANE):
        lanes = slice(cb * LANE, (cb + 1) * LANE)
        slot = cb % 2
        window = jnp.concatenate([prev[:, lanes], main_ref[:, lanes], nxt[:, lanes]], axis=0)
        for b in sorted({st % SUBLANE for st in starts}):
            rolled_ref[slot, b] = window if b == 0 else pltpu.roll(window, span - b, axis=0)
        for r0 in range(0, ts, rows):
            acc = jnp.zeros((nv, SUBLANE, LANE), F32)
            for k, st in enumerate(starts):
                shifted = rolled_ref[slot, st % SUBLANE, pl.ds(r0 + st - st % SUBLANE, rows), :]
                acc = acc + shifted.reshape(nv, SUBLANE, LANE) * wb_ref[k, :, lanes][None]
            emit(r0, pl.ds(cb * LANE, LANE), acc.reshape(rows, LANE))


def _window(tile, r0, cb, ts, rows):
    main_ref, prev, nxt = tile
    lanes = slice(cb * LANE, (cb + 1) * LANE)
    lo, hi = max(r0 - HALO, 0), min(r0 + rows + HALO, ts)
    pieces = [prev[:, lanes]] if r0 - HALO < 0 else []
    pieces.append(main_ref[lo:hi, lanes])
    if r0 + rows + HALO > ts:
        pieces.append(nxt[:, lanes])
    return pieces[0] if len(pieces) == 1 else jnp.concatenate(pieces, axis=0)


def _phases(starts):
    groups = {}
    for k, st in enumerate(starts):
        groups.setdefault(st % SUBLANE, []).append((k, st // SUBLANE))
    return sorted(groups.items())


def _shifted(blk, b):
    n = blk.shape[0]
    rolled = blk if b == 0 else pltpu.roll(blk, n - b, axis=0)
    return rolled.reshape(n // SUBLANE, SUBLANE, blk.shape[1])


def _conv_bwd_tile(dv_tile, u_ref, wb_ref, acc_ref, n_taps, ts, width, emit, rows=BWD_CONV_ROWS):
    groups = _phases(_bwd_starts(n_taps))
    nv = rows // SUBLANE
    for r0 in range(0, ts, rows):
        for cb in range(width // LANE):
            lanes = pl.ds(cb * LANE, LANE)
            blk = _window(dv_tile, r0, cb, ts, rows)
            u = u_ref[pl.ds(r0, rows), lanes].reshape(nv, SUBLANE, LANE)
            du = jnp.zeros((nv, SUBLANE, LANE), F32)
            for b, taps in groups:
                sh = _shifted(blk, b)
                for k, m in taps:
                    du = du + sh[m:m + nv] * wb_ref[k, :, lanes][None]
                    acc_ref[k, :, lanes] += jnp.sum(sh[m:m + nv] * u, axis=0)
            emit(r0, lanes, du.reshape(rows, LANE))


def _fwd_starts(n_taps):
    pad = (n_taps - 1) // 2
    return [HALO - pad + k for k in range(n_taps)]


def _bwd_starts(n_taps):
    pad = (n_taps - 1) // 2
    return [HALO + pad - k for k in range(n_taps)]


def _peer(x, y, c, flip):
    fx, fy, fc = flip
    return (1 - x if fx else x, 1 - y if fy else y, 1 - c if fc else c)


def _all_gather(shards, name):
    n = len(shards)

    def body(*refs):
        ins, outs = refs[:n], refs[n:2 * n]
        send_sems, recv_sems, local_sems = refs[2 * n:]
        x, y, c = lax.axis_index("x"), lax.axis_index("y"), lax.axis_index("c")
        me = 4 * x + 2 * y + c
        local = [pltpu.make_async_copy(ins[j], outs[j].at[me], local_sems.at[j]) for j in range(n)]
        for cp in local:
            cp.start()
        sends, recvs = [], []
        for k, flip in enumerate(FLIPS):
            px, py, pc = _peer(x, y, c, flip)
            peer = 4 * px + 2 * py + pc
            for j in range(n):
                sem = k * n + j
                sends.append(pltpu.make_async_remote_copy(
                    src_ref=ins[j], dst_ref=outs[j].at[me], send_sem=send_sems.at[sem], recv_sem=recv_sems.at[sem],
                    device_id=(px, py, pc), device_id_type=MESH))
                recvs.append(pltpu.make_async_remote_copy(
                    src_ref=ins[j], dst_ref=outs[j].at[peer], send_sem=send_sems.at[sem], recv_sem=recv_sems.at[sem],
                    device_id=(px, py, pc), device_id_type=MESH))
        for cp in sends:
            cp.start()
        for cp in recvs:
            cp.wait_recv()
        for cp in sends:
            cp.wait_send()
        for cp in local:
            cp.wait()

    return pl.pallas_call(
        body, name=name,
        out_shape=[jax.ShapeDtypeStruct((N_DEV,) + s.shape, s.dtype) for s in shards],
        in_specs=[_hbm()] * n, out_specs=[_hbm()] * n,
        scratch_shapes=[pltpu.SemaphoreType.DMA((7 * n,)), pltpu.SemaphoreType.DMA((7 * n,)),
                        pltpu.SemaphoreType.DMA((n,))],
    )(*shards)


def _place_own(src, n_slots_shape, me, name):
    rows, width = src.shape
    tr = next(t for t in (256, 128, 64, 32, 16, SUBLANE) if rows % t == 0)

    def body(me_ref, src_ref, out_ref):
        out_ref[...] = src_ref[...]

    return pl.pallas_call(
        body, name=name,
        grid_spec=pltpu.PrefetchScalarGridSpec(
            num_scalar_prefetch=1, grid=(rows // tr,),
            in_specs=[pl.BlockSpec((tr, width), lambda i, me_ref: (i, 0))],
            out_specs=pl.BlockSpec((None, tr, width), lambda i, me_ref: (me_ref[0], i, 0))),
        out_shape=pltpu.HBM(n_slots_shape, src.dtype),
    )(me, src)


def _place_cast(pieces, me, name):
    n = len(pieces)
    counts = [a.shape[1] if t else a.shape[0] for a, t in pieces]
    width = pieces[0][0].shape[0] if pieces[0][1] else pieces[0][0].shape[1]
    total = sum(counts)

    def body(me_ref, *refs):
        ins, own_ref, land_ref = refs[:n], refs[n], refs[n + 1]
        first = 0
        for (a, transpose), in_ref, count in zip(pieces, ins, counts):
            block = (in_ref[...].T if transpose else in_ref[...]).astype(BF16)
            own_ref[first:first + count, :] = block
            land_ref[first:first + count, :] = block
            first += count

    return pl.pallas_call(
        body, name=name,
        grid_spec=pltpu.PrefetchScalarGridSpec(
            num_scalar_prefetch=1, grid=(1,),
            in_specs=[pl.BlockSpec(a.shape, lambda i, me_ref: (0, 0)) for a, _ in pieces],
            out_specs=[pl.BlockSpec((total, width), lambda i, me_ref: (0, 0)),
                       pl.BlockSpec((None, total, width), lambda i, me_ref: (me_ref[0], 0, 0))]),
        out_shape=[pltpu.HBM((total, width), BF16), pltpu.HBM((N_DEV, total, width), BF16)],
        compiler_params=_params(32),
    )(me, *[_in_hbm(a) for a, _ in pieces])


def _all_gather_two_level(shards, placed, name):
    n = len(shards)
    given = [j for j in range(n) if placed[j] is not None]

    def body(*refs):
        ins, outs = refs[:n], refs[n + len(given):2 * n + len(given)]
        send_sems, recv_sems, local_sems = refs[2 * n + len(given):]
        x, y, c = lax.axis_index("x"), lax.axis_index("y"), lax.axis_index("c")
        me, sibling = (x, y, c), (x, y, 1 - c)
        chips = [(1 - x, y), (x, 1 - y), (1 - x, 1 - y)]

        def slot(j, dev):
            return outs[j].at[4 * dev[0] + 2 * dev[1] + dev[2]]

        def copy(k, j, block, to, src=None):
            return pltpu.make_async_remote_copy(
                src_ref=slot(j, block) if src is None else src, dst_ref=slot(j, block),
                send_sem=send_sems.at[k * n + j], recv_sem=recv_sems.at[k * n + j], device_id=to, device_id_type=MESH)

        local = [pltpu.make_async_copy(ins[j], slot(j, me), local_sems.at[j]) for j in range(n) if j not in given]
        for cp in local:
            cp.start()
        first = [copy(0, j, me, sibling, src=ins[j]) for j in range(n)]
        first += [copy(1 + t, j, me, (*chip, c), src=ins[j]) for t, chip in enumerate(chips) for j in range(n)]
        for cp in first:
            cp.start()
        passed = []
        for t, chip in enumerate(chips):
            for j in range(n):
                copy(1 + t, j, (*chip, c), me).wait_recv()
                passed.append(copy(4 + t, j, (*chip, c), sibling))
                passed[-1].start()
        for j in range(n):
            copy(0, j, sibling, me).wait_recv()
        for t, chip in enumerate(chips):
            for j in range(n):
                copy(4 + t, j, (*chip, 1 - c), me).wait_recv()
        for cp in first + passed:
            cp.wait_send()
        for cp in local:
            cp.wait()

    return pl.pallas_call(
        body, name=name,
        out_shape=[jax.ShapeDtypeStruct((N_DEV,) + s.shape, s.dtype) for s in shards],
        in_specs=[_hbm()] * (n + len(given)), out_specs=[_hbm()] * n,
        input_output_aliases={n + i: j for i, j in enumerate(given)},
        scratch_shapes=[pltpu.SemaphoreType.DMA((7 * n,)), pltpu.SemaphoreType.DMA((7 * n,)),
                        pltpu.SemaphoreType.DMA((n,))],
    )(*shards, *[placed[j] for j in given])


class _Part(NamedTuple):
    src: jax.Array
    scatter: bool
    rows: int
    land: int
    off: int


class _Started(NamedTuple):
    send_sems: jax.Array
    recv_sems: jax.Array
    thru: tuple
    token: jax.Array
    parts: tuple


def _exchange_copies(srcs, lands, send_sems, recv_sems, parts):
    n = len(parts)
    x, y, c = lax.axis_index("x"), lax.axis_index("y"), lax.axis_index("c")
    me = 4 * x + 2 * y + c

    def block(j, dev):
        p = parts[j]
        return srcs[j].at[pl.ds(pl.multiple_of(dev * p.rows, SUBLANE), p.rows), :] if p.scatter else srcs[j]

    def slot(j, index):
        p = parts[j]
        return lands[p.land].at[index, pl.ds(p.off, p.rows), :]

    sends, recvs = [], []
    for k, flip in enumerate(FLIPS):
        px, py, pc = _peer(x, y, c, flip)
        peer = 4 * px + 2 * py + pc
        for j in range(n):
            sems = dict(send_sem=send_sems.at[k * n + j], recv_sem=recv_sems.at[k * n + j],
                        device_id=(px, py, pc), device_id_type=MESH)
            to, got = (k, k) if parts[j].scatter else (me, peer)
            sends.append(pltpu.make_async_remote_copy(src_ref=block(j, peer), dst_ref=slot(j, to), **sems))
            recvs.append(pltpu.make_async_remote_copy(src_ref=block(j, peer), dst_ref=slot(j, got), **sems))
    return sends, recvs


def _exchange_start(parts, lands, name, after=None):
    n, nl = len(parts), len(lands)
    n_in = n + nl + (after is not None)

    def body(*refs):
        srcs, land_refs = refs[:n], refs[n:n + nl]
        send_sems, recv_sems = refs[n_in], refs[n_in + 1]
        token = refs[n_in + 2 + n + nl]
        sends, _ = _exchange_copies(srcs, land_refs, send_sems, recv_sems, parts)
        for cp in sends:
            cp.start()
        token[...] = jnp.zeros_like(token)

    hbm = pl.BlockSpec(memory_space=pltpu.HBM)
    sem = pl.BlockSpec(memory_space=pltpu.SEMAPHORE)
    fresh = lambda s: lax.empty(s.shape, s.dtype) if isinstance(s, jax.ShapeDtypeStruct) else s
    args = [pltpu.with_memory_space_constraint(p.src, pltpu.HBM) for p in parts]
    args += [pltpu.with_memory_space_constraint(fresh(s), pltpu.HBM) for s in lands]
    args += [] if after is None else [after]
    out = pl.pallas_call(
        body, name=name,
        out_shape=(pltpu.SemaphoreType.DMA((7 * n,)), pltpu.SemaphoreType.DMA((7 * n,)),
                   *[pltpu.HBM(a.shape, a.dtype) for a in args[:n + nl]], jax.ShapeDtypeStruct((SUBLANE, LANE), F32)),
        in_specs=[hbm] * (n + nl) + [_hbm()] * (after is not None),
        out_specs=(sem, sem, *[hbm] * (n + nl), _whole()),
        input_output_aliases={j: 2 + j for j in range(n + nl)},
        compiler_params=pltpu.CompilerParams(has_side_effects=pltpu.SideEffectType.DATAFLOW_SIDE_EFFECTING),
    )(*args)
    return _Started(out[0], out[1], tuple(out[2:2 + n + nl]), out[2 + n + nl], tuple(parts))


def _exchange_wait(started, name, after):
    parts = started.parts
    n, nl = len(parts), len(started.thru) - len(parts)

    def body(*refs):
        srcs, land_refs = refs[:n], refs[n:n + nl]
        send_sems, recv_sems = refs[n + nl], refs[n + nl + 1]
        sends, recvs = _exchange_copies(srcs, land_refs, send_sems, recv_sems, parts)
        for cp in sends:
            cp.wait_send()
        for cp in recvs:
            cp.wait_recv()

    hbm = pl.BlockSpec(memory_space=pltpu.HBM)
    sem = pl.BlockSpec(memory_space=pltpu.SEMAPHORE)
    out = pl.pallas_call(
        body, name=name,
        out_shape=tuple(pltpu.HBM(a.shape, a.dtype) for a in started.thru),
        in_specs=[hbm] * (n + nl) + [sem, sem, _hbm()], out_specs=tuple([hbm] * (n + nl)),
        input_output_aliases={j: j for j in range(n + nl)},
        compiler_params=pltpu.CompilerParams(has_side_effects=pltpu.SideEffectType.DATAFLOW_SIDE_EFFECTING),
    )(*started.thru, started.send_sems, started.recv_sems, after)
    return list(out[:n]), list(out[n:])


def _fwd_in(x, g1, b_in, slab, ts, deps):
    n_tok, dm = x.shape
    rows, off, _ = _layout(dm)
    width = 7 * dm

    def body(x_ref, g1_ref, b_ref, slab_ref, proj_ref, p_ref, u_ref, h_ref, w_v, sems):
        copies = _weight_copies(slab_ref, off["win"], rows["win"], w_v, sems, 0)
        _on_first_step(copies, "start")
        _on_first_step(copies, "wait")
        xv = x_ref[...]
        h = (xv * _rms(xv) * g1_ref[...]).astype(BF16)
        h_ref[...] = h
        cols = []
        for j in range(7):
            pj = _dot_nt(h, w_v[pl.ds(j * dm, dm), :]) + b_ref[:, j * dm:(j + 1) * dm]
            proj_ref[:, j * dm:(j + 1) * dm] = pj.astype(proj_ref.dtype)
            if 1 <= j <= 4:
                cols.append(pj)
            if j == 2:
                p_ref[...] = cols[0] * cols[1]
            if j == 4:
                u_ref[...] = cols[2] * _sigmoid(cols[3])

    return pl.pallas_call(
        _after(body, deps), name="fwd_in", grid=(n_tok // ts,),
        in_specs=[_whole()] * len(deps) + [_rows(ts, dm), _whole(), _whole(), _hbm()],
        out_specs=[_rows(ts, width), _rows(ts, dm), _rows(ts, dm), _rows(ts, dm)],
        out_shape=[jax.ShapeDtypeStruct((n_tok, width), BF16), jax.ShapeDtypeStruct((n_tok, dm), F32),
                   jax.ShapeDtypeStruct((n_tok, dm), F32), jax.ShapeDtypeStruct((n_tok, dm), BF16)],
        scratch_shapes=[pltpu.VMEM((width, dm), BF16), pltpu.SemaphoreType.DMA((N_DEV,))],
        compiler_params=_params(56),
    )(*deps, x, g1, b_in, slab)


def _fwd_mix(p, u, proj, x, caw, cab, cbw, cbb, lng, lnb, g1post, slab, ts):
    n_tok, dm = x.shape
    rows, off, _ = _layout(dm)
    n_steps = n_tok // ts

    def body(p_ref, p_prev, p_next, u_ref, u_prev, u_next, bg_ref, za_ref, zb_ref, x_ref,
             caw_ref, cab_ref, cbw_ref, cbb_ref, lng_ref, lnb_ref, g1p_ref, slab_ref,
             va_ref, vb_ref, ya_ref, yb_ref, qa_ref, sb_ref, mg_ref, mix_ref, x1_ref,
             wa_v, wb_v, wo_v, tap_a, tap_b, sems, rolled_a, rolled_b):
        i = pl.program_id(0)
        copies = (_weight_copies(slab_ref, off["wa"], rows["wa"], wa_v, sems, 0)
                  + _weight_copies(slab_ref, off["wb"], rows["wb"], wb_v, sems, N_DEV)
                  + _weight_copies(slab_ref, off["wo"], rows["wo"], wo_v, sems, 2 * N_DEV))
        _on_first_step(copies, "start")

        @pl.when(i == 0)
        def _():
            _broadcast_taps(caw_ref, tap_a, CONV_A)
            _broadcast_taps(cbw_ref, tap_b, CONV_B)

        def emit_a(r0, lanes, acc):
            va_ref[pl.ds(r0, acc.shape[0]), lanes] = acc + cab_ref[:, lanes]

        def emit_b(r0, lanes, acc):
            vb_ref[pl.ds(r0, acc.shape[0]), lanes] = acc + cbb_ref[:, lanes]

        _conv_tile(_with_halos(p_ref, p_prev, p_next, i, n_steps), tap_a, _fwd_starts(CONV_A), ts, dm, emit_a,
                   rolled_a)
        _conv_tile(_with_halos(u_ref, u_prev, u_next, i, n_steps), tap_b, _fwd_starts(CONV_B), ts, dm, emit_b,
                   rolled_b)
        _on_first_step(copies, "wait")

        qa = (bg_ref[...].astype(F32) * va_ref[...]).astype(BF16)
        qa_ref[...] = qa
        ya = _dot(qa, wa_v[...])
        vb = vb_ref[...]
        xc = vb - jnp.mean(vb, axis=-1, keepdims=True)
        rstd = lax.rsqrt(jnp.mean(xc * xc, axis=-1, keepdims=True) + LN_EPS)
        ln = xc * rstd * lng_ref[...] + lnb_ref[...]
        sb = (ln * _sigmoid(ln)).astype(BF16)
        sb_ref[...] = sb
        yb = _dot(sb, wb_v[...])
        ya_ref[...] = ya.astype(BF16)
        yb_ref[...] = yb.astype(BF16)
        merged = (_sigmoid(za_ref[...].astype(F32)) * ya + _sigmoid(zb_ref[...].astype(F32)) * yb).astype(BF16)
        mg_ref[...] = merged
        mix = _dot(merged, wo_v[...])
        mix_ref[...] = mix
        x1_ref[...] = x_ref[...] + mix * _rms(mix) * g1p_ref[...]

    tok = lambda dt: jax.ShapeDtypeStruct((n_tok, dm), dt)
    return pl.pallas_call(
        body, name="fwd_mix", grid=(n_steps,),
        in_specs=(_halo_specs(ts, dm, n_tok) + _halo_specs(ts, dm, n_tok)
                  + [_rows(ts, dm, 0), _rows(ts, dm, 5), _rows(ts, dm, 6), _rows(ts, dm)]
                  + [_whole()] * 7 + [_hbm()]),
        out_specs=[_rows(ts, dm)] * 9,
        out_shape=[tok(F32), tok(F32), tok(BF16), tok(BF16), tok(BF16), tok(BF16), tok(BF16), tok(F32), tok(F32)],
        scratch_shapes=[pltpu.VMEM((dm, dm), BF16), pltpu.VMEM((dm, dm), BF16), pltpu.VMEM((dm, dm), BF16),
                        pltpu.VMEM((CONV_A, SUBLANE, dm), F32), pltpu.VMEM((CONV_B, SUBLANE, dm), F32),
                        pltpu.SemaphoreType.DMA((3 * N_DEV,)),
                        pltpu.VMEM((2, SUBLANE, ts + 2 * HALO, LANE), F32),
                        pltpu.VMEM((2, SUBLANE, ts + 2 * HALO, LANE), F32)],
        compiler_params=_params(48),
    )(p, p, p, u, u, u, proj, proj, proj, x, caw, cab, cbw, cbb, lng, lnb, g1post, slab)


def _mlp_fwd_bwd(x1, mix, tgt, g1post, g2pre, g2post, slab, ts):
    n_tok, dm = x1.shape
    rows, off, _ = _layout(dm)
    ff = 4 * dm

    def body(x1_ref, mix_ref, t_ref, g1p_ref, g2pre_ref, g2post_ref, slab_ref,
             f_ref, df1_ref, h2_ref, df2_ref, dmix_ref, dx1_ref, small_ref, w1_v, w2_v, relu_v, sems):
        w1_copies = _weight_copies(slab_ref, off["w1"], rows["w1"], w1_v, sems, 0)
        w2_copies = _weight_copies(slab_ref, off["w2"], rows["w2"], w2_v, sems, N_DEV)
        _on_first_step(w1_copies + w2_copies, "start")

        @pl.when(pl.program_id(0) == 0)
        def _():
            small_ref[...] = jnp.zeros_like(small_ref)

        _on_first_step(w1_copies + w2_copies, "wait")
        x1v = x1_ref[...]
        r3 = _rms(x1v)
        g2pre = g2pre_ref[...]
        h2 = (x1v * r3 * g2pre).astype(BF16)
        h2_ref[...] = h2
        for c in range(4):
            blk = pl.ds(c * dm, dm)
            relu = jnp.maximum(_dot_nt(h2, w1_v[blk, :]), 0.0)
            relu_v[:, c * dm:(c + 1) * dm] = relu
            f_ref[:, c * dm:(c + 1) * dm] = (relu * relu).astype(BF16)
        f2 = _dot(f_ref[...], w2_v[...])
        r4 = _rms(f2)
        g2post = g2post_ref[...]
        err = x1v + f2 * r4 * g2post - t_ref[...]
        dy = err * (1.0 / dm)
        small_ref[3:4, :] += _colsum(err * err)
        small_ref[0:1, :] += _colsum(dy * f2 * r4)
        df2 = _rms_bwd(dy, f2, r4, g2post).astype(BF16)
        df2_ref[...] = df2
        for c in range(4):
            blk = pl.ds(c * dm, dm)
            df1 = (_dot_nt(df2, w2_v[blk, :]) * (2.0 * relu_v[:, c * dm:(c + 1) * dm])).astype(BF16)
            df1_ref[:, c * dm:(c + 1) * dm] = df1
        dh2 = _dot(df1_ref[...], w1_v[...])
        small_ref[1:2, :] += _colsum(dh2 * x1v * r3)
        dx1 = dy + _rms_bwd(dh2, x1v, r3, g2pre)
        dx1_ref[...] = dx1
        mixv = mix_ref[...]
        r2 = _rms(mixv)
        small_ref[2:3, :] += _colsum(dx1 * mixv * r2)
        dmix_ref[...] = _rms_bwd(dx1, mixv, r2, g1p_ref[...]).astype(BF16)

    tok = lambda w, dt: jax.ShapeDtypeStruct((n_tok, w), dt)
    return pl.pallas_call(
        body, name="mlp_fwd_bwd", grid=(n_tok // ts,),
        in_specs=[_rows(ts, dm)] * 3 + [_whole()] * 3 + [_hbm()],
        out_specs=[_rows(ts, ff), _rows(ts, ff), _rows(ts, dm), _rows(ts, dm), _rows(ts, dm), _rows(ts, dm),
                   pl.BlockSpec((SUBLANE, dm), lambda i: (0, 0))],
        out_shape=[tok(ff, BF16), tok(ff, BF16), tok(dm, BF16), tok(dm, BF16), tok(dm, BF16), tok(dm, F32),
                   jax.ShapeDtypeStruct((SUBLANE, dm), F32)],
        scratch_shapes=[pltpu.VMEM((ff, dm), BF16), pltpu.VMEM((ff, dm), BF16), pltpu.VMEM((ts, ff), F32),
                        pltpu.SemaphoreType.DMA((2 * N_DEV,))],
        compiler_params=_params(56),
    )(x1, mix, tgt, g1post, g2pre, g2post, slab)


def _bwd_mix(dmix, ya, yb, proj, va, vb, lng, lnb, slab, ts, deps):
    n_tok, dm = dmix.shape
    rows, off, _ = _layout(dm)
    n_steps = n_tok // ts

    def body(dmix_ref, ya_ref, yb_ref, bg_ref, za_ref, zb_ref, va_ref, vb_ref, lng_ref, lnb_ref, slab_ref,
             dpa_ref, dya_ref, dyb_ref, dva_ref, dvb_ref, small_ref, wa_v, wb_v, wo_v, sems):
        wo_copies = _weight_copies(slab_ref, off["wo"], rows["wo"], wo_v, sems, 2 * N_DEV)
        ab_copies = (_weight_copies(slab_ref, off["wa"], rows["wa"], wa_v, sems, 0)
                     + _weight_copies(slab_ref, off["wb"], rows["wb"], wb_v, sems, N_DEV))
        _on_first_step(wo_copies + ab_copies, "start")

        @pl.when(pl.program_id(0) == 0)
        def _():
            small_ref[...] = jnp.zeros_like(small_ref)

        _on_first_step(wo_copies, "wait")
        dmerged = _dot_nt(dmix_ref[...], wo_v[...])
        _on_first_step(ab_copies, "wait")
        sa = _sigmoid(za_ref[...].astype(F32))
        sg = _sigmoid(zb_ref[...].astype(F32))
        dza = dmerged * ya_ref[...].astype(F32) * sa * (1.0 - sa)
        dzb = dmerged * yb_ref[...].astype(F32) * sg * (1.0 - sg)
        dpa_ref[:, dm:2 * dm] = dza.astype(BF16)
        dpa_ref[:, 2 * dm:3 * dm] = dzb.astype(BF16)
        small_ref[5:6, :] += _colsum(dza)
        small_ref[6:7, :] += _colsum(dzb)

        dya = (dmerged * sa).astype(BF16)
        dya_ref[...] = dya
        dqa = _dot_nt(dya, wa_v[...])
        dbg = dqa * va_ref[...]
        dpa_ref[:, 0:dm] = dbg.astype(BF16)
        small_ref[4:5, :] += _colsum(dbg)
        dva = dqa * bg_ref[...].astype(F32)
        dva_ref[...] = dva
        small_ref[2:3, :] += _colsum(dva)

        dyb = (dmerged * sg).astype(BF16)
        dyb_ref[...] = dyb
        dsb = _dot_nt(dyb, wb_v[...])
        vb = vb_ref[...]
        xc = vb - jnp.mean(vb, axis=-1, keepdims=True)
        rstd = lax.rsqrt(jnp.mean(xc * xc, axis=-1, keepdims=True) + LN_EPS)
        nrm = xc * rstd
        lng_v = lng_ref[...]
        ln = nrm * lng_v + lnb_ref[...]
        sl = _sigmoid(ln)
        dln = dsb * (sl * (1.0 + ln * (1.0 - sl)))
        small_ref[0:1, :] += _colsum(dln * nrm)
        small_ref[1:2, :] += _colsum(dln)
        dn = dln * lng_v
        dvb = rstd * (dn - jnp.mean(dn, axis=-1, keepdims=True)
                      - nrm * jnp.mean(dn * nrm, axis=-1, keepdims=True))
        dvb_ref[...] = dvb
        small_ref[3:4, :] += _colsum(dvb)

    tok = lambda w, dt: jax.ShapeDtypeStruct((n_tok, w), dt)
    return pl.pallas_call(
        _after(body, deps), name="bwd_mix", grid=(n_steps,),
        in_specs=([_whole()] * len(deps) + [_rows(ts, dm)] * 3
                  + [_rows(ts, dm, 0), _rows(ts, dm, 5), _rows(ts, dm, 6)]
                  + [_rows(ts, dm)] * 2 + [_whole()] * 2 + [_hbm()]),
        out_specs=[_rows(ts, 3 * dm), _rows(ts, dm), _rows(ts, dm), _rows(ts, dm), _rows(ts, dm),
                   pl.BlockSpec((SUBLANE, dm), lambda i: (0, 0))],
        out_shape=[tok(3 * dm, BF16), tok(dm, BF16), tok(dm, BF16), tok(dm, F32), tok(dm, F32),
                   jax.ShapeDtypeStruct((SUBLANE, dm), F32)],
        scratch_shapes=[pltpu.VMEM((dm, dm), BF16), pltpu.VMEM((dm, dm), BF16), pltpu.VMEM((dm, dm), BF16),
                        pltpu.SemaphoreType.DMA((3 * N_DEV,))],
        compiler_params=_params(48),
    )(*deps, dmix, ya, yb, proj, proj, proj, va, vb, lng, lnb, slab)


def _bwd_conv(dva, dvb, p, u, proj, dpa, caw, cbw, ts, deps):
    n_tok, dm = dva.shape
    n_steps = n_tok // ts
    small_rows = 40

    def body(dva_ref, dva_prev, dva_next, dvb_ref, dvb_prev, dvb_next, p_ref, u_ref,
             cg_ref, ha_ref, a_ref, g_ref, dpa_ref, caw_ref, cbw_ref,
             dproj_ref, small_ref,
             dp_v, du_v, tap_a, tap_b, gwa_v, gwb_v):
        i = pl.program_id(0)

        @pl.when(i == 0)
        def _():
            _broadcast_taps(caw_ref, tap_a, CONV_A)
            _broadcast_taps(cbw_ref, tap_b, CONV_B)
            small_ref[...] = jnp.zeros_like(small_ref)
            gwa_v[...] = jnp.zeros_like(gwa_v)
            gwb_v[...] = jnp.zeros_like(gwb_v)

        def emit_dp(r0, lanes, acc):
            dp_v[pl.ds(r0, acc.shape[0]), lanes] = acc

        def emit_du(r0, lanes, acc):
            du_v[pl.ds(r0, acc.shape[0]), lanes] = acc

        _conv_bwd_tile(_with_halos(dva_ref, dva_prev, dva_next, i, n_steps), p_ref, tap_a, gwa_v, CONV_A, ts, dm,
                       emit_dp)
        _conv_bwd_tile(_with_halos(dvb_ref, dvb_prev, dvb_next, i, n_steps), u_ref, tap_b, gwb_v, CONV_B, ts, dm,
                       emit_du)

        dp = dp_v[...]
        dcg = dp * ha_ref[...].astype(F32)
        dha = dp * cg_ref[...].astype(F32)
        du = du_v[...]
        sg = _sigmoid(g_ref[...].astype(F32))
        da = du * sg
        dg = du * a_ref[...].astype(F32) * sg * (1.0 - sg)
        dproj_ref[:, 0:dm] = dpa_ref[:, 0:dm]
        dproj_ref[:, dm:2 * dm] = dcg.astype(BF16)
        dproj_ref[:, 2 * dm:3 * dm] = dha.astype(BF16)
        dproj_ref[:, 3 * dm:4 * dm] = da.astype(BF16)
        dproj_ref[:, 4 * dm:5 * dm] = dg.astype(BF16)
        dproj_ref[:, 5 * dm:7 * dm] = dpa_ref[:, dm:3 * dm]
        small_ref[3:4, :] += _colsum(dcg)
        small_ref[4:5, :] += _colsum(dha)
        small_ref[5:6, :] += _colsum(da)
        small_ref[6:7, :] += _colsum(dg)

        @pl.when(i == n_steps - 1)
        def _():
            for k in range(CONV_A):
                small_ref[k:k + 1, :] = _colsum(gwa_v[k])
            for k in range(CONV_B):
                small_ref[SUBLANE + k:SUBLANE + k + 1, :] = _colsum(gwb_v[k])

    return pl.pallas_call(
        _after(body, deps), name="bwd_conv", grid=(n_steps,),
        in_specs=([_whole()] * len(deps) + _halo_specs(ts, dm, n_tok) * 2 + [_rows(ts, dm)] * 2
                  + [_rows(ts, dm, 1), _rows(ts, dm, 2), _rows(ts, dm, 3), _rows(ts, dm, 4), _rows(ts, 3 * dm)]
                  + [_whole()] * 2),
        out_specs=[_rows(ts, 7 * dm), pl.BlockSpec((small_rows, dm), lambda i: (0, 0))],
        out_shape=[jax.ShapeDtypeStruct((n_tok, 7 * dm), BF16), jax.ShapeDtypeStruct((small_rows, dm), F32)],
        scratch_shapes=[pltpu.VMEM((ts, dm), F32), pltpu.VMEM((ts, dm), F32),
                        pltpu.VMEM((CONV_A, SUBLANE, dm), F32), pltpu.VMEM((CONV_B, SUBLANE, dm), F32),
                        pltpu.VMEM((CONV_A, SUBLANE, dm), F32), pltpu.VMEM((CONV_B, SUBLANE, dm), F32)],
        compiler_params=_params(48),
    )(*deps, dva, dva, dva, dvb, dvb, dvb, p, u, proj, proj, proj, proj, dpa, caw, cbw)


def _bwd_in(dproj, x, dx1, g1, slab, ts, deps):
    n_tok, dm = x.shape
    rows, off, _ = _layout(dm)
    width = 7 * dm

    def body(dproj_ref, x_ref, dx1_ref, g1_ref, slab_ref, gx_ref, small_ref, w_v, sems):
        copies = _weight_copies(slab_ref, off["win"], rows["win"], w_v, sems, 0)
        _on_first_step(copies, "start")

        @pl.when(pl.program_id(0) == 0)
        def _():
            small_ref[...] = jnp.zeros_like(small_ref)

        _on_first_step(copies, "wait")
        dh = _dot(dproj_ref[...], w_v[...])
        xv = x_ref[...]
        r1 = _rms(xv)
        small_ref[0:1, :] += _colsum(dh * xv * r1)
        gx_ref[...] = dx1_ref[...] + _rms_bwd(dh, xv, r1, g1_ref[...])

    return pl.pallas_call(
        _after(body, deps), name="bwd_in", grid=(n_tok // ts,),
        in_specs=[_whole()] * len(deps) + [_rows(ts, width), _rows(ts, dm), _rows(ts, dm), _whole(), _hbm()],
        out_specs=[_rows(ts, dm), pl.BlockSpec((SUBLANE, dm), lambda i: (0, 0))],
        out_shape=[jax.ShapeDtypeStruct((n_tok, dm), F32), jax.ShapeDtypeStruct((SUBLANE, dm), F32)],
        scratch_shapes=[pltpu.VMEM((width, dm), BF16), pltpu.SemaphoreType.DMA((N_DEV,))],
        compiler_params=_params(56),
    )(*deps, dproj, x, dx1, g1, slab)


def _wgrad(a, b, name, tm, tk, out_dtype):
    n_tok, m = a.shape
    n = b.shape[1]
    k_steps = n_tok // tk

    def body(a_ref, b_ref, o_ref, acc_v):
        k = pl.program_id(1)

        @pl.when(k == 0)
        def _():
            acc_v[...] = jnp.zeros_like(acc_v)

        acc_v[...] += _dot_tn(a_ref[...], b_ref[...])

        @pl.when(k == k_steps - 1)
        def _():
            o_ref[...] = acc_v[...].astype(o_ref.dtype)

    return pl.pallas_call(
        body, name=name, grid=(m // tm, k_steps),
        in_specs=[pl.BlockSpec((tk, tm), lambda i, k: (k, i)), pl.BlockSpec((tk, n), lambda i, k: (k, 0))],
        out_specs=pl.BlockSpec((tm, n), lambda i, k: (i, 0)),
        out_shape=pltpu.HBM((m, n), out_dtype),
        scratch_shapes=[pltpu.VMEM((tm, n), F32)],
        compiler_params=pltpu.CompilerParams(dimension_semantics=("arbitrary", "arbitrary"),
                                             vmem_limit_bytes=52 * MIB),
    )(a, b)


def _adamw(w, g, m, v):
    m = ADAM_B1 * m + (1.0 - ADAM_B1) * g
    v = ADAM_B2 * v + (1.0 - ADAM_B2) * (g * g)
    m_hat = m / (1.0 - ADAM_B1 ** ADAM_STEP)
    v_hat = v / (1.0 - ADAM_B2 ** ADAM_STEP)
    delta = -ADAM_LR * (m_hat / (jnp.sqrt(v_hat) + ADAM_EPS) + ADAM_WD * w)
    return delta, m, v


def _adam_big(recv, part, me, off, rows, w, m, v, transpose, name, tr):
    dm = recv.shape[2]
    per = rows // tr

    def body(me_ref, own_ref, r_ref, w_ref, m_ref, v_ref, g_ref, d_ref, mo_ref, vo_ref):
        g = own_ref[...].astype(F32)
        for k in range(len(FLIPS)):
            g = g + r_ref[k].astype(F32)
        if transpose:
            g = g.T
        delta, m_new, v_new = _adamw(w_ref[...], g, m_ref[...], v_ref[...])
        g_ref[...] = g
        d_ref[...] = delta
        mo_ref[...] = m_new
        vo_ref[...] = v_new

    if transpose:
        blk = pl.BlockSpec((dm, tr), lambda i, me_ref: (0, i))
    else:
        blk = pl.BlockSpec((tr, dm), lambda i, me_ref: (i, 0))
    first = off // tr
    return pl.pallas_call(
        body, name=name,
        grid_spec=pltpu.PrefetchScalarGridSpec(
            num_scalar_prefetch=1, grid=(per,),
            in_specs=[pl.BlockSpec((tr, dm), lambda i, me_ref: (me_ref[0] * per + i, 0)),
                      pl.BlockSpec((len(FLIPS), tr, dm), lambda i, me_ref: (0, first + i, 0)), blk, blk, blk],
            out_specs=[blk] * 4),
        out_shape=[jax.ShapeDtypeStruct(w.shape, F32)] * 4,
        compiler_params=_params(32),
    )(me, *[_in_hbm(a) for a in (part, recv, w, m, v)])


LOSS_ROW = 15
CONV_A_ROW = 16
CONV_B_ROW = 24


def _adam_small(recv_small, recv_last, me, params, d_model):
    n = len(params)
    cw = d_model // N_DEV

    def body(me_ref, r_ref, rc_ref, l_ref, *refs):
        ins, loss_ref, outs = refs[:3 * n], refs[3 * n], refs[3 * n + 1:3 * n + 1 + 4 * n]
        g_v, gc_v, last_v = refs[3 * n + 1 + 4 * n:]
        g, gc, last = r_ref[0], rc_ref[0], l_ref[0]
        for d in range(1, N_DEV):
            g, gc, last = g + r_ref[d], gc + rc_ref[d], last + l_ref[d]
        g_v[...], gc_v[...], last_v[...] = g, gc, last
        loss_ref[...] = (0.5 / d_model) * jnp.sum(g_v[LOSS_ROW:LOSS_ROW + 1, :], axis=-1, keepdims=True)
        for j, (row0, own_columns, (w, _, _)) in enumerate(params):
            w_ref, m_ref, v_ref = ins[3 * j:3 * j + 3]
            source = gc_v if own_columns else (last_v if row0 == 0 else g_v)
            grad = source[row0:row0 + w.shape[0], :]
            delta, m_new, v_new = _adamw(w_ref[...], grad, m_ref[...], v_ref[...])
            for ref, val in zip(outs[4 * j:4 * j + 4], (grad, delta, m_new, v_new)):
                ref[...] = val

    full = lambda shape: pl.BlockSpec(shape, lambda i, me_ref: (0,) * len(shape))
    stack_rows = recv_small.shape[1]
    flat = [a for _, _, triple in params for a in triple]
    shapes = [w.shape for _, _, (w, _, _) in params for _ in range(4)]
    out = pl.pallas_call(
        body, name="adam_small",
        grid_spec=pltpu.PrefetchScalarGridSpec(
            num_scalar_prefetch=1, grid=(1,),
            in_specs=[full(recv_small.shape),
                      pl.BlockSpec((N_DEV, stack_rows, cw), lambda i, me_ref: (0, 0, me_ref[0])),
                      full(recv_last.shape)] + [full(a.shape) for a in flat],
            out_specs=[full((1, 1))] + [full(s) for s in shapes],
            scratch_shapes=[pltpu.VMEM((stack_rows, d_model), F32), pltpu.VMEM((stack_rows, cw), F32),
                            pltpu.VMEM(recv_last.shape[1:], F32)]),
        out_shape=[jax.ShapeDtypeStruct((1, 1), F32)] + [jax.ShapeDtypeStruct(s, F32) for s in shapes],
    )(me, *[_in_hbm(a) for a in (recv_small, recv_small, recv_last, *flat)])
    return out[0], [tuple(out[1 + 4 * j:5 + 4 * j]) for j in range(n)]


def _tile(n_tok, want):
    return min(want, n_tok)


def kernel(x, norm1_pre_g, w_in, b_in, conv_a_w, conv_a_b, w_a_out, conv_b_w, conv_b_b, ln_b_g, ln_b_b, w_b_out, w_o, norm1_post_g, norm2_pre_g, w_mlp_in, w_mlp_out, norm2_post_g, loss_target, m_norm1_pre_g, m_w_in, m_b_in, m_conv_a_w, m_conv_a_b, m_w_a_out, m_conv_b_w, m_conv_b_b, m_ln_b_g, m_ln_b_b, m_w_b_out, m_w_o, m_norm1_post_g, m_norm2_pre_g, m_w_mlp_in, m_w_mlp_out, m_norm2_post_g, v_norm1_pre_g, v_w_in, v_b_in, v_conv_a_w, v_conv_a_b, v_w_a_out, v_conv_b_w, v_conv_b_b, v_ln_b_g, v_ln_b_b, v_w_b_out, v_w_o, v_norm1_post_g, v_norm2_pre_g, v_w_mlp_in, v_w_mlp_out, v_norm2_post_g):
    n_tok, dm = x.shape[1], x.shape[2]
    rows, off, slab_rows = _layout(dm)
    cw = dm // N_DEV
    xs = x.reshape(n_tok, dm)
    tgt = loss_target.reshape(n_tok, dm)
    row = lambda vec: vec.reshape(1, -1)
    scattered = lambda group: jax.ShapeDtypeStruct((len(FLIPS), slab_rows[group], dm), BF16)
    tm, tk = min(dm, 1024), _tile(n_tok, 2048)
    me = (4 * lax.axis_index("x") + 2 * lax.axis_index("y") + lax.axis_index("c")).astype(jnp.int32).reshape(1)

    conv_own = jnp.concatenate([conv_a_w, jnp.zeros((SUBLANE - CONV_A, cw), F32), conv_b_w,
                                jnp.zeros((1, cw), F32)], axis=0)
    own_in, land_in = _place_cast([(w_in, True)], me, "place_w_in")
    slab_in, conv_all = _all_gather_two_level([own_in, conv_own], [land_in, None], "gather_w_in")
    conv_full = conv_all.transpose(1, 0, 2).reshape(conv_own.shape[0], dm)
    caw, cbw = conv_full[0:CONV_A], conv_full[SUBLANE:SUBLANE + CONV_B]
    own_abo, land_abo = _place_cast([(w_a_out, False), (w_b_out, False), (w_o, False)], me, "place_abo")
    own_mlp, land_mlp = _place_cast([(w_mlp_in, True), (w_mlp_out, False)], me, "place_mlp")
    ag_abo = _exchange_start([_Part(own_abo, False, slab_rows["abo"], 0, 0)], [land_abo],
                             "gather_abo_start", after=slab_in)
    ag_mlp = _exchange_start([_Part(own_mlp, False, slab_rows["mlp"], 0, 0)], [land_mlp],
                             "gather_mlp_start", after=ag_abo.token)

    proj, p, u, h = _fwd_in(xs, row(norm1_pre_g), row(b_in), slab_in, _tile(n_tok, 512),
                            [ag_abo.token, ag_mlp.token])
    _, (slab_abo,) = _exchange_wait(ag_abo, "gather_abo_wait", after=proj)
    va, vb, ya, yb, qa, sb, merged, mix, x1 = _fwd_mix(
        p, u, proj, xs, caw, row(conv_a_b), cbw, row(conv_b_b), row(ln_b_g), row(ln_b_b), row(norm1_post_g),
        slab_abo, _tile(n_tok, 256))
    _, (slab_mlp,) = _exchange_wait(ag_mlp, "gather_mlp_wait", after=x1)
    f, df1, h2, df2, dmix, dx1, small_mlp = _mlp_fwd_bwd(
        x1, mix, tgt, row(norm1_post_g), row(norm2_pre_g), row(norm2_post_g), slab_mlp, _tile(n_tok, 256))

    rs_mlp = _exchange_start(
        [_Part(_wgrad(df1, h2, "wgrad_mlp_in", 2 * tm, tk, BF16), True, rows["w1"], 0, off["w1"]),
         _Part(_wgrad(f, df2, "wgrad_mlp_out", 2 * tm, tk, BF16), True, rows["w2"], 0, off["w2"])],
        [scattered("mlp")], "scatter_mlp_start")
    dpa, dya, dyb, dva, dvb, small_mix = _bwd_mix(
        dmix, ya, yb, proj, va, vb, row(ln_b_g), row(ln_b_b), slab_abo, _tile(n_tok, 256), [rs_mlp.token])
    rs_abo = _exchange_start(
        [_Part(_wgrad(qa, dya, "wgrad_a_out", tm, tk, BF16), True, rows["wa"], 0, off["wa"]),
         _Part(_wgrad(sb, dyb, "wgrad_b_out", tm, tk, BF16), True, rows["wb"], 0, off["wb"]),
         _Part(_wgrad(merged, dmix, "wgrad_o", tm, tk, BF16), True, rows["wo"], 0, off["wo"])],
        [scattered("abo")], "scatter_abo_start")
    dproj, small_conv = _bwd_conv(dva, dvb, p, u, proj, dpa, caw, cbw, _tile(n_tok, 256), [rs_abo.token])

    zeros = lambda r: jnp.zeros((r, dm), F32)
    small = jnp.concatenate([
        zeros(1),
        small_mix[2:3],
        small_mix[3:4],
        small_mix[0:2],
        small_mlp[2:3],
        small_mlp[1:2],
        small_mlp[0:1],
        small_mix[4:5], small_conv[3:7], small_mix[5:7],
        small_mlp[3:4],
        small_conv[0:CONV_A], zeros(SUBLANE - CONV_A),
        small_conv[8:8 + CONV_B], zeros(1),
    ], axis=0)

    rs_in = _exchange_start(
        [_Part(_wgrad(dproj, h, "wgrad_in", 7 * tm // 4, tk, BF16), True, rows["win"], 0, off["win"]),
         _Part(small, False, small.shape[0], 1, 0)],
        [scattered("in"), _place_own(small, (N_DEV,) + small.shape, me, "place_small")], "scatter_in_start")
    grad_x, small_in = _bwd_in(dproj, xs, dx1, row(norm1_pre_g), slab_in, _tile(n_tok, 256), [rs_in.token])

    tr = min(LANE, rows["wa"])
    recv_last, = _all_gather([small_in], "gather_last")
    (g_w1, g_w2), (recv_mlp,) = _exchange_wait(rs_mlp, "scatter_mlp_wait", after=recv_last)
    (g_wa, g_wb, g_wo), (recv_abo,) = _exchange_wait(rs_abo, "scatter_abo_wait", after=recv_last)
    big = {
        "w_mlp_in": _adam_big(recv_mlp, g_w1, me, off["w1"], rows["w1"], w_mlp_in, m_w_mlp_in, v_w_mlp_in, True,
                              "adam_w_mlp_in", tr),
        "w_mlp_out": _adam_big(recv_mlp, g_w2, me, off["w2"], rows["w2"], w_mlp_out, m_w_mlp_out, v_w_mlp_out,
                               False, "adam_w_mlp_out", tr),
        "w_a_out": _adam_big(recv_abo, g_wa, me, off["wa"], rows["wa"], w_a_out, m_w_a_out, v_w_a_out, False,
                             "adam_w_a_out", tr),
        "w_b_out": _adam_big(recv_abo, g_wb, me, off["wb"], rows["wb"], w_b_out, m_w_b_out, v_w_b_out, False,
                             "adam_w_b_out", tr),
        "w_o": _adam_big(recv_abo, g_wo, me, off["wo"], rows["wo"], w_o, m_w_o, v_w_o, False, "adam_w_o", tr),
    }
    (g_win, _), (recv_in, recv_small) = _exchange_wait(rs_in, "scatter_in_wait", after=big["w_o"][3])
    big["w_in"] = _adam_big(recv_in, g_win, me, off["win"], rows["win"], w_in, m_w_in, v_w_in, True, "adam_w_in", tr)

    small_names = ("norm1_pre_g", "conv_a_b", "conv_b_b", "ln_b_g", "ln_b_b", "norm1_post_g", "norm2_pre_g",
                   "norm2_post_g")
    given = dict(
        norm1_pre_g=(norm1_pre_g, m_norm1_pre_g, v_norm1_pre_g), conv_a_b=(conv_a_b, m_conv_a_b, v_conv_a_b),
        conv_b_b=(conv_b_b, m_conv_b_b, v_conv_b_b), ln_b_g=(ln_b_g, m_ln_b_g, v_ln_b_g),
        ln_b_b=(ln_b_b, m_ln_b_b, v_ln_b_b), norm1_post_g=(norm1_post_g, m_norm1_post_g, v_norm1_post_g),
        norm2_pre_g=(norm2_pre_g, m_norm2_pre_g, v_norm2_pre_g),
        norm2_post_g=(norm2_post_g, m_norm2_post_g, v_norm2_post_g))
    params = [(j, False, tuple(row(a) for a in given[name])) for j, name in enumerate(small_names)]
    params.append((SUBLANE, False, tuple(a.reshape(7, dm) for a in (b_in, m_b_in, v_b_in))))
    params.append((CONV_A_ROW, True, (conv_a_w, m_conv_a_w, v_conv_a_w)))
    params.append((CONV_B_ROW, True, (conv_b_w, m_conv_b_w, v_conv_b_w)))
    loss, small_out = _adam_small(recv_small, recv_last, me, params, dm)
    small_leaves = {name: tuple(a.reshape(dm) for a in small_out[j]) for j, name in enumerate(small_names)}
    small_leaves["b_in"] = tuple(a.reshape(7 * dm) for a in small_out[len(small_names)])
    small_leaves["conv_a_w"] = small_out[len(small_names) + 1]
    small_leaves["conv_b_w"] = small_out[len(small_names) + 2]

    order = ("norm1_pre_g", "w_in", "b_in", "conv_a_w", "conv_a_b", "w_a_out", "conv_b_w", "conv_b_b", "ln_b_g",
             "ln_b_b", "w_b_out", "w_o", "norm1_post_g", "norm2_pre_g", "w_mlp_in", "w_mlp_out", "norm2_post_g")
    leaves = [big[name] if name in big else small_leaves[name] for name in order]
    grads, deltas, new_m, new_v = zip(*leaves)
    return (loss.reshape(()), grad_x.reshape(x.shape), *grads, *deltas, *new_m, *new_v)
```

```python
from typing import NamedTuple

import jax
import jax.numpy as jnp
from jax import lax
from jax.experimental import pallas as pl
from jax.experimental.pallas import tpu as pltpu

F32 = jnp.float32
BF16 = jnp.bfloat16

RMS_EPS = 1e-6
LN_EPS = 1e-5
ADAM_LR = 0.001
ADAM_B1 = 0.9
ADAM_B2 = 0.999
ADAM_EPS = 1e-08
ADAM_WD = 0.01
ADAM_STEP = 10

N_DEV = 8
CONV_A = 3
CONV_B = 31
LANE = 128
SUBLANE = 8
HALO = 16
FWD_CONV_ROWS = 32
BWD_CONV_ROWS = 64
MIB = 1 << 20
FLIPS = ((0, 0, 1), (0, 1, 0), (1, 0, 0), (0, 1, 1), (1, 0, 1), (1, 1, 0), (1, 1, 1))
MESH = pl.DeviceIdType.MESH


def _layout(d_model):
    e = d_model // N_DEV
    rows = {"win": 7 * e, "w1": 4 * e, "w2": 4 * e, "wa": e, "wb": e, "wo": e}
    off = {"win": 0, "w1": 0, "w2": 4 * e, "wa": 0, "wb": e, "wo": 2 * e}
    return rows, off, {"in": 7 * e, "mlp": 8 * e, "abo": 3 * e}


def _after(body, deps):
    def ordered(*refs):
        return body(*refs[len(deps):])
    return ordered


def _params(vmem_mib):
    return pltpu.CompilerParams(dimension_semantics=("arbitrary",), vmem_limit_bytes=vmem_mib * MIB)


def _whole():
    return pl.BlockSpec(memory_space=pltpu.VMEM)


def _hbm():
    return pl.BlockSpec(memory_space=pl.ANY)


def _in_hbm(a):
    return pltpu.with_memory_space_constraint(a, pltpu.HBM)


def _rows(ts, width, col=0):
    return pl.BlockSpec((ts, width), lambda i: (i, col))


def _halo_specs(ts, width, n_rows):
    per = ts // HALO
    last = n_rows // HALO - 1
    return [
        pl.BlockSpec((ts, width), lambda i: (i, 0)),
        pl.BlockSpec((HALO, width), lambda i: (jnp.maximum(i * per - 1, 0), 0)),
        pl.BlockSpec((HALO, width), lambda i: (jnp.minimum((i + 1) * per, last), 0)),
    ]


def _dot(a, b):
    return jnp.dot(a, b, preferred_element_type=F32)


def _dot_nt(a, b):
    return lax.dot_general(a, b, (((1,), (1,)), ((), ())), preferred_element_type=F32)


def _dot_tn(a, b):
    return lax.dot_general(a, b, (((0,), (0,)), ((), ())), preferred_element_type=F32)


def _rms(u):
    return lax.rsqrt(jnp.mean(u * u, axis=-1, keepdims=True) + RMS_EPS)


def _rms_bwd(dz, u, r, g):
    dzg = dz * g
    return r * dzg - u * (r * r * r) * jnp.mean(dzg * u, axis=-1, keepdims=True)


def _colsum(v):
    return jnp.sum(v, axis=0, keepdims=True)


def _sigmoid(v):
    return jax.nn.sigmoid(v)


def _weight_copies(slab_ref, off, rows, dst_ref, sems, first_sem):
    return [pltpu.make_async_copy(slab_ref.at[d, pl.ds(off, rows), :], dst_ref.at[pl.ds(d * rows, rows), :],
                                  sems.at[first_sem + d]) for d in range(N_DEV)]


def _on_first_step(copies, method):
    @pl.when(pl.program_id(0) == 0)
    def _():
        for cp in copies:
            getattr(cp, method)()


def _with_halos(main_ref, prev_ref, next_ref, i, n_steps):
    return (main_ref, jnp.where(i > 0, prev_ref[...], 0.0), jnp.where(i < n_steps - 1, next_ref[...], 0.0))


def _broadcast_taps(w_ref, wb_ref, n_taps):
    for k in range(n_taps):
        wb_ref[k] = jnp.broadcast_to(w_ref[k:k + 1, :], wb_ref.shape[1:])


def _conv_tile(tile, wb_ref, starts, ts, width, emit, rolled_ref, rows=FWD_CONV_ROWS):
    main_ref, prev, nxt = tile
    span = ts + 2 * HALO
    nv = rows // SUBLANE
    for cb in range(width // LANE):
        lanes = slice(cb * LANE, (cb + 1) * LANE)
        slot = cb % 2
        window = jnp.concatenate([prev[:, lanes], main_ref[:, lanes], nxt[:, lanes]], axis=0)
        for b in sorted({st % SUBLANE for st in starts}):
            rolled_ref[slot, b] = window if b == 0 else pltpu.roll(window, span - b, axis=0)
        for r0 in range(0, ts, rows):
            acc = jnp.zeros((nv, SUBLANE, LANE), F32)
            for k, st in enumerate(starts):
                shifted = rolled_ref[slot, st % SUBLANE, pl.ds(r0 + st - st % SUBLANE, rows), :]
                acc = acc + shifted.reshape(nv, SUBLANE, LANE) * wb_ref[k, :, lanes][None]
            emit(r0, pl.ds(cb * LANE, LANE), acc.reshape(rows, LANE))


def _window(tile, r0, cb, ts, rows):
    main_ref, prev, nxt = tile
    lanes = slice(cb * LANE, (cb + 1) * LANE)
    lo, hi = max(r0 - HALO, 0), min(r0 + rows + HALO, ts)
    pieces = [prev[:, lanes]] if r0 - HALO < 0 else []
    pieces.append(main_ref[lo:hi, lanes])
    if r0 + rows + HALO > ts:
        pieces.append(nxt[:, lanes])
    return (pieces[0] if len(pieces) == 1 else jnp.concatenate(pieces, axis=0)).astype(F32)


def _phases(starts):
    groups = {}
    for k, st in enumerate(starts):
        groups.setdefault(st % SUBLANE, []).append((k, st // SUBLANE))
    return sorted(groups.items())


def _shifted(blk, b):
    n = blk.shape[0]
    rolled = blk if b == 0 else pltpu.roll(blk, n - b, axis=0)
    return rolled.reshape(n // SUBLANE, SUBLANE, blk.shape[1])


def _conv_bwd_tile(dv_tile, u_ref, wb_ref, acc_ref, n_taps, ts, width, emit, rows=BWD_CONV_ROWS):
    groups = _phases(_bwd_starts(n_taps))
    nv = rows // SUBLANE
    for r0 in range(0, ts, rows):
        for cb in range(width // LANE):
            lanes = pl.ds(cb * LANE, LANE)
            blk = _window(dv_tile, r0, cb, ts, rows)
            u = u_ref[pl.ds(r0, rows), lanes].reshape(nv, SUBLANE, LANE)
            du = jnp.zeros((nv, SUBLANE, LANE), F32)
            for b, taps in groups:
                sh = _shifted(blk, b)
                for k, m in taps:
                    du = du + sh[m:m + nv] * wb_ref[k, :, lanes][None]
                    acc_ref[k, :, lanes] += jnp.sum(sh[m:m + nv] * u, axis=0)
            emit(r0, lanes, du.reshape(rows, LANE))


def _fwd_starts(n_taps):
    pad = (n_taps - 1) // 2
    return [HALO - pad + k for k in range(n_taps)]


def _bwd_starts(n_taps):
    pad = (n_taps - 1) // 2
    return [HALO + pad - k for k in range(n_taps)]


def _peer(x, y, c, flip):
    fx, fy, fc = flip
    return (1 - x if fx else x, 1 - y if fy else y, 1 - c if fc else c)


def _all_gather(shards, name):
    n = len(shards)

    def body(*refs):
        ins, outs = refs[:n], refs[n:2 * n]
        send_sems, recv_sems, local_sems = refs[2 * n:]
        x, y, c = lax.axis_index("x"), lax.axis_index("y"), lax.axis_index("c")
        me = 4 * x + 2 * y + c
        local = [pltpu.make_async_copy(ins[j], outs[j].at[me], local_sems.at[j]) for j in range(n)]
        for cp in local:
            cp.start()
        sends, recvs = [], []
        for k, flip in enumerate(FLIPS):
            px, py, pc = _peer(x, y, c, flip)
            peer = 4 * px + 2 * py + pc
            for j in range(n):
                sem = k * n + j
                sends.append(pltpu.make_async_remote_copy(
                    src_ref=ins[j], dst_ref=outs[j].at[me], send_sem=send_sems.at[sem], recv_sem=recv_sems.at[sem],
                    device_id=(px, py, pc), device_id_type=MESH))
                recvs.append(pltpu.make_async_remote_copy(
                    src_ref=ins[j], dst_ref=outs[j].at[peer], send_sem=send_sems.at[sem], recv_sem=recv_sems.at[sem],
                    device_id=(px, py, pc), device_id_type=MESH))
        for cp in sends:
            cp.start()
        for cp in recvs:
            cp.wait_recv()
        for cp in sends:
            cp.wait_send()
        for cp in local:
            cp.wait()

    return pl.pallas_call(
        body, name=name,
        out_shape=[jax.ShapeDtypeStruct((N_DEV,) + s.shape, s.dtype) for s in shards],
        in_specs=[_hbm()] * n, out_specs=[_hbm()] * n,
        scratch_shapes=[pltpu.SemaphoreType.DMA((7 * n,)), pltpu.SemaphoreType.DMA((7 * n,)),
                        pltpu.SemaphoreType.DMA((n,))],
    )(*shards)


def _place_own(src, n_slots_shape, me, name):
    rows, width = src.shape
    tr = next(t for t in (256, 128, 64, 32, 16, SUBLANE) if rows % t == 0)

    def body(me_ref, src_ref, out_ref):
        out_ref[...] = src_ref[...]

    return pl.pallas_call(
        body, name=name,
        grid_spec=pltpu.PrefetchScalarGridSpec(
            num_scalar_prefetch=1, grid=(rows // tr,),
            in_specs=[pl.BlockSpec((tr, width), lambda i, me_ref: (i, 0))],
            out_specs=pl.BlockSpec((None, tr, width), lambda i, me_ref: (me_ref[0], i, 0))),
        out_shape=pltpu.HBM(n_slots_shape, src.dtype),
    )(me, src)


def _place_cast(pieces, me, name):
    n = len(pieces)
    counts = [a.shape[1] if t else a.shape[0] for a, t in pieces]
    width = pieces[0][0].shape[0] if pieces[0][1] else pieces[0][0].shape[1]
    total = sum(counts)

    def body(me_ref, *refs):
        ins, own_ref, land_ref = refs[:n], refs[n], refs[n + 1]
        first = 0
        for (a, transpose), in_ref, count in zip(pieces, ins, counts):
            block = (in_ref[...].T if transpose else in_ref[...]).astype(BF16)
            own_ref[first:first + count, :] = block
            land_ref[first:first + count, :] = block
            first += count

    return pl.pallas_call(
        body, name=name,
        grid_spec=pltpu.PrefetchScalarGridSpec(
            num_scalar_prefetch=1, grid=(1,),
            in_specs=[pl.BlockSpec(a.shape, lambda i, me_ref: (0, 0)) for a, _ in pieces],
            out_specs=[pl.BlockSpec((total, width), lambda i, me_ref: (0, 0)),
                       pl.BlockSpec((None, total, width), lambda i, me_ref: (me_ref[0], 0, 0))]),
        out_shape=[pltpu.HBM((total, width), BF16), pltpu.HBM((N_DEV, total, width), BF16)],
        compiler_params=_params(32),
    )(me, *[_in_hbm(a) for a, _ in pieces])


def _all_gather_two_level(shards, placed, name):
    n = len(shards)
    given = [j for j in range(n) if placed[j] is not None]

    def body(*refs):
        ins, outs = refs[:n], refs[n + len(given):2 * n + len(given)]
        send_sems, recv_sems, local_sems = refs[2 * n + len(given):]
        x, y, c = lax.axis_index("x"), lax.axis_index("y"), lax.axis_index("c")
        me, sibling = (x, y, c), (x, y, 1 - c)
        chips = [(1 - x, y), (x, 1 - y), (1 - x, 1 - y)]

        def slot(j, dev):
            return outs[j].at[4 * dev[0] + 2 * dev[1] + dev[2]]

        def copy(k, j, block, to, src=None):
            return pltpu.make_async_remote_copy(
                src_ref=slot(j, block) if src is None else src, dst_ref=slot(j, block),
                send_sem=send_sems.at[k * n + j], recv_sem=recv_sems.at[k * n + j], device_id=to, device_id_type=MESH)

        local = [pltpu.make_async_copy(ins[j], slot(j, me), local_sems.at[j]) for j in range(n) if j not in given]
        for cp in local:
            cp.start()
        first = [copy(0, j, me, sibling, src=ins[j]) for j in range(n)]
        first += [copy(1 + t, j, me, (*chip, c), src=ins[j]) for t, chip in enumerate(chips) for j in range(n)]
        for cp in first:
            cp.start()
        passed = []
        for t, chip in enumerate(chips):
            for j in range(n):
                copy(1 + t, j, (*chip, c), me).wait_recv()
                passed.append(copy(4 + t, j, (*chip, c), sibling))
                passed[-1].start()
        for j in range(n):
            copy(0, j, sibling, me).wait_recv()
        for t, chip in enumerate(chips):
            for j in range(n):
                copy(4 + t, j, (*chip, 1 - c), me).wait_recv()
        for cp in first + passed:
            cp.wait_send()
        for cp in local:
            cp.wait()

    return pl.pallas_call(
        body, name=name,
        out_shape=[jax.ShapeDtypeStruct((N_DEV,) + s.shape, s.dtype) for s in shards],
        in_specs=[_hbm()] * (n + len(given)), out_specs=[_hbm()] * n,
        input_output_aliases={n + i: j for i, j in enumerate(given)},
        scratch_shapes=[pltpu.SemaphoreType.DMA((7 * n,)), pltpu.SemaphoreType.DMA((7 * n,)),
                        pltpu.SemaphoreType.DMA((n,))],
    )(*shards, *[placed[j] for j in given])


class _Part(NamedTuple):
    src: jax.Array
    scatter: bool
    rows: int
    land: int
    off: int


class _Started(NamedTuple):
    send_sems: jax.Array
    recv_sems: jax.Array
    thru: tuple
    token: jax.Array
    parts: tuple


def _exchange_copies(srcs, lands, send_sems, recv_sems, parts):
    n = len(parts)
    x, y, c = lax.axis_index("x"), lax.axis_index("y"), lax.axis_index("c")
    me = 4 * x + 2 * y + c

    def block(j, dev):
        p = parts[j]
        return srcs[j].at[pl.ds(pl.multiple_of(dev * p.rows, SUBLANE), p.rows), :] if p.scatter else srcs[j]

    def slot(j, index):
        p = parts[j]
        return lands[p.land].at[index, pl.ds(p.off, p.rows), :]

    sends, recvs = [], []
    for k, flip in enumerate(FLIPS):
        px, py, pc = _peer(x, y, c, flip)
        peer = 4 * px + 2 * py + pc
        for j in range(n):
            sems = dict(send_sem=send_sems.at[k * n + j], recv_sem=recv_sems.at[k * n + j],
                        device_id=(px, py, pc), device_id_type=MESH)
            to, got = (k, k) if parts[j].scatter else (me, peer)
            sends.append(pltpu.make_async_remote_copy(src_ref=block(j, peer), dst_ref=slot(j, to), **sems))
            recvs.append(pltpu.make_async_remote_copy(src_ref=block(j, peer), dst_ref=slot(j, got), **sems))
    return sends, recvs


def _exchange_start(parts, lands, name, after=None):
    n, nl = len(parts), len(lands)
    n_in = n + nl + (after is not None)

    def body(*refs):
        srcs, land_refs = refs[:n], refs[n:n + nl]
        send_sems, recv_sems = refs[n_in], refs[n_in + 1]
        token = refs[n_in + 2 + n + nl]
        sends, _ = _exchange_copies(srcs, land_refs, send_sems, recv_sems, parts)
        for cp in sends:
            cp.start()
        token[...] = jnp.zeros_like(token)

    hbm = pl.BlockSpec(memory_space=pltpu.HBM)
    sem = pl.BlockSpec(memory_space=pltpu.SEMAPHORE)
    fresh = lambda s: lax.empty(s.shape, s.dtype) if isinstance(s, jax.ShapeDtypeStruct) else s
    args = [pltpu.with_memory_space_constraint(p.src, pltpu.HBM) for p in parts]
    args += [pltpu.with_memory_space_constraint(fresh(s), pltpu.HBM) for s in lands]
    args += [] if after is None else [after]
    out = pl.pallas_call(
        body, name=name,
        out_shape=(pltpu.SemaphoreType.DMA((7 * n,)), pltpu.SemaphoreType.DMA((7 * n,)),
                   *[pltpu.HBM(a.shape, a.dtype) for a in args[:n + nl]], jax.ShapeDtypeStruct((SUBLANE, LANE), F32)),
        in_specs=[hbm] * (n + nl) + [_hbm()] * (after is not None),
        out_specs=(sem, sem, *[hbm] * (n + nl), _whole()),
        input_output_aliases={j: 2 + j for j in range(n + nl)},
        compiler_params=pltpu.CompilerParams(has_side_effects=pltpu.SideEffectType.DATAFLOW_SIDE_EFFECTING),
    )(*args)
    return _Started(out[0], out[1], tuple(out[2:2 + n + nl]), out[2 + n + nl], tuple(parts))


def _exchange_wait(started, name, after):
    parts = started.parts
    n, nl = len(parts), len(started.thru) - len(parts)

    def body(*refs):
        srcs, land_refs = refs[:n], refs[n:n + nl]
        send_sems, recv_sems = refs[n + nl], refs[n + nl + 1]
        sends, recvs = _exchange_copies(srcs, land_refs, send_sems, recv_sems, parts)
        for cp in sends:
            cp.wait_send()
        for cp in recvs:
            cp.wait_recv()

    hbm = pl.BlockSpec(memory_space=pltpu.HBM)
    sem = pl.BlockSpec(memory_space=pltpu.SEMAPHORE)
    out = pl.pallas_call(
        body, name=name,
        out_shape=tuple(pltpu.HBM(a.shape, a.dtype) for a in started.thru),
        in_specs=[hbm] * (n + nl) + [sem, sem, _hbm()], out_specs=tuple([hbm] * (n + nl)),
        input_output_aliases={j: j for j in range(n + nl)},
        compiler_params=pltpu.CompilerParams(has_side_effects=pltpu.SideEffectType.DATAFLOW_SIDE_EFFECTING),
    )(*started.thru, started.send_sems, started.recv_sems, after)
    return list(out[:n]), list(out[n:])


def _fwd_in(x, g1, b_in, slab, ts, deps):
    n_tok, dm = x.shape
    rows, off, _ = _layout(dm)
    width = 7 * dm

    def body(x_ref, g1_ref, b_ref, slab_ref, proj_ref, p_ref, u_ref, h_ref, w_v, sems):
        copies = _weight_copies(slab_ref, off["win"], rows["win"], w_v, sems, 0)
        _on_first_step(copies, "start")
        _on_first_step(copies, "wait")
        xv = x_ref[...]
        h = (xv * _rms(xv) * g1_ref[...]).astype(BF16)
        h_ref[...] = h
        cols = []
        for j in range(7):
            pj = _dot_nt(h, w_v[pl.ds(j * dm, dm), :]) + b_ref[:, j * dm:(j + 1) * dm]
            proj_ref[:, j * dm:(j + 1) * dm] = pj.astype(proj_ref.dtype)
            if 1 <= j <= 4:
                cols.append(pj)
            if j == 2:
                p_ref[...] = cols[0] * cols[1]
            if j == 4:
                u_ref[...] = cols[2] * _sigmoid(cols[3])

    return pl.pallas_call(
        _after(body, deps), name="fwd_in", grid=(n_tok // ts,),
        in_specs=[_whole()] * len(deps) + [_rows(ts, dm), _whole(), _whole(), _hbm()],
        out_specs=[_rows(ts, width), _rows(ts, dm), _rows(ts, dm), _rows(ts, dm)],
        out_shape=[jax.ShapeDtypeStruct((n_tok, width), BF16), jax.ShapeDtypeStruct((n_tok, dm), F32),
                   jax.ShapeDtypeStruct((n_tok, dm), F32), jax.ShapeDtypeStruct((n_tok, dm), BF16)],
        scratch_shapes=[pltpu.VMEM((width, dm), BF16), pltpu.SemaphoreType.DMA((N_DEV,))],
        compiler_params=_params(56),
    )(*deps, x, g1, b_in, slab)


def _fwd_mix(p, u, proj, x, caw, cab, cbw, cbb, lng, lnb, g1post, slab, ts):
    n_tok, dm = x.shape
    rows, off, _ = _layout(dm)
    n_steps = n_tok // ts

    def body(p_ref, p_prev, p_next, u_ref, u_prev, u_next, bg_ref, za_ref, zb_ref, x_ref,
             caw_ref, cab_ref, cbw_ref, cbb_ref, lng_ref, lnb_ref, g1p_ref, slab_ref,
             va_ref, vb_ref, ya_ref, yb_ref, qa_ref, sb_ref, mg_ref, mix_ref, x1_ref,
             wa_v, wb_v, wo_v, tap_a, tap_b, sems, rolled_a, rolled_b, va_v, vb_v):
        i = pl.program_id(0)
        copies = (_weight_copies(slab_ref, off["wa"], rows["wa"], wa_v, sems, 0)
                  + _weight_copies(slab_ref, off["wb"], rows["wb"], wb_v, sems, N_DEV)
                  + _weight_copies(slab_ref, off["wo"], rows["wo"], wo_v, sems, 2 * N_DEV))
        _on_first_step(copies, "start")

        @pl.when(i == 0)
        def _():
            _broadcast_taps(caw_ref, tap_a, CONV_A)
            _broadcast_taps(cbw_ref, tap_b, CONV_B)

        def emit_a(r0, lanes, acc):
            va_v[pl.ds(r0, acc.shape[0]), lanes] = acc + cab_ref[:, lanes]

        def emit_b(r0, lanes, acc):
            vb_v[pl.ds(r0, acc.shape[0]), lanes] = acc + cbb_ref[:, lanes]

        _conv_tile(_with_halos(p_ref, p_prev, p_next, i, n_steps), tap_a, _fwd_starts(CONV_A), ts, dm, emit_a,
                   rolled_a)
        _conv_tile(_with_halos(u_ref, u_prev, u_next, i, n_steps), tap_b, _fwd_starts(CONV_B), ts, dm, emit_b,
                   rolled_b)
        _on_first_step(copies, "wait")

        va = va_v[...]
        va_ref[...] = va.astype(BF16)
        qa = (bg_ref[...].astype(F32) * va).astype(BF16)
        qa_ref[...] = qa
        ya = _dot(qa, wa_v[...])
        vb = vb_v[...]
        vb_ref[...] = vb.astype(BF16)
        xc = vb - jnp.mean(vb, axis=-1, keepdims=True)
        rstd = lax.rsqrt(jnp.mean(xc * xc, axis=-1, keepdims=True) + LN_EPS)
        ln = xc * rstd * lng_ref[...] + lnb_ref[...]
        sb = (ln * _sigmoid(ln)).astype(BF16)
        sb_ref[...] = sb
        yb = _dot(sb, wb_v[...])
        ya_ref[...] = ya.astype(BF16)
        yb_ref[...] = yb.astype(BF16)
        merged = (_sigmoid(za_ref[...].astype(F32)) * ya + _sigmoid(zb_ref[...].astype(F32)) * yb).astype(BF16)
        mg_ref[...] = merged
        mix = _dot(merged, wo_v[...])
        mix_ref[...] = mix
        x1_ref[...] = x_ref[...] + mix * _rms(mix) * g1p_ref[...]

    tok = lambda dt: jax.ShapeDtypeStruct((n_tok, dm), dt)
    return pl.pallas_call(
        body, name="fwd_mix", grid=(n_steps,),
        in_specs=(_halo_specs(ts, dm, n_tok) + _halo_specs(ts, dm, n_tok)
                  + [_rows(ts, dm, 0), _rows(ts, dm, 5), _rows(ts, dm, 6), _rows(ts, dm)]
                  + [_whole()] * 7 + [_hbm()]),
        out_specs=[_rows(ts, dm)] * 9,
        out_shape=[tok(BF16), tok(BF16), tok(BF16), tok(BF16), tok(BF16), tok(BF16), tok(BF16), tok(F32), tok(F32)],
        scratch_shapes=[pltpu.VMEM((dm, dm), BF16), pltpu.VMEM((dm, dm), BF16), pltpu.VMEM((dm, dm), BF16),
                        pltpu.VMEM((CONV_A, SUBLANE, dm), F32), pltpu.VMEM((CONV_B, SUBLANE, dm), F32),
                        pltpu.SemaphoreType.DMA((3 * N_DEV,)),
                        pltpu.VMEM((2, SUBLANE, ts + 2 * HALO, LANE), F32),
                        pltpu.VMEM((2, SUBLANE, ts + 2 * HALO, LANE), F32),
                        pltpu.VMEM((ts, dm), F32), pltpu.VMEM((ts, dm), F32)],
        compiler_params=_params(48),
    )(p, p, p, u, u, u, proj, proj, proj, x, caw, cab, cbw, cbb, lng, lnb, g1post, slab)


def _mlp_fwd_bwd(x1, mix, tgt, g1post, g2pre, g2post, slab, ts):
    n_tok, dm = x1.shape
    rows, off, _ = _layout(dm)
    ff = 4 * dm

    def body(x1_ref, mix_ref, t_ref, g1p_ref, g2pre_ref, g2post_ref, slab_ref,
             f_ref, df1_ref, h2_ref, df2_ref, dmix_ref, dx1_ref, small_ref, w1_v, w2_v, relu_v, sems):
        w1_copies = _weight_copies(slab_ref, off["w1"], rows["w1"], w1_v, sems, 0)
        w2_copies = _weight_copies(slab_ref, off["w2"], rows["w2"], w2_v, sems, N_DEV)
        _on_first_step(w1_copies + w2_copies, "start")

        @pl.when(pl.program_id(0) == 0)
        def _():
            small_ref[...] = jnp.zeros_like(small_ref)

        _on_first_step(w1_copies + w2_copies, "wait")
        x1v = x1_ref[...]
        r3 = _rms(x1v)
        g2pre = g2pre_ref[...]
        h2 = (x1v * r3 * g2pre).astype(BF16)
        h2_ref[...] = h2
        for c in range(4):
            blk = pl.ds(c * dm, dm)
            relu = jnp.maximum(_dot_nt(h2, w1_v[blk, :]), 0.0)
            relu_v[:, c * dm:(c + 1) * dm] = relu
            f_ref[:, c * dm:(c + 1) * dm] = (relu * relu).astype(BF16)
        f2 = _dot(f_ref[...], w2_v[...])
        r4 = _rms(f2)
        g2post = g2post_ref[...]
        err = x1v + f2 * r4 * g2post - t_ref[...]
        dy = err * (1.0 / dm)
        small_ref[3:4, :] += _colsum(err * err)
        small_ref[0:1, :] += _colsum(dy * f2 * r4)
        df2 = _rms_bwd(dy, f2, r4, g2post).astype(BF16)
        df2_ref[...] = df2
        for c in range(4):
            blk = pl.ds(c * dm, dm)
            df1 = (_dot_nt(df2, w2_v[blk, :]) * (2.0 * relu_v[:, c * dm:(c + 1) * dm])).astype(BF16)
            df1_ref[:, c * dm:(c + 1) * dm] = df1
        dh2 = _dot(df1_ref[...], w1_v[...])
        small_ref[1:2, :] += _colsum(dh2 * x1v * r3)
        dx1 = dy + _rms_bwd(dh2, x1v, r3, g2pre)
        dx1_ref[...] = dx1
        mixv = mix_ref[...]
        r2 = _rms(mixv)
        small_ref[2:3, :] += _colsum(dx1 * mixv * r2)
        dmix_ref[...] = _rms_bwd(dx1, mixv, r2, g1p_ref[...]).astype(BF16)

    tok = lambda w, dt: jax.ShapeDtypeStruct((n_tok, w), dt)
    return pl.pallas_call(
        body, name="mlp_fwd_bwd", grid=(n_tok // ts,),
        in_specs=[_rows(ts, dm)] * 3 + [_whole()] * 3 + [_hbm()],
        out_specs=[_rows(ts, ff), _rows(ts, ff), _rows(ts, dm), _rows(ts, dm), _rows(ts, dm), _rows(ts, dm),
                   pl.BlockSpec((SUBLANE, dm), lambda i: (0, 0))],
        out_shape=[tok(ff, BF16), tok(ff, BF16), tok(dm, BF16), tok(dm, BF16), tok(dm, BF16), tok(dm, F32),
                   jax.ShapeDtypeStruct((SUBLANE, dm), F32)],
        scratch_shapes=[pltpu.VMEM((ff, dm), BF16), pltpu.VMEM((ff, dm), BF16), pltpu.VMEM((ts, ff), F32),
                        pltpu.SemaphoreType.DMA((2 * N_DEV,))],
        compiler_params=_params(56),
    )(x1, mix, tgt, g1post, g2pre, g2post, slab)


def _bwd_mix(dmix, ya, yb, proj, va, vb, lng, lnb, slab, ts, deps):
    n_tok, dm = dmix.shape
    rows, off, _ = _layout(dm)
    n_steps = n_tok // ts

    def body(dmix_ref, ya_ref, yb_ref, bg_ref, za_ref, zb_ref, va_ref, vb_ref, lng_ref, lnb_ref, slab_ref,
             dpa_ref, dya_ref, dyb_ref, dva_ref, dvb_ref, small_ref, wa_v, wb_v, wo_v, sems):
        wo_copies = _weight_copies(slab_ref, off["wo"], rows["wo"], wo_v, sems, 2 * N_DEV)
        ab_copies = (_weight_copies(slab_ref, off["wa"], rows["wa"], wa_v, sems, 0)
                     + _weight_copies(slab_ref, off["wb"], rows["wb"], wb_v, sems, N_DEV))
        _on_first_step(wo_copies + ab_copies, "start")

        @pl.when(pl.program_id(0) == 0)
        def _():
            small_ref[...] = jnp.zeros_like(small_ref)

        _on_first_step(wo_copies, "wait")
        dmerged = _dot_nt(dmix_ref[...], wo_v[...])
        _on_first_step(ab_copies, "wait")
        sa = _sigmoid(za_ref[...].astype(F32))
        sg = _sigmoid(zb_ref[...].astype(F32))
        dza = dmerged * ya_ref[...].astype(F32) * sa * (1.0 - sa)
        dzb = dmerged * yb_ref[...].astype(F32) * sg * (1.0 - sg)
        dpa_ref[:, dm:2 * dm] = dza.astype(BF16)
        dpa_ref[:, 2 * dm:3 * dm] = dzb.astype(BF16)
        small_ref[5:6, :] += _colsum(dza)
        small_ref[6:7, :] += _colsum(dzb)

        dya = (dmerged * sa).astype(BF16)
        dya_ref[...] = dya
        dqa = _dot_nt(dya, wa_v[...])
        dbg = dqa * va_ref[...].astype(F32)
        dpa_ref[:, 0:dm] = dbg.astype(BF16)
        small_ref[4:5, :] += _colsum(dbg)
        dva = dqa * bg_ref[...].astype(F32)
        dva_ref[...] = dva.astype(BF16)
        small_ref[2:3, :] += _colsum(dva)

        dyb = (dmerged * sg).astype(BF16)
        dyb_ref[...] = dyb
        dsb = _dot_nt(dyb, wb_v[...])
        vb = vb_ref[...].astype(F32)
        xc = vb - jnp.mean(vb, axis=-1, keepdims=True)
        rstd = lax.rsqrt(jnp.mean(xc * xc, axis=-1, keepdims=True) + LN_EPS)
        nrm = xc * rstd
        lng_v = lng_ref[...]
        ln = nrm * lng_v + lnb_ref[...]
        sl = _sigmoid(ln)
        dln = dsb * (sl * (1.0 + ln * (1.0 - sl)))
        small_ref[0:1, :] += _colsum(dln * nrm)
        small_ref[1:2, :] += _colsum(dln)
        dn = dln * lng_v
        dvb = rstd * (dn - jnp.mean(dn, axis=-1, keepdims=True)
                      - nrm * jnp.mean(dn * nrm, axis=-1, keepdims=True))
        dvb_ref[...] = dvb.astype(BF16)
        small_ref[3:4, :] += _colsum(dvb)

    tok = lambda w, dt: jax.ShapeDtypeStruct((n_tok, w), dt)
    return pl.pallas_call(
        _after(body, deps), name="bwd_mix", grid=(n_steps,),
        in_specs=([_whole()] * len(deps) + [_rows(ts, dm)] * 3
                  + [_rows(ts, dm, 0), _rows(ts, dm, 5), _rows(ts, dm, 6)]
                  + [_rows(ts, dm)] * 2 + [_whole()] * 2 + [_hbm()]),
        out_specs=[_rows(ts, 3 * dm), _rows(ts, dm), _rows(ts, dm), _rows(ts, dm), _rows(ts, dm),
                   pl.BlockSpec((SUBLANE, dm), lambda i: (0, 0))],
        out_shape=[tok(3 * dm, BF16), tok(dm, BF16), tok(dm, BF16), tok(dm, BF16), tok(dm, BF16),
                   jax.ShapeDtypeStruct((SUBLANE, dm), F32)],
        scratch_shapes=[pltpu.VMEM((dm, dm), BF16), pltpu.VMEM((dm, dm), BF16), pltpu.VMEM((dm, dm), BF16),
                        pltpu.SemaphoreType.DMA((3 * N_DEV,))],
        compiler_params=_params(48),
    )(*deps, dmix, ya, yb, proj, proj, proj, va, vb, lng, lnb, slab)


def _bwd_conv(dva, dvb, p, u, proj, dpa, caw, cbw, ts, deps):
    n_tok, dm = dva.shape
    n_steps = n_tok // ts
    small_rows = 40

    def body(dva_ref, dva_prev, dva_next, dvb_ref, dvb_prev, dvb_next, p_ref, u_ref,
             cg_ref, ha_ref, a_ref, g_ref, dpa_ref, caw_ref, cbw_ref,
             dproj_ref, small_ref,
             dp_v, du_v, tap_a, tap_b, gwa_v, gwb_v):
        i = pl.program_id(0)

        @pl.when(i == 0)
        def _():
            _broadcast_taps(caw_ref, tap_a, CONV_A)
            _broadcast_taps(cbw_ref, tap_b, CONV_B)
            small_ref[...] = jnp.zeros_like(small_ref)
            gwa_v[...] = jnp.zeros_like(gwa_v)
            gwb_v[...] = jnp.zeros_like(gwb_v)

        def emit_dp(r0, lanes, acc):
            dp_v[pl.ds(r0, acc.shape[0]), lanes] = acc

        def emit_du(r0, lanes, acc):
            du_v[pl.ds(r0, acc.shape[0]), lanes] = acc

        _conv_bwd_tile(_with_halos(dva_ref, dva_prev, dva_next, i, n_steps), p_ref, tap_a, gwa_v, CONV_A, ts, dm,
                       emit_dp)
        _conv_bwd_tile(_with_halos(dvb_ref, dvb_prev, dvb_next, i, n_steps), u_ref, tap_b, gwb_v, CONV_B, ts, dm,
                       emit_du)

        dp = dp_v[...]
        dcg = dp * ha_ref[...].astype(F32)
        dha = dp * cg_ref[...].astype(F32)
        du = du_v[...]
        sg = _sigmoid(g_ref[...].astype(F32))
        da = du * sg
        dg = du * a_ref[...].astype(F32) * sg * (1.0 - sg)
        dproj_ref[:, 0:dm] = dpa_ref[:, 0:dm]
        dproj_ref[:, dm:2 * dm] = dcg.astype(BF16)
        dproj_ref[:, 2 * dm:3 * dm] = dha.astype(BF16)
        dproj_ref[:, 3 * dm:4 * dm] = da.astype(BF16)
        dproj_ref[:, 4 * dm:5 * dm] = dg.astype(BF16)
        dproj_ref[:, 5 * dm:7 * dm] = dpa_ref[:, dm:3 * dm]
        small_ref[3:4, :] += _colsum(dcg)
        small_ref[4:5, :] += _colsum(dha)
        small_ref[5:6, :] += _colsum(da)
        small_ref[6:7, :] += _colsum(dg)

        @pl.when(i == n_steps - 1)
        def _():
            for k in range(CONV_A):
                small_ref[k:k + 1, :] = _colsum(gwa_v[k])
            for k in range(CONV_B):
                small_ref[SUBLANE + k:SUBLANE + k + 1, :] = _colsum(gwb_v[k])

    return pl.pallas_call(
        _after(body, deps), name="bwd_conv", grid=(n_steps,),
        in_specs=([_whole()] * len(deps) + _halo_specs(ts, dm, n_tok) * 2 + [_rows(ts, dm)] * 2
                  + [_rows(ts, dm, 1), _rows(ts, dm, 2), _rows(ts, dm, 3), _rows(ts, dm, 4), _rows(ts, 3 * dm)]
                  + [_whole()] * 2),
        out_specs=[_rows(ts, 7 * dm), pl.BlockSpec((small_rows, dm), lambda i: (0, 0))],
        out_shape=[jax.ShapeDtypeStruct((n_tok, 7 * dm), BF16), jax.ShapeDtypeStruct((small_rows, dm), F32)],
        scratch_shapes=[pltpu.VMEM((ts, dm), F32), pltpu.VMEM((ts, dm), F32),
                        pltpu.VMEM((CONV_A, SUBLANE, dm), F32), pltpu.VMEM((CONV_B, SUBLANE, dm), F32),
                        pltpu.VMEM((CONV_A, SUBLANE, dm), F32), pltpu.VMEM((CONV_B, SUBLANE, dm), F32)],
        compiler_params=_params(48),
    )(*deps, dva, dva, dva, dvb, dvb, dvb, p, u, proj, proj, proj, proj, dpa, caw, cbw)


def _bwd_in(dproj, x, dx1, g1, slab, ts, deps):
    n_tok, dm = x.shape
    rows, off, _ = _layout(dm)
    width = 7 * dm

    def body(dproj_ref, x_ref, dx1_ref, g1_ref, slab_ref, gx_ref, small_ref, w_v, sems):
        copies = _weight_copies(slab_ref, off["win"], rows["win"], w_v, sems, 0)
        _on_first_step(copies, "start")

        @pl.when(pl.program_id(0) == 0)
        def _():
            small_ref[...] = jnp.zeros_like(small_ref)

        _on_first_step(copies, "wait")
        dh = _dot(dproj_ref[...], w_v[...])
        xv = x_ref[...]
        r1 = _rms(xv)
        small_ref[0:1, :] += _colsum(dh * xv * r1)
        gx_ref[...] = dx1_ref[...] + _rms_bwd(dh, xv, r1, g1_ref[...])

    return pl.pallas_call(
        _after(body, deps), name="bwd_in", grid=(n_tok // ts,),
        in_specs=[_whole()] * len(deps) + [_rows(ts, width), _rows(ts, dm), _rows(ts, dm), _whole(), _hbm()],
        out_specs=[_rows(ts, dm), pl.BlockSpec((SUBLANE, dm), lambda i: (0, 0))],
        out_shape=[jax.ShapeDtypeStruct((n_tok, dm), F32), jax.ShapeDtypeStruct((SUBLANE, dm), F32)],
        scratch_shapes=[pltpu.VMEM((width, dm), BF16), pltpu.SemaphoreType.DMA((N_DEV,))],
        compiler_params=_params(56),
    )(*deps, dproj, x, dx1, g1, slab)


def _wgrad(a, b, name, tm, tk, out_dtype):
    n_tok, m = a.shape
    n = b.shape[1]
    k_steps = n_tok // tk

    def body(a_ref, b_ref, o_ref, acc_v):
        k = pl.program_id(1)

        @pl.when(k == 0)
        def _():
            acc_v[...] = jnp.zeros_like(acc_v)

        acc_v[...] += _dot_tn(a_ref[...], b_ref[...])

        @pl.when(k == k_steps - 1)
        def _():
            o_ref[...] = acc_v[...].astype(o_ref.dtype)

    return pl.pallas_call(
        body, name=name, grid=(m // tm, k_steps),
        in_specs=[pl.BlockSpec((tk, tm), lambda i, k: (k, i)), pl.BlockSpec((tk, n), lambda i, k: (k, 0))],
        out_specs=pl.BlockSpec((tm, n), lambda i, k: (i, 0)),
        out_shape=pltpu.HBM((m, n), out_dtype),
        scratch_shapes=[pltpu.VMEM((tm, n), F32)],
        compiler_params=pltpu.CompilerParams(dimension_semantics=("arbitrary", "arbitrary"),
                                             vmem_limit_bytes=40 * MIB),
    )(a, b)


def _adamw(w, g, m, v):
    m = ADAM_B1 * m + (1.0 - ADAM_B1) * g
    v = ADAM_B2 * v + (1.0 - ADAM_B2) * (g * g)
    m_hat = m / (1.0 - ADAM_B1 ** ADAM_STEP)
    v_hat = v / (1.0 - ADAM_B2 ** ADAM_STEP)
    delta = -ADAM_LR * (m_hat / (jnp.sqrt(v_hat) + ADAM_EPS) + ADAM_WD * w)
    return delta, m, v


def _adam_big(recv, part, me, off, rows, w, m, v, transpose, name, tr):
    dm = recv.shape[2]
    per = rows // tr

    def body(me_ref, own_ref, r_ref, w_ref, m_ref, v_ref, g_ref, d_ref, mo_ref, vo_ref):
        g = own_ref[...].astype(F32)
        for k in range(len(FLIPS)):
            g = g + r_ref[k].astype(F32)
        if transpose:
            g = g.T
        delta, m_new, v_new = _adamw(w_ref[...], g, m_ref[...], v_ref[...])
        g_ref[...] = g
        d_ref[...] = delta
        mo_ref[...] = m_new
        vo_ref[...] = v_new

    if transpose:
        blk = pl.BlockSpec((dm, tr), lambda i, me_ref: (0, i))
    else:
        blk = pl.BlockSpec((tr, dm), lambda i, me_ref: (i, 0))
    first = off // tr
    return pl.pallas_call(
        body, name=name,
        grid_spec=pltpu.PrefetchScalarGridSpec(
            num_scalar_prefetch=1, grid=(per,),
            in_specs=[pl.BlockSpec((tr, dm), lambda i, me_ref: (me_ref[0] * per + i, 0)),
                      pl.BlockSpec((len(FLIPS), tr, dm), lambda i, me_ref: (0, first + i, 0)), blk, blk, blk],
            out_specs=[blk] * 4),
        out_shape=[jax.ShapeDtypeStruct(w.shape, F32)] * 4,
        compiler_params=_params(32),
    )(me, *[_in_hbm(a) for a in (part, recv, w, m, v)])


LOSS_ROW = 15
CONV_A_ROW = 16
CONV_B_ROW = 24


def _adam_small(recv_small, recv_last, me, params, d_model):
    n = len(params)
    cw = d_model // N_DEV

    def body(me_ref, r_ref, rc_ref, l_ref, *refs):
        ins, loss_ref, outs = refs[:3 * n], refs[3 * n], refs[3 * n + 1:3 * n + 1 + 4 * n]
        g_v, gc_v, last_v = refs[3 * n + 1 + 4 * n:]
        g, gc, last = r_ref[0], rc_ref[0], l_ref[0]
        for d in range(1, N_DEV):
            g, gc, last = g + r_ref[d], gc + rc_ref[d], last + l_ref[d]
        g_v[...], gc_v[...], last_v[...] = g, gc, last
        loss_ref[...] = (0.5 / d_model) * jnp.sum(g_v[LOSS_ROW:LOSS_ROW + 1, :], axis=-1, keepdims=True)
        for j, (row0, own_columns, (w, _, _)) in enumerate(params):
            w_ref, m_ref, v_ref = ins[3 * j:3 * j + 3]
            source = gc_v if own_columns else (last_v if row0 == 0 else g_v)
            grad = source[row0:row0 + w.shape[0], :]
            delta, m_new, v_new = _adamw(w_ref[...], grad, m_ref[...], v_ref[...])
            for ref, val in zip(outs[4 * j:4 * j + 4], (grad, delta, m_new, v_new)):
                ref[...] = val

    full = lambda shape: pl.BlockSpec(shape, lambda i, me_ref: (0,) * len(shape))
    stack_rows = recv_small.shape[1]
    flat = [a for _, _, triple in params for a in triple]
    shapes = [w.shape for _, _, (w, _, _) in params for _ in range(4)]
    out = pl.pallas_call(
        body, name="adam_small",
        grid_spec=pltpu.PrefetchScalarGridSpec(
            num_scalar_prefetch=1, grid=(1,),
            in_specs=[full(recv_small.shape),
                      pl.BlockSpec((N_DEV, stack_rows, cw), lambda i, me_ref: (0, 0, me_ref[0])),
                      full(recv_last.shape)] + [full(a.shape) for a in flat],
            out_specs=[full((1, 1))] + [full(s) for s in shapes],
            scratch_shapes=[pltpu.VMEM((stack_rows, d_model), F32), pltpu.VMEM((stack_rows, cw), F32),
                            pltpu.VMEM(recv_last.shape[1:], F32)]),
        out_shape=[jax.ShapeDtypeStruct((1, 1), F32)] + [jax.ShapeDtypeStruct(s, F32) for s in shapes],
    )(me, *[_in_hbm(a) for a in (recv_small, recv_small, recv_last, *flat)])
    return out[0], [tuple(out[1 + 4 * j:5 + 4 * j]) for j in range(n)]


def _tile(n_tok, want):
    return min(want, n_tok)


def kernel(x, norm1_pre_g, w_in, b_in, conv_a_w, conv_a_b, w_a_out, conv_b_w, conv_b_b, ln_b_g, ln_b_b, w_b_out, w_o, norm1_post_g, norm2_pre_g, w_mlp_in, w_mlp_out, norm2_post_g, loss_target, m_norm1_pre_g, m_w_in, m_b_in, m_conv_a_w, m_conv_a_b, m_w_a_out, m_conv_b_w, m_conv_b_b, m_ln_b_g, m_ln_b_b, m_w_b_out, m_w_o, m_norm1_post_g, m_norm2_pre_g, m_w_mlp_in, m_w_mlp_out, m_norm2_post_g, v_norm1_pre_g, v_w_in, v_b_in, v_conv_a_w, v_conv_a_b, v_w_a_out, v_conv_b_w, v_conv_b_b, v_ln_b_g, v_ln_b_b, v_w_b_out, v_w_o, v_norm1_post_g, v_norm2_pre_g, v_w_mlp_in, v_w_mlp_out, v_norm2_post_g):
    n_tok, dm = x.shape[1], x.shape[2]
    rows, off, slab_rows = _layout(dm)
    cw = dm // N_DEV
    xs = x.reshape(n_tok, dm)
    tgt = loss_target.reshape(n_tok, dm)
    row = lambda vec: vec.reshape(1, -1)
    scattered = lambda group: jax.ShapeDtypeStruct((len(FLIPS), slab_rows[group], dm), BF16)
    tm, tk = min(dm, 1024), _tile(n_tok, 2048)
    me = (4 * lax.axis_index("x") + 2 * lax.axis_index("y") + lax.axis_index("c")).astype(jnp.int32).reshape(1)

    conv_own = jnp.concatenate([conv_a_w, jnp.zeros((SUBLANE - CONV_A, cw), F32), conv_b_w,
                                jnp.zeros((1, cw), F32)], axis=0)
    own_in, land_in = _place_cast([(w_in, True)], me, "place_w_in")
    slab_in, conv_all = _all_gather_two_level([own_in, conv_own], [land_in, None], "gather_w_in")
    conv_full = conv_all.transpose(1, 0, 2).reshape(conv_own.shape[0], dm)
    caw, cbw = conv_full[0:CONV_A], conv_full[SUBLANE:SUBLANE + CONV_B]
    own_abo, land_abo = _place_cast([(w_a_out, False), (w_b_out, False), (w_o, False)], me, "place_abo")
    own_mlp, land_mlp = _place_cast([(w_mlp_in, True), (w_mlp_out, False)], me, "place_mlp")
    ag_abo = _exchange_start([_Part(own_abo, False, slab_rows["abo"], 0, 0)], [land_abo],
                             "gather_abo_start", after=slab_in)
    ag_mlp = _exchange_start([_Part(own_mlp, False, slab_rows["mlp"], 0, 0)], [land_mlp],
                             "gather_mlp_start", after=ag_abo.token)

    proj, p, u, h = _fwd_in(xs, row(norm1_pre_g), row(b_in), slab_in, _tile(n_tok, 512),
                            [ag_abo.token, ag_mlp.token])
    _, (slab_abo,) = _exchange_wait(ag_abo, "gather_abo_wait", after=proj)
    va, vb, ya, yb, qa, sb, merged, mix, x1 = _fwd_mix(
        p, u, proj, xs, caw, row(conv_a_b), cbw, row(conv_b_b), row(ln_b_g), row(ln_b_b), row(norm1_post_g),
        slab_abo, _tile(n_tok, 256))
    _, (slab_mlp,) = _exchange_wait(ag_mlp, "gather_mlp_wait", after=x1)
    f, df1, h2, df2, dmix, dx1, small_mlp = _mlp_fwd_bwd(
        x1, mix, tgt, row(norm1_post_g), row(norm2_pre_g), row(norm2_post_g), slab_mlp, _tile(n_tok, 256))

    rs_mlp = _exchange_start(
        [_Part(_wgrad(df1, h2, "wgrad_mlp_in", tm, tk, BF16), True, rows["w1"], 0, off["w1"]),
         _Part(_wgrad(f, df2, "wgrad_mlp_out", tm, tk, BF16), True, rows["w2"], 0, off["w2"])],
        [scattered("mlp")], "scatter_mlp_start")
    dpa, dya, dyb, dva, dvb, small_mix = _bwd_mix(
        dmix, ya, yb, proj, va, vb, row(ln_b_g), row(ln_b_b), slab_abo, _tile(n_tok, 256), [rs_mlp.token])
    rs_abo = _exchange_start(
        [_Part(_wgrad(qa, dya, "wgrad_a_out", tm, tk, BF16), True, rows["wa"], 0, off["wa"]),
         _Part(_wgrad(sb, dyb, "wgrad_b_out", tm, tk, BF16), True, rows["wb"], 0, off["wb"]),
         _Part(_wgrad(merged, dmix, "wgrad_o", tm, tk, BF16), True, rows["wo"], 0, off["wo"])],
        [scattered("abo")], "scatter_abo_start")
    dproj, small_conv = _bwd_conv(dva, dvb, p, u, proj, dpa, caw, cbw, _tile(n_tok, 256), [rs_abo.token])

    zeros = lambda r: jnp.zeros((r, dm), F32)
    small = jnp.concatenate([
        zeros(1),
        small_mix[2:3],
        small_mix[3:4],
        small_mix[0:2],
        small_mlp[2:3],
        small_mlp[1:2],
        small_mlp[0:1],
        small_mix[4:5], small_conv[3:7], small_mix[5:7],
        small_mlp[3:4],
        small_conv[0:CONV_A], zeros(SUBLANE - CONV_A),
        small_conv[8:8 + CONV_B], zeros(1),
    ], axis=0)

    rs_in = _exchange_start(
        [_Part(_wgrad(dproj, h, "wgrad_in", tm, tk, BF16), True, rows["win"], 0, off["win"]),
         _Part(small, False, small.shape[0], 1, 0)],
        [scattered("in"), _place_own(small, (N_DEV,) + small.shape, me, "place_small")], "scatter_in_start")
    grad_x, small_in = _bwd_in(dproj, xs, dx1, row(norm1_pre_g), slab_in, _tile(n_tok, 256), [rs_in.token])

    tr = min(LANE, rows["wa"])
    (g_w1, g_w2), (recv_mlp,) = _exchange_wait(rs_mlp, "scatter_mlp_wait", after=grad_x)
    (g_wa, g_wb, g_wo), (recv_abo,) = _exchange_wait(rs_abo, "scatter_abo_wait", after=grad_x)
    big = {
        "w_mlp_in": _adam_big(recv_mlp, g_w1, me, off["w1"], rows["w1"], w_mlp_in, m_w_mlp_in, v_w_mlp_in, True,
                              "adam_w_mlp_in", tr),
        "w_mlp_out": _adam_big(recv_mlp, g_w2, me, off["w2"], rows["w2"], w_mlp_out, m_w_mlp_out, v_w_mlp_out,
                               False, "adam_w_mlp_out", tr),
        "w_a_out": _adam_big(recv_abo, g_wa, me, off["wa"], rows["wa"], w_a_out, m_w_a_out, v_w_a_out, False,
                             "adam_w_a_out", tr),
        "w_b_out": _adam_big(recv_abo, g_wb, me, off["wb"], rows["wb"], w_b_out, m_w_b_out, v_w_b_out, False,
                             "adam_w_b_out", tr),
        "w_o": _adam_big(recv_abo, g_wo, me, off["wo"], rows["wo"], w_o, m_w_o, v_w_o, False, "adam_w_o", tr),
    }
    (g_win, _), (recv_in, recv_small) = _exchange_wait(rs_in, "scatter_in_wait", after=big["w_o"][3])
    recv_last, = _all_gather([small_in], "gather_last")
    big["w_in"] = _adam_big(recv_in, g_win, me, off["win"], rows["win"], w_in, m_w_in, v_w_in, True, "adam_w_in", tr)

    small_names = ("norm1_pre_g", "conv_a_b", "conv_b_b", "ln_b_g", "ln_b_b", "norm1_post_g", "norm2_pre_g",
                   "norm2_post_g")
    given = dict(
        norm1_pre_g=(norm1_pre_g, m_norm1_pre_g, v_norm1_pre_g), conv_a_b=(conv_a_b, m_conv_a_b, v_conv_a_b),
        conv_b_b=(conv_b_b, m_conv_b_b, v_conv_b_b), ln_b_g=(ln_b_g, m_ln_b_g, v_ln_b_g),
        ln_b_b=(ln_b_b, m_ln_b_b, v_ln_b_b), norm1_post_g=(norm1_post_g, m_norm1_post_g, v_norm1_post_g),
        norm2_pre_g=(norm2_pre_g, m_norm2_pre_g, v_norm2_pre_g),
        norm2_post_g=(norm2_post_g, m_norm2_post_g, v_norm2_post_g))
    params = [(j, False, tuple(row(a) for a in given[name])) for j, name in enumerate(small_names)]
    params.append((SUBLANE, False, tuple(a.reshape(7, dm) for a in (b_in, m_b_in, v_b_in))))
    params.append((CONV_A_ROW, True, (conv_a_w, m_conv_a_w, v_conv_a_w)))
    params.append((CONV_B_ROW, True, (conv_b_w, m_conv_b_w, v_conv_b_w)))
    loss, small_out = _adam_small(recv_small, recv_last, me, params, dm)
    small_leaves = {name: tuple(a.reshape(dm) for a in small_out[j]) for j, name in enumerate(small_names)}
    small_leaves["b_in"] = tuple(a.reshape(7 * dm) for a in small_out[len(small_names)])
    small_leaves["conv_a_w"] = small_out[len(small_names) + 1]
    small_leaves["conv_b_w"] = small_out[len(small_names) + 2]

    order = ("norm1_pre_g", "w_in", "b_in", "conv_a_w", "conv_a_b", "w_a_out", "conv_b_w", "conv_b_b", "ln_b_g",
             "ln_b_b", "w_b_out", "w_o", "norm1_post_g", "norm2_pre_g", "w_mlp_in", "w_mlp_out", "norm2_post_g")
    leaves = [big[name] if name in big else small_leaves[name] for name in order]
    grads, deltas, new_m, new_v = zip(*leaves)
    return (loss.reshape(()), grad_x.reshape(x.shape), *grads, *deltas, *new_m, *new_v)
```

```python
from typing import NamedTuple

import jax
import jax.numpy as jnp
from jax import lax
from jax.experimental import pallas as pl
from jax.experimental.pallas import tpu as pltpu

F32 = jnp.float32
BF16 = jnp.bfloat16

RMS_EPS = 1e-6
LN_EPS = 1e-5
ADAM_LR = 0.001
ADAM_B1 = 0.9
ADAM_B2 = 0.999
ADAM_EPS = 1e-08
ADAM_WD = 0.01
ADAM_STEP = 10

N_DEV = 8
CONV_A = 3
CONV_B = 31
LANE = 128
SUBLANE = 8
HALO = 16
FWD_CONV_ROWS = 32
BWD_CONV_ROWS = 64
MIB = 1 << 20
FLIPS = ((0, 0, 1), (0, 1, 0), (1, 0, 0), (0, 1, 1), (1, 0, 1), (1, 1, 0), (1, 1, 1))
MESH = pl.DeviceIdType.MESH


def _layout(d_model):
    e = d_model // N_DEV
    rows = {"win": 7 * e, "w1": 4 * e, "w2": 4 * e, "wa": e, "wb": e, "wo": e}
    off = {"win": 0, "w1": 0, "w2": 4 * e, "wa": 0, "wb": e, "wo": 2 * e}
    return rows, off, {"in": 7 * e, "mlp": 8 * e, "abo": 3 * e}


def _after(body, deps):
    def ordered(*refs):
        return body(*refs[len(deps):])
    return ordered


def _params(vmem_mib):
    return pltpu.CompilerParams(dimension_semantics=("arbitrary",), vmem_limit_bytes=vmem_mib * MIB)


def _whole():
    return pl.BlockSpec(memory_space=pltpu.VMEM)


def _hbm():
    return pl.BlockSpec(memory_space=pl.ANY)


def _in_hbm(a):
    return pltpu.with_memory_space_constraint(a, pltpu.HBM)


def _rows(ts, width, col=0):
    return pl.BlockSpec((ts, width), lambda i: (i, col))


def _halo_specs(ts, width, n_rows):
    per = ts // HALO
    last = n_rows // HALO - 1
    return [
        pl.BlockSpec((ts, width), lambda i: (i, 0)),
        pl.BlockSpec((HALO, width), lambda i: (jnp.maximum(i * per - 1, 0), 0)),
        pl.BlockSpec((HALO, width), lambda i: (jnp.minimum((i + 1) * per, last), 0)),
    ]


def _dot(a, b):
    return jnp.dot(a, b, preferred_element_type=F32)


def _dot_nt(a, b):
    return lax.dot_general(a, b, (((1,), (1,)), ((), ())), preferred_element_type=F32)


def _dot_tn(a, b):
    return lax.dot_general(a, b, (((0,), (0,)), ((), ())), preferred_element_type=F32)


def _rms(u):
    return lax.rsqrt(jnp.mean(u * u, axis=-1, keepdims=True) + RMS_EPS)


def _rms_bwd(dz, u, r, g):
    dzg = dz * g
    return r * dzg - u * (r * r * r) * jnp.mean(dzg * u, axis=-1, keepdims=True)


def _colsum(v):
    return jnp.sum(v, axis=0, keepdims=True)


def _sigmoid(v):
    return jax.nn.sigmoid(v)


def _weight_copies(slab_ref, off, rows, dst_ref, sems, first_sem):
    return [pltpu.make_async_copy(slab_ref.at[d, pl.ds(off, rows), :], dst_ref.at[pl.ds(d * rows, rows), :],
                                  sems.at[first_sem + d]) for d in range(N_DEV)]


def _on_first_step(copies, method):
    @pl.when(pl.program_id(0) == 0)
    def _():
        for cp in copies:
            getattr(cp, method)()


def _with_halos(main_ref, prev_ref, next_ref, i, n_steps):
    return (main_ref, jnp.where(i > 0, prev_ref[...], 0.0), jnp.where(i < n_steps - 1, next_ref[...], 0.0))


def _broadcast_taps(w_ref, wb_ref, n_taps):
    for k in range(n_taps):
        wb_ref[k] = jnp.broadcast_to(w_ref[k:k + 1, :], wb_ref.shape[1:])


def _conv_tile(tile, wb_ref, starts, ts, width, emit, rolled_ref, rows=FWD_CONV_ROWS):
    main_ref, prev, nxt = tile
    span = ts + 2 * HALO
    nv = rows // SUBLANE
    for cb in range(width // LANE):
        lanes = slice(cb * LANE, (cb + 1) * LANE)
        slot = cb % 2
        window = jnp.concatenate([prev[:, lanes], main_ref[:, lanes], nxt[:, lanes]], axis=0)
        for b in sorted({st % SUBLANE for st in starts}):
            rolled_ref[slot, b] = window if b == 0 else pltpu.roll(window, span - b, axis=0)
        for r0 in range(0, ts, rows):
            acc = jnp.zeros((nv, SUBLANE, LANE), F32)
            for k, st in enumerate(starts):
                shifted = rolled_ref[slot, st % SUBLANE, pl.ds(r0 + st - st % SUBLANE, rows), :]
                acc = acc + shifted.reshape(nv, SUBLANE, LANE) * wb_ref[k, :, lanes][None]
            emit(r0, pl.ds(cb * LANE, LANE), acc.reshape(rows, LANE))


def _window(tile, r0, cb, ts, rows):
    main_ref, prev, nxt = tile
    lanes = slice(cb * LANE, (cb + 1) * LANE)
    lo, hi = max(r0 - HALO, 0), min(r0 + rows + HALO, ts)
    pieces = [prev[:, lanes]] if r0 - HALO < 0 else []
    pieces.append(main_ref[lo:hi, lanes])
    if r0 + rows + HALO > ts:
        pieces.append(nxt[:, lanes])
    return pieces[0] if len(pieces) == 1 else jnp.concatenate(pieces, axis=0)


def _phases(starts):
    groups = {}
    for k, st in enumerate(starts):
        groups.setdefault(st % SUBLANE, []).append((k, st // SUBLANE))
    return sorted(groups.items())


def _shifted(blk, b):
    n = blk.shape[0]
    rolled = blk if b == 0 else pltpu.roll(blk, n - b, axis=0)
    return rolled.reshape(n // SUBLANE, SUBLANE, blk.shape[1])


def _conv_bwd_tile(dv_tile, u_ref, wb_ref, acc_ref, n_taps, ts, width, emit, rows=BWD_CONV_ROWS):
    groups = _phases(_bwd_starts(n_taps))
    nv = rows // SUBLANE
    for r0 in range(0, ts, rows):
        for cb in range(width // LANE):
            lanes = pl.ds(cb * LANE, LANE)
            blk = _window(dv_tile, r0, cb, ts, rows)
            u = u_ref[pl.ds(r0, rows), lanes].reshape(nv, SUBLANE, LANE)
            du = jnp.zeros((nv, SUBLANE, LANE), F32)
            for b, taps in groups:
                sh = _shifted(blk, b)
                for k, m in taps:
                    du = du + sh[m:m + nv] * wb_ref[k, :, lanes][None]
                    acc_ref[k, :, lanes] += jnp.sum(sh[m:m + nv] * u, axis=0)
            emit(r0, lanes, du.reshape(rows, LANE))


def _fwd_starts(n_taps):
    pad = (n_taps - 1) // 2
    return [HALO - pad + k for k in range(n_taps)]


def _bwd_starts(n_taps):
    pad = (n_taps - 1) // 2
    return [HALO + pad - k for k in range(n_taps)]


def _peer(x, y, c, flip):
    fx, fy, fc = flip
    return (1 - x if fx else x, 1 - y if fy else y, 1 - c if fc else c)


def _all_gather(shards, name):
    n = len(shards)

    def body(*refs):
        ins, outs = refs[:n], refs[n:2 * n]
        send_sems, recv_sems, local_sems = refs[2 * n:]
        x, y, c = lax.axis_index("x"), lax.axis_index("y"), lax.axis_index("c")
        me = 4 * x + 2 * y + c
        local = [pltpu.make_async_copy(ins[j], outs[j].at[me], local_sems.at[j]) for j in range(n)]
        for cp in local:
            cp.start()
        sends, recvs = [], []
        for k, flip in enumerate(FLIPS):
            px, py, pc = _peer(x, y, c, flip)
            peer = 4 * px + 2 * py + pc
            for j in range(n):
                sem = k * n + j
                sends.append(pltpu.make_async_remote_copy(
                    src_ref=ins[j], dst_ref=outs[j].at[me], send_sem=send_sems.at[sem], recv_sem=recv_sems.at[sem],
                    device_id=(px, py, pc), device_id_type=MESH))
                recvs.append(pltpu.make_async_remote_copy(
                    src_ref=ins[j], dst_ref=outs[j].at[peer], send_sem=send_sems.at[sem], recv_sem=recv_sems.at[sem],
                    device_id=(px, py, pc), device_id_type=MESH))
        for cp in sends:
            cp.start()
        for cp in recvs:
            cp.wait_recv()
        for cp in sends:
            cp.wait_send()
        for cp in local:
            cp.wait()

    return pl.pallas_call(
        body, name=name,
        out_shape=[jax.ShapeDtypeStruct((N_DEV,) + s.shape, s.dtype) for s in shards],
        in_specs=[_hbm()] * n, out_specs=[_hbm()] * n,
        scratch_shapes=[pltpu.SemaphoreType.DMA((7 * n,)), pltpu.SemaphoreType.DMA((7 * n,)),
                        pltpu.SemaphoreType.DMA((n,))],
    )(*shards)


def _place_own(src, n_slots_shape, me, name):
    rows, width = src.shape
    tr = next(t for t in (256, 128, 64, 32, 16, SUBLANE) if rows % t == 0)

    def body(me_ref, src_ref, out_ref):
        out_ref[...] = src_ref[...]

    return pl.pallas_call(
        body, name=name,
        grid_spec=pltpu.PrefetchScalarGridSpec(
            num_scalar_prefetch=1, grid=(rows // tr,),
            in_specs=[pl.BlockSpec((tr, width), lambda i, me_ref: (i, 0))],
            out_specs=pl.BlockSpec((None, tr, width), lambda i, me_ref: (me_ref[0], i, 0))),
        out_shape=pltpu.HBM(n_slots_shape, src.dtype),
    )(me, src)


def _place_cast(pieces, me, name):
    n = len(pieces)
    counts = [a.shape[1] if t else a.shape[0] for a, t in pieces]
    width = pieces[0][0].shape[0] if pieces[0][1] else pieces[0][0].shape[1]
    total = sum(counts)

    def body(me_ref, *refs):
        ins, own_ref, land_ref = refs[:n], refs[n], refs[n + 1]
        first = 0
        for (a, transpose), in_ref, count in zip(pieces, ins, counts):
            block = (in_ref[...].T if transpose else in_ref[...]).astype(BF16)
            own_ref[first:first + count, :] = block
            land_ref[first:first + count, :] = block
            first += count

    return pl.pallas_call(
        body, name=name,
        grid_spec=pltpu.PrefetchScalarGridSpec(
            num_scalar_prefetch=1, grid=(1,),
            in_specs=[pl.BlockSpec(a.shape, lambda i, me_ref: (0, 0)) for a, _ in pieces],
            out_specs=[pl.BlockSpec((total, width), lambda i, me_ref: (0, 0)),
                       pl.BlockSpec((None, total, width), lambda i, me_ref: (me_ref[0], 0, 0))]),
        out_shape=[pltpu.HBM((total, width), BF16), pltpu.HBM((N_DEV, total, width), BF16)],
        compiler_params=_params(32),
    )(me, *[_in_hbm(a) for a, _ in pieces])


def _all_gather_two_level(shards, placed, name):
    n = len(shards)
    given = [j for j in range(n) if placed[j] is not None]

    def body(*refs):
        ins, outs = refs[:n], refs[n + len(given):2 * n + len(given)]
        send_sems, recv_sems, local_sems = refs[2 * n + len(given):]
        x, y, c = lax.axis_index("x"), lax.axis_index("y"), lax.axis_index("c")
        me, sibling = (x, y, c), (x, y, 1 - c)
        chips = [(1 - x, y), (x, 1 - y), (1 - x, 1 - y)]

        def slot(j, dev):
            return outs[j].at[4 * dev[0] + 2 * dev[1] + dev[2]]

        def copy(k, j, block, to, src=None):
            return pltpu.make_async_remote_copy(
                src_ref=slot(j, block) if src is None else src, dst_ref=slot(j, block),
                send_sem=send_sems.at[k * n + j], recv_sem=recv_sems.at[k * n + j], device_id=to, device_id_type=MESH)

        local = [pltpu.make_async_copy(ins[j], slot(j, me), local_sems.at[j]) for j in range(n) if j not in given]
        for cp in local:
            cp.start()
        first = [copy(0, j, me, sibling, src=ins[j]) for j in range(n)]
        first += [copy(1 + t, j, me, (*chip, c), src=ins[j]) for t, chip in enumerate(chips) for j in range(n)]
        for cp in first:
            cp.start()
        passed = []
        for t, chip in enumerate(chips):
            for j in range(n):
                copy(1 + t, j, (*chip, c), me).wait_recv()
                passed.append(copy(4 + t, j, (*chip, c), sibling))
                passed[-1].start()
        for j in range(n):
            copy(0, j, sibling, me).wait_recv()
        for t, chip in enumerate(chips):
            for j in range(n):
                copy(4 + t, j, (*chip, 1 - c), me).wait_recv()
        for cp in first + passed:
            cp.wait_send()
        for cp in local:
            cp.wait()

    return pl.pallas_call(
        body, name=name,
        out_shape=[jax.ShapeDtypeStruct((N_DEV,) + s.shape, s.dtype) for s in shards],
        in_specs=[_hbm()] * (n + len(given)), out_specs=[_hbm()] * n,
        input_output_aliases={n + i: j for i, j in enumerate(given)},
        scratch_shapes=[pltpu.SemaphoreType.DMA((7 * n,)), pltpu.SemaphoreType.DMA((7 * n,)),
                        pltpu.SemaphoreType.DMA((n,))],
    )(*shards, *[placed[j] for j in given])


class _Part(NamedTuple):
    src: jax.Array
    scatter: bool
    rows: int
    land: int
    off: int


class _Started(NamedTuple):
    send_sems: jax.Array
    recv_sems: jax.Array
    thru: tuple
    token: jax.Array
    parts: tuple


def _exchange_copies(srcs, lands, send_sems, recv_sems, parts):
    n = len(parts)
    x, y, c = lax.axis_index("x"), lax.axis_index("y"), lax.axis_index("c")
    me = 4 * x + 2 * y + c

    def block(j, dev):
        p = parts[j]
        return srcs[j].at[pl.ds(pl.multiple_of(dev * p.rows, SUBLANE), p.rows), :] if p.scatter else srcs[j]

    def slot(j, index):
        p = parts[j]
        return lands[p.land].at[index, pl.ds(p.off, p.rows), :]

    sends, recvs = [], []
    for k, flip in enumerate(FLIPS):
        px, py, pc = _peer(x, y, c, flip)
        peer = 4 * px + 2 * py + pc
        for j in range(n):
            sems = dict(send_sem=send_sems.at[k * n + j], recv_sem=recv_sems.at[k * n + j],
                        device_id=(px, py, pc), device_id_type=MESH)
            to, got = (k, k) if parts[j].scatter else (me, peer)
            sends.append(pltpu.make_async_remote_copy(src_ref=block(j, peer), dst_ref=slot(j, to), **sems))
            recvs.append(pltpu.make_async_remote_copy(src_ref=block(j, peer), dst_ref=slot(j, got), **sems))
    return sends, recvs


def _exchange_start(parts, lands, name, after=None):
    n, nl = len(parts), len(lands)
    n_in = n + nl + (after is not None)

    def body(*refs):
        srcs, land_refs = refs[:n], refs[n:n + nl]
        send_sems, recv_sems = refs[n_in], refs[n_in + 1]
        token = refs[n_in + 2 + n + nl]
        sends, _ = _exchange_copies(srcs, land_refs, send_sems, recv_sems, parts)
        for cp in sends:
            cp.start()
        token[...] = jnp.zeros_like(token)

    hbm = pl.BlockSpec(memory_space=pltpu.HBM)
    sem = pl.BlockSpec(memory_space=pltpu.SEMAPHORE)
    fresh = lambda s: lax.empty(s.shape, s.dtype) if isinstance(s, jax.ShapeDtypeStruct) else s
    args = [pltpu.with_memory_space_constraint(p.src, pltpu.HBM) for p in parts]
    args += [pltpu.with_memory_space_constraint(fresh(s), pltpu.HBM) for s in lands]
    args += [] if after is None else [after]
    out = pl.pallas_call(
        body, name=name,
        out_shape=(pltpu.SemaphoreType.DMA((7 * n,)), pltpu.SemaphoreType.DMA((7 * n,)),
                   *[pltpu.HBM(a.shape, a.dtype) for a in args[:n + nl]], jax.ShapeDtypeStruct((SUBLANE, LANE), F32)),
        in_specs=[hbm] * (n + nl) + [_hbm()] * (after is not None),
        out_specs=(sem, sem, *[hbm] * (n + nl), _whole()),
        input_output_aliases={j: 2 + j for j in range(n + nl)},
        compiler_params=pltpu.CompilerParams(has_side_effects=pltpu.SideEffectType.DATAFLOW_SIDE_EFFECTING),
    )(*args)
    return _Started(out[0], out[1], tuple(out[2:2 + n + nl]), out[2 + n + nl], tuple(parts))


def _exchange_wait(started, name, after):
    parts = started.parts
    n, nl = len(parts), len(started.thru) - len(parts)

    def body(*refs):
        srcs, land_refs = refs[:n], refs[n:n + nl]
        send_sems, recv_sems = refs[n + nl], refs[n + nl + 1]
        sends, recvs = _exchange_copies(srcs, land_refs, send_sems, recv_sems, parts)
        for cp in sends:
            cp.wait_send()
        for cp in recvs:
            cp.wait_recv()

    hbm = pl.BlockSpec(memory_space=pltpu.HBM)
    sem = pl.BlockSpec(memory_space=pltpu.SEMAPHORE)
    out = pl.pallas_call(
        body, name=name,
        out_shape=tuple(pltpu.HBM(a.shape, a.dtype) for a in started.thru),
        in_specs=[hbm] * (n + nl) + [sem, sem, _hbm()], out_specs=tuple([hbm] * (n + nl)),
        input_output_aliases={j: j for j in range(n + nl)},
        compiler_params=pltpu.CompilerParams(has_side_effects=pltpu.SideEffectType.DATAFLOW_SIDE_EFFECTING),
    )(*started.thru, started.send_sems, started.recv_sems, after)
    return list(out[:n]), list(out[n:])


def _fwd_in(x, g1, b_in, slab, ts, deps):
    n_tok, dm = x.shape
    rows, off, _ = _layout(dm)
    width = 7 * dm

    def body(x_ref, g1_ref, b_ref, slab_ref, proj_ref, p_ref, u_ref, h_ref, w_v, sems):
        copies = _weight_copies(slab_ref, off["win"], rows["win"], w_v, sems, 0)
        _on_first_step(copies, "start")
        _on_first_step(copies, "wait")
        xv = x_ref[...]
        h = (xv * _rms(xv) * g1_ref[...]).astype(BF16)
        h_ref[...] = h
        cols = []
        for j in range(7):
            pj = _dot_nt(h, w_v[pl.ds(j * dm, dm), :]) + b_ref[:, j * dm:(j + 1) * dm]
            proj_ref[:, j * dm:(j + 1) * dm] = pj.astype(proj_ref.dtype)
            if 1 <= j <= 4:
                cols.append(pj)
            if j == 2:
                p_ref[...] = cols[0] * cols[1]
            if j == 4:
                u_ref[...] = cols[2] * _sigmoid(cols[3])

    return pl.pallas_call(
        _after(body, deps), name="fwd_in", grid=(n_tok // ts,),
        in_specs=[_whole()] * len(deps) + [_rows(ts, dm), _whole(), _whole(), _hbm()],
        out_specs=[_rows(ts, width), _rows(ts, dm), _rows(ts, dm), _rows(ts, dm)],
        out_shape=[jax.ShapeDtypeStruct((n_tok, width), BF16), jax.ShapeDtypeStruct((n_tok, dm), F32),
                   jax.ShapeDtypeStruct((n_tok, dm), F32), jax.ShapeDtypeStruct((n_tok, dm), BF16)],
        scratch_shapes=[pltpu.VMEM((width, dm), BF16), pltpu.SemaphoreType.DMA((N_DEV,))],
        compiler_params=_params(56),
    )(*deps, x, g1, b_in, slab)


def _fwd_mix(p, u, proj, x, caw, cab, cbw, cbb, lng, lnb, g1post, slab, ts):
    n_tok, dm = x.shape
    rows, off, _ = _layout(dm)
    n_steps = n_tok // ts

    def body(p_ref, p_prev, p_next, u_ref, u_prev, u_next, bg_ref, za_ref, zb_ref, x_ref,
             caw_ref, cab_ref, cbw_ref, cbb_ref, lng_ref, lnb_ref, g1p_ref, slab_ref,
             va_ref, vb_ref, ya_ref, yb_ref, qa_ref, sb_ref, mg_ref, mix_ref, x1_ref,
             wa_v, wb_v, wo_v, tap_a, tap_b, sems, rolled_a, rolled_b):
        i = pl.program_id(0)
        copies = (_weight_copies(slab_ref, off["wa"], rows["wa"], wa_v, sems, 0)
                  + _weight_copies(slab_ref, off["wb"], rows["wb"], wb_v, sems, N_DEV)
                  + _weight_copies(slab_ref, off["wo"], rows["wo"], wo_v, sems, 2 * N_DEV))
        _on_first_step(copies, "start")

        @pl.when(i == 0)
        def _():
            _broadcast_taps(caw_ref, tap_a, CONV_A)
            _broadcast_taps(cbw_ref, tap_b, CONV_B)

        def emit_a(r0, lanes, acc):
            va_ref[pl.ds(r0, acc.shape[0]), lanes] = acc + cab_ref[:, lanes]

        def emit_b(r0, lanes, acc):
            vb_ref[pl.ds(r0, acc.shape[0]), lanes] = acc + cbb_ref[:, lanes]

        _conv_tile(_with_halos(p_ref, p_prev, p_next, i, n_steps), tap_a, _fwd_starts(CONV_A), ts, dm, emit_a,
                   rolled_a)
        _conv_tile(_with_halos(u_ref, u_prev, u_next, i, n_steps), tap_b, _fwd_starts(CONV_B), ts, dm, emit_b,
                   rolled_b)
        _on_first_step(copies, "wait")

        qa = (bg_ref[...].astype(F32) * va_ref[...]).astype(BF16)
        qa_ref[...] = qa
        ya = _dot(qa, wa_v[...])
        vb = vb_ref[...]
        xc = vb - jnp.mean(vb, axis=-1, keepdims=True)
        rstd = lax.rsqrt(jnp.mean(xc * xc, axis=-1, keepdims=True) + LN_EPS)
        ln = xc * rstd * lng_ref[...] + lnb_ref[...]
        sb = (ln * _sigmoid(ln)).astype(BF16)
        sb_ref[...] = sb
        yb = _dot(sb, wb_v[...])
        ya_ref[...] = ya.astype(BF16)
        yb_ref[...] = yb.astype(BF16)
        merged = (_sigmoid(za_ref[...].astype(F32)) * ya + _sigmoid(zb_ref[...].astype(F32)) * yb).astype(BF16)
        mg_ref[...] = merged
        mix = _dot(merged, wo_v[...])
        mix_ref[...] = mix
        x1_ref[...] = x_ref[...] + mix * _rms(mix) * g1p_ref[...]

    tok = lambda dt: jax.ShapeDtypeStruct((n_tok, dm), dt)
    return pl.pallas_call(
        body, name="fwd_mix", grid=(n_steps,),
        in_specs=(_halo_specs(ts, dm, n_tok) + _halo_specs(ts, dm, n_tok)
                  + [_rows(ts, dm, 0), _rows(ts, dm, 5), _rows(ts, dm, 6), _rows(ts, dm)]
                  + [_whole()] * 7 + [_hbm()]),
        out_specs=[_rows(ts, dm)] * 9,
        out_shape=[tok(F32), tok(F32), tok(BF16), tok(BF16), tok(BF16), tok(BF16), tok(BF16), tok(F32), tok(F32)],
        scratch_shapes=[pltpu.VMEM((dm, dm), BF16), pltpu.VMEM((dm, dm), BF16), pltpu.VMEM((dm, dm), BF16),
                        pltpu.VMEM((CONV_A, SUBLANE, dm), F32), pltpu.VMEM((CONV_B, SUBLANE, dm), F32),
                        pltpu.SemaphoreType.DMA((3 * N_DEV,)),
                        pltpu.VMEM((2, SUBLANE, ts + 2 * HALO, LANE), F32),
                        pltpu.VMEM((2, SUBLANE, ts + 2 * HALO, LANE), F32)],
        compiler_params=_params(48),
    )(p, p, p, u, u, u, proj, proj, proj, x, caw, cab, cbw, cbb, lng, lnb, g1post, slab)


def _mlp_fwd_bwd(x1, mix, tgt, g1post, g2pre, g2post, slab, ts):
    n_tok, dm = x1.shape
    rows, off, _ = _layout(dm)
    ff = 4 * dm

    def body(x1_ref, mix_ref, t_ref, g1p_ref, g2pre_ref, g2post_ref, slab_ref,
             f_ref, df1_ref, h2_ref, df2_ref, dmix_ref, dx1_ref, small_ref, w1_v, w2_v, relu_v, sems):
        w1_copies = _weight_copies(slab_ref, off["w1"], rows["w1"], w1_v, sems, 0)
        w2_copies = _weight_copies(slab_ref, off["w2"], rows["w2"], w2_v, sems, N_DEV)
        _on_first_step(w1_copies + w2_copies, "start")

        @pl.when(pl.program_id(0) == 0)
        def _():
            small_ref[...] = jnp.zeros_like(small_ref)

        _on_first_step(w1_copies + w2_copies, "wait")
        x1v = x1_ref[...]
        r3 = _rms(x1v)
        g2pre = g2pre_ref[...]
        h2 = (x1v * r3 * g2pre).astype(BF16)
        h2_ref[...] = h2
        for c in range(4):
            blk = pl.ds(c * dm, dm)
            relu = jnp.maximum(_dot_nt(h2, w1_v[blk, :]), 0.0)
            relu_v[:, c * dm:(c + 1) * dm] = relu
            f_ref[:, c * dm:(c + 1) * dm] = (relu * relu).astype(BF16)
        f2 = _dot(f_ref[...], w2_v[...])
        r4 = _rms(f2)
        g2post = g2post_ref[...]
        err = x1v + f2 * r4 * g2post - t_ref[...]
        dy = err * (1.0 / dm)
        small_ref[3:4, :] += _colsum(err * err)
        small_ref[0:1, :] += _colsum(dy * f2 * r4)
        df2 = _rms_bwd(dy, f2, r4, g2post).astype(BF16)
        df2_ref[...] = df2
        for c in range(4):
            blk = pl.ds(c * dm, dm)
            df1 = (_dot_nt(df2, w2_v[blk, :]) * (2.0 * relu_v[:, c * dm:(c + 1) * dm])).astype(BF16)
            df1_ref[:, c * dm:(c + 1) * dm] = df1
        dh2 = _dot(df1_ref[...], w1_v[...])
        small_ref[1:2, :] += _colsum(dh2 * x1v * r3)
        dx1 = dy + _rms_bwd(dh2, x1v, r3, g2pre)
        dx1_ref[...] = dx1
        mixv = mix_ref[...]
        r2 = _rms(mixv)
        small_ref[2:3, :] += _colsum(dx1 * mixv * r2)
        dmix_ref[...] = _rms_bwd(dx1, mixv, r2, g1p_ref[...]).astype(BF16)

    tok = lambda w, dt: jax.ShapeDtypeStruct((n_tok, w), dt)
    return pl.pallas_call(
        body, name="mlp_fwd_bwd", grid=(n_tok // ts,),
        in_specs=[_rows(ts, dm)] * 3 + [_whole()] * 3 + [_hbm()],
        out_specs=[_rows(ts, ff), _rows(ts, ff), _rows(ts, dm), _rows(ts, dm), _rows(ts, dm), _rows(ts, dm),
                   pl.BlockSpec((SUBLANE, dm), lambda i: (0, 0))],
        out_shape=[tok(ff, BF16), tok(ff, BF16), tok(dm, BF16), tok(dm, BF16), tok(dm, BF16), tok(dm, F32),
                   jax.ShapeDtypeStruct((SUBLANE, dm), F32)],
        scratch_shapes=[pltpu.VMEM((ff, dm), BF16), pltpu.VMEM((ff, dm), BF16), pltpu.VMEM((ts, ff), F32),
                        pltpu.SemaphoreType.DMA((2 * N_DEV,))],
        compiler_params=_params(56),
    )(x1, mix, tgt, g1post, g2pre, g2post, slab)


def _bwd_mix(dmix, ya, yb, proj, va, vb, lng, lnb, slab, ts, deps):
    n_tok, dm = dmix.shape
    rows, off, _ = _layout(dm)
    n_steps = n_tok // ts

    def body(dmix_ref, ya_ref, yb_ref, bg_ref, za_ref, zb_ref, va_ref, vb_ref, lng_ref, lnb_ref, slab_ref,
             dpa_ref, dya_ref, dyb_ref, dva_ref, dvb_ref, small_ref, wa_v, wb_v, wo_v, sems):
        wo_copies = _weight_copies(slab_ref, off["wo"], rows["wo"], wo_v, sems, 2 * N_DEV)
        ab_copies = (_weight_copies(slab_ref, off["wa"], rows["wa"], wa_v, sems, 0)
                     + _weight_copies(slab_ref, off["wb"], rows["wb"], wb_v, sems, N_DEV))
        _on_first_step(wo_copies + ab_copies, "start")

        @pl.when(pl.program_id(0) == 0)
        def _():
            small_ref[...] = jnp.zeros_like(small_ref)

        _on_first_step(wo_copies, "wait")
        dmerged = _dot_nt(dmix_ref[...], wo_v[...])
        _on_first_step(ab_copies, "wait")
        sa = _sigmoid(za_ref[...].astype(F32))
        sg = _sigmoid(zb_ref[...].astype(F32))
        dza = dmerged * ya_ref[...].astype(F32) * sa * (1.0 - sa)
        dzb = dmerged * yb_ref[...].astype(F32) * sg * (1.0 - sg)
        dpa_ref[:, dm:2 * dm] = dza.astype(BF16)
        dpa_ref[:, 2 * dm:3 * dm] = dzb.astype(BF16)
        small_ref[5:6, :] += _colsum(dza)
        small_ref[6:7, :] += _colsum(dzb)

        dya = (dmerged * sa).astype(BF16)
        dya_ref[...] = dya
        dqa = _dot_nt(dya, wa_v[...])
        dbg = dqa * va_ref[...]
        dpa_ref[:, 0:dm] = dbg.astype(BF16)
        small_ref[4:5, :] += _colsum(dbg)
        dva = dqa * bg_ref[...].astype(F32)
        dva_ref[...] = dva
        small_ref[2:3, :] += _colsum(dva)

        dyb = (dmerged * sg).astype(BF16)
        dyb_ref[...] = dyb
        dsb = _dot_nt(dyb, wb_v[...])
        vb = vb_ref[...]
        xc = vb - jnp.mean(vb, axis=-1, keepdims=True)
        rstd = lax.rsqrt(jnp.mean(xc * xc, axis=-1, keepdims=True) + LN_EPS)
        nrm = xc * rstd
        lng_v = lng_ref[...]
        ln = nrm * lng_v + lnb_ref[...]
        sl = _sigmoid(ln)
        dln = dsb * (sl * (1.0 + ln * (1.0 - sl)))
        small_ref[0:1, :] += _colsum(dln * nrm)
        small_ref[1:2, :] += _colsum(dln)
        dn = dln * lng_v
        dvb = rstd * (dn - jnp.mean(dn, axis=-1, keepdims=True)
                      - nrm * jnp.mean(dn * nrm, axis=-1, keepdims=True))
        dvb_ref[...] = dvb
        small_ref[3:4, :] += _colsum(dvb)

    tok = lambda w, dt: jax.ShapeDtypeStruct((n_tok, w), dt)
    return pl.pallas_call(
        _after(body, deps), name="bwd_mix", grid=(n_steps,),
        in_specs=([_whole()] * len(deps) + [_rows(ts, dm)] * 3
                  + [_rows(ts, dm, 0), _rows(ts, dm, 5), _rows(ts, dm, 6)]
                  + [_rows(ts, dm)] * 2 + [_whole()] * 2 + [_hbm()]),
        out_specs=[_rows(ts, 3 * dm), _rows(ts, dm), _rows(ts, dm), _rows(ts, dm), _rows(ts, dm),
                   pl.BlockSpec((SUBLANE, dm), lambda i: (0, 0))],
        out_shape=[tok(3 * dm, BF16), tok(dm, BF16), tok(dm, BF16), tok(dm, F32), tok(dm, F32),
                   jax.ShapeDtypeStruct((SUBLANE, dm), F32)],
        scratch_shapes=[pltpu.VMEM((dm, dm), BF16), pltpu.VMEM((dm, dm), BF16), pltpu.VMEM((dm, dm), BF16),
                        pltpu.SemaphoreType.DMA((3 * N_DEV,))],
        compiler_params=_params(56),
    )(*deps, dmix, ya, yb, proj, proj, proj, va, vb, lng, lnb, slab)


def _bwd_conv(dva, dvb, p, u, proj, dpa, caw, cbw, ts, deps):
    n_tok, dm = dva.shape
    n_steps = n_tok // ts
    small_rows = 40

    def body(dva_ref, dva_prev, dva_next, dvb_ref, dvb_prev, dvb_next, p_ref, u_ref,
             cg_ref, ha_ref, a_ref, g_ref, dpa_ref, caw_ref, cbw_ref,
             dproj_ref, small_ref,
             dp_v, du_v, tap_a, tap_b, gwa_v, gwb_v):
        i = pl.program_id(0)

        @pl.when(i == 0)
        def _():
            _broadcast_taps(caw_ref, tap_a, CONV_A)
            _broadcast_taps(cbw_ref, tap_b, CONV_B)
            small_ref[...] = jnp.zeros_like(small_ref)
            gwa_v[...] = jnp.zeros_like(gwa_v)
            gwb_v[...] = jnp.zeros_like(gwb_v)

        def emit_dp(r0, lanes, acc):
            dp_v[pl.ds(r0, acc.shape[0]), lanes] = acc

        def emit_du(r0, lanes, acc):
            du_v[pl.ds(r0, acc.shape[0]), lanes] = acc

        _conv_bwd_tile(_with_halos(dva_ref, dva_prev, dva_next, i, n_steps), p_ref, tap_a, gwa_v, CONV_A, ts, dm,
                       emit_dp)
        _conv_bwd_tile(_with_halos(dvb_ref, dvb_prev, dvb_next, i, n_steps), u_ref, tap_b, gwb_v, CONV_B, ts, dm,
                       emit_du)

        dp = dp_v[...]
        dcg = dp * ha_ref[...].astype(F32)
        dha = dp * cg_ref[...].astype(F32)
        du = du_v[...]
        sg = _sigmoid(g_ref[...].astype(F32))
        da = du * sg
        dg = du * a_ref[...].astype(F32) * sg * (1.0 - sg)
        dproj_ref[:, 0:dm] = dpa_ref[:, 0:dm]
        dproj_ref[:, dm:2 * dm] = dcg.astype(BF16)
        dproj_ref[:, 2 * dm:3 * dm] = dha.astype(BF16)
        dproj_ref[:, 3 * dm:4 * dm] = da.astype(BF16)
        dproj_ref[:, 4 * dm:5 * dm] = dg.astype(BF16)
        dproj_ref[:, 5 * dm:7 * dm] = dpa_ref[:, dm:3 * dm]
        small_ref[3:4, :] += _colsum(dcg)
        small_ref[4:5, :] += _colsum(dha)
        small_ref[5:6, :] += _colsum(da)
        small_ref[6:7, :] += _colsum(dg)

        @pl.when(i == n_steps - 1)
        def _():
            for k in range(CONV_A):
                small_ref[k:k + 1, :] = _colsum(gwa_v[k])
            for k in range(CONV_B):
                small_ref[SUBLANE + k:SUBLANE + k + 1, :] = _colsum(gwb_v[k])

    return pl.pallas_call(
        _after(body, deps), name="bwd_conv", grid=(n_steps,),
        in_specs=([_whole()] * len(deps) + _halo_specs(ts, dm, n_tok) * 2 + [_rows(ts, dm)] * 2
                  + [_rows(ts, dm, 1), _rows(ts, dm, 2), _rows(ts, dm, 3), _rows(ts, dm, 4), _rows(ts, 3 * dm)]
                  + [_whole()] * 2),
        out_specs=[_rows(ts, 7 * dm), pl.BlockSpec((small_rows, dm), lambda i: (0, 0))],
        out_shape=[jax.ShapeDtypeStruct((n_tok, 7 * dm), BF16), jax.ShapeDtypeStruct((small_rows, dm), F32)],
        scratch_shapes=[pltpu.VMEM((ts, dm), F32), pltpu.VMEM((ts, dm), F32),
                        pltpu.VMEM((CONV_A, SUBLANE, dm), F32), pltpu.VMEM((CONV_B, SUBLANE, dm), F32),
                        pltpu.VMEM((CONV_A, SUBLANE, dm), F32), pltpu.VMEM((CONV_B, SUBLANE, dm), F32)],
        compiler_params=_params(48),
    )(*deps, dva, dva, dva, dvb, dvb, dvb, p, u, proj, proj, proj, proj, dpa, caw, cbw)


def _bwd_in(dproj, x, dx1, g1, slab, ts, deps):
    n_tok, dm = x.shape
    rows, off, _ = _layout(dm)
    width = 7 * dm

    def body(dproj_ref, x_ref, dx1_ref, g1_ref, slab_ref, gx_ref, small_ref, w_v, sems):
        copies = _weight_copies(slab_ref, off["win"], rows["win"], w_v, sems, 0)
        _on_first_step(copies, "start")

        @pl.when(pl.program_id(0) == 0)
        def _():
            small_ref[...] = jnp.zeros_like(small_ref)

        _on_first_step(copies, "wait")
        dh = _dot(dproj_ref[...], w_v[...])
        xv = x_ref[...]
        r1 = _rms(xv)
        small_ref[0:1, :] += _colsum(dh * xv * r1)
        gx_ref[...] = dx1_ref[...] + _rms_bwd(dh, xv, r1, g1_ref[...])

    return pl.pallas_call(
        _after(body, deps), name="bwd_in", grid=(n_tok // ts,),
        in_specs=[_whole()] * len(deps) + [_rows(ts, width), _rows(ts, dm), _rows(ts, dm), _whole(), _hbm()],
        out_specs=[_rows(ts, dm), pl.BlockSpec((SUBLANE, dm), lambda i: (0, 0))],
        out_shape=[jax.ShapeDtypeStruct((n_tok, dm), F32), jax.ShapeDtypeStruct((SUBLANE, dm), F32)],
        scratch_shapes=[pltpu.VMEM((width, dm), BF16), pltpu.SemaphoreType.DMA((N_DEV,))],
        compiler_params=_params(56),
    )(*deps, dproj, x, dx1, g1, slab)


def _wgrad(a, b, name, tm, tk, out_dtype):
    n_tok, m = a.shape
    n = b.shape[1]
    k_steps = n_tok // tk

    def body(a_ref, b_ref, o_ref, acc_v):
        k = pl.program_id(1)

        @pl.when(k == 0)
        def _():
            acc_v[...] = jnp.zeros_like(acc_v)

        acc_v[...] += _dot_tn(a_ref[...], b_ref[...])

        @pl.when(k == k_steps - 1)
        def _():
            o_ref[...] = acc_v[...].astype(o_ref.dtype)

    return pl.pallas_call(
        body, name=name, grid=(m // tm, k_steps),
        in_specs=[pl.BlockSpec((tk, tm), lambda i, k: (k, i)), pl.BlockSpec((tk, n), lambda i, k: (k, 0))],
        out_specs=pl.BlockSpec((tm, n), lambda i, k: (i, 0)),
        out_shape=pltpu.HBM((m, n), out_dtype),
        scratch_shapes=[pltpu.VMEM((tm, n), F32)],
        compiler_params=pltpu.CompilerParams(dimension_semantics=("arbitrary", "arbitrary"),
                                             vmem_limit_bytes=40 * MIB),
    )(a, b)


def _adamw(w, g, m, v):
    m = ADAM_B1 * m + (1.0 - ADAM_B1) * g
    v = ADAM_B2 * v + (1.0 - ADAM_B2) * (g * g)
    m_hat = m / (1.0 - ADAM_B1 ** ADAM_STEP)
    v_hat = v / (1.0 - ADAM_B2 ** ADAM_STEP)
    delta = -ADAM_LR * (m_hat / (jnp.sqrt(v_hat) + ADAM_EPS) + ADAM_WD * w)
    return delta, m, v


def _adam_big(recv, part, me, off, rows, w, m, v, transpose, name, tr):
    dm = recv.shape[2]
    per = rows // tr

    def body(me_ref, own_ref, r_ref, w_ref, m_ref, v_ref, g_ref, d_ref, mo_ref, vo_ref):
        g = own_ref[...].astype(F32)
        for k in range(len(FLIPS)):
            g = g + r_ref[k].astype(F32)
        if transpose:
            g = g.T
        delta, m_new, v_new = _adamw(w_ref[...], g, m_ref[...], v_ref[...])
        g_ref[...] = g
        d_ref[...] = delta
        mo_ref[...] = m_new
        vo_ref[...] = v_new

    if transpose:
        blk = pl.BlockSpec((dm, tr), lambda i, me_ref: (0, i))
    else:
        blk = pl.BlockSpec((tr, dm), lambda i, me_ref: (i, 0))
    first = off // tr
    return pl.pallas_call(
        body, name=name,
        grid_spec=pltpu.PrefetchScalarGridSpec(
            num_scalar_prefetch=1, grid=(per,),
            in_specs=[pl.BlockSpec((tr, dm), lambda i, me_ref: (me_ref[0] * per + i, 0)),
                      pl.BlockSpec((len(FLIPS), tr, dm), lambda i, me_ref: (0, first + i, 0)), blk, blk, blk],
            out_specs=[blk] * 4),
        out_shape=[jax.ShapeDtypeStruct(w.shape, F32)] * 4,
        compiler_params=_params(32),
    )(me, *[_in_hbm(a) for a in (part, recv, w, m, v)])


LOSS_ROW = 15
CONV_A_ROW = 16
CONV_B_ROW = 24


def _adam_small(recv_small, recv_last, me, params, d_model):
    n = len(params)
    cw = d_model // N_DEV

    def body(me_ref, r_ref, rc_ref, l_ref, *refs):
        ins, loss_ref, outs = refs[:3 * n], refs[3 * n], refs[3 * n + 1:3 * n + 1 + 4 * n]
        g_v, gc_v, last_v = refs[3 * n + 1 + 4 * n:]
        g, gc, last = r_ref[0], rc_ref[0], l_ref[0]
        for d in range(1, N_DEV):
            g, gc, last = g + r_ref[d], gc + rc_ref[d], last + l_ref[d]
        g_v[...], gc_v[...], last_v[...] = g, gc, last
        loss_ref[...] = (0.5 / d_model) * jnp.sum(g_v[LOSS_ROW:LOSS_ROW + 1, :], axis=-1, keepdims=True)
        for j, (row0, own_columns, (w, _, _)) in enumerate(params):
            w_ref, m_ref, v_ref = ins[3 * j:3 * j + 3]
            source = gc_v if own_columns else (last_v if row0 == 0 else g_v)
            grad = source[row0:row0 + w.shape[0], :]
            delta, m_new, v_new = _adamw(w_ref[...], grad, m_ref[...], v_ref[...])
            for ref, val in zip(outs[4 * j:4 * j + 4], (grad, delta, m_new, v_new)):
                ref[...] = val

    full = lambda shape: pl.BlockSpec(shape, lambda i, me_ref: (0,) * len(shape))
    stack_rows = recv_small.shape[1]
    flat = [a for _, _, triple in params for a in triple]
    shapes = [w.shape for _, _, (w, _, _) in params for _ in range(4)]
    out = pl.pallas_call(
        body, name="adam_small",
        grid_spec=pltpu.PrefetchScalarGridSpec(
            num_scalar_prefetch=1, grid=(1,),
            in_specs=[full(recv_small.shape),
                      pl.BlockSpec((N_DEV, stack_rows, cw), lambda i, me_ref: (0, 0, me_ref[0])),
                      full(recv_last.shape)] + [full(a.shape) for a in flat],
            out_specs=[full((1, 1))] + [full(s) for s in shapes],
            scratch_shapes=[pltpu.VMEM((stack_rows, d_model), F32), pltpu.VMEM((stack_rows, cw), F32),
                            pltpu.VMEM(recv_last.shape[1:], F32)]),
        out_shape=[jax.ShapeDtypeStruct((1, 1), F32)] + [jax.ShapeDtypeStruct(s, F32) for s in shapes],
    )(me, *[_in_hbm(a) for a in (recv_small, recv_small, recv_last, *flat)])
    return out[0], [tuple(out[1 + 4 * j:5 + 4 * j]) for j in range(n)]


def _tile(n_tok, want):
    return min(want, n_tok)


def kernel(x, norm1_pre_g, w_in, b_in, conv_a_w, conv_a_b, w_a_out, conv_b_w, conv_b_b, ln_b_g, ln_b_b, w_b_out, w_o, norm1_post_g, norm2_pre_g, w_mlp_in, w_mlp_out, norm2_post_g, loss_target, m_norm1_pre_g, m_w_in, m_b_in, m_conv_a_w, m_conv_a_b, m_w_a_out, m_conv_b_w, m_conv_b_b, m_ln_b_g, m_ln_b_b, m_w_b_out, m_w_o, m_norm1_post_g, m_norm2_pre_g, m_w_mlp_in, m_w_mlp_out, m_norm2_post_g, v_norm1_pre_g, v_w_in, v_b_in, v_conv_a_w, v_conv_a_b, v_w_a_out, v_conv_b_w, v_conv_b_b, v_ln_b_g, v_ln_b_b, v_w_b_out, v_w_o, v_norm1_post_g, v_norm2_pre_g, v_w_mlp_in, v_w_mlp_out, v_norm2_post_g):
    n_tok, dm = x.shape[1], x.shape[2]
    rows, off, slab_rows = _layout(dm)
    cw = dm // N_DEV
    xs = x.reshape(n_tok, dm)
    tgt = loss_target.reshape(n_tok, dm)
    row = lambda vec: vec.reshape(1, -1)
    scattered = lambda group: jax.ShapeDtypeStruct((len(FLIPS), slab_rows[group], dm), BF16)
    tm, tk = min(dm, 1024), _tile(n_tok, 2048)
    me = (4 * lax.axis_index("x") + 2 * lax.axis_index("y") + lax.axis_index("c")).astype(jnp.int32).reshape(1)

    conv_own = jnp.concatenate([conv_a_w, jnp.zeros((SUBLANE - CONV_A, cw), F32), conv_b_w,
                                jnp.zeros((1, cw), F32)], axis=0)
    own_in, land_in = _place_cast([(w_in, True)], me, "place_w_in")
    slab_in, conv_all = _all_gather_two_level([own_in, conv_own], [land_in, None], "gather_w_in")
    conv_full = conv_all.transpose(1, 0, 2).reshape(conv_own.shape[0], dm)
    caw, cbw = conv_full[0:CONV_A], conv_full[SUBLANE:SUBLANE + CONV_B]
    own_abo, land_abo = _place_cast([(w_a_out, False), (w_b_out, False), (w_o, False)], me, "place_abo")
    own_mlp, land_mlp = _place_cast([(w_mlp_in, True), (w_mlp_out, False)], me, "place_mlp")
    ag_abo = _exchange_start([_Part(own_abo, False, slab_rows["abo"], 0, 0)], [land_abo],
                             "gather_abo_start", after=slab_in)
    ag_mlp = _exchange_start([_Part(own_mlp, False, slab_rows["mlp"], 0, 0)], [land_mlp],
                             "gather_mlp_start", after=ag_abo.token)

    proj, p, u, h = _fwd_in(xs, row(norm1_pre_g), row(b_in), slab_in, _tile(n_tok, 512),
                            [ag_abo.token, ag_mlp.token])
    _, (slab_abo,) = _exchange_wait(ag_abo, "gather_abo_wait", after=proj)
    va, vb, ya, yb, qa, sb, merged, mix, x1 = _fwd_mix(
        p, u, proj, xs, caw, row(conv_a_b), cbw, row(conv_b_b), row(ln_b_g), row(ln_b_b), row(norm1_post_g),
        slab_abo, _tile(n_tok, 256))
    _, (slab_mlp,) = _exchange_wait(ag_mlp, "gather_mlp_wait", after=x1)
    f, df1, h2, df2, dmix, dx1, small_mlp = _mlp_fwd_bwd(
        x1, mix, tgt, row(norm1_post_g), row(norm2_pre_g), row(norm2_post_g), slab_mlp, _tile(n_tok, 256))

    rs_mlp = _exchange_start(
        [_Part(_wgrad(df1, h2, "wgrad_mlp_in", tm, tk, BF16), True, rows["w1"], 0, off["w1"]),
         _Part(_wgrad(f, df2, "wgrad_mlp_out", tm, tk, BF16), True, rows["w2"], 0, off["w2"])],
        [scattered("mlp")], "scatter_mlp_start")
    dpa, dya, dyb, dva, dvb, small_mix = _bwd_mix(
        dmix, ya, yb, proj, va, vb, row(ln_b_g), row(ln_b_b), slab_abo, _tile(n_tok, 512), [rs_mlp.token])
    rs_abo = _exchange_start(
        [_Part(_wgrad(qa, dya, "wgrad_a_out", tm, tk, BF16), True, rows["wa"], 0, off["wa"]),
         _Part(_wgrad(sb, dyb, "wgrad_b_out", tm, tk, BF16), True, rows["wb"], 0, off["wb"]),
         _Part(_wgrad(merged, dmix, "wgrad_o", tm, tk, BF16), True, rows["wo"], 0, off["wo"])],
        [scattered("abo")], "scatter_abo_start")
    dproj, small_conv = _bwd_conv(dva, dvb, p, u, proj, dpa, caw, cbw, _tile(n_tok, 256), [rs_abo.token])

    zeros = lambda r: jnp.zeros((r, dm), F32)
    small = jnp.concatenate([
        zeros(1),
        small_mix[2:3],
        small_mix[3:4],
        small_mix[0:2],
        small_mlp[2:3],
        small_mlp[1:2],
        small_mlp[0:1],
        small_mix[4:5], small_conv[3:7], small_mix[5:7],
        small_mlp[3:4],
        small_conv[0:CONV_A], zeros(SUBLANE - CONV_A),
        small_conv[8:8 + CONV_B], zeros(1),
    ], axis=0)

    rs_in = _exchange_start(
        [_Part(_wgrad(dproj, h, "wgrad_in", tm, tk, BF16), True, rows["win"], 0, off["win"]),
         _Part(small, False, small.shape[0], 1, 0)],
        [scattered("in"), _place_own(small, (N_DEV,) + small.shape, me, "place_small")], "scatter_in_start")
    grad_x, small_in = _bwd_in(dproj, xs, dx1, row(norm1_pre_g), slab_in, _tile(n_tok, 512), [rs_in.token])

    tr = min(LANE, rows["wa"])
    (g_w1, g_w2), (recv_mlp,) = _exchange_wait(rs_mlp, "scatter_mlp_wait", after=grad_x)
    (g_wa, g_wb, g_wo), (recv_abo,) = _exchange_wait(rs_abo, "scatter_abo_wait", after=grad_x)
    big = {
        "w_mlp_in": _adam_big(recv_mlp, g_w1, me, off["w1"], rows["w1"], w_mlp_in, m_w_mlp_in, v_w_mlp_in, True,
                              "adam_w_mlp_in", tr),
        "w_mlp_out": _adam_big(recv_mlp, g_w2, me, off["w2"], rows["w2"], w_mlp_out, m_w_mlp_out, v_w_mlp_out,
                               False, "adam_w_mlp_out", tr),
        "w_a_out": _adam_big(recv_abo, g_wa, me, off["wa"], rows["wa"], w_a_out, m_w_a_out, v_w_a_out, False,
                             "adam_w_a_out", tr),
        "w_b_out": _adam_big(recv_abo, g_wb, me, off["wb"], rows["wb"], w_b_out, m_w_b_out, v_w_b_out, False,
                             "adam_w_b_out", tr),
        "w_o": _adam_big(recv_abo, g_wo, me, off["wo"], rows["wo"], w_o, m_w_o, v_w_o, False, "adam_w_o", tr),
    }
    (g_win, _), (recv_in, recv_small) = _exchange_wait(rs_in, "scatter_in_wait", after=big["w_o"][3])
    recv_last, = _all_gather([small_in], "gather_last")
    big["w_in"] = _adam_big(recv_in, g_win, me, off["win"], rows["win"], w_in, m_w_in, v_w_in, True, "adam_w_in", tr)

    small_names = ("norm1_pre_g", "conv_a_b", "conv_b_b", "ln_b_g", "ln_b_b", "norm1_post_g", "norm2_pre_g",
                   "norm2_post_g")
    given = dict(
        norm1_pre_g=(norm1_pre_g, m_norm1_pre_g, v_norm1_pre_g), conv_a_b=(conv_a_b, m_conv_a_b, v_conv_a_b),
        conv_b_b=(conv_b_b, m_conv_b_b, v_conv_b_b), ln_b_g=(ln_b_g, m_ln_b_g, v_ln_b_g),
        ln_b_b=(ln_b_b, m_ln_b_b, v_ln_b_b), norm1_post_g=(norm1_post_g, m_norm1_post_g, v_norm1_post_g),
        norm2_pre_g=(norm2_pre_g, m_norm2_pre_g, v_norm2_pre_g),
        norm2_post_g=(norm2_post_g, m_norm2_post_g, v_norm2_post_g))
    params = [(j, False, tuple(row(a) for a in given[name])) for j, name in enumerate(small_names)]
    params.append((SUBLANE, False, tuple(a.reshape(7, dm) for a in (b_in, m_b_in, v_b_in))))
    params.append((CONV_A_ROW, True, (conv_a_w, m_conv_a_w, v_conv_a_w)))
    params.append((CONV_B_ROW, True, (conv_b_w, m_conv_b_w, v_conv_b_w)))
    loss, small_out = _adam_small(recv_small, recv_last, me, params, dm)
    small_leaves = {name: tuple(a.reshape(dm) for a in small_out[j]) for j, name in enumerate(small_names)}
    small_leaves["b_in"] = tuple(a.reshape(7 * dm) for a in small_out[len(small_names)])
    small_leaves["conv_a_w"] = small_out[len(small_names) + 1]
    small_leaves["conv_b_w"] = small_out[len(small_names) + 2]

    order = ("norm1_pre_g", "w_in", "b_in", "conv_a_w", "conv_a_b", "w_a_out", "conv_b_w", "conv_b_b", "ln_b_g",
             "ln_b_b", "w_b_out", "w_o", "norm1_post_g", "norm2_pre_g", "w_mlp_in", "w_mlp_out", "norm2_post_g")
    leaves = [big[name] if name in big else small_leaves[name] for name in order]
    grads, deltas, new_m, new_v = zip(*leaves)
    return (loss.reshape(()), grad_x.reshape(x.shape), *grads, *deltas, *new_m, *new_v)
```

```python
from typing import NamedTuple

import jax
import jax.numpy as jnp
from jax import lax
from jax.experimental import pallas as pl
from jax.experimental.pallas import tpu as pltpu

F32 = jnp.float32
BF16 = jnp.bfloat16

RMS_EPS = 1e-6
LN_EPS = 1e-5
ADAM_LR = 0.001
ADAM_B1 = 0.9
ADAM_B2 = 0.999
ADAM_EPS = 1e-08
ADAM_WD = 0.01
ADAM_STEP = 10

N_DEV = 8
CONV_A = 3
CONV_B = 31
LANE = 128
SUBLANE = 8
HALO = 16
FWD_CONV_ROWS = 32
BWD_CONV_ROWS = 64
MIB = 1 << 20
FLIPS = ((0, 0, 1), (0, 1, 0), (1, 0, 0), (0, 1, 1), (1, 0, 1), (1, 1, 0), (1, 1, 1))
MESH = pl.DeviceIdType.MESH


def _layout(d_model):
    e = d_model // N_DEV
    rows = {"win": 7 * e, "w1": 4 * e, "w2": 4 * e, "wa": e, "wb": e, "wo": e}
    off = {"win": 0, "w1": 0, "w2": 4 * e, "wa": 0, "wb": e, "wo": 2 * e}
    return rows, off, {"in": 7 * e, "mlp": 8 * e, "abo": 3 * e}


def _after(body, deps):
    def ordered(*refs):
        return body(*refs[len(deps):])
    return ordered


def _params(vmem_mib):
    return pltpu.CompilerParams(dimension_semantics=("arbitrary",), vmem_limit_bytes=vmem_mib * MIB)


def _whole():
    return pl.BlockSpec(memory_space=pltpu.VMEM)


def _hbm():
    return pl.BlockSpec(memory_space=pl.ANY)


def _in_hbm(a):
    return pltpu.with_memory_space_constraint(a, pltpu.HBM)


def _rows(ts, width, col=0):
    return pl.BlockSpec((ts, width), lambda i: (i, col))


def _halo_specs(ts, width, n_rows):
    per = ts // HALO
    last = n_rows // HALO - 1
    return [
        pl.BlockSpec((ts, width), lambda i: (i, 0)),
        pl.BlockSpec((HALO, width), lambda i: (jnp.maximum(i * per - 1, 0), 0)),
        pl.BlockSpec((HALO, width), lambda i: (jnp.minimum((i + 1) * per, last), 0)),
    ]


def _dot(a, b):
    return jnp.dot(a, b, preferred_element_type=F32)


def _dot_nt(a, b):
    return lax.dot_general(a, b, (((1,), (1,)), ((), ())), preferred_element_type=F32)


def _dot_tn(a, b):
    return lax.dot_general(a, b, (((0,), (0,)), ((), ())), preferred_element_type=F32)


def _rms(u):
    return lax.rsqrt(jnp.mean(u * u, axis=-1, keepdims=True) + RMS_EPS)


def _rms_bwd(dz, u, r, g):
    dzg = dz * g
    return r * dzg - u * (r * r * r) * jnp.mean(dzg * u, axis=-1, keepdims=True)


def _colsum(v):
    return jnp.sum(v, axis=0, keepdims=True)


def _sigmoid(v):
    return jax.nn.sigmoid(v)


def _weight_copies(slab_ref, off, rows, dst_ref, sems, first_sem):
    return [pltpu.make_async_copy(slab_ref.at[d, pl.ds(off, rows), :], dst_ref.at[pl.ds(d * rows, rows), :],
                                  sems.at[first_sem + d]) for d in range(N_DEV)]


def _on_first_step(copies, method):
    @pl.when(pl.program_id(0) == 0)
    def _():
        for cp in copies:
            getattr(cp, method)()


def _with_halos(main_ref, prev_ref, next_ref, i, n_steps):
    return (main_ref, jnp.where(i > 0, prev_ref[...], 0.0), jnp.where(i < n_steps - 1, next_ref[...], 0.0))


def _broadcast_taps(w_ref, wb_ref, n_taps):
    for k in range(n_taps):
        wb_ref[k] = jnp.broadcast_to(w_ref[k:k + 1, :], wb_ref.shape[1:])


def _conv_tile(tile, wb_ref, starts, ts, width, emit, rolled_ref, rows=FWD_CONV_ROWS):
    main_ref, prev, nxt = tile
    span = ts + 2 * HALO
    nv = rows // SUBLANE
    for cb in range(width // LANE):
        lanes = slice(cb * LANE, (cb + 1) * LANE)
        slot = cb % 2
        window = jnp.concatenate([prev[:, lanes], main_ref[:, lanes], nxt[:, lanes]], axis=0)
        for b in sorted({st % SUBLANE for st in starts}):
            rolled_ref[slot, b] = window if b == 0 else pltpu.roll(window, span - b, axis=0)
        for r0 in range(0, ts, rows):
            acc = jnp.zeros((nv, SUBLANE, LANE), F32)
            for k, st in enumerate(starts):
                shifted = rolled_ref[slot, st % SUBLANE, pl.ds(r0 + st - st % SUBLANE, rows), :]
                acc = acc + shifted.reshape(nv, SUBLANE, LANE) * wb_ref[k, :, lanes][None]
            emit(r0, pl.ds(cb * LANE, LANE), acc.reshape(rows, LANE))


def _window(tile, r0, cb, ts, rows):
    main_ref, prev, nxt = tile
    lanes = slice(cb * LANE, (cb + 1) * LANE)
    lo, hi = max(r0 - HALO, 0), min(r0 + rows + HALO, ts)
    pieces = [prev[:, lanes]] if r0 - HALO < 0 else []
    pieces.append(main_ref[lo:hi, lanes])
    if r0 + rows + HALO > ts:
        pieces.append(nxt[:, lanes])
    return pieces[0] if len(pieces) == 1 else jnp.concatenate(pieces, axis=0)


def _phases(starts):
    groups = {}
    for k, st in enumerate(starts):
        groups.setdefault(st % SUBLANE, []).append((k, st // SUBLANE))
    return sorted(groups.items())


def _shifted(blk, b):
    n = blk.shape[0]
    rolled = blk if b == 0 else pltpu.roll(blk, n - b, axis=0)
    return rolled.reshape(n // SUBLANE, SUBLANE, blk.shape[1])


def _conv_bwd_tile(dv_tile, u_ref, wb_ref, acc_ref, n_taps, ts, width, emit, rows=BWD_CONV_ROWS):
    groups = _phases(_bwd_starts(n_taps))
    nv = rows // SUBLANE
    for r0 in range(0, ts, rows):
        for cb in range(width // LANE):
            lanes = pl.ds(cb * LANE, LANE)
            blk = _window(dv_tile, r0, cb, ts, rows)
            u = u_ref[pl.ds(r0, rows), lanes].reshape(nv, SUBLANE, LANE)
            du = jnp.zeros((nv, SUBLANE, LANE), F32)
            for b, taps in groups:
                sh = _shifted(blk, b)
                for k, m in taps:
                    du = du + sh[m:m + nv] * wb_ref[k, :, lanes][None]
                    acc_ref[k, :, lanes] += jnp.sum(sh[m:m + nv] * u, axis=0)
            emit(r0, lanes, du.reshape(rows, LANE))


def _fwd_starts(n_taps):
    pad = (n_taps - 1) // 2
    return [HALO - pad + k for k in range(n_taps)]


def _bwd_starts(n_taps):
    pad = (n_taps - 1) // 2
    return [HALO + pad - k for k in range(n_taps)]


def _peer(x, y, c, flip):
    fx, fy, fc = flip
    return (1 - x if fx else x, 1 - y if fy else y, 1 - c if fc else c)


def _all_gather(shards, name):
    n = len(shards)

    def body(*refs):
        ins, outs = refs[:n], refs[n:2 * n]
        send_sems, recv_sems, local_sems = refs[2 * n:]
        x, y, c = lax.axis_index("x"), lax.axis_index("y"), lax.axis_index("c")
        me = 4 * x + 2 * y + c
        local = [pltpu.make_async_copy(ins[j], outs[j].at[me], local_sems.at[j]) for j in range(n)]
        for cp in local:
            cp.start()
        sends, recvs = [], []
        for k, flip in enumerate(FLIPS):
            px, py, pc = _peer(x, y, c, flip)
            peer = 4 * px + 2 * py + pc
            for j in range(n):
                sem = k * n + j
                sends.append(pltpu.make_async_remote_copy(
                    src_ref=ins[j], dst_ref=outs[j].at[me], send_sem=send_sems.at[sem], recv_sem=recv_sems.at[sem],
                    device_id=(px, py, pc), device_id_type=MESH))
                recvs.append(pltpu.make_async_remote_copy(
                    src_ref=ins[j], dst_ref=outs[j].at[peer], send_sem=send_sems.at[sem], recv_sem=recv_sems.at[sem],
                    device_id=(px, py, pc), device_id_type=MESH))
        for cp in sends:
            cp.start()
        for cp in recvs:
            cp.wait_recv()
        for cp in sends:
            cp.wait_send()
        for cp in local:
            cp.wait()

    return pl.pallas_call(
        body, name=name,
        out_shape=[jax.ShapeDtypeStruct((N_DEV,) + s.shape, s.dtype) for s in shards],
        in_specs=[_hbm()] * n, out_specs=[_hbm()] * n,
        scratch_shapes=[pltpu.SemaphoreType.DMA((7 * n,)), pltpu.SemaphoreType.DMA((7 * n,)),
                        pltpu.SemaphoreType.DMA((n,))],
    )(*shards)


def _place_own(src, n_slots_shape, me, name):
    rows, width = src.shape
    tr = next(t for t in (256, 128, 64, 32, 16, SUBLANE) if rows % t == 0)

    def body(me_ref, src_ref, out_ref):
        out_ref[...] = src_ref[...]

    return pl.pallas_call(
        body, name=name,
        grid_spec=pltpu.PrefetchScalarGridSpec(
            num_scalar_prefetch=1, grid=(rows // tr,),
            in_specs=[pl.BlockSpec((tr, width), lambda i, me_ref: (i, 0))],
            out_specs=pl.BlockSpec((None, tr, width), lambda i, me_ref: (me_ref[0], i, 0))),
        out_shape=pltpu.HBM(n_slots_shape, src.dtype),
    )(me, src)


def _place_cast(pieces, me, name):
    n = len(pieces)
    counts = [a.shape[1] if t else a.shape[0] for a, t in pieces]
    width = pieces[0][0].shape[0] if pieces[0][1] else pieces[0][0].shape[1]
    total = sum(counts)

    def body(me_ref, *refs):
        ins, own_ref, land_ref = refs[:n], refs[n], refs[n + 1]
        first = 0
        for (a, transpose), in_ref, count in zip(pieces, ins, counts):
            block = (in_ref[...].T if transpose else in_ref[...]).astype(BF16)
            own_ref[first:first + count, :] = block
            land_ref[first:first + count, :] = block
            first += count

    return pl.pallas_call(
        body, name=name,
        grid_spec=pltpu.PrefetchScalarGridSpec(
            num_scalar_prefetch=1, grid=(1,),
            in_specs=[pl.BlockSpec(a.shape, lambda i, me_ref: (0, 0)) for a, _ in pieces],
            out_specs=[pl.BlockSpec((total, width), lambda i, me_ref: (0, 0)),
                       pl.BlockSpec((None, total, width), lambda i, me_ref: (me_ref[0], 0, 0))]),
        out_shape=[pltpu.HBM((total, width), BF16), pltpu.HBM((N_DEV, total, width), BF16)],
        compiler_params=_params(32),
    )(me, *[_in_hbm(a) for a, _ in pieces])


def _all_gather_two_level(shards, placed, name):
    n = len(shards)
    given = [j for j in range(n) if placed[j] is not None]

    def body(*refs):
        ins, outs = refs[:n], refs[n + len(given):2 * n + len(given)]
        send_sems, recv_sems, local_sems = refs[2 * n + len(given):]
        x, y, c = lax.axis_index("x"), lax.axis_index("y"), lax.axis_index("c")
        me, sibling = (x, y, c), (x, y, 1 - c)
        chips = [(1 - x, y), (x, 1 - y), (1 - x, 1 - y)]

        def slot(j, dev):
            return outs[j].at[4 * dev[0] + 2 * dev[1] + dev[2]]

        def copy(k, j, block, to, src=None):
            return pltpu.make_async_remote_copy(
                src_ref=slot(j, block) if src is None else src, dst_ref=slot(j, block),
                send_sem=send_sems.at[k * n + j], recv_sem=recv_sems.at[k * n + j], device_id=to, device_id_type=MESH)

        local = [pltpu.make_async_copy(ins[j], slot(j, me), local_sems.at[j]) for j in range(n) if j not in given]
        for cp in local:
            cp.start()
        first = [copy(0, j, me, sibling, src=ins[j]) for j in range(n)]
        first += [copy(1 + t, j, me, (*chip, c), src=ins[j]) for t, chip in enumerate(chips) for j in range(n)]
        for cp in first:
            cp.start()
        passed = []
        for t, chip in enumerate(chips):
            for j in range(n):
                copy(1 + t, j, (*chip, c), me).wait_recv()
                passed.append(copy(4 + t, j, (*chip, c), sibling))
                passed[-1].start()
        for j in range(n):
            copy(0, j, sibling, me).wait_recv()
        for t, chip in enumerate(chips):
            for j in range(n):
                copy(4 + t, j, (*chip, 1 - c), me).wait_recv()
        for cp in first + passed:
            cp.wait_send()
        for cp in local:
            cp.wait()

    return pl.pallas_call(
        body, name=name,
        out_shape=[jax.ShapeDtypeStruct((N_DEV,) + s.shape, s.dtype) for s in shards],
        in_specs=[_hbm()] * (n + len(given)), out_specs=[_hbm()] * n,
        input_output_aliases={n + i: j for i, j in enumerate(given)},
        scratch_shapes=[pltpu.SemaphoreType.DMA((7 * n,)), pltpu.SemaphoreType.DMA((7 * n,)),
                        pltpu.SemaphoreType.DMA((n,))],
    )(*shards, *[placed[j] for j in given])


class _Part(NamedTuple):
    src: jax.Array
    scatter: bool
    rows: int
    land: int
    off: int


class _Started(NamedTuple):
    send_sems: jax.Array
    recv_sems: jax.Array
    thru: tuple
    token: jax.Array
    parts: tuple


def _exchange_copies(srcs, lands, send_sems, recv_sems, parts):
    n = len(parts)
    x, y, c = lax.axis_index("x"), lax.axis_index("y"), lax.axis_index("c")
    me = 4 * x + 2 * y + c

    def block(j, dev):
        p = parts[j]
        return srcs[j].at[pl.ds(pl.multiple_of(dev * p.rows, SUBLANE), p.rows), :] if p.scatter else srcs[j]

    def slot(j, index):
        p = parts[j]
        return lands[p.land].at[index, pl.ds(p.off, p.rows), :]

    sends, recvs = [], []
    for k, flip in enumerate(FLIPS):
        px, py, pc = _peer(x, y, c, flip)
        peer = 4 * px + 2 * py + pc
        for j in range(n):
            sems = dict(send_sem=send_sems.at[k * n + j], recv_sem=recv_sems.at[k * n + j],
                        device_id=(px, py, pc), device_id_type=MESH)
            to, got = (k, k) if parts[j].scatter else (me, peer)
            sends.append(pltpu.make_async_remote_copy(src_ref=block(j, peer), dst_ref=slot(j, to), **sems))
            recvs.append(pltpu.make_async_remote_copy(src_ref=block(j, peer), dst_ref=slot(j, got), **sems))
    return sends, recvs


def _exchange_start(parts, lands, name, after=None):
    n, nl = len(parts), len(lands)
    n_in = n + nl + (after is not None)

    def body(*refs):
        srcs, land_refs = refs[:n], refs[n:n + nl]
        send_sems, recv_sems = refs[n_in], refs[n_in + 1]
        token = refs[n_in + 2 + n + nl]
        sends, _ = _exchange_copies(srcs, land_refs, send_sems, recv_sems, parts)
        for cp in sends:
            cp.start()
        token[...] = jnp.zeros_like(token)

    hbm = pl.BlockSpec(memory_space=pltpu.HBM)
    sem = pl.BlockSpec(memory_space=pltpu.SEMAPHORE)
    fresh = lambda s: lax.empty(s.shape, s.dtype) if isinstance(s, jax.ShapeDtypeStruct) else s
    args = [pltpu.with_memory_space_constraint(p.src, pltpu.HBM) for p in parts]
    args += [pltpu.with_memory_space_constraint(fresh(s), pltpu.HBM) for s in lands]
    args += [] if after is None else [after]
    out = pl.pallas_call(
        body, name=name,
        out_shape=(pltpu.SemaphoreType.DMA((7 * n,)), pltpu.SemaphoreType.DMA((7 * n,)),
                   *[pltpu.HBM(a.shape, a.dtype) for a in args[:n + nl]], jax.ShapeDtypeStruct((SUBLANE, LANE), F32)),
        in_specs=[hbm] * (n + nl) + [_hbm()] * (after is not None),
        out_specs=(sem, sem, *[hbm] * (n + nl), _whole()),
        input_output_aliases={j: 2 + j for j in range(n + nl)},
        compiler_params=pltpu.CompilerParams(has_side_effects=pltpu.SideEffectType.DATAFLOW_SIDE_EFFECTING),
    )(*args)
    return _Started(out[0], out[1], tuple(out[2:2 + n + nl]), out[2 + n + nl], tuple(parts))


def _exchange_wait(started, name, after):
    parts = started.parts
    n, nl = len(parts), len(started.thru) - len(parts)

    def body(*refs):
        srcs, land_refs = refs[:n], refs[n:n + nl]
        send_sems, recv_sems = refs[n + nl], refs[n + nl + 1]
        sends, recvs = _exchange_copies(srcs, land_refs, send_sems, recv_sems, parts)
        for cp in sends:
            cp.wait_send()
        for cp in recvs:
            cp.wait_recv()

    hbm = pl.BlockSpec(memory_space=pltpu.HBM)
    sem = pl.BlockSpec(memory_space=pltpu.SEMAPHORE)
    out = pl.pallas_call(
        body, name=name,
        out_shape=tuple(pltpu.HBM(a.shape, a.dtype) for a in started.thru),
        in_specs=[hbm] * (n + nl) + [sem, sem, _hbm()], out_specs=tuple([hbm] * (n + nl)),
        input_output_aliases={j: j for j in range(n + nl)},
        compiler_params=pltpu.CompilerParams(has_side_effects=pltpu.SideEffectType.DATAFLOW_SIDE_EFFECTING),
    )(*started.thru, started.send_sems, started.recv_sems, after)
    return list(out[:n]), list(out[n:])


def _fwd_in(x, g1, b_in, slab, ts, deps):
    n_tok, dm = x.shape
    rows, off, _ = _layout(dm)
    width = 7 * dm

    def body(x_ref, g1_ref, b_ref, slab_ref, proj_ref, p_ref, u_ref, h_ref, w_v, sems):
        copies = _weight_copies(slab_ref, off["win"], rows["win"], w_v, sems, 0)
        _on_first_step(copies, "start")
        _on_first_step(copies, "wait")
        xv = x_ref[...]
        h = (xv * _rms(xv) * g1_ref[...]).astype(BF16)
        h_ref[...] = h
        cols = []
        for j in range(7):
            pj = _dot_nt(h, w_v[pl.ds(j * dm, dm), :]) + b_ref[:, j * dm:(j + 1) * dm]
            proj_ref[:, j * dm:(j + 1) * dm] = pj.astype(proj_ref.dtype)
            if 1 <= j <= 4:
                cols.append(pj)
            if j == 2:
                p_ref[...] = cols[0] * cols[1]
            if j == 4:
                u_ref[...] = cols[2] * _sigmoid(cols[3])

    return pl.pallas_call(
        _after(body, deps), name="fwd_in", grid=(n_tok // ts,),
        in_specs=[_whole()] * len(deps) + [_rows(ts, dm), _whole(), _whole(), _hbm()],
        out_specs=[_rows(ts, width), _rows(ts, dm), _rows(ts, dm), _rows(ts, dm)],
        out_shape=[jax.ShapeDtypeStruct((n_tok, width), BF16), jax.ShapeDtypeStruct((n_tok, dm), F32),
                   jax.ShapeDtypeStruct((n_tok, dm), F32), jax.ShapeDtypeStruct((n_tok, dm), BF16)],
        scratch_shapes=[pltpu.VMEM((width, dm), BF16), pltpu.SemaphoreType.DMA((N_DEV,))],
        compiler_params=_params(56),
    )(*deps, x, g1, b_in, slab)


def _fwd_mix(p, u, proj, x, caw, cab, cbw, cbb, lng, lnb, g1post, slab, ts):
    n_tok, dm = x.shape
    rows, off, _ = _layout(dm)
    n_steps = n_tok // ts

    def body(p_ref, p_prev, p_next, u_ref, u_prev, u_next, bg_ref, za_ref, zb_ref, x_ref,
             caw_ref, cab_ref, cbw_ref, cbb_ref, lng_ref, lnb_ref, g1p_ref, slab_ref,
             va_ref, vb_ref, ya_ref, yb_ref, qa_ref, sb_ref, mg_ref, mix_ref, x1_ref,
             wa_v, wb_v, wo_v, tap_a, tap_b, sems, rolled_a, rolled_b):
        i = pl.program_id(0)
        copies = (_weight_copies(slab_ref, off["wa"], rows["wa"], wa_v, sems, 0)
                  + _weight_copies(slab_ref, off["wb"], rows["wb"], wb_v, sems, N_DEV)
                  + _weight_copies(slab_ref, off["wo"], rows["wo"], wo_v, sems, 2 * N_DEV))
        _on_first_step(copies, "start")

        @pl.when(i == 0)
        def _():
            _broadcast_taps(caw_ref, tap_a, CONV_A)
            _broadcast_taps(cbw_ref, tap_b, CONV_B)

        def emit_a(r0, lanes, acc):
            va_ref[pl.ds(r0, acc.shape[0]), lanes] = acc + cab_ref[:, lanes]

        def emit_b(r0, lanes, acc):
            vb_ref[pl.ds(r0, acc.shape[0]), lanes] = acc + cbb_ref[:, lanes]

        _conv_tile(_with_halos(p_ref, p_prev, p_next, i, n_steps), tap_a, _fwd_starts(CONV_A), ts, dm, emit_a,
                   rolled_a)
        _conv_tile(_with_halos(u_ref, u_prev, u_next, i, n_steps), tap_b, _fwd_starts(CONV_B), ts, dm, emit_b,
                   rolled_b)
        _on_first_step(copies, "wait")

        qa = (bg_ref[...].astype(F32) * va_ref[...]).astype(BF16)
        qa_ref[...] = qa
        ya = _dot(qa, wa_v[...])
        vb = vb_ref[...]
        xc = vb - jnp.mean(vb, axis=-1, keepdims=True)
        rstd = lax.rsqrt(jnp.mean(xc * xc, axis=-1, keepdims=True) + LN_EPS)
        ln = xc * rstd * lng_ref[...] + lnb_ref[...]
        sb = (ln * _sigmoid(ln)).astype(BF16)
        sb_ref[...] = sb
        yb = _dot(sb, wb_v[...])
        ya_ref[...] = ya.astype(BF16)
        yb_ref[...] = yb.astype(BF16)
        merged = (_sigmoid(za_ref[...].astype(F32)) * ya + _sigmoid(zb_ref[...].astype(F32)) * yb).astype(BF16)
        mg_ref[...] = merged
        mix = _dot(merged, wo_v[...])
        mix_ref[...] = mix
        x1_ref[...] = x_ref[...] + mix * _rms(mix) * g1p_ref[...]

    tok = lambda dt: jax.ShapeDtypeStruct((n_tok, dm), dt)
    return pl.pallas_call(
        body, name="fwd_mix", grid=(n_steps,),
        in_specs=(_halo_specs(ts, dm, n_tok) + _halo_specs(ts, dm, n_tok)
                  + [_rows(ts, dm, 0), _rows(ts, dm, 5), _rows(ts, dm, 6), _rows(ts, dm)]
                  + [_whole()] * 7 + [_hbm()]),
        out_specs=[_rows(ts, dm)] * 9,
        out_shape=[tok(F32), tok(F32), tok(BF16), tok(BF16), tok(BF16), tok(BF16), tok(BF16), tok(F32), tok(F32)],
        scratch_shapes=[pltpu.VMEM((dm, dm), BF16), pltpu.VMEM((dm, dm), BF16), pltpu.VMEM((dm, dm), BF16),
                        pltpu.VMEM((CONV_A, SUBLANE, dm), F32), pltpu.VMEM((CONV_B, SUBLANE, dm), F32),
                        pltpu.SemaphoreType.DMA((3 * N_DEV,)),
                        pltpu.VMEM((2, SUBLANE, ts + 2 * HALO, LANE), F32),
                        pltpu.VMEM((2, SUBLANE, ts + 2 * HALO, LANE), F32)],
        compiler_params=_params(48),
    )(p, p, p, u, u, u, proj, proj, proj, x, caw, cab, cbw, cbb, lng, lnb, g1post, slab)


def _mlp_fwd_bwd(x1, mix, tgt, g1post, g2pre, g2post, slab, ts):
    n_tok, dm = x1.shape
    rows, off, _ = _layout(dm)
    ff = 4 * dm

    def body(x1_ref, mix_ref, t_ref, g1p_ref, g2pre_ref, g2post_ref, slab_ref,
             f_ref, df1_ref, h2_ref, df2_ref, dmix_ref, dx1_ref, small_ref, w1_v, w2_v, relu_v, sems):
        w1_copies = _weight_copies(slab_ref, off["w1"], rows["w1"], w1_v, sems, 0)
        w2_copies = _weight_copies(slab_ref, off["w2"], rows["w2"], w2_v, sems, N_DEV)
        _on_first_step(w1_copies + w2_copies, "start")

        @pl.when(pl.program_id(0) == 0)
        def _():
            small_ref[...] = jnp.zeros_like(small_ref)

        _on_first_step(w1_copies + w2_copies, "wait")
        x1v = x1_ref[...]
        r3 = _rms(x1v)
        g2pre = g2pre_ref[...]
        h2 = (x1v * r3 * g2pre).astype(BF16)
        h2_ref[...] = h2
        for c in range(4):
            blk = pl.ds(c * dm, dm)
            relu = jnp.maximum(_dot_nt(h2, w1_v[blk, :]), 0.0)
            relu_v[:, c * dm:(c + 1) * dm] = relu
            f_ref[:, c * dm:(c + 1) * dm] = (relu * relu).astype(BF16)
        f2 = _dot(f_ref[...], w2_v[...])
        r4 = _rms(f2)
        g2post = g2post_ref[...]
        err = x1v + f2 * r4 * g2post - t_ref[...]
        dy = err * (1.0 / dm)
        small_ref[3:4, :] += _colsum(err * err)
        small_ref[0:1, :] += _colsum(dy * f2 * r4)
        df2 = _rms_bwd(dy, f2, r4, g2post).astype(BF16)
        df2_ref[...] = df2
        for c in range(4):
            blk = pl.ds(c * dm, dm)
            df1 = (_dot_nt(df2, w2_v[blk, :]) * (2.0 * relu_v[:, c * dm:(c + 1) * dm])).astype(BF16)
            df1_ref[:, c * dm:(c + 1) * dm] = df1
        dh2 = _dot(df1_ref[...], w1_v[...])
        small_ref[1:2, :] += _colsum(dh2 * x1v * r3)
        dx1 = dy + _rms_bwd(dh2, x1v, r3, g2pre)
        dx1_ref[...] = dx1
        mixv = mix_ref[...]
        r2 = _rms(mixv)
        small_ref[2:3, :] += _colsum(dx1 * mixv * r2)
        dmix_ref[...] = _rms_bwd(dx1, mixv, r2, g1p_ref[...]).astype(BF16)

    tok = lambda w, dt: jax.ShapeDtypeStruct((n_tok, w), dt)
    return pl.pallas_call(
        body, name="mlp_fwd_bwd", grid=(n_tok // ts,),
        in_specs=[_rows(ts, dm)] * 3 + [_whole()] * 3 + [_hbm()],
        out_specs=[_rows(ts, ff), _rows(ts, ff), _rows(ts, dm), _rows(ts, dm), _rows(ts, dm), _rows(ts, dm),
                   pl.BlockSpec((SUBLANE, dm), lambda i: (0, 0))],
        out_shape=[tok(ff, BF16), tok(ff, BF16), tok(dm, BF16), tok(dm, BF16), tok(dm, BF16), tok(dm, F32),
                   jax.ShapeDtypeStruct((SUBLANE, dm), F32)],
        scratch_shapes=[pltpu.VMEM((ff, dm), BF16), pltpu.VMEM((ff, dm), BF16), pltpu.VMEM((ts, ff), F32),
                        pltpu.SemaphoreType.DMA((2 * N_DEV,))],
        compiler_params=_params(56),
    )(x1, mix, tgt, g1post, g2pre, g2post, slab)


def _bwd_mix(dmix, ya, yb, proj, va, vb, lng, lnb, slab, ts, deps):
    n_tok, dm = dmix.shape
    rows, off, _ = _layout(dm)
    n_steps = n_tok // ts

    def body(dmix_ref, ya_ref, yb_ref, bg_ref, za_ref, zb_ref, va_ref, vb_ref, lng_ref, lnb_ref, slab_ref,
             dpa_ref, dya_ref, dyb_ref, dva_ref, dvb_ref, small_ref, wa_v, wb_v, wo_v, sems):
        wo_copies = _weight_copies(slab_ref, off["wo"], rows["wo"], wo_v, sems, 2 * N_DEV)
        ab_copies = (_weight_copies(slab_ref, off["wa"], rows["wa"], wa_v, sems, 0)
                     + _weight_copies(slab_ref, off["wb"], rows["wb"], wb_v, sems, N_DEV))
        _on_first_step(wo_copies + ab_copies, "start")

        @pl.when(pl.program_id(0) == 0)
        def _():
            small_ref[...] = jnp.zeros_like(small_ref)

        _on_first_step(wo_copies, "wait")
        dmerged = _dot_nt(dmix_ref[...], wo_v[...])
        _on_first_step(ab_copies, "wait")
        sa = _sigmoid(za_ref[...].astype(F32))
        sg = _sigmoid(zb_ref[...].astype(F32))
        dza = dmerged * ya_ref[...].astype(F32) * sa * (1.0 - sa)
        dzb = dmerged * yb_ref[...].astype(F32) * sg * (1.0 - sg)
        dpa_ref[:, dm:2 * dm] = dza.astype(BF16)
        dpa_ref[:, 2 * dm:3 * dm] = dzb.astype(BF16)
        small_ref[5:6, :] += _colsum(dza)
        small_ref[6:7, :] += _colsum(dzb)

        dya = (dmerged * sa).astype(BF16)
        dya_ref[...] = dya
        dqa = _dot_nt(dya, wa_v[...])
        dbg = dqa * va_ref[...]
        dpa_ref[:, 0:dm] = dbg.astype(BF16)
        small_ref[4:5, :] += _colsum(dbg)
        dva = dqa * bg_ref[...].astype(F32)
        dva_ref[...] = dva
        small_ref[2:3, :] += _colsum(dva)

        dyb = (dmerged * sg).astype(BF16)
        dyb_ref[...] = dyb
        dsb = _dot_nt(dyb, wb_v[...])
        vb = vb_ref[...]
        xc = vb - jnp.mean(vb, axis=-1, keepdims=True)
        rstd = lax.rsqrt(jnp.mean(xc * xc, axis=-1, keepdims=True) + LN_EPS)
        nrm = xc * rstd
        lng_v = lng_ref[...]
        ln = nrm * lng_v + lnb_ref[...]
        sl = _sigmoid(ln)
        dln = dsb * (sl * (1.0 + ln * (1.0 - sl)))
        small_ref[0:1, :] += _colsum(dln * nrm)
        small_ref[1:2, :] += _colsum(dln)
        dn = dln * lng_v
        dvb = rstd * (dn - jnp.mean(dn, axis=-1, keepdims=True)
                      - nrm * jnp.mean(dn * nrm, axis=-1, keepdims=True))
        dvb_ref[...] = dvb
        small_ref[3:4, :] += _colsum(dvb)

    tok = lambda w, dt: jax.ShapeDtypeStruct((n_tok, w), dt)
    return pl.pallas_call(
        _after(body, deps), name="bwd_mix", grid=(n_steps,),
        in_specs=([_whole()] * len(deps) + [_rows(ts, dm)] * 3
                  + [_rows(ts, dm, 0), _rows(ts, dm, 5), _rows(ts, dm, 6)]
                  + [_rows(ts, dm)] * 2 + [_whole()] * 2 + [_hbm()]),
        out_specs=[_rows(ts, 3 * dm), _rows(ts, dm), _rows(ts, dm), _rows(ts, dm), _rows(ts, dm),
                   pl.BlockSpec((SUBLANE, dm), lambda i: (0, 0))],
        out_shape=[tok(3 * dm, BF16), tok(dm, BF16), tok(dm, BF16), tok(dm, F32), tok(dm, F32),
                   jax.ShapeDtypeStruct((SUBLANE, dm), F32)],
        scratch_shapes=[pltpu.VMEM((dm, dm), BF16), pltpu.VMEM((dm, dm), BF16), pltpu.VMEM((dm, dm), BF16),
                        pltpu.SemaphoreType.DMA((3 * N_DEV,))],
        compiler_params=_params(56),
    )(*deps, dmix, ya, yb, proj, proj, proj, va, vb, lng, lnb, slab)


def _bwd_conv(dva, dvb, p, u, proj, dpa, caw, cbw, ts, deps):
    n_tok, dm = dva.shape
    n_steps = n_tok // ts
    small_rows = 40

    def body(dva_ref, dva_prev, dva_next, dvb_ref, dvb_prev, dvb_next, p_ref, u_ref,
             cg_ref, ha_ref, a_ref, g_ref, dpa_ref, caw_ref, cbw_ref,
             dproj_ref, small_ref,
             dp_v, du_v, tap_a, tap_b, gwa_v, gwb_v):
        i = pl.program_id(0)

        @pl.when(i == 0)
        def _():
            _broadcast_taps(caw_ref, tap_a, CONV_A)
            _broadcast_taps(cbw_ref, tap_b, CONV_B)
            small_ref[...] = jnp.zeros_like(small_ref)
            gwa_v[...] = jnp.zeros_like(gwa_v)
            gwb_v[...] = jnp.zeros_like(gwb_v)

        def emit_dp(r0, lanes, acc):
            dp_v[pl.ds(r0, acc.shape[0]), lanes] = acc

        def emit_du(r0, lanes, acc):
            du_v[pl.ds(r0, acc.shape[0]), lanes] = acc

        _conv_bwd_tile(_with_halos(dva_ref, dva_prev, dva_next, i, n_steps), p_ref, tap_a, gwa_v, CONV_A, ts, dm,
                       emit_dp)
        _conv_bwd_tile(_with_halos(dvb_ref, dvb_prev, dvb_next, i, n_steps), u_ref, tap_b, gwb_v, CONV_B, ts, dm,
                       emit_du)

        dp = dp_v[...]
        dcg = dp * ha_ref[...].astype(F32)
        dha = dp * cg_ref[...].astype(F32)
        du = du_v[...]
        sg = _sigmoid(g_ref[...].astype(F32))
        da = du * sg
        dg = du * a_ref[...].astype(F32) * sg * (1.0 - sg)
        dproj_ref[:, 0:dm] = dpa_ref[:, 0:dm]
        dproj_ref[:, dm:2 * dm] = dcg.astype(BF16)
        dproj_ref[:, 2 * dm:3 * dm] = dha.astype(BF16)
        dproj_ref[:, 3 * dm:4 * dm] = da.astype(BF16)
        dproj_ref[:, 4 * dm:5 * dm] = dg.astype(BF16)
        dproj_ref[:, 5 * dm:7 * dm] = dpa_ref[:, dm:3 * dm]
        small_ref[3:4, :] += _colsum(dcg)
        small_ref[4:5, :] += _colsum(dha)
        small_ref[5:6, :] += _colsum(da)
        small_ref[6:7, :] += _colsum(dg)

        @pl.when(i == n_steps - 1)
        def _():
            for k in range(CONV_A):
                small_ref[k:k + 1, :] = _colsum(gwa_v[k])
            for k in range(CONV_B):
                small_ref[SUBLANE + k:SUBLANE + k + 1, :] = _colsum(gwb_v[k])

    return pl.pallas_call(
        _after(body, deps), name="bwd_conv", grid=(n_steps,),
        in_specs=([_whole()] * len(deps) + _halo_specs(ts, dm, n_tok) * 2 + [_rows(ts, dm)] * 2
                  + [_rows(ts, dm, 1), _rows(ts, dm, 2), _rows(ts, dm, 3), _rows(ts, dm, 4), _rows(ts, 3 * dm)]
                  + [_whole()] * 2),
        out_specs=[_rows(ts, 7 * dm), pl.BlockSpec((small_rows, dm), lambda i: (0, 0))],
        out_shape=[jax.ShapeDtypeStruct((n_tok, 7 * dm), BF16), jax.ShapeDtypeStruct((small_rows, dm), F32)],
        scratch_shapes=[pltpu.VMEM((ts, dm), F32), pltpu.VMEM((ts, dm), F32),
                        pltpu.VMEM((CONV_A, SUBLANE, dm), F32), pltpu.VMEM((CONV_B, SUBLANE, dm), F32),
                        pltpu.VMEM((CONV_A, SUBLANE, dm), F32), pltpu.VMEM((CONV_B, SUBLANE, dm), F32)],
        compiler_params=_params(48),
    )(*deps, dva, dva, dva, dvb, dvb, dvb, p, u, proj, proj, proj, proj, dpa, caw, cbw)


def _bwd_in(dproj, x, dx1, g1, slab, ts, deps):
    n_tok, dm = x.shape
    rows, off, _ = _layout(dm)
    width = 7 * dm

    def body(dproj_ref, x_ref, dx1_ref, g1_ref, slab_ref, gx_ref, small_ref, w_v, sems):
        copies = _weight_copies(slab_ref, off["win"], rows["win"], w_v, sems, 0)
        _on_first_step(copies, "start")

        @pl.when(pl.program_id(0) == 0)
        def _():
            small_ref[...] = jnp.zeros_like(small_ref)

        _on_first_step(copies, "wait")
        dh = _dot(dproj_ref[...], w_v[...])
        xv = x_ref[...]
        r1 = _rms(xv)
        small_ref[0:1, :] += _colsum(dh * xv * r1)
        gx_ref[...] = dx1_ref[...] + _rms_bwd(dh, xv, r1, g1_ref[...])

    return pl.pallas_call(
        _after(body, deps), name="bwd_in", grid=(n_tok // ts,),
        in_specs=[_whole()] * len(deps) + [_rows(ts, width), _rows(ts, dm), _rows(ts, dm), _whole(), _hbm()],
        out_specs=[_rows(ts, dm), pl.BlockSpec((SUBLANE, dm), lambda i: (0, 0))],
        out_shape=[jax.ShapeDtypeStruct((n_tok, dm), F32), jax.ShapeDtypeStruct((SUBLANE, dm), F32)],
        scratch_shapes=[pltpu.VMEM((width, dm), BF16), pltpu.SemaphoreType.DMA((N_DEV,))],
        compiler_params=_params(56),
    )(*deps, dproj, x, dx1, g1, slab)


def _wgrad(a, b, name, tm, tk, out_dtype):
    n_tok, m = a.shape
    n = b.shape[1]
    k_steps = n_tok // tk

    def body(a_ref, b_ref, o_ref, acc_v):
        k = pl.program_id(1)

        @pl.when(k == 0)
        def _():
            acc_v[...] = jnp.zeros_like(acc_v)

        acc_v[...] += _dot_tn(a_ref[...], b_ref[...])

        @pl.when(k == k_steps - 1)
        def _():
            o_ref[...] = acc_v[...].astype(o_ref.dtype)

    return pl.pallas_call(
        body, name=name, grid=(m // tm, k_steps),
        in_specs=[pl.BlockSpec((tk, tm), lambda i, k: (k, i)), pl.BlockSpec((tk, n), lambda i, k: (k, 0))],
        out_specs=pl.BlockSpec((tm, n), lambda i, k: (i, 0)),
        out_shape=pltpu.HBM((m, n), out_dtype),
        scratch_shapes=[pltpu.VMEM((tm, n), F32)],
        compiler_params=pltpu.CompilerParams(dimension_semantics=("arbitrary", "arbitrary"),
                                             vmem_limit_bytes=40 * MIB),
    )(a, b)


def _adamw(w, g, m, v):
    m = ADAM_B1 * m + (1.0 - ADAM_B1) * g
    v = ADAM_B2 * v + (1.0 - ADAM_B2) * (g * g)
    m_hat = m / (1.0 - ADAM_B1 ** ADAM_STEP)
    v_hat = v / (1.0 - ADAM_B2 ** ADAM_STEP)
    delta = -ADAM_LR * (m_hat / (jnp.sqrt(v_hat) + ADAM_EPS) + ADAM_WD * w)
    return delta, m, v


def _adam_big(recv, part, me, off, rows, w, m, v, transpose, name, tr):
    dm = recv.shape[2]
    per = rows // tr

    def body(me_ref, own_ref, r_ref, w_ref, m_ref, v_ref, g_ref, d_ref, mo_ref, vo_ref):
        g = own_ref[...].astype(F32)
        for k in range(len(FLIPS)):
            g = g + r_ref[k].astype(F32)
        if transpose:
            g = g.T
        delta, m_new, v_new = _adamw(w_ref[...], g, m_ref[...], v_ref[...])
        g_ref[...] = g
        d_ref[...] = delta
        mo_ref[...] = m_new
        vo_ref[...] = v_new

    if transpose:
        blk = pl.BlockSpec((dm, tr), lambda i, me_ref: (0, i))
    else:
        blk = pl.BlockSpec((tr, dm), lambda i, me_ref: (i, 0))
    first = off // tr
    return pl.pallas_call(
        body, name=name,
        grid_spec=pltpu.PrefetchScalarGridSpec(
            num_scalar_prefetch=1, grid=(per,),
            in_specs=[pl.BlockSpec((tr, dm), lambda i, me_ref: (me_ref[0] * per + i, 0)),
                      pl.BlockSpec((len(FLIPS), tr, dm), lambda i, me_ref: (0, first + i, 0)), blk, blk, blk],
            out_specs=[blk] * 4),
        out_shape=[jax.ShapeDtypeStruct(w.shape, F32)] * 4,
        compiler_params=_params(32),
    )(me, *[_in_hbm(a) for a in (part, recv, w, m, v)])


LOSS_ROW = 15
CONV_A_ROW = 16
CONV_B_ROW = 24


def _adam_small(recv_small, recv_last, me, params, d_model):
    n = len(params)
    cw = d_model // N_DEV

    def body(me_ref, r_ref, rc_ref, l_ref, *refs):
        ins, loss_ref, outs = refs[:3 * n], refs[3 * n], refs[3 * n + 1:3 * n + 1 + 4 * n]
        g_v, gc_v, last_v = refs[3 * n + 1 + 4 * n:]
        g, gc, last = r_ref[0], rc_ref[0], l_ref[0]
        for d in range(1, N_DEV):
            g, gc, last = g + r_ref[d], gc + rc_ref[d], last + l_ref[d]
        g_v[...], gc_v[...], last_v[...] = g, gc, last
        loss_ref[...] = (0.5 / d_model) * jnp.sum(g_v[LOSS_ROW:LOSS_ROW + 1, :], axis=-1, keepdims=True)
        for j, (row0, own_columns, (w, _, _)) in enumerate(params):
            w_ref, m_ref, v_ref = ins[3 * j:3 * j + 3]
            source = gc_v if own_columns else (last_v if row0 == 0 else g_v)
            grad = source[row0:row0 + w.shape[0], :]
            delta, m_new, v_new = _adamw(w_ref[...], grad, m_ref[...], v_ref[...])
            for ref, val in zip(outs[4 * j:4 * j + 4], (grad, delta, m_new, v_new)):
                ref[...] = val

    full = lambda shape: pl.BlockSpec(shape, lambda i, me_ref: (0,) * len(shape))
    stack_rows = recv_small.shape[1]
    flat = [a for _, _, triple in params for a in triple]
    shapes = [w.shape for _, _, (w, _, _) in params for _ in range(4)]
    out = pl.pallas_call(
        body, name="adam_small",
        grid_spec=pltpu.PrefetchScalarGridSpec(
            num_scalar_prefetch=1, grid=(1,),
            in_specs=[full(recv_small.shape),
                      pl.BlockSpec((N_DEV, stack_rows, cw), lambda i, me_ref: (0, 0, me_ref[0])),
                      full(recv_last.shape)] + [full(a.shape) for a in flat],
            out_specs=[full((1, 1))] + [full(s) for s in shapes],
            scratch_shapes=[pltpu.VMEM((stack_rows, d_model), F32), pltpu.VMEM((stack_rows, cw), F32),
                            pltpu.VMEM(recv_last.shape[1:], F32)]),
        out_shape=[jax.ShapeDtypeStruct((1, 1), F32)] + [jax.ShapeDtypeStruct(s, F32) for s in shapes],
    )(me, *[_in_hbm(a) for a in (recv_small, recv_small, recv_last, *flat)])
    return out[0], [tuple(out[1 + 4 * j:5 + 4 * j]) for j in range(n)]


def _tile(n_tok, want):
    return min(want, n_tok)


def kernel(x, norm1_pre_g, w_in, b_in, conv_a_w, conv_a_b, w_a_out, conv_b_w, conv_b_b, ln_b_g, ln_b_b, w_b_out, w_o, norm1_post_g, norm2_pre_g, w_mlp_in, w_mlp_out, norm2_post_g, loss_target, m_norm1_pre_g, m_w_in, m_b_in, m_conv_a_w, m_conv_a_b, m_w_a_out, m_conv_b_w, m_conv_b_b, m_ln_b_g, m_ln_b_b, m_w_b_out, m_w_o, m_norm1_post_g, m_norm2_pre_g, m_w_mlp_in, m_w_mlp_out, m_norm2_post_g, v_norm1_pre_g, v_w_in, v_b_in, v_conv_a_w, v_conv_a_b, v_w_a_out, v_conv_b_w, v_conv_b_b, v_ln_b_g, v_ln_b_b, v_w_b_out, v_w_o, v_norm1_post_g, v_norm2_pre_g, v_w_mlp_in, v_w_mlp_out, v_norm2_post_g):
    n_tok, dm = x.shape[1], x.shape[2]
    rows, off, slab_rows = _layout(dm)
    cw = dm // N_DEV
    xs = x.reshape(n_tok, dm)
    tgt = loss_target.reshape(n_tok, dm)
    row = lambda vec: vec.reshape(1, -1)
    scattered = lambda group: jax.ShapeDtypeStruct((len(FLIPS), slab_rows[group], dm), BF16)
    tm, tk = min(dm, 1024), _tile(n_tok, 2048)
    me = (4 * lax.axis_index("x") + 2 * lax.axis_index("y") + lax.axis_index("c")).astype(jnp.int32).reshape(1)

    conv_own = jnp.concatenate([conv_a_w, jnp.zeros((SUBLANE - CONV_A, cw), F32), conv_b_w,
                                jnp.zeros((1, cw), F32)], axis=0)
    own_in, land_in = _place_cast([(w_in, True)], me, "place_w_in")
    slab_in, conv_all = _all_gather_two_level([own_in, conv_own], [land_in, None], "gather_w_in")
    conv_full = conv_all.transpose(1, 0, 2).reshape(conv_own.shape[0], dm)
    caw, cbw = conv_full[0:CONV_A], conv_full[SUBLANE:SUBLANE + CONV_B]
    own_abo, land_abo = _place_cast([(w_a_out, False), (w_b_out, False), (w_o, False)], me, "place_abo")
    own_mlp, land_mlp = _place_cast([(w_mlp_in, True), (w_mlp_out, False)], me, "place_mlp")
    ag_abo = _exchange_start([_Part(own_abo, False, slab_rows["abo"], 0, 0)], [land_abo],
                             "gather_abo_start", after=slab_in)
    ag_mlp = _exchange_start([_Part(own_mlp, False, slab_rows["mlp"], 0, 0)], [land_mlp],
                             "gather_mlp_start", after=ag_abo.token)

    proj, p, u, h = _fwd_in(xs, row(norm1_pre_g), row(b_in), slab_in, _tile(n_tok, 512),
                            [ag_abo.token, ag_mlp.token])
    _, (slab_abo,) = _exchange_wait(ag_abo, "gather_abo_wait", after=proj)
    va, vb, ya, yb, qa, sb, merged, mix, x1 = _fwd_mix(
        p, u, proj, xs, caw, row(conv_a_b), cbw, row(conv_b_b), row(ln_b_g), row(ln_b_b), row(norm1_post_g),
        slab_abo, _tile(n_tok, 256))
    _, (slab_mlp,) = _exchange_wait(ag_mlp, "gather_mlp_wait", after=x1)
    f, df1, h2, df2, dmix, dx1, small_mlp = _mlp_fwd_bwd(
        x1, mix, tgt, row(norm1_post_g), row(norm2_pre_g), row(norm2_post_g), slab_mlp, _tile(n_tok, 256))

    g_w1_part = _wgrad(df1, h2, "wgrad_mlp_in", tm, tk, BF16)
    g_w2_part = _wgrad(f, df2, "wgrad_mlp_out", tm, tk, BF16)
    dpa, dya, dyb, dva, dvb, small_mix = _bwd_mix(
        dmix, ya, yb, proj, va, vb, row(ln_b_g), row(ln_b_b), slab_abo, _tile(n_tok, 512), [])
    rs_mlp = _exchange_start(
        [_Part(g_w1_part, True, rows["w1"], 0, off["w1"]), _Part(g_w2_part, True, rows["w2"], 0, off["w2"])],
        [scattered("mlp")], "scatter_mlp_start", after=dpa)
    rs_abo = _exchange_start(
        [_Part(_wgrad(qa, dya, "wgrad_a_out", tm, tk, BF16), True, rows["wa"], 0, off["wa"]),
         _Part(_wgrad(sb, dyb, "wgrad_b_out", tm, tk, BF16), True, rows["wb"], 0, off["wb"]),
         _Part(_wgrad(merged, dmix, "wgrad_o", tm, tk, BF16), True, rows["wo"], 0, off["wo"])],
        [scattered("abo")], "scatter_abo_start", after=rs_mlp.token)
    dproj, small_conv = _bwd_conv(dva, dvb, p, u, proj, dpa, caw, cbw, _tile(n_tok, 256),
                                  [rs_mlp.token, rs_abo.token])

    zeros = lambda r: jnp.zeros((r, dm), F32)
    small = jnp.concatenate([
        zeros(1),
        small_mix[2:3],
        small_mix[3:4],
        small_mix[0:2],
        small_mlp[2:3],
        small_mlp[1:2],
        small_mlp[0:1],
        small_mix[4:5], small_conv[3:7], small_mix[5:7],
        small_mlp[3:4],
        small_conv[0:CONV_A], zeros(SUBLANE - CONV_A),
        small_conv[8:8 + CONV_B], zeros(1),
    ], axis=0)

    rs_in = _exchange_start(
        [_Part(_wgrad(dproj, h, "wgrad_in", tm, tk, BF16), True, rows["win"], 0, off["win"]),
         _Part(small, False, small.shape[0], 1, 0)],
        [scattered("in"), _place_own(small, (N_DEV,) + small.shape, me, "place_small")], "scatter_in_start")
    grad_x, small_in = _bwd_in(dproj, xs, dx1, row(norm1_pre_g), slab_in, _tile(n_tok, 512), [rs_in.token])

    tr = min(LANE, rows["wa"])
    (g_w1, g_w2), (recv_mlp,) = _exchange_wait(rs_mlp, "scatter_mlp_wait", after=grad_x)
    (g_wa, g_wb, g_wo), (recv_abo,) = _exchange_wait(rs_abo, "scatter_abo_wait", after=grad_x)
    big = {
        "w_mlp_in": _adam_big(recv_mlp, g_w1, me, off["w1"], rows["w1"], w_mlp_in, m_w_mlp_in, v_w_mlp_in, True,
                              "adam_w_mlp_in", tr),
        "w_mlp_out": _adam_big(recv_mlp, g_w2, me, off["w2"], rows["w2"], w_mlp_out, m_w_mlp_out, v_w_mlp_out,
                               False, "adam_w_mlp_out", tr),
        "w_a_out": _adam_big(recv_abo, g_wa, me, off["wa"], rows["wa"], w_a_out, m_w_a_out, v_w_a_out, False,
                             "adam_w_a_out", tr),
        "w_b_out": _adam_big(recv_abo, g_wb, me, off["wb"], rows["wb"], w_b_out, m_w_b_out, v_w_b_out, False,
                             "adam_w_b_out", tr),
        "w_o": _adam_big(recv_abo, g_wo, me, off["wo"], rows["wo"], w_o, m_w_o, v_w_o, False, "adam_w_o", tr),
    }
    (g_win, _), (recv_in, recv_small) = _exchange_wait(rs_in, "scatter_in_wait", after=big["w_o"][3])
    recv_last, = _all_gather([small_in], "gather_last")
    big["w_in"] = _adam_big(recv_in, g_win, me, off["win"], rows["win"], w_in, m_w_in, v_w_in, True, "adam_w_in", tr)

    small_names = ("norm1_pre_g", "conv_a_b", "conv_b_b", "ln_b_g", "ln_b_b", "norm1_post_g", "norm2_pre_g",
                   "norm2_post_g")
    given = dict(
        norm1_pre_g=(norm1_pre_g, m_norm1_pre_g, v_norm1_pre_g), conv_a_b=(conv_a_b, m_conv_a_b, v_conv_a_b),
        conv_b_b=(conv_b_b, m_conv_b_b, v_conv_b_b), ln_b_g=(ln_b_g, m_ln_b_g, v_ln_b_g),
        ln_b_b=(ln_b_b, m_ln_b_b, v_ln_b_b), norm1_post_g=(norm1_post_g, m_norm1_post_g, v_norm1_post_g),
        norm2_pre_g=(norm2_pre_g, m_norm2_pre_g, v_norm2_pre_g),
        norm2_post_g=(norm2_post_g, m_norm2_post_g, v_norm2_post_g))
    params = [(j, False, tuple(row(a) for a in given[name])) for j, name in enumerate(small_names)]
    params.append((SUBLANE, False, tuple(a.reshape(7, dm) for a in (b_in, m_b_in, v_b_in))))
    params.append((CONV_A_ROW, True, (conv_a_w, m_conv_a_w, v_conv_a_w)))
    params.append((CONV_B_ROW, True, (conv_b_w, m_conv_b_w, v_conv_b_w)))
    loss, small_out = _adam_small(recv_small, recv_last, me, params, dm)
    small_leaves = {name: tuple(a.reshape(dm) for a in small_out[j]) for j, name in enumerate(small_names)}
    small_leaves["b_in"] = tuple(a.reshape(7 * dm) for a in small_out[len(small_names)])
    small_leaves["conv_a_w"] = small_out[len(small_names) + 1]
    small_leaves["conv_b_w"] = small_out[len(small_names) + 2]

    order = ("norm1_pre_g", "w_in", "b_in", "conv_a_w", "conv_a_b", "w_a_out", "conv_b_w", "conv_b_b", "ln_b_g",
             "ln_b_b", "w_b_out", "w_o", "norm1_post_g", "norm2_pre_g", "w_mlp_in", "w_mlp_out", "norm2_post_g")
    leaves = [big[name] if name in big else small_leaves[name] for name in order]
    grads, deltas, new_m, new_v = zip(*leaves)
    return (loss.reshape(()), grad_x.reshape(x.shape), *grads, *deltas, *new_m, *new_v)
```

```python
from typing import NamedTuple

import jax
import jax.numpy as jnp
from jax import lax
from jax.experimental import pallas as pl
from jax.experimental.pallas import tpu as pltpu

F32 = jnp.float32
BF16 = jnp.bfloat16

RMS_EPS = 1e-6
LN_EPS = 1e-5
ADAM_LR = 0.001
ADAM_B1 = 0.9
ADAM_B2 = 0.999
ADAM_EPS = 1e-08
ADAM_WD = 0.01
ADAM_STEP = 10

N_DEV = 8
CONV_A = 3
CONV_B = 31
LANE = 128
SUBLANE = 8
HALO = 16
FWD_CONV_ROWS = 32
BWD_CONV_ROWS = 64
MIB = 1 << 20
FLIPS = ((0, 0, 1), (0, 1, 0), (1, 0, 0), (0, 1, 1), (1, 0, 1), (1, 1, 0), (1, 1, 1))
MESH = pl.DeviceIdType.MESH


def _layout(d_model):
    e = d_model // N_DEV
    rows = {"win": 7 * e, "w1": 4 * e, "w2": 4 * e, "wa": e, "wb": e, "wo": e}
    off = {"win": 0, "w1": 0, "w2": 4 * e, "wa": 0, "wb": e, "wo": 2 * e}
    return rows, off, {"in": 7 * e, "mlp": 8 * e, "abo": 3 * e}


def _after(body, deps):
    def ordered(*refs):
        return body(*refs[len(deps):])
    return ordered


def _params(vmem_mib):
    return pltpu.CompilerParams(dimension_semantics=("arbitrary",), vmem_limit_bytes=vmem_mib * MIB)


def _whole():
    return pl.BlockSpec(memory_space=pltpu.VMEM)


def _hbm():
    return pl.BlockSpec(memory_space=pl.ANY)


def _in_hbm(a):
    return pltpu.with_memory_space_constraint(a, pltpu.HBM)


def _rows(ts, width, col=0):
    return pl.BlockSpec((ts, width), lambda i: (i, col))


def _halo_specs(ts, width, n_rows):
    per = ts // HALO
    last = n_rows // HALO - 1
    return [
        pl.BlockSpec((ts, width), lambda i: (i, 0)),
        pl.BlockSpec((HALO, width), lambda i: (jnp.maximum(i * per - 1, 0), 0)),
        pl.BlockSpec((HALO, width), lambda i: (jnp.minimum((i + 1) * per, last), 0)),
    ]


def _dot(a, b):
    return jnp.dot(a, b, preferred_element_type=F32)


def _dot_nt(a, b):
    return lax.dot_general(a, b, (((1,), (1,)), ((), ())), preferred_element_type=F32)


def _dot_tn(a, b):
    return lax.dot_general(a, b, (((0,), (0,)), ((), ())), preferred_element_type=F32)


def _rms(u):
    return lax.rsqrt(jnp.mean(u * u, axis=-1, keepdims=True) + RMS_EPS)


def _rms_bwd(dz, u, r, g):
    dzg = dz * g
    return r * dzg - u * (r * r * r) * jnp.mean(dzg * u, axis=-1, keepdims=True)


def _colsum(v):
    return jnp.sum(v, axis=0, keepdims=True)


def _sigmoid(v):
    return jax.nn.sigmoid(v)


def _weight_copies(slab_ref, off, rows, dst_ref, sems, first_sem):
    return [pltpu.make_async_copy(slab_ref.at[d, pl.ds(off, rows), :], dst_ref.at[pl.ds(d * rows, rows), :],
                                  sems.at[first_sem + d]) for d in range(N_DEV)]


def _on_first_step(copies, method):
    @pl.when(pl.program_id(0) == 0)
    def _():
        for cp in copies:
            getattr(cp, method)()


def _with_halos(main_ref, prev_ref, next_ref, i, n_steps):
    return (main_ref, jnp.where(i > 0, prev_ref[...], 0.0), jnp.where(i < n_steps - 1, next_ref[...], 0.0))


def _broadcast_taps(w_ref, wb_ref, n_taps):
    for k in range(n_taps):
        wb_ref[k] = jnp.broadcast_to(w_ref[k:k + 1, :], wb_ref.shape[1:])


def _conv_tile(tile, wb_ref, starts, ts, width, emit, rolled_ref, rows=FWD_CONV_ROWS):
    main_ref, prev, nxt = tile
    span = ts + 2 * HALO
    nv = rows // SUBLANE
    for cb in range(width // LANE):
        lanes = slice(cb * LANE, (cb + 1) * LANE)
        slot = cb % 2
        window = jnp.concatenate([prev[:, lanes], main_ref[:, lanes], nxt[:, lanes]], axis=0)
        for b in sorted({st % SUBLANE for st in starts}):
            rolled_ref[slot, b] = window if b == 0 else pltpu.roll(window, span - b, axis=0)
        for r0 in range(0, ts, rows):
            acc = jnp.zeros((nv, SUBLANE, LANE), F32)
            for k, st in enumerate(starts):
                shifted = rolled_ref[slot, st % SUBLANE, pl.ds(r0 + st - st % SUBLANE, rows), :]
                acc = acc + shifted.reshape(nv, SUBLANE, LANE) * wb_ref[k, :, lanes][None]
            emit(r0, pl.ds(cb * LANE, LANE), acc.reshape(rows, LANE))


def _window(tile, r0, cb, ts, rows):
    main_ref, prev, nxt = tile
    lanes = slice(cb * LANE, (cb + 1) * LANE)
    lo, hi = max(r0 - HALO, 0), min(r0 + rows + HALO, ts)
    pieces = [prev[:, lanes]] if r0 - HALO < 0 else []
    pieces.append(main_ref[lo:hi, lanes])
    if r0 + rows + HALO > ts:
        pieces.append(nxt[:, lanes])
    return pieces[0] if len(pieces) == 1 else jnp.concatenate(pieces, axis=0)


def _phases(starts):
    groups = {}
    for k, st in enumerate(starts):
        groups.setdefault(st % SUBLANE, []).append((k, st // SUBLANE))
    return sorted(groups.items())


def _shifted(blk, b):
    n = blk.shape[0]
    rolled = blk if b == 0 else pltpu.roll(blk, n - b, axis=0)
    return rolled.reshape(n // SUBLANE, SUBLANE, blk.shape[1])


def _conv_bwd_tile(dv_tile, u_ref, wb_ref, acc_ref, n_taps, ts, width, emit, rows=BWD_CONV_ROWS):
    groups = _phases(_bwd_starts(n_taps))
    nv = rows // SUBLANE
    for r0 in range(0, ts, rows):
        for cb in range(width // LANE):
            lanes = pl.ds(cb * LANE, LANE)
            blk = _window(dv_tile, r0, cb, ts, rows)
            u = u_ref[pl.ds(r0, rows), lanes].reshape(nv, SUBLANE, LANE)
            du = jnp.zeros((nv, SUBLANE, LANE), F32)
            for b, taps in groups:
                sh = _shifted(blk, b)
                for k, m in taps:
                    du = du + sh[m:m + nv] * wb_ref[k, :, lanes][None]
                    acc_ref[k, :, lanes] += jnp.sum(sh[m:m + nv] * u, axis=0)
            emit(r0, lanes, du.reshape(rows, LANE))


def _fwd_starts(n_taps):
    pad = (n_taps - 1) // 2
    return [HALO - pad + k for k in range(n_taps)]


def _bwd_starts(n_taps):
    pad = (n_taps - 1) // 2
    return [HALO + pad - k for k in range(n_taps)]


def _peer(x, y, c, flip):
    fx, fy, fc = flip
    return (1 - x if fx else x, 1 - y if fy else y, 1 - c if fc else c)


def _all_gather(shards, name):
    n = len(shards)

    def body(*refs):
        ins, outs = refs[:n], refs[n:2 * n]
        send_sems, recv_sems, local_sems = refs[2 * n:]
        x, y, c = lax.axis_index("x"), lax.axis_index("y"), lax.axis_index("c")
        me = 4 * x + 2 * y + c
        local = [pltpu.make_async_copy(ins[j], outs[j].at[me], local_sems.at[j]) for j in range(n)]
        for cp in local:
            cp.start()
        sends, recvs = [], []
        for k, flip in enumerate(FLIPS):
            px, py, pc = _peer(x, y, c, flip)
            peer = 4 * px + 2 * py + pc
            for j in range(n):
                sem = k * n + j
                sends.append(pltpu.make_async_remote_copy(
                    src_ref=ins[j], dst_ref=outs[j].at[me], send_sem=send_sems.at[sem], recv_sem=recv_sems.at[sem],
                    device_id=(px, py, pc), device_id_type=MESH))
                recvs.append(pltpu.make_async_remote_copy(
                    src_ref=ins[j], dst_ref=outs[j].at[peer], send_sem=send_sems.at[sem], recv_sem=recv_sems.at[sem],
                    device_id=(px, py, pc), device_id_type=MESH))
        for cp in sends:
            cp.start()
        for cp in recvs:
            cp.wait_recv()
        for cp in sends:
            cp.wait_send()
        for cp in local:
            cp.wait()

    return pl.pallas_call(
        body, name=name,
        out_shape=[jax.ShapeDtypeStruct((N_DEV,) + s.shape, s.dtype) for s in shards],
        in_specs=[_hbm()] * n, out_specs=[_hbm()] * n,
        scratch_shapes=[pltpu.SemaphoreType.DMA((7 * n,)), pltpu.SemaphoreType.DMA((7 * n,)),
                        pltpu.SemaphoreType.DMA((n,))],
    )(*shards)


def _place_own(src, n_slots_shape, me, name):
    rows, width = src.shape
    tr = next(t for t in (256, 128, 64, 32, 16, SUBLANE) if rows % t == 0)

    def body(me_ref, src_ref, out_ref):
        out_ref[...] = src_ref[...]

    return pl.pallas_call(
        body, name=name,
        grid_spec=pltpu.PrefetchScalarGridSpec(
            num_scalar_prefetch=1, grid=(rows // tr,),
            in_specs=[pl.BlockSpec((tr, width), lambda i, me_ref: (i, 0))],
            out_specs=pl.BlockSpec((None, tr, width), lambda i, me_ref: (me_ref[0], i, 0))),
        out_shape=pltpu.HBM(n_slots_shape, src.dtype),
    )(me, src)


def _place_cast(pieces, me, name):
    n = len(pieces)
    counts = [a.shape[1] if t else a.shape[0] for a, t in pieces]
    width = pieces[0][0].shape[0] if pieces[0][1] else pieces[0][0].shape[1]
    total = sum(counts)

    def body(me_ref, *refs):
        ins, own_ref, land_ref = refs[:n], refs[n], refs[n + 1]
        first = 0
        for (a, transpose), in_ref, count in zip(pieces, ins, counts):
            block = (in_ref[...].T if transpose else in_ref[...]).astype(BF16)
            own_ref[first:first + count, :] = block
            land_ref[first:first + count, :] = block
            first += count

    return pl.pallas_call(
        body, name=name,
        grid_spec=pltpu.PrefetchScalarGridSpec(
            num_scalar_prefetch=1, grid=(1,),
            in_specs=[pl.BlockSpec(a.shape, lambda i, me_ref: (0, 0)) for a, _ in pieces],
            out_specs=[pl.BlockSpec((total, width), lambda i, me_ref: (0, 0)),
                       pl.BlockSpec((None, total, width), lambda i, me_ref: (me_ref[0], 0, 0))]),
        out_shape=[pltpu.HBM((total, width), BF16), pltpu.HBM((N_DEV, total, width), BF16)],
        compiler_params=_params(32),
    )(me, *[_in_hbm(a) for a, _ in pieces])


def _all_gather_two_level(shards, placed, name):
    n = len(shards)
    given = [j for j in range(n) if placed[j] is not None]

    def body(*refs):
        ins, outs = refs[:n], refs[n + len(given):2 * n + len(given)]
        send_sems, recv_sems, local_sems = refs[2 * n + len(given):]
        x, y, c = lax.axis_index("x"), lax.axis_index("y"), lax.axis_index("c")
        me, sibling = (x, y, c), (x, y, 1 - c)
        chips = [(1 - x, y), (x, 1 - y), (1 - x, 1 - y)]

        def slot(j, dev):
            return outs[j].at[4 * dev[0] + 2 * dev[1] + dev[2]]

        def copy(k, j, block, to, src=None):
            return pltpu.make_async_remote_copy(
                src_ref=slot(j, block) if src is None else src, dst_ref=slot(j, block),
                send_sem=send_sems.at[k * n + j], recv_sem=recv_sems.at[k * n + j], device_id=to, device_id_type=MESH)

        local = [pltpu.make_async_copy(ins[j], slot(j, me), local_sems.at[j]) for j in range(n) if j not in given]
        for cp in local:
            cp.start()
        first = [copy(0, j, me, sibling, src=ins[j]) for j in range(n)]
        first += [copy(1 + t, j, me, (*chip, c), src=ins[j]) for t, chip in enumerate(chips) for j in range(n)]
        for cp in first:
            cp.start()
        passed = []
        for t, chip in enumerate(chips):
            for j in range(n):
                copy(1 + t, j, (*chip, c), me).wait_recv()
                passed.append(copy(4 + t, j, (*chip, c), sibling))
                passed[-1].start()
        for j in range(n):
            copy(0, j, sibling, me).wait_recv()
        for t, chip in enumerate(chips):
            for j in range(n):
                copy(4 + t, j, (*chip, 1 - c), me).wait_recv()
        for cp in first + passed:
            cp.wait_send()
        for cp in local:
            cp.wait()

    return pl.pallas_call(
        body, name=name,
        out_shape=[jax.ShapeDtypeStruct((N_DEV,) + s.shape, s.dtype) for s in shards],
        in_specs=[_hbm()] * (n + len(given)), out_specs=[_hbm()] * n,
        input_output_aliases={n + i: j for i, j in enumerate(given)},
        scratch_shapes=[pltpu.SemaphoreType.DMA((7 * n,)), pltpu.SemaphoreType.DMA((7 * n,)),
                        pltpu.SemaphoreType.DMA((n,))],
    )(*shards, *[placed[j] for j in given])


class _Part(NamedTuple):
    src: jax.Array
    scatter: bool
    rows: int
    land: int
    off: int


class _Started(NamedTuple):
    send_sems: jax.Array
    recv_sems: jax.Array
    thru: tuple
    token: jax.Array
    parts: tuple


def _exchange_copies(srcs, lands, send_sems, recv_sems, parts):
    n = len(parts)
    x, y, c = lax.axis_index("x"), lax.axis_index("y"), lax.axis_index("c")
    me = 4 * x + 2 * y + c

    def block(j, dev):
        p = parts[j]
        return srcs[j].at[pl.ds(pl.multiple_of(dev * p.rows, SUBLANE), p.rows), :] if p.scatter else srcs[j]

    def slot(j, index):
        p = parts[j]
        return lands[p.land].at[index, pl.ds(p.off, p.rows), :]

    sends, recvs = [], []
    for k, flip in enumerate(FLIPS):
        px, py, pc = _peer(x, y, c, flip)
        peer = 4 * px + 2 * py + pc
        for j in range(n):
            sems = dict(send_sem=send_sems.at[k * n + j], recv_sem=recv_sems.at[k * n + j],
                        device_id=(px, py, pc), device_id_type=MESH)
            to, got = (k, k) if parts[j].scatter else (me, peer)
            sends.append(pltpu.make_async_remote_copy(src_ref=block(j, peer), dst_ref=slot(j, to), **sems))
            recvs.append(pltpu.make_async_remote_copy(src_ref=block(j, peer), dst_ref=slot(j, got), **sems))
    return sends, recvs


def _exchange_start(parts, lands, name, after=None):
    n, nl = len(parts), len(lands)
    n_in = n + nl + (after is not None)

    def body(*refs):
        srcs, land_refs = refs[:n], refs[n:n + nl]
        send_sems, recv_sems = refs[n_in], refs[n_in + 1]
        token = refs[n_in + 2 + n + nl]
        sends, _ = _exchange_copies(srcs, land_refs, send_sems, recv_sems, parts)
        for cp in sends:
            cp.start()
        token[...] = jnp.zeros_like(token)

    hbm = pl.BlockSpec(memory_space=pltpu.HBM)
    sem = pl.BlockSpec(memory_space=pltpu.SEMAPHORE)
    fresh = lambda s: lax.empty(s.shape, s.dtype) if isinstance(s, jax.ShapeDtypeStruct) else s
    args = [pltpu.with_memory_space_constraint(p.src, pltpu.HBM) for p in parts]
    args += [pltpu.with_memory_space_constraint(fresh(s), pltpu.HBM) for s in lands]
    args += [] if after is None else [after]
    out = pl.pallas_call(
        body, name=name,
        out_shape=(pltpu.SemaphoreType.DMA((7 * n,)), pltpu.SemaphoreType.DMA((7 * n,)),
                   *[pltpu.HBM(a.shape, a.dtype) for a in args[:n + nl]], jax.ShapeDtypeStruct((SUBLANE, LANE), F32)),
        in_specs=[hbm] * (n + nl) + [_hbm()] * (after is not None),
        out_specs=(sem, sem, *[hbm] * (n + nl), _whole()),
        input_output_aliases={j: 2 + j for j in range(n + nl)},
        compiler_params=pltpu.CompilerParams(has_side_effects=pltpu.SideEffectType.DATAFLOW_SIDE_EFFECTING),
    )(*args)
    return _Started(out[0], out[1], tuple(out[2:2 + n + nl]), out[2 + n + nl], tuple(parts))


def _exchange_wait(started, name, after):
    parts = started.parts
    n, nl = len(parts), len(started.thru) - len(parts)

    def body(*refs):
        srcs, land_refs = refs[:n], refs[n:n + nl]
        send_sems, recv_sems = refs[n + nl], refs[n + nl + 1]
        sends, recvs = _exchange_copies(srcs, land_refs, send_sems, recv_sems, parts)
        for cp in sends:
            cp.wait_send()
        for cp in recvs:
            cp.wait_recv()

    hbm = pl.BlockSpec(memory_space=pltpu.HBM)
    sem = pl.BlockSpec(memory_space=pltpu.SEMAPHORE)
    out = pl.pallas_call(
        body, name=name,
        out_shape=tuple(pltpu.HBM(a.shape, a.dtype) for a in started.thru),
        in_specs=[hbm] * (n + nl) + [sem, sem, _hbm()], out_specs=tuple([hbm] * (n + nl)),
        input_output_aliases={j: j for j in range(n + nl)},
        compiler_params=pltpu.CompilerParams(has_side_effects=pltpu.SideEffectType.DATAFLOW_SIDE_EFFECTING),
    )(*started.thru, started.send_sems, started.recv_sems, after)
    return list(out[:n]), list(out[n:])


def _fwd_in(x, g1, b_in, slab, ts, deps):
    n_tok, dm = x.shape
    rows, off, _ = _layout(dm)
    width = 7 * dm

    def body(x_ref, g1_ref, b_ref, slab_ref, proj_ref, p_ref, u_ref, h_ref, w_v, sems):
        copies = _weight_copies(slab_ref, off["win"], rows["win"], w_v, sems, 0)
        _on_first_step(copies, "start")
        _on_first_step(copies, "wait")
        xv = x_ref[...]
        h = (xv * _rms(xv) * g1_ref[...]).astype(BF16)
        h_ref[...] = h
        cols = []
        for j in range(7):
            pj = _dot_nt(h, w_v[pl.ds(j * dm, dm), :]) + b_ref[:, j * dm:(j + 1) * dm]
            proj_ref[:, j * dm:(j + 1) * dm] = pj.astype(proj_ref.dtype)
            if 1 <= j <= 4:
                cols.append(pj)
            if j == 2:
                p_ref[...] = cols[0] * cols[1]
            if j == 4:
                u_ref[...] = cols[2] * _sigmoid(cols[3])

    return pl.pallas_call(
        _after(body, deps), name="fwd_in", grid=(n_tok // ts,),
        in_specs=[_whole()] * len(deps) + [_rows(ts, dm), _whole(), _whole(), _hbm()],
        out_specs=[_rows(ts, width), _rows(ts, dm), _rows(ts, dm), _rows(ts, dm)],
        out_shape=[jax.ShapeDtypeStruct((n_tok, width), BF16), jax.ShapeDtypeStruct((n_tok, dm), F32),
                   jax.ShapeDtypeStruct((n_tok, dm), F32), jax.ShapeDtypeStruct((n_tok, dm), BF16)],
        scratch_shapes=[pltpu.VMEM((width, dm), BF16), pltpu.SemaphoreType.DMA((N_DEV,))],
        compiler_params=_params(56),
    )(*deps, x, g1, b_in, slab)


def _fwd_mix(p, u, proj, x, caw, cab, cbw, cbb, lng, lnb, g1post, slab, ts):
    n_tok, dm = x.shape
    rows, off, _ = _layout(dm)
    n_steps = n_tok // ts

    def body(p_ref, p_prev, p_next, u_ref, u_prev, u_next, bg_ref, za_ref, zb_ref, x_ref,
             caw_ref, cab_ref, cbw_ref, cbb_ref, lng_ref, lnb_ref, g1p_ref, slab_ref,
             va_ref, vb_ref, ya_ref, yb_ref, qa_ref, sb_ref, mg_ref, mix_ref, x1_ref,
             wa_v, wb_v, wo_v, tap_a, tap_b, sems, rolled_a, rolled_b):
        i = pl.program_id(0)
        copies = (_weight_copies(slab_ref, off["wa"], rows["wa"], wa_v, sems, 0)
                  + _weight_copies(slab_ref, off["wb"], rows["wb"], wb_v, sems, N_DEV)
                  + _weight_copies(slab_ref, off["wo"], rows["wo"], wo_v, sems, 2 * N_DEV))
        _on_first_step(copies, "start")

        @pl.when(i == 0)
        def _():
            _broadcast_taps(caw_ref, tap_a, CONV_A)
            _broadcast_taps(cbw_ref, tap_b, CONV_B)

        def emit_a(r0, lanes, acc):
            va_ref[pl.ds(r0, acc.shape[0]), lanes] = acc + cab_ref[:, lanes]

        def emit_b(r0, lanes, acc):
            vb_ref[pl.ds(r0, acc.shape[0]), lanes] = acc + cbb_ref[:, lanes]

        _conv_tile(_with_halos(p_ref, p_prev, p_next, i, n_steps), tap_a, _fwd_starts(CONV_A), ts, dm, emit_a,
                   rolled_a)
        _conv_tile(_with_halos(u_ref, u_prev, u_next, i, n_steps), tap_b, _fwd_starts(CONV_B), ts, dm, emit_b,
                   rolled_b)
        _on_first_step(copies, "wait")

        qa = (bg_ref[...].astype(F32) * va_ref[...]).astype(BF16)
        qa_ref[...] = qa
        ya = _dot(qa, wa_v[...])
        vb = vb_ref[...]
        xc = vb - jnp.mean(vb, axis=-1, keepdims=True)
        rstd = lax.rsqrt(jnp.mean(xc * xc, axis=-1, keepdims=True) + LN_EPS)
        ln = xc * rstd * lng_ref[...] + lnb_ref[...]
        sb = (ln * _sigmoid(ln)).astype(BF16)
        sb_ref[...] = sb
        yb = _dot(sb, wb_v[...])
        ya_ref[...] = ya.astype(BF16)
        yb_ref[...] = yb.astype(BF16)
        merged = (_sigmoid(za_ref[...].astype(F32)) * ya + _sigmoid(zb_ref[...].astype(F32)) * yb).astype(BF16)
        mg_ref[...] = merged
        mix = _dot(merged, wo_v[...])
        mix_ref[...] = mix
        x1_ref[...] = x_ref[...] + mix * _rms(mix) * g1p_ref[...]

    tok = lambda dt: jax.ShapeDtypeStruct((n_tok, dm), dt)
    return pl.pallas_call(
        body, name="fwd_mix", grid=(n_steps,),
        in_specs=(_halo_specs(ts, dm, n_tok) + _halo_specs(ts, dm, n_tok)
                  + [_rows(ts, dm, 0), _rows(ts, dm, 5), _rows(ts, dm, 6), _rows(ts, dm)]
                  + [_whole()] * 7 + [_hbm()]),
        out_specs=[_rows(ts, dm)] * 9,
        out_shape=[tok(F32), tok(F32), tok(BF16), tok(BF16), tok(BF16), tok(BF16), tok(BF16), tok(F32), tok(F32)],
        scratch_shapes=[pltpu.VMEM((dm, dm), BF16), pltpu.VMEM((dm, dm), BF16), pltpu.VMEM((dm, dm), BF16),
                        pltpu.VMEM((CONV_A, SUBLANE, dm), F32), pltpu.VMEM((CONV_B, SUBLANE, dm), F32),
                        pltpu.SemaphoreType.DMA((3 * N_DEV,)),
                        pltpu.VMEM((2, SUBLANE, ts + 2 * HALO, LANE), F32),
                        pltpu.VMEM((2, SUBLANE, ts + 2 * HALO, LANE), F32)],
        compiler_params=_params(48),
    )(p, p, p, u, u, u, proj, proj, proj, x, caw, cab, cbw, cbb, lng, lnb, g1post, slab)


def _mlp_fwd_bwd(x1, mix, tgt, g1post, g2pre, g2post, slab, ts):
    n_tok, dm = x1.shape
    rows, off, _ = _layout(dm)
    ff = 4 * dm

    def body(x1_ref, mix_ref, t_ref, g1p_ref, g2pre_ref, g2post_ref, slab_ref,
             f_ref, df1_ref, h2_ref, df2_ref, dmix_ref, dx1_ref, small_ref, w1_v, w2_v, relu_v, sems):
        w1_copies = _weight_copies(slab_ref, off["w1"], rows["w1"], w1_v, sems, 0)
        w2_copies = _weight_copies(slab_ref, off["w2"], rows["w2"], w2_v, sems, N_DEV)
        _on_first_step(w1_copies + w2_copies, "start")

        @pl.when(pl.program_id(0) == 0)
        def _():
            small_ref[...] = jnp.zeros_like(small_ref)

        _on_first_step(w1_copies + w2_copies, "wait")
        x1v = x1_ref[...]
        r3 = _rms(x1v)
        g2pre = g2pre_ref[...]
        h2 = (x1v * r3 * g2pre).astype(BF16)
        h2_ref[...] = h2
        for c in range(4):
            blk = pl.ds(c * dm, dm)
            relu = jnp.maximum(_dot_nt(h2, w1_v[blk, :]), 0.0)
            relu_v[:, c * dm:(c + 1) * dm] = relu
            f_ref[:, c * dm:(c + 1) * dm] = (relu * relu).astype(BF16)
        f2 = _dot(f_ref[...], w2_v[...])
        r4 = _rms(f2)
        g2post = g2post_ref[...]
        err = x1v + f2 * r4 * g2post - t_ref[...]
        dy = err * (1.0 / dm)
        small_ref[3:4, :] += _colsum(err * err)
        small_ref[0:1, :] += _colsum(dy * f2 * r4)
        df2 = _rms_bwd(dy, f2, r4, g2post).astype(BF16)
        df2_ref[...] = df2
        for c in range(4):
            blk = pl.ds(c * dm, dm)
            df1 = (_dot_nt(df2, w2_v[blk, :]) * (2.0 * relu_v[:, c * dm:(c + 1) * dm])).astype(BF16)
            df1_ref[:, c * dm:(c + 1) * dm] = df1
        dh2 = _dot(df1_ref[...], w1_v[...])
        small_ref[1:2, :] += _colsum(dh2 * x1v * r3)
        dx1 = dy + _rms_bwd(dh2, x1v, r3, g2pre)
        dx1_ref[...] = dx1
        mixv = mix_ref[...]
        r2 = _rms(mixv)
        small_ref[2:3, :] += _colsum(dx1 * mixv * r2)
        dmix_ref[...] = _rms_bwd(dx1, mixv, r2, g1p_ref[...]).astype(BF16)

    tok = lambda w, dt: jax.ShapeDtypeStruct((n_tok, w), dt)
    return pl.pallas_call(
        body, name="mlp_fwd_bwd", grid=(n_tok // ts,),
        in_specs=[_rows(ts, dm)] * 3 + [_whole()] * 3 + [_hbm()],
        out_specs=[_rows(ts, ff), _rows(ts, ff), _rows(ts, dm), _rows(ts, dm), _rows(ts, dm), _rows(ts, dm),
                   pl.BlockSpec((SUBLANE, dm), lambda i: (0, 0))],
        out_shape=[tok(ff, BF16), tok(ff, BF16), tok(dm, BF16), tok(dm, BF16), tok(dm, BF16), tok(dm, F32),
                   jax.ShapeDtypeStruct((SUBLANE, dm), F32)],
        scratch_shapes=[pltpu.VMEM((ff, dm), BF16), pltpu.VMEM((ff, dm), BF16), pltpu.VMEM((ts, ff), F32),
                        pltpu.SemaphoreType.DMA((2 * N_DEV,))],
        compiler_params=_params(56),
    )(x1, mix, tgt, g1post, g2pre, g2post, slab)


def _bwd_mix(dmix, ya, yb, proj, va, vb, lng, lnb, slab, ts, deps):
    n_tok, dm = dmix.shape
    rows, off, _ = _layout(dm)
    n_steps = n_tok // ts

    def body(dmix_ref, ya_ref, yb_ref, bg_ref, za_ref, zb_ref, va_ref, vb_ref, lng_ref, lnb_ref, slab_ref,
             dpa_ref, dya_ref, dyb_ref, dva_ref, dvb_ref, small_ref, wa_v, wb_v, wo_v, sems):
        wo_copies = _weight_copies(slab_ref, off["wo"], rows["wo"], wo_v, sems, 2 * N_DEV)
        ab_copies = (_weight_copies(slab_ref, off["wa"], rows["wa"], wa_v, sems, 0)
                     + _weight_copies(slab_ref, off["wb"], rows["wb"], wb_v, sems, N_DEV))
        _on_first_step(wo_copies + ab_copies, "start")

        @pl.when(pl.program_id(0) == 0)
        def _():
            small_ref[...] = jnp.zeros_like(small_ref)

        _on_first_step(wo_copies, "wait")
        dmerged = _dot_nt(dmix_ref[...], wo_v[...])
        _on_first_step(ab_copies, "wait")
        sa = _sigmoid(za_ref[...].astype(F32))
        sg = _sigmoid(zb_ref[...].astype(F32))
        dza = dmerged * ya_ref[...].astype(F32) * sa * (1.0 - sa)
        dzb = dmerged * yb_ref[...].astype(F32) * sg * (1.0 - sg)
        dpa_ref[:, dm:2 * dm] = dza.astype(BF16)
        dpa_ref[:, 2 * dm:3 * dm] = dzb.astype(BF16)
        small_ref[5:6, :] += _colsum(dza)
        small_ref[6:7, :] += _colsum(dzb)

        dya = (dmerged * sa).astype(BF16)
        dya_ref[...] = dya
        dqa = _dot_nt(dya, wa_v[...])
        dbg = dqa * va_ref[...]
        dpa_ref[:, 0:dm] = dbg.astype(BF16)
        small_ref[4:5, :] += _colsum(dbg)
        dva = dqa * bg_ref[...].astype(F32)
        dva_ref[...] = dva
        small_ref[2:3, :] += _colsum(dva)

        dyb = (dmerged * sg).astype(BF16)
        dyb_ref[...] = dyb
        dsb = _dot_nt(dyb, wb_v[...])
        vb = vb_ref[...]
        xc = vb - jnp.mean(vb, axis=-1, keepdims=True)
        rstd = lax.rsqrt(jnp.mean(xc * xc, axis=-1, keepdims=True) + LN_EPS)
        nrm = xc * rstd
        lng_v = lng_ref[...]
        ln = nrm * lng_v + lnb_ref[...]
        sl = _sigmoid(ln)
        dln = dsb * (sl * (1.0 + ln * (1.0 - sl)))
        small_ref[0:1, :] += _colsum(dln * nrm)
        small_ref[1:2, :] += _colsum(dln)
        dn = dln * lng_v
        dvb = rstd * (dn - jnp.mean(dn, axis=-1, keepdims=True)
                      - nrm * jnp.mean(dn * nrm, axis=-1, keepdims=True))
        dvb_ref[...] = dvb
        small_ref[3:4, :] += _colsum(dvb)

    tok = lambda w, dt: jax.ShapeDtypeStruct((n_tok, w), dt)
    return pl.pallas_call(
        _after(body, deps), name="bwd_mix", grid=(n_steps,),
        in_specs=([_whole()] * len(deps) + [_rows(ts, dm)] * 3
                  + [_rows(ts, dm, 0), _rows(ts, dm, 5), _rows(ts, dm, 6)]
                  + [_rows(ts, dm)] * 2 + [_whole()] * 2 + [_hbm()]),
        out_specs=[_rows(ts, 3 * dm), _rows(ts, dm), _rows(ts, dm), _rows(ts, dm), _rows(ts, dm),
                   pl.BlockSpec((SUBLANE, dm), lambda i: (0, 0))],
        out_shape=[tok(3 * dm, BF16), tok(dm, BF16), tok(dm, BF16), tok(dm, F32), tok(dm, F32),
                   jax.ShapeDtypeStruct((SUBLANE, dm), F32)],
        scratch_shapes=[pltpu.VMEM((dm, dm), BF16), pltpu.VMEM((dm, dm), BF16), pltpu.VMEM((dm, dm), BF16),
                        pltpu.SemaphoreType.DMA((3 * N_DEV,))],
        compiler_params=_params(56),
    )(*deps, dmix, ya, yb, proj, proj, proj, va, vb, lng, lnb, slab)


def _bwd_conv(dva, dvb, p, u, proj, caw, cbw, ts, deps):
    n_tok, dm = dva.shape
    n_steps = n_tok // ts
    small_rows = 40

    def body(dva_ref, dva_prev, dva_next, dvb_ref, dvb_prev, dvb_next, p_ref, u_ref,
             cg_ref, ha_ref, a_ref, g_ref, caw_ref, cbw_ref,
             dproj_ref, small_ref,
             dp_v, du_v, tap_a, tap_b, gwa_v, gwb_v):
        i = pl.program_id(0)

        @pl.when(i == 0)
        def _():
            _broadcast_taps(caw_ref, tap_a, CONV_A)
            _broadcast_taps(cbw_ref, tap_b, CONV_B)
            small_ref[...] = jnp.zeros_like(small_ref)
            gwa_v[...] = jnp.zeros_like(gwa_v)
            gwb_v[...] = jnp.zeros_like(gwb_v)

        def emit_dp(r0, lanes, acc):
            dp_v[pl.ds(r0, acc.shape[0]), lanes] = acc

        def emit_du(r0, lanes, acc):
            du_v[pl.ds(r0, acc.shape[0]), lanes] = acc

        _conv_bwd_tile(_with_halos(dva_ref, dva_prev, dva_next, i, n_steps), p_ref, tap_a, gwa_v, CONV_A, ts, dm,
                       emit_dp)
        _conv_bwd_tile(_with_halos(dvb_ref, dvb_prev, dvb_next, i, n_steps), u_ref, tap_b, gwb_v, CONV_B, ts, dm,
                       emit_du)

        dp = dp_v[...]
        dcg = dp * ha_ref[...].astype(F32)
        dha = dp * cg_ref[...].astype(F32)
        du = du_v[...]
        sg = _sigmoid(g_ref[...].astype(F32))
        da = du * sg
        dg = du * a_ref[...].astype(F32) * sg * (1.0 - sg)
        dproj_ref[:, 0:dm] = dcg.astype(BF16)
        dproj_ref[:, dm:2 * dm] = dha.astype(BF16)
        dproj_ref[:, 2 * dm:3 * dm] = da.astype(BF16)
        dproj_ref[:, 3 * dm:4 * dm] = dg.astype(BF16)
        small_ref[3:4, :] += _colsum(dcg)
        small_ref[4:5, :] += _colsum(dha)
        small_ref[5:6, :] += _colsum(da)
        small_ref[6:7, :] += _colsum(dg)

        @pl.when(i == n_steps - 1)
        def _():
            for k in range(CONV_A):
                small_ref[k:k + 1, :] = _colsum(gwa_v[k])
            for k in range(CONV_B):
                small_ref[SUBLANE + k:SUBLANE + k + 1, :] = _colsum(gwb_v[k])

    return pl.pallas_call(
        _after(body, deps), name="bwd_conv", grid=(n_steps,),
        in_specs=([_whole()] * len(deps) + _halo_specs(ts, dm, n_tok) * 2 + [_rows(ts, dm)] * 2
                  + [_rows(ts, dm, 1), _rows(ts, dm, 2), _rows(ts, dm, 3), _rows(ts, dm, 4)]
                  + [_whole()] * 2),
        out_specs=[_rows(ts, 4 * dm), pl.BlockSpec((small_rows, dm), lambda i: (0, 0))],
        out_shape=[jax.ShapeDtypeStruct((n_tok, 4 * dm), BF16), jax.ShapeDtypeStruct((small_rows, dm), F32)],
        scratch_shapes=[pltpu.VMEM((ts, dm), F32), pltpu.VMEM((ts, dm), F32),
                        pltpu.VMEM((CONV_A, SUBLANE, dm), F32), pltpu.VMEM((CONV_B, SUBLANE, dm), F32),
                        pltpu.VMEM((CONV_A, SUBLANE, dm), F32), pltpu.VMEM((CONV_B, SUBLANE, dm), F32)],
        compiler_params=_params(48),
    )(*deps, dva, dva, dva, dvb, dvb, dvb, p, u, proj, proj, proj, proj, caw, cbw)


def _bwd_in(dpa, dpb, x, dx1, g1, slab, ts, deps):
    n_tok, dm = x.shape
    rows, off, _ = _layout(dm)
    width = 7 * dm

    def body(dpa_ref, dpb_ref, x_ref, dx1_ref, g1_ref, slab_ref, gx_ref, small_ref, w_v, sems):
        copies = _weight_copies(slab_ref, off["win"], rows["win"], w_v, sems, 0)
        _on_first_step(copies, "start")

        @pl.when(pl.program_id(0) == 0)
        def _():
            small_ref[...] = jnp.zeros_like(small_ref)

        _on_first_step(copies, "wait")
        dh = (_dot(dpa_ref[:, 0:dm], w_v[0:dm, :]) + _dot(dpb_ref[...], w_v[dm:5 * dm, :])
              + _dot(dpa_ref[:, dm:3 * dm], w_v[5 * dm:7 * dm, :]))
        xv = x_ref[...]
        r1 = _rms(xv)
        small_ref[0:1, :] += _colsum(dh * xv * r1)
        gx_ref[...] = dx1_ref[...] + _rms_bwd(dh, xv, r1, g1_ref[...])

    return pl.pallas_call(
        _after(body, deps), name="bwd_in", grid=(n_tok // ts,),
        in_specs=[_whole()] * len(deps) + [_rows(ts, 3 * dm), _rows(ts, 4 * dm), _rows(ts, dm), _rows(ts, dm),
                                           _whole(), _hbm()],
        out_specs=[_rows(ts, dm), pl.BlockSpec((SUBLANE, dm), lambda i: (0, 0))],
        out_shape=[jax.ShapeDtypeStruct((n_tok, dm), F32), jax.ShapeDtypeStruct((SUBLANE, dm), F32)],
        scratch_shapes=[pltpu.VMEM((width, dm), BF16), pltpu.SemaphoreType.DMA((N_DEV,))],
        compiler_params=_params(56),
    )(*deps, dpa, dpb, x, dx1, g1, slab)


def _wgrad(a, b, name, tm, tk, out_dtype):
    n_tok, m = a.shape
    n = b.shape[1]
    k_steps = n_tok // tk

    def body(a_ref, b_ref, o_ref, acc_v):
        k = pl.program_id(1)

        @pl.when(k == 0)
        def _():
            acc_v[...] = jnp.zeros_like(acc_v)

        acc_v[...] += _dot_tn(a_ref[...], b_ref[...])

        @pl.when(k == k_steps - 1)
        def _():
            o_ref[...] = acc_v[...].astype(o_ref.dtype)

    return pl.pallas_call(
        body, name=name, grid=(m // tm, k_steps),
        in_specs=[pl.BlockSpec((tk, tm), lambda i, k: (k, i)), pl.BlockSpec((tk, n), lambda i, k: (k, 0))],
        out_specs=pl.BlockSpec((tm, n), lambda i, k: (i, 0)),
        out_shape=pltpu.HBM((m, n), out_dtype),
        scratch_shapes=[pltpu.VMEM((tm, n), F32)],
        compiler_params=pltpu.CompilerParams(dimension_semantics=("arbitrary", "arbitrary"),
                                             vmem_limit_bytes=40 * MIB),
    )(a, b)


def _wgrad_in(dpa, dpb, h, tk):
    n_tok, dm = h.shape
    k_steps = n_tok // tk
    last = k_steps - 1

    def from_a(i):
        return (i == 0) | (i >= 5)

    def body(a_ref, b_ref, h_ref, o_ref, acc_v):
        i, k = pl.program_id(0), pl.program_id(1)

        @pl.when(k == 0)
        def _():
            acc_v[...] = jnp.zeros_like(acc_v)

        @pl.when(from_a(i))
        def _():
            acc_v[...] += _dot_tn(a_ref[...], h_ref[...])

        @pl.when(jnp.logical_not(from_a(i)))
        def _():
            acc_v[...] += _dot_tn(b_ref[...], h_ref[...])

        @pl.when(k == last)
        def _():
            o_ref[...] = acc_v[...].astype(o_ref.dtype)

    a_index = lambda i, k: (jnp.where(from_a(i), k, last), jnp.where(i >= 5, i - 4, 0))
    b_index = lambda i, k: (jnp.where(from_a(i), jnp.where(i == 0, 0, last), k), jnp.clip(i - 1, 0, 3))
    return pl.pallas_call(
        body, name="wgrad_in", grid=(7, k_steps),
        in_specs=[pl.BlockSpec((tk, dm), a_index), pl.BlockSpec((tk, dm), b_index),
                  pl.BlockSpec((tk, dm), lambda i, k: (k, 0))],
        out_specs=pl.BlockSpec((dm, dm), lambda i, k: (i, 0)),
        out_shape=pltpu.HBM((7 * dm, dm), BF16),
        scratch_shapes=[pltpu.VMEM((dm, dm), F32)],
        compiler_params=pltpu.CompilerParams(dimension_semantics=("arbitrary", "arbitrary"),
                                             vmem_limit_bytes=48 * MIB),
    )(dpa, dpb, h)


def _adamw(w, g, m, v):
    m = ADAM_B1 * m + (1.0 - ADAM_B1) * g
    v = ADAM_B2 * v + (1.0 - ADAM_B2) * (g * g)
    m_hat = m / (1.0 - ADAM_B1 ** ADAM_STEP)
    v_hat = v / (1.0 - ADAM_B2 ** ADAM_STEP)
    delta = -ADAM_LR * (m_hat / (jnp.sqrt(v_hat) + ADAM_EPS) + ADAM_WD * w)
    return delta, m, v


def _adam_big(recv, part, me, off, rows, w, m, v, transpose, name, tr):
    dm = recv.shape[2]
    per = rows // tr

    def body(me_ref, own_ref, r_ref, w_ref, m_ref, v_ref, g_ref, d_ref, mo_ref, vo_ref):
        g = own_ref[...].astype(F32)
        for k in range(len(FLIPS)):
            g = g + r_ref[k].astype(F32)
        if transpose:
            g = g.T
        delta, m_new, v_new = _adamw(w_ref[...], g, m_ref[...], v_ref[...])
        g_ref[...] = g
        d_ref[...] = delta
        mo_ref[...] = m_new
        vo_ref[...] = v_new

    if transpose:
        blk = pl.BlockSpec((dm, tr), lambda i, me_ref: (0, i))
    else:
        blk = pl.BlockSpec((tr, dm), lambda i, me_ref: (i, 0))
    first = off // tr
    return pl.pallas_call(
        body, name=name,
        grid_spec=pltpu.PrefetchScalarGridSpec(
            num_scalar_prefetch=1, grid=(per,),
            in_specs=[pl.BlockSpec((tr, dm), lambda i, me_ref: (me_ref[0] * per + i, 0)),
                      pl.BlockSpec((len(FLIPS), tr, dm), lambda i, me_ref: (0, first + i, 0)), blk, blk, blk],
            out_specs=[blk] * 4),
        out_shape=[jax.ShapeDtypeStruct(w.shape, F32)] * 4,
        compiler_params=_params(32),
    )(me, *[_in_hbm(a) for a in (part, recv, w, m, v)])


LOSS_ROW = 15
CONV_A_ROW = 16
CONV_B_ROW = 24


def _adam_small(recv_small, recv_last, me, params, d_model):
    n = len(params)
    cw = d_model // N_DEV

    def body(me_ref, r_ref, rc_ref, l_ref, *refs):
        ins, loss_ref, outs = refs[:3 * n], refs[3 * n], refs[3 * n + 1:3 * n + 1 + 4 * n]
        g_v, gc_v, last_v = refs[3 * n + 1 + 4 * n:]
        g, gc, last = r_ref[0], rc_ref[0], l_ref[0]
        for d in range(1, N_DEV):
            g, gc, last = g + r_ref[d], gc + rc_ref[d], last + l_ref[d]
        g_v[...], gc_v[...], last_v[...] = g, gc, last
        loss_ref[...] = (0.5 / d_model) * jnp.sum(g_v[LOSS_ROW:LOSS_ROW + 1, :], axis=-1, keepdims=True)
        for j, (row0, own_columns, (w, _, _)) in enumerate(params):
            w_ref, m_ref, v_ref = ins[3 * j:3 * j + 3]
            source = gc_v if own_columns else (last_v if row0 == 0 else g_v)
            grad = source[row0:row0 + w.shape[0], :]
            delta, m_new, v_new = _adamw(w_ref[...], grad, m_ref[...], v_ref[...])
            for ref, val in zip(outs[4 * j:4 * j + 4], (grad, delta, m_new, v_new)):
                ref[...] = val

    full = lambda shape: pl.BlockSpec(shape, lambda i, me_ref: (0,) * len(shape))
    stack_rows = recv_small.shape[1]
    flat = [a for _, _, triple in params for a in triple]
    shapes = [w.shape for _, _, (w, _, _) in params for _ in range(4)]
    out = pl.pallas_call(
        body, name="adam_small",
        grid_spec=pltpu.PrefetchScalarGridSpec(
            num_scalar_prefetch=1, grid=(1,),
            in_specs=[full(recv_small.shape),
                      pl.BlockSpec((N_DEV, stack_rows, cw), lambda i, me_ref: (0, 0, me_ref[0])),
                      full(recv_last.shape)] + [full(a.shape) for a in flat],
            out_specs=[full((1, 1))] + [full(s) for s in shapes],
            scratch_shapes=[pltpu.VMEM((stack_rows, d_model), F32), pltpu.VMEM((stack_rows, cw), F32),
                            pltpu.VMEM(recv_last.shape[1:], F32)]),
        out_shape=[jax.ShapeDtypeStruct((1, 1), F32)] + [jax.ShapeDtypeStruct(s, F32) for s in shapes],
    )(me, *[_in_hbm(a) for a in (recv_small, recv_small, recv_last, *flat)])
    return out[0], [tuple(out[1 + 4 * j:5 + 4 * j]) for j in range(n)]


def _tile(n_tok, want):
    return min(want, n_tok)


def kernel(x, norm1_pre_g, w_in, b_in, conv_a_w, conv_a_b, w_a_out, conv_b_w, conv_b_b, ln_b_g, ln_b_b, w_b_out, w_o, norm1_post_g, norm2_pre_g, w_mlp_in, w_mlp_out, norm2_post_g, loss_target, m_norm1_pre_g, m_w_in, m_b_in, m_conv_a_w, m_conv_a_b, m_w_a_out, m_conv_b_w, m_conv_b_b, m_ln_b_g, m_ln_b_b, m_w_b_out, m_w_o, m_norm1_post_g, m_norm2_pre_g, m_w_mlp_in, m_w_mlp_out, m_norm2_post_g, v_norm1_pre_g, v_w_in, v_b_in, v_conv_a_w, v_conv_a_b, v_w_a_out, v_conv_b_w, v_conv_b_b, v_ln_b_g, v_ln_b_b, v_w_b_out, v_w_o, v_norm1_post_g, v_norm2_pre_g, v_w_mlp_in, v_w_mlp_out, v_norm2_post_g):
    n_tok, dm = x.shape[1], x.shape[2]
    rows, off, slab_rows = _layout(dm)
    cw = dm // N_DEV
    xs = x.reshape(n_tok, dm)
    tgt = loss_target.reshape(n_tok, dm)
    row = lambda vec: vec.reshape(1, -1)
    scattered = lambda group: jax.ShapeDtypeStruct((len(FLIPS), slab_rows[group], dm), BF16)
    tm, tk = min(dm, 1024), _tile(n_tok, 2048)
    me = (4 * lax.axis_index("x") + 2 * lax.axis_index("y") + lax.axis_index("c")).astype(jnp.int32).reshape(1)

    conv_own = jnp.concatenate([conv_a_w, jnp.zeros((SUBLANE - CONV_A, cw), F32), conv_b_w,
                                jnp.zeros((1, cw), F32)], axis=0)
    own_in, land_in = _place_cast([(w_in, True)], me, "place_w_in")
    slab_in, conv_all = _all_gather_two_level([own_in, conv_own], [land_in, None], "gather_w_in")
    conv_full = conv_all.transpose(1, 0, 2).reshape(conv_own.shape[0], dm)
    caw, cbw = conv_full[0:CONV_A], conv_full[SUBLANE:SUBLANE + CONV_B]
    own_abo, land_abo = _place_cast([(w_a_out, False), (w_b_out, False), (w_o, False)], me, "place_abo")
    own_mlp, land_mlp = _place_cast([(w_mlp_in, True), (w_mlp_out, False)], me, "place_mlp")
    ag_abo = _exchange_start([_Part(own_abo, False, slab_rows["abo"], 0, 0)], [land_abo],
                             "gather_abo_start", after=slab_in)
    ag_mlp = _exchange_start([_Part(own_mlp, False, slab_rows["mlp"], 0, 0)], [land_mlp],
                             "gather_mlp_start", after=ag_abo.token)

    proj, p, u, h = _fwd_in(xs, row(norm1_pre_g), row(b_in), slab_in, _tile(n_tok, 512),
                            [ag_abo.token, ag_mlp.token])
    _, (slab_abo,) = _exchange_wait(ag_abo, "gather_abo_wait", after=proj)
    va, vb, ya, yb, qa, sb, merged, mix, x1 = _fwd_mix(
        p, u, proj, xs, caw, row(conv_a_b), cbw, row(conv_b_b), row(ln_b_g), row(ln_b_b), row(norm1_post_g),
        slab_abo, _tile(n_tok, 256))
    _, (slab_mlp,) = _exchange_wait(ag_mlp, "gather_mlp_wait", after=x1)
    f, df1, h2, df2, dmix, dx1, small_mlp = _mlp_fwd_bwd(
        x1, mix, tgt, row(norm1_post_g), row(norm2_pre_g), row(norm2_post_g), slab_mlp, _tile(n_tok, 256))

    rs_mlp = _exchange_start(
        [_Part(_wgrad(df1, h2, "wgrad_mlp_in", tm, tk, BF16), True, rows["w1"], 0, off["w1"]),
         _Part(_wgrad(f, df2, "wgrad_mlp_out", tm, tk, BF16), True, rows["w2"], 0, off["w2"])],
        [scattered("mlp")], "scatter_mlp_start")
    dpa, dya, dyb, dva, dvb, small_mix = _bwd_mix(
        dmix, ya, yb, proj, va, vb, row(ln_b_g), row(ln_b_b), slab_abo, _tile(n_tok, 512), [rs_mlp.token])
    rs_abo = _exchange_start(
        [_Part(_wgrad(qa, dya, "wgrad_a_out", tm, tk, BF16), True, rows["wa"], 0, off["wa"]),
         _Part(_wgrad(sb, dyb, "wgrad_b_out", tm, tk, BF16), True, rows["wb"], 0, off["wb"]),
         _Part(_wgrad(merged, dmix, "wgrad_o", tm, tk, BF16), True, rows["wo"], 0, off["wo"])],
        [scattered("abo")], "scatter_abo_start")
    dpb, small_conv = _bwd_conv(dva, dvb, p, u, proj, caw, cbw, _tile(n_tok, 256), [rs_abo.token])

    zeros = lambda r: jnp.zeros((r, dm), F32)
    small = jnp.concatenate([
        zeros(1),
        small_mix[2:3],
        small_mix[3:4],
        small_mix[0:2],
        small_mlp[2:3],
        small_mlp[1:2],
        small_mlp[0:1],
        small_mix[4:5], small_conv[3:7], small_mix[5:7],
        small_mlp[3:4],
        small_conv[0:CONV_A], zeros(SUBLANE - CONV_A),
        small_conv[8:8 + CONV_B], zeros(1),
    ], axis=0)

    rs_in = _exchange_start(
        [_Part(_wgrad_in(dpa, dpb, h, tk), True, rows["win"], 0, off["win"]),
         _Part(small, False, small.shape[0], 1, 0)],
        [scattered("in"), _place_own(small, (N_DEV,) + small.shape, me, "place_small")], "scatter_in_start")
    grad_x, small_in = _bwd_in(dpa, dpb, xs, dx1, row(norm1_pre_g), slab_in, _tile(n_tok, 512), [rs_in.token])

    tr = min(LANE, rows["wa"])
    (g_w1, g_w2), (recv_mlp,) = _exchange_wait(rs_mlp, "scatter_mlp_wait", after=grad_x)
    (g_wa, g_wb, g_wo), (recv_abo,) = _exchange_wait(rs_abo, "scatter_abo_wait", after=grad_x)
    big = {
        "w_mlp_in": _adam_big(recv_mlp, g_w1, me, off["w1"], rows["w1"], w_mlp_in, m_w_mlp_in, v_w_mlp_in, True,
                              "adam_w_mlp_in", tr),
        "w_mlp_out": _adam_big(recv_mlp, g_w2, me, off["w2"], rows["w2"], w_mlp_out, m_w_mlp_out, v_w_mlp_out,
                               False, "adam_w_mlp_out", tr),
        "w_a_out": _adam_big(recv_abo, g_wa, me, off["wa"], rows["wa"], w_a_out, m_w_a_out, v_w_a_out, False,
                             "adam_w_a_out", tr),
        "w_b_out": _adam_big(recv_abo, g_wb, me, off["wb"], rows["wb"], w_b_out, m_w_b_out, v_w_b_out, False,
                             "adam_w_b_out", tr),
        "w_o": _adam_big(recv_abo, g_wo, me, off["wo"], rows["wo"], w_o, m_w_o, v_w_o, False, "adam_w_o", tr),
    }
    (g_win, _), (recv_in, recv_small) = _exchange_wait(rs_in, "scatter_in_wait", after=big["w_o"][3])
    recv_last, = _all_gather([small_in], "gather_last")
    big["w_in"] = _adam_big(recv_in, g_win, me, off["win"], rows["win"], w_in, m_w_in, v_w_in, True, "adam_w_in", tr)

    small_names = ("norm1_pre_g", "conv_a_b", "conv_b_b", "ln_b_g", "ln_b_b", "norm1_post_g", "norm2_pre_g",
                   "norm2_post_g")
    given = dict(
        norm1_pre_g=(norm1_pre_g, m_norm1_pre_g, v_norm1_pre_g), conv_a_b=(conv_a_b, m_conv_a_b, v_conv_a_b),
        conv_b_b=(conv_b_b, m_conv_b_b, v_conv_b_b), ln_b_g=(ln_b_g, m_ln_b_g, v_ln_b_g),
        ln_b_b=(ln_b_b, m_ln_b_b, v_ln_b_b), norm1_post_g=(norm1_post_g, m_norm1_post_g, v_norm1_post_g),
        norm2_pre_g=(norm2_pre_g, m_norm2_pre_g, v_norm2_pre_g),
        norm2_post_g=(norm2_post_g, m_norm2_post_g, v_norm2_post_g))
    params = [(j, False, tuple(row(a) for a in given[name])) for j, name in enumerate(small_names)]
    params.append((SUBLANE, False, tuple(a.reshape(7, dm) for a in (b_in, m_b_in, v_b_in))))
    params.append((CONV_A_ROW, True, (conv_a_w, m_conv_a_w, v_conv_a_w)))
    params.append((CONV_B_ROW, True, (conv_b_w, m_conv_b_w, v_conv_b_w)))
    loss, small_out = _adam_small(recv_small, recv_last, me, params, dm)
    small_leaves = {name: tuple(a.reshape(dm) for a in small_out[j]) for j, name in enumerate(small_names)}
    small_leaves["b_in"] = tuple(a.reshape(7 * dm) for a in small_out[len(small_names)])
    small_leaves["conv_a_w"] = small_out[len(small_names) + 1]
    small_leaves["conv_b_w"] = small_out[len(small_names) + 2]

    order = ("norm1_pre_g", "w_in", "b_in", "conv_a_w", "conv_a_b", "w_a_out", "conv_b_w", "conv_b_b", "ln_b_g",
             "ln_b_b", "w_b_out", "w_o", "norm1_post_g", "norm2_pre_g", "w_mlp_in", "w_mlp_out", "norm2_post_g")
    leaves = [big[name] if name in big else small_leaves[name] for name in order]
    grads, deltas, new_m, new_v = zip(*leaves)
    return (loss.reshape(()), grad_x.reshape(x.shape), *grads, *deltas, *new_m, *new_v)
```

```python
from typing import NamedTuple

import jax
import jax.numpy as jnp
from jax import lax
from jax.experimental import pallas as pl
from jax.experimental.pallas import tpu as pltpu

F32 = jnp.float32
BF16 = jnp.bfloat16

RMS_EPS = 1e-6
LN_EPS = 1e-5
ADAM_LR = 0.001
ADAM_B1 = 0.9
ADAM_B2 = 0.999
ADAM_EPS = 1e-08
ADAM_WD = 0.01
ADAM_STEP = 10

N_DEV = 8
CONV_A = 3
CONV_B = 31
LANE = 128
SUBLANE = 8
HALO = 16
FWD_CONV_ROWS = 32
BWD_CONV_ROWS = 64
MIB = 1 << 20
FLIPS = ((0, 0, 1), (0, 1, 0), (1, 0, 0), (0, 1, 1), (1, 0, 1), (1, 1, 0), (1, 1, 1))
MESH = pl.DeviceIdType.MESH


def _layout(d_model):
    e = d_model // N_DEV
    rows = {"win": 7 * e, "w1": 4 * e, "w2": 4 * e, "wa": e, "wb": e, "wo": e}
    off = {"win": 0, "w1": 0, "w2": 4 * e, "wa": 0, "wb": e, "wo": 2 * e}
    return rows, off, {"in": 7 * e, "mlp": 8 * e, "abo": 3 * e}


def _after(body, deps):
    def ordered(*refs):
        return body(*refs[len(deps):])
    return ordered


def _params(vmem_mib):
    return pltpu.CompilerParams(dimension_semantics=("arbitrary",), vmem_limit_bytes=vmem_mib * MIB)


def _whole():
    return pl.BlockSpec(memory_space=pltpu.VMEM)


def _hbm():
    return pl.BlockSpec(memory_space=pl.ANY)


def _in_hbm(a):
    return pltpu.with_memory_space_constraint(a, pltpu.HBM)


def _rows(ts, width, col=0):
    return pl.BlockSpec((ts, width), lambda i: (i, col))


def _halo_specs(ts, width, n_rows):
    per = ts // HALO
    last = n_rows // HALO - 1
    return [
        pl.BlockSpec((ts, width), lambda i: (i, 0)),
        pl.BlockSpec((HALO, width), lambda i: (jnp.maximum(i * per - 1, 0), 0)),
        pl.BlockSpec((HALO, width), lambda i: (jnp.minimum((i + 1) * per, last), 0)),
    ]


def _dot(a, b):
    return jnp.dot(a, b, preferred_element_type=F32)


def _dot_nt(a, b):
    return lax.dot_general(a, b, (((1,), (1,)), ((), ())), preferred_element_type=F32)


def _dot_tn(a, b):
    return lax.dot_general(a, b, (((0,), (0,)), ((), ())), preferred_element_type=F32)


def _rms(u):
    return lax.rsqrt(jnp.mean(u * u, axis=-1, keepdims=True) + RMS_EPS)


def _rms_bwd(dz, u, r, g):
    dzg = dz * g
    return r * dzg - u * (r * r * r) * jnp.mean(dzg * u, axis=-1, keepdims=True)


def _colsum(v):
    return jnp.sum(v, axis=0, keepdims=True)


def _sigmoid(v):
    return jax.nn.sigmoid(v)


def _weight_copies(slab_ref, off, rows, dst_ref, sems, first_sem):
    return [pltpu.make_async_copy(slab_ref.at[d, pl.ds(off, rows), :], dst_ref.at[pl.ds(d * rows, rows), :],
                                  sems.at[first_sem + d]) for d in range(N_DEV)]


def _on_first_step(copies, method):
    @pl.when(pl.program_id(0) == 0)
    def _():
        for cp in copies:
            getattr(cp, method)()


def _with_halos(main_ref, prev_ref, next_ref, i, n_steps):
    return (main_ref, jnp.where(i > 0, prev_ref[...], 0.0), jnp.where(i < n_steps - 1, next_ref[...], 0.0))


def _broadcast_taps(w_ref, wb_ref, n_taps):
    for k in range(n_taps):
        wb_ref[k] = jnp.broadcast_to(w_ref[k:k + 1, :], wb_ref.shape[1:])


def _conv_tile(tile, wb_ref, starts, ts, width, emit, rolled_ref, rows=FWD_CONV_ROWS):
    main_ref, prev, nxt = tile
    span = ts + 2 * HALO
    nv = rows // SUBLANE
    for cb in range(width // LANE):
        lanes = slice(cb * LANE, (cb + 1) * LANE)
        slot = cb % 2
        window = jnp.concatenate([prev[:, lanes], main_ref[:, lanes], nxt[:, lanes]], axis=0)
        for b in sorted({st % SUBLANE for st in starts}):
            rolled_ref[slot, b] = window if b == 0 else pltpu.roll(window, span - b, axis=0)
        for r0 in range(0, ts, rows):
            acc = jnp.zeros((nv, SUBLANE, LANE), F32)
            for k, st in enumerate(starts):
                shifted = rolled_ref[slot, st % SUBLANE, pl.ds(r0 + st - st % SUBLANE, rows), :]
                acc = acc + shifted.reshape(nv, SUBLANE, LANE) * wb_ref[k, :, lanes][None]
            emit(r0, pl.ds(cb * LANE, LANE), acc.reshape(rows, LANE))


def _window(tile, r0, cb, ts, rows):
    main_ref, prev, nxt = tile
    lanes = slice(cb * LANE, (cb + 1) * LANE)
    lo, hi = max(r0 - HALO, 0), min(r0 + rows + HALO, ts)
    pieces = [prev[:, lanes]] if r0 - HALO < 0 else []
    pieces.append(main_ref[lo:hi, lanes])
    if r0 + rows + HALO > ts:
        pieces.append(nxt[:, lanes])
    return pieces[0] if len(pieces) == 1 else jnp.concatenate(pieces, axis=0)


def _phases(starts):
    groups = {}
    for k, st in enumerate(starts):
        groups.setdefault(st % SUBLANE, []).append((k, st // SUBLANE))
    return sorted(groups.items())


def _shifted(blk, b):
    n = blk.shape[0]
    rolled = blk if b == 0 else pltpu.roll(blk, n - b, axis=0)
    return rolled.reshape(n // SUBLANE, SUBLANE, blk.shape[1])


def _conv_bwd_tile(dv_tile, u_ref, wb_ref, acc_ref, n_taps, ts, width, emit, rows=BWD_CONV_ROWS):
    groups = _phases(_bwd_starts(n_taps))
    nv = rows // SUBLANE
    for r0 in range(0, ts, rows):
        for cb in range(width // LANE):
            lanes = pl.ds(cb * LANE, LANE)
            blk = _window(dv_tile, r0, cb, ts, rows)
            u = u_ref[pl.ds(r0, rows), lanes].reshape(nv, SUBLANE, LANE)
            du = jnp.zeros((nv, SUBLANE, LANE), F32)
            for b, taps in groups:
                sh = _shifted(blk, b)
                for k, m in taps:
                    du = du + sh[m:m + nv] * wb_ref[k, :, lanes][None]
                    acc_ref[k, :, lanes] += jnp.sum(sh[m:m + nv] * u, axis=0)
            emit(r0, lanes, du.reshape(rows, LANE))


def _fwd_starts(n_taps):
    pad = (n_taps - 1) // 2
    return [HALO - pad + k for k in range(n_taps)]


def _bwd_starts(n_taps):
    pad = (n_taps - 1) // 2
    return [HALO + pad - k for k in range(n_taps)]


def _peer(x, y, c, flip):
    fx, fy, fc = flip
    return (1 - x if fx else x, 1 - y if fy else y, 1 - c if fc else c)


def _all_gather(shards, name):
    n = len(shards)

    def body(*refs):
        ins, outs = refs[:n], refs[n:2 * n]
        send_sems, recv_sems, local_sems = refs[2 * n:]
        x, y, c = lax.axis_index("x"), lax.axis_index("y"), lax.axis_index("c")
        me = 4 * x + 2 * y + c
        local = [pltpu.make_async_copy(ins[j], outs[j].at[me], local_sems.at[j]) for j in range(n)]
        for cp in local:
            cp.start()
        sends, recvs = [], []
        for k, flip in enumerate(FLIPS):
            px, py, pc = _peer(x, y, c, flip)
            peer = 4 * px + 2 * py + pc
            for j in range(n):
                sem = k * n + j
                sends.append(pltpu.make_async_remote_copy(
                    src_ref=ins[j], dst_ref=outs[j].at[me], send_sem=send_sems.at[sem], recv_sem=recv_sems.at[sem],
                    device_id=(px, py, pc), device_id_type=MESH))
                recvs.append(pltpu.make_async_remote_copy(
                    src_ref=ins[j], dst_ref=outs[j].at[peer], send_sem=send_sems.at[sem], recv_sem=recv_sems.at[sem],
                    device_id=(px, py, pc), device_id_type=MESH))
        for cp in sends:
            cp.start()
        for cp in recvs:
            cp.wait_recv()
        for cp in sends:
            cp.wait_send()
        for cp in local:
            cp.wait()

    return pl.pallas_call(
        body, name=name,
        out_shape=[jax.ShapeDtypeStruct((N_DEV,) + s.shape, s.dtype) for s in shards],
        in_specs=[_hbm()] * n, out_specs=[_hbm()] * n,
        scratch_shapes=[pltpu.SemaphoreType.DMA((7 * n,)), pltpu.SemaphoreType.DMA((7 * n,)),
                        pltpu.SemaphoreType.DMA((n,))],
    )(*shards)


def _place_own(src, n_slots_shape, me, name):
    rows, width = src.shape
    tr = next(t for t in (256, 128, 64, 32, 16, SUBLANE) if rows % t == 0)

    def body(me_ref, src_ref, out_ref):
        out_ref[...] = src_ref[...]

    return pl.pallas_call(
        body, name=name,
        grid_spec=pltpu.PrefetchScalarGridSpec(
            num_scalar_prefetch=1, grid=(rows // tr,),
            in_specs=[pl.BlockSpec((tr, width), lambda i, me_ref: (i, 0))],
            out_specs=pl.BlockSpec((None, tr, width), lambda i, me_ref: (me_ref[0], i, 0))),
        out_shape=pltpu.HBM(n_slots_shape, src.dtype),
    )(me, src)


def _place_cast(pieces, me, name):
    n = len(pieces)
    counts = [a.shape[1] if t else a.shape[0] for a, t in pieces]
    width = pieces[0][0].shape[0] if pieces[0][1] else pieces[0][0].shape[1]
    total = sum(counts)

    def body(me_ref, *refs):
        ins, own_ref, land_ref = refs[:n], refs[n], refs[n + 1]
        first = 0
        for (a, transpose), in_ref, count in zip(pieces, ins, counts):
            block = (in_ref[...].T if transpose else in_ref[...]).astype(BF16)
            own_ref[first:first + count, :] = block
            land_ref[first:first + count, :] = block
            first += count

    return pl.pallas_call(
        body, name=name,
        grid_spec=pltpu.PrefetchScalarGridSpec(
            num_scalar_prefetch=1, grid=(1,),
            in_specs=[pl.BlockSpec(a.shape, lambda i, me_ref: (0, 0)) for a, _ in pieces],
            out_specs=[pl.BlockSpec((total, width), lambda i, me_ref: (0, 0)),
                       pl.BlockSpec((None, total, width), lambda i, me_ref: (me_ref[0], 0, 0))]),
        out_shape=[pltpu.HBM((total, width), BF16), pltpu.HBM((N_DEV, total, width), BF16)],
        compiler_params=_params(32),
    )(me, *[_in_hbm(a) for a, _ in pieces])


def _all_gather_two_level(shards, placed, name):
    n = len(shards)
    given = [j for j in range(n) if placed[j] is not None]

    def body(*refs):
        ins, outs = refs[:n], refs[n + len(given):2 * n + len(given)]
        send_sems, recv_sems, local_sems = refs[2 * n + len(given):]
        x, y, c = lax.axis_index("x"), lax.axis_index("y"), lax.axis_index("c")
        me, sibling = (x, y, c), (x, y, 1 - c)
        chips = [(1 - x, y), (x, 1 - y), (1 - x, 1 - y)]

        def slot(j, dev):
            return outs[j].at[4 * dev[0] + 2 * dev[1] + dev[2]]

        def copy(k, j, block, to, src=None):
            return pltpu.make_async_remote_copy(
                src_ref=slot(j, block) if src is None else src, dst_ref=slot(j, block),
                send_sem=send_sems.at[k * n + j], recv_sem=recv_sems.at[k * n + j], device_id=to, device_id_type=MESH)

        local = [pltpu.make_async_copy(ins[j], slot(j, me), local_sems.at[j]) for j in range(n) if j not in given]
        for cp in local:
            cp.start()
        first = [copy(0, j, me, sibling, src=ins[j]) for j in range(n)]
        first += [copy(1 + t, j, me, (*chip, c), src=ins[j]) for t, chip in enumerate(chips) for j in range(n)]
        for cp in first:
            cp.start()
        passed = []
        for t, chip in enumerate(chips):
            for j in range(n):
                copy(1 + t, j, (*chip, c), me).wait_recv()
                passed.append(copy(4 + t, j, (*chip, c), sibling))
                passed[-1].start()
        for j in range(n):
            copy(0, j, sibling, me).wait_recv()
        for t, chip in enumerate(chips):
            for j in range(n):
                copy(4 + t, j, (*chip, 1 - c), me).wait_recv()
        for cp in first + passed:
            cp.wait_send()
        for cp in local:
            cp.wait()

    return pl.pallas_call(
        body, name=name,
        out_shape=[jax.ShapeDtypeStruct((N_DEV,) + s.shape, s.dtype) for s in shards],
        in_specs=[_hbm()] * (n + len(given)), out_specs=[_hbm()] * n,
        input_output_aliases={n + i: j for i, j in enumerate(given)},
        scratch_shapes=[pltpu.SemaphoreType.DMA((7 * n,)), pltpu.SemaphoreType.DMA((7 * n,)),
                        pltpu.SemaphoreType.DMA((n,))],
    )(*shards, *[placed[j] for j in given])


class _Part(NamedTuple):
    src: jax.Array
    scatter: bool
    rows: int
    land: int
    off: int


class _Started(NamedTuple):
    send_sems: jax.Array
    recv_sems: jax.Array
    thru: tuple
    token: jax.Array
    parts: tuple


def _exchange_copies(srcs, lands, send_sems, recv_sems, parts):
    n = len(parts)
    x, y, c = lax.axis_index("x"), lax.axis_index("y"), lax.axis_index("c")
    me = 4 * x + 2 * y + c

    def block(j, dev):
        p = parts[j]
        return srcs[j].at[pl.ds(pl.multiple_of(dev * p.rows, SUBLANE), p.rows), :] if p.scatter else srcs[j]

    def slot(j, index):
        p = parts[j]
        return lands[p.land].at[index, pl.ds(p.off, p.rows), :]

    sends, recvs = [], []
    for k, flip in enumerate(FLIPS):
        px, py, pc = _peer(x, y, c, flip)
        peer = 4 * px + 2 * py + pc
        for j in range(n):
            sems = dict(send_sem=send_sems.at[k * n + j], recv_sem=recv_sems.at[k * n + j],
                        device_id=(px, py, pc), device_id_type=MESH)
            to, got = (k, k) if parts[j].scatter else (me, peer)
            sends.append(pltpu.make_async_remote_copy(src_ref=block(j, peer), dst_ref=slot(j, to), **sems))
            recvs.append(pltpu.make_async_remote_copy(src_ref=block(j, peer), dst_ref=slot(j, got), **sems))
    return sends, recvs


def _exchange_start(parts, lands, name, after=None):
    n, nl = len(parts), len(lands)
    n_in = n + nl + (after is not None)

    def body(*refs):
        srcs, land_refs = refs[:n], refs[n:n + nl]
        send_sems, recv_sems = refs[n_in], refs[n_in + 1]
        token = refs[n_in + 2 + n + nl]
        sends, _ = _exchange_copies(srcs, land_refs, send_sems, recv_sems, parts)
        for cp in sends:
            cp.start()
        token[...] = jnp.zeros_like(token)

    hbm = pl.BlockSpec(memory_space=pltpu.HBM)
    sem = pl.BlockSpec(memory_space=pltpu.SEMAPHORE)
    fresh = lambda s: lax.empty(s.shape, s.dtype) if isinstance(s, jax.ShapeDtypeStruct) else s
    args = [pltpu.with_memory_space_constraint(p.src, pltpu.HBM) for p in parts]
    args += [pltpu.with_memory_space_constraint(fresh(s), pltpu.HBM) for s in lands]
    args += [] if after is None else [after]
    out = pl.pallas_call(
        body, name=name,
        out_shape=(pltpu.SemaphoreType.DMA((7 * n,)), pltpu.SemaphoreType.DMA((7 * n,)),
                   *[pltpu.HBM(a.shape, a.dtype) for a in args[:n + nl]], jax.ShapeDtypeStruct((SUBLANE, LANE), F32)),
        in_specs=[hbm] * (n + nl) + [_hbm()] * (after is not None),
        out_specs=(sem, sem, *[hbm] * (n + nl), _whole()),
        input_output_aliases={j: 2 + j for j in range(n + nl)},
        compiler_params=pltpu.CompilerParams(has_side_effects=pltpu.SideEffectType.DATAFLOW_SIDE_EFFECTING),
    )(*args)
    return _Started(out[0], out[1], tuple(out[2:2 + n + nl]), out[2 + n + nl], tuple(parts))


def _exchange_wait(started, name, after):
    parts = started.parts
    n, nl = len(parts), len(started.thru) - len(parts)

    def body(*refs):
        srcs, land_refs = refs[:n], refs[n:n + nl]
        send_sems, recv_sems = refs[n + nl], refs[n + nl + 1]
        sends, recvs = _exchange_copies(srcs, land_refs, send_sems, recv_sems, parts)
        for cp in sends:
            cp.wait_send()
        for cp in recvs:
            cp.wait_recv()

    hbm = pl.BlockSpec(memory_space=pltpu.HBM)
    sem = pl.BlockSpec(memory_space=pltpu.SEMAPHORE)
    out = pl.pallas_call(
        body, name=name,
        out_shape=tuple(pltpu.HBM(a.shape, a.dtype) for a in started.thru),
        in_specs=[hbm] * (n + nl) + [sem, sem, _hbm()], out_specs=tuple([hbm] * (n + nl)),
        input_output_aliases={j: j for j in range(n + nl)},
        compiler_params=pltpu.CompilerParams(has_side_effects=pltpu.SideEffectType.DATAFLOW_SIDE_EFFECTING),
    )(*started.thru, started.send_sems, started.recv_sems, after)
    return list(out[:n]), list(out[n:])


def _fwd_in(x, g1, b_in, slab, ts, deps):
    n_tok, dm = x.shape
    rows, off, _ = _layout(dm)
    width = 7 * dm

    def body(x_ref, g1_ref, b_ref, slab_ref, proj_ref, p_ref, u_ref, h_ref, w_v, sems):
        copies = _weight_copies(slab_ref, off["win"], rows["win"], w_v, sems, 0)
        _on_first_step(copies, "start")
        _on_first_step(copies, "wait")
        xv = x_ref[...]
        h = (xv * _rms(xv) * g1_ref[...]).astype(BF16)
        h_ref[...] = h
        cols = []
        for j in range(7):
            pj = _dot_nt(h, w_v[pl.ds(j * dm, dm), :]) + b_ref[:, j * dm:(j + 1) * dm]
            proj_ref[:, j * dm:(j + 1) * dm] = pj.astype(proj_ref.dtype)
            if 1 <= j <= 4:
                cols.append(pj)
            if j == 2:
                p_ref[...] = cols[0] * cols[1]
            if j == 4:
                u_ref[...] = cols[2] * _sigmoid(cols[3])

    return pl.pallas_call(
        _after(body, deps), name="fwd_in", grid=(n_tok // ts,),
        in_specs=[_whole()] * len(deps) + [_rows(ts, dm), _whole(), _whole(), _hbm()],
        out_specs=[_rows(ts, width), _rows(ts, dm), _rows(ts, dm), _rows(ts, dm)],
        out_shape=[jax.ShapeDtypeStruct((n_tok, width), BF16), jax.ShapeDtypeStruct((n_tok, dm), F32),
                   jax.ShapeDtypeStruct((n_tok, dm), F32), jax.ShapeDtypeStruct((n_tok, dm), BF16)],
        scratch_shapes=[pltpu.VMEM((width, dm), BF16), pltpu.SemaphoreType.DMA((N_DEV,))],
        compiler_params=_params(56),
    )(*deps, x, g1, b_in, slab)


def _fwd_mix(p, u, proj, x, caw, cab, cbw, cbb, lng, lnb, g1post, slab, ts):
    n_tok, dm = x.shape
    rows, off, _ = _layout(dm)
    n_steps = n_tok // ts

    def body(p_ref, p_prev, p_next, u_ref, u_prev, u_next, bg_ref, za_ref, zb_ref, x_ref,
             caw_ref, cab_ref, cbw_ref, cbb_ref, lng_ref, lnb_ref, g1p_ref, slab_ref,
             va_ref, vb_ref, ya_ref, yb_ref, qa_ref, sb_ref, mg_ref, mix_ref, x1_ref,
             wa_v, wb_v, wo_v, tap_a, tap_b, sems, rolled_a, rolled_b):
        i = pl.program_id(0)
        copies = (_weight_copies(slab_ref, off["wa"], rows["wa"], wa_v, sems, 0)
                  + _weight_copies(slab_ref, off["wb"], rows["wb"], wb_v, sems, N_DEV)
                  + _weight_copies(slab_ref, off["wo"], rows["wo"], wo_v, sems, 2 * N_DEV))
        _on_first_step(copies, "start")

        @pl.when(i == 0)
        def _():
            _broadcast_taps(caw_ref, tap_a, CONV_A)
            _broadcast_taps(cbw_ref, tap_b, CONV_B)

        def emit_a(r0, lanes, acc):
            va_ref[pl.ds(r0, acc.shape[0]), lanes] = acc + cab_ref[:, lanes]

        def emit_b(r0, lanes, acc):
            vb_ref[pl.ds(r0, acc.shape[0]), lanes] = acc + cbb_ref[:, lanes]

        _conv_tile(_with_halos(p_ref, p_prev, p_next, i, n_steps), tap_a, _fwd_starts(CONV_A), ts, dm, emit_a,
                   rolled_a)
        _conv_tile(_with_halos(u_ref, u_prev, u_next, i, n_steps), tap_b, _fwd_starts(CONV_B), ts, dm, emit_b,
                   rolled_b)
        _on_first_step(copies, "wait")

        qa = (bg_ref[...].astype(F32) * va_ref[...]).astype(BF16)
        qa_ref[...] = qa
        ya = _dot(qa, wa_v[...])
        vb = vb_ref[...]
        xc = vb - jnp.mean(vb, axis=-1, keepdims=True)
        rstd = lax.rsqrt(jnp.mean(xc * xc, axis=-1, keepdims=True) + LN_EPS)
        ln = xc * rstd * lng_ref[...] + lnb_ref[...]
        sb = (ln * _sigmoid(ln)).astype(BF16)
        sb_ref[...] = sb
        yb = _dot(sb, wb_v[...])
        ya_ref[...] = ya.astype(BF16)
        yb_ref[...] = yb.astype(BF16)
        merged = (_sigmoid(za_ref[...].astype(F32)) * ya + _sigmoid(zb_ref[...].astype(F32)) * yb).astype(BF16)
        mg_ref[...] = merged
        mix = _dot(merged, wo_v[...])
        mix_ref[...] = mix
        x1_ref[...] = x_ref[...] + mix * _rms(mix) * g1p_ref[...]

    tok = lambda dt: jax.ShapeDtypeStruct((n_tok, dm), dt)
    return pl.pallas_call(
        body, name="fwd_mix", grid=(n_steps,),
        in_specs=(_halo_specs(ts, dm, n_tok) + _halo_specs(ts, dm, n_tok)
                  + [_rows(ts, dm, 0), _rows(ts, dm, 5), _rows(ts, dm, 6), _rows(ts, dm)]
                  + [_whole()] * 7 + [_hbm()]),
        out_specs=[_rows(ts, dm)] * 9,
        out_shape=[tok(F32), tok(F32), tok(BF16), tok(BF16), tok(BF16), tok(BF16), tok(BF16), tok(F32), tok(F32)],
        scratch_shapes=[pltpu.VMEM((dm, dm), BF16), pltpu.VMEM((dm, dm), BF16), pltpu.VMEM((dm, dm), BF16),
                        pltpu.VMEM((CONV_A, SUBLANE, dm), F32), pltpu.VMEM((CONV_B, SUBLANE, dm), F32),
                        pltpu.SemaphoreType.DMA((3 * N_DEV,)),
                        pltpu.VMEM((2, SUBLANE, ts + 2 * HALO, LANE), F32),
                        pltpu.VMEM((2, SUBLANE, ts + 2 * HALO, LANE), F32)],
        compiler_params=_params(48),
    )(p, p, p, u, u, u, proj, proj, proj, x, caw, cab, cbw, cbb, lng, lnb, g1post, slab)


def _mlp_fwd_bwd(x1, mix, tgt, g1post, g2pre, g2post, slab, ts):
    n_tok, dm = x1.shape
    rows, off, _ = _layout(dm)
    ff = 4 * dm

    def body(x1_ref, mix_ref, t_ref, g1p_ref, g2pre_ref, g2post_ref, slab_ref,
             f_ref, df1_ref, h2_ref, df2_ref, dmix_ref, dx1_ref, small_ref, w1_v, w2_v, relu_v, sems):
        w1_copies = _weight_copies(slab_ref, off["w1"], rows["w1"], w1_v, sems, 0)
        w2_copies = _weight_copies(slab_ref, off["w2"], rows["w2"], w2_v, sems, N_DEV)
        _on_first_step(w1_copies + w2_copies, "start")

        @pl.when(pl.program_id(0) == 0)
        def _():
            small_ref[...] = jnp.zeros_like(small_ref)

        _on_first_step(w1_copies + w2_copies, "wait")
        x1v = x1_ref[...]
        r3 = _rms(x1v)
        g2pre = g2pre_ref[...]
        h2 = (x1v * r3 * g2pre).astype(BF16)
        h2_ref[...] = h2
        for c in range(4):
            blk = pl.ds(c * dm, dm)
            relu = jnp.maximum(_dot_nt(h2, w1_v[blk, :]), 0.0)
            relu_v[:, c * dm:(c + 1) * dm] = relu
            f_ref[:, c * dm:(c + 1) * dm] = (relu * relu).astype(BF16)
        f2 = _dot(f_ref[...], w2_v[...])
        r4 = _rms(f2)
        g2post = g2post_ref[...]
        err = x1v + f2 * r4 * g2post - t_ref[...]
        dy = err * (1.0 / dm)
        small_ref[3:4, :] += _colsum(err * err)
        small_ref[0:1, :] += _colsum(dy * f2 * r4)
        df2 = _rms_bwd(dy, f2, r4, g2post).astype(BF16)
        df2_ref[...] = df2
        for c in range(4):
            blk = pl.ds(c * dm, dm)
            df1 = (_dot_nt(df2, w2_v[blk, :]) * (2.0 * relu_v[:, c * dm:(c + 1) * dm])).astype(BF16)
            df1_ref[:, c * dm:(c + 1) * dm] = df1
        dh2 = _dot(df1_ref[...], w1_v[...])
        small_ref[1:2, :] += _colsum(dh2 * x1v * r3)
        dx1 = dy + _rms_bwd(dh2, x1v, r3, g2pre)
        dx1_ref[...] = dx1
        mixv = mix_ref[...]
        r2 = _rms(mixv)
        small_ref[2:3, :] += _colsum(dx1 * mixv * r2)
        dmix_ref[...] = _rms_bwd(dx1, mixv, r2, g1p_ref[...]).astype(BF16)

    tok = lambda w, dt: jax.ShapeDtypeStruct((n_tok, w), dt)
    return pl.pallas_call(
        body, name="mlp_fwd_bwd", grid=(n_tok // ts,),
        in_specs=[_rows(ts, dm)] * 3 + [_whole()] * 3 + [_hbm()],
        out_specs=[_rows(ts, ff), _rows(ts, ff), _rows(ts, dm), _rows(ts, dm), _rows(ts, dm), _rows(ts, dm),
                   pl.BlockSpec((SUBLANE, dm), lambda i: (0, 0))],
        out_shape=[tok(ff, BF16), tok(ff, BF16), tok(dm, BF16), tok(dm, BF16), tok(dm, BF16), tok(dm, F32),
                   jax.ShapeDtypeStruct((SUBLANE, dm), F32)],
        scratch_shapes=[pltpu.VMEM((ff, dm), BF16), pltpu.VMEM((ff, dm), BF16), pltpu.VMEM((ts, ff), F32),
                        pltpu.SemaphoreType.DMA((2 * N_DEV,))],
        compiler_params=_params(56),
    )(x1, mix, tgt, g1post, g2pre, g2post, slab)


def _bwd_mix(dmix, ya, yb, proj, va, vb, lng, lnb, slab, ts, deps):
    n_tok, dm = dmix.shape
    rows, off, _ = _layout(dm)
    n_steps = n_tok // ts

    def body(dmix_ref, ya_ref, yb_ref, bg_ref, za_ref, zb_ref, va_ref, vb_ref, lng_ref, lnb_ref, slab_ref,
             dpa_ref, dya_ref, dyb_ref, dva_ref, dvb_ref, small_ref, wa_v, wb_v, wo_v, sems):
        wo_copies = _weight_copies(slab_ref, off["wo"], rows["wo"], wo_v, sems, 2 * N_DEV)
        ab_copies = (_weight_copies(slab_ref, off["wa"], rows["wa"], wa_v, sems, 0)
                     + _weight_copies(slab_ref, off["wb"], rows["wb"], wb_v, sems, N_DEV))
        _on_first_step(wo_copies + ab_copies, "start")

        @pl.when(pl.program_id(0) == 0)
        def _():
            small_ref[...] = jnp.zeros_like(small_ref)

        _on_first_step(wo_copies, "wait")
        dmerged = _dot_nt(dmix_ref[...], wo_v[...])
        _on_first_step(ab_copies, "wait")
        sa = _sigmoid(za_ref[...].astype(F32))
        sg = _sigmoid(zb_ref[...].astype(F32))
        dza = dmerged * ya_ref[...].astype(F32) * sa * (1.0 - sa)
        dzb = dmerged * yb_ref[...].astype(F32) * sg * (1.0 - sg)
        dpa_ref[:, dm:2 * dm] = dza.astype(BF16)
        dpa_ref[:, 2 * dm:3 * dm] = dzb.astype(BF16)
        small_ref[5:6, :] += _colsum(dza)
        small_ref[6:7, :] += _colsum(dzb)

        dya = (dmerged * sa).astype(BF16)
        dya_ref[...] = dya
        dqa = _dot_nt(dya, wa_v[...])
        dbg = dqa * va_ref[...]
        dpa_ref[:, 0:dm] = dbg.astype(BF16)
        small_ref[4:5, :] += _colsum(dbg)
        dva = dqa * bg_ref[...].astype(F32)
        dva_ref[...] = dva
        small_ref[2:3, :] += _colsum(dva)

        dyb = (dmerged * sg).astype(BF16)
        dyb_ref[...] = dyb
        dsb = _dot_nt(dyb, wb_v[...])
        vb = vb_ref[...]
        xc = vb - jnp.mean(vb, axis=-1, keepdims=True)
        rstd = lax.rsqrt(jnp.mean(xc * xc, axis=-1, keepdims=True) + LN_EPS)
        nrm = xc * rstd
        lng_v = lng_ref[...]
        ln = nrm * lng_v + lnb_ref[...]
        sl = _sigmoid(ln)
        dln = dsb * (sl * (1.0 + ln * (1.0 - sl)))
        small_ref[0:1, :] += _colsum(dln * nrm)
        small_ref[1:2, :] += _colsum(dln)
        dn = dln * lng_v
        dvb = rstd * (dn - jnp.mean(dn, axis=-1, keepdims=True)
                      - nrm * jnp.mean(dn * nrm, axis=-1, keepdims=True))
        dvb_ref[...] = dvb
        small_ref[3:4, :] += _colsum(dvb)

    tok = lambda w, dt: jax.ShapeDtypeStruct((n_tok, w), dt)
    return pl.pallas_call(
        _after(body, deps), name="bwd_mix", grid=(n_steps,),
        in_specs=([_whole()] * len(deps) + [_rows(ts, dm)] * 3
                  + [_rows(ts, dm, 0), _rows(ts, dm, 5), _rows(ts, dm, 6)]
                  + [_rows(ts, dm)] * 2 + [_whole()] * 2 + [_hbm()]),
        out_specs=[_rows(ts, 3 * dm), _rows(ts, dm), _rows(ts, dm), _rows(ts, dm), _rows(ts, dm),
                   pl.BlockSpec((SUBLANE, dm), lambda i: (0, 0))],
        out_shape=[tok(3 * dm, BF16), tok(dm, BF16), tok(dm, BF16), tok(dm, F32), tok(dm, F32),
                   jax.ShapeDtypeStruct((SUBLANE, dm), F32)],
        scratch_shapes=[pltpu.VMEM((dm, dm), BF16), pltpu.VMEM((dm, dm), BF16), pltpu.VMEM((dm, dm), BF16),
                        pltpu.SemaphoreType.DMA((3 * N_DEV,))],
        compiler_params=_params(56),
    )(*deps, dmix, ya, yb, proj, proj, proj, va, vb, lng, lnb, slab)


def _bwd_conv(dva, dvb, p, u, proj, caw, cbw, ts, deps):
    n_tok, dm = dva.shape
    n_steps = n_tok // ts
    small_rows = 40

    def body(dva_ref, dva_prev, dva_next, dvb_ref, dvb_prev, dvb_next, p_ref, u_ref,
             cg_ref, ha_ref, a_ref, g_ref, caw_ref, cbw_ref,
             dproj_ref, small_ref,
             dp_v, du_v, tap_a, tap_b, gwa_v, gwb_v):
        i = pl.program_id(0)

        @pl.when(i == 0)
        def _():
            _broadcast_taps(caw_ref, tap_a, CONV_A)
            _broadcast_taps(cbw_ref, tap_b, CONV_B)
            small_ref[...] = jnp.zeros_like(small_ref)
            gwa_v[...] = jnp.zeros_like(gwa_v)
            gwb_v[...] = jnp.zeros_like(gwb_v)

        def emit_dp(r0, lanes, acc):
            dp_v[pl.ds(r0, acc.shape[0]), lanes] = acc

        def emit_du(r0, lanes, acc):
            du_v[pl.ds(r0, acc.shape[0]), lanes] = acc

        _conv_bwd_tile(_with_halos(dva_ref, dva_prev, dva_next, i, n_steps), p_ref, tap_a, gwa_v, CONV_A, ts, dm,
                       emit_dp)
        _conv_bwd_tile(_with_halos(dvb_ref, dvb_prev, dvb_next, i, n_steps), u_ref, tap_b, gwb_v, CONV_B, ts, dm,
                       emit_du)

        dp = dp_v[...]
        dcg = dp * ha_ref[...].astype(F32)
        dha = dp * cg_ref[...].astype(F32)
        du = du_v[...]
        sg = _sigmoid(g_ref[...].astype(F32))
        da = du * sg
        dg = du * a_ref[...].astype(F32) * sg * (1.0 - sg)
        dproj_ref[:, 0:dm] = dcg.astype(BF16)
        dproj_ref[:, dm:2 * dm] = dha.astype(BF16)
        dproj_ref[:, 2 * dm:3 * dm] = da.astype(BF16)
        dproj_ref[:, 3 * dm:4 * dm] = dg.astype(BF16)
        small_ref[3:4, :] += _colsum(dcg)
        small_ref[4:5, :] += _colsum(dha)
        small_ref[5:6, :] += _colsum(da)
        small_ref[6:7, :] += _colsum(dg)

        @pl.when(i == n_steps - 1)
        def _():
            for k in range(CONV_A):
                small_ref[k:k + 1, :] = _colsum(gwa_v[k])
            for k in range(CONV_B):
                small_ref[SUBLANE + k:SUBLANE + k + 1, :] = _colsum(gwb_v[k])

    return pl.pallas_call(
        _after(body, deps), name="bwd_conv", grid=(n_steps,),
        in_specs=([_whole()] * len(deps) + _halo_specs(ts, dm, n_tok) * 2 + [_rows(ts, dm)] * 2
                  + [_rows(ts, dm, 1), _rows(ts, dm, 2), _rows(ts, dm, 3), _rows(ts, dm, 4)]
                  + [_whole()] * 2),
        out_specs=[_rows(ts, 4 * dm), pl.BlockSpec((small_rows, dm), lambda i: (0, 0))],
        out_shape=[jax.ShapeDtypeStruct((n_tok, 4 * dm), BF16), jax.ShapeDtypeStruct((small_rows, dm), F32)],
        scratch_shapes=[pltpu.VMEM((ts, dm), F32), pltpu.VMEM((ts, dm), F32),
                        pltpu.VMEM((CONV_A, SUBLANE, dm), F32), pltpu.VMEM((CONV_B, SUBLANE, dm), F32),
                        pltpu.VMEM((CONV_A, SUBLANE, dm), F32), pltpu.VMEM((CONV_B, SUBLANE, dm), F32)],
        compiler_params=_params(48),
    )(*deps, dva, dva, dva, dvb, dvb, dvb, p, u, proj, proj, proj, proj, caw, cbw)


def _bwd_in(dpa, dpb, x, dx1, g1, slab, ts, deps):
    n_tok, dm = x.shape
    rows, off, _ = _layout(dm)
    width = 7 * dm

    def body(dpa_ref, dpb_ref, x_ref, dx1_ref, g1_ref, slab_ref, gx_ref, small_ref, w_v, sems):
        copies = _weight_copies(slab_ref, off["win"], rows["win"], w_v, sems, 0)
        _on_first_step(copies, "start")

        @pl.when(pl.program_id(0) == 0)
        def _():
            small_ref[...] = jnp.zeros_like(small_ref)

        _on_first_step(copies, "wait")
        dh = (_dot(dpa_ref[:, 0:dm], w_v[0:dm, :]) + _dot(dpb_ref[...], w_v[dm:5 * dm, :])
              + _dot(dpa_ref[:, dm:3 * dm], w_v[5 * dm:7 * dm, :]))
        xv = x_ref[...]
        r1 = _rms(xv)
        small_ref[0:1, :] += _colsum(dh * xv * r1)
        gx_ref[...] = dx1_ref[...] + _rms_bwd(dh, xv, r1, g1_ref[...])

    return pl.pallas_call(
        _after(body, deps), name="bwd_in", grid=(n_tok // ts,),
        in_specs=[_whole()] * len(deps) + [_rows(ts, 3 * dm), _rows(ts, 4 * dm), _rows(ts, dm), _rows(ts, dm),
                                           _whole(), _hbm()],
        out_specs=[_rows(ts, dm), pl.BlockSpec((SUBLANE, dm), lambda i: (0, 0))],
        out_shape=[jax.ShapeDtypeStruct((n_tok, dm), F32), jax.ShapeDtypeStruct((SUBLANE, dm), F32)],
        scratch_shapes=[pltpu.VMEM((width, dm), BF16), pltpu.SemaphoreType.DMA((N_DEV,))],
        compiler_params=_params(56),
    )(*deps, dpa, dpb, x, dx1, g1, slab)


def _wgrad(a, b, name, tm, tk, out_dtype):
    n_tok, m = a.shape
    n = b.shape[1]
    k_steps = n_tok // tk

    def body(a_ref, b_ref, o_ref, acc_v):
        k = pl.program_id(1)

        @pl.when(k == 0)
        def _():
            acc_v[...] = jnp.zeros_like(acc_v)

        acc_v[...] += _dot_tn(a_ref[...], b_ref[...])

        @pl.when(k == k_steps - 1)
        def _():
            o_ref[...] = acc_v[...].astype(o_ref.dtype)

    return pl.pallas_call(
        body, name=name, grid=(m // tm, k_steps),
        in_specs=[pl.BlockSpec((tk, tm), lambda i, k: (k, i)), pl.BlockSpec((tk, n), lambda i, k: (k, 0))],
        out_specs=pl.BlockSpec((tm, n), lambda i, k: (i, 0)),
        out_shape=pltpu.HBM((m, n), out_dtype),
        scratch_shapes=[pltpu.VMEM((tm, n), F32)],
        compiler_params=pltpu.CompilerParams(dimension_semantics=("arbitrary", "arbitrary"),
                                             vmem_limit_bytes=40 * MIB),
    )(a, b)


def _wgrad_in(dpa, dpb, h, tk):
    n_tok, dm = h.shape
    k_steps = n_tok // tk
    last = k_steps - 1

    def from_a(i):
        return (i == 0) | (i >= 5)

    def body(a_ref, b_ref, h_ref, o_ref, acc_v):
        i, k = pl.program_id(0), pl.program_id(1)

        @pl.when(k == 0)
        def _():
            acc_v[...] = jnp.zeros_like(acc_v)

        @pl.when(from_a(i))
        def _():
            acc_v[...] += _dot_tn(a_ref[...], h_ref[...])

        @pl.when(jnp.logical_not(from_a(i)))
        def _():
            acc_v[...] += _dot_tn(b_ref[...], h_ref[...])

        @pl.when(k == last)
        def _():
            o_ref[...] = acc_v[...].astype(o_ref.dtype)

    a_index = lambda i, k: (jnp.where(from_a(i), k, last), jnp.where(i >= 5, i - 4, 0))
    b_index = lambda i, k: (jnp.where(from_a(i), jnp.where(i == 0, 0, last), k), jnp.clip(i - 1, 0, 3))
    return pl.pallas_call(
        body, name="wgrad_in", grid=(7, k_steps),
        in_specs=[pl.BlockSpec((tk, dm), a_index), pl.BlockSpec((tk, dm), b_index),
                  pl.BlockSpec((tk, dm), lambda i, k: (k, 0))],
        out_specs=pl.BlockSpec((dm, dm), lambda i, k: (i, 0)),
        out_shape=pltpu.HBM((7 * dm, dm), BF16),
        scratch_shapes=[pltpu.VMEM((dm, dm), F32)],
        compiler_params=pltpu.CompilerParams(dimension_semantics=("arbitrary", "arbitrary"),
                                             vmem_limit_bytes=48 * MIB),
    )(dpa, dpb, h)


def _adamw(w, g, m, v):
    m = ADAM_B1 * m + (1.0 - ADAM_B1) * g
    v = ADAM_B2 * v + (1.0 - ADAM_B2) * (g * g)
    m_hat = m / (1.0 - ADAM_B1 ** ADAM_STEP)
    v_hat = v / (1.0 - ADAM_B2 ** ADAM_STEP)
    delta = -ADAM_LR * (m_hat / (jnp.sqrt(v_hat) + ADAM_EPS) + ADAM_WD * w)
    return delta, m, v


def _adam_big(recv, part, me, off, rows, w, m, v, transpose, name, tr):
    dm = recv.shape[2]
    per = rows // tr

    def body(me_ref, own_ref, r_ref, w_ref, m_ref, v_ref, g_ref, d_ref, mo_ref, vo_ref):
        g = own_ref[...].astype(F32)
        for k in range(len(FLIPS)):
            g = g + r_ref[k].astype(F32)
        if transpose:
            g = g.T
        delta, m_new, v_new = _adamw(w_ref[...], g, m_ref[...], v_ref[...])
        g_ref[...] = g
        d_ref[...] = delta
        mo_ref[...] = m_new
        vo_ref[...] = v_new

    if transpose:
        blk = pl.BlockSpec((dm, tr), lambda i, me_ref: (0, i))
    else:
        blk = pl.BlockSpec((tr, dm), lambda i, me_ref: (i, 0))
    first = off // tr
    return pl.pallas_call(
        body, name=name,
        grid_spec=pltpu.PrefetchScalarGridSpec(
            num_scalar_prefetch=1, grid=(per,),
            in_specs=[pl.BlockSpec((tr, dm), lambda i, me_ref: (me_ref[0] * per + i, 0)),
                      pl.BlockSpec((len(FLIPS), tr, dm), lambda i, me_ref: (0, first + i, 0)), blk, blk, blk],
            out_specs=[blk] * 4),
        out_shape=[jax.ShapeDtypeStruct(w.shape, F32)] * 4,
        compiler_params=_params(32),
    )(me, *[_in_hbm(a) for a in (part, recv, w, m, v)])


LOSS_ROW = 15
CONV_A_ROW = 16
CONV_B_ROW = 24


def _adam_small(recv_small, recv_last, me, params, d_model):
    n = len(params)
    cw = d_model // N_DEV

    def body(me_ref, r_ref, rc_ref, l_ref, *refs):
        ins, loss_ref, outs = refs[:3 * n], refs[3 * n], refs[3 * n + 1:3 * n + 1 + 4 * n]
        g_v, gc_v, last_v = refs[3 * n + 1 + 4 * n:]
        g, gc, last = r_ref[0], rc_ref[0], l_ref[0]
        for d in range(1, N_DEV):
            g, gc, last = g + r_ref[d], gc + rc_ref[d], last + l_ref[d]
        g_v[...], gc_v[...], last_v[...] = g, gc, last
        loss_ref[...] = (0.5 / d_model) * jnp.sum(g_v[LOSS_ROW:LOSS_ROW + 1, :], axis=-1, keepdims=True)
        for j, (row0, own_columns, (w, _, _)) in enumerate(params):
            w_ref, m_ref, v_ref = ins[3 * j:3 * j + 3]
            source = gc_v if own_columns else (last_v if row0 == 0 else g_v)
            grad = source[row0:row0 + w.shape[0], :]
            delta, m_new, v_new = _adamw(w_ref[...], grad, m_ref[...], v_ref[...])
            for ref, val in zip(outs[4 * j:4 * j + 4], (grad, delta, m_new, v_new)):
                ref[...] = val

    full = lambda shape: pl.BlockSpec(shape, lambda i, me_ref: (0,) * len(shape))
    stack_rows = recv_small.shape[1]
    flat = [a for _, _, triple in params for a in triple]
    shapes = [w.shape for _, _, (w, _, _) in params for _ in range(4)]
    out = pl.pallas_call(
        body, name="adam_small",
        grid_spec=pltpu.PrefetchScalarGridSpec(
            num_scalar_prefetch=1, grid=(1,),
            in_specs=[full(recv_small.shape),
                      pl.BlockSpec((N_DEV, stack_rows, cw), lambda i, me_ref: (0, 0, me_ref[0])),
                      full(recv_last.shape)] + [full(a.shape) for a in flat],
            out_specs=[full((1, 1))] + [full(s) for s in shapes],
            scratch_shapes=[pltpu.VMEM((stack_rows, d_model), F32), pltpu.VMEM((stack_rows, cw), F32),
                            pltpu.VMEM(recv_last.shape[1:], F32)]),
        out_shape=[jax.ShapeDtypeStruct((1, 1), F32)] + [jax.ShapeDtypeStruct(s, F32) for s in shapes],
    )(me, *[_in_hbm(a) for a in (recv_small, recv_small, recv_last, *flat)])
    return out[0], [tuple(out[1 + 4 * j:5 + 4 * j]) for j in range(n)]


def _tile(n_tok, want):
    return min(want, n_tok)


def kernel(x, norm1_pre_g, w_in, b_in, conv_a_w, conv_a_b, w_a_out, conv_b_w, conv_b_b, ln_b_g, ln_b_b, w_b_out, w_o, norm1_post_g, norm2_pre_g, w_mlp_in, w_mlp_out, norm2_post_g, loss_target, m_norm1_pre_g, m_w_in, m_b_in, m_conv_a_w, m_conv_a_b, m_w_a_out, m_conv_b_w, m_conv_b_b, m_ln_b_g, m_ln_b_b, m_w_b_out, m_w_o, m_norm1_post_g, m_norm2_pre_g, m_w_mlp_in, m_w_mlp_out, m_norm2_post_g, v_norm1_pre_g, v_w_in, v_b_in, v_conv_a_w, v_conv_a_b, v_w_a_out, v_conv_b_w, v_conv_b_b, v_ln_b_g, v_ln_b_b, v_w_b_out, v_w_o, v_norm1_post_g, v_norm2_pre_g, v_w_mlp_in, v_w_mlp_out, v_norm2_post_g):
    n_tok, dm = x.shape[1], x.shape[2]
    rows, off, slab_rows = _layout(dm)
    cw = dm // N_DEV
    xs = x.reshape(n_tok, dm)
    tgt = loss_target.reshape(n_tok, dm)
    row = lambda vec: vec.reshape(1, -1)
    scattered = lambda group: jax.ShapeDtypeStruct((len(FLIPS), slab_rows[group], dm), BF16)
    tm, tk = min(dm, 1024), _tile(n_tok, 2048)
    me = (4 * lax.axis_index("x") + 2 * lax.axis_index("y") + lax.axis_index("c")).astype(jnp.int32).reshape(1)

    conv_own = jnp.concatenate([conv_a_w, jnp.zeros((SUBLANE - CONV_A, cw), F32), conv_b_w,
                                jnp.zeros((1, cw), F32)], axis=0)
    own_in, land_in = _place_cast([(w_in, True)], me, "place_w_in")
    slab_in, conv_all = _all_gather_two_level([own_in, conv_own], [land_in, None], "gather_w_in")
    conv_full = conv_all.transpose(1, 0, 2).reshape(conv_own.shape[0], dm)
    caw, cbw = conv_full[0:CONV_A], conv_full[SUBLANE:SUBLANE + CONV_B]
    own_abo, land_abo = _place_cast([(w_a_out, False), (w_b_out, False), (w_o, False)], me, "place_abo")
    own_mlp, land_mlp = _place_cast([(w_mlp_in, True), (w_mlp_out, False)], me, "place_mlp")
    ag_abo = _exchange_start([_Part(own_abo, False, slab_rows["abo"], 0, 0)], [land_abo],
                             "gather_abo_start", after=slab_in)
    ag_mlp = _exchange_start([_Part(own_mlp, False, slab_rows["mlp"], 0, 0)], [land_mlp],
                             "gather_mlp_start", after=ag_abo.token)

    proj, p, u, h = _fwd_in(xs, row(norm1_pre_g), row(b_in), slab_in, _tile(n_tok, 512),
                            [ag_abo.token, ag_mlp.token])
    _, (slab_abo,) = _exchange_wait(ag_abo, "gather_abo_wait", after=proj)
    va, vb, ya, yb, qa, sb, merged, mix, x1 = _fwd_mix(
        p, u, proj, xs, caw, row(conv_a_b), cbw, row(conv_b_b), row(ln_b_g), row(ln_b_b), row(norm1_post_g),
        slab_abo, _tile(n_tok, 256))
    _, (slab_mlp,) = _exchange_wait(ag_mlp, "gather_mlp_wait", after=x1)
    f, df1, h2, df2, dmix, dx1, small_mlp = _mlp_fwd_bwd(
        x1, mix, tgt, row(norm1_post_g), row(norm2_pre_g), row(norm2_post_g), slab_mlp, _tile(n_tok, 256))

    rs_mlp = _exchange_start(
        [_Part(_wgrad(df1, h2, "wgrad_mlp_in", tm, tk, BF16), True, rows["w1"], 0, off["w1"]),
         _Part(_wgrad(f, df2, "wgrad_mlp_out", tm, tk, BF16), True, rows["w2"], 0, off["w2"])],
        [scattered("mlp")], "scatter_mlp_start")
    dpa, dya, dyb, dva, dvb, small_mix = _bwd_mix(
        dmix, ya, yb, proj, va, vb, row(ln_b_g), row(ln_b_b), slab_abo, _tile(n_tok, 512), [rs_mlp.token])
    rs_abo = _exchange_start(
        [_Part(_wgrad(qa, dya, "wgrad_a_out", tm, tk, BF16), True, rows["wa"], 0, off["wa"]),
         _Part(_wgrad(sb, dyb, "wgrad_b_out", tm, tk, BF16), True, rows["wb"], 0, off["wb"]),
         _Part(_wgrad(merged, dmix, "wgrad_o", tm, tk, BF16), True, rows["wo"], 0, off["wo"])],
        [scattered("abo")], "scatter_abo_start")
    dpb, small_conv = _bwd_conv(dva, dvb, p, u, proj, caw, cbw, _tile(n_tok, 512), [rs_abo.token])

    zeros = lambda r: jnp.zeros((r, dm), F32)
    small = jnp.concatenate([
        zeros(1),
        small_mix[2:3],
        small_mix[3:4],
        small_mix[0:2],
        small_mlp[2:3],
        small_mlp[1:2],
        small_mlp[0:1],
        small_mix[4:5], small_conv[3:7], small_mix[5:7],
        small_mlp[3:4],
        small_conv[0:CONV_A], zeros(SUBLANE - CONV_A),
        small_conv[8:8 + CONV_B], zeros(1),
    ], axis=0)

    rs_in = _exchange_start(
        [_Part(_wgrad_in(dpa, dpb, h, tk), True, rows["win"], 0, off["win"]),
         _Part(small, False, small.shape[0], 1, 0)],
        [scattered("in"), _place_own(small, (N_DEV,) + small.shape, me, "place_small")], "scatter_in_start")
    grad_x, small_in = _bwd_in(dpa, dpb, xs, dx1, row(norm1_pre_g), slab_in, _tile(n_tok, 512), [rs_in.token])

    tr = min(LANE, rows["wa"])
    (g_w1, g_w2), (recv_mlp,) = _exchange_wait(rs_mlp, "scatter_mlp_wait", after=grad_x)
    (g_wa, g_wb, g_wo), (recv_abo,) = _exchange_wait(rs_abo, "scatter_abo_wait", after=grad_x)
    big = {
        "w_mlp_in": _adam_big(recv_mlp, g_w1, me, off["w1"], rows["w1"], w_mlp_in, m_w_mlp_in, v_w_mlp_in, True,
                              "adam_w_mlp_in", tr),
        "w_mlp_out": _adam_big(recv_mlp, g_w2, me, off["w2"], rows["w2"], w_mlp_out, m_w_mlp_out, v_w_mlp_out,
                               False, "adam_w_mlp_out", tr),
        "w_a_out": _adam_big(recv_abo, g_wa, me, off["wa"], rows["wa"], w_a_out, m_w_a_out, v_w_a_out, False,
                             "adam_w_a_out", tr),
        "w_b_out": _adam_big(recv_abo, g_wb, me, off["wb"], rows["wb"], w_b_out, m_w_b_out, v_w_b_out, False,
                             "adam_w_b_out", tr),
        "w_o": _adam_big(recv_abo, g_wo, me, off["wo"], rows["wo"], w_o, m_w_o, v_w_o, False, "adam_w_o", tr),
    }
    (g_win, _), (recv_in, recv_small) = _exchange_wait(rs_in, "scatter_in_wait", after=big["w_o"][3])
    recv_last, = _all_gather([small_in], "gather_last")
    big["w_in"] = _adam_big(recv_in, g_win, me, off["win"], rows["win"], w_in, m_w_in, v_w_in, True, "adam_w_in", tr)

    small_names = ("norm1_pre_g", "conv_a_b", "conv_b_b", "ln_b_g", "ln_b_b", "norm1_post_g", "norm2_pre_g",
                   "norm2_post_g")
    given = dict(
        norm1_pre_g=(norm1_pre_g, m_norm1_pre_g, v_norm1_pre_g), conv_a_b=(conv_a_b, m_conv_a_b, v_conv_a_b),
        conv_b_b=(conv_b_b, m_conv_b_b, v_conv_b_b), ln_b_g=(ln_b_g, m_ln_b_g, v_ln_b_g),
        ln_b_b=(ln_b_b, m_ln_b_b, v_ln_b_b), norm1_post_g=(norm1_post_g, m_norm1_post_g, v_norm1_post_g),
        norm2_pre_g=(norm2_pre_g, m_norm2_pre_g, v_norm2_pre_g),
        norm2_post_g=(norm2_post_g, m_norm2_post_g, v_norm2_post_g))
    params = [(j, False, tuple(row(a) for a in given[name])) for j, name in enumerate(small_names)]
    params.append((SUBLANE, False, tuple(a.reshape(7, dm) for a in (b_in, m_b_in, v_b_in))))
    params.append((CONV_A_ROW, True, (conv_a_w, m_conv_a_w, v_conv_a_w)))
    params.append((CONV_B_ROW, True, (conv_b_w, m_conv_b_w, v_conv_b_w)))
    loss, small_out = _adam_small(recv_small, recv_last, me, params, dm)
    small_leaves = {name: tuple(a.reshape(dm) for a in small_out[j]) for j, name in enumerate(small_names)}
    small_leaves["b_in"] = tuple(a.reshape(7 * dm) for a in small_out[len(small_names)])
    small_leaves["conv_a_w"] = small_out[len(small_names) + 1]
    small_leaves["conv_b_w"] = small_out[len(small_names) + 2]

    order = ("norm1_pre_g", "w_in", "b_in", "conv_a_w", "conv_a_b", "w_a_out", "conv_b_w", "conv_b_b", "ln_b_g",
             "ln_b_b", "w_b_out", "w_o", "norm1_post_g", "norm2_pre_g", "w_mlp_in", "w_mlp_out", "norm2_post_g")
    leaves = [big[name] if name in big else small_leaves[name] for name in order]
    grads, deltas, new_m, new_v = zip(*leaves)
    return (loss.reshape(()), grad_x.reshape(x.shape), *grads, *deltas, *new_m, *new_v)
```

```python
from typing import NamedTuple

import jax
import jax.numpy as jnp
from jax import lax
from jax.experimental import pallas as pl
from jax.experimental.pallas import tpu as pltpu

F32 = jnp.float32
BF16 = jnp.bfloat16

RMS_EPS = 1e-6
LN_EPS = 1e-5
ADAM_LR = 0.001
ADAM_B1 = 0.9
ADAM_B2 = 0.999
ADAM_EPS = 1e-08
ADAM_WD = 0.01
ADAM_STEP = 10

N_DEV = 8
CONV_A = 3
CONV_B = 31
LANE = 128
SUBLANE = 8
HALO = 16
FWD_CONV_ROWS = 32
BWD_CONV_ROWS = 64
MIB = 1 << 20
FLIPS = ((0, 0, 1), (0, 1, 0), (1, 0, 0), (0, 1, 1), (1, 0, 1), (1, 1, 0), (1, 1, 1))
MESH = pl.DeviceIdType.MESH


def _layout(d_model):
    e = d_model // N_DEV
    rows = {"win": 7 * e, "w1": 4 * e, "w2": 4 * e, "wa": e, "wb": e, "wo": e}
    off = {"win": 0, "w1": 0, "w2": 4 * e, "wa": 0, "wb": e, "wo": 2 * e}
    return rows, off, {"in": 7 * e, "mlp": 8 * e, "abo": 3 * e}


def _after(body, deps):
    def ordered(*refs):
        return body(*refs[len(deps):])
    return ordered


def _params(vmem_mib):
    return pltpu.CompilerParams(dimension_semantics=("arbitrary",), vmem_limit_bytes=vmem_mib * MIB)


def _whole():
    return pl.BlockSpec(memory_space=pltpu.VMEM)


def _hbm():
    return pl.BlockSpec(memory_space=pl.ANY)


def _in_hbm(a):
    return pltpu.with_memory_space_constraint(a, pltpu.HBM)


def _rows(ts, width, col=0):
    return pl.BlockSpec((ts, width), lambda i: (i, col))


def _halo_specs(ts, width, n_rows):
    per = ts // HALO
    last = n_rows // HALO - 1
    return [
        pl.BlockSpec((ts, width), lambda i: (i, 0)),
        pl.BlockSpec((HALO, width), lambda i: (jnp.maximum(i * per - 1, 0), 0)),
        pl.BlockSpec((HALO, width), lambda i: (jnp.minimum((i + 1) * per, last), 0)),
    ]


def _dot(a, b):
    return jnp.dot(a, b, preferred_element_type=F32)


def _dot_nt(a, b):
    return lax.dot_general(a, b, (((1,), (1,)), ((), ())), preferred_element_type=F32)


def _dot_tn(a, b):
    return lax.dot_general(a, b, (((0,), (0,)), ((), ())), preferred_element_type=F32)


def _rms(u):
    return lax.rsqrt(jnp.mean(u * u, axis=-1, keepdims=True) + RMS_EPS)


def _rms_bwd(dz, u, r, g):
    dzg = dz * g
    return r * dzg - u * (r * r * r) * jnp.mean(dzg * u, axis=-1, keepdims=True)


def _colsum(v):
    return jnp.sum(v, axis=0, keepdims=True)


def _sigmoid(v):
    return jax.nn.sigmoid(v)


def _weight_copies(slab_ref, off, rows, dst_ref, sems, first_sem):
    return [pltpu.make_async_copy(slab_ref.at[d, pl.ds(off, rows), :], dst_ref.at[pl.ds(d * rows, rows), :],
                                  sems.at[first_sem + d]) for d in range(N_DEV)]


def _on_first_step(copies, method):
    @pl.when(pl.program_id(0) == 0)
    def _():
        for cp in copies:
            getattr(cp, method)()


def _with_halos(main_ref, prev_ref, next_ref, i, n_steps):
    return (main_ref, jnp.where(i > 0, prev_ref[...], 0.0), jnp.where(i < n_steps - 1, next_ref[...], 0.0))


def _broadcast_taps(w_ref, wb_ref, n_taps):
    for k in range(n_taps):
        wb_ref[k] = jnp.broadcast_to(w_ref[k:k + 1, :], wb_ref.shape[1:])


def _conv_tile(tile, wb_ref, starts, ts, width, emit, rolled_ref, rows=FWD_CONV_ROWS):
    main_ref, prev, nxt = tile
    span = ts + 2 * HALO
    nv = rows // SUBLANE
    for cb in range(width // LANE):
        lanes = slice(cb * LANE, (cb + 1) * LANE)
        slot = cb % 2
        window = jnp.concatenate([prev[:, lanes], main_ref[:, lanes], nxt[:, lanes]], axis=0)
        for b in sorted({st % SUBLANE for st in starts}):
            rolled_ref[slot, b] = window if b == 0 else pltpu.roll(window, span - b, axis=0)
        for r0 in range(0, ts, rows):
            acc = jnp.zeros((nv, SUBLANE, LANE), F32)
            for k, st in enumerate(starts):
                shifted = rolled_ref[slot, st % SUBLANE, pl.ds(r0 + st - st % SUBLANE, rows), :]
                acc = acc + shifted.reshape(nv, SUBLANE, LANE) * wb_ref[k, :, lanes][None]
            emit(r0, pl.ds(cb * LANE, LANE), acc.reshape(rows, LANE))


def _window(tile, r0, cb, ts, rows):
    main_ref, prev, nxt = tile
    lanes = slice(cb * LANE, (cb + 1) * LANE)
    lo, hi = max(r0 - HALO, 0), min(r0 + rows + HALO, ts)
    pieces = [prev[:, lanes]] if r0 - HALO < 0 else []
    pieces.append(main_ref[lo:hi, lanes])
    if r0 + rows + HALO > ts:
        pieces.append(nxt[:, lanes])
    return pieces[0] if len(pieces) == 1 else jnp.concatenate(pieces, axis=0)


def _phases(starts):
    groups = {}
    for k, st in enumerate(starts):
        groups.setdefault(st % SUBLANE, []).append((k, st // SUBLANE))
    return sorted(groups.items())


def _shifted(blk, b):
    n = blk.shape[0]
    rolled = blk if b == 0 else pltpu.roll(blk, n - b, axis=0)
    return rolled.reshape(n // SUBLANE, SUBLANE, blk.shape[1])


def _conv_bwd_tile(dv_tile, u_ref, wb_ref, acc_ref, n_taps, ts, width, emit, rows=BWD_CONV_ROWS):
    groups = _phases(_bwd_starts(n_taps))
    nv = rows // SUBLANE
    for r0 in range(0, ts, rows):
        for cb in range(width // LANE):
            lanes = pl.ds(cb * LANE, LANE)
            blk = _window(dv_tile, r0, cb, ts, rows)
            u = u_ref[pl.ds(r0, rows), lanes].reshape(nv, SUBLANE, LANE)
            du = jnp.zeros((nv, SUBLANE, LANE), F32)
            for b, taps in groups:
                sh = _shifted(blk, b)
                for k, m in taps:
                    du = du + sh[m:m + nv] * wb_ref[k, :, lanes][None]
                    acc_ref[k, :, lanes] += jnp.sum(sh[m:m + nv] * u, axis=0)
            emit(r0, lanes, du.reshape(rows, LANE))


def _fwd_starts(n_taps):
    pad = (n_taps - 1) // 2
    return [HALO - pad + k for k in range(n_taps)]


def _bwd_starts(n_taps):
    pad = (n_taps - 1) // 2
    return [HALO + pad - k for k in range(n_taps)]


def _peer(x, y, c, flip):
    fx, fy, fc = flip
    return (1 - x if fx else x, 1 - y if fy else y, 1 - c if fc else c)


def _place_cast(pieces, me, name):
    n = len(pieces)
    counts = [a.shape[1] if t else a.shape[0] for a, t in pieces]
    width = pieces[0][0].shape[0] if pieces[0][1] else pieces[0][0].shape[1]
    total = sum(counts)

    def body(me_ref, *refs):
        ins, own_ref, land_ref = refs[:n], refs[n], refs[n + 1]
        first = 0
        for (a, transpose), in_ref, count in zip(pieces, ins, counts):
            block = (in_ref[...].T if transpose else in_ref[...]).astype(BF16)
            own_ref[first:first + count, :] = block
            land_ref[first:first + count, :] = block
            first += count

    return pl.pallas_call(
        body, name=name,
        grid_spec=pltpu.PrefetchScalarGridSpec(
            num_scalar_prefetch=1, grid=(1,),
            in_specs=[pl.BlockSpec(a.shape, lambda i, me_ref: (0, 0)) for a, _ in pieces],
            out_specs=[pl.BlockSpec((total, width), lambda i, me_ref: (0, 0)),
                       pl.BlockSpec((None, total, width), lambda i, me_ref: (me_ref[0], 0, 0))]),
        out_shape=[pltpu.HBM((total, width), BF16), pltpu.HBM((N_DEV, total, width), BF16)],
        compiler_params=_params(32),
    )(me, *[_in_hbm(a) for a, _ in pieces])


def _all_gather_two_level(shards, placed, name):
    n = len(shards)
    given = [j for j in range(n) if placed[j] is not None]

    def body(*refs):
        ins, outs = refs[:n], refs[n + len(given):2 * n + len(given)]
        send_sems, recv_sems, local_sems = refs[2 * n + len(given):]
        x, y, c = lax.axis_index("x"), lax.axis_index("y"), lax.axis_index("c")
        me, sibling = (x, y, c), (x, y, 1 - c)
        chips = [(1 - x, y), (x, 1 - y), (1 - x, 1 - y)]

        def slot(j, dev):
            return outs[j].at[4 * dev[0] + 2 * dev[1] + dev[2]]

        def copy(k, j, block, to, src=None):
            return pltpu.make_async_remote_copy(
                src_ref=slot(j, block) if src is None else src, dst_ref=slot(j, block),
                send_sem=send_sems.at[k * n + j], recv_sem=recv_sems.at[k * n + j], device_id=to, device_id_type=MESH)

        local = [pltpu.make_async_copy(ins[j], slot(j, me), local_sems.at[j]) for j in range(n) if j not in given]
        for cp in local:
            cp.start()
        first = [copy(0, j, me, sibling, src=ins[j]) for j in range(n)]
        first += [copy(1 + t, j, me, (*chip, c), src=ins[j]) for t, chip in enumerate(chips) for j in range(n)]
        for cp in first:
            cp.start()
        passed = []
        for t, chip in enumerate(chips):
            for j in range(n):
                copy(1 + t, j, (*chip, c), me).wait_recv()
                passed.append(copy(4 + t, j, (*chip, c), sibling))
                passed[-1].start()
        for j in range(n):
            copy(0, j, sibling, me).wait_recv()
        for t, chip in enumerate(chips):
            for j in range(n):
                copy(4 + t, j, (*chip, 1 - c), me).wait_recv()
        for cp in first + passed:
            cp.wait_send()
        for cp in local:
            cp.wait()

    return pl.pallas_call(
        body, name=name,
        out_shape=[jax.ShapeDtypeStruct((N_DEV,) + s.shape, s.dtype) for s in shards],
        in_specs=[_hbm()] * (n + len(given)), out_specs=[_hbm()] * n,
        input_output_aliases={n + i: j for i, j in enumerate(given)},
        scratch_shapes=[pltpu.SemaphoreType.DMA((7 * n,)), pltpu.SemaphoreType.DMA((7 * n,)),
                        pltpu.SemaphoreType.DMA((n,))],
    )(*shards, *[placed[j] for j in given])


class _Part(NamedTuple):
    src: jax.Array
    scatter: bool
    rows: int
    land: int
    off: int


class _Started(NamedTuple):
    send_sems: jax.Array
    recv_sems: jax.Array
    thru: tuple
    token: jax.Array
    parts: tuple


def _exchange_copies(srcs, lands, send_sems, recv_sems, parts):
    n = len(parts)
    x, y, c = lax.axis_index("x"), lax.axis_index("y"), lax.axis_index("c")
    me = 4 * x + 2 * y + c

    def block(j, dev):
        p = parts[j]
        return srcs[j].at[pl.ds(pl.multiple_of(dev * p.rows, SUBLANE), p.rows), :] if p.scatter else srcs[j]

    def slot(j, index):
        p = parts[j]
        return lands[p.land].at[index, pl.ds(p.off, p.rows), :]

    sends, recvs = [], []
    for k, flip in enumerate(FLIPS):
        px, py, pc = _peer(x, y, c, flip)
        peer = 4 * px + 2 * py + pc
        for j in range(n):
            sems = dict(send_sem=send_sems.at[k * n + j], recv_sem=recv_sems.at[k * n + j],
                        device_id=(px, py, pc), device_id_type=MESH)
            to, got = (k, k) if parts[j].scatter else (me, peer)
            sends.append(pltpu.make_async_remote_copy(src_ref=block(j, peer), dst_ref=slot(j, to), **sems))
            recvs.append(pltpu.make_async_remote_copy(src_ref=block(j, peer), dst_ref=slot(j, got), **sems))
    return sends, recvs


def _exchange_start(parts, lands, name, after=None):
    n, nl = len(parts), len(lands)
    n_in = n + nl + (after is not None)

    def body(*refs):
        srcs, land_refs = refs[:n], refs[n:n + nl]
        send_sems, recv_sems = refs[n_in], refs[n_in + 1]
        token = refs[n_in + 2 + n + nl]
        sends, _ = _exchange_copies(srcs, land_refs, send_sems, recv_sems, parts)
        for cp in sends:
            cp.start()
        token[...] = jnp.zeros_like(token)

    hbm = pl.BlockSpec(memory_space=pltpu.HBM)
    sem = pl.BlockSpec(memory_space=pltpu.SEMAPHORE)
    fresh = lambda s: lax.empty(s.shape, s.dtype) if isinstance(s, jax.ShapeDtypeStruct) else s
    args = [pltpu.with_memory_space_constraint(p.src, pltpu.HBM) for p in parts]
    args += [pltpu.with_memory_space_constraint(fresh(s), pltpu.HBM) for s in lands]
    args += [] if after is None else [after]
    out = pl.pallas_call(
        body, name=name,
        out_shape=(pltpu.SemaphoreType.DMA((7 * n,)), pltpu.SemaphoreType.DMA((7 * n,)),
                   *[pltpu.HBM(a.shape, a.dtype) for a in args[:n + nl]], jax.ShapeDtypeStruct((SUBLANE, LANE), F32)),
        in_specs=[hbm] * (n + nl) + [_hbm()] * (after is not None),
        out_specs=(sem, sem, *[hbm] * (n + nl), _whole()),
        input_output_aliases={j: 2 + j for j in range(n + nl)},
        compiler_params=pltpu.CompilerParams(has_side_effects=pltpu.SideEffectType.DATAFLOW_SIDE_EFFECTING),
    )(*args)
    return _Started(out[0], out[1], tuple(out[2:2 + n + nl]), out[2 + n + nl], tuple(parts))


def _exchange_wait(started, name, after):
    parts = started.parts
    n, nl = len(parts), len(started.thru) - len(parts)

    def body(*refs):
        srcs, land_refs = refs[:n], refs[n:n + nl]
        send_sems, recv_sems = refs[n + nl], refs[n + nl + 1]
        sends, recvs = _exchange_copies(srcs, land_refs, send_sems, recv_sems, parts)
        for cp in sends:
            cp.wait_send()
        for cp in recvs:
            cp.wait_recv()

    hbm = pl.BlockSpec(memory_space=pltpu.HBM)
    sem = pl.BlockSpec(memory_space=pltpu.SEMAPHORE)
    out = pl.pallas_call(
        body, name=name,
        out_shape=tuple(pltpu.HBM(a.shape, a.dtype) for a in started.thru),
        in_specs=[hbm] * (n + nl) + [sem, sem, _hbm()], out_specs=tuple([hbm] * (n + nl)),
        input_output_aliases={j: j for j in range(n + nl)},
        compiler_params=pltpu.CompilerParams(has_side_effects=pltpu.SideEffectType.DATAFLOW_SIDE_EFFECTING),
    )(*started.thru, started.send_sems, started.recv_sems, after)
    return list(out[:n]), list(out[n:])


def _fwd_in(x, g1, b_in, slab, ts, deps):
    n_tok, dm = x.shape
    rows, off, _ = _layout(dm)
    width = 7 * dm

    def body(x_ref, g1_ref, b_ref, slab_ref, proj_ref, p_ref, u_ref, h_ref, w_v, sems):
        copies = _weight_copies(slab_ref, off["win"], rows["win"], w_v, sems, 0)
        _on_first_step(copies, "start")
        _on_first_step(copies, "wait")
        xv = x_ref[...]
        h = (xv * _rms(xv) * g1_ref[...]).astype(BF16)
        h_ref[...] = h
        cols = []
        for j in range(7):
            pj = _dot_nt(h, w_v[pl.ds(j * dm, dm), :]) + b_ref[:, j * dm:(j + 1) * dm]
            proj_ref[:, j * dm:(j + 1) * dm] = pj.astype(proj_ref.dtype)
            if 1 <= j <= 4:
                cols.append(pj)
            if j == 2:
                p_ref[...] = cols[0] * cols[1]
            if j == 4:
                u_ref[...] = cols[2] * _sigmoid(cols[3])

    return pl.pallas_call(
        _after(body, deps), name="fwd_in", grid=(n_tok // ts,),
        in_specs=[_whole()] * len(deps) + [_rows(ts, dm), _whole(), _whole(), _hbm()],
        out_specs=[_rows(ts, width), _rows(ts, dm), _rows(ts, dm), _rows(ts, dm)],
        out_shape=[jax.ShapeDtypeStruct((n_tok, width), BF16), jax.ShapeDtypeStruct((n_tok, dm), F32),
                   jax.ShapeDtypeStruct((n_tok, dm), F32), jax.ShapeDtypeStruct((n_tok, dm), BF16)],
        scratch_shapes=[pltpu.VMEM((width, dm), BF16), pltpu.SemaphoreType.DMA((N_DEV,))],
        compiler_params=_params(56),
    )(*deps, x, g1, b_in, slab)


def _fwd_mix(p, u, proj, x, caw, cab, cbw, cbb, lng, lnb, g1post, slab, ts):
    n_tok, dm = x.shape
    rows, off, _ = _layout(dm)
    n_steps = n_tok // ts

    def body(p_ref, p_prev, p_next, u_ref, u_prev, u_next, bg_ref, za_ref, zb_ref, x_ref,
             caw_ref, cab_ref, cbw_ref, cbb_ref, lng_ref, lnb_ref, g1p_ref, slab_ref,
             va_ref, vb_ref, ya_ref, yb_ref, qa_ref, sb_ref, mg_ref, mix_ref, x1_ref,
             wa_v, wb_v, wo_v, tap_a, tap_b, sems, rolled_a, rolled_b):
        i = pl.program_id(0)
        copies = (_weight_copies(slab_ref, off["wa"], rows["wa"], wa_v, sems, 0)
                  + _weight_copies(slab_ref, off["wb"], rows["wb"], wb_v, sems, N_DEV)
                  + _weight_copies(slab_ref, off["wo"], rows["wo"], wo_v, sems, 2 * N_DEV))
        _on_first_step(copies, "start")

        @pl.when(i == 0)
        def _():
            _broadcast_taps(caw_ref, tap_a, CONV_A)
            _broadcast_taps(cbw_ref, tap_b, CONV_B)

        def emit_a(r0, lanes, acc):
            va_ref[pl.ds(r0, acc.shape[0]), lanes] = acc + cab_ref[:, lanes]

        def emit_b(r0, lanes, acc):
            vb_ref[pl.ds(r0, acc.shape[0]), lanes] = acc + cbb_ref[:, lanes]

        _conv_tile(_with_halos(p_ref, p_prev, p_next, i, n_steps), tap_a, _fwd_starts(CONV_A), ts, dm, emit_a,
                   rolled_a)
        _conv_tile(_with_halos(u_ref, u_prev, u_next, i, n_steps), tap_b, _fwd_starts(CONV_B), ts, dm, emit_b,
                   rolled_b)
        _on_first_step(copies, "wait")

        qa = (bg_ref[...].astype(F32) * va_ref[...]).astype(BF16)
        qa_ref[...] = qa
        ya = _dot(qa, wa_v[...])
        vb = vb_ref[...]
        xc = vb - jnp.mean(vb, axis=-1, keepdims=True)
        rstd = lax.rsqrt(jnp.mean(xc * xc, axis=-1, keepdims=True) + LN_EPS)
        ln = xc * rstd * lng_ref[...] + lnb_ref[...]
        sb = (ln * _sigmoid(ln)).astype(BF16)
        sb_ref[...] = sb
        yb = _dot(sb, wb_v[...])
        ya_ref[...] = ya.astype(BF16)
        yb_ref[...] = yb.astype(BF16)
        merged = (_sigmoid(za_ref[...].astype(F32)) * ya + _sigmoid(zb_ref[...].astype(F32)) * yb).astype(BF16)
        mg_ref[...] = merged
        mix = _dot(merged, wo_v[...])
        mix_ref[...] = mix
        x1_ref[...] = x_ref[...] + mix * _rms(mix) * g1p_ref[...]

    tok = lambda dt: jax.ShapeDtypeStruct((n_tok, dm), dt)
    return pl.pallas_call(
        body, name="fwd_mix", grid=(n_steps,),
        in_specs=(_halo_specs(ts, dm, n_tok) + _halo_specs(ts, dm, n_tok)
                  + [_rows(ts, dm, 0), _rows(ts, dm, 5), _rows(ts, dm, 6), _rows(ts, dm)]
                  + [_whole()] * 7 + [_hbm()]),
        out_specs=[_rows(ts, dm)] * 9,
        out_shape=[tok(F32), tok(F32), tok(BF16), tok(BF16), tok(BF16), tok(BF16), tok(BF16), tok(F32), tok(F32)],
        scratch_shapes=[pltpu.VMEM((dm, dm), BF16), pltpu.VMEM((dm, dm), BF16), pltpu.VMEM((dm, dm), BF16),
                        pltpu.VMEM((CONV_A, SUBLANE, dm), F32), pltpu.VMEM((CONV_B, SUBLANE, dm), F32),
                        pltpu.SemaphoreType.DMA((3 * N_DEV,)),
                        pltpu.VMEM((2, SUBLANE, ts + 2 * HALO, LANE), F32),
                        pltpu.VMEM((2, SUBLANE, ts + 2 * HALO, LANE), F32)],
        compiler_params=_params(48),
    )(p, p, p, u, u, u, proj, proj, proj, x, caw, cab, cbw, cbb, lng, lnb, g1post, slab)


def _mlp_fwd_bwd(x1, mix, tgt, g1post, g2pre, g2post, slab, ts):
    n_tok, dm = x1.shape
    rows, off, _ = _layout(dm)
    ff = 4 * dm

    def body(x1_ref, mix_ref, t_ref, g1p_ref, g2pre_ref, g2post_ref, slab_ref,
             f_ref, df1_ref, h2_ref, df2_ref, dmix_ref, dx1_ref, small_ref, w1_v, w2_v, relu_v, sems):
        w1_copies = _weight_copies(slab_ref, off["w1"], rows["w1"], w1_v, sems, 0)
        w2_copies = _weight_copies(slab_ref, off["w2"], rows["w2"], w2_v, sems, N_DEV)
        _on_first_step(w1_copies + w2_copies, "start")

        @pl.when(pl.program_id(0) == 0)
        def _():
            small_ref[...] = jnp.zeros_like(small_ref)

        _on_first_step(w1_copies + w2_copies, "wait")
        x1v = x1_ref[...]
        r3 = _rms(x1v)
        g2pre = g2pre_ref[...]
        h2 = (x1v * r3 * g2pre).astype(BF16)
        h2_ref[...] = h2
        for c in range(4):
            blk = pl.ds(c * dm, dm)
            relu = jnp.maximum(_dot_nt(h2, w1_v[blk, :]), 0.0)
            relu_v[:, c * dm:(c + 1) * dm] = relu
            f_ref[:, c * dm:(c + 1) * dm] = (relu * relu).astype(BF16)
        f2 = _dot(f_ref[...], w2_v[...])
        r4 = _rms(f2)
        g2post = g2post_ref[...]
        err = x1v + f2 * r4 * g2post - t_ref[...]
        dy = err * (1.0 / dm)
        small_ref[3:4, :] += _colsum(err * err)
        small_ref[0:1, :] += _colsum(dy * f2 * r4)
        df2 = _rms_bwd(dy, f2, r4, g2post).astype(BF16)
        df2_ref[...] = df2
        for c in range(4):
            blk = pl.ds(c * dm, dm)
            df1 = (_dot_nt(df2, w2_v[blk, :]) * (2.0 * relu_v[:, c * dm:(c + 1) * dm])).astype(BF16)
            df1_ref[:, c * dm:(c + 1) * dm] = df1
        dh2 = _dot(df1_ref[...], w1_v[...])
        small_ref[1:2, :] += _colsum(dh2 * x1v * r3)
        dx1 = dy + _rms_bwd(dh2, x1v, r3, g2pre)
        dx1_ref[...] = dx1
        mixv = mix_ref[...]
        r2 = _rms(mixv)
        small_ref[2:3, :] += _colsum(dx1 * mixv * r2)
        dmix_ref[...] = _rms_bwd(dx1, mixv, r2, g1p_ref[...]).astype(BF16)

    tok = lambda w, dt: jax.ShapeDtypeStruct((n_tok, w), dt)
    return pl.pallas_call(
        body, name="mlp_fwd_bwd", grid=(n_tok // ts,),
        in_specs=[_rows(ts, dm)] * 3 + [_whole()] * 3 + [_hbm()],
        out_specs=[_rows(ts, ff), _rows(ts, ff), _rows(ts, dm), _rows(ts, dm), _rows(ts, dm), _rows(ts, dm),
                   pl.BlockSpec((SUBLANE, dm), lambda i: (0, 0))],
        out_shape=[tok(ff, BF16), tok(ff, BF16), tok(dm, BF16), tok(dm, BF16), tok(dm, BF16), tok(dm, F32),
                   jax.ShapeDtypeStruct((SUBLANE, dm), F32)],
        scratch_shapes=[pltpu.VMEM((ff, dm), BF16), pltpu.VMEM((ff, dm), BF16), pltpu.VMEM((ts, ff), F32),
                        pltpu.SemaphoreType.DMA((2 * N_DEV,))],
        compiler_params=_params(56),
    )(x1, mix, tgt, g1post, g2pre, g2post, slab)


def _bwd_mix(dmix, ya, yb, proj, va, vb, lng, lnb, slab, ts, deps):
    n_tok, dm = dmix.shape
    rows, off, _ = _layout(dm)
    n_steps = n_tok // ts

    def body(dmix_ref, ya_ref, yb_ref, bg_ref, za_ref, zb_ref, va_ref, vb_ref, lng_ref, lnb_ref, slab_ref,
             dpa_ref, dya_ref, dyb_ref, dva_ref, dvb_ref, small_ref, wa_v, wb_v, wo_v, sems):
        wo_copies = _weight_copies(slab_ref, off["wo"], rows["wo"], wo_v, sems, 2 * N_DEV)
        ab_copies = (_weight_copies(slab_ref, off["wa"], rows["wa"], wa_v, sems, 0)
                     + _weight_copies(slab_ref, off["wb"], rows["wb"], wb_v, sems, N_DEV))
        _on_first_step(wo_copies + ab_copies, "start")

        @pl.when(pl.program_id(0) == 0)
        def _():
            small_ref[...] = jnp.zeros_like(small_ref)

        _on_first_step(wo_copies, "wait")
        dmerged = _dot_nt(dmix_ref[...], wo_v[...])
        _on_first_step(ab_copies, "wait")
        sa = _sigmoid(za_ref[...].astype(F32))
        sg = _sigmoid(zb_ref[...].astype(F32))
        dza = dmerged * ya_ref[...].astype(F32) * sa * (1.0 - sa)
        dzb = dmerged * yb_ref[...].astype(F32) * sg * (1.0 - sg)
        dpa_ref[:, dm:2 * dm] = dza.astype(BF16)
        dpa_ref[:, 2 * dm:3 * dm] = dzb.astype(BF16)
        small_ref[5:6, :] += _colsum(dza)
        small_ref[6:7, :] += _colsum(dzb)

        dya = (dmerged * sa).astype(BF16)
        dya_ref[...] = dya
        dqa = _dot_nt(dya, wa_v[...])
        dbg = dqa * va_ref[...]
        dpa_ref[:, 0:dm] = dbg.astype(BF16)
        small_ref[4:5, :] += _colsum(dbg)
        dva = dqa * bg_ref[...].astype(F32)
        dva_ref[...] = dva
        small_ref[2:3, :] += _colsum(dva)

        dyb = (dmerged * sg).astype(BF16)
        dyb_ref[...] = dyb
        dsb = _dot_nt(dyb, wb_v[...])
        vb = vb_ref[...]
        xc = vb - jnp.mean(vb, axis=-1, keepdims=True)
        rstd = lax.rsqrt(jnp.mean(xc * xc, axis=-1, keepdims=True) + LN_EPS)
        nrm = xc * rstd
        lng_v = lng_ref[...]
        ln = nrm * lng_v + lnb_ref[...]
        sl = _sigmoid(ln)
        dln = dsb * (sl * (1.0 + ln * (1.0 - sl)))
        small_ref[0:1, :] += _colsum(dln * nrm)
        small_ref[1:2, :] += _colsum(dln)
        dn = dln * lng_v
        dvb = rstd * (dn - jnp.mean(dn, axis=-1, keepdims=True)
                      - nrm * jnp.mean(dn * nrm, axis=-1, keepdims=True))
        dvb_ref[...] = dvb
        small_ref[3:4, :] += _colsum(dvb)

    tok = lambda w, dt: jax.ShapeDtypeStruct((n_tok, w), dt)
    return pl.pallas_call(
        _after(body, deps), name="bwd_mix", grid=(n_steps,),
        in_specs=([_whole()] * len(deps) + [_rows(ts, dm)] * 3
                  + [_rows(ts, dm, 0), _rows(ts, dm, 5), _rows(ts, dm, 6)]
                  + [_rows(ts, dm)] * 2 + [_whole()] * 2 + [_hbm()]),
        out_specs=[_rows(ts, 3 * dm), _rows(ts, dm), _rows(ts, dm), _rows(ts, dm), _rows(ts, dm),
                   pl.BlockSpec((SUBLANE, dm), lambda i: (0, 0))],
        out_shape=[tok(3 * dm, BF16), tok(dm, BF16), tok(dm, BF16), tok(dm, F32), tok(dm, F32),
                   jax.ShapeDtypeStruct((SUBLANE, dm), F32)],
        scratch_shapes=[pltpu.VMEM((dm, dm), BF16), pltpu.VMEM((dm, dm), BF16), pltpu.VMEM((dm, dm), BF16),
                        pltpu.SemaphoreType.DMA((3 * N_DEV,))],
        compiler_params=_params(56),
    )(*deps, dmix, ya, yb, proj, proj, proj, va, vb, lng, lnb, slab)


def _bwd_conv(dva, dvb, p, u, proj, caw, cbw, ts, deps):
    n_tok, dm = dva.shape
    n_steps = n_tok // ts
    small_rows = 40

    def body(dva_ref, dva_prev, dva_next, dvb_ref, dvb_prev, dvb_next, p_ref, u_ref,
             cg_ref, ha_ref, a_ref, g_ref, caw_ref, cbw_ref,
             dproj_ref, small_ref,
             dp_v, du_v, tap_a, tap_b, gwa_v, gwb_v):
        i = pl.program_id(0)

        @pl.when(i == 0)
        def _():
            _broadcast_taps(caw_ref, tap_a, CONV_A)
            _broadcast_taps(cbw_ref, tap_b, CONV_B)
            small_ref[...] = jnp.zeros_like(small_ref)
            gwa_v[...] = jnp.zeros_like(gwa_v)
            gwb_v[...] = jnp.zeros_like(gwb_v)

        def emit_dp(r0, lanes, acc):
            dp_v[pl.ds(r0, acc.shape[0]), lanes] = acc

        def emit_du(r0, lanes, acc):
            du_v[pl.ds(r0, acc.shape[0]), lanes] = acc

        _conv_bwd_tile(_with_halos(dva_ref, dva_prev, dva_next, i, n_steps), p_ref, tap_a, gwa_v, CONV_A, ts, dm,
                       emit_dp)
        _conv_bwd_tile(_with_halos(dvb_ref, dvb_prev, dvb_next, i, n_steps), u_ref, tap_b, gwb_v, CONV_B, ts, dm,
                       emit_du)

        dp = dp_v[...]
        dcg = dp * ha_ref[...].astype(F32)
        dha = dp * cg_ref[...].astype(F32)
        du = du_v[...]
        sg = _sigmoid(g_ref[...].astype(F32))
        da = du * sg
        dg = du * a_ref[...].astype(F32) * sg * (1.0 - sg)
        dproj_ref[:, 0:dm] = dcg.astype(BF16)
        dproj_ref[:, dm:2 * dm] = dha.astype(BF16)
        dproj_ref[:, 2 * dm:3 * dm] = da.astype(BF16)
        dproj_ref[:, 3 * dm:4 * dm] = dg.astype(BF16)
        small_ref[3:4, :] += _colsum(dcg)
        small_ref[4:5, :] += _colsum(dha)
        small_ref[5:6, :] += _colsum(da)
        small_ref[6:7, :] += _colsum(dg)

        @pl.when(i == n_steps - 1)
        def _():
            for k in range(CONV_A):
                small_ref[k:k + 1, :] = _colsum(gwa_v[k])
            for k in range(CONV_B):
                small_ref[SUBLANE + k:SUBLANE + k + 1, :] = _colsum(gwb_v[k])

    return pl.pallas_call(
        _after(body, deps), name="bwd_conv", grid=(n_steps,),
        in_specs=([_whole()] * len(deps) + _halo_specs(ts, dm, n_tok) * 2 + [_rows(ts, dm)] * 2
                  + [_rows(ts, dm, 1), _rows(ts, dm, 2), _rows(ts, dm, 3), _rows(ts, dm, 4)]
                  + [_whole()] * 2),
        out_specs=[_rows(ts, 4 * dm), pl.BlockSpec((small_rows, dm), lambda i: (0, 0))],
        out_shape=[jax.ShapeDtypeStruct((n_tok, 4 * dm), BF16), jax.ShapeDtypeStruct((small_rows, dm), F32)],
        scratch_shapes=[pltpu.VMEM((ts, dm), F32), pltpu.VMEM((ts, dm), F32),
                        pltpu.VMEM((CONV_A, SUBLANE, dm), F32), pltpu.VMEM((CONV_B, SUBLANE, dm), F32),
                        pltpu.VMEM((CONV_A, SUBLANE, dm), F32), pltpu.VMEM((CONV_B, SUBLANE, dm), F32)],
        compiler_params=_params(48),
    )(*deps, dva, dva, dva, dvb, dvb, dvb, p, u, proj, proj, proj, proj, caw, cbw)


def _bwd_in(dpa, dpb, x, dx1, g1, slab, ts, deps):
    n_tok, dm = x.shape
    rows, off, _ = _layout(dm)
    width = 7 * dm

    def body(dpa_ref, dpb_ref, x_ref, dx1_ref, g1_ref, slab_ref, gx_ref, small_ref, w_v, sems):
        copies = _weight_copies(slab_ref, off["win"], rows["win"], w_v, sems, 0)
        _on_first_step(copies, "start")

        @pl.when(pl.program_id(0) == 0)
        def _():
            small_ref[...] = jnp.zeros_like(small_ref)

        _on_first_step(copies, "wait")
        dh = (_dot(dpa_ref[:, 0:dm], w_v[0:dm, :]) + _dot(dpb_ref[...], w_v[dm:5 * dm, :])
              + _dot(dpa_ref[:, dm:3 * dm], w_v[5 * dm:7 * dm, :]))
        xv = x_ref[...]
        r1 = _rms(xv)
        small_ref[0:1, :] += _colsum(dh * xv * r1)
        gx_ref[...] = dx1_ref[...] + _rms_bwd(dh, xv, r1, g1_ref[...])

    return pl.pallas_call(
        _after(body, deps), name="bwd_in", grid=(n_tok // ts,),
        in_specs=[_whole()] * len(deps) + [_rows(ts, 3 * dm), _rows(ts, 4 * dm), _rows(ts, dm), _rows(ts, dm),
                                           _whole(), _hbm()],
        out_specs=[_rows(ts, dm), pl.BlockSpec((SUBLANE, dm), lambda i: (0, 0))],
        out_shape=[jax.ShapeDtypeStruct((n_tok, dm), F32), jax.ShapeDtypeStruct((SUBLANE, dm), F32)],
        scratch_shapes=[pltpu.VMEM((width, dm), BF16), pltpu.SemaphoreType.DMA((N_DEV,))],
        compiler_params=_params(56),
    )(*deps, dpa, dpb, x, dx1, g1, slab)


def _wgrad(a, b, name, tm, tk, out_dtype):
    n_tok, m = a.shape
    n = b.shape[1]
    k_steps = n_tok // tk

    def body(a_ref, b_ref, o_ref, acc_v):
        k = pl.program_id(1)

        @pl.when(k == 0)
        def _():
            acc_v[...] = jnp.zeros_like(acc_v)

        acc_v[...] += _dot_tn(a_ref[...], b_ref[...])

        @pl.when(k == k_steps - 1)
        def _():
            o_ref[...] = acc_v[...].astype(o_ref.dtype)

    return pl.pallas_call(
        body, name=name, grid=(m // tm, k_steps),
        in_specs=[pl.BlockSpec((tk, tm), lambda i, k: (k, i)), pl.BlockSpec((tk, n), lambda i, k: (k, 0))],
        out_specs=pl.BlockSpec((tm, n), lambda i, k: (i, 0)),
        out_shape=pltpu.HBM((m, n), out_dtype),
        scratch_shapes=[pltpu.VMEM((tm, n), F32)],
        compiler_params=pltpu.CompilerParams(dimension_semantics=("arbitrary", "arbitrary"),
                                             vmem_limit_bytes=40 * MIB),
    )(a, b)


def _wgrad_in(dpa, dpb, h, tk):
    n_tok, dm = h.shape
    k_steps = n_tok // tk
    last = k_steps - 1

    def from_a(i):
        return (i == 0) | (i >= 5)

    def body(a_ref, b_ref, h_ref, o_ref, acc_v):
        i, k = pl.program_id(0), pl.program_id(1)

        @pl.when(k == 0)
        def _():
            acc_v[...] = jnp.zeros_like(acc_v)

        @pl.when(from_a(i))
        def _():
            acc_v[...] += _dot_tn(a_ref[...], h_ref[...])

        @pl.when(jnp.logical_not(from_a(i)))
        def _():
            acc_v[...] += _dot_tn(b_ref[...], h_ref[...])

        @pl.when(k == last)
        def _():
            o_ref[...] = acc_v[...].astype(o_ref.dtype)

    a_index = lambda i, k: (jnp.where(from_a(i), k, last), jnp.where(i >= 5, i - 4, 0))
    b_index = lambda i, k: (jnp.where(from_a(i), jnp.where(i == 0, 0, last), k), jnp.clip(i - 1, 0, 3))
    return pl.pallas_call(
        body, name="wgrad_in", grid=(7, k_steps),
        in_specs=[pl.BlockSpec((tk, dm), a_index), pl.BlockSpec((tk, dm), b_index),
                  pl.BlockSpec((tk, dm), lambda i, k: (k, 0))],
        out_specs=pl.BlockSpec((dm, dm), lambda i, k: (i, 0)),
        out_shape=pltpu.HBM((7 * dm, dm), BF16),
        scratch_shapes=[pltpu.VMEM((dm, dm), F32)],
        compiler_params=pltpu.CompilerParams(dimension_semantics=("arbitrary", "arbitrary"),
                                             vmem_limit_bytes=48 * MIB),
    )(dpa, dpb, h)


def _adamw(w, g, m, v):
    m = ADAM_B1 * m + (1.0 - ADAM_B1) * g
    v = ADAM_B2 * v + (1.0 - ADAM_B2) * (g * g)
    m_hat = m / (1.0 - ADAM_B1 ** ADAM_STEP)
    v_hat = v / (1.0 - ADAM_B2 ** ADAM_STEP)
    delta = -ADAM_LR * (m_hat / (jnp.sqrt(v_hat) + ADAM_EPS) + ADAM_WD * w)
    return delta, m, v


def _adam_big(recv, part, me, off, rows, w, m, v, transpose, name, tr):
    dm = recv.shape[2]
    per = rows // tr

    def body(me_ref, own_ref, r_ref, w_ref, m_ref, v_ref, g_ref, d_ref, mo_ref, vo_ref):
        g = own_ref[...].astype(F32)
        for k in range(len(FLIPS)):
            g = g + r_ref[k].astype(F32)
        if transpose:
            g = g.T
        delta, m_new, v_new = _adamw(w_ref[...], g, m_ref[...], v_ref[...])
        g_ref[...] = g
        d_ref[...] = delta
        mo_ref[...] = m_new
        vo_ref[...] = v_new

    if transpose:
        blk = pl.BlockSpec((dm, tr), lambda i, me_ref: (0, i))
    else:
        blk = pl.BlockSpec((tr, dm), lambda i, me_ref: (i, 0))
    first = off // tr
    return pl.pallas_call(
        body, name=name,
        grid_spec=pltpu.PrefetchScalarGridSpec(
            num_scalar_prefetch=1, grid=(per,),
            in_specs=[pl.BlockSpec((tr, dm), lambda i, me_ref: (me_ref[0] * per + i, 0)),
                      pl.BlockSpec((len(FLIPS), tr, dm), lambda i, me_ref: (0, first + i, 0)), blk, blk, blk],
            out_specs=[blk] * 4),
        out_shape=[jax.ShapeDtypeStruct(w.shape, F32)] * 4,
        compiler_params=_params(32),
    )(me, *[_in_hbm(a) for a in (part, recv, w, m, v)])


LOSS_ROW = 15
CONV_A_ROW = 16
CONV_B_ROW = 24


def _adam_small(recv_small, recv_last, small_own, last_own, me, params, d_model):
    n = len(params)
    cw = d_model // N_DEV

    def body(me_ref, r_ref, rc_ref, l_ref, so_ref, soc_ref, lo_ref, *refs):
        ins, loss_ref, outs = refs[:3 * n], refs[3 * n], refs[3 * n + 1:3 * n + 1 + 4 * n]
        g_v, gc_v, last_v = refs[3 * n + 1 + 4 * n:]
        me_pos = me_ref[0]

        def slot(d, own_ref, slots_ref):
            return jnp.where(d == me_pos, own_ref[...], slots_ref[d])

        g, gc, last = slot(0, so_ref, r_ref), slot(0, soc_ref, rc_ref), slot(0, lo_ref, l_ref)
        for d in range(1, N_DEV):
            g, gc, last = g + slot(d, so_ref, r_ref), gc + slot(d, soc_ref, rc_ref), last + slot(d, lo_ref, l_ref)
        g_v[...], gc_v[...], last_v[...] = g, gc, last
        loss_ref[...] = (0.5 / d_model) * jnp.sum(g_v[LOSS_ROW:LOSS_ROW + 1, :], axis=-1, keepdims=True)
        for j, (row0, own_columns, (w, _, _)) in enumerate(params):
            w_ref, m_ref, v_ref = ins[3 * j:3 * j + 3]
            source = gc_v if own_columns else (last_v if row0 == 0 else g_v)
            grad = source[row0:row0 + w.shape[0], :]
            delta, m_new, v_new = _adamw(w_ref[...], grad, m_ref[...], v_ref[...])
            for ref, val in zip(outs[4 * j:4 * j + 4], (grad, delta, m_new, v_new)):
                ref[...] = val

    full = lambda shape: pl.BlockSpec(shape, lambda i, me_ref: (0,) * len(shape))
    stack_rows = recv_small.shape[1]
    flat = [a for _, _, triple in params for a in triple]
    shapes = [w.shape for _, _, (w, _, _) in params for _ in range(4)]
    out = pl.pallas_call(
        body, name="adam_small",
        grid_spec=pltpu.PrefetchScalarGridSpec(
            num_scalar_prefetch=1, grid=(1,),
            in_specs=[full(recv_small.shape),
                      pl.BlockSpec((N_DEV, stack_rows, cw), lambda i, me_ref: (0, 0, me_ref[0])),
                      full(recv_last.shape), full(small_own.shape),
                      pl.BlockSpec((stack_rows, cw), lambda i, me_ref: (0, me_ref[0])),
                      full(last_own.shape)] + [full(a.shape) for a in flat],
            out_specs=[full((1, 1))] + [full(s) for s in shapes],
            scratch_shapes=[pltpu.VMEM((stack_rows, d_model), F32), pltpu.VMEM((stack_rows, cw), F32),
                            pltpu.VMEM(recv_last.shape[1:], F32)]),
        out_shape=[jax.ShapeDtypeStruct((1, 1), F32)] + [jax.ShapeDtypeStruct(s, F32) for s in shapes],
    )(me, *[_in_hbm(a) for a in (recv_small, recv_small, recv_last, small_own, small_own, last_own, *flat)])
    return out[0], [tuple(out[1 + 4 * j:5 + 4 * j]) for j in range(n)]


def _tile(n_tok, want):
    return min(want, n_tok)


def kernel(x, norm1_pre_g, w_in, b_in, conv_a_w, conv_a_b, w_a_out, conv_b_w, conv_b_b, ln_b_g, ln_b_b, w_b_out, w_o, norm1_post_g, norm2_pre_g, w_mlp_in, w_mlp_out, norm2_post_g, loss_target, m_norm1_pre_g, m_w_in, m_b_in, m_conv_a_w, m_conv_a_b, m_w_a_out, m_conv_b_w, m_conv_b_b, m_ln_b_g, m_ln_b_b, m_w_b_out, m_w_o, m_norm1_post_g, m_norm2_pre_g, m_w_mlp_in, m_w_mlp_out, m_norm2_post_g, v_norm1_pre_g, v_w_in, v_b_in, v_conv_a_w, v_conv_a_b, v_w_a_out, v_conv_b_w, v_conv_b_b, v_ln_b_g, v_ln_b_b, v_w_b_out, v_w_o, v_norm1_post_g, v_norm2_pre_g, v_w_mlp_in, v_w_mlp_out, v_norm2_post_g):
    n_tok, dm = x.shape[1], x.shape[2]
    rows, off, slab_rows = _layout(dm)
    cw = dm // N_DEV
    xs = x.reshape(n_tok, dm)
    tgt = loss_target.reshape(n_tok, dm)
    row = lambda vec: vec.reshape(1, -1)
    scattered = lambda group: jax.ShapeDtypeStruct((len(FLIPS), slab_rows[group], dm), BF16)
    tm, tk = min(dm, 1024), _tile(n_tok, 2048)
    me = (4 * lax.axis_index("x") + 2 * lax.axis_index("y") + lax.axis_index("c")).astype(jnp.int32).reshape(1)

    conv_own = jnp.concatenate([conv_a_w, jnp.zeros((SUBLANE - CONV_A, cw), F32), conv_b_w,
                                jnp.zeros((1, cw), F32)], axis=0)
    own_in, land_in = _place_cast([(w_in, True)], me, "place_w_in")
    slab_in, conv_all = _all_gather_two_level([own_in, conv_own], [land_in, None], "gather_w_in")
    conv_full = conv_all.transpose(1, 0, 2).reshape(conv_own.shape[0], dm)
    caw, cbw = conv_full[0:CONV_A], conv_full[SUBLANE:SUBLANE + CONV_B]
    own_abo, land_abo = _place_cast([(w_a_out, False), (w_b_out, False), (w_o, False)], me, "place_abo")
    own_mlp, land_mlp = _place_cast([(w_mlp_in, True), (w_mlp_out, False)], me, "place_mlp")
    ag_abo = _exchange_start([_Part(own_abo, False, slab_rows["abo"], 0, 0)], [land_abo],
                             "gather_abo_start", after=slab_in)
    ag_mlp = _exchange_start([_Part(own_mlp, False, slab_rows["mlp"], 0, 0)], [land_mlp],
                             "gather_mlp_start", after=ag_abo.token)

    proj, p, u, h = _fwd_in(xs, row(norm1_pre_g), row(b_in), slab_in, _tile(n_tok, 512),
                            [ag_abo.token, ag_mlp.token])
    _, (slab_abo,) = _exchange_wait(ag_abo, "gather_abo_wait", after=proj)
    va, vb, ya, yb, qa, sb, merged, mix, x1 = _fwd_mix(
        p, u, proj, xs, caw, row(conv_a_b), cbw, row(conv_b_b), row(ln_b_g), row(ln_b_b), row(norm1_post_g),
        slab_abo, _tile(n_tok, 256))
    _, (slab_mlp,) = _exchange_wait(ag_mlp, "gather_mlp_wait", after=x1)
    f, df1, h2, df2, dmix, dx1, small_mlp = _mlp_fwd_bwd(
        x1, mix, tgt, row(norm1_post_g), row(norm2_pre_g), row(norm2_post_g), slab_mlp, _tile(n_tok, 256))

    rs_mlp = _exchange_start(
        [_Part(_wgrad(df1, h2, "wgrad_mlp_in", tm, tk, BF16), True, rows["w1"], 0, off["w1"]),
         _Part(_wgrad(f, df2, "wgrad_mlp_out", tm, tk, BF16), True, rows["w2"], 0, off["w2"])],
        [scattered("mlp")], "scatter_mlp_start")
    dpa, dya, dyb, dva, dvb, small_mix = _bwd_mix(
        dmix, ya, yb, proj, va, vb, row(ln_b_g), row(ln_b_b), slab_abo, _tile(n_tok, 512), [rs_mlp.token])
    rs_abo = _exchange_start(
        [_Part(_wgrad(qa, dya, "wgrad_a_out", tm, tk, BF16), True, rows["wa"], 0, off["wa"]),
         _Part(_wgrad(sb, dyb, "wgrad_b_out", tm, tk, BF16), True, rows["wb"], 0, off["wb"]),
         _Part(_wgrad(merged, dmix, "wgrad_o", tm, tk, BF16), True, rows["wo"], 0, off["wo"])],
        [scattered("abo")], "scatter_abo_start")
    dpb, small_conv = _bwd_conv(dva, dvb, p, u, proj, caw, cbw, _tile(n_tok, 256), [rs_abo.token])

    zeros = lambda r: jnp.zeros((r, dm), F32)
    small = jnp.concatenate([
        zeros(1),
        small_mix[2:3],
        small_mix[3:4],
        small_mix[0:2],
        small_mlp[2:3],
        small_mlp[1:2],
        small_mlp[0:1],
        small_mix[4:5], small_conv[3:7], small_mix[5:7],
        small_mlp[3:4],
        small_conv[0:CONV_A], zeros(SUBLANE - CONV_A),
        small_conv[8:8 + CONV_B], zeros(1),
    ], axis=0)

    rs_in = _exchange_start(
        [_Part(_wgrad_in(dpa, dpb, h, tk), True, rows["win"], 0, off["win"]),
         _Part(small, False, small.shape[0], 1, 0)],
        [scattered("in"), jax.ShapeDtypeStruct((N_DEV,) + small.shape, F32)], "scatter_in_start")
    grad_x, small_in = _bwd_in(dpa, dpb, xs, dx1, row(norm1_pre_g), slab_in, _tile(n_tok, 512), [rs_in.token])

    tr = min(LANE, rows["wa"])
    (g_w1, g_w2), (recv_mlp,) = _exchange_wait(rs_mlp, "scatter_mlp_wait", after=grad_x)
    (g_wa, g_wb, g_wo), (recv_abo,) = _exchange_wait(rs_abo, "scatter_abo_wait", after=grad_x)
    big = {
        "w_mlp_in": _adam_big(recv_mlp, g_w1, me, off["w1"], rows["w1"], w_mlp_in, m_w_mlp_in, v_w_mlp_in, True,
                              "adam_w_mlp_in", tr),
        "w_mlp_out": _adam_big(recv_mlp, g_w2, me, off["w2"], rows["w2"], w_mlp_out, m_w_mlp_out, v_w_mlp_out,
                               False, "adam_w_mlp_out", tr),
        "w_a_out": _adam_big(recv_abo, g_wa, me, off["wa"], rows["wa"], w_a_out, m_w_a_out, v_w_a_out, False,
                             "adam_w_a_out", tr),
        "w_b_out": _adam_big(recv_abo, g_wb, me, off["wb"], rows["wb"], w_b_out, m_w_b_out, v_w_b_out, False,
                             "adam_w_b_out", tr),
        "w_o": _adam_big(recv_abo, g_wo, me, off["wo"], rows["wo"], w_o, m_w_o, v_w_o, False, "adam_w_o", tr),
    }
    (g_win, small), (recv_in, recv_small) = _exchange_wait(rs_in, "scatter_in_wait", after=big["w_o"][3])
    gather_last = _exchange_start([_Part(small_in, False, SUBLANE, 0, 0)],
                                  [jax.ShapeDtypeStruct((N_DEV, SUBLANE, dm), F32)], "gather_last_start",
                                  after=recv_in)
    big["w_in"] = _adam_big(recv_in, g_win, me, off["win"], rows["win"], w_in, m_w_in, v_w_in, True, "adam_w_in", tr)
    (small_in,), (recv_last,) = _exchange_wait(gather_last, "gather_last_wait", after=big["w_in"][3])

    small_names = ("norm1_pre_g", "conv_a_b", "conv_b_b", "ln_b_g", "ln_b_b", "norm1_post_g", "norm2_pre_g",
                   "norm2_post_g")
    given = dict(
        norm1_pre_g=(norm1_pre_g, m_norm1_pre_g, v_norm1_pre_g), conv_a_b=(conv_a_b, m_conv_a_b, v_conv_a_b),
        conv_b_b=(conv_b_b, m_conv_b_b, v_conv_b_b), ln_b_g=(ln_b_g, m_ln_b_g, v_ln_b_g),
        ln_b_b=(ln_b_b, m_ln_b_b, v_ln_b_b), norm1_post_g=(norm1_post_g, m_norm1_post_g, v_norm1_post_g),
        norm2_pre_g=(norm2_pre_g, m_norm2_pre_g, v_norm2_pre_g),
        norm2_post_g=(norm2_post_g, m_norm2_post_g, v_norm2_post_g))
    params = [(j, False, tuple(row(a) for a in given[name])) for j, name in enumerate(small_names)]
    params.append((SUBLANE, False, tuple(a.reshape(7, dm) for a in (b_in, m_b_in, v_b_in))))
    params.append((CONV_A_ROW, True, (conv_a_w, m_conv_a_w, v_conv_a_w)))
    params.append((CONV_B_ROW, True, (conv_b_w, m_conv_b_w, v_conv_b_w)))
    loss, small_out = _adam_small(recv_small, recv_last, small, small_in, me, params, dm)
    small_leaves = {name: tuple(a.reshape(dm) for a in small_out[j]) for j, name in enumerate(small_names)}
    small_leaves["b_in"] = tuple(a.reshape(7 * dm) for a in small_out[len(small_names)])
    small_leaves["conv_a_w"] = small_out[len(small_names) + 1]
    small_leaves["conv_b_w"] = small_out[len(small_names) + 2]

    order = ("norm1_pre_g", "w_in", "b_in", "conv_a_w", "conv_a_b", "w_a_out", "conv_b_w", "conv_b_b", "ln_b_g",
             "ln_b_b", "w_b_out", "w_o", "norm1_post_g", "norm2_pre_g", "w_mlp_in", "w_mlp_out", "norm2_post_g")
    leaves = [big[name] if name in big else small_leaves[name] for name in order]
    grads, deltas, new_m, new_v = zip(*leaves)
    return (loss.reshape(()), grad_x.reshape(x.shape), *grads, *deltas, *new_m, *new_v)
```

```python
from typing import NamedTuple

import jax
import jax.numpy as jnp
from jax import lax
from jax.experimental import pallas as pl
from jax.experimental.pallas import tpu as pltpu

F32 = jnp.float32
BF16 = jnp.bfloat16

RMS_EPS = 1e-6
LN_EPS = 1e-5
ADAM_LR = 0.001
ADAM_B1 = 0.9
ADAM_B2 = 0.999
ADAM_EPS = 1e-08
ADAM_WD = 0.01
ADAM_STEP = 10

N_DEV = 8
CONV_A = 3
CONV_B = 31
LANE = 128
SUBLANE = 8
HALO = 16
FWD_CONV_ROWS = 32
BWD_CONV_ROWS = 64
MIB = 1 << 20
FLIPS = ((0, 0, 1), (0, 1, 0), (1, 0, 0), (0, 1, 1), (1, 0, 1), (1, 1, 0), (1, 1, 1))
MESH = pl.DeviceIdType.MESH


def _layout(d_model):
    e = d_model // N_DEV
    rows = {"win": 7 * e, "w1": 4 * e, "w2": 4 * e, "wa": e, "wb": e, "wo": e}
    off = {"win": 0, "w1": 0, "w2": 4 * e, "wa": 0, "wb": e, "wo": 2 * e}
    return rows, off, {"in": 7 * e, "mlp": 8 * e, "abo": 3 * e}


def _after(body, deps):
    def ordered(*refs):
        return body(*refs[len(deps):])
    return ordered


def _params(vmem_mib):
    return pltpu.CompilerParams(dimension_semantics=("arbitrary",), vmem_limit_bytes=vmem_mib * MIB)


def _whole():
    return pl.BlockSpec(memory_space=pltpu.VMEM)


def _hbm():
    return pl.BlockSpec(memory_space=pl.ANY)


def _in_hbm(a):
    return pltpu.with_memory_space_constraint(a, pltpu.HBM)


def _rows(ts, width, col=0):
    return pl.BlockSpec((ts, width), lambda i: (i, col))


def _halo_specs(ts, width, n_rows):
    per = ts // HALO
    last = n_rows // HALO - 1
    return [
        pl.BlockSpec((ts, width), lambda i: (i, 0)),
        pl.BlockSpec((HALO, width), lambda i: (jnp.maximum(i * per - 1, 0), 0)),
        pl.BlockSpec((HALO, width), lambda i: (jnp.minimum((i + 1) * per, last), 0)),
    ]


def _dot(a, b):
    return jnp.dot(a, b, preferred_element_type=F32)


def _dot_nt(a, b):
    return lax.dot_general(a, b, (((1,), (1,)), ((), ())), preferred_element_type=F32)


def _dot_tn(a, b):
    return lax.dot_general(a, b, (((0,), (0,)), ((), ())), preferred_element_type=F32)


def _rms(u):
    return lax.rsqrt(jnp.mean(u * u, axis=-1, keepdims=True) + RMS_EPS)


def _rms_bwd(dz, u, r, g):
    dzg = dz * g
    return r * dzg - u * (r * r * r) * jnp.mean(dzg * u, axis=-1, keepdims=True)


def _colsum(v):
    return jnp.sum(v, axis=0, keepdims=True)


def _sigmoid(v):
    return jax.nn.sigmoid(v)


def _weight_copies(slab_ref, off, rows, dst_ref, sems, first_sem):
    return [pltpu.make_async_copy(slab_ref.at[d, pl.ds(off, rows), :], dst_ref.at[pl.ds(d * rows, rows), :],
                                  sems.at[first_sem + d]) for d in range(N_DEV)]


def _on_first_step(copies, method):
    @pl.when(pl.program_id(0) == 0)
    def _():
        for cp in copies:
            getattr(cp, method)()


def _with_halos(main_ref, prev_ref, next_ref, i, n_steps):
    return (main_ref, jnp.where(i > 0, prev_ref[...], 0.0), jnp.where(i < n_steps - 1, next_ref[...], 0.0))


def _broadcast_taps(w_ref, wb_ref, n_taps):
    for k in range(n_taps):
        wb_ref[k] = jnp.broadcast_to(w_ref[k:k + 1, :], wb_ref.shape[1:])


def _conv_tile(tile, wb_ref, starts, ts, width, emit, rolled_ref, rows=FWD_CONV_ROWS):
    main_ref, prev, nxt = tile
    span = ts + 2 * HALO
    nv = rows // SUBLANE
    for cb in range(width // LANE):
        lanes = slice(cb * LANE, (cb + 1) * LANE)
        slot = cb % 2
        window = jnp.concatenate([prev[:, lanes], main_ref[:, lanes], nxt[:, lanes]], axis=0)
        for b in sorted({st % SUBLANE for st in starts}):
            rolled_ref[slot, b] = window if b == 0 else pltpu.roll(window, span - b, axis=0)
        for r0 in range(0, ts, rows):
            acc = jnp.zeros((nv, SUBLANE, LANE), F32)
            for k, st in enumerate(starts):
                shifted = rolled_ref[slot, st % SUBLANE, pl.ds(r0 + st - st % SUBLANE, rows), :]
                acc = acc + shifted.reshape(nv, SUBLANE, LANE) * wb_ref[k, :, lanes][None]
            emit(r0, pl.ds(cb * LANE, LANE), acc.reshape(rows, LANE))


def _window(tile, r0, cb, ts, rows):
    main_ref, prev, nxt = tile
    lanes = slice(cb * LANE, (cb + 1) * LANE)
    lo, hi = max(r0 - HALO, 0), min(r0 + rows + HALO, ts)
    pieces = [prev[:, lanes]] if r0 - HALO < 0 else []
    pieces.append(main_ref[lo:hi, lanes])
    if r0 + rows + HALO > ts:
        pieces.append(nxt[:, lanes])
    return pieces[0] if len(pieces) == 1 else jnp.concatenate(pieces, axis=0)


def _phases(starts):
    groups = {}
    for k, st in enumerate(starts):
        groups.setdefault(st % SUBLANE, []).append((k, st // SUBLANE))
    return sorted(groups.items())


def _shifted(blk, b):
    n = blk.shape[0]
    rolled = blk if b == 0 else pltpu.roll(blk, n - b, axis=0)
    return rolled.reshape(n // SUBLANE, SUBLANE, blk.shape[1])


def _conv_bwd_tile(dv_tile, u_ref, wb_ref, acc_ref, n_taps, ts, width, emit, rows=BWD_CONV_ROWS):
    groups = _phases(_bwd_starts(n_taps))
    nv = rows // SUBLANE
    for r0 in range(0, ts, rows):
        for cb in range(width // LANE):
            lanes = pl.ds(cb * LANE, LANE)
            blk = _window(dv_tile, r0, cb, ts, rows)
            u = u_ref[pl.ds(r0, rows), lanes].reshape(nv, SUBLANE, LANE)
            du = jnp.zeros((nv, SUBLANE, LANE), F32)
            for b, taps in groups:
                sh = _shifted(blk, b)
                for k, m in taps:
                    du = du + sh[m:m + nv] * wb_ref[k, :, lanes][None]
                    acc_ref[k, :, lanes] += jnp.sum(sh[m:m + nv] * u, axis=0)
            emit(r0, lanes, du.reshape(rows, LANE))


def _fwd_starts(n_taps):
    pad = (n_taps - 1) // 2
    return [HALO - pad + k for k in range(n_taps)]


def _bwd_starts(n_taps):
    pad = (n_taps - 1) // 2
    return [HALO + pad - k for k in range(n_taps)]


def _peer(x, y, c, flip):
    fx, fy, fc = flip
    return (1 - x if fx else x, 1 - y if fy else y, 1 - c if fc else c)


def _place_cast(pieces, me, name):
    n = len(pieces)
    counts = [a.shape[1] if t else a.shape[0] for a, t in pieces]
    width = pieces[0][0].shape[0] if pieces[0][1] else pieces[0][0].shape[1]
    total = sum(counts)

    def body(me_ref, *refs):
        ins, own_ref, land_ref = refs[:n], refs[n], refs[n + 1]
        first = 0
        for (a, transpose), in_ref, count in zip(pieces, ins, counts):
            block = (in_ref[...].T if transpose else in_ref[...]).astype(BF16)
            own_ref[first:first + count, :] = block
            land_ref[first:first + count, :] = block
            first += count

    return pl.pallas_call(
        body, name=name,
        grid_spec=pltpu.PrefetchScalarGridSpec(
            num_scalar_prefetch=1, grid=(1,),
            in_specs=[pl.BlockSpec(a.shape, lambda i, me_ref: (0, 0)) for a, _ in pieces],
            out_specs=[pl.BlockSpec((total, width), lambda i, me_ref: (0, 0)),
                       pl.BlockSpec((None, total, width), lambda i, me_ref: (me_ref[0], 0, 0))]),
        out_shape=[pltpu.HBM((total, width), BF16), pltpu.HBM((N_DEV, total, width), BF16)],
        compiler_params=_params(32),
    )(me, *[_in_hbm(a) for a, _ in pieces])


def _all_gather_two_level(shards, placed, name):
    n = len(shards)
    given = [j for j in range(n) if placed[j] is not None]

    def body(*refs):
        ins, outs = refs[:n], refs[n + len(given):2 * n + len(given)]
        send_sems, recv_sems, local_sems = refs[2 * n + len(given):]
        x, y, c = lax.axis_index("x"), lax.axis_index("y"), lax.axis_index("c")
        me, sibling = (x, y, c), (x, y, 1 - c)
        chips = [(1 - x, y), (x, 1 - y), (1 - x, 1 - y)]

        def slot(j, dev):
            return outs[j].at[4 * dev[0] + 2 * dev[1] + dev[2]]

        def copy(k, j, block, to, src=None):
            return pltpu.make_async_remote_copy(
                src_ref=slot(j, block) if src is None else src, dst_ref=slot(j, block),
                send_sem=send_sems.at[k * n + j], recv_sem=recv_sems.at[k * n + j], device_id=to, device_id_type=MESH)

        local = [pltpu.make_async_copy(ins[j], slot(j, me), local_sems.at[j]) for j in range(n) if j not in given]
        for cp in local:
            cp.start()
        first = [copy(0, j, me, sibling, src=ins[j]) for j in range(n)]
        first += [copy(1 + t, j, me, (*chip, c), src=ins[j]) for t, chip in enumerate(chips) for j in range(n)]
        for cp in first:
            cp.start()
        passed = []
        for t, chip in enumerate(chips):
            for j in range(n):
                copy(1 + t, j, (*chip, c), me).wait_recv()
                passed.append(copy(4 + t, j, (*chip, c), sibling))
                passed[-1].start()
        for j in range(n):
            copy(0, j, sibling, me).wait_recv()
        for t, chip in enumerate(chips):
            for j in range(n):
                copy(4 + t, j, (*chip, 1 - c), me).wait_recv()
        for cp in first + passed:
            cp.wait_send()
        for cp in local:
            cp.wait()

    return pl.pallas_call(
        body, name=name,
        out_shape=[jax.ShapeDtypeStruct((N_DEV,) + s.shape, s.dtype) for s in shards],
        in_specs=[_hbm()] * (n + len(given)), out_specs=[_hbm()] * n,
        input_output_aliases={n + i: j for i, j in enumerate(given)},
        scratch_shapes=[pltpu.SemaphoreType.DMA((7 * n,)), pltpu.SemaphoreType.DMA((7 * n,)),
                        pltpu.SemaphoreType.DMA((n,))],
    )(*shards, *[placed[j] for j in given])


class _Part(NamedTuple):
    src: jax.Array
    scatter: bool
    rows: int
    land: int
    off: int


class _Started(NamedTuple):
    send_sems: jax.Array
    recv_sems: jax.Array
    thru: tuple
    token: jax.Array
    parts: tuple


def _exchange_copies(srcs, lands, send_sems, recv_sems, parts):
    n = len(parts)
    x, y, c = lax.axis_index("x"), lax.axis_index("y"), lax.axis_index("c")
    me = 4 * x + 2 * y + c

    def block(j, dev):
        p = parts[j]
        return srcs[j].at[pl.ds(pl.multiple_of(dev * p.rows, SUBLANE), p.rows), :] if p.scatter else srcs[j]

    def slot(j, index):
        p = parts[j]
        return lands[p.land].at[index, pl.ds(p.off, p.rows), :]

    sends, recvs = [], []
    for k, flip in enumerate(FLIPS):
        px, py, pc = _peer(x, y, c, flip)
        peer = 4 * px + 2 * py + pc
        for j in range(n):
            sems = dict(send_sem=send_sems.at[k * n + j], recv_sem=recv_sems.at[k * n + j],
                        device_id=(px, py, pc), device_id_type=MESH)
            to, got = (k, k) if parts[j].scatter else (me, peer)
            sends.append(pltpu.make_async_remote_copy(src_ref=block(j, peer), dst_ref=slot(j, to), **sems))
            recvs.append(pltpu.make_async_remote_copy(src_ref=block(j, peer), dst_ref=slot(j, got), **sems))
    return sends, recvs


def _exchange_start(parts, lands, name, after=None):
    n, nl = len(parts), len(lands)
    n_in = n + nl + (after is not None)

    def body(*refs):
        srcs, land_refs = refs[:n], refs[n:n + nl]
        send_sems, recv_sems = refs[n_in], refs[n_in + 1]
        token = refs[n_in + 2 + n + nl]
        sends, _ = _exchange_copies(srcs, land_refs, send_sems, recv_sems, parts)
        for cp in sends:
            cp.start()
        token[...] = jnp.zeros_like(token)

    hbm = pl.BlockSpec(memory_space=pltpu.HBM)
    sem = pl.BlockSpec(memory_space=pltpu.SEMAPHORE)
    fresh = lambda s: lax.empty(s.shape, s.dtype) if isinstance(s, jax.ShapeDtypeStruct) else s
    args = [pltpu.with_memory_space_constraint(p.src, pltpu.HBM) for p in parts]
    args += [pltpu.with_memory_space_constraint(fresh(s), pltpu.HBM) for s in lands]
    args += [] if after is None else [after]
    out = pl.pallas_call(
        body, name=name,
        out_shape=(pltpu.SemaphoreType.DMA((7 * n,)), pltpu.SemaphoreType.DMA((7 * n,)),
                   *[pltpu.HBM(a.shape, a.dtype) for a in args[:n + nl]], jax.ShapeDtypeStruct((SUBLANE, LANE), F32)),
        in_specs=[hbm] * (n + nl) + [_hbm()] * (after is not None),
        out_specs=(sem, sem, *[hbm] * (n + nl), _whole()),
        input_output_aliases={j: 2 + j for j in range(n + nl)},
        compiler_params=pltpu.CompilerParams(has_side_effects=pltpu.SideEffectType.DATAFLOW_SIDE_EFFECTING),
    )(*args)
    return _Started(out[0], out[1], tuple(out[2:2 + n + nl]), out[2 + n + nl], tuple(parts))


def _exchange_wait(started, name, after):
    parts = started.parts
    n, nl = len(parts), len(started.thru) - len(parts)

    def body(*refs):
        srcs, land_refs = refs[:n], refs[n:n + nl]
        send_sems, recv_sems = refs[n + nl], refs[n + nl + 1]
        sends, recvs = _exchange_copies(srcs, land_refs, send_sems, recv_sems, parts)
        for cp in sends:
            cp.wait_send()
        for cp in recvs:
            cp.wait_recv()

    hbm = pl.BlockSpec(memory_space=pltpu.HBM)
    sem = pl.BlockSpec(memory_space=pltpu.SEMAPHORE)
    out = pl.pallas_call(
        body, name=name,
        out_shape=tuple(pltpu.HBM(a.shape, a.dtype) for a in started.thru),
        in_specs=[hbm] * (n + nl) + [sem, sem, _hbm()], out_specs=tuple([hbm] * (n + nl)),
        input_output_aliases={j: j for j in range(n + nl)},
        compiler_params=pltpu.CompilerParams(has_side_effects=pltpu.SideEffectType.DATAFLOW_SIDE_EFFECTING),
    )(*started.thru, started.send_sems, started.recv_sems, after)
    return list(out[:n]), list(out[n:])


def _fwd_in(x, g1, b_in, slab, ts, deps):
    n_tok, dm = x.shape
    rows, off, _ = _layout(dm)
    width = 7 * dm

    def body(x_ref, g1_ref, b_ref, slab_ref, proj_ref, p_ref, u_ref, h_ref, w_v, sems):
        copies = _weight_copies(slab_ref, off["win"], rows["win"], w_v, sems, 0)
        _on_first_step(copies, "start")
        _on_first_step(copies, "wait")
        xv = x_ref[...]
        h = (xv * _rms(xv) * g1_ref[...]).astype(BF16)
        h_ref[...] = h
        cols = []
        for j in range(7):
            pj = _dot_nt(h, w_v[pl.ds(j * dm, dm), :]) + b_ref[:, j * dm:(j + 1) * dm]
            proj_ref[:, j * dm:(j + 1) * dm] = pj.astype(proj_ref.dtype)
            if 1 <= j <= 4:
                cols.append(pj)
            if j == 2:
                p_ref[...] = cols[0] * cols[1]
            if j == 4:
                u_ref[...] = cols[2] * _sigmoid(cols[3])

    return pl.pallas_call(
        _after(body, deps), name="fwd_in", grid=(n_tok // ts,),
        in_specs=[_whole()] * len(deps) + [_rows(ts, dm), _whole(), _whole(), _hbm()],
        out_specs=[_rows(ts, width), _rows(ts, dm), _rows(ts, dm), _rows(ts, dm)],
        out_shape=[jax.ShapeDtypeStruct((n_tok, width), BF16), jax.ShapeDtypeStruct((n_tok, dm), F32),
                   jax.ShapeDtypeStruct((n_tok, dm), F32), jax.ShapeDtypeStruct((n_tok, dm), BF16)],
        scratch_shapes=[pltpu.VMEM((width, dm), BF16), pltpu.SemaphoreType.DMA((N_DEV,))],
        compiler_params=_params(56),
    )(*deps, x, g1, b_in, slab)


def _fwd_mix(p, u, proj, x, caw, cab, cbw, cbb, lng, lnb, g1post, slab, ts):
    n_tok, dm = x.shape
    rows, off, _ = _layout(dm)
    n_steps = n_tok // ts

    def body(p_ref, p_prev, p_next, u_ref, u_prev, u_next, bg_ref, za_ref, zb_ref, x_ref,
             caw_ref, cab_ref, cbw_ref, cbb_ref, lng_ref, lnb_ref, g1p_ref, slab_ref,
             va_ref, vb_ref, ya_ref, yb_ref, qa_ref, sb_ref, mg_ref, mix_ref, x1_ref,
             wa_v, wb_v, wo_v, tap_a, tap_b, sems, rolled_a, rolled_b):
        i = pl.program_id(0)
        copies = (_weight_copies(slab_ref, off["wa"], rows["wa"], wa_v, sems, 0)
                  + _weight_copies(slab_ref, off["wb"], rows["wb"], wb_v, sems, N_DEV)
                  + _weight_copies(slab_ref, off["wo"], rows["wo"], wo_v, sems, 2 * N_DEV))
        _on_first_step(copies, "start")

        @pl.when(i == 0)
        def _():
            _broadcast_taps(caw_ref, tap_a, CONV_A)
            _broadcast_taps(cbw_ref, tap_b, CONV_B)

        def emit_a(r0, lanes, acc):
            va_ref[pl.ds(r0, acc.shape[0]), lanes] = acc + cab_ref[:, lanes]

        def emit_b(r0, lanes, acc):
            vb_ref[pl.ds(r0, acc.shape[0]), lanes] = acc + cbb_ref[:, lanes]

        _conv_tile(_with_halos(p_ref, p_prev, p_next, i, n_steps), tap_a, _fwd_starts(CONV_A), ts, dm, emit_a,
                   rolled_a)
        _conv_tile(_with_halos(u_ref, u_prev, u_next, i, n_steps), tap_b, _fwd_starts(CONV_B), ts, dm, emit_b,
                   rolled_b)
        _on_first_step(copies, "wait")

        qa = (bg_ref[...].astype(F32) * va_ref[...]).astype(BF16)
        qa_ref[...] = qa
        ya = _dot(qa, wa_v[...])
        vb = vb_ref[...]
        xc = vb - jnp.mean(vb, axis=-1, keepdims=True)
        rstd = lax.rsqrt(jnp.mean(xc * xc, axis=-1, keepdims=True) + LN_EPS)
        ln = xc * rstd * lng_ref[...] + lnb_ref[...]
        sb = (ln * _sigmoid(ln)).astype(BF16)
        sb_ref[...] = sb
        yb = _dot(sb, wb_v[...])
        ya_ref[...] = ya.astype(BF16)
        yb_ref[...] = yb.astype(BF16)
        merged = (_sigmoid(za_ref[...].astype(F32)) * ya + _sigmoid(zb_ref[...].astype(F32)) * yb).astype(BF16)
        mg_ref[...] = merged
        mix = _dot(merged, wo_v[...])
        mix_ref[...] = mix
        x1_ref[...] = x_ref[...] + mix * _rms(mix) * g1p_ref[...]

    tok = lambda dt: jax.ShapeDtypeStruct((n_tok, dm), dt)
    return pl.pallas_call(
        body, name="fwd_mix", grid=(n_steps,),
        in_specs=(_halo_specs(ts, dm, n_tok) + _halo_specs(ts, dm, n_tok)
                  + [_rows(ts, dm, 0), _rows(ts, dm, 5), _rows(ts, dm, 6), _rows(ts, dm)]
                  + [_whole()] * 7 + [_hbm()]),
        out_specs=[_rows(ts, dm)] * 9,
        out_shape=[tok(F32), tok(F32), tok(BF16), tok(BF16), tok(BF16), tok(BF16), tok(BF16), tok(F32), tok(F32)],
        scratch_shapes=[pltpu.VMEM((dm, dm), BF16), pltpu.VMEM((dm, dm), BF16), pltpu.VMEM((dm, dm), BF16),
                        pltpu.VMEM((CONV_A, SUBLANE, dm), F32), pltpu.VMEM((CONV_B, SUBLANE, dm), F32),
                        pltpu.SemaphoreType.DMA((3 * N_DEV,)),
                        pltpu.VMEM((2, SUBLANE, ts + 2 * HALO, LANE), F32),
                        pltpu.VMEM((2, SUBLANE, ts + 2 * HALO, LANE), F32)],
        compiler_params=_params(48),
    )(p, p, p, u, u, u, proj, proj, proj, x, caw, cab, cbw, cbb, lng, lnb, g1post, slab)


def _mlp_fwd_bwd(x1, mix, tgt, g1post, g2pre, g2post, slab, ts):
    n_tok, dm = x1.shape
    rows, off, _ = _layout(dm)
    ff = 4 * dm

    def body(x1_ref, mix_ref, t_ref, g1p_ref, g2pre_ref, g2post_ref, slab_ref,
             f_ref, df1_ref, h2_ref, df2_ref, dmix_ref, dx1_ref, small_ref, w1_v, w2_v, relu_v, sems):
        w1_copies = _weight_copies(slab_ref, off["w1"], rows["w1"], w1_v, sems, 0)
        w2_copies = _weight_copies(slab_ref, off["w2"], rows["w2"], w2_v, sems, N_DEV)
        _on_first_step(w1_copies + w2_copies, "start")

        @pl.when(pl.program_id(0) == 0)
        def _():
            small_ref[...] = jnp.zeros_like(small_ref)

        _on_first_step(w1_copies + w2_copies, "wait")
        x1v = x1_ref[...]
        r3 = _rms(x1v)
        g2pre = g2pre_ref[...]
        h2 = (x1v * r3 * g2pre).astype(BF16)
        h2_ref[...] = h2
        for c in range(4):
            blk = pl.ds(c * dm, dm)
            relu = jnp.maximum(_dot_nt(h2, w1_v[blk, :]), 0.0)
            relu_v[:, c * dm:(c + 1) * dm] = relu
            f_ref[:, c * dm:(c + 1) * dm] = (relu * relu).astype(BF16)
        f2 = _dot(f_ref[...], w2_v[...])
        r4 = _rms(f2)
        g2post = g2post_ref[...]
        err = x1v + f2 * r4 * g2post - t_ref[...]
        dy = err * (1.0 / dm)
        small_ref[3:4, :] += _colsum(err * err)
        small_ref[0:1, :] += _colsum(dy * f2 * r4)
        df2 = _rms_bwd(dy, f2, r4, g2post).astype(BF16)
        df2_ref[...] = df2
        for c in range(4):
            blk = pl.ds(c * dm, dm)
            df1 = (_dot_nt(df2, w2_v[blk, :]) * (2.0 * relu_v[:, c * dm:(c + 1) * dm])).astype(BF16)
            df1_ref[:, c * dm:(c + 1) * dm] = df1
        dh2 = _dot(df1_ref[...], w1_v[...])
        small_ref[1:2, :] += _colsum(dh2 * x1v * r3)
        dx1 = dy + _rms_bwd(dh2, x1v, r3, g2pre)
        dx1_ref[...] = dx1
        mixv = mix_ref[...]
        r2 = _rms(mixv)
        small_ref[2:3, :] += _colsum(dx1 * mixv * r2)
        dmix_ref[...] = _rms_bwd(dx1, mixv, r2, g1p_ref[...]).astype(BF16)

    tok = lambda w, dt: jax.ShapeDtypeStruct((n_tok, w), dt)
    return pl.pallas_call(
        body, name="mlp_fwd_bwd", grid=(n_tok // ts,),
        in_specs=[_rows(ts, dm)] * 3 + [_whole()] * 3 + [_hbm()],
        out_specs=[_rows(ts, ff), _rows(ts, ff), _rows(ts, dm), _rows(ts, dm), _rows(ts, dm), _rows(ts, dm),
                   pl.BlockSpec((SUBLANE, dm), lambda i: (0, 0))],
        out_shape=[tok(ff, BF16), tok(ff, BF16), tok(dm, BF16), tok(dm, BF16), tok(dm, BF16), tok(dm, F32),
                   jax.ShapeDtypeStruct((SUBLANE, dm), F32)],
        scratch_shapes=[pltpu.VMEM((ff, dm), BF16), pltpu.VMEM((ff, dm), BF16), pltpu.VMEM((ts, ff), F32),
                        pltpu.SemaphoreType.DMA((2 * N_DEV,))],
        compiler_params=_params(56),
    )(x1, mix, tgt, g1post, g2pre, g2post, slab)


def _bwd_mix(dmix, ya, yb, proj, va, vb, lng, lnb, slab, ts, deps):
    n_tok, dm = dmix.shape
    rows, off, _ = _layout(dm)
    n_steps = n_tok // ts

    def body(dmix_ref, ya_ref, yb_ref, bg_ref, za_ref, zb_ref, va_ref, vb_ref, lng_ref, lnb_ref, slab_ref,
             dpa_ref, dya_ref, dyb_ref, dva_ref, dvb_ref, small_ref, wa_v, wb_v, wo_v, sems):
        wo_copies = _weight_copies(slab_ref, off["wo"], rows["wo"], wo_v, sems, 2 * N_DEV)
        ab_copies = (_weight_copies(slab_ref, off["wa"], rows["wa"], wa_v, sems, 0)
                     + _weight_copies(slab_ref, off["wb"], rows["wb"], wb_v, sems, N_DEV))
        _on_first_step(wo_copies + ab_copies, "start")

        @pl.when(pl.program_id(0) == 0)
        def _():
            small_ref[...] = jnp.zeros_like(small_ref)

        _on_first_step(wo_copies, "wait")
        dmerged = _dot_nt(dmix_ref[...], wo_v[...])
        _on_first_step(ab_copies, "wait")
        sa = _sigmoid(za_ref[...].astype(F32))
        sg = _sigmoid(zb_ref[...].astype(F32))
        dza = dmerged * ya_ref[...].astype(F32) * sa * (1.0 - sa)
        dzb = dmerged * yb_ref[...].astype(F32) * sg * (1.0 - sg)
        dpa_ref[:, dm:2 * dm] = dza.astype(BF16)
        dpa_ref[:, 2 * dm:3 * dm] = dzb.astype(BF16)
        small_ref[5:6, :] += _colsum(dza)
        small_ref[6:7, :] += _colsum(dzb)

        dya = (dmerged * sa).astype(BF16)
        dya_ref[...] = dya
        dqa = _dot_nt(dya, wa_v[...])
        dbg = dqa * va_ref[...]
        dpa_ref[:, 0:dm] = dbg.astype(BF16)
        small_ref[4:5, :] += _colsum(dbg)
        dva = dqa * bg_ref[...].astype(F32)
        dva_ref[...] = dva
        small_ref[2:3, :] += _colsum(dva)

        dyb = (dmerged * sg).astype(BF16)
        dyb_ref[...] = dyb
        dsb = _dot_nt(dyb, wb_v[...])
        vb = vb_ref[...]
        xc = vb - jnp.mean(vb, axis=-1, keepdims=True)
        rstd = lax.rsqrt(jnp.mean(xc * xc, axis=-1, keepdims=True) + LN_EPS)
        nrm = xc * rstd
        lng_v = lng_ref[...]
        ln = nrm * lng_v + lnb_ref[...]
        sl = _sigmoid(ln)
        dln = dsb * (sl * (1.0 + ln * (1.0 - sl)))
        small_ref[0:1, :] += _colsum(dln * nrm)
        small_ref[1:2, :] += _colsum(dln)
        dn = dln * lng_v
        dvb = rstd * (dn - jnp.mean(dn, axis=-1, keepdims=True)
                      - nrm * jnp.mean(dn * nrm, axis=-1, keepdims=True))
        dvb_ref[...] = dvb
        small_ref[3:4, :] += _colsum(dvb)

    tok = lambda w, dt: jax.ShapeDtypeStruct((n_tok, w), dt)
    return pl.pallas_call(
        _after(body, deps), name="bwd_mix", grid=(n_steps,),
        in_specs=([_whole()] * len(deps) + [_rows(ts, dm)] * 3
                  + [_rows(ts, dm, 0), _rows(ts, dm, 5), _rows(ts, dm, 6)]
                  + [_rows(ts, dm)] * 2 + [_whole()] * 2 + [_hbm()]),
        out_specs=[_rows(ts, 3 * dm), _rows(ts, dm), _rows(ts, dm), _rows(ts, dm), _rows(ts, dm),
                   pl.BlockSpec((SUBLANE, dm), lambda i: (0, 0))],
        out_shape=[tok(3 * dm, BF16), tok(dm, BF16), tok(dm, BF16), tok(dm, F32), tok(dm, F32),
                   jax.ShapeDtypeStruct((SUBLANE, dm), F32)],
        scratch_shapes=[pltpu.VMEM((dm, dm), BF16), pltpu.VMEM((dm, dm), BF16), pltpu.VMEM((dm, dm), BF16),
                        pltpu.SemaphoreType.DMA((3 * N_DEV,))],
        compiler_params=_params(56),
    )(*deps, dmix, ya, yb, proj, proj, proj, va, vb, lng, lnb, slab)


def _bwd_conv(dva, dvb, p, u, proj, caw, cbw, ts, deps):
    n_tok, dm = dva.shape
    n_steps = n_tok // ts
    small_rows = 40

    def body(dva_ref, dva_prev, dva_next, dvb_ref, dvb_prev, dvb_next, p_ref, u_ref,
             cg_ref, ha_ref, a_ref, g_ref, caw_ref, cbw_ref,
             dproj_ref, small_ref,
             dp_v, du_v, tap_a, tap_b, gwa_v, gwb_v):
        i = pl.program_id(0)

        @pl.when(i == 0)
        def _():
            _broadcast_taps(caw_ref, tap_a, CONV_A)
            _broadcast_taps(cbw_ref, tap_b, CONV_B)
            small_ref[...] = jnp.zeros_like(small_ref)
            gwa_v[...] = jnp.zeros_like(gwa_v)
            gwb_v[...] = jnp.zeros_like(gwb_v)

        def emit_dp(r0, lanes, acc):
            dp_v[pl.ds(r0, acc.shape[0]), lanes] = acc

        def emit_du(r0, lanes, acc):
            du_v[pl.ds(r0, acc.shape[0]), lanes] = acc

        _conv_bwd_tile(_with_halos(dva_ref, dva_prev, dva_next, i, n_steps), p_ref, tap_a, gwa_v, CONV_A, ts, dm,
                       emit_dp)
        _conv_bwd_tile(_with_halos(dvb_ref, dvb_prev, dvb_next, i, n_steps), u_ref, tap_b, gwb_v, CONV_B, ts, dm,
                       emit_du)

        dp = dp_v[...]
        dcg = dp * ha_ref[...].astype(F32)
        dha = dp * cg_ref[...].astype(F32)
        du = du_v[...]
        sg = _sigmoid(g_ref[...].astype(F32))
        da = du * sg
        dg = du * a_ref[...].astype(F32) * sg * (1.0 - sg)
        dproj_ref[:, 0:dm] = dcg.astype(BF16)
        dproj_ref[:, dm:2 * dm] = dha.astype(BF16)
        dproj_ref[:, 2 * dm:3 * dm] = da.astype(BF16)
        dproj_ref[:, 3 * dm:4 * dm] = dg.astype(BF16)
        small_ref[3:4, :] += _colsum(dcg)
        small_ref[4:5, :] += _colsum(dha)
        small_ref[5:6, :] += _colsum(da)
        small_ref[6:7, :] += _colsum(dg)

        @pl.when(i == n_steps - 1)
        def _():
            for k in range(CONV_A):
                small_ref[k:k + 1, :] = _colsum(gwa_v[k])
            for k in range(CONV_B):
                small_ref[SUBLANE + k:SUBLANE + k + 1, :] = _colsum(gwb_v[k])

    return pl.pallas_call(
        _after(body, deps), name="bwd_conv", grid=(n_steps,),
        in_specs=([_whole()] * len(deps) + _halo_specs(ts, dm, n_tok) * 2 + [_rows(ts, dm)] * 2
                  + [_rows(ts, dm, 1), _rows(ts, dm, 2), _rows(ts, dm, 3), _rows(ts, dm, 4)]
                  + [_whole()] * 2),
        out_specs=[_rows(ts, 4 * dm), pl.BlockSpec((small_rows, dm), lambda i: (0, 0))],
        out_shape=[jax.ShapeDtypeStruct((n_tok, 4 * dm), BF16), jax.ShapeDtypeStruct((small_rows, dm), F32)],
        scratch_shapes=[pltpu.VMEM((ts, dm), F32), pltpu.VMEM((ts, dm), F32),
                        pltpu.VMEM((CONV_A, SUBLANE, dm), F32), pltpu.VMEM((CONV_B, SUBLANE, dm), F32),
                        pltpu.VMEM((CONV_A, SUBLANE, dm), F32), pltpu.VMEM((CONV_B, SUBLANE, dm), F32)],
        compiler_params=_params(48),
    )(*deps, dva, dva, dva, dvb, dvb, dvb, p, u, proj, proj, proj, proj, caw, cbw)


def _bwd_in(dpa, dpb, x, dx1, g1, slab, ts, deps):
    n_tok, dm = x.shape
    rows, off, _ = _layout(dm)
    width = 7 * dm

    def body(dpa_ref, dpb_ref, x_ref, dx1_ref, g1_ref, slab_ref, gx_ref, small_ref, w_v, sems):
        copies = _weight_copies(slab_ref, off["win"], rows["win"], w_v, sems, 0)
        _on_first_step(copies, "start")

        @pl.when(pl.program_id(0) == 0)
        def _():
            small_ref[...] = jnp.zeros_like(small_ref)

        _on_first_step(copies, "wait")
        dh = (_dot(dpa_ref[:, 0:dm], w_v[0:dm, :]) + _dot(dpb_ref[...], w_v[dm:5 * dm, :])
              + _dot(dpa_ref[:, dm:3 * dm], w_v[5 * dm:7 * dm, :]))
        xv = x_ref[...]
        r1 = _rms(xv)
        small_ref[0:1, :] += _colsum(dh * xv * r1)
        gx_ref[...] = dx1_ref[...] + _rms_bwd(dh, xv, r1, g1_ref[...])

    return pl.pallas_call(
        _after(body, deps), name="bwd_in", grid=(n_tok // ts,),
        in_specs=[_whole()] * len(deps) + [_rows(ts, 3 * dm), _rows(ts, 4 * dm), _rows(ts, dm), _rows(ts, dm),
                                           _whole(), _hbm()],
        out_specs=[_rows(ts, dm), pl.BlockSpec((SUBLANE, dm), lambda i: (0, 0))],
        out_shape=[jax.ShapeDtypeStruct((n_tok, dm), F32), jax.ShapeDtypeStruct((SUBLANE, dm), F32)],
        scratch_shapes=[pltpu.VMEM((width, dm), BF16), pltpu.SemaphoreType.DMA((N_DEV,))],
        compiler_params=_params(56),
    )(*deps, dpa, dpb, x, dx1, g1, slab)


def _wgrad(a, b, name, tm, tk, out_dtype):
    n_tok, m = a.shape
    n = b.shape[1]
    k_steps = n_tok // tk

    def body(a_ref, b_ref, o_ref, acc_v):
        k = pl.program_id(1)

        @pl.when(k == 0)
        def _():
            acc_v[...] = jnp.zeros_like(acc_v)

        acc_v[...] += _dot_tn(a_ref[...], b_ref[...])

        @pl.when(k == k_steps - 1)
        def _():
            o_ref[...] = acc_v[...].astype(o_ref.dtype)

    return pl.pallas_call(
        body, name=name, grid=(m // tm, k_steps),
        in_specs=[pl.BlockSpec((tk, tm), lambda i, k: (k, i)), pl.BlockSpec((tk, n), lambda i, k: (k, 0))],
        out_specs=pl.BlockSpec((tm, n), lambda i, k: (i, 0)),
        out_shape=pltpu.HBM((m, n), out_dtype),
        scratch_shapes=[pltpu.VMEM((tm, n), F32)],
        compiler_params=pltpu.CompilerParams(dimension_semantics=("arbitrary", "arbitrary"),
                                             vmem_limit_bytes=40 * MIB),
    )(a, b)


def _wgrad_in(dpa, dpb, h, tk):
    n_tok, dm = h.shape
    k_steps = n_tok // tk
    last = k_steps - 1

    def from_a(i):
        return (i == 0) | (i >= 5)

    def body(a_ref, b_ref, h_ref, o_ref, acc_v):
        i, k = pl.program_id(0), pl.program_id(1)

        @pl.when(k == 0)
        def _():
            acc_v[...] = jnp.zeros_like(acc_v)

        @pl.when(from_a(i))
        def _():
            acc_v[...] += _dot_tn(a_ref[...], h_ref[...])

        @pl.when(jnp.logical_not(from_a(i)))
        def _():
            acc_v[...] += _dot_tn(b_ref[...], h_ref[...])

        @pl.when(k == last)
        def _():
            o_ref[...] = acc_v[...].astype(o_ref.dtype)

    a_index = lambda i, k: (jnp.where(from_a(i), k, last), jnp.where(i >= 5, i - 4, 0))
    b_index = lambda i, k: (jnp.where(from_a(i), jnp.where(i == 0, 0, last), k), jnp.clip(i - 1, 0, 3))
    return pl.pallas_call(
        body, name="wgrad_in", grid=(7, k_steps),
        in_specs=[pl.BlockSpec((tk, dm), a_index), pl.BlockSpec((tk, dm), b_index),
                  pl.BlockSpec((tk, dm), lambda i, k: (k, 0))],
        out_specs=pl.BlockSpec((dm, dm), lambda i, k: (i, 0)),
        out_shape=pltpu.HBM((7 * dm, dm), BF16),
        scratch_shapes=[pltpu.VMEM((dm, dm), F32)],
        compiler_params=pltpu.CompilerParams(dimension_semantics=("arbitrary", "arbitrary"),
                                             vmem_limit_bytes=48 * MIB),
    )(dpa, dpb, h)


def _adamw(w, g, m, v):
    m = ADAM_B1 * m + (1.0 - ADAM_B1) * g
    v = ADAM_B2 * v + (1.0 - ADAM_B2) * (g * g)
    m_hat = m / (1.0 - ADAM_B1 ** ADAM_STEP)
    v_hat = v / (1.0 - ADAM_B2 ** ADAM_STEP)
    delta = -ADAM_LR * (m_hat / (jnp.sqrt(v_hat) + ADAM_EPS) + ADAM_WD * w)
    return delta, m, v


def _adam_big(recv, part, me, off, rows, w, m, v, transpose, name, tr):
    dm = recv.shape[2]
    per = rows // tr

    def body(me_ref, own_ref, r_ref, w_ref, m_ref, v_ref, g_ref, d_ref, mo_ref, vo_ref):
        g = own_ref[...].astype(F32)
        for k in range(len(FLIPS)):
            g = g + r_ref[k].astype(F32)
        if transpose:
            g = g.T
        delta, m_new, v_new = _adamw(w_ref[...], g, m_ref[...], v_ref[...])
        g_ref[...] = g
        d_ref[...] = delta
        mo_ref[...] = m_new
        vo_ref[...] = v_new

    if transpose:
        blk = pl.BlockSpec((dm, tr), lambda i, me_ref: (0, i))
    else:
        blk = pl.BlockSpec((tr, dm), lambda i, me_ref: (i, 0))
    first = off // tr
    return pl.pallas_call(
        body, name=name,
        grid_spec=pltpu.PrefetchScalarGridSpec(
            num_scalar_prefetch=1, grid=(per,),
            in_specs=[pl.BlockSpec((tr, dm), lambda i, me_ref: (me_ref[0] * per + i, 0)),
                      pl.BlockSpec((len(FLIPS), tr, dm), lambda i, me_ref: (0, first + i, 0)), blk, blk, blk],
            out_specs=[blk] * 4),
        out_shape=[jax.ShapeDtypeStruct(w.shape, F32)] * 4,
        compiler_params=_params(32),
    )(me, *[_in_hbm(a) for a in (part, recv, w, m, v)])


LOSS_ROW = 15
CONV_A_ROW = 16
CONV_B_ROW = 24


def _adam_small(recv_small, recv_last, small_own, last_own, me, params, d_model):
    n = len(params)
    cw = d_model // N_DEV

    def body(me_ref, r_ref, rc_ref, l_ref, so_ref, soc_ref, lo_ref, *refs):
        ins, loss_ref, outs = refs[:3 * n], refs[3 * n], refs[3 * n + 1:3 * n + 1 + 4 * n]
        g_v, gc_v, last_v = refs[3 * n + 1 + 4 * n:]
        me_pos = me_ref[0]

        def slot(d, own_ref, slots_ref):
            return jnp.where(d == me_pos, own_ref[...], slots_ref[d])

        g, gc, last = slot(0, so_ref, r_ref), slot(0, soc_ref, rc_ref), slot(0, lo_ref, l_ref)
        for d in range(1, N_DEV):
            g, gc, last = g + slot(d, so_ref, r_ref), gc + slot(d, soc_ref, rc_ref), last + slot(d, lo_ref, l_ref)
        g_v[...], gc_v[...], last_v[...] = g, gc, last
        loss_ref[...] = (0.5 / d_model) * jnp.sum(g_v[LOSS_ROW:LOSS_ROW + 1, :], axis=-1, keepdims=True)
        for j, (row0, own_columns, (w, _, _)) in enumerate(params):
            w_ref, m_ref, v_ref = ins[3 * j:3 * j + 3]
            source = gc_v if own_columns else (last_v if row0 == 0 else g_v)
            width = source.shape[1]
            for c in range(w.shape[1] // width):
                cols = slice(c * width, (c + 1) * width)
                grad = source[row0 + c * w.shape[0]:row0 + (c + 1) * w.shape[0], :]
                delta, m_new, v_new = _adamw(w_ref[:, cols], grad, m_ref[:, cols], v_ref[:, cols])
                for ref, val in zip(outs[4 * j:4 * j + 4], (grad, delta, m_new, v_new)):
                    ref[:, cols] = val

    full = lambda shape: pl.BlockSpec(shape, lambda i, me_ref: (0,) * len(shape))
    stack_rows = recv_small.shape[1]
    flat = [a for _, _, triple in params for a in triple]
    shapes = [w.shape for _, _, (w, _, _) in params for _ in range(4)]
    out = pl.pallas_call(
        body, name="adam_small",
        grid_spec=pltpu.PrefetchScalarGridSpec(
            num_scalar_prefetch=1, grid=(1,),
            in_specs=[full(recv_small.shape),
                      pl.BlockSpec((N_DEV, stack_rows, cw), lambda i, me_ref: (0, 0, me_ref[0])),
                      full(recv_last.shape), full(small_own.shape),
                      pl.BlockSpec((stack_rows, cw), lambda i, me_ref: (0, me_ref[0])),
                      full(last_own.shape)] + [full(a.shape) for a in flat],
            out_specs=[full((1, 1))] + [full(s) for s in shapes],
            scratch_shapes=[pltpu.VMEM((stack_rows, d_model), F32), pltpu.VMEM((stack_rows, cw), F32),
                            pltpu.VMEM(recv_last.shape[1:], F32)]),
        out_shape=[jax.ShapeDtypeStruct((1, 1), F32)] + [jax.ShapeDtypeStruct(s, F32) for s in shapes],
    )(me, *[_in_hbm(a) for a in (recv_small, recv_small, recv_last, small_own, small_own, last_own, *flat)])
    return out[0], [tuple(out[1 + 4 * j:5 + 4 * j]) for j in range(n)]


def _tile(n_tok, want):
    return min(want, n_tok)


def kernel(x, norm1_pre_g, w_in, b_in, conv_a_w, conv_a_b, w_a_out, conv_b_w, conv_b_b, ln_b_g, ln_b_b, w_b_out, w_o, norm1_post_g, norm2_pre_g, w_mlp_in, w_mlp_out, norm2_post_g, loss_target, m_norm1_pre_g, m_w_in, m_b_in, m_conv_a_w, m_conv_a_b, m_w_a_out, m_conv_b_w, m_conv_b_b, m_ln_b_g, m_ln_b_b, m_w_b_out, m_w_o, m_norm1_post_g, m_norm2_pre_g, m_w_mlp_in, m_w_mlp_out, m_norm2_post_g, v_norm1_pre_g, v_w_in, v_b_in, v_conv_a_w, v_conv_a_b, v_w_a_out, v_conv_b_w, v_conv_b_b, v_ln_b_g, v_ln_b_b, v_w_b_out, v_w_o, v_norm1_post_g, v_norm2_pre_g, v_w_mlp_in, v_w_mlp_out, v_norm2_post_g):
    n_tok, dm = x.shape[1], x.shape[2]
    rows, off, slab_rows = _layout(dm)
    cw = dm // N_DEV
    xs = x.reshape(n_tok, dm)
    tgt = loss_target.reshape(n_tok, dm)
    row = lambda vec: vec.reshape(1, -1)
    scattered = lambda group: jax.ShapeDtypeStruct((len(FLIPS), slab_rows[group], dm), BF16)
    tm, tk = min(dm, 1024), _tile(n_tok, 2048)
    me = (4 * lax.axis_index("x") + 2 * lax.axis_index("y") + lax.axis_index("c")).astype(jnp.int32).reshape(1)

    conv_own = jnp.concatenate([conv_a_w, jnp.zeros((SUBLANE - CONV_A, cw), F32), conv_b_w,
                                jnp.zeros((1, cw), F32)], axis=0)
    own_in, land_in = _place_cast([(w_in, True)], me, "place_w_in")
    slab_in, conv_all = _all_gather_two_level([own_in, conv_own], [land_in, None], "gather_w_in")
    conv_full = conv_all.transpose(1, 0, 2).reshape(conv_own.shape[0], dm)
    caw, cbw = conv_full[0:CONV_A], conv_full[SUBLANE:SUBLANE + CONV_B]
    own_abo, land_abo = _place_cast([(w_a_out, False), (w_b_out, False), (w_o, False)], me, "place_abo")
    own_mlp, land_mlp = _place_cast([(w_mlp_in, True), (w_mlp_out, False)], me, "place_mlp")
    ag_abo = _exchange_start([_Part(own_abo, False, slab_rows["abo"], 0, 0)], [land_abo],
                             "gather_abo_start", after=slab_in)
    ag_mlp = _exchange_start([_Part(own_mlp, False, slab_rows["mlp"], 0, 0)], [land_mlp],
                             "gather_mlp_start", after=ag_abo.token)

    proj, p, u, h = _fwd_in(xs, row(norm1_pre_g), row(b_in), slab_in, _tile(n_tok, 512),
                            [ag_abo.token, ag_mlp.token])
    _, (slab_abo,) = _exchange_wait(ag_abo, "gather_abo_wait", after=proj)
    va, vb, ya, yb, qa, sb, merged, mix, x1 = _fwd_mix(
        p, u, proj, xs, caw, row(conv_a_b), cbw, row(conv_b_b), row(ln_b_g), row(ln_b_b), row(norm1_post_g),
        slab_abo, _tile(n_tok, 256))
    _, (slab_mlp,) = _exchange_wait(ag_mlp, "gather_mlp_wait", after=x1)
    f, df1, h2, df2, dmix, dx1, small_mlp = _mlp_fwd_bwd(
        x1, mix, tgt, row(norm1_post_g), row(norm2_pre_g), row(norm2_post_g), slab_mlp, _tile(n_tok, 256))

    rs_mlp = _exchange_start(
        [_Part(_wgrad(df1, h2, "wgrad_mlp_in", tm, tk, BF16), True, rows["w1"], 0, off["w1"]),
         _Part(_wgrad(f, df2, "wgrad_mlp_out", tm, tk, BF16), True, rows["w2"], 0, off["w2"])],
        [scattered("mlp")], "scatter_mlp_start")
    dpa, dya, dyb, dva, dvb, small_mix = _bwd_mix(
        dmix, ya, yb, proj, va, vb, row(ln_b_g), row(ln_b_b), slab_abo, _tile(n_tok, 512), [rs_mlp.token])
    rs_abo = _exchange_start(
        [_Part(_wgrad(qa, dya, "wgrad_a_out", tm, tk, BF16), True, rows["wa"], 0, off["wa"]),
         _Part(_wgrad(sb, dyb, "wgrad_b_out", tm, tk, BF16), True, rows["wb"], 0, off["wb"]),
         _Part(_wgrad(merged, dmix, "wgrad_o", tm, tk, BF16), True, rows["wo"], 0, off["wo"])],
        [scattered("abo")], "scatter_abo_start")
    dpb, small_conv = _bwd_conv(dva, dvb, p, u, proj, caw, cbw, _tile(n_tok, 256), [rs_abo.token])

    zeros = lambda r: jnp.zeros((r, dm), F32)
    small = jnp.concatenate([
        zeros(1),
        small_mix[2:3],
        small_mix[3:4],
        small_mix[0:2],
        small_mlp[2:3],
        small_mlp[1:2],
        small_mlp[0:1],
        small_mix[4:5], small_conv[3:7], small_mix[5:7],
        small_mlp[3:4],
        small_conv[0:CONV_A], zeros(SUBLANE - CONV_A),
        small_conv[8:8 + CONV_B], zeros(1),
    ], axis=0)

    rs_in = _exchange_start(
        [_Part(_wgrad_in(dpa, dpb, h, tk), True, rows["win"], 0, off["win"]),
         _Part(small, False, small.shape[0], 1, 0)],
        [scattered("in"), jax.ShapeDtypeStruct((N_DEV,) + small.shape, F32)], "scatter_in_start")
    grad_x, small_in = _bwd_in(dpa, dpb, xs, dx1, row(norm1_pre_g), slab_in, _tile(n_tok, 512), [rs_in.token])

    tr = min(LANE, rows["wa"])
    (g_w1, g_w2), (recv_mlp,) = _exchange_wait(rs_mlp, "scatter_mlp_wait", after=grad_x)
    (g_wa, g_wb, g_wo), (recv_abo,) = _exchange_wait(rs_abo, "scatter_abo_wait", after=grad_x)
    big = {
        "w_mlp_in": _adam_big(recv_mlp, g_w1, me, off["w1"], rows["w1"], w_mlp_in, m_w_mlp_in, v_w_mlp_in, True,
                              "adam_w_mlp_in", tr),
        "w_mlp_out": _adam_big(recv_mlp, g_w2, me, off["w2"], rows["w2"], w_mlp_out, m_w_mlp_out, v_w_mlp_out,
                               False, "adam_w_mlp_out", tr),
        "w_a_out": _adam_big(recv_abo, g_wa, me, off["wa"], rows["wa"], w_a_out, m_w_a_out, v_w_a_out, False,
                             "adam_w_a_out", tr),
        "w_b_out": _adam_big(recv_abo, g_wb, me, off["wb"], rows["wb"], w_b_out, m_w_b_out, v_w_b_out, False,
                             "adam_w_b_out", tr),
        "w_o": _adam_big(recv_abo, g_wo, me, off["wo"], rows["wo"], w_o, m_w_o, v_w_o, False, "adam_w_o", tr),
    }
    (g_win, small), (recv_in, recv_small) = _exchange_wait(rs_in, "scatter_in_wait", after=big["w_o"][3])
    gather_last = _exchange_start([_Part(small_in, False, SUBLANE, 0, 0)],
                                  [jax.ShapeDtypeStruct((N_DEV, SUBLANE, dm), F32)], "gather_last_start",
                                  after=recv_in)
    big["w_in"] = _adam_big(recv_in, g_win, me, off["win"], rows["win"], w_in, m_w_in, v_w_in, True, "adam_w_in", tr)
    (small_in,), (recv_last,) = _exchange_wait(gather_last, "gather_last_wait", after=big["w_in"][3])

    small_names = ("norm1_pre_g", "conv_a_b", "conv_b_b", "ln_b_g", "ln_b_b", "norm1_post_g", "norm2_pre_g",
                   "norm2_post_g")
    given = dict(
        norm1_pre_g=(norm1_pre_g, m_norm1_pre_g, v_norm1_pre_g), conv_a_b=(conv_a_b, m_conv_a_b, v_conv_a_b),
        conv_b_b=(conv_b_b, m_conv_b_b, v_conv_b_b), ln_b_g=(ln_b_g, m_ln_b_g, v_ln_b_g),
        ln_b_b=(ln_b_b, m_ln_b_b, v_ln_b_b), norm1_post_g=(norm1_post_g, m_norm1_post_g, v_norm1_post_g),
        norm2_pre_g=(norm2_pre_g, m_norm2_pre_g, v_norm2_pre_g),
        norm2_post_g=(norm2_post_g, m_norm2_post_g, v_norm2_post_g))
    params = [(j, False, tuple(row(a) for a in given[name])) for j, name in enumerate(small_names)]
    params.append((SUBLANE, False, tuple(row(a) for a in (b_in, m_b_in, v_b_in))))
    params.append((CONV_A_ROW, True, (conv_a_w, m_conv_a_w, v_conv_a_w)))
    params.append((CONV_B_ROW, True, (conv_b_w, m_conv_b_w, v_conv_b_w)))
    loss, small_out = _adam_small(recv_small, recv_last, small, small_in, me, params, dm)
    small_leaves = {name: tuple(a.reshape(dm) for a in small_out[j]) for j, name in enumerate(small_names)}
    small_leaves["b_in"] = tuple(a.reshape(7 * dm) for a in small_out[len(small_names)])
    small_leaves["conv_a_w"] = small_out[len(small_names) + 1]
    small_leaves["conv_b_w"] = small_out[len(small_names) + 2]

    order = ("norm1_pre_g", "w_in", "b_in", "conv_a_w", "conv_a_b", "w_a_out", "conv_b_w", "conv_b_b", "ln_b_g",
             "ln_b_b", "w_b_out", "w_o", "norm1_post_g", "norm2_pre_g", "w_mlp_in", "w_mlp_out", "norm2_post_g")
    leaves = [big[name] if name in big else small_leaves[name] for name in order]
    grads, deltas, new_m, new_v = zip(*leaves)
    return (loss.reshape(()), grad_x.reshape(x.shape), *grads, *deltas, *new_m, *new_v)
```

```python
from typing import NamedTuple

import jax
import jax.numpy as jnp
from jax import lax
from jax.experimental import pallas as pl
from jax.experimental.pallas import tpu as pltpu

F32 = jnp.float32
BF16 = jnp.bfloat16

RMS_EPS = 1e-6
LN_EPS = 1e-5
ADAM_LR = 0.001
ADAM_B1 = 0.9
ADAM_B2 = 0.999
ADAM_EPS = 1e-08
ADAM_WD = 0.01
ADAM_STEP = 10

N_DEV = 8
CONV_A = 3
CONV_B = 31
LANE = 128
SUBLANE = 8
HALO = 16
FWD_CONV_ROWS = 32
BWD_CONV_ROWS = 64
MIB = 1 << 20
FLIPS = ((0, 0, 1), (0, 1, 0), (1, 0, 0), (0, 1, 1), (1, 0, 1), (1, 1, 0), (1, 1, 1))
MESH = pl.DeviceIdType.MESH


def _layout(d_model):
    e = d_model // N_DEV
    rows = {"win": 7 * e, "w1": 4 * e, "w2": 4 * e, "wa": e, "wb": e, "wo": e}
    off = {"win": 0, "w1": 0, "w2": 4 * e, "wa": 0, "wb": e, "wo": 2 * e}
    return rows, off, {"in": 7 * e, "mlp": 8 * e, "abo": 3 * e}


def _after(body, deps):
    def ordered(*refs):
        return body(*refs[len(deps):])
    return ordered


def _params(vmem_mib):
    return pltpu.CompilerParams(dimension_semantics=("arbitrary",), vmem_limit_bytes=vmem_mib * MIB)


def _whole():
    return pl.BlockSpec(memory_space=pltpu.VMEM)


def _hbm():
    return pl.BlockSpec(memory_space=pl.ANY)


def _in_hbm(a):
    return pltpu.with_memory_space_constraint(a, pltpu.HBM)


def _rows(ts, width, col=0):
    return pl.BlockSpec((ts, width), lambda i: (i, col))


def _halo_specs(ts, width, n_rows):
    per = ts // HALO
    last = n_rows // HALO - 1
    return [
        pl.BlockSpec((ts, width), lambda i: (i, 0)),
        pl.BlockSpec((HALO, width), lambda i: (jnp.maximum(i * per - 1, 0), 0)),
        pl.BlockSpec((HALO, width), lambda i: (jnp.minimum((i + 1) * per, last), 0)),
    ]


def _dot(a, b):
    return jnp.dot(a, b, preferred_element_type=F32)


def _dot_nt(a, b):
    return lax.dot_general(a, b, (((1,), (1,)), ((), ())), preferred_element_type=F32)


def _dot_tn(a, b):
    return lax.dot_general(a, b, (((0,), (0,)), ((), ())), preferred_element_type=F32)


def _rms(u):
    return lax.rsqrt(jnp.mean(u * u, axis=-1, keepdims=True) + RMS_EPS)


def _rms_bwd(dz, u, r, g):
    dzg = dz * g
    return r * dzg - u * (r * r * r) * jnp.mean(dzg * u, axis=-1, keepdims=True)


def _colsum(v):
    return jnp.sum(v, axis=0, keepdims=True)


def _sigmoid(v):
    return jax.nn.sigmoid(v)


def _weight_copies(slab_ref, off, rows, dst_ref, sems, first_sem):
    return [pltpu.make_async_copy(slab_ref.at[d, pl.ds(off, rows), :], dst_ref.at[pl.ds(d * rows, rows), :],
                                  sems.at[first_sem + d]) for d in range(N_DEV)]


def _on_first_step(copies, method):
    @pl.when(pl.program_id(0) == 0)
    def _():
        for cp in copies:
            getattr(cp, method)()


def _with_halos(main_ref, prev_ref, next_ref, i, n_steps):
    return (main_ref, jnp.where(i > 0, prev_ref[...], 0.0), jnp.where(i < n_steps - 1, next_ref[...], 0.0))


def _broadcast_taps(w_ref, wb_ref, n_taps):
    for k in range(n_taps):
        wb_ref[k] = jnp.broadcast_to(w_ref[k:k + 1, :], wb_ref.shape[1:])


def _conv_tile(tile, wb_ref, starts, ts, width, emit, rolled_ref, rows=FWD_CONV_ROWS):
    main_ref, prev, nxt = tile
    span = ts + 2 * HALO
    nv = rows // SUBLANE
    for cb in range(width // LANE):
        lanes = slice(cb * LANE, (cb + 1) * LANE)
        slot = cb % 2
        window = jnp.concatenate([prev[:, lanes], main_ref[:, lanes], nxt[:, lanes]], axis=0)
        for b in sorted({st % SUBLANE for st in starts}):
            rolled_ref[slot, b] = window if b == 0 else pltpu.roll(window, span - b, axis=0)
        for r0 in range(0, ts, rows):
            acc = jnp.zeros((nv, SUBLANE, LANE), F32)
            for k, st in enumerate(starts):
                shifted = rolled_ref[slot, st % SUBLANE, pl.ds(r0 + st - st % SUBLANE, rows), :]
                acc = acc + shifted.reshape(nv, SUBLANE, LANE) * wb_ref[k, :, lanes][None]
            emit(r0, pl.ds(cb * LANE, LANE), acc.reshape(rows, LANE))


def _window(tile, r0, cb, ts, rows):
    main_ref, prev, nxt = tile
    lanes = slice(cb * LANE, (cb + 1) * LANE)
    lo, hi = max(r0 - HALO, 0), min(r0 + rows + HALO, ts)
    pieces = [prev[:, lanes]] if r0 - HALO < 0 else []
    pieces.append(main_ref[lo:hi, lanes])
    if r0 + rows + HALO > ts:
        pieces.append(nxt[:, lanes])
    return pieces[0] if len(pieces) == 1 else jnp.concatenate(pieces, axis=0)


def _phases(starts):
    groups = {}
    for k, st in enumerate(starts):
        groups.setdefault(st % SUBLANE, []).append((k, st // SUBLANE))
    return sorted(groups.items())


def _shifted(blk, b):
    n = blk.shape[0]
    rolled = blk if b == 0 else pltpu.roll(blk, n - b, axis=0)
    return rolled.reshape(n // SUBLANE, SUBLANE, blk.shape[1])


def _conv_bwd_tile(dv_tile, u_ref, wb_ref, acc_ref, n_taps, ts, width, emit, rows=BWD_CONV_ROWS):
    groups = _phases(_bwd_starts(n_taps))
    nv = rows // SUBLANE
    for r0 in range(0, ts, rows):
        for cb in range(width // LANE):
            lanes = pl.ds(cb * LANE, LANE)
            blk = _window(dv_tile, r0, cb, ts, rows)
            u = u_ref[pl.ds(r0, rows), lanes].reshape(nv, SUBLANE, LANE)
            du = jnp.zeros((nv, SUBLANE, LANE), F32)
            for b, taps in groups:
                sh = _shifted(blk, b)
                for k, m in taps:
                    du = du + sh[m:m + nv] * wb_ref[k, :, lanes][None]
                    acc_ref[k, :, lanes] += jnp.sum(sh[m:m + nv] * u, axis=0)
            emit(r0, lanes, du.reshape(rows, LANE))


def _fwd_starts(n_taps):
    pad = (n_taps - 1) // 2
    return [HALO - pad + k for k in range(n_taps)]


def _bwd_starts(n_taps):
    pad = (n_taps - 1) // 2
    return [HALO + pad - k for k in range(n_taps)]


def _peer(x, y, c, flip):
    fx, fy, fc = flip
    return (1 - x if fx else x, 1 - y if fy else y, 1 - c if fc else c)


def _place_cast(pieces, me, name):
    n = len(pieces)
    counts = [a.shape[1] if t else a.shape[0] for a, t in pieces]
    width = pieces[0][0].shape[0] if pieces[0][1] else pieces[0][0].shape[1]
    total = sum(counts)

    def body(me_ref, *refs):
        ins, own_ref, land_ref = refs[:n], refs[n], refs[n + 1]
        first = 0
        for (a, transpose), in_ref, count in zip(pieces, ins, counts):
            block = (in_ref[...].T if transpose else in_ref[...]).astype(BF16)
            own_ref[first:first + count, :] = block
            land_ref[first:first + count, :] = block
            first += count

    return pl.pallas_call(
        body, name=name,
        grid_spec=pltpu.PrefetchScalarGridSpec(
            num_scalar_prefetch=1, grid=(1,),
            in_specs=[pl.BlockSpec(a.shape, lambda i, me_ref: (0, 0)) for a, _ in pieces],
            out_specs=[pl.BlockSpec((total, width), lambda i, me_ref: (0, 0)),
                       pl.BlockSpec((None, total, width), lambda i, me_ref: (me_ref[0], 0, 0))]),
        out_shape=[pltpu.HBM((total, width), BF16), pltpu.HBM((N_DEV, total, width), BF16)],
        compiler_params=_params(32),
    )(me, *[_in_hbm(a) for a, _ in pieces])


def _all_gather_two_level(shards, placed, name):
    n = len(shards)
    given = [j for j in range(n) if placed[j] is not None]

    def body(*refs):
        ins, outs = refs[:n], refs[n + len(given):2 * n + len(given)]
        send_sems, recv_sems, local_sems = refs[2 * n + len(given):]
        x, y, c = lax.axis_index("x"), lax.axis_index("y"), lax.axis_index("c")
        me, sibling = (x, y, c), (x, y, 1 - c)
        chips = [(1 - x, y), (x, 1 - y), (1 - x, 1 - y)]

        def slot(j, dev):
            return outs[j].at[4 * dev[0] + 2 * dev[1] + dev[2]]

        def copy(k, j, block, to, src=None):
            return pltpu.make_async_remote_copy(
                src_ref=slot(j, block) if src is None else src, dst_ref=slot(j, block),
                send_sem=send_sems.at[k * n + j], recv_sem=recv_sems.at[k * n + j], device_id=to, device_id_type=MESH)

        local = [pltpu.make_async_copy(ins[j], slot(j, me), local_sems.at[j]) for j in range(n) if j not in given]
        for cp in local:
            cp.start()
        first = [copy(0, j, me, sibling, src=ins[j]) for j in range(n)]
        first += [copy(1 + t, j, me, (*chip, c), src=ins[j]) for t, chip in enumerate(chips) for j in range(n)]
        for cp in first:
            cp.start()
        passed = []
        for t, chip in enumerate(chips):
            for j in range(n):
                copy(1 + t, j, (*chip, c), me).wait_recv()
                passed.append(copy(4 + t, j, (*chip, c), sibling))
                passed[-1].start()
        for j in range(n):
            copy(0, j, sibling, me).wait_recv()
        for t, chip in enumerate(chips):
            for j in range(n):
                copy(4 + t, j, (*chip, 1 - c), me).wait_recv()
        for cp in first + passed:
            cp.wait_send()
        for cp in local:
            cp.wait()

    return pl.pallas_call(
        body, name=name,
        out_shape=[jax.ShapeDtypeStruct((N_DEV,) + s.shape, s.dtype) for s in shards],
        in_specs=[_hbm()] * (n + len(given)), out_specs=[_hbm()] * n,
        input_output_aliases={n + i: j for i, j in enumerate(given)},
        scratch_shapes=[pltpu.SemaphoreType.DMA((7 * n,)), pltpu.SemaphoreType.DMA((7 * n,)),
                        pltpu.SemaphoreType.DMA((n,))],
    )(*shards, *[placed[j] for j in given])


class _Part(NamedTuple):
    src: jax.Array
    scatter: bool
    rows: int
    land: int
    off: int


class _Started(NamedTuple):
    send_sems: jax.Array
    recv_sems: jax.Array
    thru: tuple
    token: jax.Array
    parts: tuple


def _exchange_copies(srcs, lands, send_sems, recv_sems, parts):
    n = len(parts)
    x, y, c = lax.axis_index("x"), lax.axis_index("y"), lax.axis_index("c")
    me = 4 * x + 2 * y + c

    def block(j, dev):
        p = parts[j]
        return srcs[j].at[pl.ds(pl.multiple_of(dev * p.rows, SUBLANE), p.rows), :] if p.scatter else srcs[j]

    def slot(j, index):
        p = parts[j]
        return lands[p.land].at[index, pl.ds(p.off, p.rows), :]

    sends, recvs = [], []
    for k, flip in enumerate(FLIPS):
        px, py, pc = _peer(x, y, c, flip)
        peer = 4 * px + 2 * py + pc
        for j in range(n):
            sems = dict(send_sem=send_sems.at[k * n + j], recv_sem=recv_sems.at[k * n + j],
                        device_id=(px, py, pc), device_id_type=MESH)
            to, got = (k, k) if parts[j].scatter else (me, peer)
            sends.append(pltpu.make_async_remote_copy(src_ref=block(j, peer), dst_ref=slot(j, to), **sems))
            recvs.append(pltpu.make_async_remote_copy(src_ref=block(j, peer), dst_ref=slot(j, got), **sems))
    return sends, recvs


def _exchange_start(parts, lands, name, after=None):
    n, nl = len(parts), len(lands)
    n_in = n + nl + (after is not None)

    def body(*refs):
        srcs, land_refs = refs[:n], refs[n:n + nl]
        send_sems, recv_sems = refs[n_in], refs[n_in + 1]
        token = refs[n_in + 2 + n + nl]
        sends, _ = _exchange_copies(srcs, land_refs, send_sems, recv_sems, parts)
        for cp in sends:
            cp.start()
        token[...] = jnp.zeros_like(token)

    hbm = pl.BlockSpec(memory_space=pltpu.HBM)
    sem = pl.BlockSpec(memory_space=pltpu.SEMAPHORE)
    fresh = lambda s: lax.empty(s.shape, s.dtype) if isinstance(s, jax.ShapeDtypeStruct) else s
    args = [pltpu.with_memory_space_constraint(p.src, pltpu.HBM) for p in parts]
    args += [pltpu.with_memory_space_constraint(fresh(s), pltpu.HBM) for s in lands]
    args += [] if after is None else [after]
    out = pl.pallas_call(
        body, name=name,
        out_shape=(pltpu.SemaphoreType.DMA((7 * n,)), pltpu.SemaphoreType.DMA((7 * n,)),
                   *[pltpu.HBM(a.shape, a.dtype) for a in args[:n + nl]], jax.ShapeDtypeStruct((SUBLANE, LANE), F32)),
        in_specs=[hbm] * (n + nl) + [_hbm()] * (after is not None),
        out_specs=(sem, sem, *[hbm] * (n + nl), _whole()),
        input_output_aliases={j: 2 + j for j in range(n + nl)},
        compiler_params=pltpu.CompilerParams(has_side_effects=pltpu.SideEffectType.DATAFLOW_SIDE_EFFECTING),
    )(*args)
    return _Started(out[0], out[1], tuple(out[2:2 + n + nl]), out[2 + n + nl], tuple(parts))


def _exchange_wait(started, name, after):
    parts = started.parts
    n, nl = len(parts), len(started.thru) - len(parts)

    def body(*refs):
        srcs, land_refs = refs[:n], refs[n:n + nl]
        send_sems, recv_sems = refs[n + nl], refs[n + nl + 1]
        sends, recvs = _exchange_copies(srcs, land_refs, send_sems, recv_sems, parts)
        for cp in sends:
            cp.wait_send()
        for cp in recvs:
            cp.wait_recv()

    hbm = pl.BlockSpec(memory_space=pltpu.HBM)
    sem = pl.BlockSpec(memory_space=pltpu.SEMAPHORE)
    out = pl.pallas_call(
        body, name=name,
        out_shape=tuple(pltpu.HBM(a.shape, a.dtype) for a in started.thru),
        in_specs=[hbm] * (n + nl) + [sem, sem, _hbm()], out_specs=tuple([hbm] * (n + nl)),
        input_output_aliases={j: j for j in range(n + nl)},
        compiler_params=pltpu.CompilerParams(has_side_effects=pltpu.SideEffectType.DATAFLOW_SIDE_EFFECTING),
    )(*started.thru, started.send_sems, started.recv_sems, after)
    return list(out[:n]), list(out[n:])


def _fwd_in(x, g1, b_in, slab, ts, deps):
    n_tok, dm = x.shape
    rows, off, _ = _layout(dm)
    width = 7 * dm

    def body(x_ref, g1_ref, b_ref, slab_ref, proj_ref, p_ref, u_ref, h_ref, w_v, sems):
        copies = _weight_copies(slab_ref, off["win"], rows["win"], w_v, sems, 0)
        _on_first_step(copies, "start")
        _on_first_step(copies, "wait")
        xv = x_ref[...]
        h = (xv * _rms(xv) * g1_ref[...]).astype(BF16)
        h_ref[...] = h
        cols = []
        for j in range(7):
            pj = _dot_nt(h, w_v[pl.ds(j * dm, dm), :]) + b_ref[:, j * dm:(j + 1) * dm]
            proj_ref[:, j * dm:(j + 1) * dm] = pj.astype(proj_ref.dtype)
            if 1 <= j <= 4:
                cols.append(pj)
            if j == 2:
                p_ref[...] = cols[0] * cols[1]
            if j == 4:
                u_ref[...] = cols[2] * _sigmoid(cols[3])

    return pl.pallas_call(
        _after(body, deps), name="fwd_in", grid=(n_tok // ts,),
        in_specs=[_whole()] * len(deps) + [_rows(ts, dm), _whole(), _whole(), _hbm()],
        out_specs=[_rows(ts, width), _rows(ts, dm), _rows(ts, dm), _rows(ts, dm)],
        out_shape=[jax.ShapeDtypeStruct((n_tok, width), BF16), jax.ShapeDtypeStruct((n_tok, dm), F32),
                   jax.ShapeDtypeStruct((n_tok, dm), F32), jax.ShapeDtypeStruct((n_tok, dm), BF16)],
        scratch_shapes=[pltpu.VMEM((width, dm), BF16), pltpu.SemaphoreType.DMA((N_DEV,))],
        compiler_params=_params(56),
    )(*deps, x, g1, b_in, slab)


def _fwd_mix(p, u, proj, x, caw, cab, cbw, cbb, lng, lnb, g1post, slab, ts):
    n_tok, dm = x.shape
    rows, off, _ = _layout(dm)
    n_steps = n_tok // ts

    def body(p_ref, p_prev, p_next, u_ref, u_prev, u_next, bg_ref, za_ref, zb_ref, x_ref,
             caw_ref, cab_ref, cbw_ref, cbb_ref, lng_ref, lnb_ref, g1p_ref, slab_ref,
             va_ref, vb_ref, ya_ref, yb_ref, qa_ref, sb_ref, mg_ref, mix_ref, x1_ref,
             wa_v, wb_v, wo_v, tap_a, tap_b, sems, rolled_a, rolled_b):
        i = pl.program_id(0)
        copies = (_weight_copies(slab_ref, off["wa"], rows["wa"], wa_v, sems, 0)
                  + _weight_copies(slab_ref, off["wb"], rows["wb"], wb_v, sems, N_DEV)
                  + _weight_copies(slab_ref, off["wo"], rows["wo"], wo_v, sems, 2 * N_DEV))
        _on_first_step(copies, "start")

        @pl.when(i == 0)
        def _():
            _broadcast_taps(caw_ref, tap_a, CONV_A)
            _broadcast_taps(cbw_ref, tap_b, CONV_B)

        def emit_a(r0, lanes, acc):
            va_ref[pl.ds(r0, acc.shape[0]), lanes] = acc + cab_ref[:, lanes]

        def emit_b(r0, lanes, acc):
            vb_ref[pl.ds(r0, acc.shape[0]), lanes] = acc + cbb_ref[:, lanes]

        _conv_tile(_with_halos(p_ref, p_prev, p_next, i, n_steps), tap_a, _fwd_starts(CONV_A), ts, dm, emit_a,
                   rolled_a)
        _conv_tile(_with_halos(u_ref, u_prev, u_next, i, n_steps), tap_b, _fwd_starts(CONV_B), ts, dm, emit_b,
                   rolled_b)
        _on_first_step(copies, "wait")

        qa = (bg_ref[...].astype(F32) * va_ref[...]).astype(BF16)
        qa_ref[...] = qa
        ya = _dot(qa, wa_v[...])
        vb = vb_ref[...]
        xc = vb - jnp.mean(vb, axis=-1, keepdims=True)
        rstd = lax.rsqrt(jnp.mean(xc * xc, axis=-1, keepdims=True) + LN_EPS)
        ln = xc * rstd * lng_ref[...] + lnb_ref[...]
        sb = (ln * _sigmoid(ln)).astype(BF16)
        sb_ref[...] = sb
        yb = _dot(sb, wb_v[...])
        ya_ref[...] = ya.astype(BF16)
        yb_ref[...] = yb.astype(BF16)
        merged = (_sigmoid(za_ref[...].astype(F32)) * ya + _sigmoid(zb_ref[...].astype(F32)) * yb).astype(BF16)
        mg_ref[...] = merged
        mix = _dot(merged, wo_v[...])
        mix_ref[...] = mix
        x1_ref[...] = x_ref[...] + mix * _rms(mix) * g1p_ref[...]

    tok = lambda dt: jax.ShapeDtypeStruct((n_tok, dm), dt)
    return pl.pallas_call(
        body, name="fwd_mix", grid=(n_steps,),
        in_specs=(_halo_specs(ts, dm, n_tok) + _halo_specs(ts, dm, n_tok)
                  + [_rows(ts, dm, 0), _rows(ts, dm, 5), _rows(ts, dm, 6), _rows(ts, dm)]
                  + [_whole()] * 7 + [_hbm()]),
        out_specs=[_rows(ts, dm)] * 9,
        out_shape=[tok(F32), tok(F32), tok(BF16), tok(BF16), tok(BF16), tok(BF16), tok(BF16), tok(F32), tok(F32)],
        scratch_shapes=[pltpu.VMEM((dm, dm), BF16), pltpu.VMEM((dm, dm), BF16), pltpu.VMEM((dm, dm), BF16),
                        pltpu.VMEM((CONV_A, SUBLANE, dm), F32), pltpu.VMEM((CONV_B, SUBLANE, dm), F32),
                        pltpu.SemaphoreType.DMA((3 * N_DEV,)),
                        pltpu.VMEM((2, SUBLANE, ts + 2 * HALO, LANE), F32),
                        pltpu.VMEM((2, SUBLANE, ts + 2 * HALO, LANE), F32)],
        compiler_params=_params(48),
    )(p, p, p, u, u, u, proj, proj, proj, x, caw, cab, cbw, cbb, lng, lnb, g1post, slab)


def _mlp_fwd_bwd(x1, mix, tgt, g1post, g2pre, g2post, slab, ts):
    n_tok, dm = x1.shape
    rows, off, _ = _layout(dm)
    ff = 4 * dm

    def body(x1_ref, mix_ref, t_ref, g1p_ref, g2pre_ref, g2post_ref, slab_ref,
             f_ref, df1_ref, h2_ref, df2_ref, dmix_ref, dx1_ref, small_ref, w1_v, w2_v, relu_v, sems):
        w1_copies = _weight_copies(slab_ref, off["w1"], rows["w1"], w1_v, sems, 0)
        w2_copies = _weight_copies(slab_ref, off["w2"], rows["w2"], w2_v, sems, N_DEV)
        _on_first_step(w1_copies + w2_copies, "start")

        @pl.when(pl.program_id(0) == 0)
        def _():
            small_ref[...] = jnp.zeros_like(small_ref)

        _on_first_step(w1_copies + w2_copies, "wait")
        x1v = x1_ref[...]
        r3 = _rms(x1v)
        g2pre = g2pre_ref[...]
        h2 = (x1v * r3 * g2pre).astype(BF16)
        h2_ref[...] = h2
        for c in range(4):
            blk = pl.ds(c * dm, dm)
            relu = jnp.maximum(_dot_nt(h2, w1_v[blk, :]), 0.0)
            relu_v[:, c * dm:(c + 1) * dm] = relu
            f_ref[:, c * dm:(c + 1) * dm] = (relu * relu).astype(BF16)
        f2 = _dot(f_ref[...], w2_v[...])
        r4 = _rms(f2)
        g2post = g2post_ref[...]
        err = x1v + f2 * r4 * g2post - t_ref[...]
        dy = err * (1.0 / dm)
        small_ref[3:4, :] += _colsum(err * err)
        small_ref[0:1, :] += _colsum(dy * f2 * r4)
        df2 = _rms_bwd(dy, f2, r4, g2post).astype(BF16)
        df2_ref[...] = df2
        for c in range(4):
            blk = pl.ds(c * dm, dm)
            df1 = (_dot_nt(df2, w2_v[blk, :]) * (2.0 * relu_v[:, c * dm:(c + 1) * dm])).astype(BF16)
            df1_ref[:, c * dm:(c + 1) * dm] = df1
        dh2 = _dot(df1_ref[...], w1_v[...])
        small_ref[1:2, :] += _colsum(dh2 * x1v * r3)
        dx1 = dy + _rms_bwd(dh2, x1v, r3, g2pre)
        dx1_ref[...] = dx1
        mixv = mix_ref[...]
        r2 = _rms(mixv)
        small_ref[2:3, :] += _colsum(dx1 * mixv * r2)
        dmix_ref[...] = _rms_bwd(dx1, mixv, r2, g1p_ref[...]).astype(BF16)

    tok = lambda w, dt: jax.ShapeDtypeStruct((n_tok, w), dt)
    return pl.pallas_call(
        body, name="mlp_fwd_bwd", grid=(n_tok // ts,),
        in_specs=[_rows(ts, dm)] * 3 + [_whole()] * 3 + [_hbm()],
        out_specs=[_rows(ts, ff), _rows(ts, ff), _rows(ts, dm), _rows(ts, dm), _rows(ts, dm), _rows(ts, dm),
                   pl.BlockSpec((SUBLANE, dm), lambda i: (0, 0))],
        out_shape=[tok(ff, BF16), tok(ff, BF16), tok(dm, BF16), tok(dm, BF16), tok(dm, BF16), tok(dm, F32),
                   jax.ShapeDtypeStruct((SUBLANE, dm), F32)],
        scratch_shapes=[pltpu.VMEM((ff, dm), BF16), pltpu.VMEM((ff, dm), BF16), pltpu.VMEM((ts, ff), F32),
                        pltpu.SemaphoreType.DMA((2 * N_DEV,))],
        compiler_params=_params(56),
    )(x1, mix, tgt, g1post, g2pre, g2post, slab)


def _bwd_mix(dmix, ya, yb, proj, va, vb, lng, lnb, slab, ts, deps):
    n_tok, dm = dmix.shape
    rows, off, _ = _layout(dm)
    n_steps = n_tok // ts

    def body(dmix_ref, ya_ref, yb_ref, bg_ref, za_ref, zb_ref, va_ref, vb_ref, lng_ref, lnb_ref, slab_ref,
             dpa_ref, dya_ref, dyb_ref, dva_ref, dvb_ref, small_ref, wa_v, wb_v, wo_v, sems):
        wo_copies = _weight_copies(slab_ref, off["wo"], rows["wo"], wo_v, sems, 2 * N_DEV)
        ab_copies = (_weight_copies(slab_ref, off["wa"], rows["wa"], wa_v, sems, 0)
                     + _weight_copies(slab_ref, off["wb"], rows["wb"], wb_v, sems, N_DEV))
        _on_first_step(wo_copies + ab_copies, "start")

        @pl.when(pl.program_id(0) == 0)
        def _():
            small_ref[...] = jnp.zeros_like(small_ref)

        _on_first_step(wo_copies, "wait")
        dmerged = _dot_nt(dmix_ref[...], wo_v[...])
        _on_first_step(ab_copies, "wait")
        sa = _sigmoid(za_ref[...].astype(F32))
        sg = _sigmoid(zb_ref[...].astype(F32))
        dza = dmerged * ya_ref[...].astype(F32) * sa * (1.0 - sa)
        dzb = dmerged * yb_ref[...].astype(F32) * sg * (1.0 - sg)
        dpa_ref[:, dm:2 * dm] = dza.astype(BF16)
        dpa_ref[:, 2 * dm:3 * dm] = dzb.astype(BF16)
        small_ref[5:6, :] += _colsum(dza)
        small_ref[6:7, :] += _colsum(dzb)

        dya = (dmerged * sa).astype(BF16)
        dya_ref[...] = dya
        dqa = _dot_nt(dya, wa_v[...])
        dbg = dqa * va_ref[...]
        dpa_ref[:, 0:dm] = dbg.astype(BF16)
        small_ref[4:5, :] += _colsum(dbg)
        dva = dqa * bg_ref[...].astype(F32)
        dva_ref[...] = dva
        small_ref[2:3, :] += _colsum(dva)

        dyb = (dmerged * sg).astype(BF16)
        dyb_ref[...] = dyb
        dsb = _dot_nt(dyb, wb_v[...])
        vb = vb_ref[...]
        xc = vb - jnp.mean(vb, axis=-1, keepdims=True)
        rstd = lax.rsqrt(jnp.mean(xc * xc, axis=-1, keepdims=True) + LN_EPS)
        nrm = xc * rstd
        lng_v = lng_ref[...]
        ln = nrm * lng_v + lnb_ref[...]
        sl = _sigmoid(ln)
        dln = dsb * (sl * (1.0 + ln * (1.0 - sl)))
        small_ref[0:1, :] += _colsum(dln * nrm)
        small_ref[1:2, :] += _colsum(dln)
        dn = dln * lng_v
        dvb = rstd * (dn - jnp.mean(dn, axis=-1, keepdims=True)
                      - nrm * jnp.mean(dn * nrm, axis=-1, keepdims=True))
        dvb_ref[...] = dvb
        small_ref[3:4, :] += _colsum(dvb)

    tok = lambda w, dt: jax.ShapeDtypeStruct((n_tok, w), dt)
    return pl.pallas_call(
        _after(body, deps), name="bwd_mix", grid=(n_steps,),
        in_specs=([_whole()] * len(deps) + [_rows(ts, dm)] * 3
                  + [_rows(ts, dm, 0), _rows(ts, dm, 5), _rows(ts, dm, 6)]
                  + [_rows(ts, dm)] * 2 + [_whole()] * 2 + [_hbm()]),
        out_specs=[_rows(ts, 3 * dm), _rows(ts, dm), _rows(ts, dm), _rows(ts, dm), _rows(ts, dm),
                   pl.BlockSpec((SUBLANE, dm), lambda i: (0, 0))],
        out_shape=[tok(3 * dm, BF16), tok(dm, BF16), tok(dm, BF16), tok(dm, F32), tok(dm, F32),
                   jax.ShapeDtypeStruct((SUBLANE, dm), F32)],
        scratch_shapes=[pltpu.VMEM((dm, dm), BF16), pltpu.VMEM((dm, dm), BF16), pltpu.VMEM((dm, dm), BF16),
                        pltpu.SemaphoreType.DMA((3 * N_DEV,))],
        compiler_params=_params(56),
    )(*deps, dmix, ya, yb, proj, proj, proj, va, vb, lng, lnb, slab)


def _bwd_conv(dva, dvb, p, u, proj, caw, cbw, ts, deps):
    n_tok, dm = dva.shape
    n_steps = n_tok // ts
    small_rows = 40

    def body(dva_ref, dva_prev, dva_next, dvb_ref, dvb_prev, dvb_next, p_ref, u_ref,
             cg_ref, ha_ref, a_ref, g_ref, caw_ref, cbw_ref,
             dproj_ref, small_ref,
             dp_v, du_v, tap_a, tap_b, gwa_v, gwb_v):
        i = pl.program_id(0)

        @pl.when(i == 0)
        def _():
            _broadcast_taps(caw_ref, tap_a, CONV_A)
            _broadcast_taps(cbw_ref, tap_b, CONV_B)
            small_ref[...] = jnp.zeros_like(small_ref)
            gwa_v[...] = jnp.zeros_like(gwa_v)
            gwb_v[...] = jnp.zeros_like(gwb_v)

        def emit_dp(r0, lanes, acc):
            dp_v[pl.ds(r0, acc.shape[0]), lanes] = acc

        def emit_du(r0, lanes, acc):
            du_v[pl.ds(r0, acc.shape[0]), lanes] = acc

        _conv_bwd_tile(_with_halos(dva_ref, dva_prev, dva_next, i, n_steps), p_ref, tap_a, gwa_v, CONV_A, ts, dm,
                       emit_dp)
        _conv_bwd_tile(_with_halos(dvb_ref, dvb_prev, dvb_next, i, n_steps), u_ref, tap_b, gwb_v, CONV_B, ts, dm,
                       emit_du)

        dp = dp_v[...]
        dcg = dp * ha_ref[...].astype(F32)
        dha = dp * cg_ref[...].astype(F32)
        du = du_v[...]
        sg = _sigmoid(g_ref[...].astype(F32))
        da = du * sg
        dg = du * a_ref[...].astype(F32) * sg * (1.0 - sg)
        dproj_ref[:, 0:dm] = dcg.astype(BF16)
        dproj_ref[:, dm:2 * dm] = dha.astype(BF16)
        dproj_ref[:, 2 * dm:3 * dm] = da.astype(BF16)
        dproj_ref[:, 3 * dm:4 * dm] = dg.astype(BF16)
        small_ref[3:4, :] += _colsum(dcg)
        small_ref[4:5, :] += _colsum(dha)
        small_ref[5:6, :] += _colsum(da)
        small_ref[6:7, :] += _colsum(dg)

        @pl.when(i == n_steps - 1)
        def _():
            for k in range(CONV_A):
                small_ref[k:k + 1, :] = _colsum(gwa_v[k])
            for k in range(CONV_B):
                small_ref[SUBLANE + k:SUBLANE + k + 1, :] = _colsum(gwb_v[k])

    return pl.pallas_call(
        _after(body, deps), name="bwd_conv", grid=(n_steps,),
        in_specs=([_whole()] * len(deps) + _halo_specs(ts, dm, n_tok) * 2 + [_rows(ts, dm)] * 2
                  + [_rows(ts, dm, 1), _rows(ts, dm, 2), _rows(ts, dm, 3), _rows(ts, dm, 4)]
                  + [_whole()] * 2),
        out_specs=[_rows(ts, 4 * dm), pl.BlockSpec((small_rows, dm), lambda i: (0, 0))],
        out_shape=[jax.ShapeDtypeStruct((n_tok, 4 * dm), BF16), jax.ShapeDtypeStruct((small_rows, dm), F32)],
        scratch_shapes=[pltpu.VMEM((ts, dm), F32), pltpu.VMEM((ts, dm), F32),
                        pltpu.VMEM((CONV_A, SUBLANE, dm), F32), pltpu.VMEM((CONV_B, SUBLANE, dm), F32),
                        pltpu.VMEM((CONV_A, SUBLANE, dm), F32), pltpu.VMEM((CONV_B, SUBLANE, dm), F32)],
        compiler_params=_params(48),
    )(*deps, dva, dva, dva, dvb, dvb, dvb, p, u, proj, proj, proj, proj, caw, cbw)


def _bwd_in(dpa, dpb, x, dx1, g1, slab, ts, deps):
    n_tok, dm = x.shape
    rows, off, _ = _layout(dm)
    width = 7 * dm

    def body(dpa_ref, dpb_ref, x_ref, dx1_ref, g1_ref, slab_ref, gx_ref, small_ref, w_v, sems):
        copies = _weight_copies(slab_ref, off["win"], rows["win"], w_v, sems, 0)
        _on_first_step(copies, "start")

        @pl.when(pl.program_id(0) == 0)
        def _():
            small_ref[...] = jnp.zeros_like(small_ref)

        _on_first_step(copies, "wait")
        dh = (_dot(dpa_ref[:, 0:dm], w_v[0:dm, :]) + _dot(dpb_ref[...], w_v[dm:5 * dm, :])
              + _dot(dpa_ref[:, dm:3 * dm], w_v[5 * dm:7 * dm, :]))
        xv = x_ref[...]
        r1 = _rms(xv)
        small_ref[0:1, :] += _colsum(dh * xv * r1)
        gx_ref[...] = dx1_ref[...] + _rms_bwd(dh, xv, r1, g1_ref[...])

    return pl.pallas_call(
        _after(body, deps), name="bwd_in", grid=(n_tok // ts,),
        in_specs=[_whole()] * len(deps) + [_rows(ts, 3 * dm), _rows(ts, 4 * dm), _rows(ts, dm), _rows(ts, dm),
                                           _whole(), _hbm()],
        out_specs=[_rows(ts, dm), pl.BlockSpec((SUBLANE, dm), lambda i: (0, 0))],
        out_shape=[jax.ShapeDtypeStruct((n_tok, dm), F32), jax.ShapeDtypeStruct((SUBLANE, dm), F32)],
        scratch_shapes=[pltpu.VMEM((width, dm), BF16), pltpu.SemaphoreType.DMA((N_DEV,))],
        compiler_params=_params(56),
    )(*deps, dpa, dpb, x, dx1, g1, slab)


def _wgrad(a, b, name, tm, tk, out_dtype):
    n_tok, m = a.shape
    n = b.shape[1]
    k_steps = n_tok // tk

    def body(a_ref, b_ref, o_ref, acc_v):
        k = pl.program_id(1)

        @pl.when(k == 0)
        def _():
            acc_v[...] = jnp.zeros_like(acc_v)

        acc_v[...] += _dot_tn(a_ref[...], b_ref[...])

        @pl.when(k == k_steps - 1)
        def _():
            o_ref[...] = acc_v[...].astype(o_ref.dtype)

    return pl.pallas_call(
        body, name=name, grid=(m // tm, k_steps),
        in_specs=[pl.BlockSpec((tk, tm), lambda i, k: (k, i)), pl.BlockSpec((tk, n), lambda i, k: (k, 0))],
        out_specs=pl.BlockSpec((tm, n), lambda i, k: (i, 0)),
        out_shape=pltpu.HBM((m, n), out_dtype),
        scratch_shapes=[pltpu.VMEM((tm, n), F32)],
        compiler_params=pltpu.CompilerParams(dimension_semantics=("arbitrary", "arbitrary"),
                                             vmem_limit_bytes=40 * MIB),
    )(a, b)


def _wgrad_in(dpa, dpb, h, tk):
    n_tok, dm = h.shape
    k_steps = n_tok // tk
    last = k_steps - 1

    def from_a(i):
        return (i == 0) | (i >= 5)

    def body(a_ref, b_ref, h_ref, o_ref, acc_v):
        i, k = pl.program_id(0), pl.program_id(1)

        @pl.when(k == 0)
        def _():
            acc_v[...] = jnp.zeros_like(acc_v)

        @pl.when(from_a(i))
        def _():
            acc_v[...] += _dot_tn(a_ref[...], h_ref[...])

        @pl.when(jnp.logical_not(from_a(i)))
        def _():
            acc_v[...] += _dot_tn(b_ref[...], h_ref[...])

        @pl.when(k == last)
        def _():
            o_ref[...] = acc_v[...].astype(o_ref.dtype)

    a_index = lambda i, k: (jnp.where(from_a(i), k, last), jnp.where(i >= 5, i - 4, 0))
    b_index = lambda i, k: (jnp.where(from_a(i), jnp.where(i == 0, 0, last), k), jnp.clip(i - 1, 0, 3))
    return pl.pallas_call(
        body, name="wgrad_in", grid=(7, k_steps),
        in_specs=[pl.BlockSpec((tk, dm), a_index), pl.BlockSpec((tk, dm), b_index),
                  pl.BlockSpec((tk, dm), lambda i, k: (k, 0))],
        out_specs=pl.BlockSpec((dm, dm), lambda i, k: (i, 0)),
        out_shape=pltpu.HBM((7 * dm, dm), BF16),
        scratch_shapes=[pltpu.VMEM((dm, dm), F32)],
        compiler_params=pltpu.CompilerParams(dimension_semantics=("arbitrary", "arbitrary"),
                                             vmem_limit_bytes=48 * MIB),
    )(dpa, dpb, h)


def _adamw(w, g, m, v):
    m = ADAM_B1 * m + (1.0 - ADAM_B1) * g
    v = ADAM_B2 * v + (1.0 - ADAM_B2) * (g * g)
    m_hat = m / (1.0 - ADAM_B1 ** ADAM_STEP)
    v_hat = v / (1.0 - ADAM_B2 ** ADAM_STEP)
    delta = -ADAM_LR * (m_hat / (jnp.sqrt(v_hat) + ADAM_EPS) + ADAM_WD * w)
    return delta, m, v


def _adam_big(recv, part, me, off, rows, w, m, v, transpose, name, tr):
    dm = recv.shape[2]
    per = rows // tr

    def body(me_ref, own_ref, r_ref, w_ref, m_ref, v_ref, g_ref, d_ref, mo_ref, vo_ref):
        g = own_ref[...].astype(F32)
        for k in range(len(FLIPS)):
            g = g + r_ref[k].astype(F32)
        if transpose:
            g = g.T
        delta, m_new, v_new = _adamw(w_ref[...], g, m_ref[...], v_ref[...])
        g_ref[...] = g
        d_ref[...] = delta
        mo_ref[...] = m_new
        vo_ref[...] = v_new

    if transpose:
        blk = pl.BlockSpec((dm, tr), lambda i, me_ref: (0, i))
    else:
        blk = pl.BlockSpec((tr, dm), lambda i, me_ref: (i, 0))
    first = off // tr
    return pl.pallas_call(
        body, name=name,
        grid_spec=pltpu.PrefetchScalarGridSpec(
            num_scalar_prefetch=1, grid=(per,),
            in_specs=[pl.BlockSpec((tr, dm), lambda i, me_ref: (me_ref[0] * per + i, 0)),
                      pl.BlockSpec((len(FLIPS), tr, dm), lambda i, me_ref: (0, first + i, 0)), blk, blk, blk],
            out_specs=[blk] * 4),
        out_shape=[jax.ShapeDtypeStruct(w.shape, F32)] * 4,
        compiler_params=_params(32),
    )(me, *[_in_hbm(a) for a in (part, recv, w, m, v)])


def _adam_rows_group(recv, parts, me, offs, rows, triples, name):
    dm = recv.shape[2]
    n = len(parts)

    def body(me_ref, *refs):
        ins, outs = refs[:5 * n], refs[5 * n:]
        for j in range(n):
            own_ref, r_ref, w_ref, m_ref, v_ref = ins[5 * j:5 * j + 5]
            g = own_ref[...].astype(F32)
            for k in range(len(FLIPS)):
                g = g + r_ref[k].astype(F32)
            delta, m_new, v_new = _adamw(w_ref[...], g, m_ref[...], v_ref[...])
            for ref, val in zip(outs[4 * j:4 * j + 4], (g, delta, m_new, v_new)):
                ref[...] = val

    blk = pl.BlockSpec((rows, dm), lambda i, me_ref: (0, 0))
    in_specs, args = [], []
    for part, off, triple in zip(parts, offs, triples):
        in_specs += [pl.BlockSpec((rows, dm), lambda i, me_ref: (me_ref[0], 0)),
                     pl.BlockSpec((len(FLIPS), rows, dm), lambda i, me_ref, first=off // rows: (0, first, 0)),
                     blk, blk, blk]
        args += [part, recv, *triple]
    out = pl.pallas_call(
        body, name=name,
        grid_spec=pltpu.PrefetchScalarGridSpec(num_scalar_prefetch=1, grid=(1,), in_specs=in_specs,
                                               out_specs=[blk] * (4 * n)),
        out_shape=[jax.ShapeDtypeStruct((rows, dm), F32)] * (4 * n),
        compiler_params=_params(48),
    )(me, *[_in_hbm(a) for a in args])
    return [tuple(out[4 * j:4 * j + 4]) for j in range(n)]


LOSS_ROW = 15
CONV_A_ROW = 16
CONV_B_ROW = 24


def _adam_small(recv_small, recv_last, small_own, last_own, me, params, d_model):
    n = len(params)
    cw = d_model // N_DEV

    def body(me_ref, r_ref, rc_ref, l_ref, so_ref, soc_ref, lo_ref, *refs):
        ins, loss_ref, outs = refs[:3 * n], refs[3 * n], refs[3 * n + 1:3 * n + 1 + 4 * n]
        g_v, gc_v, last_v = refs[3 * n + 1 + 4 * n:]
        me_pos = me_ref[0]

        def slot(d, own_ref, slots_ref):
            return jnp.where(d == me_pos, own_ref[...], slots_ref[d])

        g, gc, last = slot(0, so_ref, r_ref), slot(0, soc_ref, rc_ref), slot(0, lo_ref, l_ref)
        for d in range(1, N_DEV):
            g, gc, last = g + slot(d, so_ref, r_ref), gc + slot(d, soc_ref, rc_ref), last + slot(d, lo_ref, l_ref)
        g_v[...], gc_v[...], last_v[...] = g, gc, last
        loss_ref[...] = (0.5 / d_model) * jnp.sum(g_v[LOSS_ROW:LOSS_ROW + 1, :], axis=-1, keepdims=True)
        for j, (row0, own_columns, (w, _, _)) in enumerate(params):
            w_ref, m_ref, v_ref = ins[3 * j:3 * j + 3]
            source = gc_v if own_columns else (last_v if row0 == 0 else g_v)
            width = source.shape[1]
            for c in range(w.shape[1] // width):
                cols = slice(c * width, (c + 1) * width)
                grad = source[row0 + c * w.shape[0]:row0 + (c + 1) * w.shape[0], :]
                delta, m_new, v_new = _adamw(w_ref[:, cols], grad, m_ref[:, cols], v_ref[:, cols])
                for ref, val in zip(outs[4 * j:4 * j + 4], (grad, delta, m_new, v_new)):
                    ref[:, cols] = val

    full = lambda shape: pl.BlockSpec(shape, lambda i, me_ref: (0,) * len(shape))
    stack_rows = recv_small.shape[1]
    flat = [a for _, _, triple in params for a in triple]
    shapes = [w.shape for _, _, (w, _, _) in params for _ in range(4)]
    out = pl.pallas_call(
        body, name="adam_small",
        grid_spec=pltpu.PrefetchScalarGridSpec(
            num_scalar_prefetch=1, grid=(1,),
            in_specs=[full(recv_small.shape),
                      pl.BlockSpec((N_DEV, stack_rows, cw), lambda i, me_ref: (0, 0, me_ref[0])),
                      full(recv_last.shape), full(small_own.shape),
                      pl.BlockSpec((stack_rows, cw), lambda i, me_ref: (0, me_ref[0])),
                      full(last_own.shape)] + [full(a.shape) for a in flat],
            out_specs=[full((1, 1))] + [full(s) for s in shapes],
            scratch_shapes=[pltpu.VMEM((stack_rows, d_model), F32), pltpu.VMEM((stack_rows, cw), F32),
                            pltpu.VMEM(recv_last.shape[1:], F32)]),
        out_shape=[jax.ShapeDtypeStruct((1, 1), F32)] + [jax.ShapeDtypeStruct(s, F32) for s in shapes],
    )(me, *[_in_hbm(a) for a in (recv_small, recv_small, recv_last, small_own, small_own, last_own, *flat)])
    return out[0], [tuple(out[1 + 4 * j:5 + 4 * j]) for j in range(n)]


def _tile(n_tok, want):
    return min(want, n_tok)


def kernel(x, norm1_pre_g, w_in, b_in, conv_a_w, conv_a_b, w_a_out, conv_b_w, conv_b_b, ln_b_g, ln_b_b, w_b_out, w_o, norm1_post_g, norm2_pre_g, w_mlp_in, w_mlp_out, norm2_post_g, loss_target, m_norm1_pre_g, m_w_in, m_b_in, m_conv_a_w, m_conv_a_b, m_w_a_out, m_conv_b_w, m_conv_b_b, m_ln_b_g, m_ln_b_b, m_w_b_out, m_w_o, m_norm1_post_g, m_norm2_pre_g, m_w_mlp_in, m_w_mlp_out, m_norm2_post_g, v_norm1_pre_g, v_w_in, v_b_in, v_conv_a_w, v_conv_a_b, v_w_a_out, v_conv_b_w, v_conv_b_b, v_ln_b_g, v_ln_b_b, v_w_b_out, v_w_o, v_norm1_post_g, v_norm2_pre_g, v_w_mlp_in, v_w_mlp_out, v_norm2_post_g):
    n_tok, dm = x.shape[1], x.shape[2]
    rows, off, slab_rows = _layout(dm)
    cw = dm // N_DEV
    xs = x.reshape(n_tok, dm)
    tgt = loss_target.reshape(n_tok, dm)
    row = lambda vec: vec.reshape(1, -1)
    scattered = lambda group: jax.ShapeDtypeStruct((len(FLIPS), slab_rows[group], dm), BF16)
    tm, tk = min(dm, 1024), _tile(n_tok, 2048)
    me = (4 * lax.axis_index("x") + 2 * lax.axis_index("y") + lax.axis_index("c")).astype(jnp.int32).reshape(1)

    conv_own = jnp.concatenate([conv_a_w, jnp.zeros((SUBLANE - CONV_A, cw), F32), conv_b_w,
                                jnp.zeros((1, cw), F32)], axis=0)
    own_in, land_in = _place_cast([(w_in, True)], me, "place_w_in")
    slab_in, conv_all = _all_gather_two_level([own_in, conv_own], [land_in, None], "gather_w_in")
    conv_full = conv_all.transpose(1, 0, 2).reshape(conv_own.shape[0], dm)
    caw, cbw = conv_full[0:CONV_A], conv_full[SUBLANE:SUBLANE + CONV_B]
    own_abo, land_abo = _place_cast([(w_a_out, False), (w_b_out, False), (w_o, False)], me, "place_abo")
    own_mlp, land_mlp = _place_cast([(w_mlp_in, True), (w_mlp_out, False)], me, "place_mlp")
    ag_abo = _exchange_start([_Part(own_abo, False, slab_rows["abo"], 0, 0)], [land_abo],
                             "gather_abo_start", after=slab_in)
    ag_mlp = _exchange_start([_Part(own_mlp, False, slab_rows["mlp"], 0, 0)], [land_mlp],
                             "gather_mlp_start", after=ag_abo.token)

    proj, p, u, h = _fwd_in(xs, row(norm1_pre_g), row(b_in), slab_in, _tile(n_tok, 512),
                            [ag_abo.token, ag_mlp.token])
    _, (slab_abo,) = _exchange_wait(ag_abo, "gather_abo_wait", after=proj)
    va, vb, ya, yb, qa, sb, merged, mix, x1 = _fwd_mix(
        p, u, proj, xs, caw, row(conv_a_b), cbw, row(conv_b_b), row(ln_b_g), row(ln_b_b), row(norm1_post_g),
        slab_abo, _tile(n_tok, 256))
    _, (slab_mlp,) = _exchange_wait(ag_mlp, "gather_mlp_wait", after=x1)
    f, df1, h2, df2, dmix, dx1, small_mlp = _mlp_fwd_bwd(
        x1, mix, tgt, row(norm1_post_g), row(norm2_pre_g), row(norm2_post_g), slab_mlp, _tile(n_tok, 256))

    rs_mlp = _exchange_start(
        [_Part(_wgrad(df1, h2, "wgrad_mlp_in", tm, tk, BF16), True, rows["w1"], 0, off["w1"]),
         _Part(_wgrad(f, df2, "wgrad_mlp_out", tm, tk, BF16), True, rows["w2"], 0, off["w2"])],
        [scattered("mlp")], "scatter_mlp_start")
    dpa, dya, dyb, dva, dvb, small_mix = _bwd_mix(
        dmix, ya, yb, proj, va, vb, row(ln_b_g), row(ln_b_b), slab_abo, _tile(n_tok, 512), [rs_mlp.token])
    rs_abo = _exchange_start(
        [_Part(_wgrad(qa, dya, "wgrad_a_out", tm, tk, BF16), True, rows["wa"], 0, off["wa"]),
         _Part(_wgrad(sb, dyb, "wgrad_b_out", tm, tk, BF16), True, rows["wb"], 0, off["wb"]),
         _Part(_wgrad(merged, dmix, "wgrad_o", tm, tk, BF16), True, rows["wo"], 0, off["wo"])],
        [scattered("abo")], "scatter_abo_start")
    dpb, small_conv = _bwd_conv(dva, dvb, p, u, proj, caw, cbw, _tile(n_tok, 256), [rs_abo.token])

    zeros = lambda r: jnp.zeros((r, dm), F32)
    small = jnp.concatenate([
        zeros(1),
        small_mix[2:3],
        small_mix[3:4],
        small_mix[0:2],
        small_mlp[2:3],
        small_mlp[1:2],
        small_mlp[0:1],
        small_mix[4:5], small_conv[3:7], small_mix[5:7],
        small_mlp[3:4],
        small_conv[0:CONV_A], zeros(SUBLANE - CONV_A),
        small_conv[8:8 + CONV_B], zeros(1),
    ], axis=0)

    rs_in = _exchange_start(
        [_Part(_wgrad_in(dpa, dpb, h, tk), True, rows["win"], 0, off["win"]),
         _Part(small, False, small.shape[0], 1, 0)],
        [scattered("in"), jax.ShapeDtypeStruct((N_DEV,) + small.shape, F32)], "scatter_in_start")
    grad_x, small_in = _bwd_in(dpa, dpb, xs, dx1, row(norm1_pre_g), slab_in, _tile(n_tok, 512), [rs_in.token])

    tr = min(LANE, rows["wa"])
    (g_w1, g_w2), (recv_mlp,) = _exchange_wait(rs_mlp, "scatter_mlp_wait", after=grad_x)
    (g_wa, g_wb, g_wo), (recv_abo,) = _exchange_wait(rs_abo, "scatter_abo_wait", after=grad_x)
    big = {
        "w_mlp_in": _adam_big(recv_mlp, g_w1, me, off["w1"], rows["w1"], w_mlp_in, m_w_mlp_in, v_w_mlp_in, True,
                              "adam_w_mlp_in", tr),
        "w_mlp_out": _adam_big(recv_mlp, g_w2, me, off["w2"], rows["w2"], w_mlp_out, m_w_mlp_out, v_w_mlp_out,
                               False, "adam_w_mlp_out", tr),
    }
    big["w_a_out"], big["w_b_out"], big["w_o"] = _adam_rows_group(
        recv_abo, (g_wa, g_wb, g_wo), me, (off["wa"], off["wb"], off["wo"]), rows["wa"],
        ((w_a_out, m_w_a_out, v_w_a_out), (w_b_out, m_w_b_out, v_w_b_out), (w_o, m_w_o, v_w_o)), "adam_abo")
    (g_win, small), (recv_in, recv_small) = _exchange_wait(rs_in, "scatter_in_wait", after=big["w_o"][3])
    gather_last = _exchange_start([_Part(small_in, False, SUBLANE, 0, 0)],
                                  [jax.ShapeDtypeStruct((N_DEV, SUBLANE, dm), F32)], "gather_last_start",
                                  after=recv_in)
    big["w_in"] = _adam_big(recv_in, g_win, me, off["win"], rows["win"], w_in, m_w_in, v_w_in, True, "adam_w_in", tr)
    (small_in,), (recv_last,) = _exchange_wait(gather_last, "gather_last_wait", after=big["w_in"][3])

    small_names = ("norm1_pre_g", "conv_a_b", "conv_b_b", "ln_b_g", "ln_b_b", "norm1_post_g", "norm2_pre_g",
                   "norm2_post_g")
    given = dict(
        norm1_pre_g=(norm1_pre_g, m_norm1_pre_g, v_norm1_pre_g), conv_a_b=(conv_a_b, m_conv_a_b, v_conv_a_b),
        conv_b_b=(conv_b_b, m_conv_b_b, v_conv_b_b), ln_b_g=(ln_b_g, m_ln_b_g, v_ln_b_g),
        ln_b_b=(ln_b_b, m_ln_b_b, v_ln_b_b), norm1_post_g=(norm1_post_g, m_norm1_post_g, v_norm1_post_g),
        norm2_pre_g=(norm2_pre_g, m_norm2_pre_g, v_norm2_pre_g),
        norm2_post_g=(norm2_post_g, m_norm2_post_g, v_norm2_post_g))
    params = [(j, False, tuple(row(a) for a in given[name])) for j, name in enumerate(small_names)]
    params.append((SUBLANE, False, tuple(row(a) for a in (b_in, m_b_in, v_b_in))))
    params.append((CONV_A_ROW, True, (conv_a_w, m_conv_a_w, v_conv_a_w)))
    params.append((CONV_B_ROW, True, (conv_b_w, m_conv_b_w, v_conv_b_w)))
    loss, small_out = _adam_small(recv_small, recv_last, small, small_in, me, params, dm)
    small_leaves = {name: tuple(a.reshape(dm) for a in small_out[j]) for j, name in enumerate(small_names)}
    small_leaves["b_in"] = tuple(a.reshape(7 * dm) for a in small_out[len(small_names)])
    small_leaves["conv_a_w"] = small_out[len(small_names) + 1]
    small_leaves["conv_b_w"] = small_out[len(small_names) + 2]

    order = ("norm1_pre_g", "w_in", "b_in", "conv_a_w", "conv_a_b", "w_a_out", "conv_b_w", "conv_b_b", "ln_b_g",
             "ln_b_b", "w_b_out", "w_o", "norm1_post_g", "norm2_pre_g", "w_mlp_in", "w_mlp_out", "norm2_post_g")
    leaves = [big[name] if name in big else small_leaves[name] for name in order]
    grads, deltas, new_m, new_v = zip(*leaves)
    return (loss.reshape(()), grad_x.reshape(x.shape), *grads, *deltas, *new_m, *new_v)
```

```python
from typing import NamedTuple

import jax
import jax.numpy as jnp
from jax import lax
from jax.experimental import pallas as pl
from jax.experimental.pallas import tpu as pltpu

F32 = jnp.float32
BF16 = jnp.bfloat16

RMS_EPS = 1e-6
LN_EPS = 1e-5
ADAM_LR = 0.001
ADAM_B1 = 0.9
ADAM_B2 = 0.999
ADAM_EPS = 1e-08
ADAM_WD = 0.01
ADAM_STEP = 10

N_DEV = 8
CONV_A = 3
CONV_B = 31
LANE = 128
SUBLANE = 8
HALO = 16
FWD_CONV_ROWS = 32
BWD_CONV_ROWS = 64
MIB = 1 << 20
FLIPS = ((0, 0, 1), (0, 1, 0), (1, 0, 0), (0, 1, 1), (1, 0, 1), (1, 1, 0), (1, 1, 1))
MESH = pl.DeviceIdType.MESH


def _layout(d_model):
    e = d_model // N_DEV
    rows = {"win": 7 * e, "w1": 4 * e, "w2": 4 * e, "wa": e, "wb": e, "wo": e}
    off = {"win": 0, "w1": 0, "w2": 4 * e, "wa": 0, "wb": e, "wo": 2 * e}
    return rows, off, {"in": 7 * e, "mlp": 8 * e, "abo": 3 * e}


def _after(body, deps):
    def ordered(*refs):
        return body(*refs[len(deps):])
    return ordered


def _params(vmem_mib):
    return pltpu.CompilerParams(dimension_semantics=("arbitrary",), vmem_limit_bytes=vmem_mib * MIB)


def _whole():
    return pl.BlockSpec(memory_space=pltpu.VMEM)


def _hbm():
    return pl.BlockSpec(memory_space=pl.ANY)


def _in_hbm(a):
    return pltpu.with_memory_space_constraint(a, pltpu.HBM)


def _rows(ts, width, col=0):
    return pl.BlockSpec((ts, width), lambda i: (i, col))


def _halo_specs(ts, width, n_rows):
    per = ts // HALO
    last = n_rows // HALO - 1
    return [
        pl.BlockSpec((ts, width), lambda i: (i, 0)),
        pl.BlockSpec((HALO, width), lambda i: (jnp.maximum(i * per - 1, 0), 0)),
        pl.BlockSpec((HALO, width), lambda i: (jnp.minimum((i + 1) * per, last), 0)),
    ]


def _dot(a, b):
    return jnp.dot(a, b, preferred_element_type=F32)


def _dot_nt(a, b):
    return lax.dot_general(a, b, (((1,), (1,)), ((), ())), preferred_element_type=F32)


def _dot_tn(a, b):
    return lax.dot_general(a, b, (((0,), (0,)), ((), ())), preferred_element_type=F32)


def _rms(u):
    return lax.rsqrt(jnp.mean(u * u, axis=-1, keepdims=True) + RMS_EPS)


def _rms_bwd(dz, u, r, g):
    dzg = dz * g
    return r * dzg - u * (r * r * r) * jnp.mean(dzg * u, axis=-1, keepdims=True)


def _colsum(v):
    return jnp.sum(v, axis=0, keepdims=True)


def _sigmoid(v):
    return jax.nn.sigmoid(v)


def _weight_copies(slab_ref, off, rows, dst_ref, sems, first_sem):
    return [pltpu.make_async_copy(slab_ref.at[d, pl.ds(off, rows), :], dst_ref.at[pl.ds(d * rows, rows), :],
                                  sems.at[first_sem + d]) for d in range(N_DEV)]


def _on_first_step(copies, method):
    @pl.when(pl.program_id(0) == 0)
    def _():
        for cp in copies:
            getattr(cp, method)()


def _with_halos(main_ref, prev_ref, next_ref, i, n_steps):
    return (main_ref, jnp.where(i > 0, prev_ref[...], 0.0), jnp.where(i < n_steps - 1, next_ref[...], 0.0))


def _broadcast_taps(w_ref, wb_ref, n_taps):
    for k in range(n_taps):
        wb_ref[k] = jnp.broadcast_to(w_ref[k:k + 1, :], wb_ref.shape[1:])


def _conv_tile(tile, wb_ref, starts, ts, width, emit, rolled_ref, rows=FWD_CONV_ROWS):
    main_ref, prev, nxt = tile
    span = ts + 2 * HALO
    nv = rows // SUBLANE
    for cb in range(width // LANE):
        lanes = slice(cb * LANE, (cb + 1) * LANE)
        slot = cb % 2
        window = jnp.concatenate([prev[:, lanes], main_ref[:, lanes], nxt[:, lanes]], axis=0)
        for b in sorted({st % SUBLANE for st in starts}):
            rolled_ref[slot, b] = window if b == 0 else pltpu.roll(window, span - b, axis=0)
        for r0 in range(0, ts, rows):
            acc = jnp.zeros((nv, SUBLANE, LANE), F32)
            for k, st in enumerate(starts):
                shifted = rolled_ref[slot, st % SUBLANE, pl.ds(r0 + st - st % SUBLANE, rows), :]
                acc = acc + shifted.reshape(nv, SUBLANE, LANE) * wb_ref[k, :, lanes][None]
            emit(r0, pl.ds(cb * LANE, LANE), acc.reshape(rows, LANE))


def _window(tile, r0, cb, ts, rows):
    main_ref, prev, nxt = tile
    lanes = slice(cb * LANE, (cb + 1) * LANE)
    lo, hi = max(r0 - HALO, 0), min(r0 + rows + HALO, ts)
    pieces = [prev[:, lanes]] if r0 - HALO < 0 else []
    pieces.append(main_ref[lo:hi, lanes])
    if r0 + rows + HALO > ts:
        pieces.append(nxt[:, lanes])
    return pieces[0] if len(pieces) == 1 else jnp.concatenate(pieces, axis=0)


def _phases(starts):
    groups = {}
    for k, st in enumerate(starts):
        groups.setdefault(st % SUBLANE, []).append((k, st // SUBLANE))
    return sorted(groups.items())


def _shifted(blk, b):
    n = blk.shape[0]
    rolled = blk if b == 0 else pltpu.roll(blk, n - b, axis=0)
    return rolled.reshape(n // SUBLANE, SUBLANE, blk.shape[1])


def _conv_bwd_tile(dv_tile, u_ref, wb_ref, acc_ref, n_taps, ts, width, emit, rows=BWD_CONV_ROWS):
    groups = _phases(_bwd_starts(n_taps))
    nv = rows // SUBLANE
    for r0 in range(0, ts, rows):
        for cb in range(width // LANE):
            lanes = pl.ds(cb * LANE, LANE)
            blk = _window(dv_tile, r0, cb, ts, rows)
            u = u_ref[pl.ds(r0, rows), lanes].reshape(nv, SUBLANE, LANE)
            du = jnp.zeros((nv, SUBLANE, LANE), F32)
            for b, taps in groups:
                sh = _shifted(blk, b)
                for k, m in taps:
                    du = du + sh[m:m + nv] * wb_ref[k, :, lanes][None]
                    acc_ref[k, :, lanes] += jnp.sum(sh[m:m + nv] * u, axis=0)
            emit(r0, lanes, du.reshape(rows, LANE))


def _fwd_starts(n_taps):
    pad = (n_taps - 1) // 2
    return [HALO - pad + k for k in range(n_taps)]


def _bwd_starts(n_taps):
    pad = (n_taps - 1) // 2
    return [HALO + pad - k for k in range(n_taps)]


def _peer(x, y, c, flip):
    fx, fy, fc = flip
    return (1 - x if fx else x, 1 - y if fy else y, 1 - c if fc else c)


def _place_cast(pieces, me, name):
    n = len(pieces)
    counts = [a.shape[1] if t else a.shape[0] for a, t in pieces]
    width = pieces[0][0].shape[0] if pieces[0][1] else pieces[0][0].shape[1]
    total = sum(counts)

    def body(me_ref, *refs):
        ins, own_ref, land_ref = refs[:n], refs[n], refs[n + 1]
        first = 0
        for (a, transpose), in_ref, count in zip(pieces, ins, counts):
            block = (in_ref[...].T if transpose else in_ref[...]).astype(BF16)
            own_ref[first:first + count, :] = block
            land_ref[first:first + count, :] = block
            first += count

    return pl.pallas_call(
        body, name=name,
        grid_spec=pltpu.PrefetchScalarGridSpec(
            num_scalar_prefetch=1, grid=(1,),
            in_specs=[pl.BlockSpec(a.shape, lambda i, me_ref: (0, 0)) for a, _ in pieces],
            out_specs=[pl.BlockSpec((total, width), lambda i, me_ref: (0, 0)),
                       pl.BlockSpec((None, total, width), lambda i, me_ref: (me_ref[0], 0, 0))]),
        out_shape=[pltpu.HBM((total, width), BF16), pltpu.HBM((N_DEV, total, width), BF16)],
        compiler_params=_params(32),
    )(me, *[_in_hbm(a) for a, _ in pieces])


def _all_gather_two_level(shards, placed, name):
    n = len(shards)
    given = [j for j in range(n) if placed[j] is not None]

    def body(*refs):
        ins, outs = refs[:n], refs[n + len(given):2 * n + len(given)]
        send_sems, recv_sems, local_sems = refs[2 * n + len(given):]
        x, y, c = lax.axis_index("x"), lax.axis_index("y"), lax.axis_index("c")
        me, sibling = (x, y, c), (x, y, 1 - c)
        chips = [(1 - x, y), (x, 1 - y), (1 - x, 1 - y)]

        def slot(j, dev):
            return outs[j].at[4 * dev[0] + 2 * dev[1] + dev[2]]

        def copy(k, j, block, to, src=None):
            return pltpu.make_async_remote_copy(
                src_ref=slot(j, block) if src is None else src, dst_ref=slot(j, block),
                send_sem=send_sems.at[k * n + j], recv_sem=recv_sems.at[k * n + j], device_id=to, device_id_type=MESH)

        local = [pltpu.make_async_copy(ins[j], slot(j, me), local_sems.at[j]) for j in range(n) if j not in given]
        for cp in local:
            cp.start()
        first = [copy(0, j, me, sibling, src=ins[j]) for j in range(n)]
        first += [copy(1 + t, j, me, (*chip, c), src=ins[j]) for t, chip in enumerate(chips) for j in range(n)]
        for cp in first:
            cp.start()
        passed = []
        for t, chip in enumerate(chips):
            for j in range(n):
                copy(1 + t, j, (*chip, c), me).wait_recv()
                passed.append(copy(4 + t, j, (*chip, c), sibling))
                passed[-1].start()
        for j in range(n):
            copy(0, j, sibling, me).wait_recv()
        for t, chip in enumerate(chips):
            for j in range(n):
                copy(4 + t, j, (*chip, 1 - c), me).wait_recv()
        for cp in first + passed:
            cp.wait_send()
        for cp in local:
            cp.wait()

    return pl.pallas_call(
        body, name=name,
        out_shape=[jax.ShapeDtypeStruct((N_DEV,) + s.shape, s.dtype) for s in shards],
        in_specs=[_hbm()] * (n + len(given)), out_specs=[_hbm()] * n,
        input_output_aliases={n + i: j for i, j in enumerate(given)},
        scratch_shapes=[pltpu.SemaphoreType.DMA((7 * n,)), pltpu.SemaphoreType.DMA((7 * n,)),
                        pltpu.SemaphoreType.DMA((n,))],
    )(*shards, *[placed[j] for j in given])


class _Part(NamedTuple):
    src: jax.Array
    scatter: bool
    rows: int
    land: int
    off: int


class _Started(NamedTuple):
    send_sems: jax.Array
    recv_sems: jax.Array
    thru: tuple
    token: jax.Array
    parts: tuple


def _exchange_copies(srcs, lands, send_sems, recv_sems, parts):
    n = len(parts)
    x, y, c = lax.axis_index("x"), lax.axis_index("y"), lax.axis_index("c")
    me = 4 * x + 2 * y + c

    def block(j, dev):
        p = parts[j]
        return srcs[j].at[pl.ds(pl.multiple_of(dev * p.rows, SUBLANE), p.rows), :] if p.scatter else srcs[j]

    def slot(j, index):
        p = parts[j]
        return lands[p.land].at[index, pl.ds(p.off, p.rows), :]

    sends, recvs = [], []
    for k, flip in enumerate(FLIPS):
        px, py, pc = _peer(x, y, c, flip)
        peer = 4 * px + 2 * py + pc
        for j in range(n):
            sems = dict(send_sem=send_sems.at[k * n + j], recv_sem=recv_sems.at[k * n + j],
                        device_id=(px, py, pc), device_id_type=MESH)
            to, got = (k, k) if parts[j].scatter else (me, peer)
            sends.append(pltpu.make_async_remote_copy(src_ref=block(j, peer), dst_ref=slot(j, to), **sems))
            recvs.append(pltpu.make_async_remote_copy(src_ref=block(j, peer), dst_ref=slot(j, got), **sems))
    return sends, recvs


def _exchange_start(parts, lands, name, after=None):
    n, nl = len(parts), len(lands)
    n_in = n + nl + (after is not None)

    def body(*refs):
        srcs, land_refs = refs[:n], refs[n:n + nl]
        send_sems, recv_sems = refs[n_in], refs[n_in + 1]
        token = refs[n_in + 2 + n + nl]
        sends, _ = _exchange_copies(srcs, land_refs, send_sems, recv_sems, parts)
        for cp in sends:
            cp.start()
        token[...] = jnp.zeros_like(token)

    hbm = pl.BlockSpec(memory_space=pltpu.HBM)
    sem = pl.BlockSpec(memory_space=pltpu.SEMAPHORE)
    fresh = lambda s: lax.empty(s.shape, s.dtype) if isinstance(s, jax.ShapeDtypeStruct) else s
    args = [pltpu.with_memory_space_constraint(p.src, pltpu.HBM) for p in parts]
    args += [pltpu.with_memory_space_constraint(fresh(s), pltpu.HBM) for s in lands]
    args += [] if after is None else [after]
    out = pl.pallas_call(
        body, name=name,
        out_shape=(pltpu.SemaphoreType.DMA((7 * n,)), pltpu.SemaphoreType.DMA((7 * n,)),
                   *[pltpu.HBM(a.shape, a.dtype) for a in args[:n + nl]], jax.ShapeDtypeStruct((SUBLANE, LANE), F32)),
        in_specs=[hbm] * (n + nl) + [_hbm()] * (after is not None),
        out_specs=(sem, sem, *[hbm] * (n + nl), _whole()),
        input_output_aliases={j: 2 + j for j in range(n + nl)},
        compiler_params=pltpu.CompilerParams(has_side_effects=pltpu.SideEffectType.DATAFLOW_SIDE_EFFECTING),
    )(*args)
    return _Started(out[0], out[1], tuple(out[2:2 + n + nl]), out[2 + n + nl], tuple(parts))


def _exchange_wait(started, name, after):
    parts = started.parts
    n, nl = len(parts), len(started.thru) - len(parts)

    def body(*refs):
        srcs, land_refs = refs[:n], refs[n:n + nl]
        send_sems, recv_sems = refs[n + nl], refs[n + nl + 1]
        sends, recvs = _exchange_copies(srcs, land_refs, send_sems, recv_sems, parts)
        for cp in sends:
            cp.wait_send()
        for cp in recvs:
            cp.wait_recv()

    hbm = pl.BlockSpec(memory_space=pltpu.HBM)
    sem = pl.BlockSpec(memory_space=pltpu.SEMAPHORE)
    out = pl.pallas_call(
        body, name=name,
        out_shape=tuple(pltpu.HBM(a.shape, a.dtype) for a in started.thru),
        in_specs=[hbm] * (n + nl) + [sem, sem, _hbm()], out_specs=tuple([hbm] * (n + nl)),
        input_output_aliases={j: j for j in range(n + nl)},
        compiler_params=pltpu.CompilerParams(has_side_effects=pltpu.SideEffectType.DATAFLOW_SIDE_EFFECTING),
    )(*started.thru, started.send_sems, started.recv_sems, after)
    return list(out[:n]), list(out[n:])


def _fwd_in(x, g1, b_in, slab, ts, deps):
    n_tok, dm = x.shape
    rows, off, _ = _layout(dm)
    width = 7 * dm

    def body(x_ref, g1_ref, b_ref, slab_ref, proj_ref, p_ref, u_ref, h_ref, w_v, sems):
        copies = _weight_copies(slab_ref, off["win"], rows["win"], w_v, sems, 0)
        _on_first_step(copies, "start")
        _on_first_step(copies, "wait")
        xv = x_ref[...]
        h = (xv * _rms(xv) * g1_ref[...]).astype(BF16)
        h_ref[...] = h
        cols = []
        for j in range(7):
            pj = _dot_nt(h, w_v[pl.ds(j * dm, dm), :]) + b_ref[:, j * dm:(j + 1) * dm]
            proj_ref[:, j * dm:(j + 1) * dm] = pj.astype(proj_ref.dtype)
            if 1 <= j <= 4:
                cols.append(pj)
            if j == 2:
                p_ref[...] = cols[0] * cols[1]
            if j == 4:
                u_ref[...] = cols[2] * _sigmoid(cols[3])

    return pl.pallas_call(
        _after(body, deps), name="fwd_in", grid=(n_tok // ts,),
        in_specs=[_whole()] * len(deps) + [_rows(ts, dm), _whole(), _whole(), _hbm()],
        out_specs=[_rows(ts, width), _rows(ts, dm), _rows(ts, dm), _rows(ts, dm)],
        out_shape=[jax.ShapeDtypeStruct((n_tok, width), BF16), jax.ShapeDtypeStruct((n_tok, dm), F32),
                   jax.ShapeDtypeStruct((n_tok, dm), F32), jax.ShapeDtypeStruct((n_tok, dm), BF16)],
        scratch_shapes=[pltpu.VMEM((width, dm), BF16), pltpu.SemaphoreType.DMA((N_DEV,))],
        compiler_params=_params(56),
    )(*deps, x, g1, b_in, slab)


def _fwd_mix(p, u, proj, x, caw, cab, cbw, cbb, lng, lnb, g1post, slab, ts):
    n_tok, dm = x.shape
    rows, off, _ = _layout(dm)
    n_steps = n_tok // ts

    def body(p_ref, p_prev, p_next, u_ref, u_prev, u_next, bg_ref, za_ref, zb_ref, x_ref,
             caw_ref, cab_ref, cbw_ref, cbb_ref, lng_ref, lnb_ref, g1p_ref, slab_ref,
             va_ref, vb_ref, ya_ref, yb_ref, qa_ref, sb_ref, mg_ref, mix_ref, x1_ref,
             wa_v, wb_v, wo_v, tap_a, tap_b, sems, rolled_a, rolled_b):
        i = pl.program_id(0)
        copies = (_weight_copies(slab_ref, off["wa"], rows["wa"], wa_v, sems, 0)
                  + _weight_copies(slab_ref, off["wb"], rows["wb"], wb_v, sems, N_DEV)
                  + _weight_copies(slab_ref, off["wo"], rows["wo"], wo_v, sems, 2 * N_DEV))
        _on_first_step(copies, "start")

        @pl.when(i == 0)
        def _():
            _broadcast_taps(caw_ref, tap_a, CONV_A)
            _broadcast_taps(cbw_ref, tap_b, CONV_B)

        def emit_a(r0, lanes, acc):
            va_ref[pl.ds(r0, acc.shape[0]), lanes] = acc + cab_ref[:, lanes]

        def emit_b(r0, lanes, acc):
            vb_ref[pl.ds(r0, acc.shape[0]), lanes] = acc + cbb_ref[:, lanes]

        _conv_tile(_with_halos(p_ref, p_prev, p_next, i, n_steps), tap_a, _fwd_starts(CONV_A), ts, dm, emit_a,
                   rolled_a)
        _conv_tile(_with_halos(u_ref, u_prev, u_next, i, n_steps), tap_b, _fwd_starts(CONV_B), ts, dm, emit_b,
                   rolled_b)
        _on_first_step(copies, "wait")

        qa = (bg_ref[...].astype(F32) * va_ref[...]).astype(BF16)
        qa_ref[...] = qa
        ya = _dot(qa, wa_v[...])
        vb = vb_ref[...]
        xc = vb - jnp.mean(vb, axis=-1, keepdims=True)
        rstd = lax.rsqrt(jnp.mean(xc * xc, axis=-1, keepdims=True) + LN_EPS)
        ln = xc * rstd * lng_ref[...] + lnb_ref[...]
        sb = (ln * _sigmoid(ln)).astype(BF16)
        sb_ref[...] = sb
        yb = _dot(sb, wb_v[...])
        ya_ref[...] = ya.astype(BF16)
        yb_ref[...] = yb.astype(BF16)
        merged = (_sigmoid(za_ref[...].astype(F32)) * ya + _sigmoid(zb_ref[...].astype(F32)) * yb).astype(BF16)
        mg_ref[...] = merged
        mix = _dot(merged, wo_v[...])
        mix_ref[...] = mix
        x1_ref[...] = x_ref[...] + mix * _rms(mix) * g1p_ref[...]

    tok = lambda dt: jax.ShapeDtypeStruct((n_tok, dm), dt)
    return pl.pallas_call(
        body, name="fwd_mix", grid=(n_steps,),
        in_specs=(_halo_specs(ts, dm, n_tok) + _halo_specs(ts, dm, n_tok)
                  + [_rows(ts, dm, 0), _rows(ts, dm, 5), _rows(ts, dm, 6), _rows(ts, dm)]
                  + [_whole()] * 7 + [_hbm()]),
        out_specs=[_rows(ts, dm)] * 9,
        out_shape=[tok(F32), tok(F32), tok(BF16), tok(BF16), tok(BF16), tok(BF16), tok(BF16), tok(F32), tok(F32)],
        scratch_shapes=[pltpu.VMEM((dm, dm), BF16), pltpu.VMEM((dm, dm), BF16), pltpu.VMEM((dm, dm), BF16),
                        pltpu.VMEM((CONV_A, SUBLANE, dm), F32), pltpu.VMEM((CONV_B, SUBLANE, dm), F32),
                        pltpu.SemaphoreType.DMA((3 * N_DEV,)),
                        pltpu.VMEM((2, SUBLANE, ts + 2 * HALO, LANE), F32),
                        pltpu.VMEM((2, SUBLANE, ts + 2 * HALO, LANE), F32)],
        compiler_params=_params(48),
    )(p, p, p, u, u, u, proj, proj, proj, x, caw, cab, cbw, cbb, lng, lnb, g1post, slab)


def _mlp_fwd_bwd(x1, mix, tgt, g1post, g2pre, g2post, slab, ts):
    n_tok, dm = x1.shape
    rows, off, _ = _layout(dm)
    ff = 4 * dm

    def body(x1_ref, mix_ref, t_ref, g1p_ref, g2pre_ref, g2post_ref, slab_ref,
             f_ref, df1_ref, h2_ref, df2_ref, dmix_ref, dx1_ref, small_ref, w1_v, w2_v, relu_v, sems):
        w1_copies = _weight_copies(slab_ref, off["w1"], rows["w1"], w1_v, sems, 0)
        w2_copies = _weight_copies(slab_ref, off["w2"], rows["w2"], w2_v, sems, N_DEV)
        _on_first_step(w1_copies + w2_copies, "start")

        @pl.when(pl.program_id(0) == 0)
        def _():
            small_ref[...] = jnp.zeros_like(small_ref)

        _on_first_step(w1_copies + w2_copies, "wait")
        x1v = x1_ref[...]
        r3 = _rms(x1v)
        g2pre = g2pre_ref[...]
        h2 = (x1v * r3 * g2pre).astype(BF16)
        h2_ref[...] = h2
        for c in range(4):
            blk = pl.ds(c * dm, dm)
            relu = jnp.maximum(_dot_nt(h2, w1_v[blk, :]), 0.0)
            relu_v[:, c * dm:(c + 1) * dm] = relu
            f_ref[:, c * dm:(c + 1) * dm] = (relu * relu).astype(BF16)
        f2 = _dot(f_ref[...], w2_v[...])
        r4 = _rms(f2)
        g2post = g2post_ref[...]
        err = x1v + f2 * r4 * g2post - t_ref[...]
        dy = err * (1.0 / dm)
        small_ref[3:4, :] += _colsum(err * err)
        small_ref[0:1, :] += _colsum(dy * f2 * r4)
        df2 = _rms_bwd(dy, f2, r4, g2post).astype(BF16)
        df2_ref[...] = df2
        for c in range(4):
            blk = pl.ds(c * dm, dm)
            df1 = (_dot_nt(df2, w2_v[blk, :]) * (2.0 * relu_v[:, c * dm:(c + 1) * dm])).astype(BF16)
            df1_ref[:, c * dm:(c + 1) * dm] = df1
        dh2 = _dot(df1_ref[...], w1_v[...])
        small_ref[1:2, :] += _colsum(dh2 * x1v * r3)
        dx1 = dy + _rms_bwd(dh2, x1v, r3, g2pre)
        dx1_ref[...] = dx1
        mixv = mix_ref[...]
        r2 = _rms(mixv)
        small_ref[2:3, :] += _colsum(dx1 * mixv * r2)
        dmix_ref[...] = _rms_bwd(dx1, mixv, r2, g1p_ref[...]).astype(BF16)

    tok = lambda w, dt: jax.ShapeDtypeStruct((n_tok, w), dt)
    return pl.pallas_call(
        body, name="mlp_fwd_bwd", grid=(n_tok // ts,),
        in_specs=[_rows(ts, dm)] * 3 + [_whole()] * 3 + [_hbm()],
        out_specs=[_rows(ts, ff), _rows(ts, ff), _rows(ts, dm), _rows(ts, dm), _rows(ts, dm), _rows(ts, dm),
                   pl.BlockSpec((SUBLANE, dm), lambda i: (0, 0))],
        out_shape=[tok(ff, BF16), tok(ff, BF16), tok(dm, BF16), tok(dm, BF16), tok(dm, BF16), tok(dm, F32),
                   jax.ShapeDtypeStruct((SUBLANE, dm), F32)],
        scratch_shapes=[pltpu.VMEM((ff, dm), BF16), pltpu.VMEM((ff, dm), BF16), pltpu.VMEM((ts, ff), F32),
                        pltpu.SemaphoreType.DMA((2 * N_DEV,))],
        compiler_params=_params(56),
    )(x1, mix, tgt, g1post, g2pre, g2post, slab)


def _bwd_mix(dmix, ya, yb, proj, va, vb, lng, lnb, slab, ts, deps):
    n_tok, dm = dmix.shape
    rows, off, _ = _layout(dm)
    n_steps = n_tok // ts

    def body(dmix_ref, ya_ref, yb_ref, bg_ref, za_ref, zb_ref, va_ref, vb_ref, lng_ref, lnb_ref, slab_ref,
             dpa_ref, dya_ref, dyb_ref, dva_ref, dvb_ref, small_ref, wa_v, wb_v, wo_v, sems):
        wo_copies = _weight_copies(slab_ref, off["wo"], rows["wo"], wo_v, sems, 2 * N_DEV)
        ab_copies = (_weight_copies(slab_ref, off["wa"], rows["wa"], wa_v, sems, 0)
                     + _weight_copies(slab_ref, off["wb"], rows["wb"], wb_v, sems, N_DEV))
        _on_first_step(wo_copies + ab_copies, "start")

        @pl.when(pl.program_id(0) == 0)
        def _():
            small_ref[...] = jnp.zeros_like(small_ref)

        _on_first_step(wo_copies, "wait")
        dmerged = _dot_nt(dmix_ref[...], wo_v[...])
        _on_first_step(ab_copies, "wait")
        sa = _sigmoid(za_ref[...].astype(F32))
        sg = _sigmoid(zb_ref[...].astype(F32))
        dza = dmerged * ya_ref[...].astype(F32) * sa * (1.0 - sa)
        dzb = dmerged * yb_ref[...].astype(F32) * sg * (1.0 - sg)
        dpa_ref[:, dm:2 * dm] = dza.astype(BF16)
        dpa_ref[:, 2 * dm:3 * dm] = dzb.astype(BF16)
        small_ref[5:6, :] += _colsum(dza)
        small_ref[6:7, :] += _colsum(dzb)

        dya = (dmerged * sa).astype(BF16)
        dya_ref[...] = dya
        dqa = _dot_nt(dya, wa_v[...])
        dbg = dqa * va_ref[...]
        dpa_ref[:, 0:dm] = dbg.astype(BF16)
        small_ref[4:5, :] += _colsum(dbg)
        dva = dqa * bg_ref[...].astype(F32)
        dva_ref[...] = dva
        small_ref[2:3, :] += _colsum(dva)

        dyb = (dmerged * sg).astype(BF16)
        dyb_ref[...] = dyb
        dsb = _dot_nt(dyb, wb_v[...])
        vb = vb_ref[...]
        xc = vb - jnp.mean(vb, axis=-1, keepdims=True)
        rstd = lax.rsqrt(jnp.mean(xc * xc, axis=-1, keepdims=True) + LN_EPS)
        nrm = xc * rstd
        lng_v = lng_ref[...]
        ln = nrm * lng_v + lnb_ref[...]
        sl = _sigmoid(ln)
        dln = dsb * (sl * (1.0 + ln * (1.0 - sl)))
        small_ref[0:1, :] += _colsum(dln * nrm)
        small_ref[1:2, :] += _colsum(dln)
        dn = dln * lng_v
        dvb = rstd * (dn - jnp.mean(dn, axis=-1, keepdims=True)
                      - nrm * jnp.mean(dn * nrm, axis=-1, keepdims=True))
        dvb_ref[...] = dvb
        small_ref[3:4, :] += _colsum(dvb)

    tok = lambda w, dt: jax.ShapeDtypeStruct((n_tok, w), dt)
    return pl.pallas_call(
        _after(body, deps), name="bwd_mix", grid=(n_steps,),
        in_specs=([_whole()] * len(deps) + [_rows(ts, dm)] * 3
                  + [_rows(ts, dm, 0), _rows(ts, dm, 5), _rows(ts, dm, 6)]
                  + [_rows(ts, dm)] * 2 + [_whole()] * 2 + [_hbm()]),
        out_specs=[_rows(ts, 3 * dm), _rows(ts, dm), _rows(ts, dm), _rows(ts, dm), _rows(ts, dm),
                   pl.BlockSpec((SUBLANE, dm), lambda i: (0, 0))],
        out_shape=[tok(3 * dm, BF16), tok(dm, BF16), tok(dm, BF16), tok(dm, F32), tok(dm, F32),
                   jax.ShapeDtypeStruct((SUBLANE, dm), F32)],
        scratch_shapes=[pltpu.VMEM((dm, dm), BF16), pltpu.VMEM((dm, dm), BF16), pltpu.VMEM((dm, dm), BF16),
                        pltpu.SemaphoreType.DMA((3 * N_DEV,))],
        compiler_params=_params(56),
    )(*deps, dmix, ya, yb, proj, proj, proj, va, vb, lng, lnb, slab)


def _bwd_conv(dva, dvb, p, u, proj, caw, cbw, ts, deps):
    n_tok, dm = dva.shape
    n_steps = n_tok // ts
    small_rows = 40

    def body(dva_ref, dva_prev, dva_next, dvb_ref, dvb_prev, dvb_next, p_ref, u_ref,
             cg_ref, ha_ref, a_ref, g_ref, caw_ref, cbw_ref,
             dproj_ref, small_ref,
             dp_v, du_v, tap_a, tap_b, gwa_v, gwb_v):
        i = pl.program_id(0)

        @pl.when(i == 0)
        def _():
            _broadcast_taps(caw_ref, tap_a, CONV_A)
            _broadcast_taps(cbw_ref, tap_b, CONV_B)
            small_ref[...] = jnp.zeros_like(small_ref)
            gwa_v[...] = jnp.zeros_like(gwa_v)
            gwb_v[...] = jnp.zeros_like(gwb_v)

        def emit_dp(r0, lanes, acc):
            dp_v[pl.ds(r0, acc.shape[0]), lanes] = acc

        def emit_du(r0, lanes, acc):
            du_v[pl.ds(r0, acc.shape[0]), lanes] = acc

        _conv_bwd_tile(_with_halos(dva_ref, dva_prev, dva_next, i, n_steps), p_ref, tap_a, gwa_v, CONV_A, ts, dm,
                       emit_dp)
        _conv_bwd_tile(_with_halos(dvb_ref, dvb_prev, dvb_next, i, n_steps), u_ref, tap_b, gwb_v, CONV_B, ts, dm,
                       emit_du)

        dp = dp_v[...]
        dcg = dp * ha_ref[...].astype(F32)
        dha = dp * cg_ref[...].astype(F32)
        du = du_v[...]
        sg = _sigmoid(g_ref[...].astype(F32))
        da = du * sg
        dg = du * a_ref[...].astype(F32) * sg * (1.0 - sg)
        dproj_ref[:, 0:dm] = dcg.astype(BF16)
        dproj_ref[:, dm:2 * dm] = dha.astype(BF16)
        dproj_ref[:, 2 * dm:3 * dm] = da.astype(BF16)
        dproj_ref[:, 3 * dm:4 * dm] = dg.astype(BF16)
        small_ref[3:4, :] += _colsum(dcg)
        small_ref[4:5, :] += _colsum(dha)
        small_ref[5:6, :] += _colsum(da)
        small_ref[6:7, :] += _colsum(dg)

        @pl.when(i == n_steps - 1)
        def _():
            for k in range(CONV_A):
                small_ref[k:k + 1, :] = _colsum(gwa_v[k])
            for k in range(CONV_B):
                small_ref[SUBLANE + k:SUBLANE + k + 1, :] = _colsum(gwb_v[k])

    return pl.pallas_call(
        _after(body, deps), name="bwd_conv", grid=(n_steps,),
        in_specs=([_whole()] * len(deps) + _halo_specs(ts, dm, n_tok) * 2 + [_rows(ts, dm)] * 2
                  + [_rows(ts, dm, 1), _rows(ts, dm, 2), _rows(ts, dm, 3), _rows(ts, dm, 4)]
                  + [_whole()] * 2),
        out_specs=[_rows(ts, 4 * dm), pl.BlockSpec((small_rows, dm), lambda i: (0, 0))],
        out_shape=[jax.ShapeDtypeStruct((n_tok, 4 * dm), BF16), jax.ShapeDtypeStruct((small_rows, dm), F32)],
        scratch_shapes=[pltpu.VMEM((ts, dm), F32), pltpu.VMEM((ts, dm), F32),
                        pltpu.VMEM((CONV_A, SUBLANE, dm), F32), pltpu.VMEM((CONV_B, SUBLANE, dm), F32),
                        pltpu.VMEM((CONV_A, SUBLANE, dm), F32), pltpu.VMEM((CONV_B, SUBLANE, dm), F32)],
        compiler_params=_params(48),
    )(*deps, dva, dva, dva, dvb, dvb, dvb, p, u, proj, proj, proj, proj, caw, cbw)


def _bwd_in(dpa, dpb, x, dx1, g1, slab, ts, deps):
    n_tok, dm = x.shape
    rows, off, _ = _layout(dm)
    width = 7 * dm

    def body(dpa_ref, dpb_ref, x_ref, dx1_ref, g1_ref, slab_ref, gx_ref, small_ref, w_v, sems):
        copies = _weight_copies(slab_ref, off["win"], rows["win"], w_v, sems, 0)
        _on_first_step(copies, "start")

        @pl.when(pl.program_id(0) == 0)
        def _():
            small_ref[...] = jnp.zeros_like(small_ref)

        _on_first_step(copies, "wait")
        dh = (_dot(dpa_ref[:, 0:dm], w_v[0:dm, :]) + _dot(dpb_ref[...], w_v[dm:5 * dm, :])
              + _dot(dpa_ref[:, dm:3 * dm], w_v[5 * dm:7 * dm, :]))
        xv = x_ref[...]
        r1 = _rms(xv)
        small_ref[0:1, :] += _colsum(dh * xv * r1)
        gx_ref[...] = dx1_ref[...] + _rms_bwd(dh, xv, r1, g1_ref[...])

    return pl.pallas_call(
        _after(body, deps), name="bwd_in", grid=(n_tok // ts,),
        in_specs=[_whole()] * len(deps) + [_rows(ts, 3 * dm), _rows(ts, 4 * dm), _rows(ts, dm), _rows(ts, dm),
                                           _whole(), _hbm()],
        out_specs=[_rows(ts, dm), pl.BlockSpec((SUBLANE, dm), lambda i: (0, 0))],
        out_shape=[jax.ShapeDtypeStruct((n_tok, dm), F32), jax.ShapeDtypeStruct((SUBLANE, dm), F32)],
        scratch_shapes=[pltpu.VMEM((width, dm), BF16), pltpu.SemaphoreType.DMA((N_DEV,))],
        compiler_params=_params(56),
    )(*deps, dpa, dpb, x, dx1, g1, slab)


def _wgrad(a, b, name, tm, tk, out_dtype):
    n_tok, m = a.shape
    n = b.shape[1]
    k_steps = n_tok // tk

    def body(a_ref, b_ref, o_ref, acc_v):
        k = pl.program_id(1)

        @pl.when(k == 0)
        def _():
            acc_v[...] = jnp.zeros_like(acc_v)

        acc_v[...] += _dot_tn(a_ref[...], b_ref[...])

        @pl.when(k == k_steps - 1)
        def _():
            o_ref[...] = acc_v[...].astype(o_ref.dtype)

    return pl.pallas_call(
        body, name=name, grid=(m // tm, k_steps),
        in_specs=[pl.BlockSpec((tk, tm), lambda i, k: (k, i)), pl.BlockSpec((tk, n), lambda i, k: (k, 0))],
        out_specs=pl.BlockSpec((tm, n), lambda i, k: (i, 0)),
        out_shape=pltpu.HBM((m, n), out_dtype),
        scratch_shapes=[pltpu.VMEM((tm, n), F32)],
        compiler_params=pltpu.CompilerParams(dimension_semantics=("arbitrary", "arbitrary"),
                                             vmem_limit_bytes=40 * MIB),
    )(a, b)


def _wgrad_in(dpa, dpb, h, tk):
    n_tok, dm = h.shape
    k_steps = n_tok // tk
    last = k_steps - 1

    def from_a(i):
        return (i == 0) | (i >= 5)

    def body(a_ref, b_ref, h_ref, o_ref, acc_v):
        i, k = pl.program_id(0), pl.program_id(1)

        @pl.when(k == 0)
        def _():
            acc_v[...] = jnp.zeros_like(acc_v)

        @pl.when(from_a(i))
        def _():
            acc_v[...] += _dot_tn(a_ref[...], h_ref[...])

        @pl.when(jnp.logical_not(from_a(i)))
        def _():
            acc_v[...] += _dot_tn(b_ref[...], h_ref[...])

        @pl.when(k == last)
        def _():
            o_ref[...] = acc_v[...].astype(o_ref.dtype)

    a_index = lambda i, k: (jnp.where(from_a(i), k, last), jnp.where(i >= 5, i - 4, 0))
    b_index = lambda i, k: (jnp.where(from_a(i), jnp.where(i == 0, 0, last), k), jnp.clip(i - 1, 0, 3))
    return pl.pallas_call(
        body, name="wgrad_in", grid=(7, k_steps),
        in_specs=[pl.BlockSpec((tk, dm), a_index), pl.BlockSpec((tk, dm), b_index),
                  pl.BlockSpec((tk, dm), lambda i, k: (k, 0))],
        out_specs=pl.BlockSpec((dm, dm), lambda i, k: (i, 0)),
        out_shape=pltpu.HBM((7 * dm, dm), BF16),
        scratch_shapes=[pltpu.VMEM((dm, dm), F32)],
        compiler_params=pltpu.CompilerParams(dimension_semantics=("arbitrary", "arbitrary"),
                                             vmem_limit_bytes=48 * MIB),
    )(dpa, dpb, h)


def _adamw(w, g, m, v):
    m = ADAM_B1 * m + (1.0 - ADAM_B1) * g
    v = ADAM_B2 * v + (1.0 - ADAM_B2) * (g * g)
    m_hat = m / (1.0 - ADAM_B1 ** ADAM_STEP)
    v_hat = v / (1.0 - ADAM_B2 ** ADAM_STEP)
    delta = -ADAM_LR * (m_hat / (jnp.sqrt(v_hat) + ADAM_EPS) + ADAM_WD * w)
    return delta, m, v


def _adam_big(recv, part, me, off, rows, w, m, v, transpose, name, tr):
    dm = recv.shape[2]
    per = rows // tr

    def body(me_ref, own_ref, r_ref, w_ref, m_ref, v_ref, g_ref, d_ref, mo_ref, vo_ref):
        g = own_ref[...].astype(F32)
        for k in range(len(FLIPS)):
            g = g + r_ref[k].astype(F32)
        if transpose:
            g = g.T
        delta, m_new, v_new = _adamw(w_ref[...], g, m_ref[...], v_ref[...])
        g_ref[...] = g
        d_ref[...] = delta
        mo_ref[...] = m_new
        vo_ref[...] = v_new

    if transpose:
        blk = pl.BlockSpec((dm, tr), lambda i, me_ref: (0, i))
    else:
        blk = pl.BlockSpec((tr, dm), lambda i, me_ref: (i, 0))
    first = off // tr
    return pl.pallas_call(
        body, name=name,
        grid_spec=pltpu.PrefetchScalarGridSpec(
            num_scalar_prefetch=1, grid=(per,),
            in_specs=[pl.BlockSpec((tr, dm), lambda i, me_ref: (me_ref[0] * per + i, 0)),
                      pl.BlockSpec((len(FLIPS), tr, dm), lambda i, me_ref: (0, first + i, 0)), blk, blk, blk],
            out_specs=[blk] * 4),
        out_shape=[jax.ShapeDtypeStruct(w.shape, F32)] * 4,
        compiler_params=_params(32),
    )(me, *[_in_hbm(a) for a in (part, recv, w, m, v)])


def _adam_rows_group(recv, parts, me, offs, rows, triples, name):
    dm = recv.shape[2]
    n = len(parts)

    def body(me_ref, *refs):
        ins, outs = refs[:5 * n], refs[5 * n:]
        for j in range(n):
            own_ref, r_ref, w_ref, m_ref, v_ref = ins[5 * j:5 * j + 5]
            g = own_ref[...].astype(F32)
            for k in range(len(FLIPS)):
                g = g + r_ref[k].astype(F32)
            delta, m_new, v_new = _adamw(w_ref[...], g, m_ref[...], v_ref[...])
            for ref, val in zip(outs[4 * j:4 * j + 4], (g, delta, m_new, v_new)):
                ref[...] = val

    blk = pl.BlockSpec((rows, dm), lambda i, me_ref: (0, 0))
    in_specs, args = [], []
    for part, off, triple in zip(parts, offs, triples):
        in_specs += [pl.BlockSpec((rows, dm), lambda i, me_ref: (me_ref[0], 0)),
                     pl.BlockSpec((len(FLIPS), rows, dm), lambda i, me_ref, first=off // rows: (0, first, 0)),
                     blk, blk, blk]
        args += [part, recv, *triple]
    out = pl.pallas_call(
        body, name=name,
        grid_spec=pltpu.PrefetchScalarGridSpec(num_scalar_prefetch=1, grid=(1,), in_specs=in_specs,
                                               out_specs=[blk] * (4 * n)),
        out_shape=[jax.ShapeDtypeStruct((rows, dm), F32)] * (4 * n),
        compiler_params=_params(48),
    )(me, *[_in_hbm(a) for a in args])
    return [tuple(out[4 * j:4 * j + 4]) for j in range(n)]


LOSS_ROW = 15
CONV_A_ROW = 16
CONV_B_ROW = 24


def _adam_small(recv_small, recv_last, small_own, last_own, me, params, d_model):
    n = len(params)
    cw = d_model // N_DEV

    def body(me_ref, r_ref, rc_ref, l_ref, so_ref, soc_ref, lo_ref, *refs):
        ins, loss_ref, outs = refs[:3 * n], refs[3 * n], refs[3 * n + 1:3 * n + 1 + 4 * n]
        g_v, gc_v, last_v = refs[3 * n + 1 + 4 * n:]
        me_pos = me_ref[0]

        def slot(d, own_ref, slots_ref):
            return jnp.where(d == me_pos, own_ref[...], slots_ref[d])

        g, gc, last = slot(0, so_ref, r_ref), slot(0, soc_ref, rc_ref), slot(0, lo_ref, l_ref)
        for d in range(1, N_DEV):
            g, gc, last = g + slot(d, so_ref, r_ref), gc + slot(d, soc_ref, rc_ref), last + slot(d, lo_ref, l_ref)
        g_v[...], gc_v[...], last_v[...] = g, gc, last
        loss_ref[...] = (0.5 / d_model) * jnp.sum(g_v[LOSS_ROW:LOSS_ROW + 1, :], axis=-1, keepdims=True)
        for j, (row0, own_columns, (w, _, _)) in enumerate(params):
            w_ref, m_ref, v_ref = ins[3 * j:3 * j + 3]
            source = gc_v if own_columns else (last_v if row0 == 0 else g_v)
            width = source.shape[1]
            for c in range(w.shape[1] // width):
                cols = slice(c * width, (c + 1) * width)
                grad = source[row0 + c * w.shape[0]:row0 + (c + 1) * w.shape[0], :]
                delta, m_new, v_new = _adamw(w_ref[:, cols], grad, m_ref[:, cols], v_ref[:, cols])
                for ref, val in zip(outs[4 * j:4 * j + 4], (grad, delta, m_new, v_new)):
                    ref[:, cols] = val

    full = lambda shape: pl.BlockSpec(shape, lambda i, me_ref: (0,) * len(shape))
    stack_rows = recv_small.shape[1]
    flat = [a for _, _, triple in params for a in triple]
    shapes = [w.shape for _, _, (w, _, _) in params for _ in range(4)]
    out = pl.pallas_call(
        body, name="adam_small",
        grid_spec=pltpu.PrefetchScalarGridSpec(
            num_scalar_prefetch=1, grid=(1,),
            in_specs=[full(recv_small.shape),
                      pl.BlockSpec((N_DEV, stack_rows, cw), lambda i, me_ref: (0, 0, me_ref[0])),
                      full(recv_last.shape), full(small_own.shape),
                      pl.BlockSpec((stack_rows, cw), lambda i, me_ref: (0, me_ref[0])),
                      full(last_own.shape)] + [full(a.shape) for a in flat],
            out_specs=[full((1, 1))] + [full(s) for s in shapes],
            scratch_shapes=[pltpu.VMEM((stack_rows, d_model), F32), pltpu.VMEM((stack_rows, cw), F32),
                            pltpu.VMEM(recv_last.shape[1:], F32)]),
        out_shape=[jax.ShapeDtypeStruct((1, 1), F32)] + [jax.ShapeDtypeStruct(s, F32) for s in shapes],
    )(me, *[_in_hbm(a) for a in (recv_small, recv_small, recv_last, small_own, small_own, last_own, *flat)])
    return out[0], [tuple(out[1 + 4 * j:5 + 4 * j]) for j in range(n)]


def _tile(n_tok, want):
    return min(want, n_tok)


def kernel(x, norm1_pre_g, w_in, b_in, conv_a_w, conv_a_b, w_a_out, conv_b_w, conv_b_b, ln_b_g, ln_b_b, w_b_out, w_o, norm1_post_g, norm2_pre_g, w_mlp_in, w_mlp_out, norm2_post_g, loss_target, m_norm1_pre_g, m_w_in, m_b_in, m_conv_a_w, m_conv_a_b, m_w_a_out, m_conv_b_w, m_conv_b_b, m_ln_b_g, m_ln_b_b, m_w_b_out, m_w_o, m_norm1_post_g, m_norm2_pre_g, m_w_mlp_in, m_w_mlp_out, m_norm2_post_g, v_norm1_pre_g, v_w_in, v_b_in, v_conv_a_w, v_conv_a_b, v_w_a_out, v_conv_b_w, v_conv_b_b, v_ln_b_g, v_ln_b_b, v_w_b_out, v_w_o, v_norm1_post_g, v_norm2_pre_g, v_w_mlp_in, v_w_mlp_out, v_norm2_post_g):
    n_tok, dm = x.shape[1], x.shape[2]
    rows, off, slab_rows = _layout(dm)
    cw = dm // N_DEV
    xs = x.reshape(n_tok, dm)
    tgt = loss_target.reshape(n_tok, dm)
    row = lambda vec: vec.reshape(1, -1)
    scattered = lambda group: jax.ShapeDtypeStruct((len(FLIPS), slab_rows[group], dm), BF16)
    tm, tk = min(dm, 1024), _tile(n_tok, 2048)
    me = (4 * lax.axis_index("x") + 2 * lax.axis_index("y") + lax.axis_index("c")).astype(jnp.int32).reshape(1)

    conv_own = jnp.concatenate([conv_a_w, jnp.zeros((SUBLANE - CONV_A, cw), F32), conv_b_w,
                                jnp.zeros((1, cw), F32)], axis=0)
    own_in, land_in = _place_cast([(w_in, True)], me, "place_w_in")
    slab_in, conv_all = _all_gather_two_level([own_in, conv_own], [land_in, None], "gather_w_in")
    conv_full = conv_all.transpose(1, 0, 2).reshape(conv_own.shape[0], dm)
    caw, cbw = conv_full[0:CONV_A], conv_full[SUBLANE:SUBLANE + CONV_B]
    own_abo, land_abo = _place_cast([(w_a_out, False), (w_b_out, False), (w_o, False)], me, "place_abo")
    own_mlp, land_mlp = _place_cast([(w_mlp_in, True), (w_mlp_out, False)], me, "place_mlp")
    ag_abo = _exchange_start([_Part(own_abo, False, slab_rows["abo"], 0, 0)], [land_abo],
                             "gather_abo_start", after=slab_in)
    ag_mlp = _exchange_start([_Part(own_mlp, False, slab_rows["mlp"], 0, 0)], [land_mlp],
                             "gather_mlp_start", after=ag_abo.token)

    proj, p, u, h = _fwd_in(xs, row(norm1_pre_g), row(b_in), slab_in, _tile(n_tok, 512),
                            [ag_abo.token, ag_mlp.token])
    _, (slab_abo,) = _exchange_wait(ag_abo, "gather_abo_wait", after=proj)
    va, vb, ya, yb, qa, sb, merged, mix, x1 = _fwd_mix(
        p, u, proj, xs, caw, row(conv_a_b), cbw, row(conv_b_b), row(ln_b_g), row(ln_b_b), row(norm1_post_g),
        slab_abo, _tile(n_tok, 256))
    _, (slab_mlp,) = _exchange_wait(ag_mlp, "gather_mlp_wait", after=x1)
    f, df1, h2, df2, dmix, dx1, small_mlp = _mlp_fwd_bwd(
        x1, mix, tgt, row(norm1_post_g), row(norm2_pre_g), row(norm2_post_g), slab_mlp, _tile(n_tok, 256))

    rs_mlp = _exchange_start(
        [_Part(_wgrad(df1, h2, "wgrad_mlp_in", tm, tk, BF16), True, rows["w1"], 0, off["w1"]),
         _Part(_wgrad(f, df2, "wgrad_mlp_out", tm, tk, BF16), True, rows["w2"], 0, off["w2"])],
        [scattered("mlp")], "scatter_mlp_start")
    dpa, dya, dyb, dva, dvb, small_mix = _bwd_mix(
        dmix, ya, yb, proj, va, vb, row(ln_b_g), row(ln_b_b), slab_abo, _tile(n_tok, 512), [rs_mlp.token])
    rs_abo = _exchange_start(
        [_Part(_wgrad(qa, dya, "wgrad_a_out", tm, tk, BF16), True, rows["wa"], 0, off["wa"]),
         _Part(_wgrad(sb, dyb, "wgrad_b_out", tm, tk, BF16), True, rows["wb"], 0, off["wb"]),
         _Part(_wgrad(merged, dmix, "wgrad_o", tm, tk, BF16), True, rows["wo"], 0, off["wo"])],
        [scattered("abo")], "scatter_abo_start")
    dpb, small_conv = _bwd_conv(dva, dvb, p, u, proj, caw, cbw, _tile(n_tok, 256), [rs_abo.token])

    zeros = lambda r: jnp.zeros((r, dm), F32)
    small = jnp.concatenate([
        zeros(1),
        small_mix[2:3],
        small_mix[3:4],
        small_mix[0:2],
        small_mlp[2:3],
        small_mlp[1:2],
        small_mlp[0:1],
        small_mix[4:5], small_conv[3:7], small_mix[5:7],
        small_mlp[3:4],
        small_conv[0:CONV_A], zeros(SUBLANE - CONV_A),
        small_conv[8:8 + CONV_B], zeros(1),
    ], axis=0)

    rs_in = _exchange_start(
        [_Part(_wgrad_in(dpa, dpb, h, tk), True, rows["win"], 0, off["win"]),
         _Part(small, False, small.shape[0], 1, 0)],
        [scattered("in"), jax.ShapeDtypeStruct((N_DEV,) + small.shape, F32)], "scatter_in_start")
    grad_x, small_in = _bwd_in(dpa, dpb, xs, dx1, row(norm1_pre_g), slab_in, _tile(n_tok, 512), [rs_in.token])

    tr = min(LANE, rows["wa"])
    gather_last = _exchange_start([_Part(small_in, False, SUBLANE, 0, 0)],
                                  [jax.ShapeDtypeStruct((N_DEV, SUBLANE, dm), F32)], "gather_last_start")
    (g_w1, g_w2), (recv_mlp,) = _exchange_wait(rs_mlp, "scatter_mlp_wait", after=gather_last.token)
    (g_wa, g_wb, g_wo), (recv_abo,) = _exchange_wait(rs_abo, "scatter_abo_wait", after=grad_x)
    big = {
        "w_mlp_in": _adam_big(recv_mlp, g_w1, me, off["w1"], rows["w1"], w_mlp_in, m_w_mlp_in, v_w_mlp_in, True,
                              "adam_w_mlp_in", tr),
        "w_mlp_out": _adam_big(recv_mlp, g_w2, me, off["w2"], rows["w2"], w_mlp_out, m_w_mlp_out, v_w_mlp_out,
                               False, "adam_w_mlp_out", tr),
    }
    big["w_a_out"], big["w_b_out"], big["w_o"] = _adam_rows_group(
        recv_abo, (g_wa, g_wb, g_wo), me, (off["wa"], off["wb"], off["wo"]), rows["wa"],
        ((w_a_out, m_w_a_out, v_w_a_out), (w_b_out, m_w_b_out, v_w_b_out), (w_o, m_w_o, v_w_o)), "adam_abo")
    (g_win, small), (recv_in, recv_small) = _exchange_wait(rs_in, "scatter_in_wait", after=big["w_o"][3])
    big["w_in"] = _adam_big(recv_in, g_win, me, off["win"], rows["win"], w_in, m_w_in, v_w_in, True, "adam_w_in", tr)
    (small_in,), (recv_last,) = _exchange_wait(gather_last, "gather_last_wait", after=big["w_in"][3])

    small_names = ("norm1_pre_g", "conv_a_b", "conv_b_b", "ln_b_g", "ln_b_b", "norm1_post_g", "norm2_pre_g",
                   "norm2_post_g")
    given = dict(
        norm1_pre_g=(norm1_pre_g, m_norm1_pre_g, v_norm1_pre_g), conv_a_b=(conv_a_b, m_conv_a_b, v_conv_a_b),
        conv_b_b=(conv_b_b, m_conv_b_b, v_conv_b_b), ln_b_g=(ln_b_g, m_ln_b_g, v_ln_b_g),
        ln_b_b=(ln_b_b, m_ln_b_b, v_ln_b_b), norm1_post_g=(norm1_post_g, m_norm1_post_g, v_norm1_post_g),
        norm2_pre_g=(norm2_pre_g, m_norm2_pre_g, v_norm2_pre_g),
        norm2_post_g=(norm2_post_g, m_norm2_post_g, v_norm2_post_g))
    params = [(j, False, tuple(row(a) for a in given[name])) for j, name in enumerate(small_names)]
    params.append((SUBLANE, False, tuple(row(a) for a in (b_in, m_b_in, v_b_in))))
    params.append((CONV_A_ROW, True, (conv_a_w, m_conv_a_w, v_conv_a_w)))
    params.append((CONV_B_ROW, True, (conv_b_w, m_conv_b_w, v_conv_b_w)))
    loss, small_out = _adam_small(recv_small, recv_last, small, small_in, me, params, dm)
    small_leaves = {name: tuple(a.reshape(dm) for a in small_out[j]) for j, name in enumerate(small_names)}
    small_leaves["b_in"] = tuple(a.reshape(7 * dm) for a in small_out[len(small_names)])
    small_leaves["conv_a_w"] = small_out[len(small_names) + 1]
    small_leaves["conv_b_w"] = small_out[len(small_names) + 2]

    order = ("norm1_pre_g", "w_in", "b_in", "conv_a_w", "conv_a_b", "w_a_out", "conv_b_w", "conv_b_b", "ln_b_g",
             "ln_b_b", "w_b_out", "w_o", "norm1_post_g", "norm2_pre_g", "w_mlp_in", "w_mlp_out", "norm2_post_g")
    leaves = [big[name] if name in big else small_leaves[name] for name in order]
    grads, deltas, new_m, new_v = zip(*leaves)
    return (loss.reshape(()), grad_x.reshape(x.shape), *grads, *deltas, *new_m, *new_v)
```

```python
from typing import NamedTuple

import jax
import jax.numpy as jnp
from jax import lax
from jax.experimental import pallas as pl
from jax.experimental.pallas import tpu as pltpu

F32 = jnp.float32
BF16 = jnp.bfloat16

RMS_EPS = 1e-6
LN_EPS = 1e-5
ADAM_LR = 0.001
ADAM_B1 = 0.9
ADAM_B2 = 0.999
ADAM_EPS = 1e-08
ADAM_WD = 0.01
ADAM_STEP = 10

N_DEV = 8
CONV_A = 3
CONV_B = 31
LANE = 128
SUBLANE = 8
HALO = 16
FWD_CONV_ROWS = 32
BWD_CONV_ROWS = 64
MIB = 1 << 20
FLIPS = ((0, 0, 1), (0, 1, 0), (1, 0, 0), (0, 1, 1), (1, 0, 1), (1, 1, 0), (1, 1, 1))
MESH = pl.DeviceIdType.MESH


def _layout(d_model):
    e = d_model // N_DEV
    rows = {"win": 7 * e, "w1": 4 * e, "w2": 4 * e, "wa": e, "wb": e, "wo": e}
    off = {"win": 0, "w1": 0, "w2": 4 * e, "wa": 0, "wb": e, "wo": 2 * e}
    return rows, off, {"in": 7 * e, "mlp": 8 * e, "abo": 3 * e}


def _after(body, deps):
    def ordered(*refs):
        return body(*refs[len(deps):])
    return ordered


def _params(vmem_mib):
    return pltpu.CompilerParams(dimension_semantics=("arbitrary",), vmem_limit_bytes=vmem_mib * MIB)


def _whole():
    return pl.BlockSpec(memory_space=pltpu.VMEM)


def _hbm():
    return pl.BlockSpec(memory_space=pl.ANY)


def _in_hbm(a):
    return pltpu.with_memory_space_constraint(a, pltpu.HBM)


def _rows(ts, width, col=0):
    return pl.BlockSpec((ts, width), lambda i: (i, col))


def _halo_specs(ts, width, n_rows):
    per = ts // HALO
    last = n_rows // HALO - 1
    return [
        pl.BlockSpec((ts, width), lambda i: (i, 0)),
        pl.BlockSpec((HALO, width), lambda i: (jnp.maximum(i * per - 1, 0), 0)),
        pl.BlockSpec((HALO, width), lambda i: (jnp.minimum((i + 1) * per, last), 0)),
    ]


def _dot(a, b):
    return jnp.dot(a, b, preferred_element_type=F32)


def _dot_nt(a, b):
    return lax.dot_general(a, b, (((1,), (1,)), ((), ())), preferred_element_type=F32)


def _dot_tn(a, b):
    return lax.dot_general(a, b, (((0,), (0,)), ((), ())), preferred_element_type=F32)


def _rms(u):
    return lax.rsqrt(jnp.mean(u * u, axis=-1, keepdims=True) + RMS_EPS)


def _rms_bwd(dz, u, r, g):
    dzg = dz * g
    return r * dzg - u * (r * r * r) * jnp.mean(dzg * u, axis=-1, keepdims=True)


def _colsum(v):
    return jnp.sum(v, axis=0, keepdims=True)


def _sigmoid(v):
    return jax.nn.sigmoid(v)


def _weight_copies(slab_ref, off, rows, dst_ref, sems, first_sem):
    return [pltpu.make_async_copy(slab_ref.at[d, pl.ds(off, rows), :], dst_ref.at[pl.ds(d * rows, rows), :],
                                  sems.at[first_sem + d]) for d in range(N_DEV)]


def _on_first_step(copies, method):
    @pl.when(pl.program_id(0) == 0)
    def _():
        for cp in copies:
            getattr(cp, method)()


def _with_halos(main_ref, prev_ref, next_ref, i, n_steps):
    return (main_ref, jnp.where(i > 0, prev_ref[...], 0.0), jnp.where(i < n_steps - 1, next_ref[...], 0.0))


def _broadcast_taps(w_ref, wb_ref, n_taps):
    for k in range(n_taps):
        wb_ref[k] = jnp.broadcast_to(w_ref[k:k + 1, :], wb_ref.shape[1:])


def _conv_tile(tile, wb_ref, starts, ts, width, emit, rolled_ref, rows=FWD_CONV_ROWS):
    main_ref, prev, nxt = tile
    span = ts + 2 * HALO
    nv = rows // SUBLANE
    for cb in range(width // LANE):
        lanes = slice(cb * LANE, (cb + 1) * LANE)
        slot = cb % 2
        window = jnp.concatenate([prev[:, lanes], main_ref[:, lanes], nxt[:, lanes]], axis=0)
        for b in sorted({st % SUBLANE for st in starts}):
            rolled_ref[slot, b] = window if b == 0 else pltpu.roll(window, span - b, axis=0)
        for r0 in range(0, ts, rows):
            acc = jnp.zeros((nv, SUBLANE, LANE), F32)
            for k, st in enumerate(starts):
                shifted = rolled_ref[slot, st % SUBLANE, pl.ds(r0 + st - st % SUBLANE, rows), :]
                acc = acc + shifted.reshape(nv, SUBLANE, LANE) * wb_ref[k, :, lanes][None]
            emit(r0, pl.ds(cb * LANE, LANE), acc.reshape(rows, LANE))


def _window(tile, r0, cb, ts, rows):
    main_ref, prev, nxt = tile
    lanes = slice(cb * LANE, (cb + 1) * LANE)
    lo, hi = max(r0 - HALO, 0), min(r0 + rows + HALO, ts)
    pieces = [prev[:, lanes]] if r0 - HALO < 0 else []
    pieces.append(main_ref[lo:hi, lanes])
    if r0 + rows + HALO > ts:
        pieces.append(nxt[:, lanes])
    return pieces[0] if len(pieces) == 1 else jnp.concatenate(pieces, axis=0)


def _phases(starts):
    groups = {}
    for k, st in enumerate(starts):
        groups.setdefault(st % SUBLANE, []).append((k, st // SUBLANE))
    return sorted(groups.items())


def _shifted(blk, b):
    n = blk.shape[0]
    rolled = blk if b == 0 else pltpu.roll(blk, n - b, axis=0)
    return rolled.reshape(n // SUBLANE, SUBLANE, blk.shape[1])


def _conv_bwd_tile(dv_tile, u_ref, wb_ref, acc_ref, n_taps, ts, width, emit, rows=BWD_CONV_ROWS):
    groups = _phases(_bwd_starts(n_taps))
    nv = rows // SUBLANE
    for r0 in range(0, ts, rows):
        for cb in range(width // LANE):
            lanes = pl.ds(cb * LANE, LANE)
            blk = _window(dv_tile, r0, cb, ts, rows)
            u = u_ref[pl.ds(r0, rows), lanes].reshape(nv, SUBLANE, LANE)
            du = jnp.zeros((nv, SUBLANE, LANE), F32)
            for b, taps in groups:
                sh = _shifted(blk, b)
                for k, m in taps:
                    du = du + sh[m:m + nv] * wb_ref[k, :, lanes][None]
                    acc_ref[k, :, lanes] += jnp.sum(sh[m:m + nv] * u, axis=0)
            emit(r0, lanes, du.reshape(rows, LANE))


def _fwd_starts(n_taps):
    pad = (n_taps - 1) // 2
    return [HALO - pad + k for k in range(n_taps)]


def _bwd_starts(n_taps):
    pad = (n_taps - 1) // 2
    return [HALO + pad - k for k in range(n_taps)]


def _peer(x, y, c, flip):
    fx, fy, fc = flip
    return (1 - x if fx else x, 1 - y if fy else y, 1 - c if fc else c)


def _place_cast(pieces, me, name):
    n = len(pieces)
    counts = [a.shape[1] if t else a.shape[0] for a, t in pieces]
    width = pieces[0][0].shape[0] if pieces[0][1] else pieces[0][0].shape[1]
    total = sum(counts)

    def body(me_ref, *refs):
        ins, own_ref, land_ref = refs[:n], refs[n], refs[n + 1]
        first = 0
        for (a, transpose), in_ref, count in zip(pieces, ins, counts):
            block = (in_ref[...].T if transpose else in_ref[...]).astype(BF16)
            own_ref[first:first + count, :] = block
            land_ref[first:first + count, :] = block
            first += count

    return pl.pallas_call(
        body, name=name,
        grid_spec=pltpu.PrefetchScalarGridSpec(
            num_scalar_prefetch=1, grid=(1,),
            in_specs=[pl.BlockSpec(a.shape, lambda i, me_ref: (0, 0)) for a, _ in pieces],
            out_specs=[pl.BlockSpec((total, width), lambda i, me_ref: (0, 0)),
                       pl.BlockSpec((None, total, width), lambda i, me_ref: (me_ref[0], 0, 0))]),
        out_shape=[pltpu.HBM((total, width), BF16), pltpu.HBM((N_DEV, total, width), BF16)],
        compiler_params=_params(32),
    )(me, *[_in_hbm(a) for a, _ in pieces])


def _all_gather_two_level(shards, placed, name):
    n = len(shards)
    given = [j for j in range(n) if placed[j] is not None]

    def body(*refs):
        ins, outs = refs[:n], refs[n + len(given):2 * n + len(given)]
        send_sems, recv_sems, local_sems = refs[2 * n + len(given):]
        x, y, c = lax.axis_index("x"), lax.axis_index("y"), lax.axis_index("c")
        me, sibling = (x, y, c), (x, y, 1 - c)
        relayed, direct, diagonal = (x ^ (1 - c), y ^ c), (x ^ c, y ^ (1 - c)), (1 - x, 1 - y)

        def slot(j, dev):
            return outs[j].at[4 * dev[0] + 2 * dev[1] + dev[2]]

        def copy(k, j, block, to, src=None):
            return pltpu.make_async_remote_copy(
                src_ref=slot(j, block) if src is None else src, dst_ref=slot(j, block),
                send_sem=send_sems.at[k * n + j], recv_sem=recv_sems.at[k * n + j], device_id=to, device_id_type=MESH)

        local = [pltpu.make_async_copy(ins[j], slot(j, me), local_sems.at[j]) for j in range(n) if j not in given]
        for cp in local:
            cp.start()
        sent = [copy(1, j, me, (*relayed, c), src=ins[j]) for j in range(n)]
        sent += [copy(2, j, me, (*direct, c), src=ins[j]) for j in range(n)]
        sent += [copy(0, j, me, sibling, src=ins[j]) for j in range(n)]
        for cp in sent:
            cp.start()
        for k, chip in ((1, relayed), (2, direct), (3, diagonal)):
            for j in range(n):
                copy(k, j, (*chip, c), me).wait_recv()
                if k == 1:
                    sent.append(copy(3, j, (*chip, c), (*direct, c)))
                    sent[-1].start()
                sent.append(copy(3 + k, j, (*chip, c), sibling))
                sent[-1].start()
        for j in range(n):
            copy(0, j, sibling, me).wait_recv()
        for k, chip in ((4, direct), (5, relayed), (6, diagonal)):
            for j in range(n):
                copy(k, j, (*chip, 1 - c), me).wait_recv()
        for cp in sent:
            cp.wait_send()
        for cp in local:
            cp.wait()

    return pl.pallas_call(
        body, name=name,
        out_shape=[jax.ShapeDtypeStruct((N_DEV,) + s.shape, s.dtype) for s in shards],
        in_specs=[_hbm()] * (n + len(given)), out_specs=[_hbm()] * n,
        input_output_aliases={n + i: j for i, j in enumerate(given)},
        scratch_shapes=[pltpu.SemaphoreType.DMA((7 * n,)), pltpu.SemaphoreType.DMA((7 * n,)),
                        pltpu.SemaphoreType.DMA((n,))],
    )(*shards, *[placed[j] for j in given])


class _Part(NamedTuple):
    src: jax.Array
    scatter: bool
    rows: int
    land: int
    off: int


class _Started(NamedTuple):
    send_sems: jax.Array
    recv_sems: jax.Array
    thru: tuple
    token: jax.Array
    parts: tuple


def _exchange_copies(srcs, lands, send_sems, recv_sems, parts):
    n = len(parts)
    x, y, c = lax.axis_index("x"), lax.axis_index("y"), lax.axis_index("c")
    me = 4 * x + 2 * y + c

    def block(j, dev):
        p = parts[j]
        return srcs[j].at[pl.ds(pl.multiple_of(dev * p.rows, SUBLANE), p.rows), :] if p.scatter else srcs[j]

    def slot(j, index):
        p = parts[j]
        return lands[p.land].at[index, pl.ds(p.off, p.rows), :]

    sends, recvs = [], []
    for k, flip in enumerate(FLIPS):
        px, py, pc = _peer(x, y, c, flip)
        peer = 4 * px + 2 * py + pc
        for j in range(n):
            sems = dict(send_sem=send_sems.at[k * n + j], recv_sem=recv_sems.at[k * n + j],
                        device_id=(px, py, pc), device_id_type=MESH)
            to, got = (k, k) if parts[j].scatter else (me, peer)
            sends.append(pltpu.make_async_remote_copy(src_ref=block(j, peer), dst_ref=slot(j, to), **sems))
            recvs.append(pltpu.make_async_remote_copy(src_ref=block(j, peer), dst_ref=slot(j, got), **sems))
    return sends, recvs


def _exchange_start(parts, lands, name, after=None):
    n, nl = len(parts), len(lands)
    n_in = n + nl + (after is not None)

    def body(*refs):
        srcs, land_refs = refs[:n], refs[n:n + nl]
        send_sems, recv_sems = refs[n_in], refs[n_in + 1]
        token = refs[n_in + 2 + n + nl]
        sends, _ = _exchange_copies(srcs, land_refs, send_sems, recv_sems, parts)
        for cp in sends:
            cp.start()
        token[...] = jnp.zeros_like(token)

    hbm = pl.BlockSpec(memory_space=pltpu.HBM)
    sem = pl.BlockSpec(memory_space=pltpu.SEMAPHORE)
    fresh = lambda s: lax.empty(s.shape, s.dtype) if isinstance(s, jax.ShapeDtypeStruct) else s
    args = [pltpu.with_memory_space_constraint(p.src, pltpu.HBM) for p in parts]
    args += [pltpu.with_memory_space_constraint(fresh(s), pltpu.HBM) for s in lands]
    args += [] if after is None else [after]
    out = pl.pallas_call(
        body, name=name,
        out_shape=(pltpu.SemaphoreType.DMA((7 * n,)), pltpu.SemaphoreType.DMA((7 * n,)),
                   *[pltpu.HBM(a.shape, a.dtype) for a in args[:n + nl]], jax.ShapeDtypeStruct((SUBLANE, LANE), F32)),
        in_specs=[hbm] * (n + nl) + [_hbm()] * (after is not None),
        out_specs=(sem, sem, *[hbm] * (n + nl), _whole()),
        input_output_aliases={j: 2 + j for j in range(n + nl)},
        compiler_params=pltpu.CompilerParams(has_side_effects=pltpu.SideEffectType.DATAFLOW_SIDE_EFFECTING),
    )(*args)
    return _Started(out[0], out[1], tuple(out[2:2 + n + nl]), out[2 + n + nl], tuple(parts))


def _exchange_wait(started, name, after):
    parts = started.parts
    n, nl = len(parts), len(started.thru) - len(parts)

    def body(*refs):
        srcs, land_refs = refs[:n], refs[n:n + nl]
        send_sems, recv_sems = refs[n + nl], refs[n + nl + 1]
        sends, recvs = _exchange_copies(srcs, land_refs, send_sems, recv_sems, parts)
        for cp in sends:
            cp.wait_send()
        for cp in recvs:
            cp.wait_recv()

    hbm = pl.BlockSpec(memory_space=pltpu.HBM)
    sem = pl.BlockSpec(memory_space=pltpu.SEMAPHORE)
    out = pl.pallas_call(
        body, name=name,
        out_shape=tuple(pltpu.HBM(a.shape, a.dtype) for a in started.thru),
        in_specs=[hbm] * (n + nl) + [sem, sem, _hbm()], out_specs=tuple([hbm] * (n + nl)),
        input_output_aliases={j: j for j in range(n + nl)},
        compiler_params=pltpu.CompilerParams(has_side_effects=pltpu.SideEffectType.DATAFLOW_SIDE_EFFECTING),
    )(*started.thru, started.send_sems, started.recv_sems, after)
    return list(out[:n]), list(out[n:])


def _fwd_in(x, g1, b_in, slab, ts, deps):
    n_tok, dm = x.shape
    rows, off, _ = _layout(dm)
    width = 7 * dm

    def body(x_ref, g1_ref, b_ref, slab_ref, proj_ref, p_ref, u_ref, h_ref, w_v, sems):
        copies = _weight_copies(slab_ref, off["win"], rows["win"], w_v, sems, 0)
        _on_first_step(copies, "start")
        _on_first_step(copies, "wait")
        xv = x_ref[...]
        h = (xv * _rms(xv) * g1_ref[...]).astype(BF16)
        h_ref[...] = h
        cols = []
        for j in range(7):
            pj = _dot_nt(h, w_v[pl.ds(j * dm, dm), :]) + b_ref[:, j * dm:(j + 1) * dm]
            proj_ref[:, j * dm:(j + 1) * dm] = pj.astype(proj_ref.dtype)
            if 1 <= j <= 4:
                cols.append(pj)
            if j == 2:
                p_ref[...] = cols[0] * cols[1]
            if j == 4:
                u_ref[...] = cols[2] * _sigmoid(cols[3])

    return pl.pallas_call(
        _after(body, deps), name="fwd_in", grid=(n_tok // ts,),
        in_specs=[_whole()] * len(deps) + [_rows(ts, dm), _whole(), _whole(), _hbm()],
        out_specs=[_rows(ts, width), _rows(ts, dm), _rows(ts, dm), _rows(ts, dm)],
        out_shape=[jax.ShapeDtypeStruct((n_tok, width), BF16), jax.ShapeDtypeStruct((n_tok, dm), F32),
                   jax.ShapeDtypeStruct((n_tok, dm), F32), jax.ShapeDtypeStruct((n_tok, dm), BF16)],
        scratch_shapes=[pltpu.VMEM((width, dm), BF16), pltpu.SemaphoreType.DMA((N_DEV,))],
        compiler_params=_params(56),
    )(*deps, x, g1, b_in, slab)


def _fwd_mix(p, u, proj, x, caw, cab, cbw, cbb, lng, lnb, g1post, slab, ts):
    n_tok, dm = x.shape
    rows, off, _ = _layout(dm)
    n_steps = n_tok // ts

    def body(p_ref, p_prev, p_next, u_ref, u_prev, u_next, bg_ref, za_ref, zb_ref, x_ref,
             caw_ref, cab_ref, cbw_ref, cbb_ref, lng_ref, lnb_ref, g1p_ref, slab_ref,
             va_ref, vb_ref, ya_ref, yb_ref, qa_ref, sb_ref, mg_ref, mix_ref, x1_ref,
             wa_v, wb_v, wo_v, tap_a, tap_b, sems, rolled_a, rolled_b):
        i = pl.program_id(0)
        copies = (_weight_copies(slab_ref, off["wa"], rows["wa"], wa_v, sems, 0)
                  + _weight_copies(slab_ref, off["wb"], rows["wb"], wb_v, sems, N_DEV)
                  + _weight_copies(slab_ref, off["wo"], rows["wo"], wo_v, sems, 2 * N_DEV))
        _on_first_step(copies, "start")

        @pl.when(i == 0)
        def _():
            _broadcast_taps(caw_ref, tap_a, CONV_A)
            _broadcast_taps(cbw_ref, tap_b, CONV_B)

        def emit_a(r0, lanes, acc):
            va_ref[pl.ds(r0, acc.shape[0]), lanes] = acc + cab_ref[:, lanes]

        def emit_b(r0, lanes, acc):
            vb_ref[pl.ds(r0, acc.shape[0]), lanes] = acc + cbb_ref[:, lanes]

        _conv_tile(_with_halos(p_ref, p_prev, p_next, i, n_steps), tap_a, _fwd_starts(CONV_A), ts, dm, emit_a,
                   rolled_a)
        _conv_tile(_with_halos(u_ref, u_prev, u_next, i, n_steps), tap_b, _fwd_starts(CONV_B), ts, dm, emit_b,
                   rolled_b)
        _on_first_step(copies, "wait")

        qa = (bg_ref[...].astype(F32) * va_ref[...]).astype(BF16)
        qa_ref[...] = qa
        ya = _dot(qa, wa_v[...])
        vb = vb_ref[...]
        xc = vb - jnp.mean(vb, axis=-1, keepdims=True)
        rstd = lax.rsqrt(jnp.mean(xc * xc, axis=-1, keepdims=True) + LN_EPS)
        ln = xc * rstd * lng_ref[...] + lnb_ref[...]
        sb = (ln * _sigmoid(ln)).astype(BF16)
        sb_ref[...] = sb
        yb = _dot(sb, wb_v[...])
        ya_ref[...] = ya.astype(BF16)
        yb_ref[...] = yb.astype(BF16)
        merged = (_sigmoid(za_ref[...].astype(F32)) * ya + _sigmoid(zb_ref[...].astype(F32)) * yb).astype(BF16)
        mg_ref[...] = merged
        mix = _dot(merged, wo_v[...])
        mix_ref[...] = mix
        x1_ref[...] = x_ref[...] + mix * _rms(mix) * g1p_ref[...]

    tok = lambda dt: jax.ShapeDtypeStruct((n_tok, dm), dt)
    return pl.pallas_call(
        body, name="fwd_mix", grid=(n_steps,),
        in_specs=(_halo_specs(ts, dm, n_tok) + _halo_specs(ts, dm, n_tok)
                  + [_rows(ts, dm, 0), _rows(ts, dm, 5), _rows(ts, dm, 6), _rows(ts, dm)]
                  + [_whole()] * 7 + [_hbm()]),
        out_specs=[_rows(ts, dm)] * 9,
        out_shape=[tok(F32), tok(F32), tok(BF16), tok(BF16), tok(BF16), tok(BF16), tok(BF16), tok(F32), tok(F32)],
        scratch_shapes=[pltpu.VMEM((dm, dm), BF16), pltpu.VMEM((dm, dm), BF16), pltpu.VMEM((dm, dm), BF16),
                        pltpu.VMEM((CONV_A, SUBLANE, dm), F32), pltpu.VMEM((CONV_B, SUBLANE, dm), F32),
                        pltpu.SemaphoreType.DMA((3 * N_DEV,)),
                        pltpu.VMEM((2, SUBLANE, ts + 2 * HALO, LANE), F32),
                        pltpu.VMEM((2, SUBLANE, ts + 2 * HALO, LANE), F32)],
        compiler_params=_params(48),
    )(p, p, p, u, u, u, proj, proj, proj, x, caw, cab, cbw, cbb, lng, lnb, g1post, slab)


def _mlp_fwd_bwd(x1, mix, tgt, g1post, g2pre, g2post, slab, ts):
    n_tok, dm = x1.shape
    rows, off, _ = _layout(dm)
    ff = 4 * dm

    def body(x1_ref, mix_ref, t_ref, g1p_ref, g2pre_ref, g2post_ref, slab_ref,
             f_ref, df1_ref, h2_ref, df2_ref, dmix_ref, dx1_ref, small_ref, w1_v, w2_v, relu_v, sems):
        w1_copies = _weight_copies(slab_ref, off["w1"], rows["w1"], w1_v, sems, 0)
        w2_copies = _weight_copies(slab_ref, off["w2"], rows["w2"], w2_v, sems, N_DEV)
        _on_first_step(w1_copies + w2_copies, "start")

        @pl.when(pl.program_id(0) == 0)
        def _():
            small_ref[...] = jnp.zeros_like(small_ref)

        _on_first_step(w1_copies + w2_copies, "wait")
        x1v = x1_ref[...]
        r3 = _rms(x1v)
        g2pre = g2pre_ref[...]
        h2 = (x1v * r3 * g2pre).astype(BF16)
        h2_ref[...] = h2
        for c in range(4):
            blk = pl.ds(c * dm, dm)
            relu = jnp.maximum(_dot_nt(h2, w1_v[blk, :]), 0.0)
            relu_v[:, c * dm:(c + 1) * dm] = relu
            f_ref[:, c * dm:(c + 1) * dm] = (relu * relu).astype(BF16)
        f2 = _dot(f_ref[...], w2_v[...])
        r4 = _rms(f2)
        g2post = g2post_ref[...]
        err = x1v + f2 * r4 * g2post - t_ref[...]
        dy = err * (1.0 / dm)
        small_ref[3:4, :] += _colsum(err * err)
        small_ref[0:1, :] += _colsum(dy * f2 * r4)
        df2 = _rms_bwd(dy, f2, r4, g2post).astype(BF16)
        df2_ref[...] = df2
        for c in range(4):
            blk = pl.ds(c * dm, dm)
            df1 = (_dot_nt(df2, w2_v[blk, :]) * (2.0 * relu_v[:, c * dm:(c + 1) * dm])).astype(BF16)
            df1_ref[:, c * dm:(c + 1) * dm] = df1
        dh2 = _dot(df1_ref[...], w1_v[...])
        small_ref[1:2, :] += _colsum(dh2 * x1v * r3)
        dx1 = dy + _rms_bwd(dh2, x1v, r3, g2pre)
        dx1_ref[...] = dx1
        mixv = mix_ref[...]
        r2 = _rms(mixv)
        small_ref[2:3, :] += _colsum(dx1 * mixv * r2)
        dmix_ref[...] = _rms_bwd(dx1, mixv, r2, g1p_ref[...]).astype(BF16)

    tok = lambda w, dt: jax.ShapeDtypeStruct((n_tok, w), dt)
    return pl.pallas_call(
        body, name="mlp_fwd_bwd", grid=(n_tok // ts,),
        in_specs=[_rows(ts, dm)] * 3 + [_whole()] * 3 + [_hbm()],
        out_specs=[_rows(ts, ff), _rows(ts, ff), _rows(ts, dm), _rows(ts, dm), _rows(ts, dm), _rows(ts, dm),
                   pl.BlockSpec((SUBLANE, dm), lambda i: (0, 0))],
        out_shape=[tok(ff, BF16), tok(ff, BF16), tok(dm, BF16), tok(dm, BF16), tok(dm, BF16), tok(dm, F32),
                   jax.ShapeDtypeStruct((SUBLANE, dm), F32)],
        scratch_shapes=[pltpu.VMEM((ff, dm), BF16), pltpu.VMEM((ff, dm), BF16), pltpu.VMEM((ts, ff), F32),
                        pltpu.SemaphoreType.DMA((2 * N_DEV,))],
        compiler_params=_params(56),
    )(x1, mix, tgt, g1post, g2pre, g2post, slab)


def _bwd_mix(dmix, ya, yb, proj, va, vb, lng, lnb, slab, ts, deps):
    n_tok, dm = dmix.shape
    rows, off, _ = _layout(dm)
    n_steps = n_tok // ts

    def body(dmix_ref, ya_ref, yb_ref, bg_ref, za_ref, zb_ref, va_ref, vb_ref, lng_ref, lnb_ref, slab_ref,
             dpa_ref, dya_ref, dyb_ref, dva_ref, dvb_ref, small_ref, wa_v, wb_v, wo_v, sems):
        wo_copies = _weight_copies(slab_ref, off["wo"], rows["wo"], wo_v, sems, 2 * N_DEV)
        ab_copies = (_weight_copies(slab_ref, off["wa"], rows["wa"], wa_v, sems, 0)
                     + _weight_copies(slab_ref, off["wb"], rows["wb"], wb_v, sems, N_DEV))
        _on_first_step(wo_copies + ab_copies, "start")

        @pl.when(pl.program_id(0) == 0)
        def _():
            small_ref[...] = jnp.zeros_like(small_ref)

        _on_first_step(wo_copies, "wait")
        dmerged = _dot_nt(dmix_ref[...], wo_v[...])
        _on_first_step(ab_copies, "wait")
        sa = _sigmoid(za_ref[...].astype(F32))
        sg = _sigmoid(zb_ref[...].astype(F32))
        dza = dmerged * ya_ref[...].astype(F32) * sa * (1.0 - sa)
        dzb = dmerged * yb_ref[...].astype(F32) * sg * (1.0 - sg)
        dpa_ref[:, dm:2 * dm] = dza.astype(BF16)
        dpa_ref[:, 2 * dm:3 * dm] = dzb.astype(BF16)
        small_ref[5:6, :] += _colsum(dza)
        small_ref[6:7, :] += _colsum(dzb)

        dya = (dmerged * sa).astype(BF16)
        dya_ref[...] = dya
        dqa = _dot_nt(dya, wa_v[...])
        dbg = dqa * va_ref[...]
        dpa_ref[:, 0:dm] = dbg.astype(BF16)
        small_ref[4:5, :] += _colsum(dbg)
        dva = dqa * bg_ref[...].astype(F32)
        dva_ref[...] = dva
        small_ref[2:3, :] += _colsum(dva)

        dyb = (dmerged * sg).astype(BF16)
        dyb_ref[...] = dyb
        dsb = _dot_nt(dyb, wb_v[...])
        vb = vb_ref[...]
        xc = vb - jnp.mean(vb, axis=-1, keepdims=True)
        rstd = lax.rsqrt(jnp.mean(xc * xc, axis=-1, keepdims=True) + LN_EPS)
        nrm = xc * rstd
        lng_v = lng_ref[...]
        ln = nrm * lng_v + lnb_ref[...]
        sl = _sigmoid(ln)
        dln = dsb * (sl * (1.0 + ln * (1.0 - sl)))
        small_ref[0:1, :] += _colsum(dln * nrm)
        small_ref[1:2, :] += _colsum(dln)
        dn = dln * lng_v
        dvb = rstd * (dn - jnp.mean(dn, axis=-1, keepdims=True)
                      - nrm * jnp.mean(dn * nrm, axis=-1, keepdims=True))
        dvb_ref[...] = dvb
        small_ref[3:4, :] += _colsum(dvb)

    tok = lambda w, dt: jax.ShapeDtypeStruct((n_tok, w), dt)
    return pl.pallas_call(
        _after(body, deps), name="bwd_mix", grid=(n_steps,),
        in_specs=([_whole()] * len(deps) + [_rows(ts, dm)] * 3
                  + [_rows(ts, dm, 0), _rows(ts, dm, 5), _rows(ts, dm, 6)]
                  + [_rows(ts, dm)] * 2 + [_whole()] * 2 + [_hbm()]),
        out_specs=[_rows(ts, 3 * dm), _rows(ts, dm), _rows(ts, dm), _rows(ts, dm), _rows(ts, dm),
                   pl.BlockSpec((SUBLANE, dm), lambda i: (0, 0))],
        out_shape=[tok(3 * dm, BF16), tok(dm, BF16), tok(dm, BF16), tok(dm, F32), tok(dm, F32),
                   jax.ShapeDtypeStruct((SUBLANE, dm), F32)],
        scratch_shapes=[pltpu.VMEM((dm, dm), BF16), pltpu.VMEM((dm, dm), BF16), pltpu.VMEM((dm, dm), BF16),
                        pltpu.SemaphoreType.DMA((3 * N_DEV,))],
        compiler_params=_params(56),
    )(*deps, dmix, ya, yb, proj, proj, proj, va, vb, lng, lnb, slab)


def _bwd_conv(dva, dvb, p, u, proj, caw, cbw, ts, deps):
    n_tok, dm = dva.shape
    n_steps = n_tok // ts
    small_rows = 40

    def body(dva_ref, dva_prev, dva_next, dvb_ref, dvb_prev, dvb_next, p_ref, u_ref,
             cg_ref, ha_ref, a_ref, g_ref, caw_ref, cbw_ref,
             dproj_ref, small_ref,
             dp_v, du_v, tap_a, tap_b, gwa_v, gwb_v):
        i = pl.program_id(0)

        @pl.when(i == 0)
        def _():
            _broadcast_taps(caw_ref, tap_a, CONV_A)
            _broadcast_taps(cbw_ref, tap_b, CONV_B)
            small_ref[...] = jnp.zeros_like(small_ref)
            gwa_v[...] = jnp.zeros_like(gwa_v)
            gwb_v[...] = jnp.zeros_like(gwb_v)

        def emit_dp(r0, lanes, acc):
            dp_v[pl.ds(r0, acc.shape[0]), lanes] = acc

        def emit_du(r0, lanes, acc):
            du_v[pl.ds(r0, acc.shape[0]), lanes] = acc

        _conv_bwd_tile(_with_halos(dva_ref, dva_prev, dva_next, i, n_steps), p_ref, tap_a, gwa_v, CONV_A, ts, dm,
                       emit_dp)
        _conv_bwd_tile(_with_halos(dvb_ref, dvb_prev, dvb_next, i, n_steps), u_ref, tap_b, gwb_v, CONV_B, ts, dm,
                       emit_du)

        dp = dp_v[...]
        dcg = dp * ha_ref[...].astype(F32)
        dha = dp * cg_ref[...].astype(F32)
        du = du_v[...]
        sg = _sigmoid(g_ref[...].astype(F32))
        da = du * sg
        dg = du * a_ref[...].astype(F32) * sg * (1.0 - sg)
        dproj_ref[:, 0:dm] = dcg.astype(BF16)
        dproj_ref[:, dm:2 * dm] = dha.astype(BF16)
        dproj_ref[:, 2 * dm:3 * dm] = da.astype(BF16)
        dproj_ref[:, 3 * dm:4 * dm] = dg.astype(BF16)
        small_ref[3:4, :] += _colsum(dcg)
        small_ref[4:5, :] += _colsum(dha)
        small_ref[5:6, :] += _colsum(da)
        small_ref[6:7, :] += _colsum(dg)

        @pl.when(i == n_steps - 1)
        def _():
            for k in range(CONV_A):
                small_ref[k:k + 1, :] = _colsum(gwa_v[k])
            for k in range(CONV_B):
                small_ref[SUBLANE + k:SUBLANE + k + 1, :] = _colsum(gwb_v[k])

    return pl.pallas_call(
        _after(body, deps), name="bwd_conv", grid=(n_steps,),
        in_specs=([_whole()] * len(deps) + _halo_specs(ts, dm, n_tok) * 2 + [_rows(ts, dm)] * 2
                  + [_rows(ts, dm, 1), _rows(ts, dm, 2), _rows(ts, dm, 3), _rows(ts, dm, 4)]
                  + [_whole()] * 2),
        out_specs=[_rows(ts, 4 * dm), pl.BlockSpec((small_rows, dm), lambda i: (0, 0))],
        out_shape=[jax.ShapeDtypeStruct((n_tok, 4 * dm), BF16), jax.ShapeDtypeStruct((small_rows, dm), F32)],
        scratch_shapes=[pltpu.VMEM((ts, dm), F32), pltpu.VMEM((ts, dm), F32),
                        pltpu.VMEM((CONV_A, SUBLANE, dm), F32), pltpu.VMEM((CONV_B, SUBLANE, dm), F32),
                        pltpu.VMEM((CONV_A, SUBLANE, dm), F32), pltpu.VMEM((CONV_B, SUBLANE, dm), F32)],
        compiler_params=_params(48),
    )(*deps, dva, dva, dva, dvb, dvb, dvb, p, u, proj, proj, proj, proj, caw, cbw)


def _bwd_in(dpa, dpb, x, dx1, g1, slab, ts, deps):
    n_tok, dm = x.shape
    rows, off, _ = _layout(dm)
    width = 7 * dm

    def body(dpa_ref, dpb_ref, x_ref, dx1_ref, g1_ref, slab_ref, gx_ref, small_ref, w_v, sems):
        copies = _weight_copies(slab_ref, off["win"], rows["win"], w_v, sems, 0)
        _on_first_step(copies, "start")

        @pl.when(pl.program_id(0) == 0)
        def _():
            small_ref[...] = jnp.zeros_like(small_ref)

        _on_first_step(copies, "wait")
        dh = (_dot(dpa_ref[:, 0:dm], w_v[0:dm, :]) + _dot(dpb_ref[...], w_v[dm:5 * dm, :])
              + _dot(dpa_ref[:, dm:3 * dm], w_v[5 * dm:7 * dm, :]))
        xv = x_ref[...]
        r1 = _rms(xv)
        small_ref[0:1, :] += _colsum(dh * xv * r1)
        gx_ref[...] = dx1_ref[...] + _rms_bwd(dh, xv, r1, g1_ref[...])

    return pl.pallas_call(
        _after(body, deps), name="bwd_in", grid=(n_tok // ts,),
        in_specs=[_whole()] * len(deps) + [_rows(ts, 3 * dm), _rows(ts, 4 * dm), _rows(ts, dm), _rows(ts, dm),
                                           _whole(), _hbm()],
        out_specs=[_rows(ts, dm), pl.BlockSpec((SUBLANE, dm), lambda i: (0, 0))],
        out_shape=[jax.ShapeDtypeStruct((n_tok, dm), F32), jax.ShapeDtypeStruct((SUBLANE, dm), F32)],
        scratch_shapes=[pltpu.VMEM((width, dm), BF16), pltpu.SemaphoreType.DMA((N_DEV,))],
        compiler_params=_params(56),
    )(*deps, dpa, dpb, x, dx1, g1, slab)


def _wgrad(a, b, name, tm, tk, out_dtype):
    n_tok, m = a.shape
    n = b.shape[1]
    k_steps = n_tok // tk

    def body(a_ref, b_ref, o_ref, acc_v):
        k = pl.program_id(1)

        @pl.when(k == 0)
        def _():
            acc_v[...] = jnp.zeros_like(acc_v)

        acc_v[...] += _dot_tn(a_ref[...], b_ref[...])

        @pl.when(k == k_steps - 1)
        def _():
            o_ref[...] = acc_v[...].astype(o_ref.dtype)

    return pl.pallas_call(
        body, name=name, grid=(m // tm, k_steps),
        in_specs=[pl.BlockSpec((tk, tm), lambda i, k: (k, i)), pl.BlockSpec((tk, n), lambda i, k: (k, 0))],
        out_specs=pl.BlockSpec((tm, n), lambda i, k: (i, 0)),
        out_shape=pltpu.HBM((m, n), out_dtype),
        scratch_shapes=[pltpu.VMEM((tm, n), F32)],
        compiler_params=pltpu.CompilerParams(dimension_semantics=("arbitrary", "arbitrary"),
                                             vmem_limit_bytes=40 * MIB),
    )(a, b)


def _wgrad_in(dpa, dpb, h, tk):
    n_tok, dm = h.shape
    k_steps = n_tok // tk
    last = k_steps - 1

    def from_a(i):
        return (i == 0) | (i >= 5)

    def body(a_ref, b_ref, h_ref, o_ref, acc_v):
        i, k = pl.program_id(0), pl.program_id(1)

        @pl.when(k == 0)
        def _():
            acc_v[...] = jnp.zeros_like(acc_v)

        @pl.when(from_a(i))
        def _():
            acc_v[...] += _dot_tn(a_ref[...], h_ref[...])

        @pl.when(jnp.logical_not(from_a(i)))
        def _():
            acc_v[...] += _dot_tn(b_ref[...], h_ref[...])

        @pl.when(k == last)
        def _():
            o_ref[...] = acc_v[...].astype(o_ref.dtype)

    a_index = lambda i, k: (jnp.where(from_a(i), k, last), jnp.where(i >= 5, i - 4, 0))
    b_index = lambda i, k: (jnp.where(from_a(i), jnp.where(i == 0, 0, last), k), jnp.clip(i - 1, 0, 3))
    return pl.pallas_call(
        body, name="wgrad_in", grid=(7, k_steps),
        in_specs=[pl.BlockSpec((tk, dm), a_index), pl.BlockSpec((tk, dm), b_index),
                  pl.BlockSpec((tk, dm), lambda i, k: (k, 0))],
        out_specs=pl.BlockSpec((dm, dm), lambda i, k: (i, 0)),
        out_shape=pltpu.HBM((7 * dm, dm), BF16),
        scratch_shapes=[pltpu.VMEM((dm, dm), F32)],
        compiler_params=pltpu.CompilerParams(dimension_semantics=("arbitrary", "arbitrary"),
                                             vmem_limit_bytes=48 * MIB),
    )(dpa, dpb, h)


def _adamw(w, g, m, v):
    m = ADAM_B1 * m + (1.0 - ADAM_B1) * g
    v = ADAM_B2 * v + (1.0 - ADAM_B2) * (g * g)
    m_hat = m / (1.0 - ADAM_B1 ** ADAM_STEP)
    v_hat = v / (1.0 - ADAM_B2 ** ADAM_STEP)
    delta = -ADAM_LR * (m_hat / (jnp.sqrt(v_hat) + ADAM_EPS) + ADAM_WD * w)
    return delta, m, v


def _adam_big(recv, part, me, off, rows, w, m, v, transpose, name, tr):
    dm = recv.shape[2]
    per = rows // tr

    def body(me_ref, own_ref, r_ref, w_ref, m_ref, v_ref, g_ref, d_ref, mo_ref, vo_ref):
        g = own_ref[...].astype(F32)
        for k in range(len(FLIPS)):
            g = g + r_ref[k].astype(F32)
        if transpose:
            g = g.T
        delta, m_new, v_new = _adamw(w_ref[...], g, m_ref[...], v_ref[...])
        g_ref[...] = g
        d_ref[...] = delta
        mo_ref[...] = m_new
        vo_ref[...] = v_new

    if transpose:
        blk = pl.BlockSpec((dm, tr), lambda i, me_ref: (0, i))
    else:
        blk = pl.BlockSpec((tr, dm), lambda i, me_ref: (i, 0))
    first = off // tr
    return pl.pallas_call(
        body, name=name,
        grid_spec=pltpu.PrefetchScalarGridSpec(
            num_scalar_prefetch=1, grid=(per,),
            in_specs=[pl.BlockSpec((tr, dm), lambda i, me_ref: (me_ref[0] * per + i, 0)),
                      pl.BlockSpec((len(FLIPS), tr, dm), lambda i, me_ref: (0, first + i, 0)), blk, blk, blk],
            out_specs=[blk] * 4),
        out_shape=[jax.ShapeDtypeStruct(w.shape, F32)] * 4,
        compiler_params=_params(32),
    )(me, *[_in_hbm(a) for a in (part, recv, w, m, v)])


def _adam_rows_group(recv, parts, me, offs, rows, triples, name):
    dm = recv.shape[2]
    n = len(parts)

    def body(me_ref, *refs):
        ins, outs = refs[:5 * n], refs[5 * n:]
        for j in range(n):
            own_ref, r_ref, w_ref, m_ref, v_ref = ins[5 * j:5 * j + 5]
            g = own_ref[...].astype(F32)
            for k in range(len(FLIPS)):
                g = g + r_ref[k].astype(F32)
            delta, m_new, v_new = _adamw(w_ref[...], g, m_ref[...], v_ref[...])
            for ref, val in zip(outs[4 * j:4 * j + 4], (g, delta, m_new, v_new)):
                ref[...] = val

    blk = pl.BlockSpec((rows, dm), lambda i, me_ref: (0, 0))
    in_specs, args = [], []
    for part, off, triple in zip(parts, offs, triples):
        in_specs += [pl.BlockSpec((rows, dm), lambda i, me_ref: (me_ref[0], 0)),
                     pl.BlockSpec((len(FLIPS), rows, dm), lambda i, me_ref, first=off // rows: (0, first, 0)),
                     blk, blk, blk]
        args += [part, recv, *triple]
    out = pl.pallas_call(
        body, name=name,
        grid_spec=pltpu.PrefetchScalarGridSpec(num_scalar_prefetch=1, grid=(1,), in_specs=in_specs,
                                               out_specs=[blk] * (4 * n)),
        out_shape=[jax.ShapeDtypeStruct((rows, dm), F32)] * (4 * n),
        compiler_params=_params(48),
    )(me, *[_in_hbm(a) for a in args])
    return [tuple(out[4 * j:4 * j + 4]) for j in range(n)]


LOSS_ROW = 15
CONV_A_ROW = 16
CONV_B_ROW = 24


def _adam_small(recv_small, recv_last, small_own, last_own, me, params, d_model):
    n = len(params)
    cw = d_model // N_DEV

    def body(me_ref, r_ref, rc_ref, l_ref, so_ref, soc_ref, lo_ref, *refs):
        ins, loss_ref, outs = refs[:3 * n], refs[3 * n], refs[3 * n + 1:3 * n + 1 + 4 * n]
        g_v, gc_v, last_v = refs[3 * n + 1 + 4 * n:]
        me_pos = me_ref[0]

        def slot(d, own_ref, slots_ref):
            return jnp.where(d == me_pos, own_ref[...], slots_ref[d])

        g, gc, last = slot(0, so_ref, r_ref), slot(0, soc_ref, rc_ref), slot(0, lo_ref, l_ref)
        for d in range(1, N_DEV):
            g, gc, last = g + slot(d, so_ref, r_ref), gc + slot(d, soc_ref, rc_ref), last + slot(d, lo_ref, l_ref)
        g_v[...], gc_v[...], last_v[...] = g, gc, last
        loss_ref[...] = (0.5 / d_model) * jnp.sum(g_v[LOSS_ROW:LOSS_ROW + 1, :], axis=-1, keepdims=True)
        for j, (row0, own_columns, (w, _, _)) in enumerate(params):
            w_ref, m_ref, v_ref = ins[3 * j:3 * j + 3]
            source = gc_v if own_columns else (last_v if row0 == 0 else g_v)
            width = source.shape[1]
            for c in range(w.shape[1] // width):
                cols = slice(c * width, (c + 1) * width)
                grad = source[row0 + c * w.shape[0]:row0 + (c + 1) * w.shape[0], :]
                delta, m_new, v_new = _adamw(w_ref[:, cols], grad, m_ref[:, cols], v_ref[:, cols])
                for ref, val in zip(outs[4 * j:4 * j + 4], (grad, delta, m_new, v_new)):
                    ref[:, cols] = val

    full = lambda shape: pl.BlockSpec(shape, lambda i, me_ref: (0,) * len(shape))
    stack_rows = recv_small.shape[1]
    flat = [a for _, _, triple in params for a in triple]
    shapes = [w.shape for _, _, (w, _, _) in params for _ in range(4)]
    out = pl.pallas_call(
        body, name="adam_small",
        grid_spec=pltpu.PrefetchScalarGridSpec(
            num_scalar_prefetch=1, grid=(1,),
            in_specs=[full(recv_small.shape),
                      pl.BlockSpec((N_DEV, stack_rows, cw), lambda i, me_ref: (0, 0, me_ref[0])),
                      full(recv_last.shape), full(small_own.shape),
                      pl.BlockSpec((stack_rows, cw), lambda i, me_ref: (0, me_ref[0])),
                      full(last_own.shape)] + [full(a.shape) for a in flat],
            out_specs=[full((1, 1))] + [full(s) for s in shapes],
            scratch_shapes=[pltpu.VMEM((stack_rows, d_model), F32), pltpu.VMEM((stack_rows, cw), F32),
                            pltpu.VMEM(recv_last.shape[1:], F32)]),
        out_shape=[jax.ShapeDtypeStruct((1, 1), F32)] + [jax.ShapeDtypeStruct(s, F32) for s in shapes],
    )(me, *[_in_hbm(a) for a in (recv_small, recv_small, recv_last, small_own, small_own, last_own, *flat)])
    return out[0], [tuple(out[1 + 4 * j:5 + 4 * j]) for j in range(n)]


def _tile(n_tok, want):
    return min(want, n_tok)


def kernel(x, norm1_pre_g, w_in, b_in, conv_a_w, conv_a_b, w_a_out, conv_b_w, conv_b_b, ln_b_g, ln_b_b, w_b_out, w_o, norm1_post_g, norm2_pre_g, w_mlp_in, w_mlp_out, norm2_post_g, loss_target, m_norm1_pre_g, m_w_in, m_b_in, m_conv_a_w, m_conv_a_b, m_w_a_out, m_conv_b_w, m_conv_b_b, m_ln_b_g, m_ln_b_b, m_w_b_out, m_w_o, m_norm1_post_g, m_norm2_pre_g, m_w_mlp_in, m_w_mlp_out, m_norm2_post_g, v_norm1_pre_g, v_w_in, v_b_in, v_conv_a_w, v_conv_a_b, v_w_a_out, v_conv_b_w, v_conv_b_b, v_ln_b_g, v_ln_b_b, v_w_b_out, v_w_o, v_norm1_post_g, v_norm2_pre_g, v_w_mlp_in, v_w_mlp_out, v_norm2_post_g):
    n_tok, dm = x.shape[1], x.shape[2]
    rows, off, slab_rows = _layout(dm)
    cw = dm // N_DEV
    xs = x.reshape(n_tok, dm)
    tgt = loss_target.reshape(n_tok, dm)
    row = lambda vec: vec.reshape(1, -1)
    scattered = lambda group: jax.ShapeDtypeStruct((len(FLIPS), slab_rows[group], dm), BF16)
    tm, tk = min(dm, 1024), _tile(n_tok, 2048)
    me = (4 * lax.axis_index("x") + 2 * lax.axis_index("y") + lax.axis_index("c")).astype(jnp.int32).reshape(1)

    conv_own = jnp.concatenate([conv_a_w, jnp.zeros((SUBLANE - CONV_A, cw), F32), conv_b_w,
                                jnp.zeros((1, cw), F32)], axis=0)
    own_in, land_in = _place_cast([(w_in, True)], me, "place_w_in")
    slab_in, conv_all = _all_gather_two_level([own_in, conv_own], [land_in, None], "gather_w_in")
    conv_full = conv_all.transpose(1, 0, 2).reshape(conv_own.shape[0], dm)
    caw, cbw = conv_full[0:CONV_A], conv_full[SUBLANE:SUBLANE + CONV_B]
    own_abo, land_abo = _place_cast([(w_a_out, False), (w_b_out, False), (w_o, False)], me, "place_abo")
    own_mlp, land_mlp = _place_cast([(w_mlp_in, True), (w_mlp_out, False)], me, "place_mlp")
    ag_abo = _exchange_start([_Part(own_abo, False, slab_rows["abo"], 0, 0)], [land_abo],
                             "gather_abo_start", after=slab_in)
    ag_mlp = _exchange_start([_Part(own_mlp, False, slab_rows["mlp"], 0, 0)], [land_mlp],
                             "gather_mlp_start", after=ag_abo.token)

    proj, p, u, h = _fwd_in(xs, row(norm1_pre_g), row(b_in), slab_in, _tile(n_tok, 512),
                            [ag_abo.token, ag_mlp.token])
    _, (slab_abo,) = _exchange_wait(ag_abo, "gather_abo_wait", after=proj)
    va, vb, ya, yb, qa, sb, merged, mix, x1 = _fwd_mix(
        p, u, proj, xs, caw, row(conv_a_b), cbw, row(conv_b_b), row(ln_b_g), row(ln_b_b), row(norm1_post_g),
        slab_abo, _tile(n_tok, 256))
    _, (slab_mlp,) = _exchange_wait(ag_mlp, "gather_mlp_wait", after=x1)
    f, df1, h2, df2, dmix, dx1, small_mlp = _mlp_fwd_bwd(
        x1, mix, tgt, row(norm1_post_g), row(norm2_pre_g), row(norm2_post_g), slab_mlp, _tile(n_tok, 256))

    rs_mlp = _exchange_start(
        [_Part(_wgrad(df1, h2, "wgrad_mlp_in", tm, tk, BF16), True, rows["w1"], 0, off["w1"]),
         _Part(_wgrad(f, df2, "wgrad_mlp_out", tm, tk, BF16), True, rows["w2"], 0, off["w2"])],
        [scattered("mlp")], "scatter_mlp_start")
    dpa, dya, dyb, dva, dvb, small_mix = _bwd_mix(
        dmix, ya, yb, proj, va, vb, row(ln_b_g), row(ln_b_b), slab_abo, _tile(n_tok, 512), [rs_mlp.token])
    rs_abo = _exchange_start(
        [_Part(_wgrad(qa, dya, "wgrad_a_out", tm, tk, BF16), True, rows["wa"], 0, off["wa"]),
         _Part(_wgrad(sb, dyb, "wgrad_b_out", tm, tk, BF16), True, rows["wb"], 0, off["wb"]),
         _Part(_wgrad(merged, dmix, "wgrad_o", tm, tk, BF16), True, rows["wo"], 0, off["wo"])],
        [scattered("abo")], "scatter_abo_start")
    dpb, small_conv = _bwd_conv(dva, dvb, p, u, proj, caw, cbw, _tile(n_tok, 256), [rs_abo.token])

    zeros = lambda r: jnp.zeros((r, dm), F32)
    small = jnp.concatenate([
        zeros(1),
        small_mix[2:3],
        small_mix[3:4],
        small_mix[0:2],
        small_mlp[2:3],
        small_mlp[1:2],
        small_mlp[0:1],
        small_mix[4:5], small_conv[3:7], small_mix[5:7],
        small_mlp[3:4],
        small_conv[0:CONV_A], zeros(SUBLANE - CONV_A),
        small_conv[8:8 + CONV_B], zeros(1),
    ], axis=0)

    rs_in = _exchange_start(
        [_Part(_wgrad_in(dpa, dpb, h, tk), True, rows["win"], 0, off["win"]),
         _Part(small, False, small.shape[0], 1, 0)],
        [scattered("in"), jax.ShapeDtypeStruct((N_DEV,) + small.shape, F32)], "scatter_in_start")
    grad_x, small_in = _bwd_in(dpa, dpb, xs, dx1, row(norm1_pre_g), slab_in, _tile(n_tok, 512), [rs_in.token])

    tr = min(LANE, rows["wa"])
    gather_last = _exchange_start([_Part(small_in, False, SUBLANE, 0, 0)],
                                  [jax.ShapeDtypeStruct((N_DEV, SUBLANE, dm), F32)], "gather_last_start")
    (g_w1, g_w2), (recv_mlp,) = _exchange_wait(rs_mlp, "scatter_mlp_wait", after=gather_last.token)
    (g_wa, g_wb, g_wo), (recv_abo,) = _exchange_wait(rs_abo, "scatter_abo_wait", after=grad_x)
    big = {
        "w_mlp_in": _adam_big(recv_mlp, g_w1, me, off["w1"], rows["w1"], w_mlp_in, m_w_mlp_in, v_w_mlp_in, True,
                              "adam_w_mlp_in", tr),
        "w_mlp_out": _adam_big(recv_mlp, g_w2, me, off["w2"], rows["w2"], w_mlp_out, m_w_mlp_out, v_w_mlp_out,
                               False, "adam_w_mlp_out", tr),
    }
    big["w_a_out"], big["w_b_out"], big["w_o"] = _adam_rows_group(
        recv_abo, (g_wa, g_wb, g_wo), me, (off["wa"], off["wb"], off["wo"]), rows["wa"],
        ((w_a_out, m_w_a_out, v_w_a_out), (w_b_out, m_w_b_out, v_w_b_out), (w_o, m_w_o, v_w_o)), "adam_abo")
    (g_win, small), (recv_in, recv_small) = _exchange_wait(rs_in, "scatter_in_wait", after=big["w_o"][3])
    big["w_in"] = _adam_big(recv_in, g_win, me, off["win"], rows["win"], w_in, m_w_in, v_w_in, True, "adam_w_in", tr)
    (small_in,), (recv_last,) = _exchange_wait(gather_last, "gather_last_wait", after=big["w_in"][3])

    small_names = ("norm1_pre_g", "conv_a_b", "conv_b_b", "ln_b_g", "ln_b_b", "norm1_post_g", "norm2_pre_g",
                   "norm2_post_g")
    given = dict(
        norm1_pre_g=(norm1_pre_g, m_norm1_pre_g, v_norm1_pre_g), conv_a_b=(conv_a_b, m_conv_a_b, v_conv_a_b),
        conv_b_b=(conv_b_b, m_conv_b_b, v_conv_b_b), ln_b_g=(ln_b_g, m_ln_b_g, v_ln_b_g),
        ln_b_b=(ln_b_b, m_ln_b_b, v_ln_b_b), norm1_post_g=(norm1_post_g, m_norm1_post_g, v_norm1_post_g),
        norm2_pre_g=(norm2_pre_g, m_norm2_pre_g, v_norm2_pre_g),
        norm2_post_g=(norm2_post_g, m_norm2_post_g, v_norm2_post_g))
    params = [(j, False, tuple(row(a) for a in given[name])) for j, name in enumerate(small_names)]
    params.append((SUBLANE, False, tuple(row(a) for a in (b_in, m_b_in, v_b_in))))
    params.append((CONV_A_ROW, True, (conv_a_w, m_conv_a_w, v_conv_a_w)))
    params.append((CONV_B_ROW, True, (conv_b_w, m_conv_b_w, v_conv_b_w)))
    loss, small_out = _adam_small(recv_small, recv_last, small, small_in, me, params, dm)
    small_leaves = {name: tuple(a.reshape(dm) for a in small_out[j]) for j, name in enumerate(small_names)}
    small_leaves["b_in"] = tuple(a.reshape(7 * dm) for a in small_out[len(small_names)])
    small_leaves["conv_a_w"] = small_out[len(small_names) + 1]
    small_leaves["conv_b_w"] = small_out[len(small_names) + 2]

    order = ("norm1_pre_g", "w_in", "b_in", "conv_a_w", "conv_a_b", "w_a_out", "conv_b_w", "conv_b_b", "ln_b_g",
             "ln_b_b", "w_b_out", "w_o", "norm1_post_g", "norm2_pre_g", "w_mlp_in", "w_mlp_out", "norm2_post_g")
    leaves = [big[name] if name in big else small_leaves[name] for name in order]
    grads, deltas, new_m, new_v = zip(*leaves)
    return (loss.reshape(()), grad_x.reshape(x.shape), *grads, *deltas, *new_m, *new_v)
```

```python
from typing import NamedTuple

import jax
import jax.numpy as jnp
from jax import lax
from jax.experimental import pallas as pl
from jax.experimental.pallas import tpu as pltpu

F32 = jnp.float32
BF16 = jnp.bfloat16

RMS_EPS = 1e-6
LN_EPS = 1e-5
ADAM_LR = 0.001
ADAM_B1 = 0.9
ADAM_B2 = 0.999
ADAM_EPS = 1e-08
ADAM_WD = 0.01
ADAM_STEP = 10

N_DEV = 8
CONV_A = 3
CONV_B = 31
LANE = 128
SUBLANE = 8
HALO = 16
FWD_CONV_ROWS = 32
BWD_CONV_ROWS = 64
MIB = 1 << 20
FLIPS = ((0, 0, 1), (0, 1, 0), (1, 0, 0), (0, 1, 1), (1, 0, 1), (1, 1, 0), (1, 1, 1))
MESH = pl.DeviceIdType.MESH


def _layout(d_model):
    e = d_model // N_DEV
    rows = {"win": 7 * e, "w1": 4 * e, "w2": 4 * e, "wa": e, "wb": e, "wo": e}
    off = {"win": 0, "w1": 0, "w2": 4 * e, "wa": 0, "wb": e, "wo": 2 * e}
    return rows, off, {"in": 7 * e, "mlp": 8 * e, "abo": 3 * e}


def _after(body, deps):
    def ordered(*refs):
        return body(*refs[len(deps):])
    return ordered


def _params(vmem_mib):
    return pltpu.CompilerParams(dimension_semantics=("arbitrary",), vmem_limit_bytes=vmem_mib * MIB)


def _whole():
    return pl.BlockSpec(memory_space=pltpu.VMEM)


def _hbm():
    return pl.BlockSpec(memory_space=pl.ANY)


def _in_hbm(a):
    return pltpu.with_memory_space_constraint(a, pltpu.HBM)


def _rows(ts, width, col=0):
    return pl.BlockSpec((ts, width), lambda i: (i, col))


def _halo_specs(ts, width, n_rows):
    per = ts // HALO
    last = n_rows // HALO - 1
    return [
        pl.BlockSpec((ts, width), lambda i: (i, 0)),
        pl.BlockSpec((HALO, width), lambda i: (jnp.maximum(i * per - 1, 0), 0)),
        pl.BlockSpec((HALO, width), lambda i: (jnp.minimum((i + 1) * per, last), 0)),
    ]


def _dot(a, b):
    return jnp.dot(a, b, preferred_element_type=F32)


def _dot_nt(a, b):
    return lax.dot_general(a, b, (((1,), (1,)), ((), ())), preferred_element_type=F32)


def _dot_tn(a, b):
    return lax.dot_general(a, b, (((0,), (0,)), ((), ())), preferred_element_type=F32)


def _rms(u):
    return lax.rsqrt(jnp.mean(u * u, axis=-1, keepdims=True) + RMS_EPS)


def _rms_bwd(dz, u, r, g):
    dzg = dz * g
    return r * dzg - u * (r * r * r) * jnp.mean(dzg * u, axis=-1, keepdims=True)


def _colsum(v):
    return jnp.sum(v, axis=0, keepdims=True)


def _sigmoid(v):
    return jax.nn.sigmoid(v)


def _weight_copies(slab_ref, off, rows, dst_ref, sems, first_sem):
    return [pltpu.make_async_copy(slab_ref.at[d, pl.ds(off, rows), :], dst_ref.at[pl.ds(d * rows, rows), :],
                                  sems.at[first_sem + d]) for d in range(N_DEV)]


def _on_first_step(copies, method):
    @pl.when(pl.program_id(0) == 0)
    def _():
        for cp in copies:
            getattr(cp, method)()


def _with_halos(main_ref, prev_ref, next_ref, i, n_steps):
    return (main_ref, jnp.where(i > 0, prev_ref[...], 0.0), jnp.where(i < n_steps - 1, next_ref[...], 0.0))


def _broadcast_taps(w_ref, wb_ref, n_taps):
    for k in range(n_taps):
        wb_ref[k] = jnp.broadcast_to(w_ref[k:k + 1, :], wb_ref.shape[1:])


def _conv_tile(tile, wb_ref, starts, ts, width, emit, rolled_ref, rows=FWD_CONV_ROWS):
    main_ref, prev, nxt = tile
    span = ts + 2 * HALO
    nv = rows // SUBLANE
    for cb in range(width // LANE):
        lanes = slice(cb * LANE, (cb + 1) * LANE)
        slot = cb % 2
        window = jnp.concatenate([prev[:, lanes], main_ref[:, lanes], nxt[:, lanes]], axis=0)
        for b in sorted({st % SUBLANE for st in starts}):
            rolled_ref[slot, b] = window if b == 0 else pltpu.roll(window, span - b, axis=0)
        for r0 in range(0, ts, rows):
            acc = jnp.zeros((nv, SUBLANE, LANE), F32)
            for k, st in enumerate(starts):
                shifted = rolled_ref[slot, st % SUBLANE, pl.ds(r0 + st - st % SUBLANE, rows), :]
                acc = acc + shifted.reshape(nv, SUBLANE, LANE) * wb_ref[k, :, lanes][None]
            emit(r0, pl.ds(cb * LANE, LANE), acc.reshape(rows, LANE))


def _window(tile, r0, cb, ts, rows):
    main_ref, prev, nxt = tile
    lanes = slice(cb * LANE, (cb + 1) * LANE)
    lo, hi = max(r0 - HALO, 0), min(r0 + rows + HALO, ts)
    pieces = [prev[:, lanes]] if r0 - HALO < 0 else []
    pieces.append(main_ref[lo:hi, lanes])
    if r0 + rows + HALO > ts:
        pieces.append(nxt[:, lanes])
    return pieces[0] if len(pieces) == 1 else jnp.concatenate(pieces, axis=0)


def _phases(starts):
    groups = {}
    for k, st in enumerate(starts):
        groups.setdefault(st % SUBLANE, []).append((k, st // SUBLANE))
    return sorted(groups.items())


def _shifted(blk, b):
    n = blk.shape[0]
    rolled = blk if b == 0 else pltpu.roll(blk, n - b, axis=0)
    return rolled.reshape(n // SUBLANE, SUBLANE, blk.shape[1])


def _conv_bwd_tile(dv_tile, u_ref, wb_ref, acc_ref, n_taps, ts, width, emit, rows=BWD_CONV_ROWS):
    groups = _phases(_bwd_starts(n_taps))
    nv = rows // SUBLANE
    for r0 in range(0, ts, rows):
        for cb in range(width // LANE):
            lanes = pl.ds(cb * LANE, LANE)
            blk = _window(dv_tile, r0, cb, ts, rows)
            u = u_ref[pl.ds(r0, rows), lanes].reshape(nv, SUBLANE, LANE)
            du = jnp.zeros((nv, SUBLANE, LANE), F32)
            for b, taps in groups:
                sh = _shifted(blk, b)
                for k, m in taps:
                    du = du + sh[m:m + nv] * wb_ref[k, :, lanes][None]
                    acc_ref[k, :, lanes] += jnp.sum(sh[m:m + nv] * u, axis=0)
            emit(r0, lanes, du.reshape(rows, LANE))


def _fwd_starts(n_taps):
    pad = (n_taps - 1) // 2
    return [HALO - pad + k for k in range(n_taps)]


def _bwd_starts(n_taps):
    pad = (n_taps - 1) // 2
    return [HALO + pad - k for k in range(n_taps)]


def _peer(x, y, c, flip):
    fx, fy, fc = flip
    return (1 - x if fx else x, 1 - y if fy else y, 1 - c if fc else c)


def _place_cast(pieces, me, name):
    n = len(pieces)
    counts = [a.shape[1] if t else a.shape[0] for a, t in pieces]
    width = pieces[0][0].shape[0] if pieces[0][1] else pieces[0][0].shape[1]
    total = sum(counts)

    def body(me_ref, *refs):
        ins, own_ref, land_ref = refs[:n], refs[n], refs[n + 1]
        first = 0
        for (a, transpose), in_ref, count in zip(pieces, ins, counts):
            block = (in_ref[...].T if transpose else in_ref[...]).astype(BF16)
            own_ref[first:first + count, :] = block
            land_ref[first:first + count, :] = block
            first += count

    return pl.pallas_call(
        body, name=name,
        grid_spec=pltpu.PrefetchScalarGridSpec(
            num_scalar_prefetch=1, grid=(1,),
            in_specs=[pl.BlockSpec(a.shape, lambda i, me_ref: (0, 0)) for a, _ in pieces],
            out_specs=[pl.BlockSpec((total, width), lambda i, me_ref: (0, 0)),
                       pl.BlockSpec((None, total, width), lambda i, me_ref: (me_ref[0], 0, 0))]),
        out_shape=[pltpu.HBM((total, width), BF16), pltpu.HBM((N_DEV, total, width), BF16)],
        compiler_params=_params(32),
    )(me, *[_in_hbm(a) for a, _ in pieces])


def _all_gather_two_level(shards, placed, name):
    n = len(shards)
    given = [j for j in range(n) if placed[j] is not None]

    def body(*refs):
        ins, outs = refs[:n], refs[n + len(given):2 * n + len(given)]
        send_sems, recv_sems, local_sems = refs[2 * n + len(given):]
        x, y, c = lax.axis_index("x"), lax.axis_index("y"), lax.axis_index("c")
        me, sibling = (x, y, c), (x, y, 1 - c)
        relayed, direct, diagonal = (x ^ (1 - c), y ^ c), (x ^ c, y ^ (1 - c)), (1 - x, 1 - y)

        def slot(j, dev):
            return outs[j].at[4 * dev[0] + 2 * dev[1] + dev[2]]

        def copy(k, j, block, to, src=None):
            return pltpu.make_async_remote_copy(
                src_ref=slot(j, block) if src is None else src, dst_ref=slot(j, block),
                send_sem=send_sems.at[k * n + j], recv_sem=recv_sems.at[k * n + j], device_id=to, device_id_type=MESH)

        local = [pltpu.make_async_copy(ins[j], slot(j, me), local_sems.at[j]) for j in range(n) if j not in given]
        for cp in local:
            cp.start()
        sent = [copy(1, j, me, (*relayed, c), src=ins[j]) for j in range(n)]
        sent += [copy(2, j, me, (*direct, c), src=ins[j]) for j in range(n)]
        sent += [copy(0, j, me, sibling, src=ins[j]) for j in range(n)]
        for cp in sent:
            cp.start()
        for k, chip in ((1, relayed), (2, direct), (3, diagonal)):
            for j in range(n):
                copy(k, j, (*chip, c), me).wait_recv()
                if k == 1:
                    sent.append(copy(3, j, (*chip, c), (*direct, c)))
                    sent[-1].start()
                sent.append(copy(3 + k, j, (*chip, c), sibling))
                sent[-1].start()
        for j in range(n):
            copy(0, j, sibling, me).wait_recv()
        for k, chip in ((4, direct), (5, relayed), (6, diagonal)):
            for j in range(n):
                copy(k, j, (*chip, 1 - c), me).wait_recv()
        for cp in sent:
            cp.wait_send()
        for cp in local:
            cp.wait()

    return pl.pallas_call(
        body, name=name,
        out_shape=[jax.ShapeDtypeStruct((N_DEV,) + s.shape, s.dtype) for s in shards],
        in_specs=[_hbm()] * (n + len(given)), out_specs=[_hbm()] * n,
        input_output_aliases={n + i: j for i, j in enumerate(given)},
        scratch_shapes=[pltpu.SemaphoreType.DMA((7 * n,)), pltpu.SemaphoreType.DMA((7 * n,)),
                        pltpu.SemaphoreType.DMA((n,))],
    )(*shards, *[placed[j] for j in given])


class _Part(NamedTuple):
    src: jax.Array
    scatter: bool
    rows: int
    land: int
    off: int


class _Started(NamedTuple):
    send_sems: jax.Array
    recv_sems: jax.Array
    thru: tuple
    token: jax.Array
    parts: tuple


def _exchange_copies(srcs, lands, send_sems, recv_sems, parts, only=None):
    n = len(parts)
    x, y, c = lax.axis_index("x"), lax.axis_index("y"), lax.axis_index("c")
    me = 4 * x + 2 * y + c

    def block(j, dev):
        p = parts[j]
        return srcs[j].at[pl.ds(pl.multiple_of(dev * p.rows, SUBLANE), p.rows), :] if p.scatter else srcs[j]

    def slot(j, index):
        p = parts[j]
        return lands[p.land].at[index, pl.ds(p.off, p.rows), :]

    sends, recvs = [], []
    for k, flip in enumerate(FLIPS):
        px, py, pc = _peer(x, y, c, flip)
        peer = 4 * px + 2 * py + pc
        for j in (range(n) if only is None else only):
            sems = dict(send_sem=send_sems.at[k * n + j], recv_sem=recv_sems.at[k * n + j],
                        device_id=(px, py, pc), device_id_type=MESH)
            to, got = (k, k) if parts[j].scatter else (me, peer)
            sends.append(pltpu.make_async_remote_copy(src_ref=block(j, peer), dst_ref=slot(j, to), **sems))
            recvs.append(pltpu.make_async_remote_copy(src_ref=block(j, peer), dst_ref=slot(j, got), **sems))
    return sends, recvs


def _exchange_start(parts, lands, name, after=None):
    n, nl = len(parts), len(lands)
    n_in = n + nl + (after is not None)

    def body(*refs):
        srcs, land_refs = refs[:n], refs[n:n + nl]
        send_sems, recv_sems = refs[n_in], refs[n_in + 1]
        token = refs[n_in + 2 + n + nl]
        sends, _ = _exchange_copies(srcs, land_refs, send_sems, recv_sems, parts)
        for cp in sends:
            cp.start()
        token[...] = jnp.zeros_like(token)

    hbm = pl.BlockSpec(memory_space=pltpu.HBM)
    sem = pl.BlockSpec(memory_space=pltpu.SEMAPHORE)
    fresh = lambda s: lax.empty(s.shape, s.dtype) if isinstance(s, jax.ShapeDtypeStruct) else s
    args = [pltpu.with_memory_space_constraint(p.src, pltpu.HBM) for p in parts]
    args += [pltpu.with_memory_space_constraint(fresh(s), pltpu.HBM) for s in lands]
    args += [] if after is None else [after]
    out = pl.pallas_call(
        body, name=name,
        out_shape=(pltpu.SemaphoreType.DMA((7 * n,)), pltpu.SemaphoreType.DMA((7 * n,)),
                   *[pltpu.HBM(a.shape, a.dtype) for a in args[:n + nl]], jax.ShapeDtypeStruct((SUBLANE, LANE), F32)),
        in_specs=[hbm] * (n + nl) + [_hbm()] * (after is not None),
        out_specs=(sem, sem, *[hbm] * (n + nl), _whole()),
        input_output_aliases={j: 2 + j for j in range(n + nl)},
        compiler_params=pltpu.CompilerParams(has_side_effects=pltpu.SideEffectType.DATAFLOW_SIDE_EFFECTING),
    )(*args)
    return _Started(out[0], out[1], tuple(out[2:2 + n + nl]), out[2 + n + nl], tuple(parts))


def _exchange_wait(started, name, after, only=None):
    parts = started.parts
    js = list(range(len(parts))) if only is None else list(only)
    ls = sorted({parts[j].land for j in js})
    thru = [started.thru[j] for j in js] + [started.thru[len(parts) + l] for l in ls]
    n, nl = len(js), len(ls)

    def body(*refs):
        srcs, land_refs = [None] * len(parts), [None] * (len(started.thru) - len(parts))
        for j, ref in zip(js, refs[:n]):
            srcs[j] = ref
        for l, ref in zip(ls, refs[n:n + nl]):
            land_refs[l] = ref
        send_sems, recv_sems = refs[n + nl], refs[n + nl + 1]
        sends, recvs = _exchange_copies(srcs, land_refs, send_sems, recv_sems, parts, js)
        for cp in sends:
            cp.wait_send()
        for cp in recvs:
            cp.wait_recv()

    hbm = pl.BlockSpec(memory_space=pltpu.HBM)
    sem = pl.BlockSpec(memory_space=pltpu.SEMAPHORE)
    out = pl.pallas_call(
        body, name=name,
        out_shape=tuple(pltpu.HBM(a.shape, a.dtype) for a in thru),
        in_specs=[hbm] * (n + nl) + [sem, sem, _hbm()], out_specs=tuple([hbm] * (n + nl)),
        input_output_aliases={j: j for j in range(n + nl)},
        compiler_params=pltpu.CompilerParams(has_side_effects=pltpu.SideEffectType.DATAFLOW_SIDE_EFFECTING),
    )(*thru, started.send_sems, started.recv_sems, after)
    return list(out[:n]), list(out[n:])


def _fwd_in(x, g1, b_in, slab, ts, deps):
    n_tok, dm = x.shape
    rows, off, _ = _layout(dm)
    width = 7 * dm

    def body(x_ref, g1_ref, b_ref, slab_ref, proj_ref, p_ref, u_ref, h_ref, w_v, sems):
        copies = _weight_copies(slab_ref, off["win"], rows["win"], w_v, sems, 0)
        _on_first_step(copies, "start")
        _on_first_step(copies, "wait")
        xv = x_ref[...]
        h = (xv * _rms(xv) * g1_ref[...]).astype(BF16)
        h_ref[...] = h
        cols = []
        for j in range(7):
            pj = _dot_nt(h, w_v[pl.ds(j * dm, dm), :]) + b_ref[:, j * dm:(j + 1) * dm]
            proj_ref[:, j * dm:(j + 1) * dm] = pj.astype(proj_ref.dtype)
            if 1 <= j <= 4:
                cols.append(pj)
            if j == 2:
                p_ref[...] = cols[0] * cols[1]
            if j == 4:
                u_ref[...] = cols[2] * _sigmoid(cols[3])

    return pl.pallas_call(
        _after(body, deps), name="fwd_in", grid=(n_tok // ts,),
        in_specs=[_whole()] * len(deps) + [_rows(ts, dm), _whole(), _whole(), _hbm()],
        out_specs=[_rows(ts, width), _rows(ts, dm), _rows(ts, dm), _rows(ts, dm)],
        out_shape=[jax.ShapeDtypeStruct((n_tok, width), BF16), jax.ShapeDtypeStruct((n_tok, dm), F32),
                   jax.ShapeDtypeStruct((n_tok, dm), F32), jax.ShapeDtypeStruct((n_tok, dm), BF16)],
        scratch_shapes=[pltpu.VMEM((width, dm), BF16), pltpu.SemaphoreType.DMA((N_DEV,))],
        compiler_params=_params(56),
    )(*deps, x, g1, b_in, slab)


def _fwd_mix(p, u, proj, x, caw, cab, cbw, cbb, lng, lnb, g1post, slab, ts):
    n_tok, dm = x.shape
    rows, off, _ = _layout(dm)
    n_steps = n_tok // ts

    def body(p_ref, p_prev, p_next, u_ref, u_prev, u_next, bg_ref, za_ref, zb_ref, x_ref,
             caw_ref, cab_ref, cbw_ref, cbb_ref, lng_ref, lnb_ref, g1p_ref, slab_ref,
             va_ref, vb_ref, ya_ref, yb_ref, qa_ref, sb_ref, mg_ref, mix_ref, x1_ref,
             wa_v, wb_v, wo_v, tap_a, tap_b, sems, rolled_a, rolled_b):
        i = pl.program_id(0)
        copies = (_weight_copies(slab_ref, off["wa"], rows["wa"], wa_v, sems, 0)
                  + _weight_copies(slab_ref, off["wb"], rows["wb"], wb_v, sems, N_DEV)
                  + _weight_copies(slab_ref, off["wo"], rows["wo"], wo_v, sems, 2 * N_DEV))
        _on_first_step(copies, "start")

        @pl.when(i == 0)
        def _():
            _broadcast_taps(caw_ref, tap_a, CONV_A)
            _broadcast_taps(cbw_ref, tap_b, CONV_B)

        def emit_a(r0, lanes, acc):
            va_ref[pl.ds(r0, acc.shape[0]), lanes] = acc + cab_ref[:, lanes]

        def emit_b(r0, lanes, acc):
            vb_ref[pl.ds(r0, acc.shape[0]), lanes] = acc + cbb_ref[:, lanes]

        _conv_tile(_with_halos(p_ref, p_prev, p_next, i, n_steps), tap_a, _fwd_starts(CONV_A), ts, dm, emit_a,
                   rolled_a)
        _conv_tile(_with_halos(u_ref, u_prev, u_next, i, n_steps), tap_b, _fwd_starts(CONV_B), ts, dm, emit_b,
                   rolled_b)
        _on_first_step(copies, "wait")

        qa = (bg_ref[...].astype(F32) * va_ref[...]).astype(BF16)
        qa_ref[...] = qa
        ya = _dot(qa, wa_v[...])
        vb = vb_ref[...]
        xc = vb - jnp.mean(vb, axis=-1, keepdims=True)
        rstd = lax.rsqrt(jnp.mean(xc * xc, axis=-1, keepdims=True) + LN_EPS)
        ln = xc * rstd * lng_ref[...] + lnb_ref[...]
        sb = (ln * _sigmoid(ln)).astype(BF16)
        sb_ref[...] = sb
        yb = _dot(sb, wb_v[...])
        ya_ref[...] = ya.astype(BF16)
        yb_ref[...] = yb.astype(BF16)
        merged = (_sigmoid(za_ref[...].astype(F32)) * ya + _sigmoid(zb_ref[...].astype(F32)) * yb).astype(BF16)
        mg_ref[...] = merged
        mix = _dot(merged, wo_v[...])
        mix_ref[...] = mix
        x1_ref[...] = x_ref[...] + mix * _rms(mix) * g1p_ref[...]

    tok = lambda dt: jax.ShapeDtypeStruct((n_tok, dm), dt)
    return pl.pallas_call(
        body, name="fwd_mix", grid=(n_steps,),
        in_specs=(_halo_specs(ts, dm, n_tok) + _halo_specs(ts, dm, n_tok)
                  + [_rows(ts, dm, 0), _rows(ts, dm, 5), _rows(ts, dm, 6), _rows(ts, dm)]
                  + [_whole()] * 7 + [_hbm()]),
        out_specs=[_rows(ts, dm)] * 9,
        out_shape=[tok(F32), tok(F32), tok(BF16), tok(BF16), tok(BF16), tok(BF16), tok(BF16), tok(F32), tok(F32)],
        scratch_shapes=[pltpu.VMEM((dm, dm), BF16), pltpu.VMEM((dm, dm), BF16), pltpu.VMEM((dm, dm), BF16),
                        pltpu.VMEM((CONV_A, SUBLANE, dm), F32), pltpu.VMEM((CONV_B, SUBLANE, dm), F32),
                        pltpu.SemaphoreType.DMA((3 * N_DEV,)),
                        pltpu.VMEM((2, SUBLANE, ts + 2 * HALO, LANE), F32),
                        pltpu.VMEM((2, SUBLANE, ts + 2 * HALO, LANE), F32)],
        compiler_params=_params(48),
    )(p, p, p, u, u, u, proj, proj, proj, x, caw, cab, cbw, cbb, lng, lnb, g1post, slab)


def _mlp_fwd_bwd(x1, mix, tgt, g1post, g2pre, g2post, slab, ts):
    n_tok, dm = x1.shape
    rows, off, _ = _layout(dm)
    ff = 4 * dm

    def body(x1_ref, mix_ref, t_ref, g1p_ref, g2pre_ref, g2post_ref, slab_ref,
             f_ref, df1_ref, h2_ref, df2_ref, dmix_ref, dx1_ref, small_ref, w1_v, w2_v, relu_v, sems):
        w1_copies = _weight_copies(slab_ref, off["w1"], rows["w1"], w1_v, sems, 0)
        w2_copies = _weight_copies(slab_ref, off["w2"], rows["w2"], w2_v, sems, N_DEV)
        _on_first_step(w1_copies + w2_copies, "start")

        @pl.when(pl.program_id(0) == 0)
        def _():
            small_ref[...] = jnp.zeros_like(small_ref)

        _on_first_step(w1_copies + w2_copies, "wait")
        x1v = x1_ref[...]
        r3 = _rms(x1v)
        g2pre = g2pre_ref[...]
        h2 = (x1v * r3 * g2pre).astype(BF16)
        h2_ref[...] = h2
        for c in range(4):
            blk = pl.ds(c * dm, dm)
            relu = jnp.maximum(_dot_nt(h2, w1_v[blk, :]), 0.0)
            relu_v[:, c * dm:(c + 1) * dm] = relu
            f_ref[:, c * dm:(c + 1) * dm] = (relu * relu).astype(BF16)
        f2 = _dot(f_ref[...], w2_v[...])
        r4 = _rms(f2)
        g2post = g2post_ref[...]
        err = x1v + f2 * r4 * g2post - t_ref[...]
        dy = err * (1.0 / dm)
        small_ref[3:4, :] += _colsum(err * err)
        small_ref[0:1, :] += _colsum(dy * f2 * r4)
        df2 = _rms_bwd(dy, f2, r4, g2post).astype(BF16)
        df2_ref[...] = df2
        for c in range(4):
            blk = pl.ds(c * dm, dm)
            df1 = (_dot_nt(df2, w2_v[blk, :]) * (2.0 * relu_v[:, c * dm:(c + 1) * dm])).astype(BF16)
            df1_ref[:, c * dm:(c + 1) * dm] = df1
        dh2 = _dot(df1_ref[...], w1_v[...])
        small_ref[1:2, :] += _colsum(dh2 * x1v * r3)
        dx1 = dy + _rms_bwd(dh2, x1v, r3, g2pre)
        dx1_ref[...] = dx1
        mixv = mix_ref[...]
        r2 = _rms(mixv)
        small_ref[2:3, :] += _colsum(dx1 * mixv * r2)
        dmix_ref[...] = _rms_bwd(dx1, mixv, r2, g1p_ref[...]).astype(BF16)

    tok = lambda w, dt: jax.ShapeDtypeStruct((n_tok, w), dt)
    return pl.pallas_call(
        body, name="mlp_fwd_bwd", grid=(n_tok // ts,),
        in_specs=[_rows(ts, dm)] * 3 + [_whole()] * 3 + [_hbm()],
        out_specs=[_rows(ts, ff), _rows(ts, ff), _rows(ts, dm), _rows(ts, dm), _rows(ts, dm), _rows(ts, dm),
                   pl.BlockSpec((SUBLANE, dm), lambda i: (0, 0))],
        out_shape=[tok(ff, BF16), tok(ff, BF16), tok(dm, BF16), tok(dm, BF16), tok(dm, BF16), tok(dm, F32),
                   jax.ShapeDtypeStruct((SUBLANE, dm), F32)],
        scratch_shapes=[pltpu.VMEM((ff, dm), BF16), pltpu.VMEM((ff, dm), BF16), pltpu.VMEM((ts, ff), F32),
                        pltpu.SemaphoreType.DMA((2 * N_DEV,))],
        compiler_params=_params(56),
    )(x1, mix, tgt, g1post, g2pre, g2post, slab)


def _bwd_mix(dmix, ya, yb, proj, va, vb, lng, lnb, slab, ts, deps):
    n_tok, dm = dmix.shape
    rows, off, _ = _layout(dm)
    n_steps = n_tok // ts

    def body(dmix_ref, ya_ref, yb_ref, bg_ref, za_ref, zb_ref, va_ref, vb_ref, lng_ref, lnb_ref, slab_ref,
             dpa_ref, dya_ref, dyb_ref, dva_ref, dvb_ref, small_ref, wa_v, wb_v, wo_v, sems):
        wo_copies = _weight_copies(slab_ref, off["wo"], rows["wo"], wo_v, sems, 2 * N_DEV)
        ab_copies = (_weight_copies(slab_ref, off["wa"], rows["wa"], wa_v, sems, 0)
                     + _weight_copies(slab_ref, off["wb"], rows["wb"], wb_v, sems, N_DEV))
        _on_first_step(wo_copies + ab_copies, "start")

        @pl.when(pl.program_id(0) == 0)
        def _():
            small_ref[...] = jnp.zeros_like(small_ref)

        _on_first_step(wo_copies, "wait")
        dmerged = _dot_nt(dmix_ref[...], wo_v[...])
        _on_first_step(ab_copies, "wait")
        sa = _sigmoid(za_ref[...].astype(F32))
        sg = _sigmoid(zb_ref[...].astype(F32))
        dza = dmerged * ya_ref[...].astype(F32) * sa * (1.0 - sa)
        dzb = dmerged * yb_ref[...].astype(F32) * sg * (1.0 - sg)
        dpa_ref[:, dm:2 * dm] = dza.astype(BF16)
        dpa_ref[:, 2 * dm:3 * dm] = dzb.astype(BF16)
        small_ref[5:6, :] += _colsum(dza)
        small_ref[6:7, :] += _colsum(dzb)

        dya = (dmerged * sa).astype(BF16)
        dya_ref[...] = dya
        dqa = _dot_nt(dya, wa_v[...])
        dbg = dqa * va_ref[...]
        dpa_ref[:, 0:dm] = dbg.astype(BF16)
        small_ref[4:5, :] += _colsum(dbg)
        dva = dqa * bg_ref[...].astype(F32)
        dva_ref[...] = dva
        small_ref[2:3, :] += _colsum(dva)

        dyb = (dmerged * sg).astype(BF16)
        dyb_ref[...] = dyb
        dsb = _dot_nt(dyb, wb_v[...])
        vb = vb_ref[...]
        xc = vb - jnp.mean(vb, axis=-1, keepdims=True)
        rstd = lax.rsqrt(jnp.mean(xc * xc, axis=-1, keepdims=True) + LN_EPS)
        nrm = xc * rstd
        lng_v = lng_ref[...]
        ln = nrm * lng_v + lnb_ref[...]
        sl = _sigmoid(ln)
        dln = dsb * (sl * (1.0 + ln * (1.0 - sl)))
        small_ref[0:1, :] += _colsum(dln * nrm)
        small_ref[1:2, :] += _colsum(dln)
        dn = dln * lng_v
        dvb = rstd * (dn - jnp.mean(dn, axis=-1, keepdims=True)
                      - nrm * jnp.mean(dn * nrm, axis=-1, keepdims=True))
        dvb_ref[...] = dvb
        small_ref[3:4, :] += _colsum(dvb)

    tok = lambda w, dt: jax.ShapeDtypeStruct((n_tok, w), dt)
    return pl.pallas_call(
        _after(body, deps), name="bwd_mix", grid=(n_steps,),
        in_specs=([_whole()] * len(deps) + [_rows(ts, dm)] * 3
                  + [_rows(ts, dm, 0), _rows(ts, dm, 5), _rows(ts, dm, 6)]
                  + [_rows(ts, dm)] * 2 + [_whole()] * 2 + [_hbm()]),
        out_specs=[_rows(ts, 3 * dm), _rows(ts, dm), _rows(ts, dm), _rows(ts, dm), _rows(ts, dm),
                   pl.BlockSpec((SUBLANE, dm), lambda i: (0, 0))],
        out_shape=[tok(3 * dm, BF16), tok(dm, BF16), tok(dm, BF16), tok(dm, F32), tok(dm, F32),
                   jax.ShapeDtypeStruct((SUBLANE, dm), F32)],
        scratch_shapes=[pltpu.VMEM((dm, dm), BF16), pltpu.VMEM((dm, dm), BF16), pltpu.VMEM((dm, dm), BF16),
                        pltpu.SemaphoreType.DMA((3 * N_DEV,))],
        compiler_params=_params(56),
    )(*deps, dmix, ya, yb, proj, proj, proj, va, vb, lng, lnb, slab)


def _bwd_conv(dva, dvb, p, u, proj, caw, cbw, ts, deps):
    n_tok, dm = dva.shape
    n_steps = n_tok // ts
    small_rows = 40

    def body(dva_ref, dva_prev, dva_next, dvb_ref, dvb_prev, dvb_next, p_ref, u_ref,
             cg_ref, ha_ref, a_ref, g_ref, caw_ref, cbw_ref,
             dproj_ref, small_ref,
             dp_v, du_v, tap_a, tap_b, gwa_v, gwb_v):
        i = pl.program_id(0)

        @pl.when(i == 0)
        def _():
            _broadcast_taps(caw_ref, tap_a, CONV_A)
            _broadcast_taps(cbw_ref, tap_b, CONV_B)
            small_ref[...] = jnp.zeros_like(small_ref)
            gwa_v[...] = jnp.zeros_like(gwa_v)
            gwb_v[...] = jnp.zeros_like(gwb_v)

        def emit_dp(r0, lanes, acc):
            dp_v[pl.ds(r0, acc.shape[0]), lanes] = acc

        def emit_du(r0, lanes, acc):
            du_v[pl.ds(r0, acc.shape[0]), lanes] = acc

        _conv_bwd_tile(_with_halos(dva_ref, dva_prev, dva_next, i, n_steps), p_ref, tap_a, gwa_v, CONV_A, ts, dm,
                       emit_dp)
        _conv_bwd_tile(_with_halos(dvb_ref, dvb_prev, dvb_next, i, n_steps), u_ref, tap_b, gwb_v, CONV_B, ts, dm,
                       emit_du)

        dp = dp_v[...]
        dcg = dp * ha_ref[...].astype(F32)
        dha = dp * cg_ref[...].astype(F32)
        du = du_v[...]
        sg = _sigmoid(g_ref[...].astype(F32))
        da = du * sg
        dg = du * a_ref[...].astype(F32) * sg * (1.0 - sg)
        dproj_ref[:, 0:dm] = dcg.astype(BF16)
        dproj_ref[:, dm:2 * dm] = dha.astype(BF16)
        dproj_ref[:, 2 * dm:3 * dm] = da.astype(BF16)
        dproj_ref[:, 3 * dm:4 * dm] = dg.astype(BF16)
        small_ref[3:4, :] += _colsum(dcg)
        small_ref[4:5, :] += _colsum(dha)
        small_ref[5:6, :] += _colsum(da)
        small_ref[6:7, :] += _colsum(dg)

        @pl.when(i == n_steps - 1)
        def _():
            for k in range(CONV_A):
                small_ref[k:k + 1, :] = _colsum(gwa_v[k])
            for k in range(CONV_B):
                small_ref[SUBLANE + k:SUBLANE + k + 1, :] = _colsum(gwb_v[k])

    return pl.pallas_call(
        _after(body, deps), name="bwd_conv", grid=(n_steps,),
        in_specs=([_whole()] * len(deps) + _halo_specs(ts, dm, n_tok) * 2 + [_rows(ts, dm)] * 2
                  + [_rows(ts, dm, 1), _rows(ts, dm, 2), _rows(ts, dm, 3), _rows(ts, dm, 4)]
                  + [_whole()] * 2),
        out_specs=[_rows(ts, 4 * dm), pl.BlockSpec((small_rows, dm), lambda i: (0, 0))],
        out_shape=[jax.ShapeDtypeStruct((n_tok, 4 * dm), BF16), jax.ShapeDtypeStruct((small_rows, dm), F32)],
        scratch_shapes=[pltpu.VMEM((ts, dm), F32), pltpu.VMEM((ts, dm), F32),
                        pltpu.VMEM((CONV_A, SUBLANE, dm), F32), pltpu.VMEM((CONV_B, SUBLANE, dm), F32),
                        pltpu.VMEM((CONV_A, SUBLANE, dm), F32), pltpu.VMEM((CONV_B, SUBLANE, dm), F32)],
        compiler_params=_params(48),
    )(*deps, dva, dva, dva, dvb, dvb, dvb, p, u, proj, proj, proj, proj, caw, cbw)


def _bwd_in(dpa, dpb, x, dx1, g1, slab, ts, deps):
    n_tok, dm = x.shape
    rows, off, _ = _layout(dm)
    width = 7 * dm

    def body(dpa_ref, dpb_ref, x_ref, dx1_ref, g1_ref, slab_ref, gx_ref, small_ref, w_v, sems):
        copies = _weight_copies(slab_ref, off["win"], rows["win"], w_v, sems, 0)
        _on_first_step(copies, "start")

        @pl.when(pl.program_id(0) == 0)
        def _():
            small_ref[...] = jnp.zeros_like(small_ref)

        _on_first_step(copies, "wait")
        dh = (_dot(dpa_ref[:, 0:dm], w_v[0:dm, :]) + _dot(dpb_ref[...], w_v[dm:5 * dm, :])
              + _dot(dpa_ref[:, dm:3 * dm], w_v[5 * dm:7 * dm, :]))
        xv = x_ref[...]
        r1 = _rms(xv)
        small_ref[0:1, :] += _colsum(dh * xv * r1)
        gx_ref[...] = dx1_ref[...] + _rms_bwd(dh, xv, r1, g1_ref[...])

    return pl.pallas_call(
        _after(body, deps), name="bwd_in", grid=(n_tok // ts,),
        in_specs=[_whole()] * len(deps) + [_rows(ts, 3 * dm), _rows(ts, 4 * dm), _rows(ts, dm), _rows(ts, dm),
                                           _whole(), _hbm()],
        out_specs=[_rows(ts, dm), pl.BlockSpec((SUBLANE, dm), lambda i: (0, 0))],
        out_shape=[jax.ShapeDtypeStruct((n_tok, dm), F32), jax.ShapeDtypeStruct((SUBLANE, dm), F32)],
        scratch_shapes=[pltpu.VMEM((width, dm), BF16), pltpu.SemaphoreType.DMA((N_DEV,))],
        compiler_params=_params(56),
    )(*deps, dpa, dpb, x, dx1, g1, slab)


def _wgrad(a, b, name, tm, tk, out_dtype):
    n_tok, m = a.shape
    n = b.shape[1]
    k_steps = n_tok // tk

    def body(a_ref, b_ref, o_ref, acc_v):
        k = pl.program_id(1)

        @pl.when(k == 0)
        def _():
            acc_v[...] = jnp.zeros_like(acc_v)

        acc_v[...] += _dot_tn(a_ref[...], b_ref[...])

        @pl.when(k == k_steps - 1)
        def _():
            o_ref[...] = acc_v[...].astype(o_ref.dtype)

    return pl.pallas_call(
        body, name=name, grid=(m // tm, k_steps),
        in_specs=[pl.BlockSpec((tk, tm), lambda i, k: (k, i)), pl.BlockSpec((tk, n), lambda i, k: (k, 0))],
        out_specs=pl.BlockSpec((tm, n), lambda i, k: (i, 0)),
        out_shape=pltpu.HBM((m, n), out_dtype),
        scratch_shapes=[pltpu.VMEM((tm, n), F32)],
        compiler_params=pltpu.CompilerParams(dimension_semantics=("arbitrary", "arbitrary"),
                                             vmem_limit_bytes=40 * MIB),
    )(a, b)


def _wgrad_in(dpa, dpb, h, tk):
    n_tok, dm = h.shape
    k_steps = n_tok // tk
    last = k_steps - 1

    def from_a(i):
        return (i == 0) | (i >= 5)

    def body(a_ref, b_ref, h_ref, o_ref, acc_v):
        i, k = pl.program_id(0), pl.program_id(1)

        @pl.when(k == 0)
        def _():
            acc_v[...] = jnp.zeros_like(acc_v)

        @pl.when(from_a(i))
        def _():
            acc_v[...] += _dot_tn(a_ref[...], h_ref[...])

        @pl.when(jnp.logical_not(from_a(i)))
        def _():
            acc_v[...] += _dot_tn(b_ref[...], h_ref[...])

        @pl.when(k == last)
        def _():
            o_ref[...] = acc_v[...].astype(o_ref.dtype)

    a_index = lambda i, k: (jnp.where(from_a(i), k, last), jnp.where(i >= 5, i - 4, 0))
    b_index = lambda i, k: (jnp.where(from_a(i), jnp.where(i == 0, 0, last), k), jnp.clip(i - 1, 0, 3))
    return pl.pallas_call(
        body, name="wgrad_in", grid=(7, k_steps),
        in_specs=[pl.BlockSpec((tk, dm), a_index), pl.BlockSpec((tk, dm), b_index),
                  pl.BlockSpec((tk, dm), lambda i, k: (k, 0))],
        out_specs=pl.BlockSpec((dm, dm), lambda i, k: (i, 0)),
        out_shape=pltpu.HBM((7 * dm, dm), BF16),
        scratch_shapes=[pltpu.VMEM((dm, dm), F32)],
        compiler_params=pltpu.CompilerParams(dimension_semantics=("arbitrary", "arbitrary"),
                                             vmem_limit_bytes=48 * MIB),
    )(dpa, dpb, h)


def _adamw(w, g, m, v):
    m = ADAM_B1 * m + (1.0 - ADAM_B1) * g
    v = ADAM_B2 * v + (1.0 - ADAM_B2) * (g * g)
    m_hat = m / (1.0 - ADAM_B1 ** ADAM_STEP)
    v_hat = v / (1.0 - ADAM_B2 ** ADAM_STEP)
    delta = -ADAM_LR * (m_hat / (jnp.sqrt(v_hat) + ADAM_EPS) + ADAM_WD * w)
    return delta, m, v


def _adam_big(recv, part, me, off, rows, w, m, v, transpose, name, tr):
    dm = recv.shape[2]
    per = rows // tr

    def body(me_ref, own_ref, r_ref, w_ref, m_ref, v_ref, g_ref, d_ref, mo_ref, vo_ref):
        g = own_ref[...].astype(F32)
        for k in range(len(FLIPS)):
            g = g + r_ref[k].astype(F32)
        if transpose:
            g = g.T
        delta, m_new, v_new = _adamw(w_ref[...], g, m_ref[...], v_ref[...])
        g_ref[...] = g
        d_ref[...] = delta
        mo_ref[...] = m_new
        vo_ref[...] = v_new

    if transpose:
        blk = pl.BlockSpec((dm, tr), lambda i, me_ref: (0, i))
    else:
        blk = pl.BlockSpec((tr, dm), lambda i, me_ref: (i, 0))
    first = off // tr
    return pl.pallas_call(
        body, name=name,
        grid_spec=pltpu.PrefetchScalarGridSpec(
            num_scalar_prefetch=1, grid=(per,),
            in_specs=[pl.BlockSpec((tr, dm), lambda i, me_ref: (me_ref[0] * per + i, 0)),
                      pl.BlockSpec((len(FLIPS), tr, dm), lambda i, me_ref: (0, first + i, 0)), blk, blk, blk],
            out_specs=[blk] * 4),
        out_shape=[jax.ShapeDtypeStruct(w.shape, F32)] * 4,
        compiler_params=_params(32),
    )(me, *[_in_hbm(a) for a in (part, recv, w, m, v)])


def _adam_rows_group(recv, parts, me, offs, rows, triples, name):
    dm = recv.shape[2]
    n = len(parts)

    def body(me_ref, *refs):
        ins, outs = refs[:5 * n], refs[5 * n:]
        for j in range(n):
            own_ref, r_ref, w_ref, m_ref, v_ref = ins[5 * j:5 * j + 5]
            g = own_ref[...].astype(F32)
            for k in range(len(FLIPS)):
                g = g + r_ref[k].astype(F32)
            delta, m_new, v_new = _adamw(w_ref[...], g, m_ref[...], v_ref[...])
            for ref, val in zip(outs[4 * j:4 * j + 4], (g, delta, m_new, v_new)):
                ref[...] = val

    blk = pl.BlockSpec((rows, dm), lambda i, me_ref: (0, 0))
    in_specs, args = [], []
    for part, off, triple in zip(parts, offs, triples):
        in_specs += [pl.BlockSpec((rows, dm), lambda i, me_ref: (me_ref[0], 0)),
                     pl.BlockSpec((len(FLIPS), rows, dm), lambda i, me_ref, first=off // rows: (0, first, 0)),
                     blk, blk, blk]
        args += [part, recv, *triple]
    out = pl.pallas_call(
        body, name=name,
        grid_spec=pltpu.PrefetchScalarGridSpec(num_scalar_prefetch=1, grid=(1,), in_specs=in_specs,
                                               out_specs=[blk] * (4 * n)),
        out_shape=[jax.ShapeDtypeStruct((rows, dm), F32)] * (4 * n),
        compiler_params=_params(48),
    )(me, *[_in_hbm(a) for a in args])
    return [tuple(out[4 * j:4 * j + 4]) for j in range(n)]


LOSS_ROW = 15
CONV_A_ROW = 16
CONV_B_ROW = 24


def _adam_small(recv_small, recv_last, small_own, last_own, me, params, d_model):
    n = len(params)
    cw = d_model // N_DEV

    def body(me_ref, r_ref, rc_ref, l_ref, so_ref, soc_ref, lo_ref, *refs):
        ins, loss_ref, outs = refs[:3 * n], refs[3 * n], refs[3 * n + 1:3 * n + 1 + 4 * n]
        g_v, gc_v, last_v = refs[3 * n + 1 + 4 * n:]
        me_pos = me_ref[0]

        def slot(d, own_ref, slots_ref):
            return jnp.where(d == me_pos, own_ref[...], slots_ref[d])

        g, gc, last = slot(0, so_ref, r_ref), slot(0, soc_ref, rc_ref), slot(0, lo_ref, l_ref)
        for d in range(1, N_DEV):
            g, gc, last = g + slot(d, so_ref, r_ref), gc + slot(d, soc_ref, rc_ref), last + slot(d, lo_ref, l_ref)
        g_v[...], gc_v[...], last_v[...] = g, gc, last
        loss_ref[...] = (0.5 / d_model) * jnp.sum(g_v[LOSS_ROW:LOSS_ROW + 1, :], axis=-1, keepdims=True)
        for j, (row0, own_columns, (w, _, _)) in enumerate(params):
            w_ref, m_ref, v_ref = ins[3 * j:3 * j + 3]
            source = gc_v if own_columns else (last_v if row0 == 0 else g_v)
            width = source.shape[1]
            for c in range(w.shape[1] // width):
                cols = slice(c * width, (c + 1) * width)
                grad = source[row0 + c * w.shape[0]:row0 + (c + 1) * w.shape[0], :]
                delta, m_new, v_new = _adamw(w_ref[:, cols], grad, m_ref[:, cols], v_ref[:, cols])
                for ref, val in zip(outs[4 * j:4 * j + 4], (grad, delta, m_new, v_new)):
                    ref[:, cols] = val

    full = lambda shape: pl.BlockSpec(shape, lambda i, me_ref: (0,) * len(shape))
    stack_rows = recv_small.shape[1]
    flat = [a for _, _, triple in params for a in triple]
    shapes = [w.shape for _, _, (w, _, _) in params for _ in range(4)]
    out = pl.pallas_call(
        body, name="adam_small",
        grid_spec=pltpu.PrefetchScalarGridSpec(
            num_scalar_prefetch=1, grid=(1,),
            in_specs=[full(recv_small.shape),
                      pl.BlockSpec((N_DEV, stack_rows, cw), lambda i, me_ref: (0, 0, me_ref[0])),
                      full(recv_last.shape), full(small_own.shape),
                      pl.BlockSpec((stack_rows, cw), lambda i, me_ref: (0, me_ref[0])),
                      full(last_own.shape)] + [full(a.shape) for a in flat],
            out_specs=[full((1, 1))] + [full(s) for s in shapes],
            scratch_shapes=[pltpu.VMEM((stack_rows, d_model), F32), pltpu.VMEM((stack_rows, cw), F32),
                            pltpu.VMEM(recv_last.shape[1:], F32)]),
        out_shape=[jax.ShapeDtypeStruct((1, 1), F32)] + [jax.ShapeDtypeStruct(s, F32) for s in shapes],
    )(me, *[_in_hbm(a) for a in (recv_small, recv_small, recv_last, small_own, small_own, last_own, *flat)])
    return out[0], [tuple(out[1 + 4 * j:5 + 4 * j]) for j in range(n)]


def _tile(n_tok, want):
    return min(want, n_tok)


def kernel(x, norm1_pre_g, w_in, b_in, conv_a_w, conv_a_b, w_a_out, conv_b_w, conv_b_b, ln_b_g, ln_b_b, w_b_out, w_o, norm1_post_g, norm2_pre_g, w_mlp_in, w_mlp_out, norm2_post_g, loss_target, m_norm1_pre_g, m_w_in, m_b_in, m_conv_a_w, m_conv_a_b, m_w_a_out, m_conv_b_w, m_conv_b_b, m_ln_b_g, m_ln_b_b, m_w_b_out, m_w_o, m_norm1_post_g, m_norm2_pre_g, m_w_mlp_in, m_w_mlp_out, m_norm2_post_g, v_norm1_pre_g, v_w_in, v_b_in, v_conv_a_w, v_conv_a_b, v_w_a_out, v_conv_b_w, v_conv_b_b, v_ln_b_g, v_ln_b_b, v_w_b_out, v_w_o, v_norm1_post_g, v_norm2_pre_g, v_w_mlp_in, v_w_mlp_out, v_norm2_post_g):
    n_tok, dm = x.shape[1], x.shape[2]
    rows, off, slab_rows = _layout(dm)
    cw = dm // N_DEV
    xs = x.reshape(n_tok, dm)
    tgt = loss_target.reshape(n_tok, dm)
    row = lambda vec: vec.reshape(1, -1)
    scattered = lambda group: jax.ShapeDtypeStruct((len(FLIPS), slab_rows[group], dm), BF16)
    tm, tk = min(dm, 1024), _tile(n_tok, 2048)
    me = (4 * lax.axis_index("x") + 2 * lax.axis_index("y") + lax.axis_index("c")).astype(jnp.int32).reshape(1)

    conv_own = jnp.concatenate([conv_a_w, jnp.zeros((SUBLANE - CONV_A, cw), F32), conv_b_w,
                                jnp.zeros((1, cw), F32)], axis=0)
    own_in, land_in = _place_cast([(w_in, True)], me, "place_w_in")
    slab_in, conv_all = _all_gather_two_level([own_in, conv_own], [land_in, None], "gather_w_in")
    conv_full = conv_all.transpose(1, 0, 2).reshape(conv_own.shape[0], dm)
    caw, cbw = conv_full[0:CONV_A], conv_full[SUBLANE:SUBLANE + CONV_B]
    own_abo, land_abo = _place_cast([(w_a_out, False), (w_b_out, False), (w_o, False)], me, "place_abo")
    own_mlp, land_mlp = _place_cast([(w_mlp_in, True), (w_mlp_out, False)], me, "place_mlp")
    ag = _exchange_start([_Part(own_abo, False, slab_rows["abo"], 0, 0), _Part(own_mlp, False, slab_rows["mlp"], 1, 0)],
                         [land_abo, land_mlp], "gather_abo_mlp_start", after=slab_in)

    proj, p, u, h = _fwd_in(xs, row(norm1_pre_g), row(b_in), slab_in, _tile(n_tok, 512), [ag.token])
    _, (slab_abo,) = _exchange_wait(ag, "gather_abo_wait", after=proj, only=[0])
    va, vb, ya, yb, qa, sb, merged, mix, x1 = _fwd_mix(
        p, u, proj, xs, caw, row(conv_a_b), cbw, row(conv_b_b), row(ln_b_g), row(ln_b_b), row(norm1_post_g),
        slab_abo, _tile(n_tok, 256))
    _, (slab_mlp,) = _exchange_wait(ag, "gather_mlp_wait", after=x1, only=[1])
    f, df1, h2, df2, dmix, dx1, small_mlp = _mlp_fwd_bwd(
        x1, mix, tgt, row(norm1_post_g), row(norm2_pre_g), row(norm2_post_g), slab_mlp, _tile(n_tok, 256))

    rs_mlp = _exchange_start(
        [_Part(_wgrad(df1, h2, "wgrad_mlp_in", tm, tk, BF16), True, rows["w1"], 0, off["w1"]),
         _Part(_wgrad(f, df2, "wgrad_mlp_out", tm, tk, BF16), True, rows["w2"], 0, off["w2"])],
        [scattered("mlp")], "scatter_mlp_start")
    dpa, dya, dyb, dva, dvb, small_mix = _bwd_mix(
        dmix, ya, yb, proj, va, vb, row(ln_b_g), row(ln_b_b), slab_abo, _tile(n_tok, 512), [rs_mlp.token])
    rs_abo = _exchange_start(
        [_Part(_wgrad(qa, dya, "wgrad_a_out", tm, tk, BF16), True, rows["wa"], 0, off["wa"]),
         _Part(_wgrad(sb, dyb, "wgrad_b_out", tm, tk, BF16), True, rows["wb"], 0, off["wb"]),
         _Part(_wgrad(merged, dmix, "wgrad_o", tm, tk, BF16), True, rows["wo"], 0, off["wo"])],
        [scattered("abo")], "scatter_abo_start")
    dpb, small_conv = _bwd_conv(dva, dvb, p, u, proj, caw, cbw, _tile(n_tok, 256), [rs_abo.token])

    zeros = lambda r: jnp.zeros((r, dm), F32)
    small = jnp.concatenate([
        zeros(1),
        small_mix[2:3],
        small_mix[3:4],
        small_mix[0:2],
        small_mlp[2:3],
        small_mlp[1:2],
        small_mlp[0:1],
        small_mix[4:5], small_conv[3:7], small_mix[5:7],
        small_mlp[3:4],
        small_conv[0:CONV_A], zeros(SUBLANE - CONV_A),
        small_conv[8:8 + CONV_B], zeros(1),
    ], axis=0)

    rs_in = _exchange_start(
        [_Part(_wgrad_in(dpa, dpb, h, tk), True, rows["win"], 0, off["win"]),
         _Part(small, False, small.shape[0], 1, 0)],
        [scattered("in"), jax.ShapeDtypeStruct((N_DEV,) + small.shape, F32)], "scatter_in_start")
    grad_x, small_in = _bwd_in(dpa, dpb, xs, dx1, row(norm1_pre_g), slab_in, _tile(n_tok, 512), [rs_in.token])

    tr = min(LANE, rows["wa"])
    gather_last = _exchange_start([_Part(small_in, False, SUBLANE, 0, 0)],
                                  [jax.ShapeDtypeStruct((N_DEV, SUBLANE, dm), F32)], "gather_last_start")
    (g_w1, g_w2), (recv_mlp,) = _exchange_wait(rs_mlp, "scatter_mlp_wait", after=gather_last.token)
    (g_wa, g_wb, g_wo), (recv_abo,) = _exchange_wait(rs_abo, "scatter_abo_wait", after=grad_x)
    big = {
        "w_mlp_in": _adam_big(recv_mlp, g_w1, me, off["w1"], rows["w1"], w_mlp_in, m_w_mlp_in, v_w_mlp_in, True,
                              "adam_w_mlp_in", tr),
        "w_mlp_out": _adam_big(recv_mlp, g_w2, me, off["w2"], rows["w2"], w_mlp_out, m_w_mlp_out, v_w_mlp_out,
                               False, "adam_w_mlp_out", tr),
    }
    big["w_a_out"], big["w_b_out"], big["w_o"] = _adam_rows_group(
        recv_abo, (g_wa, g_wb, g_wo), me, (off["wa"], off["wb"], off["wo"]), rows["wa"],
        ((w_a_out, m_w_a_out, v_w_a_out), (w_b_out, m_w_b_out, v_w_b_out), (w_o, m_w_o, v_w_o)), "adam_abo")
    (g_win, small), (recv_in, recv_small) = _exchange_wait(rs_in, "scatter_in_wait", after=big["w_o"][3])
    big["w_in"] = _adam_big(recv_in, g_win, me, off["win"], rows["win"], w_in, m_w_in, v_w_in, True, "adam_w_in", tr)
    (small_in,), (recv_last,) = _exchange_wait(gather_last, "gather_last_wait", after=big["w_in"][3])

    small_names = ("norm1_pre_g", "conv_a_b", "conv_b_b", "ln_b_g", "ln_b_b", "norm1_post_g", "norm2_pre_g",
                   "norm2_post_g")
    given = dict(
        norm1_pre_g=(norm1_pre_g, m_norm1_pre_g, v_norm1_pre_g), conv_a_b=(conv_a_b, m_conv_a_b, v_conv_a_b),
        conv_b_b=(conv_b_b, m_conv_b_b, v_conv_b_b), ln_b_g=(ln_b_g, m_ln_b_g, v_ln_b_g),
        ln_b_b=(ln_b_b, m_ln_b_b, v_ln_b_b), norm1_post_g=(norm1_post_g, m_norm1_post_g, v_norm1_post_g),
        norm2_pre_g=(norm2_pre_g, m_norm2_pre_g, v_norm2_pre_g),
        norm2_post_g=(norm2_post_g, m_norm2_post_g, v_norm2_post_g))
    params = [(j, False, tuple(row(a) for a in given[name])) for j, name in enumerate(small_names)]
    params.append((SUBLANE, False, tuple(row(a) for a in (b_in, m_b_in, v_b_in))))
    params.append((CONV_A_ROW, True, (conv_a_w, m_conv_a_w, v_conv_a_w)))
    params.append((CONV_B_ROW, True, (conv_b_w, m_conv_b_w, v_conv_b_w)))
    loss, small_out = _adam_small(recv_small, recv_last, small, small_in, me, params, dm)
    small_leaves = {name: tuple(a.reshape(dm) for a in small_out[j]) for j, name in enumerate(small_names)}
    small_leaves["b_in"] = tuple(a.reshape(7 * dm) for a in small_out[len(small_names)])
    small_leaves["conv_a_w"] = small_out[len(small_names) + 1]
    small_leaves["conv_b_w"] = small_out[len(small_names) + 2]

    order = ("norm1_pre_g", "w_in", "b_in", "conv_a_w", "conv_a_b", "w_a_out", "conv_b_w", "conv_b_b", "ln_b_g",
             "ln_b_b", "w_b_out", "w_o", "norm1_post_g", "norm2_pre_g", "w_mlp_in", "w_mlp_out", "norm2_post_g")
    leaves = [big[name] if name in big else small_leaves[name] for name in order]
    grads, deltas, new_m, new_v = zip(*leaves)
    return (loss.reshape(()), grad_x.reshape(x.shape), *grads, *deltas, *new_m, *new_v)
```

```python
from typing import NamedTuple

import jax
import jax.numpy as jnp
from jax import lax
from jax.experimental import pallas as pl
from jax.experimental.pallas import tpu as pltpu

F32 = jnp.float32
BF16 = jnp.bfloat16

RMS_EPS = 1e-6
LN_EPS = 1e-5
ADAM_LR = 0.001
ADAM_B1 = 0.9
ADAM_B2 = 0.999
ADAM_EPS = 1e-08
ADAM_WD = 0.01
ADAM_STEP = 10

N_DEV = 8
CONV_A = 3
CONV_B = 31
LANE = 128
SUBLANE = 8
HALO = 16
FWD_CONV_ROWS = 32
BWD_CONV_ROWS = 64
MIB = 1 << 20
FLIPS = ((0, 0, 1), (0, 1, 0), (1, 0, 0), (0, 1, 1), (1, 0, 1), (1, 1, 0), (1, 1, 1))
MESH = pl.DeviceIdType.MESH


def _layout(d_model):
    e = d_model // N_DEV
    rows = {"win": 7 * e, "w1": 4 * e, "w2": 4 * e, "wa": e, "wb": e, "wo": e}
    off = {"win": 0, "w1": 0, "w2": 4 * e, "wa": 0, "wb": e, "wo": 2 * e}
    return rows, off, {"in": 7 * e, "mlp": 8 * e, "abo": 3 * e}


def _after(body, deps):
    def ordered(*refs):
        return body(*refs[len(deps):])
    return ordered


def _params(vmem_mib):
    return pltpu.CompilerParams(dimension_semantics=("arbitrary",), vmem_limit_bytes=vmem_mib * MIB)


def _whole():
    return pl.BlockSpec(memory_space=pltpu.VMEM)


def _hbm():
    return pl.BlockSpec(memory_space=pl.ANY)


def _in_hbm(a):
    return pltpu.with_memory_space_constraint(a, pltpu.HBM)


def _rows(ts, width, col=0):
    return pl.BlockSpec((ts, width), lambda i: (i, col))


def _halo_specs(ts, width, n_rows):
    per = ts // HALO
    last = n_rows // HALO - 1
    return [
        pl.BlockSpec((ts, width), lambda i: (i, 0)),
        pl.BlockSpec((HALO, width), lambda i: (jnp.maximum(i * per - 1, 0), 0)),
        pl.BlockSpec((HALO, width), lambda i: (jnp.minimum((i + 1) * per, last), 0)),
    ]


def _dot(a, b):
    return jnp.dot(a, b, preferred_element_type=F32)


def _dot_nt(a, b):
    return lax.dot_general(a, b, (((1,), (1,)), ((), ())), preferred_element_type=F32)


def _dot_tn(a, b):
    return lax.dot_general(a, b, (((0,), (0,)), ((), ())), preferred_element_type=F32)


def _rms(u):
    return lax.rsqrt(jnp.mean(u * u, axis=-1, keepdims=True) + RMS_EPS)


def _rms_bwd(dz, u, r, g):
    dzg = dz * g
    return r * dzg - u * (r * r * r) * jnp.mean(dzg * u, axis=-1, keepdims=True)


def _colsum(v):
    return jnp.sum(v, axis=0, keepdims=True)


def _sigmoid(v):
    return jax.nn.sigmoid(v)


def _weight_copies(slab_ref, off, rows, dst_ref, sems, first_sem):
    return [pltpu.make_async_copy(slab_ref.at[d, pl.ds(off, rows), :], dst_ref.at[pl.ds(d * rows, rows), :],
                                  sems.at[first_sem + d]) for d in range(N_DEV)]


def _on_first_step(copies, method):
    @pl.when(pl.program_id(0) == 0)
    def _():
        for cp in copies:
            getattr(cp, method)()


def _with_halos(main_ref, prev_ref, next_ref, i, n_steps):
    return (main_ref, jnp.where(i > 0, prev_ref[...], 0.0), jnp.where(i < n_steps - 1, next_ref[...], 0.0))


def _broadcast_taps(w_ref, wb_ref, n_taps):
    for k in range(n_taps):
        wb_ref[k] = jnp.broadcast_to(w_ref[k:k + 1, :], wb_ref.shape[1:])


def _conv_tile(tile, wb_ref, starts, ts, width, emit, rolled_ref, rows=FWD_CONV_ROWS):
    main_ref, prev, nxt = tile
    span = ts + 2 * HALO
    nv = rows // SUBLANE
    for cb in range(width // LANE):
        lanes = slice(cb * LANE, (cb + 1) * LANE)
        slot = cb % 2
        window = jnp.concatenate([prev[:, lanes], main_ref[:, lanes], nxt[:, lanes]], axis=0)
        for b in sorted({st % SUBLANE for st in starts}):
            rolled_ref[slot, b] = window if b == 0 else pltpu.roll(window, span - b, axis=0)
        for r0 in range(0, ts, rows):
            acc = jnp.zeros((nv, SUBLANE, LANE), F32)
            for k, st in enumerate(starts):
                shifted = rolled_ref[slot, st % SUBLANE, pl.ds(r0 + st - st % SUBLANE, rows), :]
                acc = acc + shifted.reshape(nv, SUBLANE, LANE) * wb_ref[k, :, lanes][None]
            emit(r0, pl.ds(cb * LANE, LANE), acc.reshape(rows, LANE))


def _window(tile, r0, cb, ts, rows):
    main_ref, prev, nxt = tile
    lanes = slice(cb * LANE, (cb + 1) * LANE)
    lo, hi = max(r0 - HALO, 0), min(r0 + rows + HALO, ts)
    pieces = [prev[:, lanes]] if r0 - HALO < 0 else []
    pieces.append(main_ref[lo:hi, lanes])
    if r0 + rows + HALO > ts:
        pieces.append(nxt[:, lanes])
    return pieces[0] if len(pieces) == 1 else jnp.concatenate(pieces, axis=0)


def _phases(starts):
    groups = {}
    for k, st in enumerate(starts):
        groups.setdefault(st % SUBLANE, []).append((k, st // SUBLANE))
    return sorted(groups.items())


def _shifted(blk, b):
    n = blk.shape[0]
    rolled = blk if b == 0 else pltpu.roll(blk, n - b, axis=0)
    return rolled.reshape(n // SUBLANE, SUBLANE, blk.shape[1])


def _conv_bwd_tile(dv_tile, u_ref, wb_ref, acc_ref, n_taps, ts, width, emit, rows=BWD_CONV_ROWS):
    groups = _phases(_bwd_starts(n_taps))
    nv = rows // SUBLANE
    for r0 in range(0, ts, rows):
        for cb in range(width // LANE):
            lanes = pl.ds(cb * LANE, LANE)
            blk = _window(dv_tile, r0, cb, ts, rows)
            u = u_ref[pl.ds(r0, rows), lanes].reshape(nv, SUBLANE, LANE)
            du = jnp.zeros((nv, SUBLANE, LANE), F32)
            for b, taps in groups:
                sh = _shifted(blk, b)
                for k, m in taps:
                    du = du + sh[m:m + nv] * wb_ref[k, :, lanes][None]
                    acc_ref[k, :, lanes] += jnp.sum(sh[m:m + nv] * u, axis=0)
            emit(r0, lanes, du.reshape(rows, LANE))


def _fwd_starts(n_taps):
    pad = (n_taps - 1) // 2
    return [HALO - pad + k for k in range(n_taps)]


def _bwd_starts(n_taps):
    pad = (n_taps - 1) // 2
    return [HALO + pad - k for k in range(n_taps)]


def _peer(x, y, c, flip):
    fx, fy, fc = flip
    return (1 - x if fx else x, 1 - y if fy else y, 1 - c if fc else c)


def _place_cast(pieces, me, name):
    n = len(pieces)
    counts = [a.shape[1] if t else a.shape[0] for a, t in pieces]
    width = pieces[0][0].shape[0] if pieces[0][1] else pieces[0][0].shape[1]
    total = sum(counts)

    def body(me_ref, *refs):
        ins, own_ref, land_ref = refs[:n], refs[n], refs[n + 1]
        first = 0
        for (a, transpose), in_ref, count in zip(pieces, ins, counts):
            block = (in_ref[...].T if transpose else in_ref[...]).astype(BF16)
            own_ref[first:first + count, :] = block
            land_ref[first:first + count, :] = block
            first += count

    return pl.pallas_call(
        body, name=name,
        grid_spec=pltpu.PrefetchScalarGridSpec(
            num_scalar_prefetch=1, grid=(1,),
            in_specs=[pl.BlockSpec(a.shape, lambda i, me_ref: (0, 0)) for a, _ in pieces],
            out_specs=[pl.BlockSpec((total, width), lambda i, me_ref: (0, 0)),
                       pl.BlockSpec((None, total, width), lambda i, me_ref: (me_ref[0], 0, 0))]),
        out_shape=[pltpu.HBM((total, width), BF16), pltpu.HBM((N_DEV, total, width), BF16)],
        compiler_params=_params(32),
    )(me, *[_in_hbm(a) for a, _ in pieces])


def _all_gather_two_level(shards, placed, name):
    n = len(shards)
    given = [j for j in range(n) if placed[j] is not None]

    def body(*refs):
        ins, outs = refs[:n], refs[n + len(given):2 * n + len(given)]
        send_sems, recv_sems, local_sems = refs[2 * n + len(given):]
        x, y, c = lax.axis_index("x"), lax.axis_index("y"), lax.axis_index("c")
        me, sibling = (x, y, c), (x, y, 1 - c)
        relayed, direct, diagonal = (x ^ (1 - c), y ^ c), (x ^ c, y ^ (1 - c)), (1 - x, 1 - y)

        def slot(j, dev):
            return outs[j].at[4 * dev[0] + 2 * dev[1] + dev[2]]

        def copy(k, j, block, to, src=None):
            return pltpu.make_async_remote_copy(
                src_ref=slot(j, block) if src is None else src, dst_ref=slot(j, block),
                send_sem=send_sems.at[k * n + j], recv_sem=recv_sems.at[k * n + j], device_id=to, device_id_type=MESH)

        local = [pltpu.make_async_copy(ins[j], slot(j, me), local_sems.at[j]) for j in range(n) if j not in given]
        for cp in local:
            cp.start()
        sent = [copy(1, j, me, (*relayed, c), src=ins[j]) for j in range(n)]
        sent += [copy(2, j, me, (*direct, c), src=ins[j]) for j in range(n)]
        sent += [copy(0, j, me, sibling, src=ins[j]) for j in range(n)]
        for cp in sent:
            cp.start()
        for k, chip in ((1, relayed), (2, direct), (3, diagonal)):
            for j in range(n):
                copy(k, j, (*chip, c), me).wait_recv()
                if k == 1:
                    sent.append(copy(3, j, (*chip, c), (*direct, c)))
                    sent[-1].start()
                sent.append(copy(3 + k, j, (*chip, c), sibling))
                sent[-1].start()
        for j in range(n):
            copy(0, j, sibling, me).wait_recv()
        for k, chip in ((4, direct), (5, relayed), (6, diagonal)):
            for j in range(n):
                copy(k, j, (*chip, 1 - c), me).wait_recv()
        for cp in sent:
            cp.wait_send()
        for cp in local:
            cp.wait()

    return pl.pallas_call(
        body, name=name,
        out_shape=[jax.ShapeDtypeStruct((N_DEV,) + s.shape, s.dtype) for s in shards],
        in_specs=[_hbm()] * (n + len(given)), out_specs=[_hbm()] * n,
        input_output_aliases={n + i: j for i, j in enumerate(given)},
        scratch_shapes=[pltpu.SemaphoreType.DMA((7 * n,)), pltpu.SemaphoreType.DMA((7 * n,)),
                        pltpu.SemaphoreType.DMA((n,))],
    )(*shards, *[placed[j] for j in given])


class _Part(NamedTuple):
    src: jax.Array
    scatter: bool
    rows: int
    land: int
    off: int


class _Started(NamedTuple):
    send_sems: jax.Array
    recv_sems: jax.Array
    thru: tuple
    token: jax.Array
    parts: tuple


def _exchange_copies(srcs, lands, send_sems, recv_sems, parts, only=None):
    n = len(parts)
    x, y, c = lax.axis_index("x"), lax.axis_index("y"), lax.axis_index("c")
    me = 4 * x + 2 * y + c

    def block(j, dev):
        p = parts[j]
        return srcs[j].at[pl.ds(pl.multiple_of(dev * p.rows, SUBLANE), p.rows), :] if p.scatter else srcs[j]

    def slot(j, index):
        p = parts[j]
        return lands[p.land].at[index, pl.ds(p.off, p.rows), :]

    sends, recvs = [], []
    for k, flip in enumerate(FLIPS):
        px, py, pc = _peer(x, y, c, flip)
        peer = 4 * px + 2 * py + pc
        for j in (range(n) if only is None else only):
            sems = dict(send_sem=send_sems.at[k * n + j], recv_sem=recv_sems.at[k * n + j],
                        device_id=(px, py, pc), device_id_type=MESH)
            to, got = (k, k) if parts[j].scatter else (me, peer)
            sends.append(pltpu.make_async_remote_copy(src_ref=block(j, peer), dst_ref=slot(j, to), **sems))
            recvs.append(pltpu.make_async_remote_copy(src_ref=block(j, peer), dst_ref=slot(j, got), **sems))
    return sends, recvs


def _exchange_start(parts, lands, name, after=None):
    n, nl = len(parts), len(lands)
    n_in = n + nl + (after is not None)

    def body(*refs):
        srcs, land_refs = refs[:n], refs[n:n + nl]
        send_sems, recv_sems = refs[n_in], refs[n_in + 1]
        token = refs[n_in + 2 + n + nl]
        sends, _ = _exchange_copies(srcs, land_refs, send_sems, recv_sems, parts)
        for j in range(n):
            for cp in sends[j::n]:
                cp.start()
        token[...] = jnp.zeros_like(token)

    hbm = pl.BlockSpec(memory_space=pltpu.HBM)
    sem = pl.BlockSpec(memory_space=pltpu.SEMAPHORE)
    fresh = lambda s: lax.empty(s.shape, s.dtype) if isinstance(s, jax.ShapeDtypeStruct) else s
    args = [pltpu.with_memory_space_constraint(p.src, pltpu.HBM) for p in parts]
    args += [pltpu.with_memory_space_constraint(fresh(s), pltpu.HBM) for s in lands]
    args += [] if after is None else [after]
    out = pl.pallas_call(
        body, name=name,
        out_shape=(pltpu.SemaphoreType.DMA((7 * n,)), pltpu.SemaphoreType.DMA((7 * n,)),
                   *[pltpu.HBM(a.shape, a.dtype) for a in args[:n + nl]], jax.ShapeDtypeStruct((SUBLANE, LANE), F32)),
        in_specs=[hbm] * (n + nl) + [_hbm()] * (after is not None),
        out_specs=(sem, sem, *[hbm] * (n + nl), _whole()),
        input_output_aliases={j: 2 + j for j in range(n + nl)},
        compiler_params=pltpu.CompilerParams(has_side_effects=pltpu.SideEffectType.DATAFLOW_SIDE_EFFECTING),
    )(*args)
    return _Started(out[0], out[1], tuple(out[2:2 + n + nl]), out[2 + n + nl], tuple(parts))


def _exchange_wait(started, name, after, only=None):
    parts = started.parts
    js = list(range(len(parts))) if only is None else list(only)
    ls = sorted({parts[j].land for j in js})
    thru = [started.thru[j] for j in js] + [started.thru[len(parts) + l] for l in ls]
    n, nl = len(js), len(ls)

    def body(*refs):
        srcs, land_refs = [None] * len(parts), [None] * (len(started.thru) - len(parts))
        for j, ref in zip(js, refs[:n]):
            srcs[j] = ref
        for l, ref in zip(ls, refs[n:n + nl]):
            land_refs[l] = ref
        send_sems, recv_sems = refs[n + nl], refs[n + nl + 1]
        sends, recvs = _exchange_copies(srcs, land_refs, send_sems, recv_sems, parts, js)
        for cp in sends:
            cp.wait_send()
        for cp in recvs:
            cp.wait_recv()

    hbm = pl.BlockSpec(memory_space=pltpu.HBM)
    sem = pl.BlockSpec(memory_space=pltpu.SEMAPHORE)
    out = pl.pallas_call(
        body, name=name,
        out_shape=tuple(pltpu.HBM(a.shape, a.dtype) for a in thru),
        in_specs=[hbm] * (n + nl) + [sem, sem, _hbm()], out_specs=tuple([hbm] * (n + nl)),
        input_output_aliases={j: j for j in range(n + nl)},
        compiler_params=pltpu.CompilerParams(has_side_effects=pltpu.SideEffectType.DATAFLOW_SIDE_EFFECTING),
    )(*thru, started.send_sems, started.recv_sems, after)
    return list(out[:n]), list(out[n:])


def _fwd_in(x, g1, b_in, slab, ts, deps):
    n_tok, dm = x.shape
    rows, off, _ = _layout(dm)
    width = 7 * dm

    def body(x_ref, g1_ref, b_ref, slab_ref, proj_ref, p_ref, u_ref, h_ref, w_v, sems):
        copies = _weight_copies(slab_ref, off["win"], rows["win"], w_v, sems, 0)
        _on_first_step(copies, "start")
        _on_first_step(copies, "wait")
        xv = x_ref[...]
        h = (xv * _rms(xv) * g1_ref[...]).astype(BF16)
        h_ref[...] = h
        cols = []
        for j in range(7):
            pj = _dot_nt(h, w_v[pl.ds(j * dm, dm), :]) + b_ref[:, j * dm:(j + 1) * dm]
            proj_ref[:, j * dm:(j + 1) * dm] = pj.astype(proj_ref.dtype)
            if 1 <= j <= 4:
                cols.append(pj)
            if j == 2:
                p_ref[...] = cols[0] * cols[1]
            if j == 4:
                u_ref[...] = cols[2] * _sigmoid(cols[3])

    return pl.pallas_call(
        _after(body, deps), name="fwd_in", grid=(n_tok // ts,),
        in_specs=[_whole()] * len(deps) + [_rows(ts, dm), _whole(), _whole(), _hbm()],
        out_specs=[_rows(ts, width), _rows(ts, dm), _rows(ts, dm), _rows(ts, dm)],
        out_shape=[jax.ShapeDtypeStruct((n_tok, width), BF16), jax.ShapeDtypeStruct((n_tok, dm), F32),
                   jax.ShapeDtypeStruct((n_tok, dm), F32), jax.ShapeDtypeStruct((n_tok, dm), BF16)],
        scratch_shapes=[pltpu.VMEM((width, dm), BF16), pltpu.SemaphoreType.DMA((N_DEV,))],
        compiler_params=_params(56),
    )(*deps, x, g1, b_in, slab)


def _fwd_mix(p, u, proj, x, caw, cab, cbw, cbb, lng, lnb, g1post, slab, ts):
    n_tok, dm = x.shape
    rows, off, _ = _layout(dm)
    n_steps = n_tok // ts

    def body(p_ref, p_prev, p_next, u_ref, u_prev, u_next, bg_ref, za_ref, zb_ref, x_ref,
             caw_ref, cab_ref, cbw_ref, cbb_ref, lng_ref, lnb_ref, g1p_ref, slab_ref,
             va_ref, vb_ref, ya_ref, yb_ref, qa_ref, sb_ref, mg_ref, mix_ref, x1_ref,
             wa_v, wb_v, wo_v, tap_a, tap_b, sems, rolled_a, rolled_b):
        i = pl.program_id(0)
        copies = (_weight_copies(slab_ref, off["wa"], rows["wa"], wa_v, sems, 0)
                  + _weight_copies(slab_ref, off["wb"], rows["wb"], wb_v, sems, N_DEV)
                  + _weight_copies(slab_ref, off["wo"], rows["wo"], wo_v, sems, 2 * N_DEV))
        _on_first_step(copies, "start")

        @pl.when(i == 0)
        def _():
            _broadcast_taps(caw_ref, tap_a, CONV_A)
            _broadcast_taps(cbw_ref, tap_b, CONV_B)

        def emit_a(r0, lanes, acc):
            va_ref[pl.ds(r0, acc.shape[0]), lanes] = acc + cab_ref[:, lanes]

        def emit_b(r0, lanes, acc):
            vb_ref[pl.ds(r0, acc.shape[0]), lanes] = acc + cbb_ref[:, lanes]

        _conv_tile(_with_halos(p_ref, p_prev, p_next, i, n_steps), tap_a, _fwd_starts(CONV_A), ts, dm, emit_a,
                   rolled_a)
        _conv_tile(_with_halos(u_ref, u_prev, u_next, i, n_steps), tap_b, _fwd_starts(CONV_B), ts, dm, emit_b,
                   rolled_b)
        _on_first_step(copies, "wait")

        qa = (bg_ref[...].astype(F32) * va_ref[...]).astype(BF16)
        qa_ref[...] = qa
        ya = _dot(qa, wa_v[...])
        vb = vb_ref[...]
        xc = vb - jnp.mean(vb, axis=-1, keepdims=True)
        rstd = lax.rsqrt(jnp.mean(xc * xc, axis=-1, keepdims=True) + LN_EPS)
        ln = xc * rstd * lng_ref[...] + lnb_ref[...]
        sb = (ln * _sigmoid(ln)).astype(BF16)
        sb_ref[...] = sb
        yb = _dot(sb, wb_v[...])
        ya_ref[...] = ya.astype(BF16)
        yb_ref[...] = yb.astype(BF16)
        merged = (_sigmoid(za_ref[...].astype(F32)) * ya + _sigmoid(zb_ref[...].astype(F32)) * yb).astype(BF16)
        mg_ref[...] = merged
        mix = _dot(merged, wo_v[...])
        mix_ref[...] = mix
        x1_ref[...] = x_ref[...] + mix * _rms(mix) * g1p_ref[...]

    tok = lambda dt: jax.ShapeDtypeStruct((n_tok, dm), dt)
    return pl.pallas_call(
        body, name="fwd_mix", grid=(n_steps,),
        in_specs=(_halo_specs(ts, dm, n_tok) + _halo_specs(ts, dm, n_tok)
                  + [_rows(ts, dm, 0), _rows(ts, dm, 5), _rows(ts, dm, 6), _rows(ts, dm)]
                  + [_whole()] * 7 + [_hbm()]),
        out_specs=[_rows(ts, dm)] * 9,
        out_shape=[tok(F32), tok(F32), tok(BF16), tok(BF16), tok(BF16), tok(BF16), tok(BF16), tok(F32), tok(F32)],
        scratch_shapes=[pltpu.VMEM((dm, dm), BF16), pltpu.VMEM((dm, dm), BF16), pltpu.VMEM((dm, dm), BF16),
                        pltpu.VMEM((CONV_A, SUBLANE, dm), F32), pltpu.VMEM((CONV_B, SUBLANE, dm), F32),
                        pltpu.SemaphoreType.DMA((3 * N_DEV,)),
                        pltpu.VMEM((2, SUBLANE, ts + 2 * HALO, LANE), F32),
                        pltpu.VMEM((2, SUBLANE, ts + 2 * HALO, LANE), F32)],
        compiler_params=_params(48),
    )(p, p, p, u, u, u, proj, proj, proj, x, caw, cab, cbw, cbb, lng, lnb, g1post, slab)


def _mlp_fwd_bwd(x1, mix, tgt, g1post, g2pre, g2post, slab, ts):
    n_tok, dm = x1.shape
    rows, off, _ = _layout(dm)
    ff = 4 * dm

    def body(x1_ref, mix_ref, t_ref, g1p_ref, g2pre_ref, g2post_ref, slab_ref,
             f_ref, df1_ref, h2_ref, df2_ref, dmix_ref, dx1_ref, small_ref, w1_v, w2_v, relu_v, sems):
        w1_copies = _weight_copies(slab_ref, off["w1"], rows["w1"], w1_v, sems, 0)
        w2_copies = _weight_copies(slab_ref, off["w2"], rows["w2"], w2_v, sems, N_DEV)
        _on_first_step(w1_copies + w2_copies, "start")

        @pl.when(pl.program_id(0) == 0)
        def _():
            small_ref[...] = jnp.zeros_like(small_ref)

        _on_first_step(w1_copies + w2_copies, "wait")
        x1v = x1_ref[...]
        r3 = _rms(x1v)
        g2pre = g2pre_ref[...]
        h2 = (x1v * r3 * g2pre).astype(BF16)
        h2_ref[...] = h2
        for c in range(4):
            blk = pl.ds(c * dm, dm)
            relu = jnp.maximum(_dot_nt(h2, w1_v[blk, :]), 0.0)
            relu_v[:, c * dm:(c + 1) * dm] = relu
            f_ref[:, c * dm:(c + 1) * dm] = (relu * relu).astype(BF16)
        f2 = _dot(f_ref[...], w2_v[...])
        r4 = _rms(f2)
        g2post = g2post_ref[...]
        err = x1v + f2 * r4 * g2post - t_ref[...]
        dy = err * (1.0 / dm)
        small_ref[3:4, :] += _colsum(err * err)
        small_ref[0:1, :] += _colsum(dy * f2 * r4)
        df2 = _rms_bwd(dy, f2, r4, g2post).astype(BF16)
        df2_ref[...] = df2
        for c in range(4):
            blk = pl.ds(c * dm, dm)
            df1 = (_dot_nt(df2, w2_v[blk, :]) * (2.0 * relu_v[:, c * dm:(c + 1) * dm])).astype(BF16)
            df1_ref[:, c * dm:(c + 1) * dm] = df1
        dh2 = _dot(df1_ref[...], w1_v[...])
        small_ref[1:2, :] += _colsum(dh2 * x1v * r3)
        dx1 = dy + _rms_bwd(dh2, x1v, r3, g2pre)
        dx1_ref[...] = dx1
        mixv = mix_ref[...]
        r2 = _rms(mixv)
        small_ref[2:3, :] += _colsum(dx1 * mixv * r2)
        dmix_ref[...] = _rms_bwd(dx1, mixv, r2, g1p_ref[...]).astype(BF16)

    tok = lambda w, dt: jax.ShapeDtypeStruct((n_tok, w), dt)
    return pl.pallas_call(
        body, name="mlp_fwd_bwd", grid=(n_tok // ts,),
        in_specs=[_rows(ts, dm)] * 3 + [_whole()] * 3 + [_hbm()],
        out_specs=[_rows(ts, ff), _rows(ts, ff), _rows(ts, dm), _rows(ts, dm), _rows(ts, dm), _rows(ts, dm),
                   pl.BlockSpec((SUBLANE, dm), lambda i: (0, 0))],
        out_shape=[tok(ff, BF16), tok(ff, BF16), tok(dm, BF16), tok(dm, BF16), tok(dm, BF16), tok(dm, F32),
                   jax.ShapeDtypeStruct((SUBLANE, dm), F32)],
        scratch_shapes=[pltpu.VMEM((ff, dm), BF16), pltpu.VMEM((ff, dm), BF16), pltpu.VMEM((ts, ff), F32),
                        pltpu.SemaphoreType.DMA((2 * N_DEV,))],
        compiler_params=_params(56),
    )(x1, mix, tgt, g1post, g2pre, g2post, slab)


def _bwd_mix(dmix, ya, yb, proj, va, vb, lng, lnb, slab, ts, deps):
    n_tok, dm = dmix.shape
    rows, off, _ = _layout(dm)
    n_steps = n_tok // ts

    def body(dmix_ref, ya_ref, yb_ref, bg_ref, za_ref, zb_ref, va_ref, vb_ref, lng_ref, lnb_ref, slab_ref,
             dpa_ref, dya_ref, dyb_ref, dva_ref, dvb_ref, small_ref, wa_v, wb_v, wo_v, sems):
        wo_copies = _weight_copies(slab_ref, off["wo"], rows["wo"], wo_v, sems, 2 * N_DEV)
        ab_copies = (_weight_copies(slab_ref, off["wa"], rows["wa"], wa_v, sems, 0)
                     + _weight_copies(slab_ref, off["wb"], rows["wb"], wb_v, sems, N_DEV))
        _on_first_step(wo_copies + ab_copies, "start")

        @pl.when(pl.program_id(0) == 0)
        def _():
            small_ref[...] = jnp.zeros_like(small_ref)

        _on_first_step(wo_copies, "wait")
        dmerged = _dot_nt(dmix_ref[...], wo_v[...])
        _on_first_step(ab_copies, "wait")
        sa = _sigmoid(za_ref[...].astype(F32))
        sg = _sigmoid(zb_ref[...].astype(F32))
        dza = dmerged * ya_ref[...].astype(F32) * sa * (1.0 - sa)
        dzb = dmerged * yb_ref[...].astype(F32) * sg * (1.0 - sg)
        dpa_ref[:, dm:2 * dm] = dza.astype(BF16)
        dpa_ref[:, 2 * dm:3 * dm] = dzb.astype(BF16)
        small_ref[5:6, :] += _colsum(dza)
        small_ref[6:7, :] += _colsum(dzb)

        dya = (dmerged * sa).astype(BF16)
        dya_ref[...] = dya
        dqa = _dot_nt(dya, wa_v[...])
        dbg = dqa * va_ref[...]
        dpa_ref[:, 0:dm] = dbg.astype(BF16)
        small_ref[4:5, :] += _colsum(dbg)
        dva = dqa * bg_ref[...].astype(F32)
        dva_ref[...] = dva
        small_ref[2:3, :] += _colsum(dva)

        dyb = (dmerged * sg).astype(BF16)
        dyb_ref[...] = dyb
        dsb = _dot_nt(dyb, wb_v[...])
        vb = vb_ref[...]
        xc = vb - jnp.mean(vb, axis=-1, keepdims=True)
        rstd = lax.rsqrt(jnp.mean(xc * xc, axis=-1, keepdims=True) + LN_EPS)
        nrm = xc * rstd
        lng_v = lng_ref[...]
        ln = nrm * lng_v + lnb_ref[...]
        sl = _sigmoid(ln)
        dln = dsb * (sl * (1.0 + ln * (1.0 - sl)))
        small_ref[0:1, :] += _colsum(dln * nrm)
        small_ref[1:2, :] += _colsum(dln)
        dn = dln * lng_v
        dvb = rstd * (dn - jnp.mean(dn, axis=-1, keepdims=True)
                      - nrm * jnp.mean(dn * nrm, axis=-1, keepdims=True))
        dvb_ref[...] = dvb
        small_ref[3:4, :] += _colsum(dvb)

    tok = lambda w, dt: jax.ShapeDtypeStruct((n_tok, w), dt)
    return pl.pallas_call(
        _after(body, deps), name="bwd_mix", grid=(n_steps,),
        in_specs=([_whole()] * len(deps) + [_rows(ts, dm)] * 3
                  + [_rows(ts, dm, 0), _rows(ts, dm, 5), _rows(ts, dm, 6)]
                  + [_rows(ts, dm)] * 2 + [_whole()] * 2 + [_hbm()]),
        out_specs=[_rows(ts, 3 * dm), _rows(ts, dm), _rows(ts, dm), _rows(ts, dm), _rows(ts, dm),
                   pl.BlockSpec((SUBLANE, dm), lambda i: (0, 0))],
        out_shape=[tok(3 * dm, BF16), tok(dm, BF16), tok(dm, BF16), tok(dm, F32), tok(dm, F32),
                   jax.ShapeDtypeStruct((SUBLANE, dm), F32)],
        scratch_shapes=[pltpu.VMEM((dm, dm), BF16), pltpu.VMEM((dm, dm), BF16), pltpu.VMEM((dm, dm), BF16),
                        pltpu.SemaphoreType.DMA((3 * N_DEV,))],
        compiler_params=_params(56),
    )(*deps, dmix, ya, yb, proj, proj, proj, va, vb, lng, lnb, slab)


def _bwd_conv(dva, dvb, p, u, proj, caw, cbw, ts, deps):
    n_tok, dm = dva.shape
    n_steps = n_tok // ts
    small_rows = 40

    def body(dva_ref, dva_prev, dva_next, dvb_ref, dvb_prev, dvb_next, p_ref, u_ref,
             cg_ref, ha_ref, a_ref, g_ref, caw_ref, cbw_ref,
             dproj_ref, small_ref,
             dp_v, du_v, tap_a, tap_b, gwa_v, gwb_v):
        i = pl.program_id(0)

        @pl.when(i == 0)
        def _():
            _broadcast_taps(caw_ref, tap_a, CONV_A)
            _broadcast_taps(cbw_ref, tap_b, CONV_B)
            small_ref[...] = jnp.zeros_like(small_ref)
            gwa_v[...] = jnp.zeros_like(gwa_v)
            gwb_v[...] = jnp.zeros_like(gwb_v)

        def emit_dp(r0, lanes, acc):
            dp_v[pl.ds(r0, acc.shape[0]), lanes] = acc

        def emit_du(r0, lanes, acc):
            du_v[pl.ds(r0, acc.shape[0]), lanes] = acc

        _conv_bwd_tile(_with_halos(dva_ref, dva_prev, dva_next, i, n_steps), p_ref, tap_a, gwa_v, CONV_A, ts, dm,
                       emit_dp)
        _conv_bwd_tile(_with_halos(dvb_ref, dvb_prev, dvb_next, i, n_steps), u_ref, tap_b, gwb_v, CONV_B, ts, dm,
                       emit_du)

        dp = dp_v[...]
        dcg = dp * ha_ref[...].astype(F32)
        dha = dp * cg_ref[...].astype(F32)
        du = du_v[...]
        sg = _sigmoid(g_ref[...].astype(F32))
        da = du * sg
        dg = du * a_ref[...].astype(F32) * sg * (1.0 - sg)
        dproj_ref[:, 0:dm] = dcg.astype(BF16)
        dproj_ref[:, dm:2 * dm] = dha.astype(BF16)
        dproj_ref[:, 2 * dm:3 * dm] = da.astype(BF16)
        dproj_ref[:, 3 * dm:4 * dm] = dg.astype(BF16)
        small_ref[3:4, :] += _colsum(dcg)
        small_ref[4:5, :] += _colsum(dha)
        small_ref[5:6, :] += _colsum(da)
        small_ref[6:7, :] += _colsum(dg)

        @pl.when(i == n_steps - 1)
        def _():
            for k in range(CONV_A):
                small_ref[k:k + 1, :] = _colsum(gwa_v[k])
            for k in range(CONV_B):
                small_ref[SUBLANE + k:SUBLANE + k + 1, :] = _colsum(gwb_v[k])

    return pl.pallas_call(
        _after(body, deps), name="bwd_conv", grid=(n_steps,),
        in_specs=([_whole()] * len(deps) + _halo_specs(ts, dm, n_tok) * 2 + [_rows(ts, dm)] * 2
                  + [_rows(ts, dm, 1), _rows(ts, dm, 2), _rows(ts, dm, 3), _rows(ts, dm, 4)]
                  + [_whole()] * 2),
        out_specs=[_rows(ts, 4 * dm), pl.BlockSpec((small_rows, dm), lambda i: (0, 0))],
        out_shape=[jax.ShapeDtypeStruct((n_tok, 4 * dm), BF16), jax.ShapeDtypeStruct((small_rows, dm), F32)],
        scratch_shapes=[pltpu.VMEM((ts, dm), F32), pltpu.VMEM((ts, dm), F32),
                        pltpu.VMEM((CONV_A, SUBLANE, dm), F32), pltpu.VMEM((CONV_B, SUBLANE, dm), F32),
                        pltpu.VMEM((CONV_A, SUBLANE, dm), F32), pltpu.VMEM((CONV_B, SUBLANE, dm), F32)],
        compiler_params=_params(48),
    )(*deps, dva, dva, dva, dvb, dvb, dvb, p, u, proj, proj, proj, proj, caw, cbw)


def _bwd_in(dpa, dpb, x, dx1, g1, slab, ts, deps):
    n_tok, dm = x.shape
    rows, off, _ = _layout(dm)
    width = 7 * dm

    def body(dpa_ref, dpb_ref, x_ref, dx1_ref, g1_ref, slab_ref, gx_ref, small_ref, w_v, sems):
        copies = _weight_copies(slab_ref, off["win"], rows["win"], w_v, sems, 0)
        _on_first_step(copies, "start")

        @pl.when(pl.program_id(0) == 0)
        def _():
            small_ref[...] = jnp.zeros_like(small_ref)

        _on_first_step(copies, "wait")
        dh = (_dot(dpa_ref[:, 0:dm], w_v[0:dm, :]) + _dot(dpb_ref[...], w_v[dm:5 * dm, :])
              + _dot(dpa_ref[:, dm:3 * dm], w_v[5 * dm:7 * dm, :]))
        xv = x_ref[...]
        r1 = _rms(xv)
        small_ref[0:1, :] += _colsum(dh * xv * r1)
        gx_ref[...] = dx1_ref[...] + _rms_bwd(dh, xv, r1, g1_ref[...])

    return pl.pallas_call(
        _after(body, deps), name="bwd_in", grid=(n_tok // ts,),
        in_specs=[_whole()] * len(deps) + [_rows(ts, 3 * dm), _rows(ts, 4 * dm), _rows(ts, dm), _rows(ts, dm),
                                           _whole(), _hbm()],
        out_specs=[_rows(ts, dm), pl.BlockSpec((SUBLANE, dm), lambda i: (0, 0))],
        out_shape=[jax.ShapeDtypeStruct((n_tok, dm), F32), jax.ShapeDtypeStruct((SUBLANE, dm), F32)],
        scratch_shapes=[pltpu.VMEM((width, dm), BF16), pltpu.SemaphoreType.DMA((N_DEV,))],
        compiler_params=_params(56),
    )(*deps, dpa, dpb, x, dx1, g1, slab)


def _wgrad(a, b, name, tm, tk, out_dtype):
    n_tok, m = a.shape
    n = b.shape[1]
    k_steps = n_tok // tk

    def body(a_ref, b_ref, o_ref, acc_v):
        k = pl.program_id(1)

        @pl.when(k == 0)
        def _():
            acc_v[...] = jnp.zeros_like(acc_v)

        acc_v[...] += _dot_tn(a_ref[...], b_ref[...])

        @pl.when(k == k_steps - 1)
        def _():
            o_ref[...] = acc_v[...].astype(o_ref.dtype)

    return pl.pallas_call(
        body, name=name, grid=(m // tm, k_steps),
        in_specs=[pl.BlockSpec((tk, tm), lambda i, k: (k, i)), pl.BlockSpec((tk, n), lambda i, k: (k, 0))],
        out_specs=pl.BlockSpec((tm, n), lambda i, k: (i, 0)),
        out_shape=pltpu.HBM((m, n), out_dtype),
        scratch_shapes=[pltpu.VMEM((tm, n), F32)],
        compiler_params=pltpu.CompilerParams(dimension_semantics=("arbitrary", "arbitrary"),
                                             vmem_limit_bytes=40 * MIB),
    )(a, b)


def _wgrad_in(dpa, dpb, h, tk):
    n_tok, dm = h.shape
    k_steps = n_tok // tk
    last = k_steps - 1

    def from_a(i):
        return (i == 0) | (i >= 5)

    def body(a_ref, b_ref, h_ref, o_ref, acc_v):
        i, k = pl.program_id(0), pl.program_id(1)

        @pl.when(k == 0)
        def _():
            acc_v[...] = jnp.zeros_like(acc_v)

        @pl.when(from_a(i))
        def _():
            acc_v[...] += _dot_tn(a_ref[...], h_ref[...])

        @pl.when(jnp.logical_not(from_a(i)))
        def _():
            acc_v[...] += _dot_tn(b_ref[...], h_ref[...])

        @pl.when(k == last)
        def _():
            o_ref[...] = acc_v[...].astype(o_ref.dtype)

    a_index = lambda i, k: (jnp.where(from_a(i), k, last), jnp.where(i >= 5, i - 4, 0))
    b_index = lambda i, k: (jnp.where(from_a(i), jnp.where(i == 0, 0, last), k), jnp.clip(i - 1, 0, 3))
    return pl.pallas_call(
        body, name="wgrad_in", grid=(7, k_steps),
        in_specs=[pl.BlockSpec((tk, dm), a_index), pl.BlockSpec((tk, dm), b_index),
                  pl.BlockSpec((tk, dm), lambda i, k: (k, 0))],
        out_specs=pl.BlockSpec((dm, dm), lambda i, k: (i, 0)),
        out_shape=pltpu.HBM((7 * dm, dm), BF16),
        scratch_shapes=[pltpu.VMEM((dm, dm), F32)],
        compiler_params=pltpu.CompilerParams(dimension_semantics=("arbitrary", "arbitrary"),
                                             vmem_limit_bytes=48 * MIB),
    )(dpa, dpb, h)


def _adamw(w, g, m, v):
    m = ADAM_B1 * m + (1.0 - ADAM_B1) * g
    v = ADAM_B2 * v + (1.0 - ADAM_B2) * (g * g)
    m_hat = m / (1.0 - ADAM_B1 ** ADAM_STEP)
    v_hat = v / (1.0 - ADAM_B2 ** ADAM_STEP)
    delta = -ADAM_LR * (m_hat / (jnp.sqrt(v_hat) + ADAM_EPS) + ADAM_WD * w)
    return delta, m, v


def _adam_big(recv, part, me, off, rows, w, m, v, transpose, name, tr):
    dm = recv.shape[2]
    per = rows // tr

    def body(me_ref, own_ref, r_ref, w_ref, m_ref, v_ref, g_ref, d_ref, mo_ref, vo_ref):
        g = own_ref[...].astype(F32)
        for k in range(len(FLIPS)):
            g = g + r_ref[k].astype(F32)
        if transpose:
            g = g.T
        delta, m_new, v_new = _adamw(w_ref[...], g, m_ref[...], v_ref[...])
        g_ref[...] = g
        d_ref[...] = delta
        mo_ref[...] = m_new
        vo_ref[...] = v_new

    if transpose:
        blk = pl.BlockSpec((dm, tr), lambda i, me_ref: (0, i))
    else:
        blk = pl.BlockSpec((tr, dm), lambda i, me_ref: (i, 0))
    first = off // tr
    return pl.pallas_call(
        body, name=name,
        grid_spec=pltpu.PrefetchScalarGridSpec(
            num_scalar_prefetch=1, grid=(per,),
            in_specs=[pl.BlockSpec((tr, dm), lambda i, me_ref: (me_ref[0] * per + i, 0)),
                      pl.BlockSpec((len(FLIPS), tr, dm), lambda i, me_ref: (0, first + i, 0)), blk, blk, blk],
            out_specs=[blk] * 4),
        out_shape=[jax.ShapeDtypeStruct(w.shape, F32)] * 4,
        compiler_params=_params(32),
    )(me, *[_in_hbm(a) for a in (part, recv, w, m, v)])


def _adam_rows_group(recv, parts, me, offs, rows, triples, name):
    dm = recv.shape[2]
    n = len(parts)

    def body(me_ref, *refs):
        ins, outs = refs[:5 * n], refs[5 * n:]
        for j in range(n):
            own_ref, r_ref, w_ref, m_ref, v_ref = ins[5 * j:5 * j + 5]
            g = own_ref[...].astype(F32)
            for k in range(len(FLIPS)):
                g = g + r_ref[k].astype(F32)
            delta, m_new, v_new = _adamw(w_ref[...], g, m_ref[...], v_ref[...])
            for ref, val in zip(outs[4 * j:4 * j + 4], (g, delta, m_new, v_new)):
                ref[...] = val

    blk = pl.BlockSpec((rows, dm), lambda i, me_ref: (0, 0))
    in_specs, args = [], []
    for part, off, triple in zip(parts, offs, triples):
        in_specs += [pl.BlockSpec((rows, dm), lambda i, me_ref: (me_ref[0], 0)),
                     pl.BlockSpec((len(FLIPS), rows, dm), lambda i, me_ref, first=off // rows: (0, first, 0)),
                     blk, blk, blk]
        args += [part, recv, *triple]
    out = pl.pallas_call(
        body, name=name,
        grid_spec=pltpu.PrefetchScalarGridSpec(num_scalar_prefetch=1, grid=(1,), in_specs=in_specs,
                                               out_specs=[blk] * (4 * n)),
        out_shape=[jax.ShapeDtypeStruct((rows, dm), F32)] * (4 * n),
        compiler_params=_params(48),
    )(me, *[_in_hbm(a) for a in args])
    return [tuple(out[4 * j:4 * j + 4]) for j in range(n)]


LOSS_ROW = 15
CONV_A_ROW = 16
CONV_B_ROW = 24


def _adam_small(recv_small, recv_last, small_own, last_own, me, params, d_model):
    n = len(params)
    cw = d_model // N_DEV

    def body(me_ref, r_ref, rc_ref, l_ref, so_ref, soc_ref, lo_ref, *refs):
        ins, loss_ref, outs = refs[:3 * n], refs[3 * n], refs[3 * n + 1:3 * n + 1 + 4 * n]
        g_v, gc_v, last_v = refs[3 * n + 1 + 4 * n:]
        me_pos = me_ref[0]

        def slot(d, own_ref, slots_ref):
            return jnp.where(d == me_pos, own_ref[...], slots_ref[d])

        g, gc, last = slot(0, so_ref, r_ref), slot(0, soc_ref, rc_ref), slot(0, lo_ref, l_ref)
        for d in range(1, N_DEV):
            g, gc, last = g + slot(d, so_ref, r_ref), gc + slot(d, soc_ref, rc_ref), last + slot(d, lo_ref, l_ref)
        g_v[...], gc_v[...], last_v[...] = g, gc, last
        loss_ref[...] = (0.5 / d_model) * jnp.sum(g_v[LOSS_ROW:LOSS_ROW + 1, :], axis=-1, keepdims=True)
        for j, (row0, own_columns, (w, _, _)) in enumerate(params):
            w_ref, m_ref, v_ref = ins[3 * j:3 * j + 3]
            source = gc_v if own_columns else (last_v if row0 == 0 else g_v)
            width = source.shape[1]
            for c in range(w.shape[1] // width):
                cols = slice(c * width, (c + 1) * width)
                grad = source[row0 + c * w.shape[0]:row0 + (c + 1) * w.shape[0], :]
                delta, m_new, v_new = _adamw(w_ref[:, cols], grad, m_ref[:, cols], v_ref[:, cols])
                for ref, val in zip(outs[4 * j:4 * j + 4], (grad, delta, m_new, v_new)):
                    ref[:, cols] = val

    full = lambda shape: pl.BlockSpec(shape, lambda i, me_ref: (0,) * len(shape))
    stack_rows = recv_small.shape[1]
    flat = [a for _, _, triple in params for a in triple]
    shapes = [w.shape for _, _, (w, _, _) in params for _ in range(4)]
    out = pl.pallas_call(
        body, name="adam_small",
        grid_spec=pltpu.PrefetchScalarGridSpec(
            num_scalar_prefetch=1, grid=(1,),
            in_specs=[full(recv_small.shape),
                      pl.BlockSpec((N_DEV, stack_rows, cw), lambda i, me_ref: (0, 0, me_ref[0])),
                      full(recv_last.shape), full(small_own.shape),
                      pl.BlockSpec((stack_rows, cw), lambda i, me_ref: (0, me_ref[0])),
                      full(last_own.shape)] + [full(a.shape) for a in flat],
            out_specs=[full((1, 1))] + [full(s) for s in shapes],
            scratch_shapes=[pltpu.VMEM((stack_rows, d_model), F32), pltpu.VMEM((stack_rows, cw), F32),
                            pltpu.VMEM(recv_last.shape[1:], F32)]),
        out_shape=[jax.ShapeDtypeStruct((1, 1), F32)] + [jax.ShapeDtypeStruct(s, F32) for s in shapes],
    )(me, *[_in_hbm(a) for a in (recv_small, recv_small, recv_last, small_own, small_own, last_own, *flat)])
    return out[0], [tuple(out[1 + 4 * j:5 + 4 * j]) for j in range(n)]


def _tile(n_tok, want):
    return min(want, n_tok)


def kernel(x, norm1_pre_g, w_in, b_in, conv_a_w, conv_a_b, w_a_out, conv_b_w, conv_b_b, ln_b_g, ln_b_b, w_b_out, w_o, norm1_post_g, norm2_pre_g, w_mlp_in, w_mlp_out, norm2_post_g, loss_target, m_norm1_pre_g, m_w_in, m_b_in, m_conv_a_w, m_conv_a_b, m_w_a_out, m_conv_b_w, m_conv_b_b, m_ln_b_g, m_ln_b_b, m_w_b_out, m_w_o, m_norm1_post_g, m_norm2_pre_g, m_w_mlp_in, m_w_mlp_out, m_norm2_post_g, v_norm1_pre_g, v_w_in, v_b_in, v_conv_a_w, v_conv_a_b, v_w_a_out, v_conv_b_w, v_conv_b_b, v_ln_b_g, v_ln_b_b, v_w_b_out, v_w_o, v_norm1_post_g, v_norm2_pre_g, v_w_mlp_in, v_w_mlp_out, v_norm2_post_g):
    n_tok, dm = x.shape[1], x.shape[2]
    rows, off, slab_rows = _layout(dm)
    cw = dm // N_DEV
    xs = x.reshape(n_tok, dm)
    tgt = loss_target.reshape(n_tok, dm)
    row = lambda vec: vec.reshape(1, -1)
    scattered = lambda group: jax.ShapeDtypeStruct((len(FLIPS), slab_rows[group], dm), BF16)
    tm, tk = min(dm, 1024), _tile(n_tok, 2048)
    me = (4 * lax.axis_index("x") + 2 * lax.axis_index("y") + lax.axis_index("c")).astype(jnp.int32).reshape(1)

    conv_own = jnp.concatenate([conv_a_w, jnp.zeros((SUBLANE - CONV_A, cw), F32), conv_b_w,
                                jnp.zeros((1, cw), F32)], axis=0)
    own_in, land_in = _place_cast([(w_in, True)], me, "place_w_in")
    slab_in, conv_all = _all_gather_two_level([own_in, conv_own], [land_in, None], "gather_w_in")
    conv_full = conv_all.transpose(1, 0, 2).reshape(conv_own.shape[0], dm)
    caw, cbw = conv_full[0:CONV_A], conv_full[SUBLANE:SUBLANE + CONV_B]
    own_abo, land_abo = _place_cast([(w_a_out, False), (w_b_out, False), (w_o, False)], me, "place_abo")
    own_mlp, land_mlp = _place_cast([(w_mlp_in, True), (w_mlp_out, False)], me, "place_mlp")
    ag = _exchange_start([_Part(own_abo, False, slab_rows["abo"], 0, 0), _Part(own_mlp, False, slab_rows["mlp"], 1, 0)],
                         [land_abo, land_mlp], "gather_abo_mlp_start", after=slab_in)

    proj, p, u, h = _fwd_in(xs, row(norm1_pre_g), row(b_in), slab_in, _tile(n_tok, 512), [ag.token])
    _, (slab_abo,) = _exchange_wait(ag, "gather_abo_wait", after=proj, only=[0])
    va, vb, ya, yb, qa, sb, merged, mix, x1 = _fwd_mix(
        p, u, proj, xs, caw, row(conv_a_b), cbw, row(conv_b_b), row(ln_b_g), row(ln_b_b), row(norm1_post_g),
        slab_abo, _tile(n_tok, 256))
    _, (slab_mlp,) = _exchange_wait(ag, "gather_mlp_wait", after=x1, only=[1])
    f, df1, h2, df2, dmix, dx1, small_mlp = _mlp_fwd_bwd(
        x1, mix, tgt, row(norm1_post_g), row(norm2_pre_g), row(norm2_post_g), slab_mlp, _tile(n_tok, 256))

    rs_mlp = _exchange_start(
        [_Part(_wgrad(df1, h2, "wgrad_mlp_in", tm, tk, BF16), True, rows["w1"], 0, off["w1"]),
         _Part(_wgrad(f, df2, "wgrad_mlp_out", tm, tk, BF16), True, rows["w2"], 0, off["w2"])],
        [scattered("mlp")], "scatter_mlp_start")
    dpa, dya, dyb, dva, dvb, small_mix = _bwd_mix(
        dmix, ya, yb, proj, va, vb, row(ln_b_g), row(ln_b_b), slab_abo, _tile(n_tok, 512), [rs_mlp.token])
    rs_abo = _exchange_start(
        [_Part(_wgrad(qa, dya, "wgrad_a_out", tm, tk, BF16), True, rows["wa"], 0, off["wa"]),
         _Part(_wgrad(sb, dyb, "wgrad_b_out", tm, tk, BF16), True, rows["wb"], 0, off["wb"]),
         _Part(_wgrad(merged, dmix, "wgrad_o", tm, tk, BF16), True, rows["wo"], 0, off["wo"])],
        [scattered("abo")], "scatter_abo_start")
    dpb, small_conv = _bwd_conv(dva, dvb, p, u, proj, caw, cbw, _tile(n_tok, 256), [rs_abo.token])

    zeros = lambda r: jnp.zeros((r, dm), F32)
    small = jnp.concatenate([
        zeros(1),
        small_mix[2:3],
        small_mix[3:4],
        small_mix[0:2],
        small_mlp[2:3],
        small_mlp[1:2],
        small_mlp[0:1],
        small_mix[4:5], small_conv[3:7], small_mix[5:7],
        small_mlp[3:4],
        small_conv[0:CONV_A], zeros(SUBLANE - CONV_A),
        small_conv[8:8 + CONV_B], zeros(1),
    ], axis=0)

    rs_in = _exchange_start(
        [_Part(_wgrad_in(dpa, dpb, h, tk), True, rows["win"], 0, off["win"]),
         _Part(small, False, small.shape[0], 1, 0)],
        [scattered("in"), jax.ShapeDtypeStruct((N_DEV,) + small.shape, F32)], "scatter_in_start")
    grad_x, small_in = _bwd_in(dpa, dpb, xs, dx1, row(norm1_pre_g), slab_in, _tile(n_tok, 512), [rs_in.token])

    tr = min(LANE, rows["wa"])
    gather_last = _exchange_start([_Part(small_in, False, SUBLANE, 0, 0)],
                                  [jax.ShapeDtypeStruct((N_DEV, SUBLANE, dm), F32)], "gather_last_start")
    (g_w1, g_w2), (recv_mlp,) = _exchange_wait(rs_mlp, "scatter_mlp_wait", after=gather_last.token)
    (g_wa, g_wb, g_wo), (recv_abo,) = _exchange_wait(rs_abo, "scatter_abo_wait", after=grad_x)
    big = {
        "w_mlp_in": _adam_big(recv_mlp, g_w1, me, off["w1"], rows["w1"], w_mlp_in, m_w_mlp_in, v_w_mlp_in, True,
                              "adam_w_mlp_in", tr),
        "w_mlp_out": _adam_big(recv_mlp, g_w2, me, off["w2"], rows["w2"], w_mlp_out, m_w_mlp_out, v_w_mlp_out,
                               False, "adam_w_mlp_out", tr),
    }
    big["w_a_out"], big["w_b_out"], big["w_o"] = _adam_rows_group(
        recv_abo, (g_wa, g_wb, g_wo), me, (off["wa"], off["wb"], off["wo"]), rows["wa"],
        ((w_a_out, m_w_a_out, v_w_a_out), (w_b_out, m_w_b_out, v_w_b_out), (w_o, m_w_o, v_w_o)), "adam_abo")
    (g_win, small), (recv_in, recv_small) = _exchange_wait(rs_in, "scatter_in_wait", after=big["w_o"][3])
    big["w_in"] = _adam_big(recv_in, g_win, me, off["win"], rows["win"], w_in, m_w_in, v_w_in, True, "adam_w_in", tr)
    (small_in,), (recv_last,) = _exchange_wait(gather_last, "gather_last_wait", after=big["w_in"][3])

    small_names = ("norm1_pre_g", "conv_a_b", "conv_b_b", "ln_b_g", "ln_b_b", "norm1_post_g", "norm2_pre_g",
                   "norm2_post_g")
    given = dict(
        norm1_pre_g=(norm1_pre_g, m_norm1_pre_g, v_norm1_pre_g), conv_a_b=(conv_a_b, m_conv_a_b, v_conv_a_b),
        conv_b_b=(conv_b_b, m_conv_b_b, v_conv_b_b), ln_b_g=(ln_b_g, m_ln_b_g, v_ln_b_g),
        ln_b_b=(ln_b_b, m_ln_b_b, v_ln_b_b), norm1_post_g=(norm1_post_g, m_norm1_post_g, v_norm1_post_g),
        norm2_pre_g=(norm2_pre_g, m_norm2_pre_g, v_norm2_pre_g),
        norm2_post_g=(norm2_post_g, m_norm2_post_g, v_norm2_post_g))
    params = [(j, False, tuple(row(a) for a in given[name])) for j, name in enumerate(small_names)]
    params.append((SUBLANE, False, tuple(row(a) for a in (b_in, m_b_in, v_b_in))))
    params.append((CONV_A_ROW, True, (conv_a_w, m_conv_a_w, v_conv_a_w)))
    params.append((CONV_B_ROW, True, (conv_b_w, m_conv_b_w, v_conv_b_w)))
    loss, small_out = _adam_small(recv_small, recv_last, small, small_in, me, params, dm)
    small_leaves = {name: tuple(a.reshape(dm) for a in small_out[j]) for j, name in enumerate(small_names)}
    small_leaves["b_in"] = tuple(a.reshape(7 * dm) for a in small_out[len(small_names)])
    small_leaves["conv_a_w"] = small_out[len(small_names) + 1]
    small_leaves["conv_b_w"] = small_out[len(small_names) + 2]

    order = ("norm1_pre_g", "w_in", "b_in", "conv_a_w", "conv_a_b", "w_a_out", "conv_b_w", "conv_b_b", "ln_b_g",
             "ln_b_b", "w_b_out", "w_o", "norm1_post_g", "norm2_pre_g", "w_mlp_in", "w_mlp_out", "norm2_post_g")
    leaves = [big[name] if name in big else small_leaves[name] for name in order]
    grads, deltas, new_m, new_v = zip(*leaves)
    return (loss.reshape(()), grad_x.reshape(x.shape), *grads, *deltas, *new_m, *new_v)
```
